```python
import math
import jax, jax.numpy as jnp
from jax import lax
import numpy as np

D_MODEL = 1024
BATCH = 8
SEQ = 4096
DEPTH = 2

CHUNK = 64
D_FF = 2816
FFN_RES = 0.5
D_CONV = D_MODEL // 2
CONV_A_WIDTH = 31
N_HEADS_B = 8
HEAD_DIM = 64
D_ATTN = N_HEADS_B * HEAD_DIM
Q_BLOCK = 128
CONV_C_WIDTH = 3
D_SHORT = D_MODEL
N_EVEN = (DEPTH + 1) // 2
N_ODD = DEPTH // 2
D_IN_EVEN = 2 * D_CONV + 3 * D_ATTN + N_HEADS_B
D_IN_ODD = 3 * D_SHORT
EPS = 1e-6

kernel_name = "hybrid_conformer_fox_shortconv_trunk"


def rmsnorm(x, g):
    xf = x.astype(jnp.float32)
    y = xf * lax.rsqrt(jnp.mean(xf * xf, axis=-1, keepdims=True) + EPS)
    return (y * g.astype(jnp.float32)).astype(x.dtype)


def swiglu(x, w_gate, w_up, w_down):
    return (jax.nn.silu(x @ w_gate) * (x @ w_up)) @ w_down


def causal_depthwise_conv(x, w):
    k_width = w.shape[0]
    return lax.conv_general_dilated(
        x, w[:, None, :].astype(x.dtype), window_strides=(1,),
        padding=[(k_width - 1, 0)],
        dimension_numbers=("NWC", "WIO", "NWC"),
        feature_group_count=x.shape[-1])


def forgetting_attention(q, k, v, log_f):
    b, s_len, h, dh = q.shape
    nb = s_len // Q_BLOCK
    scale = 1.0 / math.sqrt(dh)
    cum = jnp.cumsum(log_f, axis=1).transpose(0, 2, 1)
    qh = q.transpose(0, 2, 1, 3)
    kh = k.transpose(0, 2, 1, 3)
    vh = v.transpose(0, 2, 1, 3)
    q_blocks = qh.reshape(b, h, nb, Q_BLOCK, dh).transpose(2, 0, 1, 3, 4)
    f_blocks = cum.reshape(b, h, nb, Q_BLOCK).transpose(2, 0, 1, 3)
    q_pos = jnp.arange(s_len).reshape(nb, Q_BLOCK)
    k_pos = jnp.arange(s_len)

    def block(args):
        qi, fi, pi = args
        logits = jnp.einsum("bhqd,bhkd->bhqk", qi, kh,
                            preferred_element_type=jnp.float32) * scale
        logits = logits + fi[..., None] - cum[:, :, None, :]
        logits = jnp.where(k_pos[None, :] <= pi[:, None], logits, -jnp.inf)
        p = jax.nn.softmax(logits, axis=-1)
        return jnp.einsum("bhqk,bhkd->bhqd", p.astype(vh.dtype), vh)

    o = lax.map(block, (q_blocks, f_blocks, q_pos))
    return o.transpose(1, 0, 3, 2, 4).reshape(b, s_len, h * dh)


def even_mixer(h, w_in, b_f, conv_w, conv_b, conv_norm, q_norm, k_norm, w_out):
    b, s_len, _ = h.shape
    z = h @ w_in
    splits = np.cumsum([D_CONV, D_CONV, D_ATTN, D_ATTN, D_ATTN]).tolist()
    u, g, q, k, v, f_logit = jnp.split(z, splits, axis=-1)
    a = u * jax.nn.sigmoid(g)
    a = causal_depthwise_conv(a, conv_w) + conv_b
    a = jax.nn.silu(rmsnorm(a, conv_norm))
    q = rmsnorm(q.reshape(b, s_len, N_HEADS_B, HEAD_DIM), q_norm)
    k = rmsnorm(k.reshape(b, s_len, N_HEADS_B, HEAD_DIM), k_norm)
    v = v.reshape(b, s_len, N_HEADS_B, HEAD_DIM)
    log_f = jax.nn.log_sigmoid(f_logit.astype(jnp.float32) + b_f.astype(jnp.float32))
    o = forgetting_attention(q, k, v, log_f)
    return jnp.concatenate([a, o.astype(a.dtype)], axis=-1) @ w_out


def odd_mixer(h, w_in, conv_w, w_out):
    z = h @ w_in
    gate_b, gate_c, hh = jnp.split(z, 3, axis=-1)
    y = gate_b * causal_depthwise_conv(gate_c * hh, conv_w)
    return y @ w_out


def _fwd_setup_inputs(seed: int = 0) -> dict:
    key = jax.random.key(seed)
    ks = iter(jax.random.split(key, 32))
    f32 = jnp.float32

    def nrm(shape, fan_in):
        return jax.random.normal(next(ks), shape, f32) * (fan_in ** -0.5)

    def gain(shape):
        return 1.0 + 0.02 * jax.random.normal(next(ks), shape, f32)

    return {
        "x": jax.random.normal(next(ks), (BATCH, SEQ, D_MODEL), f32),
        "ffn1_norm": gain((DEPTH, D_MODEL)),
        "ffn1_w_gate": nrm((DEPTH, D_MODEL, D_FF), D_MODEL),
        "ffn1_w_up": nrm((DEPTH, D_MODEL, D_FF), D_MODEL),
        "ffn1_w_down": nrm((DEPTH, D_FF, D_MODEL), D_FF),
        "mix_norm": gain((DEPTH, D_MODEL)),
        "ffn2_norm": gain((DEPTH, D_MODEL)),
        "ffn2_w_gate": nrm((DEPTH, D_MODEL, D_FF), D_MODEL),
        "ffn2_w_up": nrm((DEPTH, D_MODEL, D_FF), D_MODEL),
        "ffn2_w_down": nrm((DEPTH, D_FF, D_MODEL), D_FF),
        "ev_w_in": nrm((N_EVEN, D_MODEL, D_IN_EVEN), D_MODEL),
        "ev_b_f": jax.random.uniform(next(ks), (N_EVEN, N_HEADS_B), f32, 1.0, 5.0),
        "ev_conv_w": nrm((N_EVEN, CONV_A_WIDTH, D_CONV), CONV_A_WIDTH),
        "ev_conv_b": 0.01 * jax.random.normal(next(ks), (N_EVEN, D_CONV), f32),
        "ev_conv_norm": gain((N_EVEN, D_CONV)),
        "ev_q_norm": gain((N_EVEN, HEAD_DIM)),
        "ev_k_norm": gain((N_EVEN, HEAD_DIM)),
        "ev_w_out": nrm((N_EVEN, D_CONV + D_ATTN, D_MODEL), D_CONV + D_ATTN),
        "od_w_in": nrm((N_ODD, D_MODEL, D_IN_ODD), D_MODEL),
        "od_conv_w": nrm((N_ODD, CONV_C_WIDTH, D_SHORT), CONV_C_WIDTH),
        "od_w_out": nrm((N_ODD, D_SHORT, D_MODEL), D_SHORT),
    }


def _fwd_reference(x, ffn1_norm, ffn1_w_gate, ffn1_w_up, ffn1_w_down, mix_norm,
              ffn2_norm, ffn2_w_gate, ffn2_w_up, ffn2_w_down,
              ev_w_in, ev_b_f, ev_conv_w, ev_conv_b, ev_conv_norm, ev_q_norm,
              ev_k_norm, ev_w_out, od_w_in, od_conv_w, od_w_out):
    for layer in range(DEPTH):
        x = x + FFN_RES * swiglu(rmsnorm(x, ffn1_norm[layer]), ffn1_w_gate[layer],
                                 ffn1_w_up[layer], ffn1_w_down[layer])
        h = rmsnorm(x, mix_norm[layer])
        if layer % 2 == 0:
            i = layer // 2
            x = x + even_mixer(h, ev_w_in[i], ev_b_f[i], ev_conv_w[i], ev_conv_b[i],
                               ev_conv_norm[i], ev_q_norm[i], ev_k_norm[i], ev_w_out[i])
        else:
            i = layer // 2
            x = x + odd_mixer(h, od_w_in[i], od_conv_w[i], od_w_out[i])
        x = x + FFN_RES * swiglu(rmsnorm(x, ffn2_norm[layer]), ffn2_w_gate[layer],
                                 ffn2_w_up[layer], ffn2_w_down[layer])
    return x


import jax as _jax
import jax.numpy as _jnp

TWIN_FORMAT = 'train_step'
FWD_PARAMS = ['x', 'ffn1_norm', 'ffn1_w_gate', 'ffn1_w_up', 'ffn1_w_down', 'mix_norm', 'ffn2_norm', 'ffn2_w_gate', 'ffn2_w_up', 'ffn2_w_down', 'ev_w_in', 'ev_b_f', 'ev_conv_w', 'ev_conv_b', 'ev_conv_norm', 'ev_q_norm', 'ev_k_norm', 'ev_w_out', 'od_w_in', 'od_conv_w', 'od_w_out']
TWIN_WEIGHTS = ['ffn1_norm', 'ffn1_w_gate', 'ffn1_w_up', 'ffn1_w_down', 'mix_norm', 'ffn2_norm', 'ffn2_w_gate', 'ffn2_w_up', 'ffn2_w_down', 'ev_w_in', 'ev_b_f', 'ev_conv_w', 'ev_conv_b', 'ev_conv_norm', 'ev_q_norm', 'ev_k_norm', 'ev_w_out', 'od_w_in', 'od_conv_w', 'od_w_out']
TWIN_DIFF_INPUT = 'x'
TWIN_INPUTS = ['x', 'ffn1_norm', 'ffn1_w_gate', 'ffn1_w_up', 'ffn1_w_down', 'mix_norm', 'ffn2_norm', 'ffn2_w_gate', 'ffn2_w_up', 'ffn2_w_down', 'ev_w_in', 'ev_b_f', 'ev_conv_w', 'ev_conv_b', 'ev_conv_norm', 'ev_q_norm', 'ev_k_norm', 'ev_w_out', 'od_w_in', 'od_conv_w', 'od_w_out', 'loss_target', 'm_ffn1_norm', 'm_ffn1_w_gate', 'm_ffn1_w_up', 'm_ffn1_w_down', 'm_mix_norm', 'm_ffn2_norm', 'm_ffn2_w_gate', 'm_ffn2_w_up', 'm_ffn2_w_down', 'm_ev_w_in', 'm_ev_b_f', 'm_ev_conv_w', 'm_ev_conv_b', 'm_ev_conv_norm', 'm_ev_q_norm', 'm_ev_k_norm', 'm_ev_w_out', 'm_od_w_in', 'm_od_conv_w', 'm_od_w_out', 'v_ffn1_norm', 'v_ffn1_w_gate', 'v_ffn1_w_up', 'v_ffn1_w_down', 'v_mix_norm', 'v_ffn2_norm', 'v_ffn2_w_gate', 'v_ffn2_w_up', 'v_ffn2_w_down', 'v_ev_w_in', 'v_ev_b_f', 'v_ev_conv_w', 'v_ev_conv_b', 'v_ev_conv_norm', 'v_ev_q_norm', 'v_ev_k_norm', 'v_ev_w_out', 'v_od_w_in', 'v_od_conv_w', 'v_od_w_out']
TWIN_OUTPUTS = ['loss', 'grad_x', 'grad_ffn1_norm', 'grad_ffn1_w_gate', 'grad_ffn1_w_up', 'grad_ffn1_w_down', 'grad_mix_norm', 'grad_ffn2_norm', 'grad_ffn2_w_gate', 'grad_ffn2_w_up', 'grad_ffn2_w_down', 'grad_ev_w_in', 'grad_ev_b_f', 'grad_ev_conv_w', 'grad_ev_conv_b', 'grad_ev_conv_norm', 'grad_ev_q_norm', 'grad_ev_k_norm', 'grad_ev_w_out', 'grad_od_w_in', 'grad_od_conv_w', 'grad_od_w_out', 'delta_ffn1_norm', 'delta_ffn1_w_gate', 'delta_ffn1_w_up', 'delta_ffn1_w_down', 'delta_mix_norm', 'delta_ffn2_norm', 'delta_ffn2_w_gate', 'delta_ffn2_w_up', 'delta_ffn2_w_down', 'delta_ev_w_in', 'delta_ev_b_f', 'delta_ev_conv_w', 'delta_ev_conv_b', 'delta_ev_conv_norm', 'delta_ev_q_norm', 'delta_ev_k_norm', 'delta_ev_w_out', 'delta_od_w_in', 'delta_od_conv_w', 'delta_od_w_out', 'new_m_ffn1_norm', 'new_m_ffn1_w_gate', 'new_m_ffn1_w_up', 'new_m_ffn1_w_down', 'new_m_mix_norm', 'new_m_ffn2_norm', 'new_m_ffn2_w_gate', 'new_m_ffn2_w_up', 'new_m_ffn2_w_down', 'new_m_ev_w_in', 'new_m_ev_b_f', 'new_m_ev_conv_w', 'new_m_ev_conv_b', 'new_m_ev_conv_norm', 'new_m_ev_q_norm', 'new_m_ev_k_norm', 'new_m_ev_w_out', 'new_m_od_w_in', 'new_m_od_conv_w', 'new_m_od_w_out', 'new_v_ffn1_norm', 'new_v_ffn1_w_gate', 'new_v_ffn1_w_up', 'new_v_ffn1_w_down', 'new_v_mix_norm', 'new_v_ffn2_norm', 'new_v_ffn2_w_gate', 'new_v_ffn2_w_up', 'new_v_ffn2_w_down', 'new_v_ev_w_in', 'new_v_ev_b_f', 'new_v_ev_conv_w', 'new_v_ev_conv_b', 'new_v_ev_conv_norm', 'new_v_ev_q_norm', 'new_v_ev_k_norm', 'new_v_ev_w_out', 'new_v_od_w_in', 'new_v_od_conv_w', 'new_v_od_w_out']
TWIN_LEAF_KINDS = {'loss': 'loss', 'grad_x': 'grad_x', 'grad_ffn1_norm': 'grad_w', 'grad_ffn1_w_gate': 'grad_w', 'grad_ffn1_w_up': 'grad_w', 'grad_ffn1_w_down': 'grad_w', 'grad_mix_norm': 'grad_w', 'grad_ffn2_norm': 'grad_w', 'grad_ffn2_w_gate': 'grad_w', 'grad_ffn2_w_up': 'grad_w', 'grad_ffn2_w_down': 'grad_w', 'grad_ev_w_in': 'grad_w', 'grad_ev_b_f': 'grad_w', 'grad_ev_conv_w': 'grad_w', 'grad_ev_conv_b': 'grad_w', 'grad_ev_conv_norm': 'grad_w', 'grad_ev_q_norm': 'grad_w', 'grad_ev_k_norm': 'grad_w', 'grad_ev_w_out': 'grad_w', 'grad_od_w_in': 'grad_w', 'grad_od_conv_w': 'grad_w', 'grad_od_w_out': 'grad_w', 'delta_ffn1_norm': 'delta_w', 'delta_ffn1_w_gate': 'delta_w', 'delta_ffn1_w_up': 'delta_w', 'delta_ffn1_w_down': 'delta_w', 'delta_mix_norm': 'delta_w', 'delta_ffn2_norm': 'delta_w', 'delta_ffn2_w_gate': 'delta_w', 'delta_ffn2_w_up': 'delta_w', 'delta_ffn2_w_down': 'delta_w', 'delta_ev_w_in': 'delta_w', 'delta_ev_b_f': 'delta_w', 'delta_ev_conv_w': 'delta_w', 'delta_ev_conv_b': 'delta_w', 'delta_ev_conv_norm': 'delta_w', 'delta_ev_q_norm': 'delta_w', 'delta_ev_k_norm': 'delta_w', 'delta_ev_w_out': 'delta_w', 'delta_od_w_in': 'delta_w', 'delta_od_conv_w': 'delta_w', 'delta_od_w_out': 'delta_w', 'new_m_ffn1_norm': 'new_m', 'new_m_ffn1_w_gate': 'new_m', 'new_m_ffn1_w_up': 'new_m', 'new_m_ffn1_w_down': 'new_m', 'new_m_mix_norm': 'new_m', 'new_m_ffn2_norm': 'new_m', 'new_m_ffn2_w_gate': 'new_m', 'new_m_ffn2_w_up': 'new_m', 'new_m_ffn2_w_down': 'new_m', 'new_m_ev_w_in': 'new_m', 'new_m_ev_b_f': 'new_m', 'new_m_ev_conv_w': 'new_m', 'new_m_ev_conv_b': 'new_m', 'new_m_ev_conv_norm': 'new_m', 'new_m_ev_q_norm': 'new_m', 'new_m_ev_k_norm': 'new_m', 'new_m_ev_w_out': 'new_m', 'new_m_od_w_in': 'new_m', 'new_m_od_conv_w': 'new_m', 'new_m_od_w_out': 'new_m', 'new_v_ffn1_norm': 'new_v', 'new_v_ffn1_w_gate': 'new_v', 'new_v_ffn1_w_up': 'new_v', 'new_v_ffn1_w_down': 'new_v', 'new_v_mix_norm': 'new_v', 'new_v_ffn2_norm': 'new_v', 'new_v_ffn2_w_gate': 'new_v', 'new_v_ffn2_w_up': 'new_v', 'new_v_ffn2_w_down': 'new_v', 'new_v_ev_w_in': 'new_v', 'new_v_ev_b_f': 'new_v', 'new_v_ev_conv_w': 'new_v', 'new_v_ev_conv_b': 'new_v', 'new_v_ev_conv_norm': 'new_v', 'new_v_ev_q_norm': 'new_v', 'new_v_ev_k_norm': 'new_v', 'new_v_ev_w_out': 'new_v', 'new_v_od_w_in': 'new_v', 'new_v_od_conv_w': 'new_v', 'new_v_od_w_out': 'new_v'}


def _forward(args):
    return _fwd_reference(*[args[k] for k in FWD_PARAMS])


def _output_shape():
    out = _jax.eval_shape(lambda: _forward(_fwd_setup_inputs(0)))
    return out.shape, out.dtype

N_MICROBATCH = 1
ADAM_LR = 0.001
ADAM_B1 = 0.9
ADAM_B2 = 0.999
ADAM_EPS = 1e-08
ADAM_WD = 0.01
ADAM_STEP = 10
PER_EXAMPLE_BATCH_AXIS = {'x': 0, 'loss_target': 0}
SHARED_INPUTS = []
_WEIGHT_DTYPES = {'ffn1_norm': _jnp.float32, 'ffn1_w_gate': _jnp.float32, 'ffn1_w_up': _jnp.float32, 'ffn1_w_down': _jnp.float32, 'mix_norm': _jnp.float32, 'ffn2_norm': _jnp.float32, 'ffn2_w_gate': _jnp.float32, 'ffn2_w_up': _jnp.float32, 'ffn2_w_down': _jnp.float32, 'ev_w_in': _jnp.float32, 'ev_b_f': _jnp.float32, 'ev_conv_w': _jnp.float32, 'ev_conv_b': _jnp.float32, 'ev_conv_norm': _jnp.float32, 'ev_q_norm': _jnp.float32, 'ev_k_norm': _jnp.float32, 'ev_w_out': _jnp.float32, 'od_w_in': _jnp.float32, 'od_conv_w': _jnp.float32, 'od_w_out': _jnp.float32}
MOMENT_SCALE = {'ffn1_norm': 5.689231e+00, 'ffn1_w_gate': 2.314547e-01, 'ffn1_w_up': 2.322440e-01, 'ffn1_w_down': 3.848313e-01, 'mix_norm': 6.535462e+01, 'ffn2_norm': 6.101196e+00, 'ffn2_w_gate': 1.661750e-01, 'ffn2_w_up': 1.714688e-01, 'ffn2_w_down': 2.836910e-01, 'ev_w_in': 4.407264e-01, 'ev_b_f': 6.618868e+01, 'ev_conv_w': 8.293399e-01, 'ev_conv_b': 1.721921e+01, 'ev_conv_norm': 1.407597e+01, 'ev_q_norm': 9.183090e+00, 'ev_k_norm': 9.153906e+00, 'ev_w_out': 1.084817e+00, 'od_w_in': 1.090583e+00, 'od_conv_w': 1.812042e+01, 'od_w_out': 9.091388e-01}


def _to_microbatches(a, axis):
    t = _jnp.moveaxis(a, axis, 0)
    t = t.reshape((N_MICROBATCH, t.shape[0] // N_MICROBATCH) + t.shape[1:])
    return _jnp.moveaxis(t, 1, axis + 1)


def setup_inputs(seed: int = 0) -> dict:
    inp = _fwd_setup_inputs(seed)
    key = _jax.random.fold_in(_jax.random.key(seed), 7919)
    shape, _ = _output_shape()
    out = dict(inp)
    out["loss_target"] = _jax.random.normal(_jax.random.fold_in(key, 0), shape, _jnp.float32)
    for i, name in enumerate(TWIN_WEIGHTS):
        w = inp[name].astype(_jnp.float32)
        if MOMENT_SCALE is None:
            s = _jnp.sqrt(_jnp.mean(_jnp.square(w)) + 1e-30)
        else:
            s = MOMENT_SCALE[name]
        km, kv = _jax.random.split(_jax.random.fold_in(key, i + 1))
        out[name] = w
        out["m_" + name] = s * _jax.random.normal(km, w.shape, _jnp.float32)
        out["v_" + name] = (s * s) * _jax.random.uniform(kv, w.shape, _jnp.float32, 0.5, 1.5)
    if N_MICROBATCH > 1:
        for name, axis in PER_EXAMPLE_BATCH_AXIS.items():
            out[name] = _to_microbatches(out[name], axis)
    return {'x': out['x'], 'ffn1_norm': out['ffn1_norm'], 'ffn1_w_gate': out['ffn1_w_gate'], 'ffn1_w_up': out['ffn1_w_up'], 'ffn1_w_down': out['ffn1_w_down'], 'mix_norm': out['mix_norm'], 'ffn2_norm': out['ffn2_norm'], 'ffn2_w_gate': out['ffn2_w_gate'], 'ffn2_w_up': out['ffn2_w_up'], 'ffn2_w_down': out['ffn2_w_down'], 'ev_w_in': out['ev_w_in'], 'ev_b_f': out['ev_b_f'], 'ev_conv_w': out['ev_conv_w'], 'ev_conv_b': out['ev_conv_b'], 'ev_conv_norm': out['ev_conv_norm'], 'ev_q_norm': out['ev_q_norm'], 'ev_k_norm': out['ev_k_norm'], 'ev_w_out': out['ev_w_out'], 'od_w_in': out['od_w_in'], 'od_conv_w': out['od_conv_w'], 'od_w_out': out['od_w_out'], 'loss_target': out['loss_target'], 'm_ffn1_norm': out['m_ffn1_norm'], 'm_ffn1_w_gate': out['m_ffn1_w_gate'], 'm_ffn1_w_up': out['m_ffn1_w_up'], 'm_ffn1_w_down': out['m_ffn1_w_down'], 'm_mix_norm': out['m_mix_norm'], 'm_ffn2_norm': out['m_ffn2_norm'], 'm_ffn2_w_gate': out['m_ffn2_w_gate'], 'm_ffn2_w_up': out['m_ffn2_w_up'], 'm_ffn2_w_down': out['m_ffn2_w_down'], 'm_ev_w_in': out['m_ev_w_in'], 'm_ev_b_f': out['m_ev_b_f'], 'm_ev_conv_w': out['m_ev_conv_w'], 'm_ev_conv_b': out['m_ev_conv_b'], 'm_ev_conv_norm': out['m_ev_conv_norm'], 'm_ev_q_norm': out['m_ev_q_norm'], 'm_ev_k_norm': out['m_ev_k_norm'], 'm_ev_w_out': out['m_ev_w_out'], 'm_od_w_in': out['m_od_w_in'], 'm_od_conv_w': out['m_od_conv_w'], 'm_od_w_out': out['m_od_w_out'], 'v_ffn1_norm': out['v_ffn1_norm'], 'v_ffn1_w_gate': out['v_ffn1_w_gate'], 'v_ffn1_w_up': out['v_ffn1_w_up'], 'v_ffn1_w_down': out['v_ffn1_w_down'], 'v_mix_norm': out['v_mix_norm'], 'v_ffn2_norm': out['v_ffn2_norm'], 'v_ffn2_w_gate': out['v_ffn2_w_gate'], 'v_ffn2_w_up': out['v_ffn2_w_up'], 'v_ffn2_w_down': out['v_ffn2_w_down'], 'v_ev_w_in': out['v_ev_w_in'], 'v_ev_b_f': out['v_ev_b_f'], 'v_ev_conv_w': out['v_ev_conv_w'], 'v_ev_conv_b': out['v_ev_conv_b'], 'v_ev_conv_norm': out['v_ev_conv_norm'], 'v_ev_q_norm': out['v_ev_q_norm'], 'v_ev_k_norm': out['v_ev_k_norm'], 'v_ev_w_out': out['v_ev_w_out'], 'v_od_w_in': out['v_od_w_in'], 'v_od_conv_w': out['v_od_conv_w'], 'v_od_w_out': out['v_od_w_out']}


def _loss(weights, diff, rest, loss_target):
    with _jax.named_scope("forward"):
        args = {**rest, TWIN_DIFF_INPUT: diff, **{k: w.astype(_WEIGHT_DTYPES[k]) for k, w in weights.items()}}
        y = _forward(args)
    with _jax.named_scope("loss_head"):
        err = _jnp.square(y.astype(_jnp.float32) - loss_target)
        return 0.5 * _jnp.sum(_jnp.mean(err, axis=-1)) if err.ndim else 0.5 * err


def _adamw(w, g, m, v):
    m = ADAM_B1 * m + (1.0 - ADAM_B1) * g
    v = ADAM_B2 * v + (1.0 - ADAM_B2) * _jnp.square(g)
    m_hat = m / (1.0 - ADAM_B1 ** ADAM_STEP)
    v_hat = v / (1.0 - ADAM_B2 ** ADAM_STEP)
    delta = -ADAM_LR * (m_hat / (_jnp.sqrt(v_hat) + ADAM_EPS) + ADAM_WD * w)
    return delta, m, v


def reference(x, ffn1_norm, ffn1_w_gate, ffn1_w_up, ffn1_w_down, mix_norm, ffn2_norm, ffn2_w_gate, ffn2_w_up, ffn2_w_down, ev_w_in, ev_b_f, ev_conv_w, ev_conv_b, ev_conv_norm, ev_q_norm, ev_k_norm, ev_w_out, od_w_in, od_conv_w, od_w_out, loss_target, m_ffn1_norm, m_ffn1_w_gate, m_ffn1_w_up, m_ffn1_w_down, m_mix_norm, m_ffn2_norm, m_ffn2_w_gate, m_ffn2_w_up, m_ffn2_w_down, m_ev_w_in, m_ev_b_f, m_ev_conv_w, m_ev_conv_b, m_ev_conv_norm, m_ev_q_norm, m_ev_k_norm, m_ev_w_out, m_od_w_in, m_od_conv_w, m_od_w_out, v_ffn1_norm, v_ffn1_w_gate, v_ffn1_w_up, v_ffn1_w_down, v_mix_norm, v_ffn2_norm, v_ffn2_w_gate, v_ffn2_w_up, v_ffn2_w_down, v_ev_w_in, v_ev_b_f, v_ev_conv_w, v_ev_conv_b, v_ev_conv_norm, v_ev_q_norm, v_ev_k_norm, v_ev_w_out, v_od_w_in, v_od_conv_w, v_od_w_out):
    given = dict(x=x, ffn1_norm=ffn1_norm, ffn1_w_gate=ffn1_w_gate, ffn1_w_up=ffn1_w_up, ffn1_w_down=ffn1_w_down, mix_norm=mix_norm, ffn2_norm=ffn2_norm, ffn2_w_gate=ffn2_w_gate, ffn2_w_up=ffn2_w_up, ffn2_w_down=ffn2_w_down, ev_w_in=ev_w_in, ev_b_f=ev_b_f, ev_conv_w=ev_conv_w, ev_conv_b=ev_conv_b, ev_conv_norm=ev_conv_norm, ev_q_norm=ev_q_norm, ev_k_norm=ev_k_norm, ev_w_out=ev_w_out, od_w_in=od_w_in, od_conv_w=od_conv_w, od_w_out=od_w_out, loss_target=loss_target, m_ffn1_norm=m_ffn1_norm, m_ffn1_w_gate=m_ffn1_w_gate, m_ffn1_w_up=m_ffn1_w_up, m_ffn1_w_down=m_ffn1_w_down, m_mix_norm=m_mix_norm, m_ffn2_norm=m_ffn2_norm, m_ffn2_w_gate=m_ffn2_w_gate, m_ffn2_w_up=m_ffn2_w_up, m_ffn2_w_down=m_ffn2_w_down, m_ev_w_in=m_ev_w_in, m_ev_b_f=m_ev_b_f, m_ev_conv_w=m_ev_conv_w, m_ev_conv_b=m_ev_conv_b, m_ev_conv_norm=m_ev_conv_norm, m_ev_q_norm=m_ev_q_norm, m_ev_k_norm=m_ev_k_norm, m_ev_w_out=m_ev_w_out, m_od_w_in=m_od_w_in, m_od_conv_w=m_od_conv_w, m_od_w_out=m_od_w_out, v_ffn1_norm=v_ffn1_norm, v_ffn1_w_gate=v_ffn1_w_gate, v_ffn1_w_up=v_ffn1_w_up, v_ffn1_w_down=v_ffn1_w_down, v_mix_norm=v_mix_norm, v_ffn2_norm=v_ffn2_norm, v_ffn2_w_gate=v_ffn2_w_gate, v_ffn2_w_up=v_ffn2_w_up, v_ffn2_w_down=v_ffn2_w_down, v_ev_w_in=v_ev_w_in, v_ev_b_f=v_ev_b_f, v_ev_conv_w=v_ev_conv_w, v_ev_conv_b=v_ev_conv_b, v_ev_conv_norm=v_ev_conv_norm, v_ev_q_norm=v_ev_q_norm, v_ev_k_norm=v_ev_k_norm, v_ev_w_out=v_ev_w_out, v_od_w_in=v_od_w_in, v_od_conv_w=v_od_conv_w, v_od_w_out=v_od_w_out)
    weights = {n: given[n] for n in TWIN_WEIGHTS}
    shared = {n: given[n] for n in SHARED_INPUTS}
    per_example = {n: given[n] for n in ['x']}
    grad_fn = _jax.value_and_grad(_loss, argnums=(0, 1))

    def one_microbatch(ex, loss_target):
        ex = dict(ex)
        diff = ex.pop(TWIN_DIFF_INPUT)
        return grad_fn(weights, diff, {**shared, **ex}, loss_target)

    if N_MICROBATCH == 1:
        loss, (grad_w, grad_x) = one_microbatch(per_example, given["loss_target"])
    else:
        def body(carry, xs):
            loss_sum, grad_sum = carry
            l_k, (gw_k, gx_k) = one_microbatch(xs[0], xs[1])
            with _jax.named_scope("update"):
                return (loss_sum + l_k, _jax.tree.map(_jnp.add, grad_sum, gw_k)), gx_k

        init = (_jnp.zeros((), _jnp.float32), _jax.tree.map(_jnp.zeros_like, weights))
        (loss, grad_w), grad_x = _jax.lax.scan(body, init, (per_example, given["loss_target"]))
    with _jax.named_scope("update"):
        delta_w, new_m, new_v = {}, {}, {}
        for n in TWIN_WEIGHTS:
            delta_w[n], new_m[n], new_v[n] = _adamw(weights[n], grad_w[n], given["m_" + n], given["v_" + n])
    return (loss, grad_x, *[grad_w[n] for n in TWIN_WEIGHTS], *[delta_w[n] for n in TWIN_WEIGHTS],
            *[new_m[n] for n in TWIN_WEIGHTS], *[new_v[n] for n in TWIN_WEIGHTS])
```

```python
import functools
import math

import jax
import jax.numpy as jnp
from jax import lax
from jax.experimental import pallas as pl
from jax.experimental.pallas import tpu as pltpu

F32 = jnp.float32
BF16 = jnp.bfloat16
SDS = jax.ShapeDtypeStruct
MESH = pl.DeviceIdType.MESH

N_DEV = 8
EPS = 1e-6
FFN_RES = 0.5
HEAD_DIM = 64
N_HEADS = 8
D_ATTN = N_HEADS * HEAD_DIM
N_PAIRS = N_HEADS // 2
PAIR = 2 * HEAD_DIM
ATTN_SCALE = 1.0 / math.sqrt(HEAD_DIM)
CONV_A_WIDTH = 31
CONV_A_HALO = 32
CONV_C_WIDTH = 3
CONV_C_HALO = 8
NEG_BIG = -1e30
ADAM_LR, ADAM_B1, ADAM_B2, ADAM_EPS, ADAM_WD, ADAM_STEP = 0.001, 0.9, 0.999, 1e-08, 0.01, 10

V7X_VMEM_BYTES = 64 * 1024 * 1024
V7X_LANES = 128
BF16_ROWS = 16
MIB = 1024 * 1024

NT = (((1,), (1,)), ((), ()))
TN = (((0,), (0,)), ((), ()))


def _call(body, *, name, out_shape, in_specs, out_specs, grid=(), scratch=(), dims=None, vmem_mb=32, **kw):
    params = dict(vmem_limit_bytes=min(vmem_mb * MIB, V7X_VMEM_BYTES - 4 * MIB))
    if dims is not None:
        params["dimension_semantics"] = dims
    return pl.pallas_call(
        body, name=name, grid=grid, in_specs=in_specs, out_specs=out_specs, out_shape=out_shape,
        scratch_shapes=list(scratch), compiler_params=pltpu.CompilerParams(**params), **kw)


def _tile(n, want=512):
    return want if n % want == 0 else n


def _rows(tm, d, col=0):
    return pl.BlockSpec((tm, d), lambda i: (i, col))


def _const(shape):
    return pl.BlockSpec(shape, lambda *_: (0,) * len(shape))


ANY = pl.BlockSpec(memory_space=pl.ANY)
VMEM = pl.BlockSpec(memory_space=pltpu.VMEM)


def _sigmoid(x):
    return 1.0 / (1.0 + jnp.exp(-x))


def _rmsnorm_fwd(x, gain, name):
    s, d = x.shape
    tm = _tile(s)

    def body(x_ref, g_ref, o_ref):
        xv = x_ref[...]
        r = lax.rsqrt(jnp.mean(xv * xv, axis=-1, keepdims=True) + EPS)
        o_ref[...] = (xv * r * g_ref[...]).astype(BF16)

    return _call(body, name=name, grid=(s // tm,), in_specs=[_rows(tm, d), _const((1, d))],
                 out_specs=_rows(tm, d), out_shape=SDS((s, d), BF16), dims=("parallel",))(x, gain)


def _rmsnorm_bwd(x, gain, dxn, dres, name):
    s, d = x.shape
    tm = _tile(s)

    def body(x_ref, g_ref, dxn_ref, dres_ref, dx_ref, dg_ref):
        xv = x_ref[...]
        r = lax.rsqrt(jnp.mean(xv * xv, axis=-1, keepdims=True) + EPS)
        xh = xv * r
        dv = dxn_ref[...]

        @pl.when(pl.program_id(0) == 0)
        def _():
            dg_ref[...] = jnp.zeros_like(dg_ref)

        dg_ref[...] += jnp.sum(dv * xh, axis=0, keepdims=True)
        dxh = dv * g_ref[...]
        dx_ref[...] = dres_ref[...] + r * (dxh - xh * jnp.mean(dxh * xh, axis=-1, keepdims=True))

    return _call(body, name=name, grid=(s // tm,),
                 in_specs=[_rows(tm, d), _const((1, d)), _rows(tm, d), _rows(tm, d)],
                 out_specs=[_rows(tm, d), _const((1, d))],
                 out_shape=[SDS((s, d), F32), SDS((1, d), F32)], dims=("arbitrary",))(x, gain, dxn, dres)


def _col_tile(n):
    for t in (1024, 768, 512, 256, 128):
        if n % t == 0:
            return t
    return n


def _mm(a, b, mode, name, out_dtype=F32, add=None):
    if mode == "tn":
        k, m = a.shape
        n = b.shape[1]
        bm = 256 if m % 256 == 0 else m

        def body_tn(a_ref, b_ref, o_ref):
            o_ref[...] = lax.dot_general(a_ref[...].astype(BF16), b_ref[...].astype(BF16), TN,
                                         preferred_element_type=F32).astype(out_dtype)

        return _call(body_tn, name=name, grid=(m // bm,),
                     in_specs=[pl.BlockSpec((k, bm), lambda i: (0, i)), _const((k, n))],
                     out_specs=pl.BlockSpec((bm, n), lambda i: (i, 0)),
                     out_shape=SDS((m, n), out_dtype), dims=("parallel",), vmem_mb=48)(a, b)
    m, k = a.shape
    n = b.shape[0] if mode == "nt" else b.shape[1]
    tm, tn = _tile(m), _col_tile(n)
    dn = NT if mode == "nt" else (((1,), (0,)), ((), ()))

    def body(a_ref, b_ref, *rest):
        o_ref = rest[-1]
        acc = lax.dot_general(a_ref[...].astype(BF16), b_ref[...].astype(BF16), dn, preferred_element_type=F32)
        if add is not None:
            acc = acc + rest[0][...]
        o_ref[...] = acc.astype(out_dtype)

    b_spec = (pl.BlockSpec((tn, k), lambda i, j: (j, 0)) if mode == "nt"
              else pl.BlockSpec((k, tn), lambda i, j: (0, j)))
    in_specs = [pl.BlockSpec((tm, k), lambda i, j: (i, 0)), b_spec]
    args = [a, b]
    if add is not None:
        in_specs.append(pl.BlockSpec((tm, tn), lambda i, j: (i, j)))
        args.append(add)
    return _call(body, name=name, grid=(m // tm, n // tn), in_specs=in_specs,
                 out_specs=pl.BlockSpec((tm, tn), lambda i, j: (i, j)),
                 out_shape=SDS((m, n), out_dtype), dims=("parallel", "parallel"), vmem_mb=48)(*args)


def _mm_tn(a, b, name, out_dtype=F32):
    return _mm(a, b, "tn", name, out_dtype)


FFN_TM = 256
FFN_CHUNK = 256


def _load_ffn_weights(w_hbm, offs, fs, dsts, sems):
    copies = []
    for wi, (off, dst) in enumerate(zip(offs, dsts)):
        for j in range(N_DEV):
            cp = pltpu.make_async_copy(w_hbm.at[j, pl.ds(off, fs), :], dst.at[pl.ds(j * fs, fs), :],
                                       sems.at[wi * N_DEV + j])
            cp.start()
            copies.append(cp)
    for cp in copies:
        cp.wait()


def _ffn_fwd(x, xn, wall, offs, fs, name):
    s, d = x.shape
    f = fs * N_DEV
    tm, ch = _tile(s, FFN_TM), FFN_CHUNK

    def body(x_ref, xn_ref, w_hbm, out_ref, g_ref, u_ref, h_ref, wg_s, wu_s, wd_s, sems):
        @pl.when(pl.program_id(0) == 0)
        def _():
            _load_ffn_weights(w_hbm, offs, fs, (wg_s, wu_s, wd_s), sems)

        xnv = xn_ref[...]
        acc = jnp.zeros((tm, d), F32)
        for c in range(f // ch):
            sl = slice(c * ch, (c + 1) * ch)
            gb = lax.dot_general(xnv, wg_s[sl, :], NT, preferred_element_type=F32).astype(BF16)
            ub = lax.dot_general(xnv, wu_s[sl, :], NT, preferred_element_type=F32).astype(BF16)
            g_ref[:, sl] = gb
            u_ref[:, sl] = ub
            g = gb.astype(F32)
            hb = (g * _sigmoid(g) * ub.astype(F32)).astype(BF16)
            h_ref[:, sl] = hb
            acc = acc + jnp.dot(hb, wd_s[sl, :], preferred_element_type=F32)
        out_ref[...] = x_ref[...] + FFN_RES * acc

    return _call(body, name=name, grid=(s // tm,),
                 in_specs=[_rows(tm, d), _rows(tm, d), ANY],
                 out_specs=[_rows(tm, d), _rows(tm, f), _rows(tm, f), _rows(tm, f)],
                 out_shape=[SDS((s, d), F32), SDS((s, f), BF16), SDS((s, f), BF16), SDS((s, f), BF16)],
                 scratch=[pltpu.VMEM((f, d), BF16)] * 3 + [pltpu.SemaphoreType.DMA((3 * N_DEV,))],
                 dims=("arbitrary",), vmem_mb=56)(x, xn, wall)


def _ffn_bwd_act(dout, g, u, wall, offs, fs, name):
    s, d = dout.shape
    f = fs * N_DEV
    tm, ch = _tile(s, FFN_TM), FFN_CHUNK

    def body(dout_ref, g_ref, u_ref, w_hbm, dg_ref, du_ref, dxn_ref, wg_s, wu_s, wd_s, sems):
        @pl.when(pl.program_id(0) == 0)
        def _():
            _load_ffn_weights(w_hbm, offs, fs, (wg_s, wu_s, wd_s), sems)

        dy = (FFN_RES * dout_ref[...]).astype(BF16)
        acc = jnp.zeros((tm, d), F32)
        for c in range(f // ch):
            sl = slice(c * ch, (c + 1) * ch)
            dh = lax.dot_general(dy, wd_s[sl, :], NT, preferred_element_type=F32)
            gv = g_ref[:, sl].astype(F32)
            uv = u_ref[:, sl].astype(F32)
            sg = _sigmoid(gv)
            dgb = (dh * uv * sg * (1.0 + gv * (1.0 - sg))).astype(BF16)
            dub = (dh * gv * sg).astype(BF16)
            dg_ref[:, sl] = dgb
            du_ref[:, sl] = dub
            acc = acc + jnp.dot(dgb, wg_s[sl, :], preferred_element_type=F32)
            acc = acc + jnp.dot(dub, wu_s[sl, :], preferred_element_type=F32)
        dxn_ref[...] = acc

    return _call(body, name=name, grid=(s // tm,),
                 in_specs=[_rows(tm, d), _rows(tm, f), _rows(tm, f), ANY],
                 out_specs=[_rows(tm, f), _rows(tm, f), _rows(tm, d)],
                 out_shape=[SDS((s, f), BF16), SDS((s, f), BF16), SDS((s, d), F32)],
                 scratch=[pltpu.VMEM((f, d), BF16)] * 3 + [pltpu.SemaphoreType.DMA((3 * N_DEV,))],
                 dims=("arbitrary",), vmem_mb=56)(dout, g, u, wall)


def _prev_rows(halo, tm, c, col):
    return pl.BlockSpec((halo, c), lambda i: (jnp.maximum(i * (tm // halo) - 1, 0), col))


def _next_rows(halo, tm, c, col, n_blocks):
    return pl.BlockSpec((halo, c), lambda i: (jnp.minimum((i + 1) * (tm // halo), n_blocks - 1), col))


def _conv_a_fwd(z, cw, cb, cn, name):
    s = z.shape[0]
    c = cb.shape[1]
    tm, halo, kw = _tile(s), CONV_A_HALO, CONV_A_WIDTH

    def body(u_ref, g_ref, up_ref, gp_ref, cw_ref, cb_ref, cn_ref, a_ref, a1_ref, buf):
        i = pl.program_id(0)
        buf[0:halo, :] = jnp.where(i > 0, up_ref[...] * _sigmoid(gp_ref[...]), 0.0)
        buf[halo:halo + tm, :] = u_ref[...] * _sigmoid(g_ref[...])
        acc = jnp.zeros((tm, c), F32)
        for k in range(kw):
            acc = acc + cw_ref[k:k + 1, :] * buf[pl.ds(halo - (kw - 1) + k, tm), :]
        a1 = acc + cb_ref[...]
        a1_ref[...] = a1
        a2 = a1 * lax.rsqrt(jnp.mean(a1 * a1, axis=-1, keepdims=True) + EPS) * cn_ref[...]
        a_ref[...] = (a2 * _sigmoid(a2)).astype(BF16)

    return _call(body, name=name, grid=(s // tm,),
                 in_specs=[_rows(tm, c, 0), _rows(tm, c, 1), _prev_rows(halo, tm, c, 0), _prev_rows(halo, tm, c, 1),
                           _const(cw.shape), _const((1, c)), _const((1, c))],
                 out_specs=[_rows(tm, c), _rows(tm, c)],
                 out_shape=[SDS((s, c), BF16), SDS((s, c), F32)],
                 scratch=[pltpu.VMEM((tm + halo, c), F32)], dims=("parallel",))(z, z, z, z, cw, cb, cn)


def _conv_a_bwd_norm(dao, a1, cn, name):
    s, c = a1.shape
    tm = _tile(s)

    def body(da_ref, a1_ref, cn_ref, da1_ref, dcn_ref, dcb_ref):
        a1v = a1_ref[...]
        r = lax.rsqrt(jnp.mean(a1v * a1v, axis=-1, keepdims=True) + EPS)
        xh = a1v * r
        a2 = xh * cn_ref[...]
        sg = _sigmoid(a2)
        da2 = da_ref[...] * sg * (1.0 + a2 * (1.0 - sg))
        dxh = da2 * cn_ref[...]
        da1 = r * (dxh - xh * jnp.mean(dxh * xh, axis=-1, keepdims=True))
        da1_ref[...] = da1

        @pl.when(pl.program_id(0) == 0)
        def _():
            dcn_ref[...] = jnp.zeros_like(dcn_ref)
            dcb_ref[...] = jnp.zeros_like(dcb_ref)

        dcn_ref[...] += jnp.sum(da2 * xh, axis=0, keepdims=True)
        dcb_ref[...] += jnp.sum(da1, axis=0, keepdims=True)

    return _call(body, name=name, grid=(s // tm,),
                 in_specs=[_rows(tm, c, 0), _rows(tm, c), _const((1, c))],
                 out_specs=[_rows(tm, c), _const((1, c)), _const((1, c))],
                 out_shape=[SDS((s, c), F32), SDS((1, c), F32), SDS((1, c), F32)], dims=("arbitrary",))(dao, a1, cn)


def _conv_a_bwd_conv(da1, z, cw, name):
    s, c = da1.shape
    tm, halo, kw = _tile(s), CONV_A_HALO, CONV_A_WIDTH
    n = s // tm

    def body(d_ref, dn_ref, u_ref, g_ref, up_ref, gp_ref, cw_ref, du_ref, dg_ref, dcw_ref, buf, bd):
        i = pl.program_id(0)
        uv = u_ref[...]
        sg = _sigmoid(g_ref[...])
        buf[0:halo, :] = jnp.where(i > 0, up_ref[...] * _sigmoid(gp_ref[...]), 0.0)
        buf[halo:halo + tm, :] = uv * sg
        dv = d_ref[...]
        bd[0:tm, :] = dv
        bd[tm:tm + halo, :] = jnp.where(i < n - 1, dn_ref[...], 0.0)

        @pl.when(i == 0)
        def _():
            dcw_ref[...] = jnp.zeros_like(dcw_ref)

        da0 = jnp.zeros((tm, c), F32)
        for k in range(kw):
            da0 = da0 + cw_ref[k:k + 1, :] * bd[pl.ds(kw - 1 - k, tm), :]
            dcw_ref[k:k + 1, :] += jnp.sum(dv * buf[pl.ds(halo - (kw - 1) + k, tm), :], axis=0, keepdims=True)
        du_ref[...] = (da0 * sg).astype(BF16)
        dg_ref[...] = (da0 * uv * sg * (1.0 - sg)).astype(BF16)

    return _call(body, name=name, grid=(n,),
                 in_specs=[_rows(tm, c), _next_rows(halo, tm, c, 0, s // halo), _rows(tm, c, 0), _rows(tm, c, 1),
                           _prev_rows(halo, tm, c, 0), _prev_rows(halo, tm, c, 1), _const(cw.shape)],
                 out_specs=[_rows(tm, c), _rows(tm, c), _const(cw.shape)],
                 out_shape=[SDS((s, c), BF16), SDS((s, c), BF16), SDS(cw.shape, F32)],
                 scratch=[pltpu.VMEM((tm + halo, c), F32)] * 2, dims=("arbitrary",))(da1, da1, z, z, z, z, cw)


def _lane_is_first_head(tm):
    return lax.broadcasted_iota(jnp.int32, (tm, PAIR), 1) < HEAD_DIM


def _pair_rms(xp, first):
    x2 = xp * xp
    s0 = jnp.sum(jnp.where(first, x2, 0.0), axis=-1, keepdims=True)
    s1 = jnp.sum(jnp.where(first, 0.0, x2), axis=-1, keepdims=True)
    return jnp.where(first, lax.rsqrt(s0 / HEAD_DIM + EPS), lax.rsqrt(s1 / HEAD_DIM + EPS))


def _split3(x):
    hi = x.astype(BF16)
    r1 = x - hi.astype(F32)
    mid = r1.astype(BF16)
    lo = (r1 - mid.astype(F32)).astype(BF16)
    return hi, mid, lo


def _qk_fwd(z, flog, bf, qn2, kn2, name):
    s = z.shape[0]
    tm = _tile(s)
    col0 = (z.shape[1] - 3 * D_ATTN) // D_ATTN

    def body(q_ref, k_ref, v_ref, fl_ref, bf_ref, qn_ref, kn_ref,
             qs_ref, kh_ref, vb_ref, fb_ref, ft_ref, xt_ref, carry):
        i = pl.program_id(0)
        first = _lane_is_first_head(tm)
        for p in range(N_PAIRS):
            sl = slice(p * PAIR, (p + 1) * PAIR)
            q = q_ref[:, sl]
            qs_ref[:, sl] = (q * _pair_rms(q, first) * qn_ref[...] * ATTN_SCALE).astype(BF16)
            k = k_ref[:, sl]
            kh_ref[:, sl] = (k * _pair_rms(k, first) * kn_ref[...]).astype(BF16)
        vb_ref[...] = v_ref[...].astype(BF16)

        xg = fl_ref[...] + bf_ref[...]
        valid = lax.broadcasted_iota(jnp.int32, (tm, V7X_LANES), 1) < N_HEADS
        ls = jnp.where(valid, jnp.minimum(xg, 0.0) - jnp.log(1.0 + jnp.exp(-jnp.abs(xg))), 0.0)
        tri = (lax.broadcasted_iota(jnp.int32, (tm, tm), 1) <= lax.broadcasted_iota(jnp.int32, (tm, tm), 0)).astype(BF16)
        cs = jnp.zeros((tm, V7X_LANES), F32)
        for part in _split3(ls):
            cs = cs + jnp.dot(tri, part, preferred_element_type=F32)

        @pl.when(i == 0)
        def _():
            carry[...] = jnp.zeros_like(carry)

        fv = cs + carry[0:1, :]
        carry[0:1, :] = fv[tm - 1:tm, :]
        ft_ref[...] = fv.T[0:N_HEADS, :]
        xt_ref[...] = xg.T[0:N_HEADS, :]
        for p in range(N_PAIRS):
            fb_ref[:, p * PAIR:(p + 1) * PAIR] = jnp.where(first, fv[:, 2 * p:2 * p + 1], fv[:, 2 * p + 1:2 * p + 2])

    wide = lambda col: pl.BlockSpec((tm, D_ATTN), lambda i: (i, col))
    tcol = pl.BlockSpec((N_HEADS, tm), lambda i: (0, i))
    return _call(body, name=name, grid=(s // tm,),
                 in_specs=[wide(col0), wide(col0 + 1), wide(col0 + 2), _rows(tm, V7X_LANES),
                           _const((1, V7X_LANES)), _const((1, PAIR)), _const((1, PAIR))],
                 out_specs=[wide(0), wide(0), wide(0), wide(0), tcol, tcol],
                 out_shape=[SDS((s, D_ATTN), BF16)] * 3 + [SDS((s, D_ATTN), F32), SDS((N_HEADS, s), F32),
                                                          SDS((N_HEADS, s), F32)],
                 scratch=[pltpu.VMEM((8, V7X_LANES), F32)], dims=("arbitrary",))(z, z, z, flog, bf, qn2, kn2)


def _head_logits(qp, kp, fq, ft_ref, p, h, first, i, j, tq, tk):
    qm = jnp.where(first if h == 0 else jnp.logical_not(first), qp, jnp.zeros_like(qp))
    sc = lax.dot_general(qm, kp, NT, preferred_element_type=F32)
    sc = sc + fq[:, h * HEAD_DIM:h * HEAD_DIM + 1] - ft_ref[pl.ds(2 * p + h, 1), :]
    row = i * tq + lax.broadcasted_iota(jnp.int32, (tq, tk), 0)
    col = j * tk + lax.broadcasted_iota(jnp.int32, (tq, tk), 1)
    return jnp.where(col <= row, sc, NEG_BIG)


def _attn_fwd(qs, kh, vb, fb, ft, name):
    s = qs.shape[0]
    tq = tk = _tile(s)
    nq = s // tq

    def body(q_ref, k_ref, v_ref, fq_ref, ft_ref, o_ref, lse_ref, m_s, l_s, acc_s):
        p, i, j = pl.program_id(0), pl.program_id(1), pl.program_id(2)
        first = _lane_is_first_head(tq)

        @pl.when(j == 0)
        def _():
            m_s[...] = jnp.full_like(m_s, NEG_BIG)
            l_s[...] = jnp.zeros_like(l_s)
            acc_s[...] = jnp.zeros_like(acc_s)

        @pl.when(j <= i)
        def _():
            qp, kp, vp, fq = q_ref[...], k_ref[...], v_ref[...], fq_ref[...]
            for h in range(2):
                sc = _head_logits(qp, kp, fq, ft_ref, p, h, first, i, j, tq, tk)
                m_old = m_s[h]
                m_new = jnp.maximum(m_old, jnp.max(sc, axis=-1, keepdims=True))
                pr = jnp.exp(sc - m_new)
                alpha = jnp.exp(m_old - m_new)
                l_s[h] = alpha * l_s[h] + jnp.sum(pr, axis=-1, keepdims=True)
                m_s[h] = m_new
                pv = jnp.dot(pr.astype(BF16), vp, preferred_element_type=F32)
                mine = first if h == 0 else jnp.logical_not(first)
                acc_s[...] = jnp.where(mine, alpha * acc_s[...] + pv, acc_s[...])

        @pl.when(j == i)
        def _():
            l_pair = jnp.where(first, l_s[0], l_s[1])
            m_pair = jnp.where(first, m_s[0], m_s[1])
            o_ref[...] = acc_s[...] / l_pair
            lse_ref[...] = m_pair + jnp.log(l_pair)

    qblk = pl.BlockSpec((tq, PAIR), lambda p, i, j: (i, p))
    kblk = pl.BlockSpec((tk, PAIR), lambda p, i, j: (jnp.minimum(j, i), p))
    return _call(body, name=name, grid=(N_PAIRS, nq, nq),
                 in_specs=[qblk, kblk, kblk, qblk, pl.BlockSpec((N_HEADS, tk), lambda p, i, j: (0, jnp.minimum(j, i)))],
                 out_specs=[qblk, qblk], out_shape=[SDS((s, D_ATTN), F32)] * 2,
                 scratch=[pltpu.VMEM((2, tq, 1), F32), pltpu.VMEM((2, tq, 1), F32), pltpu.VMEM((tq, PAIR), F32)],
                 dims=("parallel", "parallel", "arbitrary"))(qs, kh, vb, fb, ft)


def _attn_probs_and_ds(q_ref, k_ref, v_ref, fq_ref, ft_ref, lse_ref, o_ref, do_ref, p, h, first, i, j, tq, tk):
    mine = first if h == 0 else jnp.logical_not(first)
    sc = _head_logits(q_ref[...], k_ref[...], fq_ref[...], ft_ref, p, h, first, i, j, tq, tk)
    pr = jnp.exp(sc - lse_ref[:, h * HEAD_DIM:h * HEAD_DIM + 1])
    dov = jnp.where(mine, do_ref[...], 0.0)
    dsum = jnp.sum(dov * o_ref[...], axis=-1, keepdims=True)
    dom = dov.astype(BF16)
    dom_lo = (dov - dom.astype(F32)).astype(BF16)
    vv = v_ref[...]
    dp = lax.dot_general(dom, vv, NT, preferred_element_type=F32)
    dp = dp + lax.dot_general(dom_lo, vv, NT, preferred_element_type=F32)
    return pr, pr * (dp - dsum), dom, mine


def _attn_bwd_dq(qs, kh, vb, fb, ft, lse, o, dao, name):
    s = qs.shape[0]
    tq = tk = _tile(s)
    nq = s // tq

    def body(q_ref, k_ref, v_ref, fq_ref, ft_ref, lse_ref, o_ref, do_ref, dq_ref, df_ref, acc_s, rs_s):
        p, i, j = pl.program_id(0), pl.program_id(1), pl.program_id(2)
        first = _lane_is_first_head(tq)

        @pl.when(j == 0)
        def _():
            acc_s[...] = jnp.zeros_like(acc_s)
            rs_s[...] = jnp.zeros_like(rs_s)

        @pl.when(j <= i)
        def _():
            for h in range(2):
                _, ds, _, mine = _attn_probs_and_ds(q_ref, k_ref, v_ref, fq_ref, ft_ref, lse_ref, o_ref, do_ref,
                                                    p, h, first, i, j, tq, tk)
                dq = jnp.dot(ds.astype(BF16), k_ref[...], preferred_element_type=F32)
                acc_s[...] += jnp.where(mine, dq, 0.0)
                rs_s[h] += jnp.sum(ds, axis=-1, keepdims=True)

        @pl.when(j == i)
        def _():
            dq_ref[...] = acc_s[...]
            as_rows = jnp.where(first, rs_s[0], rs_s[1]).T
            df_ref[0, 0:1, :] = as_rows[0:1, :]
            df_ref[0, 1:2, :] = as_rows[HEAD_DIM:HEAD_DIM + 1, :]
            df_ref[0, 2:8, :] = jnp.zeros((6, tq), F32)

    qblk = pl.BlockSpec((tq, PAIR), lambda p, i, j: (i, p))
    kblk = pl.BlockSpec((tk, PAIR), lambda p, i, j: (jnp.minimum(j, i), p))
    doblk = pl.BlockSpec((tq, PAIR), lambda p, i, j: (i, N_PAIRS + p))
    return _call(body, name=name, grid=(N_PAIRS, nq, nq),
                 in_specs=[qblk, kblk, kblk, qblk, pl.BlockSpec((N_HEADS, tk), lambda p, i, j: (0, jnp.minimum(j, i))),
                           qblk, qblk, doblk],
                 out_specs=[qblk, pl.BlockSpec((1, 8, tq), lambda p, i, j: (p, 0, i))],
                 out_shape=[SDS((s, D_ATTN), F32), SDS((N_PAIRS, 8, s), F32)],
                 scratch=[pltpu.VMEM((tq, PAIR), F32), pltpu.VMEM((2, tq, 1), F32)],
                 dims=("parallel", "parallel", "arbitrary"))(qs, kh, vb, fb, ft, lse, o, dao)


def _attn_bwd_dkv(qs, kh, vb, fb, ft, lse, o, dao, name):
    s = qs.shape[0]
    tq = tk = _tile(s)
    nq = s // tq

    def body(q_ref, k_ref, v_ref, fq_ref, ft_ref, lse_ref, o_ref, do_ref, dk_ref, dv_ref, df_ref, dk_s, dv_s, df_s):
        p, j, i = pl.program_id(0), pl.program_id(1), pl.program_id(2)
        first = _lane_is_first_head(tq)

        @pl.when(i == 0)
        def _():
            dk_s[...] = jnp.zeros_like(dk_s)
            dv_s[...] = jnp.zeros_like(dv_s)
            df_s[...] = jnp.zeros_like(df_s)

        @pl.when(i >= j)
        def _():
            for h in range(2):
                pr, ds, dom, mine = _attn_probs_and_ds(q_ref, k_ref, v_ref, fq_ref, ft_ref, lse_ref, o_ref, do_ref,
                                                       p, h, first, i, j, tq, tk)
                dk = lax.dot_general(ds.astype(BF16), q_ref[...], TN, preferred_element_type=F32)
                dv = lax.dot_general(pr.astype(BF16), dom, TN, preferred_element_type=F32)
                dk_s[...] += jnp.where(mine, dk, 0.0)
                dv_s[...] += jnp.where(mine, dv, 0.0)
                df_s[h:h + 1, :] -= jnp.sum(ds, axis=0, keepdims=True)

        @pl.when(i == nq - 1)
        def _():
            dk_ref[...] = dk_s[...]
            dv_ref[...] = dv_s[...]
            df_ref[0] = df_s[...]

    qi = lambda p, j, i: (jnp.maximum(i, j), p)
    qblk = pl.BlockSpec((tq, PAIR), qi)
    kblk = pl.BlockSpec((tk, PAIR), lambda p, j, i: (j, p))
    doblk = pl.BlockSpec((tq, PAIR), lambda p, j, i: (jnp.maximum(i, j), N_PAIRS + p))
    return _call(body, name=name, grid=(N_PAIRS, nq, nq),
                 in_specs=[qblk, kblk, kblk, qblk, pl.BlockSpec((N_HEADS, tk), lambda p, j, i: (0, j)),
                           qblk, qblk, doblk],
                 out_specs=[kblk, kblk, pl.BlockSpec((1, 8, tk), lambda p, j, i: (p, 0, j))],
                 out_shape=[SDS((s, D_ATTN), F32), SDS((s, D_ATTN), F32), SDS((N_PAIRS, 8, s), F32)],
                 scratch=[pltpu.VMEM((tk, PAIR), F32), pltpu.VMEM((tk, PAIR), F32), pltpu.VMEM((8, tk), F32)],
                 dims=("parallel", "parallel", "arbitrary"))(qs, kh, vb, fb, ft, lse, o, dao)


def _qk_bwd(z, dqs, dkh, dv, qn2, kn2, name):
    s = z.shape[0]
    tm = _tile(s)
    col0 = (z.shape[1] - 3 * D_ATTN) // D_ATTN

    def body(q_ref, k_ref, dqs_ref, dkh_ref, dv_ref, qn_ref, kn_ref, dq_ref, dk_ref, dvb_ref, dqn_ref, dkn_ref):
        first = _lane_is_first_head(tm)

        @pl.when(pl.program_id(0) == 0)
        def _():
            dqn_ref[...] = jnp.zeros_like(dqn_ref)
            dkn_ref[...] = jnp.zeros_like(dkn_ref)

        def through(x_ref, dy_ref, gain_ref, dx_ref, dgain_ref, scale):
            for p in range(N_PAIRS):
                sl = slice(p * PAIR, (p + 1) * PAIR)
                xv = x_ref[:, sl]
                r = _pair_rms(xv, first)
                xh = xv * r
                dy = dy_ref[:, sl] * scale
                dgain_ref[:, sl] += jnp.sum(dy * xh, axis=0, keepdims=True)
                dxh = dy * gain_ref[...]
                t = dxh * xh
                m0 = jnp.sum(jnp.where(first, t, 0.0), axis=-1, keepdims=True)
                m1 = jnp.sum(jnp.where(first, 0.0, t), axis=-1, keepdims=True)
                mean = jnp.where(first, m0, m1) / HEAD_DIM
                dx_ref[:, sl] = (r * (dxh - xh * mean)).astype(BF16)

        through(q_ref, dqs_ref, qn_ref, dq_ref, dqn_ref, ATTN_SCALE)
        through(k_ref, dkh_ref, kn_ref, dk_ref, dkn_ref, 1.0)
        dvb_ref[...] = dv_ref[...].astype(BF16)

    wide = lambda col: pl.BlockSpec((tm, D_ATTN), lambda i: (i, col))
    return _call(body, name=name, grid=(s // tm,),
                 in_specs=[wide(col0), wide(col0 + 1), wide(0), wide(0), wide(0), _const((1, PAIR)), _const((1, PAIR))],
                 out_specs=[wide(0), wide(0), wide(0), _const((1, D_ATTN)), _const((1, D_ATTN))],
                 out_shape=[SDS((s, D_ATTN), BF16)] * 3 + [SDS((1, D_ATTN), F32)] * 2,
                 dims=("arbitrary",))(z, z, dqs, dkh, dv, qn2, kn2)


def _gate_bwd(dft, xt, name):
    s = xt.shape[1]
    tm = _tile(s)
    n = s // tm

    def body(df_ref, xt_ref, dxt_ref, dx_ref, db_ref, carry):
        i = pl.program_id(0)

        @pl.when(i == 0)
        def _():
            carry[...] = jnp.zeros_like(carry)
            db_ref[...] = jnp.zeros_like(db_ref)

        tri = (lax.broadcasted_iota(jnp.int32, (tm, tm), 0) >= lax.broadcasted_iota(jnp.int32, (tm, tm), 1)).astype(BF16)
        rc = jnp.zeros((N_HEADS, tm), F32)
        for part in _split3(df_ref[...]):
            rc = rc + jnp.dot(part, tri, preferred_element_type=F32)
        dls = rc + carry[:, 0:1]
        carry[...] = jnp.broadcast_to(dls[:, 0:1], carry.shape)
        dxt = dls * _sigmoid(-xt_ref[...])
        dxt_ref[...] = dxt
        db_ref[...] += jnp.broadcast_to(jnp.sum(dxt, axis=-1, keepdims=True), db_ref.shape)
        padded = jnp.concatenate([dxt, jnp.zeros((V7X_LANES - N_HEADS, tm), F32)], axis=0)
        dx_ref[...] = padded.T

    rev = pl.BlockSpec((N_HEADS, tm), lambda i: (0, n - 1 - i))
    return _call(body, name=name, grid=(n,), in_specs=[rev, rev],
                 out_specs=[rev, pl.BlockSpec((tm, V7X_LANES), lambda i: (n - 1 - i, 0)), _const((N_HEADS, V7X_LANES))],
                 out_shape=[SDS((N_HEADS, s), F32), SDS((s, V7X_LANES), F32), SDS((N_HEADS, V7X_LANES), F32)],
                 scratch=[pltpu.VMEM((N_HEADS, V7X_LANES), F32)], dims=("arbitrary",))(dft, xt)


def _conv_c_fwd(z, cw, name):
    s = z.shape[0]
    c = z.shape[1] // 3
    tm, halo, kw = _tile(s), CONV_C_HALO, CONV_C_WIDTH

    def body(gb_ref, gc_ref, hh_ref, gcp_ref, hhp_ref, cw_ref, y_ref, buf):
        i = pl.program_id(0)
        buf[0:halo, :] = jnp.where(i > 0, gcp_ref[...] * hhp_ref[...], 0.0)
        buf[halo:halo + tm, :] = gc_ref[...] * hh_ref[...]
        c1 = jnp.zeros((tm, c), F32)
        for k in range(kw):
            c1 = c1 + cw_ref[k:k + 1, :] * buf[pl.ds(halo - (kw - 1) + k, tm), :]
        y_ref[...] = (gb_ref[...] * c1).astype(BF16)

    return _call(body, name=name, grid=(s // tm,),
                 in_specs=[_rows(tm, c, 0), _rows(tm, c, 1), _rows(tm, c, 2), _prev_rows(halo, tm, c, 1),
                           _prev_rows(halo, tm, c, 2), _const(cw.shape)],
                 out_specs=_rows(tm, c), out_shape=SDS((s, c), BF16),
                 scratch=[pltpu.VMEM((tm + halo, c), F32)], dims=("parallel",))(z, z, z, z, z, cw)


def _conv_c_bwd(dy0, z, cw, name):
    s = z.shape[0]
    c = z.shape[1] // 3
    tm, halo, kw = _tile(s), CONV_C_HALO, CONV_C_WIDTH
    n = s // tm

    def body(dy_ref, dyn_ref, gb_ref, gbn_ref, gc_ref, hh_ref, gcp_ref, hhp_ref, cw_ref, dz_ref, dcw_ref, buf, bd):
        i = pl.program_id(0)
        gcv, hhv, dyv = gc_ref[...], hh_ref[...], dy_ref[...]
        buf[0:halo, :] = jnp.where(i > 0, gcp_ref[...] * hhp_ref[...], 0.0)
        buf[halo:halo + tm, :] = gcv * hhv
        dc1 = dyv * gb_ref[...]
        bd[0:tm, :] = dc1
        bd[tm:tm + halo, :] = jnp.where(i < n - 1, dyn_ref[...] * gbn_ref[...], 0.0)

        @pl.when(i == 0)
        def _():
            dcw_ref[...] = jnp.zeros_like(dcw_ref)

        c1 = jnp.zeros((tm, c), F32)
        dc0 = jnp.zeros((tm, c), F32)
        for k in range(kw):
            shifted = buf[pl.ds(halo - (kw - 1) + k, tm), :]
            c1 = c1 + cw_ref[k:k + 1, :] * shifted
            dc0 = dc0 + cw_ref[k:k + 1, :] * bd[pl.ds(kw - 1 - k, tm), :]
            dcw_ref[k:k + 1, :] += jnp.sum(dc1 * shifted, axis=0, keepdims=True)
        dz_ref[:, 0:c] = (dyv * c1).astype(BF16)
        dz_ref[:, c:2 * c] = (dc0 * hhv).astype(BF16)
        dz_ref[:, 2 * c:3 * c] = (dc0 * gcv).astype(BF16)

    return _call(body, name=name, grid=(n,),
                 in_specs=[_rows(tm, c), _next_rows(halo, tm, c, 0, s // halo), _rows(tm, c, 0),
                           _next_rows(halo, tm, c, 0, s // halo), _rows(tm, c, 1), _rows(tm, c, 2),
                           _prev_rows(halo, tm, c, 1), _prev_rows(halo, tm, c, 2), _const(cw.shape)],
                 out_specs=[_rows(tm, 3 * c), _const(cw.shape)],
                 out_shape=[SDS((s, 3 * c), BF16), SDS(cw.shape, F32)],
                 scratch=[pltpu.VMEM((tm + halo, c), F32)] * 2, dims=("arbitrary",),
                 vmem_mb=48)(dy0, dy0, z, z, z, z, z, z, cw)


def _loss_head(y, target, name):
    s, d = y.shape
    tm = _tile(s)

    def body(y_ref, t_ref, loss_ref, dy_ref):
        e = y_ref[...] - t_ref[...]

        @pl.when(pl.program_id(0) == 0)
        def _():
            loss_ref[...] = jnp.zeros_like(loss_ref)

        loss_ref[...] += 0.5 * jnp.sum(jnp.mean(e * e, axis=-1, keepdims=True))
        dy_ref[...] = e / d

    return _call(body, name=name, grid=(s // tm,), in_specs=[_rows(tm, d), _rows(tm, d)],
                 out_specs=[_const((8, V7X_LANES)), _rows(tm, d)],
                 out_shape=[SDS((8, V7X_LANES), F32), SDS((s, d), F32)], dims=("arbitrary",))(y, target)


def _adamw(w, g, m, v, name):
    r, c = w.shape
    tr = next((t for t in (512, 256, 128, 64, 32, 16, 8) if r % t == 0), r)

    def body(w_ref, g_ref, m_ref, v_ref, d_ref, mo_ref, vo_ref):
        gv = g_ref[...]
        mn = ADAM_B1 * m_ref[...] + (1.0 - ADAM_B1) * gv
        vn = ADAM_B2 * v_ref[...] + (1.0 - ADAM_B2) * (gv * gv)
        m_hat = mn / (1.0 - ADAM_B1 ** ADAM_STEP)
        v_hat = vn / (1.0 - ADAM_B2 ** ADAM_STEP)
        d_ref[...] = -ADAM_LR * (m_hat / (jnp.sqrt(v_hat) + ADAM_EPS) + ADAM_WD * w_ref[...])
        mo_ref[...] = mn
        vo_ref[...] = vn

    spec = _rows(tr, c)
    return _call(body, name=name, grid=(r // tr,), in_specs=[spec] * 4, out_specs=[spec] * 3,
                 out_shape=[SDS((r, c), F32)] * 3, dims=("parallel",))(w, g, m, v)


def _position():
    return lax.axis_index("x"), lax.axis_index("y"), lax.axis_index("c")


def _other_chips(x, y):
    return [(1 - x, y), (x, 1 - y), (1 - x, 1 - y)]


def _dev_index(px, py, pc):
    return 4 * px + 2 * py + pc


def _all_gather(wloc):
    r, d = wloc.shape

    def body(x_ref, out_ref, send_sems, recv_sems, local_sem):
        x, y, c = _position()
        me, sibling = (x, y, c), (x, y, 1 - c)
        chips = _other_chips(x, y)

        def slot(dev):
            return out_ref.at[_dev_index(*dev)]

        def copy(k, block, to, src=None):
            return pltpu.make_async_remote_copy(
                src_ref=slot(block) if src is None else src, dst_ref=slot(block),
                send_sem=send_sems.at[k], recv_sem=recv_sems.at[k], device_id=to, device_id_type=MESH)

        mine = pltpu.make_async_copy(x_ref, slot(me), local_sem.at[0])
        mine.start()
        first = [copy(0, me, sibling, src=x_ref)]
        first += [copy(1 + j, me, (*chip, c), src=x_ref) for j, chip in enumerate(chips)]
        for cp in first:
            cp.start()
        passed = [copy(4 + j, (*chip, c), sibling) for j, chip in enumerate(chips)]
        for j, chip in enumerate(chips):
            copy(1 + j, (*chip, c), me).wait_recv()
            passed[j].start()
        copy(0, sibling, me).wait_recv()
        for j, chip in enumerate(chips):
            copy(4 + j, (*chip, 1 - c), me).wait_recv()
        for cp in first + passed:
            cp.wait_send()
        mine.wait()

    return _call(body, name="all_gather_weights", in_specs=[ANY], out_specs=ANY,
                 out_shape=SDS((N_DEV, r, d), wloc.dtype),
                 scratch=[pltpu.SemaphoreType.DMA((7,)), pltpu.SemaphoreType.DMA((7,)), pltpu.SemaphoreType.DMA((1,))])(wloc)


def _pair_exchange(gall):
    _, r, d = gall.shape

    def body(g_ref, out_ref, send_sems, recv_sems):
        x, y, c = _position()
        sibling = (x, y, 1 - c)
        dests = [sibling] + [(*chip, 1 - c) for chip in _other_chips(x, y)]
        copies = [pltpu.make_async_remote_copy(
            src_ref=g_ref.at[_dev_index(*dest)], dst_ref=out_ref.at[k], send_sem=send_sems.at[k],
            recv_sem=recv_sems.at[k], device_id=sibling, device_id_type=MESH) for k, dest in enumerate(dests)]
        for cp in copies:
            cp.start()
        for cp in copies:
            cp.wait()

    return _call(body, name="reduce_scatter_pair_exchange", in_specs=[ANY], out_specs=ANY,
                 out_shape=SDS((4, r, d), gall.dtype),
                 scratch=[pltpu.SemaphoreType.DMA((4,)), pltpu.SemaphoreType.DMA((4,))])(gall)


def _pair_sum(gall, sib, idx):
    _, r, d = gall.shape
    tr = next(t for t in (400, 208, 80, 16) if r % t == 0)

    def body(idx_ref, a_ref, b_ref, o_ref):
        o_ref[...] = (a_ref[...].astype(F32) + b_ref[...].astype(F32)).astype(o_ref.dtype)

    grid_spec = pltpu.PrefetchScalarGridSpec(
        num_scalar_prefetch=1, grid=(4, r // tr),
        in_specs=[pl.BlockSpec((1, tr, d), lambda k, i, idx_ref: (idx_ref[k], i, 0)),
                  pl.BlockSpec((1, tr, d), lambda k, i, idx_ref: (k, i, 0))],
        out_specs=pl.BlockSpec((1, tr, d), lambda k, i, idx_ref: (k, i, 0)))
    return pl.pallas_call(body, name="reduce_scatter_pair_sum", grid_spec=grid_spec,
                          out_shape=SDS((4, r, d), gall.dtype),
                          compiler_params=pltpu.CompilerParams(dimension_semantics=("parallel", "parallel")))(idx, gall, sib)


def _chip_exchange(tsum):
    _, r, d = tsum.shape

    def body(t_ref, out_ref, send_sems, recv_sems):
        x, y, c = _position()
        copies = [pltpu.make_async_remote_copy(
            src_ref=t_ref.at[1 + k], dst_ref=out_ref.at[k], send_sem=send_sems.at[k], recv_sem=recv_sems.at[k],
            device_id=(*chip, c), device_id_type=MESH) for k, chip in enumerate(_other_chips(x, y))]
        for cp in copies:
            cp.start()
        for cp in copies:
            cp.wait()

    return _call(body, name="reduce_scatter_chip_exchange", in_specs=[ANY], out_specs=ANY,
                 out_shape=SDS((3, r, d), tsum.dtype),
                 scratch=[pltpu.SemaphoreType.DMA((3,)), pltpu.SemaphoreType.DMA((3,))])(tsum)


def _final_sum(tsum, rcv):
    _, r, d = tsum.shape
    tr = next(t for t in (400, 208, 80, 16) if r % t == 0)

    def body(t_ref, r_ref, o_ref):
        acc = t_ref[0].astype(F32)
        for k in range(3):
            acc = acc + r_ref[k].astype(F32)
        o_ref[...] = acc

    return _call(body, name="reduce_scatter_final_sum", grid=(r // tr,),
                 in_specs=[pl.BlockSpec((1, tr, d), lambda i: (0, i, 0)), pl.BlockSpec((3, tr, d), lambda i: (0, i, 0))],
                 out_specs=_rows(tr, d), out_shape=SDS((r, d), F32), dims=("parallel",))(tsum, rcv)


def _all_reduce_small(buf):
    nr, lanes = buf.shape

    def body(b_ref, out_ref, gath, send_sems, recv_sems):
        x, y, c = _position()
        my_slot = _dev_index(x, y, c)
        gath[my_slot] = b_ref[...]
        copies = []
        for k in range(1, N_DEV):
            dx, dy, dc = (k >> 2) & 1, (k >> 1) & 1, k & 1
            peer = (1 - x if dx else x, 1 - y if dy else y, 1 - c if dc else c)
            copies.append(pltpu.make_async_remote_copy(
                src_ref=b_ref, dst_ref=gath.at[my_slot], send_sem=send_sems.at[k - 1], recv_sem=recv_sems.at[k - 1],
                device_id=peer, device_id_type=MESH))
        for cp in copies:
            cp.start()
        for cp in copies:
            cp.wait()
        acc = gath[0]
        for sidx in range(1, N_DEV):
            acc = acc + gath[sidx]
        out_ref[...] = acc

    return _call(body, name="all_reduce_small", in_specs=[VMEM], out_specs=VMEM, out_shape=SDS((nr, lanes), F32),
                 scratch=[pltpu.VMEM((N_DEV, nr, lanes), F32), pltpu.SemaphoreType.DMA((7,)),
                          pltpu.SemaphoreType.DMA((7,))])(buf)


def _ffn_block_fwd(x, gain, wall, offs, fs, tag):
    xn = _rmsnorm_fwd(x, gain, f"{tag}_norm")
    out, g, u, h = _ffn_fwd(x, xn, wall, offs, fs, f"{tag}_fwd")
    return out, (x, gain, xn, g, u, h)


def _ffn_block_bwd(dout, saved, wall, offs, fs, tag):
    x, gain, xn, g, u, h = saved
    dg, du, dxn = _ffn_bwd_act(dout, g, u, wall, offs, fs, f"{tag}_bwd_act")
    dy_b = (FFN_RES * dout).astype(BF16)
    dwg = _mm_tn(dg, xn, f"{tag}_dwg", BF16)
    dwu = _mm_tn(du, xn, f"{tag}_dwu", BF16)
    dwd = _mm_tn(h, dy_b, f"{tag}_dwd", BF16)
    dx, dgain = _rmsnorm_bwd(x, gain, dxn, dout, f"{tag}_norm_bwd")
    return dx, (dwg, dwu, dwd), dgain


def _local_step(x, target, wall, offs, fs, mixw, small):
    d = x.shape[1]
    d_conv = small["ev_conv_b"].shape[1]
    grads = {}

    x1, s_f1a = _ffn_block_fwd(x, small["ffn1_norm"][0], wall, offs[0][0], fs, "l0_ffn1")
    hn0 = _rmsnorm_fwd(x1, small["mix_norm"][0], "l0_mix_norm")
    z = _mm(hn0, mixw["ev_w_main_t"], "nt", "ev_in_proj")
    flog = _mm(hn0, mixw["ev_w_f_t"], "nt", "ev_in_proj_gate")
    a, a1 = _conv_a_fwd(z, small["ev_conv_w32"], small["ev_conv_b"], small["ev_conv_norm"], "ev_conv_fwd")
    qs, kh, vb, fb, ft, xt = _qk_fwd(z, flog, small["ev_b_f128"], small["ev_q_norm2"], small["ev_k_norm2"], "ev_qk_fwd")
    o, lse = _attn_fwd(qs, kh, vb, fb, ft, "ev_attn_fwd")
    ao = jnp.concatenate([a, o.astype(BF16)], axis=1)
    x2 = _mm(ao, mixw["ev_w_out"], "nn", "ev_out_proj", add=x1)
    x3, s_f2a = _ffn_block_fwd(x2, small["ffn2_norm"][0], wall, offs[0][1], fs, "l0_ffn2")

    x4, s_f1b = _ffn_block_fwd(x3, small["ffn1_norm"][1], wall, offs[1][0], fs, "l1_ffn1")
    hn1 = _rmsnorm_fwd(x4, small["mix_norm"][1], "l1_mix_norm")
    zo = _mm(hn1, mixw["od_w_in_t"], "nt", "od_in_proj")
    y0 = _conv_c_fwd(zo, small["od_conv_w8"], "od_conv_fwd")
    x5 = _mm(y0, mixw["od_w_out"], "nn", "od_out_proj", add=x4)
    x6, s_f2b = _ffn_block_fwd(x5, small["ffn2_norm"][1], wall, offs[1][1], fs, "l1_ffn2")

    loss, d6 = _loss_head(x6, target, "loss_head")

    d5, grads["l1_ffn2"], dn = _ffn_block_bwd(d6, s_f2b, wall, offs[1][1], fs, "l1_ffn2")
    grads["ffn2_norm_1"] = dn
    d5b = d5.astype(BF16)
    dy0 = _mm(d5b, mixw["od_w_out"], "nt", "od_out_proj_bwd")
    grads["od_w_out"] = _mm_tn(y0, d5b, "od_dw_out", BF16)
    dzo, grads["od_conv_w"] = _conv_c_bwd(dy0, zo, small["od_conv_w8"], "od_conv_bwd")
    dh1 = _mm(dzo, mixw["od_w_in_t"], "nn", "od_in_proj_bwd")
    grads["od_w_in_t"] = _mm_tn(dzo, hn1, "od_dw_in", BF16)
    d4, grads["mix_norm_1"] = _rmsnorm_bwd(x4, small["mix_norm"][1], dh1, d5, "l1_mix_norm_bwd")
    d3, grads["l1_ffn1"], grads["ffn1_norm_1"] = _ffn_block_bwd(d4, s_f1b, wall, offs[1][0], fs, "l1_ffn1")

    d2, grads["l0_ffn2"], grads["ffn2_norm_0"] = _ffn_block_bwd(d3, s_f2a, wall, offs[0][1], fs, "l0_ffn2")
    d2b = d2.astype(BF16)
    dao = _mm(d2b, mixw["ev_w_out"], "nt", "ev_out_proj_bwd")
    grads["ev_w_out"] = _mm_tn(ao, d2b, "ev_dw_out", BF16)
    da1, grads["ev_conv_norm"], grads["ev_conv_b"] = _conv_a_bwd_norm(dao, a1, small["ev_conv_norm"], "ev_conv_bwd_norm")
    du, dg, grads["ev_conv_w"] = _conv_a_bwd_conv(da1, z, small["ev_conv_w32"], "ev_conv_bwd_conv")
    dqs, dfq4 = _attn_bwd_dq(qs, kh, vb, fb, ft, lse, o, dao, "ev_attn_bwd_dq")
    dkh, dv, df4 = _attn_bwd_dkv(qs, kh, vb, fb, ft, lse, o, dao, "ev_attn_bwd_dkv")
    df4 = df4 + dfq4
    dq, dk, dvb, grads["ev_q_norm"], grads["ev_k_norm"] = _qk_bwd(
        z, dqs, dkh, dv, small["ev_q_norm2"], small["ev_k_norm2"], "ev_qk_bwd")
    dft = df4[:, 0:2, :].reshape(N_HEADS, -1)
    dxt, dflog, grads["ev_b_f"] = _gate_bwd(dft, xt, "ev_gate_bwd")
    dz = jnp.concatenate([du, dg, dq, dk, dvb], axis=1)
    dflog_b = dflog.astype(BF16)
    dh0 = _mm(dz, mixw["ev_w_main_t"], "nn", "ev_in_proj_bwd")
    dh0 = _mm(dflog_b, mixw["ev_w_f_t"], "nn", "ev_in_proj_gate_bwd", add=dh0)
    dw_main = _mm_tn(dz, hn0, "ev_dw_in", BF16)
    dw_f = _mm(dxt.astype(BF16), hn0, "nn", "ev_dw_in_gate", BF16)
    grads["ev_w_in_t"] = jnp.concatenate([dw_main, dw_f], axis=0)
    d1, grads["mix_norm_0"] = _rmsnorm_bwd(x1, small["mix_norm"][0], dh0, d2, "l0_mix_norm_bwd")
    d0, grads["l0_ffn1"], grads["ffn1_norm_0"] = _ffn_block_bwd(d1, s_f1a, wall, offs[0][0], fs, "l0_ffn1")
    return loss, d0, grads


def _round_up(n, m):
    return -(-n // m) * m


def _pad_rows(a, rows):
    return jnp.pad(a, ((0, rows - a.shape[0]), (0, 0)))


SMALL_ORDER = ("loss", "ffn1_norm", "mix_norm", "ffn2_norm", "ev_b_f", "ev_conv_b", "ev_conv_norm",
               "ev_q_norm", "ev_k_norm", "ev_conv_w", "od_conv_w")


def _pack_small(parts):
    flat = jnp.concatenate([parts[k].reshape(-1).astype(F32) for k in SMALL_ORDER])
    n = _round_up(flat.shape[0], 8 * V7X_LANES)
    return jnp.pad(flat, (0, n - flat.shape[0])).reshape(-1, V7X_LANES)


def _unpack_small(buf, shapes):
    flat = buf.reshape(-1)
    out, pos = {}, 0
    for k in SMALL_ORDER:
        n = math.prod(shapes[k])
        out[k] = flat[pos:pos + n].reshape(shapes[k])
        pos += n
    return out


def kernel(x, ffn1_norm, ffn1_w_gate, ffn1_w_up, ffn1_w_down, mix_norm, ffn2_norm, ffn2_w_gate, ffn2_w_up, ffn2_w_down, ev_w_in, ev_b_f, ev_conv_w, ev_conv_b, ev_conv_norm, ev_q_norm, ev_k_norm, ev_w_out, od_w_in, od_conv_w, od_w_out, loss_target, m_ffn1_norm, m_ffn1_w_gate, m_ffn1_w_up, m_ffn1_w_down, m_mix_norm, m_ffn2_norm, m_ffn2_w_gate, m_ffn2_w_up, m_ffn2_w_down, m_ev_w_in, m_ev_b_f, m_ev_conv_w, m_ev_conv_b, m_ev_conv_norm, m_ev_q_norm, m_ev_k_norm, m_ev_w_out, m_od_w_in, m_od_conv_w, m_od_w_out, v_ffn1_norm, v_ffn1_w_gate, v_ffn1_w_up, v_ffn1_w_down, v_mix_norm, v_ffn2_norm, v_ffn2_w_gate, v_ffn2_w_up, v_ffn2_w_down, v_ev_w_in, v_ev_b_f, v_ev_conv_w, v_ev_conv_b, v_ev_conv_norm, v_ev_q_norm, v_ev_k_norm, v_ev_w_out, v_od_w_in, v_od_conv_w, v_od_w_out):
    weights = dict(ffn1_norm=ffn1_norm, ffn1_w_gate=ffn1_w_gate, ffn1_w_up=ffn1_w_up, ffn1_w_down=ffn1_w_down,
                   mix_norm=mix_norm, ffn2_norm=ffn2_norm, ffn2_w_gate=ffn2_w_gate, ffn2_w_up=ffn2_w_up,
                   ffn2_w_down=ffn2_w_down, ev_w_in=ev_w_in, ev_b_f=ev_b_f, ev_conv_w=ev_conv_w, ev_conv_b=ev_conv_b,
                   ev_conv_norm=ev_conv_norm, ev_q_norm=ev_q_norm, ev_k_norm=ev_k_norm, ev_w_out=ev_w_out,
                   od_w_in=od_w_in, od_conv_w=od_conv_w, od_w_out=od_w_out)
    m_in = dict(ffn1_norm=m_ffn1_norm, ffn1_w_gate=m_ffn1_w_gate, ffn1_w_up=m_ffn1_w_up, ffn1_w_down=m_ffn1_w_down,
                mix_norm=m_mix_norm, ffn2_norm=m_ffn2_norm, ffn2_w_gate=m_ffn2_w_gate, ffn2_w_up=m_ffn2_w_up,
                ffn2_w_down=m_ffn2_w_down, ev_w_in=m_ev_w_in, ev_b_f=m_ev_b_f, ev_conv_w=m_ev_conv_w,
                ev_conv_b=m_ev_conv_b, ev_conv_norm=m_ev_conv_norm, ev_q_norm=m_ev_q_norm, ev_k_norm=m_ev_k_norm,
                ev_w_out=m_ev_w_out, od_w_in=m_od_w_in, od_conv_w=m_od_conv_w, od_w_out=m_od_w_out)
    v_in = dict(ffn1_norm=v_ffn1_norm, ffn1_w_gate=v_ffn1_w_gate, ffn1_w_up=v_ffn1_w_up, ffn1_w_down=v_ffn1_w_down,
                mix_norm=v_mix_norm, ffn2_norm=v_ffn2_norm, ffn2_w_gate=v_ffn2_w_gate, ffn2_w_up=v_ffn2_w_up,
                ffn2_w_down=v_ffn2_w_down, ev_w_in=v_ev_w_in, ev_b_f=v_ev_b_f, ev_conv_w=v_ev_conv_w,
                ev_conv_b=v_ev_conv_b, ev_conv_norm=v_ev_conv_norm, ev_q_norm=v_ev_q_norm, ev_k_norm=v_ev_k_norm,
                ev_w_out=v_ev_w_out, od_w_in=v_od_w_in, od_conv_w=v_od_conv_w, od_w_out=v_od_w_out)
    order = list(weights)

    d = x.shape[-1]
    fs = ffn1_w_gate.shape[2]
    n_in = ev_w_in.shape[2]
    n_in_pad = _round_up(n_in, BF16_ROWS)
    n_out = ev_w_out.shape[1]
    n_od = od_w_in.shape[2]
    d_conv = ev_conv_b.shape[1]
    d_in_even = n_in * N_DEV
    d_main = d_in_even - N_HEADS
    cx, cy, cc = _position()
    me = _dev_index(cx, cy, cc)

    pieces, offs, pos = [], [[None, None], [None, None]], 0
    for layer in range(2):
        for blk, (wg, wu, wd) in enumerate(((ffn1_w_gate, ffn1_w_up, ffn1_w_down), (ffn2_w_gate, ffn2_w_up, ffn2_w_down))):
            pieces += [wg[layer].T, wu[layer].T, wd[layer]]
            offs[layer][blk] = (pos, pos + fs, pos + 2 * fs)
            pos += 3 * fs
    off_ev_in, off_ev_out = pos, pos + n_in_pad
    off_od_in, off_od_out = off_ev_out + n_out, off_ev_out + n_out + n_od
    pieces += [_pad_rows(ev_w_in[0].T, n_in_pad), ev_w_out[0], od_w_in[0].T, od_w_out[0]]
    n_rows = off_od_out + n_out
    wloc = jnp.concatenate([p.astype(BF16) for p in pieces], axis=0)
    wall = _all_gather(wloc)

    ev_w_in_t = wall[:, off_ev_in:off_ev_in + n_in, :].reshape(d_in_even, d)
    mixw = dict(
        ev_w_main_t=ev_w_in_t[:d_main],
        ev_w_f_t=_pad_rows(ev_w_in_t[d_main:], V7X_LANES),
        ev_w_out=wall[:, off_ev_out:off_ev_out + n_out, :].reshape(N_DEV * n_out, d),
        od_w_in_t=wall[:, off_od_in:off_od_in + n_od, :].reshape(N_DEV * n_od, d),
        od_w_out=wall[:, off_od_out:off_od_out + n_out, :].reshape(N_DEV * n_out, d),
    )

    conv_shapes = dict(ev_conv_w=(CONV_A_WIDTH, d_conv), od_conv_w=(CONV_C_WIDTH, d))
    zero_small = {k: jnp.zeros(s_, F32) for k, s_ in conv_shapes.items()}
    ev_cw_part = lax.dynamic_update_slice(zero_small["ev_conv_w"], ev_conv_w[0], (0, me * ev_conv_w.shape[2]))
    od_cw_part = lax.dynamic_update_slice(zero_small["od_conv_w"], od_conv_w[0], (0, me * od_conv_w.shape[2]))
    zeros_like_small = {k: jnp.zeros((1,), F32) for k in SMALL_ORDER}
    taps = _unpack_small(_all_reduce_small(_pack_small({**zeros_like_small, "ev_conv_w": ev_cw_part,
                                                        "od_conv_w": od_cw_part})),
                         {**{k: (1,) for k in SMALL_ORDER}, **conv_shapes})
    small = dict(
        ffn1_norm=[ffn1_norm[l][None] for l in range(2)], mix_norm=[mix_norm[l][None] for l in range(2)],
        ffn2_norm=[ffn2_norm[l][None] for l in range(2)],
        ev_conv_w32=_pad_rows(taps["ev_conv_w"], CONV_A_WIDTH + 1), ev_conv_b=ev_conv_b, ev_conv_norm=ev_conv_norm,
        ev_b_f128=jnp.pad(ev_b_f, ((0, 0), (0, V7X_LANES - N_HEADS))),
        ev_q_norm2=jnp.tile(ev_q_norm, (1, 2)), ev_k_norm2=jnp.tile(ev_k_norm, (1, 2)),
        od_conv_w8=_pad_rows(taps["od_conv_w"], 8),
    )

    loss_p, grad_x, g = _local_step(x[0], loss_target[0], wall, offs, fs, mixw, small)

    def by_dev(a, rows, pad_to=None):
        a = a.reshape(N_DEV, rows, d)
        return a if pad_to is None else jnp.pad(a, ((0, 0), (0, pad_to - rows), (0, 0)))

    gpieces = []
    for layer in range(2):
        for blk in ("ffn1", "ffn2"):
            gpieces += [by_dev(t, fs) for t in g[f"l{layer}_{blk}"]]
    gpieces += [by_dev(g["ev_w_in_t"], n_in, n_in_pad), by_dev(g["ev_w_out"], n_out),
                by_dev(g["od_w_in_t"], n_od), by_dev(g["od_w_out"], n_out)]
    gall = jnp.concatenate(gpieces, axis=1)
    sib = _pair_exchange(gall)
    idx = jnp.stack([me] + [_dev_index(*chip, cc) for chip in _other_chips(cx, cy)]).astype(jnp.int32)
    tsum = _pair_sum(gall, sib, idx)
    gsum = _final_sum(tsum, _chip_exchange(tsum))

    def rows_of(off, n):
        return gsum[off:off + n]

    grad = {}
    for bi, blk in enumerate(("ffn1", "ffn2")):
        grad[f"{blk}_w_gate"] = jnp.stack([rows_of(offs[l][bi][0], fs).T for l in range(2)])
        grad[f"{blk}_w_up"] = jnp.stack([rows_of(offs[l][bi][1], fs).T for l in range(2)])
        grad[f"{blk}_w_down"] = jnp.stack([rows_of(offs[l][bi][2], fs) for l in range(2)])
    grad["ev_w_in"] = rows_of(off_ev_in, n_in).T[None]
    grad["ev_w_out"] = rows_of(off_ev_out, n_out)[None]
    grad["od_w_in"] = rows_of(off_od_in, n_od).T[None]
    grad["od_w_out"] = rows_of(off_od_out, n_out)[None]

    heads = lambda t: t.reshape(N_HEADS, HEAD_DIM).sum(axis=0)
    parts = dict(
        loss=loss_p[0, 0:1],
        ffn1_norm=jnp.stack([g["ffn1_norm_0"][0], g["ffn1_norm_1"][0]]),
        mix_norm=jnp.stack([g["mix_norm_0"][0], g["mix_norm_1"][0]]),
        ffn2_norm=jnp.stack([g["ffn2_norm_0"][0], g["ffn2_norm_1"][0]]),
        ev_b_f=g["ev_b_f"][:, 0], ev_conv_b=g["ev_conv_b"], ev_conv_norm=g["ev_conv_norm"],
        ev_q_norm=heads(g["ev_q_norm"]), ev_k_norm=heads(g["ev_k_norm"]),
        ev_conv_w=g["ev_conv_w"][:CONV_A_WIDTH], od_conv_w=g["od_conv_w"][:CONV_C_WIDTH])
    small_shapes = dict(loss=(1,), ffn1_norm=ffn1_norm.shape, mix_norm=mix_norm.shape, ffn2_norm=ffn2_norm.shape,
                        ev_b_f=ev_b_f.shape, ev_conv_b=ev_conv_b.shape, ev_conv_norm=ev_conv_norm.shape,
                        ev_q_norm=ev_q_norm.shape, ev_k_norm=ev_k_norm.shape, **conv_shapes)
    red = _unpack_small(_all_reduce_small(_pack_small(parts)), small_shapes)
    loss = red["loss"][0]
    for k in ("ffn1_norm", "mix_norm", "ffn2_norm", "ev_b_f", "ev_conv_b", "ev_conv_norm", "ev_q_norm", "ev_k_norm"):
        grad[k] = red[k]
    grad["ev_conv_w"] = lax.dynamic_slice(red["ev_conv_w"], (0, me * ev_conv_w.shape[2]),
                                          (CONV_A_WIDTH, ev_conv_w.shape[2]))[None]
    grad["od_conv_w"] = lax.dynamic_slice(red["od_conv_w"], (0, me * od_conv_w.shape[2]),
                                          (CONV_C_WIDTH, od_conv_w.shape[2]))[None]

    big = ("ffn1_w_gate", "ffn1_w_up", "ffn1_w_down", "ffn2_w_gate", "ffn2_w_up", "ffn2_w_down",
           "ev_w_in", "ev_w_out", "od_w_in", "od_w_out")
    delta, new_m, new_v = {}, {}, {}
    for k in big:
        shp = weights[k].shape
        flat = lambda t: t.reshape(-1, shp[-1])
        dk, mk, vk = _adamw(flat(weights[k]), flat(grad[k]), flat(m_in[k]), flat(v_in[k]), f"adamw_{k}")
        delta[k], new_m[k], new_v[k] = dk.reshape(shp), mk.reshape(shp), vk.reshape(shp)
    rest = [k for k in order if k not in big]
    cat = lambda src: jnp.concatenate([src[k].reshape(-1) for k in rest])
    n_small = sum(math.prod(weights[k].shape) for k in rest)
    n_pad = _round_up(n_small, 8 * V7X_LANES)
    as_rows = lambda t: jnp.pad(t, (0, n_pad - n_small)).reshape(-1, V7X_LANES)
    v_rows = jnp.pad(cat(v_in), (0, n_pad - n_small), constant_values=1.0).reshape(-1, V7X_LANES)
    ds, ms, vs = _adamw(as_rows(cat(weights)), as_rows(cat(grad)), as_rows(cat(m_in)), v_rows, "adamw_small")
    pos = 0
    for k in rest:
        n = math.prod(weights[k].shape)
        for dst, src in ((delta, ds), (new_m, ms), (new_v, vs)):
            dst[k] = src.reshape(-1)[pos:pos + n].reshape(weights[k].shape)
        pos += n

    return (loss, grad_x[None], *[grad[k] for k in order], *[delta[k] for k in order],
            *[new_m[k] for k in order], *[new_v[k] for k in order])
```

```python
import functools
import math

import jax
import jax.numpy as jnp
from jax import lax
from jax.experimental import pallas as pl
from jax.experimental.pallas import tpu as pltpu

F32 = jnp.float32
BF16 = jnp.bfloat16
SDS = jax.ShapeDtypeStruct
MESH = pl.DeviceIdType.MESH

N_DEV = 8
EPS = 1e-6
FFN_RES = 0.5
HEAD_DIM = 64
N_HEADS = 8
D_ATTN = N_HEADS * HEAD_DIM
N_PAIRS = N_HEADS // 2
PAIR = 2 * HEAD_DIM
ATTN_SCALE = 1.0 / math.sqrt(HEAD_DIM)
CONV_A_WIDTH = 31
CONV_A_HALO = 32
CONV_C_WIDTH = 3
CONV_C_HALO = 8
NEG_BIG = -1e30
ADAM_LR, ADAM_B1, ADAM_B2, ADAM_EPS, ADAM_WD, ADAM_STEP = 0.001, 0.9, 0.999, 1e-08, 0.01, 10

V7X_VMEM_BYTES = 64 * 1024 * 1024
V7X_LANES = 128
BF16_ROWS = 16
MIB = 1024 * 1024

NT = (((1,), (1,)), ((), ()))
TN = (((0,), (0,)), ((), ()))


def _call(body, *, name, out_shape, in_specs, out_specs, grid=(), scratch=(), dims=None, vmem_mb=32, **kw):
    params = dict(vmem_limit_bytes=min(vmem_mb * MIB, V7X_VMEM_BYTES - 4 * MIB))
    if dims is not None:
        params["dimension_semantics"] = dims
    return pl.pallas_call(
        body, name=name, grid=grid, in_specs=in_specs, out_specs=out_specs, out_shape=out_shape,
        scratch_shapes=list(scratch), compiler_params=pltpu.CompilerParams(**params), **kw)


def _tile(n, want=512):
    return want if n % want == 0 else n


def _rows(tm, d, col=0):
    return pl.BlockSpec((tm, d), lambda i: (i, col))


def _const(shape):
    return pl.BlockSpec(shape, lambda *_: (0,) * len(shape))


ANY = pl.BlockSpec(memory_space=pl.ANY)
VMEM = pl.BlockSpec(memory_space=pltpu.VMEM)


def _sigmoid(x):
    return 1.0 / (1.0 + jnp.exp(-x))


def _rmsnorm_fwd(x, gain, name):
    s, d = x.shape
    tm = _tile(s)

    def body(x_ref, g_ref, o_ref):
        xv = x_ref[...]
        r = lax.rsqrt(jnp.mean(xv * xv, axis=-1, keepdims=True) + EPS)
        o_ref[...] = (xv * r * g_ref[...]).astype(BF16)

    return _call(body, name=name, grid=(s // tm,), in_specs=[_rows(tm, d), _const((1, d))],
                 out_specs=_rows(tm, d), out_shape=SDS((s, d), BF16), dims=("parallel",))(x, gain)


def _rmsnorm_bwd(x, gain, dxn, dres, name):
    s, d = x.shape
    tm = _tile(s)

    def body(x_ref, g_ref, dxn_ref, dres_ref, dx_ref, dg_ref):
        xv = x_ref[...]
        r = lax.rsqrt(jnp.mean(xv * xv, axis=-1, keepdims=True) + EPS)
        xh = xv * r
        dv = dxn_ref[...]

        @pl.when(pl.program_id(0) == 0)
        def _():
            dg_ref[...] = jnp.zeros_like(dg_ref)

        dg_ref[...] += jnp.sum(dv * xh, axis=0, keepdims=True)
        dxh = dv * g_ref[...]
        dx_ref[...] = dres_ref[...] + r * (dxh - xh * jnp.mean(dxh * xh, axis=-1, keepdims=True))

    return _call(body, name=name, grid=(s // tm,),
                 in_specs=[_rows(tm, d), _const((1, d)), _rows(tm, d), _rows(tm, d)],
                 out_specs=[_rows(tm, d), _const((1, d))],
                 out_shape=[SDS((s, d), F32), SDS((1, d), F32)], dims=("arbitrary",))(x, gain, dxn, dres)


def _col_tile(n):
    for t in (1024, 768, 512, 256, 128):
        if n % t == 0:
            return t
    return n


def _mm(a, b, mode, name, out_dtype=F32, add=None):
    if mode == "tn":
        k, m = a.shape
        n = b.shape[1]
        bm = 256 if m % 256 == 0 else m

        def body_tn(a_ref, b_ref, o_ref):
            o_ref[...] = lax.dot_general(a_ref[...].astype(BF16), b_ref[...].astype(BF16), TN,
                                         preferred_element_type=F32).astype(out_dtype)

        return _call(body_tn, name=name, grid=(m // bm,),
                     in_specs=[pl.BlockSpec((k, bm), lambda i: (0, i)), _const((k, n))],
                     out_specs=pl.BlockSpec((bm, n), lambda i: (i, 0)),
                     out_shape=SDS((m, n), out_dtype), dims=("parallel",), vmem_mb=48)(a, b)
    m, k = a.shape
    n = b.shape[0] if mode == "nt" else b.shape[1]
    tm, tn = _tile(m), _col_tile(n)
    dn = NT if mode == "nt" else (((1,), (0,)), ((), ()))

    def body(a_ref, b_ref, *rest):
        o_ref = rest[-1]
        acc = lax.dot_general(a_ref[...].astype(BF16), b_ref[...].astype(BF16), dn, preferred_element_type=F32)
        if add is not None:
            acc = acc + rest[0][...]
        o_ref[...] = acc.astype(out_dtype)

    b_spec = (pl.BlockSpec((tn, k), lambda i, j: (j, 0)) if mode == "nt"
              else pl.BlockSpec((k, tn), lambda i, j: (0, j)))
    in_specs = [pl.BlockSpec((tm, k), lambda i, j: (i, 0)), b_spec]
    args = [a, b]
    if add is not None:
        in_specs.append(pl.BlockSpec((tm, tn), lambda i, j: (i, j)))
        args.append(add)
    return _call(body, name=name, grid=(m // tm, n // tn), in_specs=in_specs,
                 out_specs=pl.BlockSpec((tm, tn), lambda i, j: (i, j)),
                 out_shape=SDS((m, n), out_dtype), dims=("parallel", "parallel"), vmem_mb=48)(*args)


def _mm_tn(a, b, name, out_dtype=F32):
    return _mm(a, b, "tn", name, out_dtype)


FFN_TM = 256
FFN_CHUNK = 256


def _load_ffn_weights(w_hbm, offs, fs, dsts, sems):
    copies = []
    for wi, (off, dst) in enumerate(zip(offs, dsts)):
        for j in range(N_DEV):
            cp = pltpu.make_async_copy(w_hbm.at[j, pl.ds(off, fs), :], dst.at[pl.ds(j * fs, fs), :],
                                       sems.at[wi * N_DEV + j])
            cp.start()
            copies.append(cp)
    for cp in copies:
        cp.wait()


def _ffn_fwd(x, xn, wall, offs, fs, name):
    s, d = x.shape
    f = fs * N_DEV
    tm, ch = _tile(s, FFN_TM), FFN_CHUNK

    def body(x_ref, xn_ref, w_hbm, out_ref, g_ref, u_ref, h_ref, wg_s, wu_s, wd_s, sems):
        @pl.when(pl.program_id(0) == 0)
        def _():
            _load_ffn_weights(w_hbm, offs, fs, (wg_s, wu_s, wd_s), sems)

        xnv = xn_ref[...]
        acc = jnp.zeros((tm, d), F32)
        for c in range(f // ch):
            sl = slice(c * ch, (c + 1) * ch)
            gb = lax.dot_general(xnv, wg_s[sl, :], NT, preferred_element_type=F32).astype(BF16)
            ub = lax.dot_general(xnv, wu_s[sl, :], NT, preferred_element_type=F32).astype(BF16)
            g_ref[:, sl] = gb
            u_ref[:, sl] = ub
            g = gb.astype(F32)
            hb = (g * _sigmoid(g) * ub.astype(F32)).astype(BF16)
            h_ref[:, sl] = hb
            acc = acc + jnp.dot(hb, wd_s[sl, :], preferred_element_type=F32)
        out_ref[...] = x_ref[...] + FFN_RES * acc

    return _call(body, name=name, grid=(s // tm,),
                 in_specs=[_rows(tm, d), _rows(tm, d), ANY],
                 out_specs=[_rows(tm, d), _rows(tm, f), _rows(tm, f), _rows(tm, f)],
                 out_shape=[SDS((s, d), F32), SDS((s, f), BF16), SDS((s, f), BF16), SDS((s, f), BF16)],
                 scratch=[pltpu.VMEM((f, d), BF16)] * 3 + [pltpu.SemaphoreType.DMA((3 * N_DEV,))],
                 dims=("arbitrary",), vmem_mb=56)(x, xn, wall)


def _ffn_bwd_act(dout, g, u, wall, offs, fs, name):
    s, d = dout.shape
    f = fs * N_DEV
    tm, ch = _tile(s, FFN_TM), FFN_CHUNK

    def body(dout_ref, g_ref, u_ref, w_hbm, dg_ref, du_ref, dxn_ref, wg_s, wu_s, wd_s, sems):
        @pl.when(pl.program_id(0) == 0)
        def _():
            _load_ffn_weights(w_hbm, offs, fs, (wg_s, wu_s, wd_s), sems)

        dy = (FFN_RES * dout_ref[...]).astype(BF16)
        acc = jnp.zeros((tm, d), F32)
        for c in range(f // ch):
            sl = slice(c * ch, (c + 1) * ch)
            dh = lax.dot_general(dy, wd_s[sl, :], NT, preferred_element_type=F32)
            gv = g_ref[:, sl].astype(F32)
            uv = u_ref[:, sl].astype(F32)
            sg = _sigmoid(gv)
            dgb = (dh * uv * sg * (1.0 + gv * (1.0 - sg))).astype(BF16)
            dub = (dh * gv * sg).astype(BF16)
            dg_ref[:, sl] = dgb
            du_ref[:, sl] = dub
            acc = acc + jnp.dot(dgb, wg_s[sl, :], preferred_element_type=F32)
            acc = acc + jnp.dot(dub, wu_s[sl, :], preferred_element_type=F32)
        dxn_ref[...] = acc

    return _call(body, name=name, grid=(s // tm,),
                 in_specs=[_rows(tm, d), _rows(tm, f), _rows(tm, f), ANY],
                 out_specs=[_rows(tm, f), _rows(tm, f), _rows(tm, d)],
                 out_shape=[SDS((s, f), BF16), SDS((s, f), BF16), SDS((s, d), F32)],
                 scratch=[pltpu.VMEM((f, d), BF16)] * 3 + [pltpu.SemaphoreType.DMA((3 * N_DEV,))],
                 dims=("arbitrary",), vmem_mb=56)(dout, g, u, wall)


def _prev_rows(halo, tm, c, col):
    return pl.BlockSpec((halo, c), lambda i: (jnp.maximum(i * (tm // halo) - 1, 0), col))


def _next_rows(halo, tm, c, col, n_blocks):
    return pl.BlockSpec((halo, c), lambda i: (jnp.minimum((i + 1) * (tm // halo), n_blocks - 1), col))


def _conv_a_fwd(z, cw, cb, cn, name):
    s = z.shape[0]
    c = cb.shape[1]
    tm, halo, kw = _tile(s), CONV_A_HALO, CONV_A_WIDTH

    def body(u_ref, g_ref, up_ref, gp_ref, cw_ref, cb_ref, cn_ref, a_ref, a1_ref, buf):
        i = pl.program_id(0)
        buf[0:halo, :] = jnp.where(i > 0, up_ref[...] * _sigmoid(gp_ref[...]), 0.0)
        buf[halo:halo + tm, :] = u_ref[...] * _sigmoid(g_ref[...])
        acc = jnp.zeros((tm, c), F32)
        for k in range(kw):
            acc = acc + cw_ref[k:k + 1, :] * buf[pl.ds(halo - (kw - 1) + k, tm), :]
        a1 = acc + cb_ref[...]
        a1_ref[...] = a1
        a2 = a1 * lax.rsqrt(jnp.mean(a1 * a1, axis=-1, keepdims=True) + EPS) * cn_ref[...]
        a_ref[...] = (a2 * _sigmoid(a2)).astype(BF16)

    return _call(body, name=name, grid=(s // tm,),
                 in_specs=[_rows(tm, c, 0), _rows(tm, c, 1), _prev_rows(halo, tm, c, 0), _prev_rows(halo, tm, c, 1),
                           _const(cw.shape), _const((1, c)), _const((1, c))],
                 out_specs=[_rows(tm, c), _rows(tm, c)],
                 out_shape=[SDS((s, c), BF16), SDS((s, c), F32)],
                 scratch=[pltpu.VMEM((tm + halo, c), F32)], dims=("parallel",))(z, z, z, z, cw, cb, cn)


def _conv_a_bwd_norm(dao, a1, cn, name):
    s, c = a1.shape
    tm = _tile(s)

    def body(da_ref, a1_ref, cn_ref, da1_ref, dcn_ref, dcb_ref):
        a1v = a1_ref[...]
        r = lax.rsqrt(jnp.mean(a1v * a1v, axis=-1, keepdims=True) + EPS)
        xh = a1v * r
        a2 = xh * cn_ref[...]
        sg = _sigmoid(a2)
        da2 = da_ref[...] * sg * (1.0 + a2 * (1.0 - sg))
        dxh = da2 * cn_ref[...]
        da1 = r * (dxh - xh * jnp.mean(dxh * xh, axis=-1, keepdims=True))
        da1_ref[...] = da1

        @pl.when(pl.program_id(0) == 0)
        def _():
            dcn_ref[...] = jnp.zeros_like(dcn_ref)
            dcb_ref[...] = jnp.zeros_like(dcb_ref)

        dcn_ref[...] += jnp.sum(da2 * xh, axis=0, keepdims=True)
        dcb_ref[...] += jnp.sum(da1, axis=0, keepdims=True)

    return _call(body, name=name, grid=(s // tm,),
                 in_specs=[_rows(tm, c, 0), _rows(tm, c), _const((1, c))],
                 out_specs=[_rows(tm, c), _const((1, c)), _const((1, c))],
                 out_shape=[SDS((s, c), F32), SDS((1, c), F32), SDS((1, c), F32)], dims=("arbitrary",))(dao, a1, cn)


def _conv_a_bwd_conv(da1, z, cw, name):
    s, c = da1.shape
    tm, halo, kw = _tile(s), CONV_A_HALO, CONV_A_WIDTH
    n = s // tm

    def body(d_ref, dn_ref, u_ref, g_ref, up_ref, gp_ref, cw_ref, du_ref, dg_ref, dcw_ref, buf, bd):
        i = pl.program_id(0)
        uv = u_ref[...]
        sg = _sigmoid(g_ref[...])
        buf[0:halo, :] = jnp.where(i > 0, up_ref[...] * _sigmoid(gp_ref[...]), 0.0)
        buf[halo:halo + tm, :] = uv * sg
        dv = d_ref[...]
        bd[0:tm, :] = dv
        bd[tm:tm + halo, :] = jnp.where(i < n - 1, dn_ref[...], 0.0)

        @pl.when(i == 0)
        def _():
            dcw_ref[...] = jnp.zeros_like(dcw_ref)

        da0 = jnp.zeros((tm, c), F32)
        for k in range(kw):
            da0 = da0 + cw_ref[k:k + 1, :] * bd[pl.ds(kw - 1 - k, tm), :]
            dcw_ref[k:k + 1, :] += jnp.sum(dv * buf[pl.ds(halo - (kw - 1) + k, tm), :], axis=0, keepdims=True)
        du_ref[...] = (da0 * sg).astype(BF16)
        dg_ref[...] = (da0 * uv * sg * (1.0 - sg)).astype(BF16)

    return _call(body, name=name, grid=(n,),
                 in_specs=[_rows(tm, c), _next_rows(halo, tm, c, 0, s // halo), _rows(tm, c, 0), _rows(tm, c, 1),
                           _prev_rows(halo, tm, c, 0), _prev_rows(halo, tm, c, 1), _const(cw.shape)],
                 out_specs=[_rows(tm, c), _rows(tm, c), _const(cw.shape)],
                 out_shape=[SDS((s, c), BF16), SDS((s, c), BF16), SDS(cw.shape, F32)],
                 scratch=[pltpu.VMEM((tm + halo, c), F32)] * 2, dims=("arbitrary",))(da1, da1, z, z, z, z, cw)


def _lane_is_first_head(tm):
    return lax.broadcasted_iota(jnp.int32, (tm, PAIR), 1) < HEAD_DIM


def _pair_rms(xp, first):
    x2 = xp * xp
    s0 = jnp.sum(jnp.where(first, x2, 0.0), axis=-1, keepdims=True)
    s1 = jnp.sum(jnp.where(first, 0.0, x2), axis=-1, keepdims=True)
    return jnp.where(first, lax.rsqrt(s0 / HEAD_DIM + EPS), lax.rsqrt(s1 / HEAD_DIM + EPS))


def _split3(x):
    hi = x.astype(BF16)
    r1 = x - hi.astype(F32)
    mid = r1.astype(BF16)
    lo = (r1 - mid.astype(F32)).astype(BF16)
    return hi, mid, lo


def _qk_fwd(z, flog, bf, qn2, kn2, name):
    s = z.shape[0]
    tm = _tile(s)
    col0 = (z.shape[1] - 3 * D_ATTN) // D_ATTN

    def body(q_ref, k_ref, v_ref, fl_ref, bf_ref, qn_ref, kn_ref,
             qs_ref, kh_ref, vb_ref, fb_ref, ft_ref, xt_ref, carry):
        i = pl.program_id(0)
        first = _lane_is_first_head(tm)
        for p in range(N_PAIRS):
            sl = slice(p * PAIR, (p + 1) * PAIR)
            q = q_ref[:, sl]
            qs_ref[:, sl] = (q * _pair_rms(q, first) * qn_ref[...] * ATTN_SCALE).astype(BF16)
            k = k_ref[:, sl]
            kh_ref[:, sl] = (k * _pair_rms(k, first) * kn_ref[...]).astype(BF16)
        vb_ref[...] = v_ref[...].astype(BF16)

        xg = fl_ref[...] + bf_ref[...]
        valid = lax.broadcasted_iota(jnp.int32, (tm, V7X_LANES), 1) < N_HEADS
        ls = jnp.where(valid, jnp.minimum(xg, 0.0) - jnp.log(1.0 + jnp.exp(-jnp.abs(xg))), 0.0)
        tri = (lax.broadcasted_iota(jnp.int32, (tm, tm), 1) <= lax.broadcasted_iota(jnp.int32, (tm, tm), 0)).astype(BF16)
        cs = jnp.zeros((tm, V7X_LANES), F32)
        for part in _split3(ls):
            cs = cs + jnp.dot(tri, part, preferred_element_type=F32)

        @pl.when(i == 0)
        def _():
            carry[...] = jnp.zeros_like(carry)

        fv = cs + carry[0:1, :]
        carry[0:1, :] = fv[tm - 1:tm, :]
        ft_ref[...] = fv.T[0:N_HEADS, :]
        xt_ref[...] = xg.T[0:N_HEADS, :]
        for p in range(N_PAIRS):
            fb_ref[:, p * PAIR:(p + 1) * PAIR] = jnp.where(first, fv[:, 2 * p:2 * p + 1], fv[:, 2 * p + 1:2 * p + 2])

    wide = lambda col: pl.BlockSpec((tm, D_ATTN), lambda i: (i, col))
    tcol = pl.BlockSpec((N_HEADS, tm), lambda i: (0, i))
    return _call(body, name=name, grid=(s // tm,),
                 in_specs=[wide(col0), wide(col0 + 1), wide(col0 + 2), _rows(tm, V7X_LANES),
                           _const((1, V7X_LANES)), _const((1, PAIR)), _const((1, PAIR))],
                 out_specs=[wide(0), wide(0), wide(0), wide(0), tcol, tcol],
                 out_shape=[SDS((s, D_ATTN), BF16)] * 3 + [SDS((s, D_ATTN), F32), SDS((N_HEADS, s), F32),
                                                          SDS((N_HEADS, s), F32)],
                 scratch=[pltpu.VMEM((8, V7X_LANES), F32)], dims=("arbitrary",))(z, z, z, flog, bf, qn2, kn2)


ATTN_SUB = 128


def _causal_schedule(nq, key_major):
    if key_major:
        pairs = [(i, j) for j in range(nq) for i in range(j, nq)]
    else:
        pairs = [(i, j) for i in range(nq) for j in range(i + 1)]
    return (jnp.asarray([p[0] for p in pairs], jnp.int32), jnp.asarray([p[1] for p in pairs], jnp.int32))


def _sub_scores(qp, kp, ft_row, mine, r, masked, sub, tk):
    qm = jnp.where(mine, qp, jnp.zeros_like(qp))
    s2 = lax.dot_general(qm, kp, NT, preferred_element_type=F32) - ft_row
    if masked:
        row = r * sub + lax.broadcasted_iota(jnp.int32, (sub, tk), 0)
        s2 = jnp.where(lax.broadcasted_iota(jnp.int32, (sub, tk), 1) <= row, s2, NEG_BIG)
    return s2


def _attn_fwd(qs, kh, vb, fb, ft, name, gather_src=None):
    s = qs.shape[0]
    tq = tk = _tile(s)
    nq = s // tq
    sub = min(ATTN_SUB, tq)
    ii, jj = _causal_schedule(nq, key_major=False)
    n_steps = ii.shape[0]
    gathers = gather_src is not None

    def body(ii_ref, jj_ref, q_ref, k_ref, v_ref, fq_ref, ft_ref, *rest):
        if gathers:
            x_hbm, o_ref, lse_ref, wall_ref, m_s, l_s, acc_s, send_sems, recv_sems, local_sem = rest
        else:
            o_ref, lse_ref, m_s, l_s, acc_s = rest
        p, t = pl.program_id(0), pl.program_id(1)
        i, j = ii_ref[t], jj_ref[t]
        first = _lane_is_first_head(sub)
        if gathers:
            start, forward, finish = _gather_phases(x_hbm, wall_ref, send_sems, recv_sems, local_sem)
            pl.when(jnp.logical_and(p == 0, t == 0))(start)
            pl.when(jnp.logical_and(p == N_PAIRS // 2, t == 0))(forward)

        @pl.when(j == 0)
        def _():
            m_s[...] = jnp.full_like(m_s, NEG_BIG)
            l_s[...] = jnp.zeros_like(l_s)
            acc_s[...] = jnp.zeros_like(acc_s)

        def tile(masked):
            kp, vp = k_ref[...], v_ref[...]
            q_all, fq_all, acc_all = q_ref[...], fq_ref[...], acc_s[...]
            m_all, l_all = (m_s[0], m_s[1]), (l_s[0], l_s[1])
            ft_rows = [ft_ref[pl.ds(2 * p + h, 1), :] for h in range(2)]
            m_out, l_out, acc_out = ([], []), ([], []), []
            for r in range(tq // sub):
                rows = slice(r * sub, (r + 1) * sub)
                qp, fq, acc = q_all[rows, :], fq_all[rows, :], acc_all[rows, :]
                new = []
                for h in range(2):
                    mine = first if h == 0 else jnp.logical_not(first)
                    s2 = _sub_scores(qp, kp, ft_rows[h], mine, r, masked, sub, tk)
                    fqh = fq[:, h * HEAD_DIM:h * HEAD_DIM + 1]
                    m_old = m_all[h][rows, :]
                    m_new = jnp.maximum(m_old, jnp.max(s2, axis=-1, keepdims=True) + fqh)
                    pr = jnp.exp(s2 - (m_new - fqh))
                    alpha = jnp.exp(m_old - m_new)
                    l_out[h].append(alpha * l_all[h][rows, :] + jnp.sum(pr, axis=-1, keepdims=True))
                    m_out[h].append(m_new)
                    new.append(alpha * acc + jnp.dot(pr.astype(BF16), vp, preferred_element_type=F32))
                acc_out.append(jnp.where(first, new[0], new[1]))
            for h in range(2):
                m_s[h] = jnp.concatenate(m_out[h], axis=0)
                l_s[h] = jnp.concatenate(l_out[h], axis=0)
            acc_s[...] = jnp.concatenate(acc_out, axis=0)

        @pl.when(j < i)
        def _():
            tile(False)

        @pl.when(j == i)
        def _():
            tile(True)
            whole = _lane_is_first_head(tq)
            l_pair = jnp.where(whole, l_s[0], l_s[1])
            o_ref[...] = acc_s[...] / l_pair
            lse_ref[...] = jnp.where(whole, m_s[0], m_s[1]) + jnp.log(l_pair)

        if gathers:
            pl.when(jnp.logical_and(p == N_PAIRS - 1, t == n_steps - 1))(finish)

    qblk = pl.BlockSpec((tq, PAIR), lambda p, t, ii_r, jj_r: (ii_r[t], p))
    kblk = pl.BlockSpec((tk, PAIR), lambda p, t, ii_r, jj_r: (jj_r[t], p))
    in_specs = [qblk, kblk, kblk, qblk, pl.BlockSpec((N_HEADS, tk), lambda p, t, ii_r, jj_r: (0, jj_r[t]))]
    out_specs, out_shape = [qblk, qblk], [SDS((s, D_ATTN), F32)] * 2
    scratch = [pltpu.VMEM((2, tq, 1), F32), pltpu.VMEM((2, tq, 1), F32), pltpu.VMEM((tq, PAIR), F32)]
    args = [ii, jj, qs, kh, vb, fb, ft]
    if gathers:
        in_specs.append(ANY)
        out_specs.append(ANY)
        out_shape.append(SDS((N_DEV,) + gather_src.shape, gather_src.dtype))
        scratch += GATHER_SEMS
        args.append(gather_src)
    grid_spec = pltpu.PrefetchScalarGridSpec(num_scalar_prefetch=2, grid=(N_PAIRS, n_steps), in_specs=in_specs,
                                             out_specs=out_specs, scratch_shapes=scratch)
    return pl.pallas_call(
        body, name=name, grid_spec=grid_spec, out_shape=out_shape,
        compiler_params=pltpu.CompilerParams(dimension_semantics=("arbitrary", "arbitrary"),
                                             vmem_limit_bytes=32 * MIB))(*args)


def _attn_bwd(qs, kh, vb, fb, ft, lse, o, dao, name, exchange_src=None):
    s = qs.shape[0]
    tq = tk = _tile(s)
    nq = s // tq
    sub = min(ATTN_SUB, tq)
    ii, jj = _causal_schedule(nq, key_major=True)
    n_steps = ii.shape[0]

    exchanges = exchange_src is not None

    def body(ii_ref, jj_ref, q_ref, k_ref, v_ref, fq_ref, ft_ref, lse_ref, o_ref, do_ref, *rest):
        if exchanges:
            t_hbm, dq_ref, rs_ref, dk_ref, dv_ref, df_ref, rcv_ref, dk_s, dv_s, df_s, send_sems, recv_sems = rest
        else:
            dq_ref, rs_ref, dk_ref, dv_ref, df_ref, dk_s, dv_s, df_s = rest
        p, t = pl.program_id(0), pl.program_id(1)
        i, j = ii_ref[t], jj_ref[t]
        first = _lane_is_first_head(sub)
        first_k = _lane_is_first_head(tk)
        if exchanges:
            start, finish = _chip_exchange_phases(t_hbm, rcv_ref, send_sems, recv_sems)
            pl.when(jnp.logical_and(p == 0, t == 0))(start)

        @pl.when(t == 0)
        def _():
            dq_ref[...] = jnp.zeros_like(dq_ref)
            rs_ref[...] = jnp.zeros_like(rs_ref)

        @pl.when(i == j)
        def _():
            dk_s[...] = jnp.zeros_like(dk_s)
            dv_s[...] = jnp.zeros_like(dv_s)
            df_s[...] = jnp.zeros_like(df_s)

        def tile(masked):
            kp, vp = k_ref[...], v_ref[...]
            q_all, fq_all, lse_all, o_all, do_all = q_ref[...], fq_ref[...], lse_ref[...], o_ref[...], do_ref[...]
            ft_rows = [ft_ref[pl.ds(2 * p + h, 1), :] for h in range(2)]
            dq_out, rs_out = [], []
            dk_acc, dv_acc = jnp.zeros((tk, PAIR), F32), jnp.zeros((tk, PAIR), F32)
            df_acc = [jnp.zeros((1, tk), F32), jnp.zeros((1, tk), F32)]
            for r in range(tq // sub):
                rows = slice(r * sub, (r + 1) * sub)
                qp, fq, lse, ov, dall = q_all[rows, :], fq_all[rows, :], lse_all[rows, :], o_all[rows, :], do_all[rows, :]
                dq_h, dk_h, dv_h, rs_h = [], [], [], []
                for h in range(2):
                    mine = first if h == 0 else jnp.logical_not(first)
                    s2 = _sub_scores(qp, kp, ft_rows[h], mine, r, masked, sub, tk)
                    lane = slice(h * HEAD_DIM, h * HEAD_DIM + 1)
                    pr = jnp.exp(s2 - (lse[:, lane] - fq[:, lane]))
                    dov = jnp.where(mine, dall, 0.0)
                    dsum = jnp.sum(dov * ov, axis=-1, keepdims=True)
                    dom = dov.astype(BF16)
                    dom_lo = (dov - dom.astype(F32)).astype(BF16)
                    dp = lax.dot_general(dom, vp, NT, preferred_element_type=F32)
                    dp = dp + lax.dot_general(dom_lo, vp, NT, preferred_element_type=F32)
                    ds = pr * (dp - dsum)
                    dsb = ds.astype(BF16)
                    dq_h.append(jnp.dot(dsb, kp, preferred_element_type=F32))
                    dk_h.append(lax.dot_general(dsb, qp, TN, preferred_element_type=F32))
                    dv_h.append(lax.dot_general(pr.astype(BF16), dom, TN, preferred_element_type=F32))
                    rs_h.append(jnp.sum(ds, axis=-1, keepdims=True))
                    df_acc[h] = df_acc[h] - jnp.sum(ds, axis=0, keepdims=True)
                dq_out.append(jnp.where(first, dq_h[0], dq_h[1]))
                rs_out.append(jnp.where(first, rs_h[0], rs_h[1]))
                dk_acc = dk_acc + jnp.where(first_k, dk_h[0], dk_h[1])
                dv_acc = dv_acc + jnp.where(first_k, dv_h[0], dv_h[1])
            grows = pl.ds(pl.multiple_of(i * tq, tq), tq)
            dq_ref[grows, :] += jnp.concatenate(dq_out, axis=0)
            rs_ref[grows, :] += jnp.concatenate(rs_out, axis=0)
            dk_s[...] += dk_acc
            dv_s[...] += dv_acc
            for h in range(2):
                df_s[h:h + 1, :] += df_acc[h]

        @pl.when(j < i)
        def _():
            tile(False)

        @pl.when(j == i)
        def _():
            tile(True)

        @pl.when(i == nq - 1)
        def _():
            dk_ref[...] = dk_s[...]
            dv_ref[...] = dv_s[...]
            df_ref[0] = df_s[...]

        if exchanges:
            pl.when(jnp.logical_and(p == N_PAIRS - 1, t == n_steps - 1))(finish)

    qblk = pl.BlockSpec((tq, PAIR), lambda p, t, ii_r, jj_r: (ii_r[t], p))
    kblk = pl.BlockSpec((tk, PAIR), lambda p, t, ii_r, jj_r: (jj_r[t], p))
    doblk = pl.BlockSpec((tq, PAIR), lambda p, t, ii_r, jj_r: (ii_r[t], N_PAIRS + p))
    whole = pl.BlockSpec((s, PAIR), lambda p, t, ii_r, jj_r: (0, p))
    in_specs = [qblk, kblk, kblk, qblk, pl.BlockSpec((N_HEADS, tk), lambda p, t, ii_r, jj_r: (0, jj_r[t])),
                qblk, qblk, doblk]
    out_specs = [whole, whole, kblk, kblk, pl.BlockSpec((1, 8, tk), lambda p, t, ii_r, jj_r: (p, 0, jj_r[t]))]
    out_shape = [SDS((s, D_ATTN), F32)] * 4 + [SDS((N_PAIRS, 8, s), F32)]
    scratch = [pltpu.VMEM((tk, PAIR), F32), pltpu.VMEM((tk, PAIR), F32), pltpu.VMEM((8, tk), F32)]
    args = [ii, jj, qs, kh, vb, fb, ft, lse, o, dao]
    if exchanges:
        in_specs.append(ANY)
        out_specs.append(ANY)
        out_shape.append(SDS((3,) + exchange_src.shape[1:], exchange_src.dtype))
        scratch += EXCHANGE_SEMS
        args.append(exchange_src)
    grid_spec = pltpu.PrefetchScalarGridSpec(num_scalar_prefetch=2, grid=(N_PAIRS, n_steps), in_specs=in_specs,
                                             out_specs=out_specs, scratch_shapes=scratch)
    return pl.pallas_call(
        body, name=name, grid_spec=grid_spec, out_shape=out_shape,
        compiler_params=pltpu.CompilerParams(dimension_semantics=("arbitrary", "arbitrary"),
                                             vmem_limit_bytes=40 * MIB))(*args)


def _qk_bwd(z, dqs, dkh, dv, qn2, kn2, name):
    s = z.shape[0]
    tm = _tile(s)
    col0 = (z.shape[1] - 3 * D_ATTN) // D_ATTN

    def body(q_ref, k_ref, dqs_ref, dkh_ref, dv_ref, qn_ref, kn_ref, dq_ref, dk_ref, dvb_ref, dqn_ref, dkn_ref):
        first = _lane_is_first_head(tm)

        @pl.when(pl.program_id(0) == 0)
        def _():
            dqn_ref[...] = jnp.zeros_like(dqn_ref)
            dkn_ref[...] = jnp.zeros_like(dkn_ref)

        def through(x_ref, dy_ref, gain_ref, dx_ref, dgain_ref, scale):
            for p in range(N_PAIRS):
                sl = slice(p * PAIR, (p + 1) * PAIR)
                xv = x_ref[:, sl]
                r = _pair_rms(xv, first)
                xh = xv * r
                dy = dy_ref[:, sl] * scale
                dgain_ref[:, sl] += jnp.sum(dy * xh, axis=0, keepdims=True)
                dxh = dy * gain_ref[...]
                t = dxh * xh
                m0 = jnp.sum(jnp.where(first, t, 0.0), axis=-1, keepdims=True)
                m1 = jnp.sum(jnp.where(first, 0.0, t), axis=-1, keepdims=True)
                mean = jnp.where(first, m0, m1) / HEAD_DIM
                dx_ref[:, sl] = (r * (dxh - xh * mean)).astype(BF16)

        through(q_ref, dqs_ref, qn_ref, dq_ref, dqn_ref, ATTN_SCALE)
        through(k_ref, dkh_ref, kn_ref, dk_ref, dkn_ref, 1.0)
        dvb_ref[...] = dv_ref[...].astype(BF16)

    wide = lambda col: pl.BlockSpec((tm, D_ATTN), lambda i: (i, col))
    return _call(body, name=name, grid=(s // tm,),
                 in_specs=[wide(col0), wide(col0 + 1), wide(0), wide(0), wide(0), _const((1, PAIR)), _const((1, PAIR))],
                 out_specs=[wide(0), wide(0), wide(0), _const((1, D_ATTN)), _const((1, D_ATTN))],
                 out_shape=[SDS((s, D_ATTN), BF16)] * 3 + [SDS((1, D_ATTN), F32)] * 2,
                 dims=("arbitrary",))(z, z, dqs, dkh, dv, qn2, kn2)


def _gate_bwd(dft, xt, name):
    s = xt.shape[1]
    tm = _tile(s)
    n = s // tm

    def body(df_ref, xt_ref, dxt_ref, dx_ref, db_ref, carry):
        i = pl.program_id(0)

        @pl.when(i == 0)
        def _():
            carry[...] = jnp.zeros_like(carry)
            db_ref[...] = jnp.zeros_like(db_ref)

        tri = (lax.broadcasted_iota(jnp.int32, (tm, tm), 0) >= lax.broadcasted_iota(jnp.int32, (tm, tm), 1)).astype(BF16)
        rc = jnp.zeros((N_HEADS, tm), F32)
        for part in _split3(df_ref[...]):
            rc = rc + jnp.dot(part, tri, preferred_element_type=F32)
        dls = rc + carry[:, 0:1]
        carry[...] = jnp.broadcast_to(dls[:, 0:1], carry.shape)
        dxt = dls * _sigmoid(-xt_ref[...])
        dxt_ref[...] = dxt
        db_ref[...] += jnp.broadcast_to(jnp.sum(dxt, axis=-1, keepdims=True), db_ref.shape)
        padded = jnp.concatenate([dxt, jnp.zeros((V7X_LANES - N_HEADS, tm), F32)], axis=0)
        dx_ref[...] = padded.T

    rev = pl.BlockSpec((N_HEADS, tm), lambda i: (0, n - 1 - i))
    return _call(body, name=name, grid=(n,), in_specs=[rev, rev],
                 out_specs=[rev, pl.BlockSpec((tm, V7X_LANES), lambda i: (n - 1 - i, 0)), _const((N_HEADS, V7X_LANES))],
                 out_shape=[SDS((N_HEADS, s), F32), SDS((s, V7X_LANES), F32), SDS((N_HEADS, V7X_LANES), F32)],
                 scratch=[pltpu.VMEM((N_HEADS, V7X_LANES), F32)], dims=("arbitrary",))(dft, xt)


def _conv_c_fwd(z, cw, name):
    s = z.shape[0]
    c = z.shape[1] // 3
    tm, halo, kw = _tile(s), CONV_C_HALO, CONV_C_WIDTH

    def body(gb_ref, gc_ref, hh_ref, gcp_ref, hhp_ref, cw_ref, y_ref, buf):
        i = pl.program_id(0)
        buf[0:halo, :] = jnp.where(i > 0, gcp_ref[...] * hhp_ref[...], 0.0)
        buf[halo:halo + tm, :] = gc_ref[...] * hh_ref[...]
        c1 = jnp.zeros((tm, c), F32)
        for k in range(kw):
            c1 = c1 + cw_ref[k:k + 1, :] * buf[pl.ds(halo - (kw - 1) + k, tm), :]
        y_ref[...] = (gb_ref[...] * c1).astype(BF16)

    return _call(body, name=name, grid=(s // tm,),
                 in_specs=[_rows(tm, c, 0), _rows(tm, c, 1), _rows(tm, c, 2), _prev_rows(halo, tm, c, 1),
                           _prev_rows(halo, tm, c, 2), _const(cw.shape)],
                 out_specs=_rows(tm, c), out_shape=SDS((s, c), BF16),
                 scratch=[pltpu.VMEM((tm + halo, c), F32)], dims=("parallel",))(z, z, z, z, z, cw)


def _conv_c_bwd(dy0, z, cw, name):
    s = z.shape[0]
    c = z.shape[1] // 3
    tm, halo, kw = _tile(s), CONV_C_HALO, CONV_C_WIDTH
    n = s // tm

    def body(dy_ref, dyn_ref, gb_ref, gbn_ref, gc_ref, hh_ref, gcp_ref, hhp_ref, cw_ref, dz_ref, dcw_ref, buf, bd):
        i = pl.program_id(0)
        gcv, hhv, dyv = gc_ref[...], hh_ref[...], dy_ref[...]
        buf[0:halo, :] = jnp.where(i > 0, gcp_ref[...] * hhp_ref[...], 0.0)
        buf[halo:halo + tm, :] = gcv * hhv
        dc1 = dyv * gb_ref[...]
        bd[0:tm, :] = dc1
        bd[tm:tm + halo, :] = jnp.where(i < n - 1, dyn_ref[...] * gbn_ref[...], 0.0)

        @pl.when(i == 0)
        def _():
            dcw_ref[...] = jnp.zeros_like(dcw_ref)

        c1 = jnp.zeros((tm, c), F32)
        dc0 = jnp.zeros((tm, c), F32)
        for k in range(kw):
            shifted = buf[pl.ds(halo - (kw - 1) + k, tm), :]
            c1 = c1 + cw_ref[k:k + 1, :] * shifted
            dc0 = dc0 + cw_ref[k:k + 1, :] * bd[pl.ds(kw - 1 - k, tm), :]
            dcw_ref[k:k + 1, :] += jnp.sum(dc1 * shifted, axis=0, keepdims=True)
        dz_ref[:, 0:c] = (dyv * c1).astype(BF16)
        dz_ref[:, c:2 * c] = (dc0 * hhv).astype(BF16)
        dz_ref[:, 2 * c:3 * c] = (dc0 * gcv).astype(BF16)

    return _call(body, name=name, grid=(n,),
                 in_specs=[_rows(tm, c), _next_rows(halo, tm, c, 0, s // halo), _rows(tm, c, 0),
                           _next_rows(halo, tm, c, 0, s // halo), _rows(tm, c, 1), _rows(tm, c, 2),
                           _prev_rows(halo, tm, c, 1), _prev_rows(halo, tm, c, 2), _const(cw.shape)],
                 out_specs=[_rows(tm, 3 * c), _const(cw.shape)],
                 out_shape=[SDS((s, 3 * c), BF16), SDS(cw.shape, F32)],
                 scratch=[pltpu.VMEM((tm + halo, c), F32)] * 2, dims=("arbitrary",),
                 vmem_mb=48)(dy0, dy0, z, z, z, z, z, z, cw)


def _loss_head(y, target, name):
    s, d = y.shape
    tm = _tile(s)

    def body(y_ref, t_ref, loss_ref, dy_ref):
        e = y_ref[...] - t_ref[...]

        @pl.when(pl.program_id(0) == 0)
        def _():
            loss_ref[...] = jnp.zeros_like(loss_ref)

        loss_ref[...] += 0.5 * jnp.sum(jnp.mean(e * e, axis=-1, keepdims=True))
        dy_ref[...] = e / d

    return _call(body, name=name, grid=(s // tm,), in_specs=[_rows(tm, d), _rows(tm, d)],
                 out_specs=[_const((8, V7X_LANES)), _rows(tm, d)],
                 out_shape=[SDS((8, V7X_LANES), F32), SDS((s, d), F32)], dims=("arbitrary",))(y, target)


def _adamw(w, g, m, v, name):
    r, c = w.shape
    tr = next((t for t in (512, 256, 128, 64, 32, 16, 8) if r % t == 0), r)

    def body(w_ref, g_ref, m_ref, v_ref, d_ref, mo_ref, vo_ref):
        gv = g_ref[...]
        mn = ADAM_B1 * m_ref[...] + (1.0 - ADAM_B1) * gv
        vn = ADAM_B2 * v_ref[...] + (1.0 - ADAM_B2) * (gv * gv)
        m_hat = mn / (1.0 - ADAM_B1 ** ADAM_STEP)
        v_hat = vn / (1.0 - ADAM_B2 ** ADAM_STEP)
        d_ref[...] = -ADAM_LR * (m_hat / (jnp.sqrt(v_hat) + ADAM_EPS) + ADAM_WD * w_ref[...])
        mo_ref[...] = mn
        vo_ref[...] = vn

    spec = _rows(tr, c)
    return _call(body, name=name, grid=(r // tr,), in_specs=[spec] * 4, out_specs=[spec] * 3,
                 out_shape=[SDS((r, c), F32)] * 3, dims=("parallel",))(w, g, m, v)


def _position():
    return lax.axis_index("x"), lax.axis_index("y"), lax.axis_index("c")


def _other_chips(x, y):
    return [(1 - x, y), (x, 1 - y), (1 - x, 1 - y)]


def _dev_index(px, py, pc):
    return 4 * px + 2 * py + pc


def _all_gather(wloc):
    r, d = wloc.shape

    def body(x_ref, out_ref, send_sems, recv_sems, local_sem):
        start, forward, finish = _gather_phases(x_ref, out_ref, send_sems, recv_sems, local_sem)
        start()
        forward()
        finish()

    return _call(body, name="all_gather_weights", in_specs=[ANY], out_specs=ANY,
                 out_shape=SDS((N_DEV, r, d), wloc.dtype), scratch=GATHER_SEMS)(wloc)


GATHER_SEMS = [pltpu.SemaphoreType.DMA((7,)), pltpu.SemaphoreType.DMA((7,)), pltpu.SemaphoreType.DMA((1,))]


def _gather_phases(x_ref, out_ref, send_sems, recv_sems, local_sem):
    x, y, c = _position()
    me, sibling = (x, y, c), (x, y, 1 - c)
    chips = _other_chips(x, y)

    def slot(dev):
        return out_ref.at[_dev_index(*dev)]

    def copy(k, block, to, src=None):
        return pltpu.make_async_remote_copy(
            src_ref=slot(block) if src is None else src, dst_ref=slot(block),
            send_sem=send_sems.at[k], recv_sem=recv_sems.at[k], device_id=to, device_id_type=MESH)

    mine = pltpu.make_async_copy(x_ref, slot(me), local_sem.at[0])
    first = [copy(0, me, sibling, src=x_ref)] + [copy(1 + j, me, (*chip, c), src=x_ref) for j, chip in enumerate(chips)]
    passed = [copy(4 + j, (*chip, c), sibling) for j, chip in enumerate(chips)]

    def start():
        mine.start()
        for cp in first:
            cp.start()

    def forward():
        for j, chip in enumerate(chips):
            copy(1 + j, (*chip, c), me).wait_recv()
            passed[j].start()

    def finish():
        copy(0, sibling, me).wait_recv()
        for j, chip in enumerate(chips):
            copy(4 + j, (*chip, 1 - c), me).wait_recv()
        for cp in first + passed:
            cp.wait_send()
        mine.wait()

    return start, forward, finish


def _row_block(r):
    return next(t for t in range(704, 0, -BF16_ROWS) if r % t == 0)


def _pair_exchange(gall, name):
    _, r, d = gall.shape

    def body(g_ref, out_ref, send_sems, recv_sems):
        x, y, c = _position()
        sibling = (x, y, 1 - c)
        dests = [sibling] + [(*chip, 1 - c) for chip in _other_chips(x, y)]
        copies = [pltpu.make_async_remote_copy(
            src_ref=g_ref.at[_dev_index(*dest)], dst_ref=out_ref.at[k], send_sem=send_sems.at[k],
            recv_sem=recv_sems.at[k], device_id=sibling, device_id_type=MESH) for k, dest in enumerate(dests)]
        for cp in copies:
            cp.start()
        for cp in copies:
            cp.wait()

    return _call(body, name=name, in_specs=[ANY], out_specs=ANY,
                 out_shape=SDS((4, r, d), gall.dtype),
                 scratch=[pltpu.SemaphoreType.DMA((4,)), pltpu.SemaphoreType.DMA((4,))])(gall)


def _pair_sum(gall, sib, idx, name):
    _, r, d = gall.shape
    tr = _row_block(r)

    def body(idx_ref, a_ref, b_ref, o_ref):
        o_ref[...] = (a_ref[...].astype(F32) + b_ref[...].astype(F32)).astype(o_ref.dtype)

    grid_spec = pltpu.PrefetchScalarGridSpec(
        num_scalar_prefetch=1, grid=(4, r // tr),
        in_specs=[pl.BlockSpec((1, tr, d), lambda k, i, idx_ref: (idx_ref[k], i, 0)),
                  pl.BlockSpec((1, tr, d), lambda k, i, idx_ref: (k, i, 0))],
        out_specs=pl.BlockSpec((1, tr, d), lambda k, i, idx_ref: (k, i, 0)))
    return pl.pallas_call(body, name=name, grid_spec=grid_spec,
                          out_shape=SDS((4, r, d), gall.dtype),
                          compiler_params=pltpu.CompilerParams(dimension_semantics=("parallel", "parallel")))(idx, gall, sib)


def _chip_exchange(tsum):
    _, r, d = tsum.shape

    def body(t_ref, out_ref, send_sems, recv_sems):
        start, finish = _chip_exchange_phases(t_ref, out_ref, send_sems, recv_sems)
        start()
        finish()

    return _call(body, name="reduce_scatter_chip_exchange", in_specs=[ANY], out_specs=ANY,
                 out_shape=SDS((3, r, d), tsum.dtype), scratch=EXCHANGE_SEMS)(tsum)


EXCHANGE_SEMS = [pltpu.SemaphoreType.DMA((3,)), pltpu.SemaphoreType.DMA((3,))]


def _chip_exchange_phases(t_ref, out_ref, send_sems, recv_sems):
    x, y, c = _position()
    copies = [pltpu.make_async_remote_copy(
        src_ref=t_ref.at[1 + k], dst_ref=out_ref.at[k], send_sem=send_sems.at[k], recv_sem=recv_sems.at[k],
        device_id=(*chip, c), device_id_type=MESH) for k, chip in enumerate(_other_chips(x, y))]

    def start():
        for cp in copies:
            cp.start()

    def finish():
        for cp in copies:
            cp.wait()

    return start, finish


def _final_sum(tsum, rcv, name):
    _, r, d = tsum.shape
    tr = _row_block(r)

    def body(t_ref, r_ref, o_ref):
        acc = t_ref[0].astype(F32)
        for k in range(3):
            acc = acc + r_ref[k].astype(F32)
        o_ref[...] = acc

    return _call(body, name=name, grid=(r // tr,),
                 in_specs=[pl.BlockSpec((1, tr, d), lambda i: (0, i, 0)), pl.BlockSpec((3, tr, d), lambda i: (0, i, 0))],
                 out_specs=_rows(tr, d), out_shape=SDS((r, d), F32), dims=("parallel",))(tsum, rcv)


def _all_reduce_small(buf):
    nr, lanes = buf.shape

    def body(b_ref, out_ref, gath, send_sems, recv_sems):
        x, y, c = _position()
        my_slot = _dev_index(x, y, c)
        gath[my_slot] = b_ref[...]
        copies = []
        for k in range(1, N_DEV):
            dx, dy, dc = (k >> 2) & 1, (k >> 1) & 1, k & 1
            peer = (1 - x if dx else x, 1 - y if dy else y, 1 - c if dc else c)
            copies.append(pltpu.make_async_remote_copy(
                src_ref=b_ref, dst_ref=gath.at[my_slot], send_sem=send_sems.at[k - 1], recv_sem=recv_sems.at[k - 1],
                device_id=peer, device_id_type=MESH))
        for cp in copies:
            cp.start()
        for cp in copies:
            cp.wait()
        acc = gath[0]
        for sidx in range(1, N_DEV):
            acc = acc + gath[sidx]
        out_ref[...] = acc

    return _call(body, name="all_reduce_small", in_specs=[VMEM], out_specs=VMEM, out_shape=SDS((nr, lanes), F32),
                 scratch=[pltpu.VMEM((N_DEV, nr, lanes), F32), pltpu.SemaphoreType.DMA((7,)),
                          pltpu.SemaphoreType.DMA((7,))])(buf)


def _ffn_block_fwd(x, gain, wall, offs, fs, tag):
    xn = _rmsnorm_fwd(x, gain, f"{tag}_norm")
    out, g, u, h = _ffn_fwd(x, xn, wall, offs, fs, f"{tag}_fwd")
    return out, (x, gain, xn, g, u, h)


def _ffn_block_bwd(dout, saved, wall, offs, fs, tag):
    x, gain, xn, g, u, h = saved
    dg, du, dxn = _ffn_bwd_act(dout, g, u, wall, offs, fs, f"{tag}_bwd_act")
    dy_b = (FFN_RES * dout).astype(BF16)
    dwg = _mm_tn(dg, xn, f"{tag}_dwg", BF16)
    dwu = _mm_tn(du, xn, f"{tag}_dwu", BF16)
    dwd = _mm_tn(h, dy_b, f"{tag}_dwd", BF16)
    dx, dgain = _rmsnorm_bwd(x, gain, dxn, dout, f"{tag}_norm_bwd")
    return dx, (dwg, dwu, dwd), dgain


def _local_step(x, target, wall, offs, fs, mixw, small, layer1):
    grads = {}
    wall0 = wall

    x1, s_f1a = _ffn_block_fwd(x, small["ffn1_norm"][0], wall, offs[0][0], fs, "l0_ffn1")
    hn0 = _rmsnorm_fwd(x1, small["mix_norm"][0], "l0_mix_norm")
    z = _mm(hn0, mixw["ev_w_main_t"], "nt", "ev_in_proj")
    flog = _mm(hn0, mixw["ev_w_f_t"], "nt", "ev_in_proj_gate")
    a, a1 = _conv_a_fwd(z, small["ev_conv_w32"], small["ev_conv_b"], small["ev_conv_norm"], "ev_conv_fwd")
    qs, kh, vb, fb, ft, xt = _qk_fwd(z, flog, small["ev_b_f128"], small["ev_q_norm2"], small["ev_k_norm2"], "ev_qk_fwd")
    if "wall" in layer1:
        o, lse = _attn_fwd(qs, kh, vb, fb, ft, "ev_attn_fwd")
        wall = layer1["wall"]
    else:
        o, lse, wall = _attn_fwd(qs, kh, vb, fb, ft, "ev_attn_fwd", gather_src=layer1["shard"])
    mixw = {**mixw, **layer1["mix"](wall)}
    ao = jnp.concatenate([a, o.astype(BF16)], axis=1)
    x2 = _mm(ao, mixw["ev_w_out"], "nn", "ev_out_proj", add=x1)
    x3, s_f2a = _ffn_block_fwd(x2, small["ffn2_norm"][0], wall0, offs[0][1], fs, "l0_ffn2")

    x4, s_f1b = _ffn_block_fwd(x3, small["ffn1_norm"][1], wall, offs[1][0], fs, "l1_ffn1")
    hn1 = _rmsnorm_fwd(x4, small["mix_norm"][1], "l1_mix_norm")
    zo = _mm(hn1, mixw["od_w_in_t"], "nt", "od_in_proj")
    y0 = _conv_c_fwd(zo, small["od_conv_w8"], "od_conv_fwd")
    x5 = _mm(y0, mixw["od_w_out"], "nn", "od_out_proj", add=x4)
    x6, s_f2b = _ffn_block_fwd(x5, small["ffn2_norm"][1], wall, offs[1][1], fs, "l1_ffn2")

    loss, d6 = _loss_head(x6, target, "loss_head")

    d5, grads["l1_ffn2"], dn = _ffn_block_bwd(d6, s_f2b, wall, offs[1][1], fs, "l1_ffn2")
    grads["ffn2_norm_1"] = dn
    d5b = d5.astype(BF16)
    dy0 = _mm(d5b, mixw["od_w_out"], "nt", "od_out_proj_bwd")
    grads["od_w_out"] = _mm_tn(y0, d5b, "od_dw_out", BF16)
    dzo, grads["od_conv_w"] = _conv_c_bwd(dy0, zo, small["od_conv_w8"], "od_conv_bwd")
    dh1 = _mm(dzo, mixw["od_w_in_t"], "nn", "od_in_proj_bwd")
    grads["od_w_in_t"] = _mm_tn(dzo, hn1, "od_dw_in", BF16)
    d4, grads["mix_norm_1"] = _rmsnorm_bwd(x4, small["mix_norm"][1], dh1, d5, "l1_mix_norm_bwd")
    d3, grads["l1_ffn1"], grads["ffn1_norm_1"] = _ffn_block_bwd(d4, s_f1b, wall, offs[1][0], fs, "l1_ffn1")

    pair_sums = layer1["reduce"](grads) if "reduce" in layer1 else None

    wall = wall0
    d2, grads["l0_ffn2"], grads["ffn2_norm_0"] = _ffn_block_bwd(d3, s_f2a, wall, offs[0][1], fs, "l0_ffn2")
    d2b = d2.astype(BF16)
    dao = _mm(d2b, mixw["ev_w_out"], "nt", "ev_out_proj_bwd")
    grads["ev_w_out"] = _mm_tn(ao, d2b, "ev_dw_out", BF16)
    da1, grads["ev_conv_norm"], grads["ev_conv_b"] = _conv_a_bwd_norm(dao, a1, small["ev_conv_norm"], "ev_conv_bwd_norm")
    du, dg, grads["ev_conv_w"] = _conv_a_bwd_conv(da1, z, small["ev_conv_w32"], "ev_conv_bwd_conv")
    if pair_sums is None:
        dqs, rs, dkh, dv, df4 = _attn_bwd(qs, kh, vb, fb, ft, lse, o, dao, "ev_attn_bwd")
    else:
        dqs, rs, dkh, dv, df4, grads["exchanged"] = _attn_bwd(qs, kh, vb, fb, ft, lse, o, dao, "ev_attn_bwd",
                                                                exchange_src=pair_sums)
        grads["pair_sums"] = pair_sums
    dq, dk, dvb, grads["ev_q_norm"], grads["ev_k_norm"] = _qk_bwd(
        z, dqs, dkh, dv, small["ev_q_norm2"], small["ev_k_norm2"], "ev_qk_bwd")
    dft = df4[:, 0:2, :].reshape(N_HEADS, -1) + rs.reshape(-1, N_HEADS, HEAD_DIM)[:, :, 0].T
    dxt, dflog, grads["ev_b_f"] = _gate_bwd(dft, xt, "ev_gate_bwd")
    dz = jnp.concatenate([du, dg, dq, dk, dvb], axis=1)
    dflog_b = dflog.astype(BF16)
    dh0 = _mm(dz, mixw["ev_w_main_t"], "nn", "ev_in_proj_bwd")
    dh0 = _mm(dflog_b, mixw["ev_w_f_t"], "nn", "ev_in_proj_gate_bwd", add=dh0)
    dw_main = _mm_tn(dz, hn0, "ev_dw_in", BF16)
    dw_f = _mm(dxt.astype(BF16), hn0, "nn", "ev_dw_in_gate", BF16)
    grads["ev_w_in_t"] = jnp.concatenate([dw_main, dw_f], axis=0)
    d1, grads["mix_norm_0"] = _rmsnorm_bwd(x1, small["mix_norm"][0], dh0, d2, "l0_mix_norm_bwd")
    d0, grads["l0_ffn1"], grads["ffn1_norm_0"] = _ffn_block_bwd(d1, s_f1a, wall, offs[0][0], fs, "l0_ffn1")
    return loss, d0, grads


def _round_up(n, m):
    return -(-n // m) * m


def _pad_rows(a, rows):
    return jnp.pad(a, ((0, rows - a.shape[0]), (0, 0)))


SMALL_ORDER = ("loss", "ffn1_norm", "mix_norm", "ffn2_norm", "ev_b_f", "ev_conv_b", "ev_conv_norm",
               "ev_q_norm", "ev_k_norm", "ev_conv_w", "od_conv_w")


def _pack_small(parts):
    flat = jnp.concatenate([parts[k].reshape(-1).astype(F32) for k in SMALL_ORDER])
    n = _round_up(flat.shape[0], 8 * V7X_LANES)
    return jnp.pad(flat, (0, n - flat.shape[0])).reshape(-1, V7X_LANES)


def _unpack_small(buf, shapes):
    flat = buf.reshape(-1)
    out, pos = {}, 0
    for k in SMALL_ORDER:
        n = math.prod(shapes[k])
        out[k] = flat[pos:pos + n].reshape(shapes[k])
        pos += n
    return out


def kernel(x, ffn1_norm, ffn1_w_gate, ffn1_w_up, ffn1_w_down, mix_norm, ffn2_norm, ffn2_w_gate, ffn2_w_up, ffn2_w_down, ev_w_in, ev_b_f, ev_conv_w, ev_conv_b, ev_conv_norm, ev_q_norm, ev_k_norm, ev_w_out, od_w_in, od_conv_w, od_w_out, loss_target, m_ffn1_norm, m_ffn1_w_gate, m_ffn1_w_up, m_ffn1_w_down, m_mix_norm, m_ffn2_norm, m_ffn2_w_gate, m_ffn2_w_up, m_ffn2_w_down, m_ev_w_in, m_ev_b_f, m_ev_conv_w, m_ev_conv_b, m_ev_conv_norm, m_ev_q_norm, m_ev_k_norm, m_ev_w_out, m_od_w_in, m_od_conv_w, m_od_w_out, v_ffn1_norm, v_ffn1_w_gate, v_ffn1_w_up, v_ffn1_w_down, v_mix_norm, v_ffn2_norm, v_ffn2_w_gate, v_ffn2_w_up, v_ffn2_w_down, v_ev_w_in, v_ev_b_f, v_ev_conv_w, v_ev_conv_b, v_ev_conv_norm, v_ev_q_norm, v_ev_k_norm, v_ev_w_out, v_od_w_in, v_od_conv_w, v_od_w_out):
    weights = dict(ffn1_norm=ffn1_norm, ffn1_w_gate=ffn1_w_gate, ffn1_w_up=ffn1_w_up, ffn1_w_down=ffn1_w_down,
                   mix_norm=mix_norm, ffn2_norm=ffn2_norm, ffn2_w_gate=ffn2_w_gate, ffn2_w_up=ffn2_w_up,
                   ffn2_w_down=ffn2_w_down, ev_w_in=ev_w_in, ev_b_f=ev_b_f, ev_conv_w=ev_conv_w, ev_conv_b=ev_conv_b,
                   ev_conv_norm=ev_conv_norm, ev_q_norm=ev_q_norm, ev_k_norm=ev_k_norm, ev_w_out=ev_w_out,
                   od_w_in=od_w_in, od_conv_w=od_conv_w, od_w_out=od_w_out)
    m_in = dict(ffn1_norm=m_ffn1_norm, ffn1_w_gate=m_ffn1_w_gate, ffn1_w_up=m_ffn1_w_up, ffn1_w_down=m_ffn1_w_down,
                mix_norm=m_mix_norm, ffn2_norm=m_ffn2_norm, ffn2_w_gate=m_ffn2_w_gate, ffn2_w_up=m_ffn2_w_up,
                ffn2_w_down=m_ffn2_w_down, ev_w_in=m_ev_w_in, ev_b_f=m_ev_b_f, ev_conv_w=m_ev_conv_w,
                ev_conv_b=m_ev_conv_b, ev_conv_norm=m_ev_conv_norm, ev_q_norm=m_ev_q_norm, ev_k_norm=m_ev_k_norm,
                ev_w_out=m_ev_w_out, od_w_in=m_od_w_in, od_conv_w=m_od_conv_w, od_w_out=m_od_w_out)
    v_in = dict(ffn1_norm=v_ffn1_norm, ffn1_w_gate=v_ffn1_w_gate, ffn1_w_up=v_ffn1_w_up, ffn1_w_down=v_ffn1_w_down,
                mix_norm=v_mix_norm, ffn2_norm=v_ffn2_norm, ffn2_w_gate=v_ffn2_w_gate, ffn2_w_up=v_ffn2_w_up,
                ffn2_w_down=v_ffn2_w_down, ev_w_in=v_ev_w_in, ev_b_f=v_ev_b_f, ev_conv_w=v_ev_conv_w,
                ev_conv_b=v_ev_conv_b, ev_conv_norm=v_ev_conv_norm, ev_q_norm=v_ev_q_norm, ev_k_norm=v_ev_k_norm,
                ev_w_out=v_ev_w_out, od_w_in=v_od_w_in, od_conv_w=v_od_conv_w, od_w_out=v_od_w_out)
    order = list(weights)

    d = x.shape[-1]
    fs = ffn1_w_gate.shape[2]
    n_in = ev_w_in.shape[2]
    n_in_pad = _round_up(n_in, BF16_ROWS)
    n_out = ev_w_out.shape[1]
    n_od = od_w_in.shape[2]
    d_conv = ev_conv_b.shape[1]
    d_in_even = n_in * N_DEV
    d_main = d_in_even - N_HEADS
    cx, cy, cc = _position()
    me = _dev_index(cx, cy, cc)

    offs = [[(0, fs, 2 * fs), (3 * fs, 4 * fs, 5 * fs)]] * 2
    off_in = 6 * fs
    off_ev_out, off_od_out = off_in + n_in_pad, off_in + n_od
    mixers = ((_pad_rows(ev_w_in[0].T, n_in_pad), ev_w_out[0]), (od_w_in[0].T, od_w_out[0]))
    wlocs = []
    for layer in range(2):
        pieces = []
        for wg, wu, wd in ((ffn1_w_gate, ffn1_w_up, ffn1_w_down), (ffn2_w_gate, ffn2_w_up, ffn2_w_down)):
            pieces += [wg[layer].T, wu[layer].T, wd[layer]]
        wlocs.append(jnp.concatenate([p.astype(BF16) for p in pieces + list(mixers[layer])], axis=0))
    wall = _all_gather(wlocs[0])

    ev_w_in_t = wall[:, off_in:off_in + n_in, :].reshape(d_in_even, d)
    mixw = dict(
        ev_w_main_t=ev_w_in_t[:d_main],
        ev_w_f_t=_pad_rows(ev_w_in_t[d_main:], V7X_LANES),
        ev_w_out=wall[:, off_ev_out:off_ev_out + n_out, :].reshape(N_DEV * n_out, d),
    )

    def odd_mixer_weights(wall1):
        return dict(od_w_in_t=wall1[:, off_in:off_in + n_od, :].reshape(N_DEV * n_od, d),
                    od_w_out=wall1[:, off_od_out:off_od_out + n_out, :].reshape(N_DEV * n_out, d))

    def by_dev(a, rows, pad_to=None):
        a = a.reshape(N_DEV, rows, d)
        return a if pad_to is None else jnp.pad(a, ((0, 0), (0, pad_to - rows), (0, 0)))

    idx = jnp.stack([me] + [_dev_index(*chip, cc) for chip in _other_chips(cx, cy)]).astype(jnp.int32)

    def pair_sums_of(g, layer, mixer_pieces):
        gpieces = [by_dev(t, fs) for blk in ("ffn1", "ffn2") for t in g[f"l{layer}_{blk}"]]
        gall = jnp.concatenate(gpieces + mixer_pieces, axis=1)
        return _pair_sum(gall, _pair_exchange(gall, f"reduce_scatter_pair_exchange_l{layer}"), idx,
                         f"reduce_scatter_pair_sum_l{layer}")

    conv_shapes = dict(ev_conv_w=(CONV_A_WIDTH, d_conv), od_conv_w=(CONV_C_WIDTH, d))
    zero_small = {k: jnp.zeros(s_, F32) for k, s_ in conv_shapes.items()}
    ev_cw_part = lax.dynamic_update_slice(zero_small["ev_conv_w"], ev_conv_w[0], (0, me * ev_conv_w.shape[2]))
    od_cw_part = lax.dynamic_update_slice(zero_small["od_conv_w"], od_conv_w[0], (0, me * od_conv_w.shape[2]))
    zeros_like_small = {k: jnp.zeros((1,), F32) for k in SMALL_ORDER}
    taps = _unpack_small(_all_reduce_small(_pack_small({**zeros_like_small, "ev_conv_w": ev_cw_part,
                                                        "od_conv_w": od_cw_part})),
                         {**{k: (1,) for k in SMALL_ORDER}, **conv_shapes})
    small = dict(
        ffn1_norm=[ffn1_norm[l][None] for l in range(2)], mix_norm=[mix_norm[l][None] for l in range(2)],
        ffn2_norm=[ffn2_norm[l][None] for l in range(2)],
        ev_conv_w32=_pad_rows(taps["ev_conv_w"], CONV_A_WIDTH + 1), ev_conv_b=ev_conv_b, ev_conv_norm=ev_conv_norm,
        ev_b_f128=jnp.pad(ev_b_f, ((0, 0), (0, V7X_LANES - N_HEADS))),
        ev_q_norm2=jnp.tile(ev_q_norm, (1, 2)), ev_k_norm2=jnp.tile(ev_k_norm, (1, 2)),
        od_conv_w8=_pad_rows(taps["od_conv_w"], 8),
    )

    layer1 = dict(shard=wlocs[1], mix=odd_mixer_weights,
                  reduce=lambda g1: pair_sums_of(g1, 1, [by_dev(g1["od_w_in_t"], n_od), by_dev(g1["od_w_out"], n_out)]))
    loss_p, grad_x, g = _local_step(x[0], loss_target[0], wall, offs, fs, mixw, small, layer1)

    tsum0 = pair_sums_of(g, 0, [by_dev(g["ev_w_in_t"], n_in, n_in_pad), by_dev(g["ev_w_out"], n_out)])
    gsums = [_final_sum(tsum0, _chip_exchange(tsum0), "reduce_scatter_final_sum_l0"),
             _final_sum(g["pair_sums"], g["exchanged"], "reduce_scatter_final_sum_l1")]

    grad = {}
    for bi, blk in enumerate(("ffn1", "ffn2")):
        for wi, kind in enumerate(("gate", "up", "down")):
            rows = [gsums[l][offs[l][bi][wi]:offs[l][bi][wi] + fs] for l in range(2)]
            grad[f"{blk}_w_{kind}"] = jnp.stack(rows if kind == "down" else [r.T for r in rows])
    grad["ev_w_in"] = gsums[0][off_in:off_in + n_in].T[None]
    grad["ev_w_out"] = gsums[0][off_ev_out:off_ev_out + n_out][None]
    grad["od_w_in"] = gsums[1][off_in:off_in + n_od].T[None]
    grad["od_w_out"] = gsums[1][off_od_out:off_od_out + n_out][None]

    heads = lambda t: t.reshape(N_HEADS, HEAD_DIM).sum(axis=0)
    parts = dict(
        loss=loss_p[0, 0:1],
        ffn1_norm=jnp.stack([g["ffn1_norm_0"][0], g["ffn1_norm_1"][0]]),
        mix_norm=jnp.stack([g["mix_norm_0"][0], g["mix_norm_1"][0]]),
        ffn2_norm=jnp.stack([g["ffn2_norm_0"][0], g["ffn2_norm_1"][0]]),
        ev_b_f=g["ev_b_f"][:, 0], ev_conv_b=g["ev_conv_b"], ev_conv_norm=g["ev_conv_norm"],
        ev_q_norm=heads(g["ev_q_norm"]), ev_k_norm=heads(g["ev_k_norm"]),
        ev_conv_w=g["ev_conv_w"][:CONV_A_WIDTH], od_conv_w=g["od_conv_w"][:CONV_C_WIDTH])
    small_shapes = dict(loss=(1,), ffn1_norm=ffn1_norm.shape, mix_norm=mix_norm.shape, ffn2_norm=ffn2_norm.shape,
                        ev_b_f=ev_b_f.shape, ev_conv_b=ev_conv_b.shape, ev_conv_norm=ev_conv_norm.shape,
                        ev_q_norm=ev_q_norm.shape, ev_k_norm=ev_k_norm.shape, **conv_shapes)
    red = _unpack_small(_all_reduce_small(_pack_small(parts)), small_shapes)
    loss = red["loss"][0]
    for k in ("ffn1_norm", "mix_norm", "ffn2_norm", "ev_b_f", "ev_conv_b", "ev_conv_norm", "ev_q_norm", "ev_k_norm"):
        grad[k] = red[k]
    grad["ev_conv_w"] = lax.dynamic_slice(red["ev_conv_w"], (0, me * ev_conv_w.shape[2]),
                                          (CONV_A_WIDTH, ev_conv_w.shape[2]))[None]
    grad["od_conv_w"] = lax.dynamic_slice(red["od_conv_w"], (0, me * od_conv_w.shape[2]),
                                          (CONV_C_WIDTH, od_conv_w.shape[2]))[None]

    big = ("ffn1_w_gate", "ffn1_w_up", "ffn1_w_down", "ffn2_w_gate", "ffn2_w_up", "ffn2_w_down",
           "ev_w_in", "ev_w_out", "od_w_in", "od_w_out")
    delta, new_m, new_v = {}, {}, {}
    for k in big:
        shp = weights[k].shape
        flat = lambda t: t.reshape(-1, shp[-1])
        dk, mk, vk = _adamw(flat(weights[k]), flat(grad[k]), flat(m_in[k]), flat(v_in[k]), f"adamw_{k}")
        delta[k], new_m[k], new_v[k] = dk.reshape(shp), mk.reshape(shp), vk.reshape(shp)
    rest = [k for k in order if k not in big]
    cat = lambda src: jnp.concatenate([src[k].reshape(-1) for k in rest])
    n_small = sum(math.prod(weights[k].shape) for k in rest)
    n_pad = _round_up(n_small, 8 * V7X_LANES)
    as_rows = lambda t: jnp.pad(t, (0, n_pad - n_small)).reshape(-1, V7X_LANES)
    v_rows = jnp.pad(cat(v_in), (0, n_pad - n_small), constant_values=1.0).reshape(-1, V7X_LANES)
    ds, ms, vs = _adamw(as_rows(cat(weights)), as_rows(cat(grad)), as_rows(cat(m_in)), v_rows, "adamw_small")
    pos = 0
    for k in rest:
        n = math.prod(weights[k].shape)
        for dst, src in ((delta, ds), (new_m, ms), (new_v, vs)):
            dst[k] = src.reshape(-1)[pos:pos + n].reshape(weights[k].shape)
        pos += n

    return (loss, grad_x[None], *[grad[k] for k in order], *[delta[k] for k in order],
            *[new_m[k] for k in order], *[new_v[k] for k in order])
```

```python
import functools
import math

import jax
import jax.numpy as jnp
from jax import lax
from jax.experimental import pallas as pl
from jax.experimental.pallas import tpu as pltpu

F32 = jnp.float32
BF16 = jnp.bfloat16
SDS = jax.ShapeDtypeStruct
MESH = pl.DeviceIdType.MESH

N_DEV = 8
EPS = 1e-6
FFN_RES = 0.5
HEAD_DIM = 64
N_HEADS = 8
D_ATTN = N_HEADS * HEAD_DIM
N_PAIRS = N_HEADS // 2
PAIR = 2 * HEAD_DIM
ATTN_SCALE = 1.0 / math.sqrt(HEAD_DIM)
CONV_A_WIDTH = 31
CONV_A_HALO = 32
CONV_C_WIDTH = 3
CONV_C_HALO = 8
NEG_BIG = -1e30
ADAM_LR, ADAM_B1, ADAM_B2, ADAM_EPS, ADAM_WD, ADAM_STEP = 0.001, 0.9, 0.999, 1e-08, 0.01, 10

V7X_VMEM_BYTES = 64 * 1024 * 1024
V7X_LANES = 128
BF16_ROWS = 16
MIB = 1024 * 1024

NT = (((1,), (1,)), ((), ()))
TN = (((0,), (0,)), ((), ()))


def _call(body, *, name, out_shape, in_specs, out_specs, grid=(), scratch=(), dims=None, vmem_mb=32, **kw):
    params = dict(vmem_limit_bytes=min(vmem_mb * MIB, V7X_VMEM_BYTES - 4 * MIB))
    if dims is not None:
        params["dimension_semantics"] = dims
    return pl.pallas_call(
        body, name=name, grid=grid, in_specs=in_specs, out_specs=out_specs, out_shape=out_shape,
        scratch_shapes=list(scratch), compiler_params=pltpu.CompilerParams(**params), **kw)


def _tile(n, want=512):
    return want if n % want == 0 else n


def _rows(tm, d, col=0):
    return pl.BlockSpec((tm, d), lambda i: (i, col))


def _const(shape):
    return pl.BlockSpec(shape, lambda *_: (0,) * len(shape))


ANY = pl.BlockSpec(memory_space=pl.ANY)
VMEM = pl.BlockSpec(memory_space=pltpu.VMEM)


def _sigmoid(x):
    return 1.0 / (1.0 + jnp.exp(-x))


def _rmsnorm_fwd(x, gain, name):
    s, d = x.shape
    tm = _tile(s)

    def body(x_ref, g_ref, o_ref):
        xv = x_ref[...]
        r = lax.rsqrt(jnp.mean(xv * xv, axis=-1, keepdims=True) + EPS)
        o_ref[...] = (xv * r * g_ref[...]).astype(BF16)

    return _call(body, name=name, grid=(s // tm,), in_specs=[_rows(tm, d), _const((1, d))],
                 out_specs=_rows(tm, d), out_shape=SDS((s, d), BF16), dims=("parallel",))(x, gain)


def _rmsnorm_bwd(x, gain, dxn, dres, name):
    s, d = x.shape
    tm = _tile(s)

    def body(x_ref, g_ref, dxn_ref, dres_ref, dx_ref, dg_ref):
        xv = x_ref[...]
        r = lax.rsqrt(jnp.mean(xv * xv, axis=-1, keepdims=True) + EPS)
        xh = xv * r
        dv = dxn_ref[...]

        @pl.when(pl.program_id(0) == 0)
        def _():
            dg_ref[...] = jnp.zeros_like(dg_ref)

        dg_ref[...] += jnp.sum(dv * xh, axis=0, keepdims=True)
        dxh = dv * g_ref[...]
        dx_ref[...] = dres_ref[...] + r * (dxh - xh * jnp.mean(dxh * xh, axis=-1, keepdims=True))

    return _call(body, name=name, grid=(s // tm,),
                 in_specs=[_rows(tm, d), _const((1, d)), _rows(tm, d), _rows(tm, d)],
                 out_specs=[_rows(tm, d), _const((1, d))],
                 out_shape=[SDS((s, d), F32), SDS((1, d), F32)], dims=("arbitrary",))(x, gain, dxn, dres)


def _col_tile(n):
    for t in (1024, 768, 512, 256, 128):
        if n % t == 0:
            return t
    return n


def _mm(a, b, mode, name, out_dtype=F32, add=None):
    if mode == "tn":
        k, m = a.shape
        n = b.shape[1]
        bm = 256 if m % 256 == 0 else m

        def body_tn(a_ref, b_ref, o_ref):
            o_ref[...] = lax.dot_general(a_ref[...].astype(BF16), b_ref[...].astype(BF16), TN,
                                         preferred_element_type=F32).astype(out_dtype)

        return _call(body_tn, name=name, grid=(m // bm,),
                     in_specs=[pl.BlockSpec((k, bm), lambda i: (0, i)), _const((k, n))],
                     out_specs=pl.BlockSpec((bm, n), lambda i: (i, 0)),
                     out_shape=SDS((m, n), out_dtype), dims=("parallel",), vmem_mb=48)(a, b)
    m, k = a.shape
    n = b.shape[0] if mode == "nt" else b.shape[1]
    tm, tn = _tile(m), _col_tile(n)
    dn = NT if mode == "nt" else (((1,), (0,)), ((), ()))

    def body(a_ref, b_ref, *rest):
        o_ref = rest[-1]
        acc = lax.dot_general(a_ref[...].astype(BF16), b_ref[...].astype(BF16), dn, preferred_element_type=F32)
        if add is not None:
            acc = acc + rest[0][...]
        o_ref[...] = acc.astype(out_dtype)

    b_spec = (pl.BlockSpec((tn, k), lambda i, j: (j, 0)) if mode == "nt"
              else pl.BlockSpec((k, tn), lambda i, j: (0, j)))
    in_specs = [pl.BlockSpec((tm, k), lambda i, j: (i, 0)), b_spec]
    args = [a, b]
    if add is not None:
        in_specs.append(pl.BlockSpec((tm, tn), lambda i, j: (i, j)))
        args.append(add)
    return _call(body, name=name, grid=(m // tm, n // tn), in_specs=in_specs,
                 out_specs=pl.BlockSpec((tm, tn), lambda i, j: (i, j)),
                 out_shape=SDS((m, n), out_dtype), dims=("parallel", "parallel"), vmem_mb=48)(*args)


def _mm_tn(a, b, name, out_dtype=F32):
    return _mm(a, b, "tn", name, out_dtype)


FFN_TM = 256
FFN_CHUNK = 256


def _load_ffn_weights(w_hbm, offs, fs, dsts, sems):
    copies = []
    for wi, (off, dst) in enumerate(zip(offs, dsts)):
        for j in range(N_DEV):
            cp = pltpu.make_async_copy(w_hbm.at[j, pl.ds(off, fs), :], dst.at[pl.ds(j * fs, fs), :],
                                       sems.at[wi * N_DEV + j])
            cp.start()
            copies.append(cp)
    for cp in copies:
        cp.wait()


def _ffn_fwd(x, xn, wall, offs, fs, name, gather_src=None):
    s, d = x.shape
    f = fs * N_DEV
    tm, ch = _tile(s, FFN_TM), FFN_CHUNK
    n = s // tm
    gathers = gather_src is not None

    def body(x_ref, xn_ref, w_hbm, *rest):
        if gathers:
            src_hbm, out_ref, g_ref, u_ref, h_ref, gathered, wg_s, wu_s, wd_s, sems, send_sems, recv_sems, local_sem = rest
            start, forward, finish = _gather_phases(src_hbm, gathered, send_sems, recv_sems, local_sem)
            pl.when(pl.program_id(0) == 0)(start)
            pl.when(pl.program_id(0) == (3 * n) // 4)(forward)
        else:
            out_ref, g_ref, u_ref, h_ref, wg_s, wu_s, wd_s, sems = rest

        @pl.when(pl.program_id(0) == 0)
        def _():
            _load_ffn_weights(w_hbm, offs, fs, (wg_s, wu_s, wd_s), sems)

        xnv = xn_ref[...]
        acc = jnp.zeros((tm, d), F32)
        for c in range(f // ch):
            sl = slice(c * ch, (c + 1) * ch)
            gb = lax.dot_general(xnv, wg_s[sl, :], NT, preferred_element_type=F32).astype(BF16)
            ub = lax.dot_general(xnv, wu_s[sl, :], NT, preferred_element_type=F32).astype(BF16)
            g_ref[:, sl] = gb
            u_ref[:, sl] = ub
            g = gb.astype(F32)
            hb = (g * _sigmoid(g) * ub.astype(F32)).astype(BF16)
            h_ref[:, sl] = hb
            acc = acc + jnp.dot(hb, wd_s[sl, :], preferred_element_type=F32)
        out_ref[...] = x_ref[...] + FFN_RES * acc
        if gathers:
            pl.when(pl.program_id(0) == n - 1)(finish)

    in_specs, args = [_rows(tm, d), _rows(tm, d), ANY], [x, xn, wall]
    out_specs = [_rows(tm, d), _rows(tm, f), _rows(tm, f), _rows(tm, f)]
    out_shape = [SDS((s, d), F32), SDS((s, f), BF16), SDS((s, f), BF16), SDS((s, f), BF16)]
    scratch = [pltpu.VMEM((f, d), BF16)] * 3 + [pltpu.SemaphoreType.DMA((3 * N_DEV,))]
    if gathers:
        in_specs.append(ANY)
        args.append(gather_src)
        out_specs.append(ANY)
        out_shape.append(SDS((N_DEV,) + gather_src.shape, gather_src.dtype))
        scratch += GATHER_SEMS
    return _call(body, name=name, grid=(n,), in_specs=in_specs, out_specs=out_specs, out_shape=out_shape,
                 scratch=scratch, dims=("arbitrary",), vmem_mb=56)(*args)


def _ffn_bwd_act(dout, g, u, wall, offs, fs, name, exchange_src=None):
    s, d = dout.shape
    f = fs * N_DEV
    tm, ch = _tile(s, FFN_TM), FFN_CHUNK
    n = s // tm
    exchanges = exchange_src is not None

    def body(dout_ref, g_ref, u_ref, w_hbm, *rest):
        if exchanges:
            t_hbm, dg_ref, du_ref, dy_ref, dxn_ref, rcv_ref, wg_s, wu_s, wd_s, sems, send_sems, recv_sems = rest
            start, finish = _chip_exchange_phases(t_hbm, rcv_ref, send_sems, recv_sems)
            pl.when(pl.program_id(0) == 0)(start)
        else:
            dg_ref, du_ref, dy_ref, dxn_ref, wg_s, wu_s, wd_s, sems = rest

        @pl.when(pl.program_id(0) == 0)
        def _():
            _load_ffn_weights(w_hbm, offs, fs, (wg_s, wu_s, wd_s), sems)

        dy = (FFN_RES * dout_ref[...]).astype(BF16)
        dy_ref[...] = dy
        acc = jnp.zeros((tm, d), F32)
        for c in range(f // ch):
            sl = slice(c * ch, (c + 1) * ch)
            dh = lax.dot_general(dy, wd_s[sl, :], NT, preferred_element_type=F32)
            gv = g_ref[:, sl].astype(F32)
            uv = u_ref[:, sl].astype(F32)
            sg = _sigmoid(gv)
            dgb = (dh * uv * sg * (1.0 + gv * (1.0 - sg))).astype(BF16)
            dub = (dh * gv * sg).astype(BF16)
            dg_ref[:, sl] = dgb
            du_ref[:, sl] = dub
            acc = acc + jnp.dot(dgb, wg_s[sl, :], preferred_element_type=F32)
            acc = acc + jnp.dot(dub, wu_s[sl, :], preferred_element_type=F32)
        dxn_ref[...] = acc
        if exchanges:
            pl.when(pl.program_id(0) == n - 1)(finish)

    in_specs, args = [_rows(tm, d), _rows(tm, f), _rows(tm, f), ANY], [dout, g, u, wall]
    out_specs = [_rows(tm, f), _rows(tm, f), _rows(tm, d), _rows(tm, d)]
    out_shape = [SDS((s, f), BF16), SDS((s, f), BF16), SDS((s, d), BF16), SDS((s, d), F32)]
    scratch = [pltpu.VMEM((f, d), BF16)] * 3 + [pltpu.SemaphoreType.DMA((3 * N_DEV,))]
    if exchanges:
        in_specs.append(ANY)
        args.append(exchange_src)
        out_specs.append(ANY)
        out_shape.append(SDS((3,) + exchange_src.shape[1:], exchange_src.dtype))
        scratch += EXCHANGE_SEMS
    return _call(body, name=name, grid=(n,), in_specs=in_specs, out_specs=out_specs, out_shape=out_shape,
                 scratch=scratch, dims=("arbitrary",), vmem_mb=56)(*args)


def _prev_rows(halo, tm, c, col):
    return pl.BlockSpec((halo, c), lambda i: (jnp.maximum(i * (tm // halo) - 1, 0), col))


def _next_rows(halo, tm, c, col, n_blocks):
    return pl.BlockSpec((halo, c), lambda i: (jnp.minimum((i + 1) * (tm // halo), n_blocks - 1), col))


def _conv_a_fwd(z, cw, cb, cn, name):
    s = z.shape[0]
    c = cb.shape[1]
    tm, halo, kw = _tile(s), CONV_A_HALO, CONV_A_WIDTH

    def body(u_ref, g_ref, up_ref, gp_ref, cw_ref, cb_ref, cn_ref, a_ref, a1_ref, buf):
        i = pl.program_id(0)
        buf[0:halo, :] = jnp.where(i > 0, up_ref[...] * _sigmoid(gp_ref[...]), 0.0)
        buf[halo:halo + tm, :] = u_ref[...] * _sigmoid(g_ref[...])
        acc = jnp.zeros((tm, c), F32)
        for k in range(kw):
            acc = acc + cw_ref[k:k + 1, :] * buf[pl.ds(halo - (kw - 1) + k, tm), :]
        a1 = acc + cb_ref[...]
        a1_ref[...] = a1
        a2 = a1 * lax.rsqrt(jnp.mean(a1 * a1, axis=-1, keepdims=True) + EPS) * cn_ref[...]
        a_ref[...] = (a2 * _sigmoid(a2)).astype(BF16)

    return _call(body, name=name, grid=(s // tm,),
                 in_specs=[_rows(tm, c, 0), _rows(tm, c, 1), _prev_rows(halo, tm, c, 0), _prev_rows(halo, tm, c, 1),
                           _const(cw.shape), _const((1, c)), _const((1, c))],
                 out_specs=[_rows(tm, c), _rows(tm, c)],
                 out_shape=[SDS((s, c), BF16), SDS((s, c), F32)],
                 scratch=[pltpu.VMEM((tm + halo, c), F32)], dims=("parallel",))(z, z, z, z, cw, cb, cn)


def _conv_a_bwd_norm(dao, a1, cn, name):
    s, c = a1.shape
    tm = _tile(s)

    def body(da_ref, a1_ref, cn_ref, da1_ref, dcn_ref, dcb_ref):
        a1v = a1_ref[...]
        r = lax.rsqrt(jnp.mean(a1v * a1v, axis=-1, keepdims=True) + EPS)
        xh = a1v * r
        a2 = xh * cn_ref[...]
        sg = _sigmoid(a2)
        da2 = da_ref[...] * sg * (1.0 + a2 * (1.0 - sg))
        dxh = da2 * cn_ref[...]
        da1 = r * (dxh - xh * jnp.mean(dxh * xh, axis=-1, keepdims=True))
        da1_ref[...] = da1

        @pl.when(pl.program_id(0) == 0)
        def _():
            dcn_ref[...] = jnp.zeros_like(dcn_ref)
            dcb_ref[...] = jnp.zeros_like(dcb_ref)

        dcn_ref[...] += jnp.sum(da2 * xh, axis=0, keepdims=True)
        dcb_ref[...] += jnp.sum(da1, axis=0, keepdims=True)

    return _call(body, name=name, grid=(s // tm,),
                 in_specs=[_rows(tm, c, 0), _rows(tm, c), _const((1, c))],
                 out_specs=[_rows(tm, c), _const((1, c)), _const((1, c))],
                 out_shape=[SDS((s, c), F32), SDS((1, c), F32), SDS((1, c), F32)], dims=("arbitrary",))(dao, a1, cn)


def _conv_a_bwd_conv(da1, z, cw, name):
    s, c = da1.shape
    tm, halo, kw = _tile(s), CONV_A_HALO, CONV_A_WIDTH
    n = s // tm

    def body(d_ref, dn_ref, u_ref, g_ref, up_ref, gp_ref, cw_ref, du_ref, dg_ref, dcw_ref, buf, bd):
        i = pl.program_id(0)
        uv = u_ref[...]
        sg = _sigmoid(g_ref[...])
        buf[0:halo, :] = jnp.where(i > 0, up_ref[...] * _sigmoid(gp_ref[...]), 0.0)
        buf[halo:halo + tm, :] = uv * sg
        dv = d_ref[...]
        bd[0:tm, :] = dv
        bd[tm:tm + halo, :] = jnp.where(i < n - 1, dn_ref[...], 0.0)

        @pl.when(i == 0)
        def _():
            dcw_ref[...] = jnp.zeros_like(dcw_ref)

        da0 = jnp.zeros((tm, c), F32)
        for k in range(kw):
            da0 = da0 + cw_ref[k:k + 1, :] * bd[pl.ds(kw - 1 - k, tm), :]
            dcw_ref[k:k + 1, :] += jnp.sum(dv * buf[pl.ds(halo - (kw - 1) + k, tm), :], axis=0, keepdims=True)
        du_ref[...] = (da0 * sg).astype(BF16)
        dg_ref[...] = (da0 * uv * sg * (1.0 - sg)).astype(BF16)

    return _call(body, name=name, grid=(n,),
                 in_specs=[_rows(tm, c), _next_rows(halo, tm, c, 0, s // halo), _rows(tm, c, 0), _rows(tm, c, 1),
                           _prev_rows(halo, tm, c, 0), _prev_rows(halo, tm, c, 1), _const(cw.shape)],
                 out_specs=[_rows(tm, c), _rows(tm, c), _const(cw.shape)],
                 out_shape=[SDS((s, c), BF16), SDS((s, c), BF16), SDS(cw.shape, F32)],
                 scratch=[pltpu.VMEM((tm + halo, c), F32)] * 2, dims=("arbitrary",))(da1, da1, z, z, z, z, cw)


def _lane_is_first_head(tm):
    return lax.broadcasted_iota(jnp.int32, (tm, PAIR), 1) < HEAD_DIM


def _pair_rms(xp, first):
    x2 = xp * xp
    s0 = jnp.sum(jnp.where(first, x2, 0.0), axis=-1, keepdims=True)
    s1 = jnp.sum(jnp.where(first, 0.0, x2), axis=-1, keepdims=True)
    return jnp.where(first, lax.rsqrt(s0 / HEAD_DIM + EPS), lax.rsqrt(s1 / HEAD_DIM + EPS))


def _split3(x):
    hi = x.astype(BF16)
    r1 = x - hi.astype(F32)
    mid = r1.astype(BF16)
    lo = (r1 - mid.astype(F32)).astype(BF16)
    return hi, mid, lo


def _qk_fwd(z, flog, bf, qn2, kn2, name):
    s = z.shape[0]
    tm = _tile(s)
    col0 = (z.shape[1] - 3 * D_ATTN) // D_ATTN

    def body(q_ref, k_ref, v_ref, fl_ref, bf_ref, qn_ref, kn_ref,
             qs_ref, kh_ref, vb_ref, fb_ref, ft_ref, xt_ref, carry):
        i = pl.program_id(0)
        first = _lane_is_first_head(tm)
        for p in range(N_PAIRS):
            sl = slice(p * PAIR, (p + 1) * PAIR)
            q = q_ref[:, sl]
            qs_ref[:, sl] = (q * _pair_rms(q, first) * qn_ref[...] * ATTN_SCALE).astype(BF16)
            k = k_ref[:, sl]
            kh_ref[:, sl] = (k * _pair_rms(k, first) * kn_ref[...]).astype(BF16)
        vb_ref[...] = v_ref[...].astype(BF16)

        xg = fl_ref[...] + bf_ref[...]
        valid = lax.broadcasted_iota(jnp.int32, (tm, V7X_LANES), 1) < N_HEADS
        ls = jnp.where(valid, jnp.minimum(xg, 0.0) - jnp.log(1.0 + jnp.exp(-jnp.abs(xg))), 0.0)
        tri = (lax.broadcasted_iota(jnp.int32, (tm, tm), 1) <= lax.broadcasted_iota(jnp.int32, (tm, tm), 0)).astype(BF16)
        cs = jnp.zeros((tm, V7X_LANES), F32)
        for part in _split3(ls):
            cs = cs + jnp.dot(tri, part, preferred_element_type=F32)

        @pl.when(i == 0)
        def _():
            carry[...] = jnp.zeros_like(carry)

        fv = cs + carry[0:1, :]
        carry[0:1, :] = fv[tm - 1:tm, :]
        ft_ref[...] = fv.T[0:N_HEADS, :]
        xt_ref[...] = xg.T[0:N_HEADS, :]
        for p in range(N_PAIRS):
            fb_ref[:, p * PAIR:(p + 1) * PAIR] = jnp.where(first, fv[:, 2 * p:2 * p + 1], fv[:, 2 * p + 1:2 * p + 2])

    wide = lambda col: pl.BlockSpec((tm, D_ATTN), lambda i: (i, col))
    tcol = pl.BlockSpec((N_HEADS, tm), lambda i: (0, i))
    return _call(body, name=name, grid=(s // tm,),
                 in_specs=[wide(col0), wide(col0 + 1), wide(col0 + 2), _rows(tm, V7X_LANES),
                           _const((1, V7X_LANES)), _const((1, PAIR)), _const((1, PAIR))],
                 out_specs=[wide(0), wide(0), wide(0), wide(0), tcol, tcol],
                 out_shape=[SDS((s, D_ATTN), BF16)] * 3 + [SDS((s, D_ATTN), F32), SDS((N_HEADS, s), F32),
                                                          SDS((N_HEADS, s), F32)],
                 scratch=[pltpu.VMEM((8, V7X_LANES), F32)], dims=("arbitrary",))(z, z, z, flog, bf, qn2, kn2)


ATTN_FWD_SUB = 256
ATTN_BWD_SUB = 512


def _causal_schedule(nq, key_major):
    if key_major:
        pairs = [(i, j) for j in range(nq) for i in range(j, nq)]
    else:
        pairs = [(i, j) for i in range(nq) for j in range(i + 1)]
    return (jnp.asarray([p[0] for p in pairs], jnp.int32), jnp.asarray([p[1] for p in pairs], jnp.int32))


def _sub_scores(qp, kp, ft_row, mine, r, masked, sub, tk):
    qm = jnp.where(mine, qp, jnp.zeros_like(qp))
    s2 = lax.dot_general(qm, kp, NT, preferred_element_type=F32) - ft_row
    if masked:
        row = r * sub + lax.broadcasted_iota(jnp.int32, (sub, tk), 0)
        s2 = jnp.where(lax.broadcasted_iota(jnp.int32, (sub, tk), 1) <= row, s2, NEG_BIG)
    return s2


def _attn_fwd(qs, kh, vb, fb, ft, name, gather_src=None):
    s = qs.shape[0]
    tq = tk = _tile(s)
    nq = s // tq
    sub = min(ATTN_FWD_SUB, tq)
    ii, jj = _causal_schedule(nq, key_major=False)
    n_steps = ii.shape[0]
    gathers = gather_src is not None

    def body(ii_ref, jj_ref, q_ref, k_ref, v_ref, fq_ref, ft_ref, *rest):
        if gathers:
            x_hbm, o_ref, lse_ref, wall_ref, m_s, l_s, acc_s, send_sems, recv_sems, local_sem = rest
        else:
            o_ref, lse_ref, m_s, l_s, acc_s = rest
        p, t = pl.program_id(0), pl.program_id(1)
        i, j = ii_ref[t], jj_ref[t]
        first = _lane_is_first_head(sub)
        if gathers:
            start, forward, finish = _gather_phases(x_hbm, wall_ref, send_sems, recv_sems, local_sem)
            pl.when(jnp.logical_and(p == 0, t == 0))(start)
            pl.when(jnp.logical_and(p == N_PAIRS - 1, t == 0))(forward)

        @pl.when(j == 0)
        def _():
            m_s[...] = jnp.full_like(m_s, NEG_BIG)
            l_s[...] = jnp.zeros_like(l_s)
            acc_s[...] = jnp.zeros_like(acc_s)

        def tile(masked):
            kp, vp = k_ref[...], v_ref[...]
            q_all, fq_all, acc_all = q_ref[...], fq_ref[...], acc_s[...]
            m_all, l_all = (m_s[0], m_s[1]), (l_s[0], l_s[1])
            ft_rows = [ft_ref[pl.ds(2 * p + h, 1), :] for h in range(2)]
            m_out, l_out, acc_out = ([], []), ([], []), []
            for r in range(tq // sub):
                rows = slice(r * sub, (r + 1) * sub)
                qp, fq, acc = q_all[rows, :], fq_all[rows, :], acc_all[rows, :]
                new = []
                for h in range(2):
                    mine = first if h == 0 else jnp.logical_not(first)
                    s2 = _sub_scores(qp, kp, ft_rows[h], mine, r, masked, sub, tk)
                    fqh = fq[:, h * HEAD_DIM:h * HEAD_DIM + 1]
                    m_old = m_all[h][rows, :]
                    m_new = jnp.maximum(m_old, jnp.max(s2, axis=-1, keepdims=True) + fqh)
                    pr = jnp.exp(s2 - (m_new - fqh))
                    alpha = jnp.exp(m_old - m_new)
                    l_out[h].append(alpha * l_all[h][rows, :] + jnp.sum(pr, axis=-1, keepdims=True))
                    m_out[h].append(m_new)
                    new.append(alpha * acc + jnp.dot(pr.astype(BF16), vp, preferred_element_type=F32))
                acc_out.append(jnp.where(first, new[0], new[1]))
            for h in range(2):
                m_s[h] = jnp.concatenate(m_out[h], axis=0)
                l_s[h] = jnp.concatenate(l_out[h], axis=0)
            acc_s[...] = jnp.concatenate(acc_out, axis=0)

        @pl.when(j < i)
        def _():
            tile(False)

        @pl.when(j == i)
        def _():
            tile(True)
            whole = _lane_is_first_head(tq)
            l_pair = jnp.where(whole, l_s[0], l_s[1])
            o_ref[...] = acc_s[...] / l_pair
            lse_ref[...] = jnp.where(whole, m_s[0], m_s[1]) + jnp.log(l_pair)

        if gathers:
            pl.when(jnp.logical_and(p == N_PAIRS - 1, t == n_steps - 1))(finish)

    qblk = pl.BlockSpec((tq, PAIR), lambda p, t, ii_r, jj_r: (ii_r[t], p))
    kblk = pl.BlockSpec((tk, PAIR), lambda p, t, ii_r, jj_r: (jj_r[t], p))
    in_specs = [qblk, kblk, kblk, qblk, pl.BlockSpec((N_HEADS, tk), lambda p, t, ii_r, jj_r: (0, jj_r[t]))]
    out_specs, out_shape = [qblk, qblk], [SDS((s, D_ATTN), F32)] * 2
    scratch = [pltpu.VMEM((2, tq, 1), F32), pltpu.VMEM((2, tq, 1), F32), pltpu.VMEM((tq, PAIR), F32)]
    args = [ii, jj, qs, kh, vb, fb, ft]
    if gathers:
        in_specs.append(ANY)
        out_specs.append(ANY)
        out_shape.append(SDS((N_DEV,) + gather_src.shape, gather_src.dtype))
        scratch += GATHER_SEMS
        args.append(gather_src)
    grid_spec = pltpu.PrefetchScalarGridSpec(num_scalar_prefetch=2, grid=(N_PAIRS, n_steps), in_specs=in_specs,
                                             out_specs=out_specs, scratch_shapes=scratch)
    return pl.pallas_call(
        body, name=name, grid_spec=grid_spec, out_shape=out_shape,
        compiler_params=pltpu.CompilerParams(dimension_semantics=("arbitrary", "arbitrary"),
                                             vmem_limit_bytes=32 * MIB))(*args)


def _attn_bwd(qs, kh, vb, fb, ft, lse, o, dao, name, exchange_src=None):
    s = qs.shape[0]
    tq = tk = _tile(s)
    nq = s // tq
    sub = min(ATTN_BWD_SUB, tq)
    ii, jj = _causal_schedule(nq, key_major=True)
    n_steps = ii.shape[0]

    exchanges = exchange_src is not None

    def body(ii_ref, jj_ref, q_ref, k_ref, v_ref, fq_ref, ft_ref, lse_ref, o_ref, do_ref, *rest):
        if exchanges:
            t_hbm, dq_ref, rs_ref, dk_ref, dv_ref, df_ref, rcv_ref, dk_s, dv_s, df_s, send_sems, recv_sems = rest
        else:
            dq_ref, rs_ref, dk_ref, dv_ref, df_ref, dk_s, dv_s, df_s = rest
        p, t = pl.program_id(0), pl.program_id(1)
        i, j = ii_ref[t], jj_ref[t]
        first = _lane_is_first_head(sub)
        first_k = _lane_is_first_head(tk)
        if exchanges:
            start, finish = _chip_exchange_phases(t_hbm, rcv_ref, send_sems, recv_sems)
            pl.when(jnp.logical_and(p == 0, t == 0))(start)

        @pl.when(t == 0)
        def _():
            dq_ref[...] = jnp.zeros_like(dq_ref)
            rs_ref[...] = jnp.zeros_like(rs_ref)

        @pl.when(i == j)
        def _():
            dk_s[...] = jnp.zeros_like(dk_s)
            dv_s[...] = jnp.zeros_like(dv_s)
            df_s[...] = jnp.zeros_like(df_s)

        def tile(masked):
            kp, vp = k_ref[...], v_ref[...]
            q_all, fq_all, lse_all, o_all, do_all = q_ref[...], fq_ref[...], lse_ref[...], o_ref[...], do_ref[...]
            ft_rows = [ft_ref[pl.ds(2 * p + h, 1), :] for h in range(2)]
            dq_out, rs_out = [], []
            dk_acc, dv_acc = jnp.zeros((tk, PAIR), F32), jnp.zeros((tk, PAIR), F32)
            df_acc = [jnp.zeros((1, tk), F32), jnp.zeros((1, tk), F32)]
            for r in range(tq // sub):
                rows = slice(r * sub, (r + 1) * sub)
                qp, fq, lse, ov, dall = q_all[rows, :], fq_all[rows, :], lse_all[rows, :], o_all[rows, :], do_all[rows, :]
                dq_h, dk_h, dv_h, rs_h = [], [], [], []
                for h in range(2):
                    mine = first if h == 0 else jnp.logical_not(first)
                    s2 = _sub_scores(qp, kp, ft_rows[h], mine, r, masked, sub, tk)
                    lane = slice(h * HEAD_DIM, h * HEAD_DIM + 1)
                    pr = jnp.exp(s2 - (lse[:, lane] - fq[:, lane]))
                    dov = jnp.where(mine, dall, 0.0)
                    dsum = jnp.sum(dov * ov, axis=-1, keepdims=True)
                    dom = dov.astype(BF16)
                    dom_lo = (dov - dom.astype(F32)).astype(BF16)
                    dp = lax.dot_general(dom, vp, NT, preferred_element_type=F32)
                    dp = dp + lax.dot_general(dom_lo, vp, NT, preferred_element_type=F32)
                    ds = pr * (dp - dsum)
                    dsb = ds.astype(BF16)
                    dq_h.append(jnp.dot(dsb, kp, preferred_element_type=F32))
                    dk_h.append(lax.dot_general(dsb, qp, TN, preferred_element_type=F32))
                    dv_h.append(lax.dot_general(pr.astype(BF16), dom, TN, preferred_element_type=F32))
                    rs_h.append(jnp.sum(ds, axis=-1, keepdims=True))
                    df_acc[h] = df_acc[h] - jnp.sum(ds, axis=0, keepdims=True)
                dq_out.append(jnp.where(first, dq_h[0], dq_h[1]))
                rs_out.append(jnp.where(first, rs_h[0], rs_h[1]))
                dk_acc = dk_acc + jnp.where(first_k, dk_h[0], dk_h[1])
                dv_acc = dv_acc + jnp.where(first_k, dv_h[0], dv_h[1])
            grows = pl.ds(pl.multiple_of(i * tq, tq), tq)
            dq_ref[grows, :] += jnp.concatenate(dq_out, axis=0)
            rs_ref[grows, :] += jnp.concatenate(rs_out, axis=0)
            dk_s[...] += dk_acc
            dv_s[...] += dv_acc
            for h in range(2):
                df_s[h:h + 1, :] += df_acc[h]

        @pl.when(j < i)
        def _():
            tile(False)

        @pl.when(j == i)
        def _():
            tile(True)

        @pl.when(i == nq - 1)
        def _():
            dk_ref[...] = dk_s[...]
            dv_ref[...] = dv_s[...]
            df_ref[0] = df_s[...]

        if exchanges:
            pl.when(jnp.logical_and(p == N_PAIRS - 1, t == n_steps - 1))(finish)

    qblk = pl.BlockSpec((tq, PAIR), lambda p, t, ii_r, jj_r: (ii_r[t], p))
    kblk = pl.BlockSpec((tk, PAIR), lambda p, t, ii_r, jj_r: (jj_r[t], p))
    doblk = pl.BlockSpec((tq, PAIR), lambda p, t, ii_r, jj_r: (ii_r[t], N_PAIRS + p))
    whole = pl.BlockSpec((s, PAIR), lambda p, t, ii_r, jj_r: (0, p))
    in_specs = [qblk, kblk, kblk, qblk, pl.BlockSpec((N_HEADS, tk), lambda p, t, ii_r, jj_r: (0, jj_r[t])),
                qblk, qblk, doblk]
    out_specs = [whole, whole, kblk, kblk, pl.BlockSpec((1, 8, tk), lambda p, t, ii_r, jj_r: (p, 0, jj_r[t]))]
    out_shape = [SDS((s, D_ATTN), F32)] * 4 + [SDS((N_PAIRS, 8, s), F32)]
    scratch = [pltpu.VMEM((tk, PAIR), F32), pltpu.VMEM((tk, PAIR), F32), pltpu.VMEM((8, tk), F32)]
    args = [ii, jj, qs, kh, vb, fb, ft, lse, o, dao]
    if exchanges:
        in_specs.append(ANY)
        out_specs.append(ANY)
        out_shape.append(SDS((3,) + exchange_src.shape[1:], exchange_src.dtype))
        scratch += EXCHANGE_SEMS
        args.append(exchange_src)
    grid_spec = pltpu.PrefetchScalarGridSpec(num_scalar_prefetch=2, grid=(N_PAIRS, n_steps), in_specs=in_specs,
                                             out_specs=out_specs, scratch_shapes=scratch)
    return pl.pallas_call(
        body, name=name, grid_spec=grid_spec, out_shape=out_shape,
        compiler_params=pltpu.CompilerParams(dimension_semantics=("arbitrary", "arbitrary"),
                                             vmem_limit_bytes=40 * MIB))(*args)


def _qk_bwd(z, dqs, dkh, dv, qn2, kn2, name):
    s = z.shape[0]
    tm = _tile(s)
    col0 = (z.shape[1] - 3 * D_ATTN) // D_ATTN

    def body(q_ref, k_ref, dqs_ref, dkh_ref, dv_ref, qn_ref, kn_ref, dq_ref, dk_ref, dvb_ref, dqn_ref, dkn_ref):
        first = _lane_is_first_head(tm)

        @pl.when(pl.program_id(0) == 0)
        def _():
            dqn_ref[...] = jnp.zeros_like(dqn_ref)
            dkn_ref[...] = jnp.zeros_like(dkn_ref)

        def through(x_ref, dy_ref, gain_ref, dx_ref, dgain_ref, scale):
            for p in range(N_PAIRS):
                sl = slice(p * PAIR, (p + 1) * PAIR)
                xv = x_ref[:, sl]
                r = _pair_rms(xv, first)
                xh = xv * r
                dy = dy_ref[:, sl] * scale
                dgain_ref[:, sl] += jnp.sum(dy * xh, axis=0, keepdims=True)
                dxh = dy * gain_ref[...]
                t = dxh * xh
                m0 = jnp.sum(jnp.where(first, t, 0.0), axis=-1, keepdims=True)
                m1 = jnp.sum(jnp.where(first, 0.0, t), axis=-1, keepdims=True)
                mean = jnp.where(first, m0, m1) / HEAD_DIM
                dx_ref[:, sl] = (r * (dxh - xh * mean)).astype(BF16)

        through(q_ref, dqs_ref, qn_ref, dq_ref, dqn_ref, ATTN_SCALE)
        through(k_ref, dkh_ref, kn_ref, dk_ref, dkn_ref, 1.0)
        dvb_ref[...] = dv_ref[...].astype(BF16)

    wide = lambda col: pl.BlockSpec((tm, D_ATTN), lambda i: (i, col))
    return _call(body, name=name, grid=(s // tm,),
                 in_specs=[wide(col0), wide(col0 + 1), wide(0), wide(0), wide(0), _const((1, PAIR)), _const((1, PAIR))],
                 out_specs=[wide(0), wide(0), wide(0), _const((1, D_ATTN)), _const((1, D_ATTN))],
                 out_shape=[SDS((s, D_ATTN), BF16)] * 3 + [SDS((1, D_ATTN), F32)] * 2,
                 dims=("arbitrary",))(z, z, dqs, dkh, dv, qn2, kn2)


def _gate_bwd(dft, xt, name):
    s = xt.shape[1]
    tm = _tile(s)
    n = s // tm

    def body(df_ref, xt_ref, dxt_ref, dx_ref, db_ref, carry):
        i = pl.program_id(0)

        @pl.when(i == 0)
        def _():
            carry[...] = jnp.zeros_like(carry)
            db_ref[...] = jnp.zeros_like(db_ref)

        tri = (lax.broadcasted_iota(jnp.int32, (tm, tm), 0) >= lax.broadcasted_iota(jnp.int32, (tm, tm), 1)).astype(BF16)
        rc = jnp.zeros((N_HEADS, tm), F32)
        for part in _split3(df_ref[...]):
            rc = rc + jnp.dot(part, tri, preferred_element_type=F32)
        dls = rc + carry[:, 0:1]
        carry[...] = jnp.broadcast_to(dls[:, 0:1], carry.shape)
        dxt = dls * _sigmoid(-xt_ref[...])
        dxt_ref[...] = dxt
        db_ref[...] += jnp.broadcast_to(jnp.sum(dxt, axis=-1, keepdims=True), db_ref.shape)
        padded = jnp.concatenate([dxt, jnp.zeros((V7X_LANES - N_HEADS, tm), F32)], axis=0)
        dx_ref[...] = padded.T

    rev = pl.BlockSpec((N_HEADS, tm), lambda i: (0, n - 1 - i))
    return _call(body, name=name, grid=(n,), in_specs=[rev, rev],
                 out_specs=[rev, pl.BlockSpec((tm, V7X_LANES), lambda i: (n - 1 - i, 0)), _const((N_HEADS, V7X_LANES))],
                 out_shape=[SDS((N_HEADS, s), F32), SDS((s, V7X_LANES), F32), SDS((N_HEADS, V7X_LANES), F32)],
                 scratch=[pltpu.VMEM((N_HEADS, V7X_LANES), F32)], dims=("arbitrary",))(dft, xt)


def _conv_c_fwd(z, cw, name):
    s = z.shape[0]
    c = z.shape[1] // 3
    tm, halo, kw = _tile(s), CONV_C_HALO, CONV_C_WIDTH

    def body(gb_ref, gc_ref, hh_ref, gcp_ref, hhp_ref, cw_ref, y_ref, buf):
        i = pl.program_id(0)
        buf[0:halo, :] = jnp.where(i > 0, gcp_ref[...] * hhp_ref[...], 0.0)
        buf[halo:halo + tm, :] = gc_ref[...] * hh_ref[...]
        c1 = jnp.zeros((tm, c), F32)
        for k in range(kw):
            c1 = c1 + cw_ref[k:k + 1, :] * buf[pl.ds(halo - (kw - 1) + k, tm), :]
        y_ref[...] = (gb_ref[...] * c1).astype(BF16)

    return _call(body, name=name, grid=(s // tm,),
                 in_specs=[_rows(tm, c, 0), _rows(tm, c, 1), _rows(tm, c, 2), _prev_rows(halo, tm, c, 1),
                           _prev_rows(halo, tm, c, 2), _const(cw.shape)],
                 out_specs=_rows(tm, c), out_shape=SDS((s, c), BF16),
                 scratch=[pltpu.VMEM((tm + halo, c), F32)], dims=("parallel",))(z, z, z, z, z, cw)


def _conv_c_bwd(dy0, z, cw, name):
    s = z.shape[0]
    c = z.shape[1] // 3
    tm, halo, kw = _tile(s), CONV_C_HALO, CONV_C_WIDTH
    n = s // tm

    def body(dy_ref, dyn_ref, gb_ref, gbn_ref, gc_ref, hh_ref, gcp_ref, hhp_ref, cw_ref, dz_ref, dcw_ref, buf, bd):
        i = pl.program_id(0)
        gcv, hhv, dyv = gc_ref[...], hh_ref[...], dy_ref[...]
        buf[0:halo, :] = jnp.where(i > 0, gcp_ref[...] * hhp_ref[...], 0.0)
        buf[halo:halo + tm, :] = gcv * hhv
        dc1 = dyv * gb_ref[...]
        bd[0:tm, :] = dc1
        bd[tm:tm + halo, :] = jnp.where(i < n - 1, dyn_ref[...] * gbn_ref[...], 0.0)

        @pl.when(i == 0)
        def _():
            dcw_ref[...] = jnp.zeros_like(dcw_ref)

        c1 = jnp.zeros((tm, c), F32)
        dc0 = jnp.zeros((tm, c), F32)
        for k in range(kw):
            shifted = buf[pl.ds(halo - (kw - 1) + k, tm), :]
            c1 = c1 + cw_ref[k:k + 1, :] * shifted
            dc0 = dc0 + cw_ref[k:k + 1, :] * bd[pl.ds(kw - 1 - k, tm), :]
            dcw_ref[k:k + 1, :] += jnp.sum(dc1 * shifted, axis=0, keepdims=True)
        dz_ref[:, 0:c] = (dyv * c1).astype(BF16)
        dz_ref[:, c:2 * c] = (dc0 * hhv).astype(BF16)
        dz_ref[:, 2 * c:3 * c] = (dc0 * gcv).astype(BF16)

    return _call(body, name=name, grid=(n,),
                 in_specs=[_rows(tm, c), _next_rows(halo, tm, c, 0, s // halo), _rows(tm, c, 0),
                           _next_rows(halo, tm, c, 0, s // halo), _rows(tm, c, 1), _rows(tm, c, 2),
                           _prev_rows(halo, tm, c, 1), _prev_rows(halo, tm, c, 2), _const(cw.shape)],
                 out_specs=[_rows(tm, 3 * c), _const(cw.shape)],
                 out_shape=[SDS((s, 3 * c), BF16), SDS(cw.shape, F32)],
                 scratch=[pltpu.VMEM((tm + halo, c), F32)] * 2, dims=("arbitrary",),
                 vmem_mb=48)(dy0, dy0, z, z, z, z, z, z, cw)


def _loss_head(y, target, name):
    s, d = y.shape
    tm = _tile(s)

    def body(y_ref, t_ref, loss_ref, dy_ref):
        e = y_ref[...] - t_ref[...]

        @pl.when(pl.program_id(0) == 0)
        def _():
            loss_ref[...] = jnp.zeros_like(loss_ref)

        loss_ref[...] += 0.5 * jnp.sum(jnp.mean(e * e, axis=-1, keepdims=True))
        dy_ref[...] = e / d

    return _call(body, name=name, grid=(s // tm,), in_specs=[_rows(tm, d), _rows(tm, d)],
                 out_specs=[_const((8, V7X_LANES)), _rows(tm, d)],
                 out_shape=[SDS((8, V7X_LANES), F32), SDS((s, d), F32)], dims=("arbitrary",))(y, target)


def _adamw(w, g, m, v, name):
    r, c = w.shape
    tr = next((t for t in (512, 256, 128, 64, 32, 16, 8) if r % t == 0), r)

    def body(w_ref, g_ref, m_ref, v_ref, d_ref, mo_ref, vo_ref):
        gv = g_ref[...]
        mn = ADAM_B1 * m_ref[...] + (1.0 - ADAM_B1) * gv
        vn = ADAM_B2 * v_ref[...] + (1.0 - ADAM_B2) * (gv * gv)
        m_hat = mn / (1.0 - ADAM_B1 ** ADAM_STEP)
        v_hat = vn / (1.0 - ADAM_B2 ** ADAM_STEP)
        d_ref[...] = -ADAM_LR * (m_hat / (jnp.sqrt(v_hat) + ADAM_EPS) + ADAM_WD * w_ref[...])
        mo_ref[...] = mn
        vo_ref[...] = vn

    spec = _rows(tr, c)
    return _call(body, name=name, grid=(r // tr,), in_specs=[spec] * 4, out_specs=[spec] * 3,
                 out_shape=[SDS((r, c), F32)] * 3, dims=("parallel",))(w, g, m, v)


def _position():
    return lax.axis_index("x"), lax.axis_index("y"), lax.axis_index("c")


def _other_chips(x, y):
    return [(1 - x, y), (x, 1 - y), (1 - x, 1 - y)]


def _dev_index(px, py, pc):
    return 4 * px + 2 * py + pc


def _all_gather(wloc):
    r, d = wloc.shape

    def body(x_ref, out_ref, send_sems, recv_sems, local_sem):
        start, forward, finish = _gather_phases(x_ref, out_ref, send_sems, recv_sems, local_sem)
        start()
        forward()
        finish()

    return _call(body, name="all_gather_weights", in_specs=[ANY], out_specs=ANY,
                 out_shape=SDS((N_DEV, r, d), wloc.dtype), scratch=GATHER_SEMS)(wloc)


GATHER_SEMS = [pltpu.SemaphoreType.DMA((7,)), pltpu.SemaphoreType.DMA((7,)), pltpu.SemaphoreType.DMA((1,))]


def _gather_phases(x_ref, out_ref, send_sems, recv_sems, local_sem):
    x, y, c = _position()
    me, sibling = (x, y, c), (x, y, 1 - c)
    chips = _other_chips(x, y)

    def slot(dev):
        return out_ref.at[_dev_index(*dev)]

    def copy(k, block, to, src=None):
        return pltpu.make_async_remote_copy(
            src_ref=slot(block) if src is None else src, dst_ref=slot(block),
            send_sem=send_sems.at[k], recv_sem=recv_sems.at[k], device_id=to, device_id_type=MESH)

    mine = pltpu.make_async_copy(x_ref, slot(me), local_sem.at[0])
    first = [copy(0, me, sibling, src=x_ref)] + [copy(1 + j, me, (*chip, c), src=x_ref) for j, chip in enumerate(chips)]
    passed = [copy(4 + j, (*chip, c), sibling) for j, chip in enumerate(chips)]

    def start():
        mine.start()
        for cp in first:
            cp.start()

    def forward():
        for j, chip in enumerate(chips):
            copy(1 + j, (*chip, c), me).wait_recv()
            passed[j].start()

    def finish():
        copy(0, sibling, me).wait_recv()
        for j, chip in enumerate(chips):
            copy(4 + j, (*chip, 1 - c), me).wait_recv()
        for cp in first + passed:
            cp.wait_send()
        mine.wait()

    return start, forward, finish


def _row_block(r):
    return next(t for t in range(704, 0, -BF16_ROWS) if r % t == 0)


def _pair_exchange(gall, name):
    _, r, d = gall.shape

    def body(g_ref, out_ref, send_sems, recv_sems):
        x, y, c = _position()
        sibling = (x, y, 1 - c)
        dests = [sibling] + [(*chip, 1 - c) for chip in _other_chips(x, y)]
        copies = [pltpu.make_async_remote_copy(
            src_ref=g_ref.at[_dev_index(*dest)], dst_ref=out_ref.at[k], send_sem=send_sems.at[k],
            recv_sem=recv_sems.at[k], device_id=sibling, device_id_type=MESH) for k, dest in enumerate(dests)]
        for cp in copies:
            cp.start()
        for cp in copies:
            cp.wait()

    return _call(body, name=name, in_specs=[ANY], out_specs=ANY,
                 out_shape=SDS((4, r, d), gall.dtype),
                 scratch=[pltpu.SemaphoreType.DMA((4,)), pltpu.SemaphoreType.DMA((4,))])(gall)


def _pair_sum(gall, sib, idx, name):
    _, r, d = gall.shape
    tr = _row_block(r)

    def body(idx_ref, a_ref, b_ref, o_ref):
        o_ref[...] = (a_ref[...].astype(F32) + b_ref[...].astype(F32)).astype(o_ref.dtype)

    grid_spec = pltpu.PrefetchScalarGridSpec(
        num_scalar_prefetch=1, grid=(4, r // tr),
        in_specs=[pl.BlockSpec((1, tr, d), lambda k, i, idx_ref: (idx_ref[k], i, 0)),
                  pl.BlockSpec((1, tr, d), lambda k, i, idx_ref: (k, i, 0))],
        out_specs=pl.BlockSpec((1, tr, d), lambda k, i, idx_ref: (k, i, 0)))
    return pl.pallas_call(body, name=name, grid_spec=grid_spec,
                          out_shape=SDS((4, r, d), gall.dtype),
                          compiler_params=pltpu.CompilerParams(dimension_semantics=("parallel", "parallel")))(idx, gall, sib)


def _chip_exchange(tsum):
    _, r, d = tsum.shape

    def body(t_ref, out_ref, send_sems, recv_sems):
        start, finish = _chip_exchange_phases(t_ref, out_ref, send_sems, recv_sems)
        start()
        finish()

    return _call(body, name="reduce_scatter_chip_exchange", in_specs=[ANY], out_specs=ANY,
                 out_shape=SDS((3, r, d), tsum.dtype), scratch=EXCHANGE_SEMS)(tsum)


EXCHANGE_SEMS = [pltpu.SemaphoreType.DMA((3,)), pltpu.SemaphoreType.DMA((3,))]


def _chip_exchange_phases(t_ref, out_ref, send_sems, recv_sems):
    x, y, c = _position()
    copies = [pltpu.make_async_remote_copy(
        src_ref=t_ref.at[1 + k], dst_ref=out_ref.at[k], send_sem=send_sems.at[k], recv_sem=recv_sems.at[k],
        device_id=(*chip, c), device_id_type=MESH) for k, chip in enumerate(_other_chips(x, y))]

    def start():
        for cp in copies:
            cp.start()

    def finish():
        for cp in copies:
            cp.wait()

    return start, finish


def _final_sum(tsum, rcv, name):
    _, r, d = tsum.shape
    tr = _row_block(r)

    def body(t_ref, r_ref, o_ref):
        acc = t_ref[0].astype(F32)
        for k in range(3):
            acc = acc + r_ref[k].astype(F32)
        o_ref[...] = acc

    return _call(body, name=name, grid=(r // tr,),
                 in_specs=[pl.BlockSpec((1, tr, d), lambda i: (0, i, 0)), pl.BlockSpec((3, tr, d), lambda i: (0, i, 0))],
                 out_specs=_rows(tr, d), out_shape=SDS((r, d), F32), dims=("parallel",))(tsum, rcv)


def _all_reduce_small(buf):
    nr, lanes = buf.shape

    def body(b_ref, out_ref, gath, send_sems, recv_sems):
        x, y, c = _position()
        my_slot = _dev_index(x, y, c)
        gath[my_slot] = b_ref[...]
        copies = []
        for k in range(1, N_DEV):
            dx, dy, dc = (k >> 2) & 1, (k >> 1) & 1, k & 1
            peer = (1 - x if dx else x, 1 - y if dy else y, 1 - c if dc else c)
            copies.append(pltpu.make_async_remote_copy(
                src_ref=b_ref, dst_ref=gath.at[my_slot], send_sem=send_sems.at[k - 1], recv_sem=recv_sems.at[k - 1],
                device_id=peer, device_id_type=MESH))
        for cp in copies:
            cp.start()
        for cp in copies:
            cp.wait()
        acc = gath[0]
        for sidx in range(1, N_DEV):
            acc = acc + gath[sidx]
        out_ref[...] = acc

    return _call(body, name="all_reduce_small", in_specs=[VMEM], out_specs=VMEM, out_shape=SDS((nr, lanes), F32),
                 scratch=[pltpu.VMEM((N_DEV, nr, lanes), F32), pltpu.SemaphoreType.DMA((7,)),
                          pltpu.SemaphoreType.DMA((7,))])(buf)


def _ffn_block_fwd(x, gain, wall, offs, fs, tag, gather_src=None):
    xn = _rmsnorm_fwd(x, gain, f"{tag}_norm")
    res = _ffn_fwd(x, xn, wall, offs, fs, f"{tag}_fwd", gather_src)
    out, g, u, h = res[:4]
    return out, (x, gain, xn, g, u, h), (res[4] if gather_src is not None else None)


def _ffn_block_bwd(dout, saved, wall, offs, fs, tag, exchange_src=None):
    x, gain, xn, g, u, h = saved
    res = _ffn_bwd_act(dout, g, u, wall, offs, fs, f"{tag}_bwd_act", exchange_src)
    dg, du, dy_b, dxn = res[:4]
    dwg = _mm_tn(dg, xn, f"{tag}_dwg", BF16)
    dwu = _mm_tn(du, xn, f"{tag}_dwu", BF16)
    dwd = _mm_tn(h, dy_b, f"{tag}_dwd", BF16)
    dx, dgain = _rmsnorm_bwd(x, gain, dxn, dout, f"{tag}_norm_bwd")
    return dx, (dwg, dwu, dwd), dgain, (res[4] if exchange_src is not None else None)


def _local_step(x, target, wall_a, fs, small, plan):
    grads = {}
    first, second = (0, fs, 2 * fs), (3 * fs, 4 * fs, 5 * fs)

    x1, s_f1a, wall_b = _ffn_block_fwd(x, small["ffn1_norm"][0], wall_a, first, fs, "l0_ffn1", plan.get("shard_b"))
    wall_b = plan.get("wall_b", wall_b)
    mixw = plan["mix_b"](wall_b)
    hn0 = _rmsnorm_fwd(x1, small["mix_norm"][0], "l0_mix_norm")
    z = _mm(hn0, mixw["ev_w_main_t"], "nt", "ev_in_proj")
    flog = _mm(hn0, mixw["ev_w_f_t"], "nt", "ev_in_proj_gate")
    a, a1 = _conv_a_fwd(z, small["ev_conv_w32"], small["ev_conv_b"], small["ev_conv_norm"], "ev_conv_fwd")
    qs, kh, vb, fb, ft, xt = _qk_fwd(z, flog, small["ev_b_f128"], small["ev_q_norm2"], small["ev_k_norm2"], "ev_qk_fwd")
    if "wall_c" in plan:
        o, lse = _attn_fwd(qs, kh, vb, fb, ft, "ev_attn_fwd")
        wall_c = plan["wall_c"]
    else:
        o, lse, wall_c = _attn_fwd(qs, kh, vb, fb, ft, "ev_attn_fwd", gather_src=plan["shard_c"])
    mixw = {**mixw, **plan["mix_c"](wall_c)}
    ao = jnp.concatenate([a, o.astype(BF16)], axis=1)
    x2 = _mm(ao, mixw["ev_w_out"], "nn", "ev_out_proj", add=x1)
    x3, s_f2a, _ = _ffn_block_fwd(x2, small["ffn2_norm"][0], wall_b, first, fs, "l0_ffn2")

    x4, s_f1b, _ = _ffn_block_fwd(x3, small["ffn1_norm"][1], wall_c, first, fs, "l1_ffn1")
    hn1 = _rmsnorm_fwd(x4, small["mix_norm"][1], "l1_mix_norm")
    zo = _mm(hn1, mixw["od_w_in_t"], "nt", "od_in_proj")
    y0 = _conv_c_fwd(zo, small["od_conv_w8"], "od_conv_fwd")
    x5 = _mm(y0, mixw["od_w_out"], "nn", "od_out_proj", add=x4)
    x6, s_f2b, _ = _ffn_block_fwd(x5, small["ffn2_norm"][1], wall_c, second, fs, "l1_ffn2")

    loss, d6 = _loss_head(x6, target, "loss_head")

    d5, grads["l1_ffn2"], grads["ffn2_norm_1"], _ = _ffn_block_bwd(d6, s_f2b, wall_c, second, fs, "l1_ffn2")
    d5b = d5.astype(BF16)
    dy0 = _mm(d5b, mixw["od_w_out"], "nt", "od_out_proj_bwd")
    grads["od_w_out"] = _mm_tn(y0, d5b, "od_dw_out", BF16)
    dzo, grads["od_conv_w"] = _conv_c_bwd(dy0, zo, small["od_conv_w8"], "od_conv_bwd")
    dh1 = _mm(dzo, mixw["od_w_in_t"], "nn", "od_in_proj_bwd")
    grads["od_w_in_t"] = _mm_tn(dzo, hn1, "od_dw_in", BF16)
    d4, grads["mix_norm_1"] = _rmsnorm_bwd(x4, small["mix_norm"][1], dh1, d5, "l1_mix_norm_bwd")
    d3, grads["l1_ffn1"], grads["ffn1_norm_1"], _ = _ffn_block_bwd(d4, s_f1b, wall_c, first, fs, "l1_ffn1")
    sums_c = plan["reduce_c"](grads) if "reduce_c" in plan else None

    d2, grads["l0_ffn2"], grads["ffn2_norm_0"], _ = _ffn_block_bwd(d3, s_f2a, wall_b, first, fs, "l0_ffn2")
    d2b = d2.astype(BF16)
    dao = _mm(d2b, mixw["ev_w_out"], "nt", "ev_out_proj_bwd")
    grads["ev_w_out"] = _mm_tn(ao, d2b, "ev_dw_out", BF16)
    da1, grads["ev_conv_norm"], grads["ev_conv_b"] = _conv_a_bwd_norm(dao, a1, small["ev_conv_norm"], "ev_conv_bwd_norm")
    du, dg, grads["ev_conv_w"] = _conv_a_bwd_conv(da1, z, small["ev_conv_w32"], "ev_conv_bwd_conv")
    res = _attn_bwd(qs, kh, vb, fb, ft, lse, o, dao, "ev_attn_bwd", exchange_src=sums_c)
    dqs, rs, dkh, dv, df4 = res[:5]
    if sums_c is not None:
        grads["pair_sums_c"], grads["exchanged_c"] = sums_c, res[5]
    dq, dk, dvb, grads["ev_q_norm"], grads["ev_k_norm"] = _qk_bwd(
        z, dqs, dkh, dv, small["ev_q_norm2"], small["ev_k_norm2"], "ev_qk_bwd")
    dft = df4[:, 0:2, :].reshape(N_HEADS, -1) + rs.reshape(-1, N_HEADS, HEAD_DIM)[:, :, 0].T
    dxt, dflog, grads["ev_b_f"] = _gate_bwd(dft, xt, "ev_gate_bwd")
    dz = jnp.concatenate([du, dg, dq, dk, dvb], axis=1)
    dflog_b = dflog.astype(BF16)
    dh0 = _mm(dz, mixw["ev_w_main_t"], "nn", "ev_in_proj_bwd")
    dh0 = _mm(dflog_b, mixw["ev_w_f_t"], "nn", "ev_in_proj_gate_bwd", add=dh0)
    dw_main = _mm_tn(dz, hn0, "ev_dw_in", BF16)
    dw_f = _mm(dxt.astype(BF16), hn0, "nn", "ev_dw_in_gate", BF16)
    grads["ev_w_in_t"] = jnp.concatenate([dw_main, dw_f], axis=0)
    d1, grads["mix_norm_0"] = _rmsnorm_bwd(x1, small["mix_norm"][0], dh0, d2, "l0_mix_norm_bwd")
    sums_b = plan["reduce_b"](grads) if "reduce_b" in plan else None
    d0, grads["l0_ffn1"], grads["ffn1_norm_0"], exchanged_b = _ffn_block_bwd(d1, s_f1a, wall_a, first, fs, "l0_ffn1",
                                                                               exchange_src=sums_b)
    if sums_b is not None:
        grads["pair_sums_b"], grads["exchanged_b"] = sums_b, exchanged_b
    return loss, d0, grads


def _round_up(n, m):
    return -(-n // m) * m


def _pad_rows(a, rows):
    return jnp.pad(a, ((0, rows - a.shape[0]), (0, 0)))


SMALL_ORDER = ("loss", "ffn1_norm", "mix_norm", "ffn2_norm", "ev_b_f", "ev_conv_b", "ev_conv_norm",
               "ev_q_norm", "ev_k_norm", "ev_conv_w", "od_conv_w")


def _pack_small(parts):
    flat = jnp.concatenate([parts[k].reshape(-1).astype(F32) for k in SMALL_ORDER])
    n = _round_up(flat.shape[0], 8 * V7X_LANES)
    return jnp.pad(flat, (0, n - flat.shape[0])).reshape(-1, V7X_LANES)


def _unpack_small(buf, shapes):
    flat = buf.reshape(-1)
    out, pos = {}, 0
    for k in SMALL_ORDER:
        n = math.prod(shapes[k])
        out[k] = flat[pos:pos + n].reshape(shapes[k])
        pos += n
    return out


def kernel(x, ffn1_norm, ffn1_w_gate, ffn1_w_up, ffn1_w_down, mix_norm, ffn2_norm, ffn2_w_gate, ffn2_w_up, ffn2_w_down, ev_w_in, ev_b_f, ev_conv_w, ev_conv_b, ev_conv_norm, ev_q_norm, ev_k_norm, ev_w_out, od_w_in, od_conv_w, od_w_out, loss_target, m_ffn1_norm, m_ffn1_w_gate, m_ffn1_w_up, m_ffn1_w_down, m_mix_norm, m_ffn2_norm, m_ffn2_w_gate, m_ffn2_w_up, m_ffn2_w_down, m_ev_w_in, m_ev_b_f, m_ev_conv_w, m_ev_conv_b, m_ev_conv_norm, m_ev_q_norm, m_ev_k_norm, m_ev_w_out, m_od_w_in, m_od_conv_w, m_od_w_out, v_ffn1_norm, v_ffn1_w_gate, v_ffn1_w_up, v_ffn1_w_down, v_mix_norm, v_ffn2_norm, v_ffn2_w_gate, v_ffn2_w_up, v_ffn2_w_down, v_ev_w_in, v_ev_b_f, v_ev_conv_w, v_ev_conv_b, v_ev_conv_norm, v_ev_q_norm, v_ev_k_norm, v_ev_w_out, v_od_w_in, v_od_conv_w, v_od_w_out):
    weights = dict(ffn1_norm=ffn1_norm, ffn1_w_gate=ffn1_w_gate, ffn1_w_up=ffn1_w_up, ffn1_w_down=ffn1_w_down,
                   mix_norm=mix_norm, ffn2_norm=ffn2_norm, ffn2_w_gate=ffn2_w_gate, ffn2_w_up=ffn2_w_up,
                   ffn2_w_down=ffn2_w_down, ev_w_in=ev_w_in, ev_b_f=ev_b_f, ev_conv_w=ev_conv_w, ev_conv_b=ev_conv_b,
                   ev_conv_norm=ev_conv_norm, ev_q_norm=ev_q_norm, ev_k_norm=ev_k_norm, ev_w_out=ev_w_out,
                   od_w_in=od_w_in, od_conv_w=od_conv_w, od_w_out=od_w_out)
    m_in = dict(ffn1_norm=m_ffn1_norm, ffn1_w_gate=m_ffn1_w_gate, ffn1_w_up=m_ffn1_w_up, ffn1_w_down=m_ffn1_w_down,
                mix_norm=m_mix_norm, ffn2_norm=m_ffn2_norm, ffn2_w_gate=m_ffn2_w_gate, ffn2_w_up=m_ffn2_w_up,
                ffn2_w_down=m_ffn2_w_down, ev_w_in=m_ev_w_in, ev_b_f=m_ev_b_f, ev_conv_w=m_ev_conv_w,
                ev_conv_b=m_ev_conv_b, ev_conv_norm=m_ev_conv_norm, ev_q_norm=m_ev_q_norm, ev_k_norm=m_ev_k_norm,
                ev_w_out=m_ev_w_out, od_w_in=m_od_w_in, od_conv_w=m_od_conv_w, od_w_out=m_od_w_out)
    v_in = dict(ffn1_norm=v_ffn1_norm, ffn1_w_gate=v_ffn1_w_gate, ffn1_w_up=v_ffn1_w_up, ffn1_w_down=v_ffn1_w_down,
                mix_norm=v_mix_norm, ffn2_norm=v_ffn2_norm, ffn2_w_gate=v_ffn2_w_gate, ffn2_w_up=v_ffn2_w_up,
                ffn2_w_down=v_ffn2_w_down, ev_w_in=v_ev_w_in, ev_b_f=v_ev_b_f, ev_conv_w=v_ev_conv_w,
                ev_conv_b=v_ev_conv_b, ev_conv_norm=v_ev_conv_norm, ev_q_norm=v_ev_q_norm, ev_k_norm=v_ev_k_norm,
                ev_w_out=v_ev_w_out, od_w_in=v_od_w_in, od_conv_w=v_od_conv_w, od_w_out=v_od_w_out)
    order = list(weights)

    d = x.shape[-1]
    fs = ffn1_w_gate.shape[2]
    n_in = ev_w_in.shape[2]
    n_in_pad = _round_up(n_in, BF16_ROWS)
    n_out = ev_w_out.shape[1]
    n_od = od_w_in.shape[2]
    d_conv = ev_conv_b.shape[1]
    d_in_even = n_in * N_DEV
    d_main = d_in_even - N_HEADS
    cx, cy, cc = _position()
    me = _dev_index(cx, cy, cc)

    def block(wg, wu, wd, layer):
        return [wg[layer].T, wu[layer].T, wd[layer]]

    def stack(parts):
        return jnp.concatenate([p.astype(BF16) for p in parts], axis=0)

    ffn1, ffn2 = (ffn1_w_gate, ffn1_w_up, ffn1_w_down), (ffn2_w_gate, ffn2_w_up, ffn2_w_down)
    shard_a = stack(block(*ffn1, 0))
    shard_b = stack(block(*ffn2, 0) + [_pad_rows(ev_w_in[0].T, n_in_pad), ev_w_out[0]])
    shard_c = stack(block(*ffn1, 1) + block(*ffn2, 1) + [od_w_in[0].T, od_w_out[0]])
    off_ev_in, off_ev_out = 3 * fs, 3 * fs + n_in_pad
    off_od_in, off_od_out = 6 * fs, 6 * fs + n_od
    wall_a = _all_gather(shard_a)

    def even_mixer_weights(wall_b):
        ev_w_in_t = wall_b[:, off_ev_in:off_ev_in + n_in, :].reshape(d_in_even, d)
        return dict(ev_w_main_t=ev_w_in_t[:d_main], ev_w_f_t=_pad_rows(ev_w_in_t[d_main:], V7X_LANES),
                    ev_w_out=wall_b[:, off_ev_out:off_ev_out + n_out, :].reshape(N_DEV * n_out, d))

    def odd_mixer_weights(wall_c):
        return dict(od_w_in_t=wall_c[:, off_od_in:off_od_in + n_od, :].reshape(N_DEV * n_od, d),
                    od_w_out=wall_c[:, off_od_out:off_od_out + n_out, :].reshape(N_DEV * n_out, d))

    def by_dev(a, rows, pad_to=None):
        a = a.reshape(N_DEV, rows, d)
        return a if pad_to is None else jnp.pad(a, ((0, 0), (0, pad_to - rows), (0, 0)))

    idx = jnp.stack([me] + [_dev_index(*chip, cc) for chip in _other_chips(cx, cy)]).astype(jnp.int32)

    def pair_sums_of(pieces, tag):
        gall = jnp.concatenate(pieces, axis=1)
        return _pair_sum(gall, _pair_exchange(gall, f"reduce_scatter_pair_exchange_{tag}"), idx,
                         f"reduce_scatter_pair_sum_{tag}")

    def ffn_pieces(g, key):
        return [by_dev(t, fs) for t in g[key]]

    conv_shapes = dict(ev_conv_w=(CONV_A_WIDTH, d_conv), od_conv_w=(CONV_C_WIDTH, d))
    zero_small = {k: jnp.zeros(s_, F32) for k, s_ in conv_shapes.items()}
    ev_cw_part = lax.dynamic_update_slice(zero_small["ev_conv_w"], ev_conv_w[0], (0, me * ev_conv_w.shape[2]))
    od_cw_part = lax.dynamic_update_slice(zero_small["od_conv_w"], od_conv_w[0], (0, me * od_conv_w.shape[2]))
    zeros_like_small = {k: jnp.zeros((1,), F32) for k in SMALL_ORDER}
    taps = _unpack_small(_all_reduce_small(_pack_small({**zeros_like_small, "ev_conv_w": ev_cw_part,
                                                        "od_conv_w": od_cw_part})),
                         {**{k: (1,) for k in SMALL_ORDER}, **conv_shapes})
    small = dict(
        ffn1_norm=[ffn1_norm[l][None] for l in range(2)], mix_norm=[mix_norm[l][None] for l in range(2)],
        ffn2_norm=[ffn2_norm[l][None] for l in range(2)],
        ev_conv_w32=_pad_rows(taps["ev_conv_w"], CONV_A_WIDTH + 1), ev_conv_b=ev_conv_b, ev_conv_norm=ev_conv_norm,
        ev_b_f128=jnp.pad(ev_b_f, ((0, 0), (0, V7X_LANES - N_HEADS))),
        ev_q_norm2=jnp.tile(ev_q_norm, (1, 2)), ev_k_norm2=jnp.tile(ev_k_norm, (1, 2)),
        od_conv_w8=_pad_rows(taps["od_conv_w"], 8),
    )

    plan = dict(
        shard_b=shard_b, shard_c=shard_c, mix_b=even_mixer_weights, mix_c=odd_mixer_weights,
        reduce_c=lambda g1: pair_sums_of(ffn_pieces(g1, "l1_ffn1") + ffn_pieces(g1, "l1_ffn2")
                                         + [by_dev(g1["od_w_in_t"], n_od), by_dev(g1["od_w_out"], n_out)], "c"),
        reduce_b=lambda g1: pair_sums_of(ffn_pieces(g1, "l0_ffn2")
                                         + [by_dev(g1["ev_w_in_t"], n_in, n_in_pad), by_dev(g1["ev_w_out"], n_out)], "b"))
    loss_p, grad_x, g = _local_step(x[0], loss_target[0], wall_a, fs, small, plan)

    sums_a = pair_sums_of(ffn_pieces(g, "l0_ffn1"), "a")
    gsum_a = _final_sum(sums_a, _chip_exchange(sums_a), "reduce_scatter_final_sum_a")
    gsum_b = _final_sum(g["pair_sums_b"], g["exchanged_b"], "reduce_scatter_final_sum_b")
    gsum_c = _final_sum(g["pair_sums_c"], g["exchanged_c"], "reduce_scatter_final_sum_c")

    grad = {}
    where = dict(ffn1=((gsum_a, 0), (gsum_c, 0)), ffn2=((gsum_b, 0), (gsum_c, 3 * fs)))
    for blk, places in where.items():
        for wi, kind in enumerate(("gate", "up", "down")):
            rows = [buf[off + wi * fs:off + (wi + 1) * fs] for buf, off in places]
            grad[f"{blk}_w_{kind}"] = jnp.stack(rows if kind == "down" else [r.T for r in rows])
    grad["ev_w_in"] = gsum_b[off_ev_in:off_ev_in + n_in].T[None]
    grad["ev_w_out"] = gsum_b[off_ev_out:off_ev_out + n_out][None]
    grad["od_w_in"] = gsum_c[off_od_in:off_od_in + n_od].T[None]
    grad["od_w_out"] = gsum_c[off_od_out:off_od_out + n_out][None]

    heads = lambda t: t.reshape(N_HEADS, HEAD_DIM).sum(axis=0)
    parts = dict(
        loss=loss_p[0, 0:1],
        ffn1_norm=jnp.stack([g["ffn1_norm_0"][0], g["ffn1_norm_1"][0]]),
        mix_norm=jnp.stack([g["mix_norm_0"][0], g["mix_norm_1"][0]]),
        ffn2_norm=jnp.stack([g["ffn2_norm_0"][0], g["ffn2_norm_1"][0]]),
        ev_b_f=g["ev_b_f"][:, 0], ev_conv_b=g["ev_conv_b"], ev_conv_norm=g["ev_conv_norm"],
        ev_q_norm=heads(g["ev_q_norm"]), ev_k_norm=heads(g["ev_k_norm"]),
        ev_conv_w=g["ev_conv_w"][:CONV_A_WIDTH], od_conv_w=g["od_conv_w"][:CONV_C_WIDTH])
    small_shapes = dict(loss=(1,), ffn1_norm=ffn1_norm.shape, mix_norm=mix_norm.shape, ffn2_norm=ffn2_norm.shape,
                        ev_b_f=ev_b_f.shape, ev_conv_b=ev_conv_b.shape, ev_conv_norm=ev_conv_norm.shape,
                        ev_q_norm=ev_q_norm.shape, ev_k_norm=ev_k_norm.shape, **conv_shapes)
    red = _unpack_small(_all_reduce_small(_pack_small(parts)), small_shapes)
    loss = red["loss"][0]
    for k in ("ffn1_norm", "mix_norm", "ffn2_norm", "ev_b_f", "ev_conv_b", "ev_conv_norm", "ev_q_norm", "ev_k_norm"):
        grad[k] = red[k]
    grad["ev_conv_w"] = lax.dynamic_slice(red["ev_conv_w"], (0, me * ev_conv_w.shape[2]),
                                          (CONV_A_WIDTH, ev_conv_w.shape[2]))[None]
    grad["od_conv_w"] = lax.dynamic_slice(red["od_conv_w"], (0, me * od_conv_w.shape[2]),
                                          (CONV_C_WIDTH, od_conv_w.shape[2]))[None]

    big = ("ffn1_w_gate", "ffn1_w_up", "ffn1_w_down", "ffn2_w_gate", "ffn2_w_up", "ffn2_w_down",
           "ev_w_in", "ev_w_out", "od_w_in", "od_w_out")
    delta, new_m, new_v = {}, {}, {}
    for k in big:
        shp = weights[k].shape
        flat = lambda t: t.reshape(-1, shp[-1])
        dk, mk, vk = _adamw(flat(weights[k]), flat(grad[k]), flat(m_in[k]), flat(v_in[k]), f"adamw_{k}")
        delta[k], new_m[k], new_v[k] = dk.reshape(shp), mk.reshape(shp), vk.reshape(shp)
    rest = [k for k in order if k not in big]
    cat = lambda src: jnp.concatenate([src[k].reshape(-1) for k in rest])
    n_small = sum(math.prod(weights[k].shape) for k in rest)
    n_pad = _round_up(n_small, 8 * V7X_LANES)
    as_rows = lambda t: jnp.pad(t, (0, n_pad - n_small)).reshape(-1, V7X_LANES)
    v_rows = jnp.pad(cat(v_in), (0, n_pad - n_small), constant_values=1.0).reshape(-1, V7X_LANES)
    ds, ms, vs = _adamw(as_rows(cat(weights)), as_rows(cat(grad)), as_rows(cat(m_in)), v_rows, "adamw_small")
    pos = 0
    for k in rest:
        n = math.prod(weights[k].shape)
        for dst, src in ((delta, ds), (new_m, ms), (new_v, vs)):
            dst[k] = src.reshape(-1)[pos:pos + n].reshape(weights[k].shape)
        pos += n

    return (loss, grad_x[None], *[grad[k] for k in order], *[delta[k] for k in order],
            *[new_m[k] for k in order], *[new_v[k] for k in order])
```

```python
import functools
import math

import jax
import jax.numpy as jnp
from jax import lax
from jax.experimental import pallas as pl
from jax.experimental.pallas import tpu as pltpu

F32 = jnp.float32
BF16 = jnp.bfloat16
SDS = jax.ShapeDtypeStruct
MESH = pl.DeviceIdType.MESH

N_DEV = 8
EPS = 1e-6
FFN_RES = 0.5
HEAD_DIM = 64
N_HEADS = 8
D_ATTN = N_HEADS * HEAD_DIM
N_PAIRS = N_HEADS // 2
PAIR = 2 * HEAD_DIM
ATTN_SCALE = 1.0 / math.sqrt(HEAD_DIM)
CONV_A_WIDTH = 31
CONV_A_HALO = 32
CONV_C_WIDTH = 3
CONV_C_HALO = 8
NEG_BIG = -1e30
ADAM_LR, ADAM_B1, ADAM_B2, ADAM_EPS, ADAM_WD, ADAM_STEP = 0.001, 0.9, 0.999, 1e-08, 0.01, 10

V7X_VMEM_BYTES = 64 * 1024 * 1024
V7X_LANES = 128
BF16_ROWS = 16
MIB = 1024 * 1024

NT = (((1,), (1,)), ((), ()))
TN = (((0,), (0,)), ((), ()))


def _call(body, *, name, out_shape, in_specs, out_specs, grid=(), scratch=(), dims=None, vmem_mb=32, **kw):
    params = dict(vmem_limit_bytes=min(vmem_mb * MIB, V7X_VMEM_BYTES - 4 * MIB))
    if dims is not None:
        params["dimension_semantics"] = dims
    call = pl.pallas_call(
        body, name=name, grid=grid, in_specs=in_specs, out_specs=out_specs, out_shape=_in_hbm(out_shape),
        scratch_shapes=list(scratch), compiler_params=pltpu.CompilerParams(**params), **kw)
    return lambda *args: call(*[_keep_in_hbm(a) for a in args])


LARGE_OPERAND_BYTES = MIB


def _is_large(a):
    return a.ndim >= 2 and math.prod(a.shape) * jnp.dtype(a.dtype).itemsize >= LARGE_OPERAND_BYTES


def _keep_in_hbm(a):
    return pltpu.with_memory_space_constraint(a, pltpu.HBM) if _is_large(a) else a


def _in_hbm(out_shape):
    one = lambda s: pltpu.HBM(s.shape, s.dtype) if _is_large(s) else s
    return [one(s) for s in out_shape] if isinstance(out_shape, (list, tuple)) else one(out_shape)


def _tile(n, want=512):
    return want if n % want == 0 else n


def _rows(tm, d, col=0):
    return pl.BlockSpec((tm, d), lambda i: (i, col))


def _const(shape):
    return pl.BlockSpec(shape, lambda *_: (0,) * len(shape))


ANY = pl.BlockSpec(memory_space=pl.ANY)
VMEM = pl.BlockSpec(memory_space=pltpu.VMEM)


def _sigmoid(x):
    return 1.0 / (1.0 + jnp.exp(-x))


def _rmsnorm_fwd(x, gain, name):
    s, d = x.shape
    tm = _tile(s)

    def body(x_ref, g_ref, o_ref):
        xv = x_ref[...]
        r = lax.rsqrt(jnp.mean(xv * xv, axis=-1, keepdims=True) + EPS)
        o_ref[...] = (xv * r * g_ref[...]).astype(BF16)

    return _call(body, name=name, grid=(s // tm,), in_specs=[_rows(tm, d), _const((1, d))],
                 out_specs=_rows(tm, d), out_shape=SDS((s, d), BF16), dims=("parallel",))(x, gain)


def _rmsnorm_bwd(x, gain, dxn, dres, name):
    s, d = x.shape
    tm = _tile(s)

    def body(x_ref, g_ref, dxn_ref, dres_ref, dx_ref, dg_ref):
        xv = x_ref[...]
        r = lax.rsqrt(jnp.mean(xv * xv, axis=-1, keepdims=True) + EPS)
        xh = xv * r
        dv = dxn_ref[...]

        @pl.when(pl.program_id(0) == 0)
        def _():
            dg_ref[...] = jnp.zeros_like(dg_ref)

        dg_ref[...] += jnp.sum(dv * xh, axis=0, keepdims=True)
        dxh = dv * g_ref[...]
        dx_ref[...] = dres_ref[...] + r * (dxh - xh * jnp.mean(dxh * xh, axis=-1, keepdims=True))

    return _call(body, name=name, grid=(s // tm,),
                 in_specs=[_rows(tm, d), _const((1, d)), _rows(tm, d), _rows(tm, d)],
                 out_specs=[_rows(tm, d), _const((1, d))],
                 out_shape=[SDS((s, d), F32), SDS((1, d), F32)], dims=("arbitrary",))(x, gain, dxn, dres)


def _col_tile(n):
    for t in (1024, 768, 512, 256, 128):
        if n % t == 0:
            return t
    return n


def _mm(a, b, mode, name, out_dtype=F32, add=None):
    if mode == "tn":
        k, m = a.shape
        n = b.shape[1]
        bm = 256 if m % 256 == 0 else m

        def body_tn(a_ref, b_ref, o_ref):
            o_ref[...] = lax.dot_general(a_ref[...].astype(BF16), b_ref[...].astype(BF16), TN,
                                         preferred_element_type=F32).astype(out_dtype)

        return _call(body_tn, name=name, grid=(m // bm,),
                     in_specs=[pl.BlockSpec((k, bm), lambda i: (0, i)), _const((k, n))],
                     out_specs=pl.BlockSpec((bm, n), lambda i: (i, 0)),
                     out_shape=SDS((m, n), out_dtype), dims=("parallel",), vmem_mb=48)(a, b)
    m, k = a.shape
    n = b.shape[0] if mode == "nt" else b.shape[1]
    tm, tn = _tile(m), _col_tile(n)
    dn = NT if mode == "nt" else (((1,), (0,)), ((), ()))

    def body(a_ref, b_ref, *rest):
        o_ref = rest[-1]
        acc = lax.dot_general(a_ref[...].astype(BF16), b_ref[...].astype(BF16), dn, preferred_element_type=F32)
        if add is not None:
            acc = acc + rest[0][...]
        o_ref[...] = acc.astype(out_dtype)

    b_spec = (pl.BlockSpec((tn, k), lambda i, j: (j, 0)) if mode == "nt"
              else pl.BlockSpec((k, tn), lambda i, j: (0, j)))
    in_specs = [pl.BlockSpec((tm, k), lambda i, j: (i, 0)), b_spec]
    args = [a, b]
    if add is not None:
        in_specs.append(pl.BlockSpec((tm, tn), lambda i, j: (i, j)))
        args.append(add)
    return _call(body, name=name, grid=(m // tm, n // tn), in_specs=in_specs,
                 out_specs=pl.BlockSpec((tm, tn), lambda i, j: (i, j)),
                 out_shape=SDS((m, n), out_dtype), dims=("parallel", "parallel"), vmem_mb=48)(*args)


def _mm_tn(a, b, name, out_dtype=F32):
    return _mm(a, b, "tn", name, out_dtype)


FFN_TM = 256
FFN_FWD_TM = 512
FFN_CHUNK = 256


def _load_ffn_weights(w_hbm, offs, fs, dsts, sems):
    copies = []
    for wi, (off, dst) in enumerate(zip(offs, dsts)):
        for j in range(N_DEV):
            cp = pltpu.make_async_copy(w_hbm.at[j, pl.ds(off, fs), :], dst.at[pl.ds(j * fs, fs), :],
                                       sems.at[wi * N_DEV + j])
            cp.start()
            copies.append(cp)
    for cp in copies:
        cp.wait()


def _ffn_fwd(x, xn, wall, offs, fs, name, gather_src=None):
    s, d = x.shape
    f = fs * N_DEV
    tm, ch = _tile(s, FFN_FWD_TM), FFN_CHUNK
    n = s // tm
    gathers = gather_src is not None

    def body(x_ref, xn_ref, w_hbm, *rest):
        if gathers:
            src_hbm, out_ref, g_ref, u_ref, h_ref, gathered, wg_s, wu_s, wd_s, sems, send_sems, recv_sems, local_sem = rest
            start, forward, finish = _gather_phases(src_hbm, gathered, send_sems, recv_sems, local_sem)
            pl.when(pl.program_id(0) == 0)(start)
            pl.when(pl.program_id(0) == (3 * n) // 4)(forward)
        else:
            out_ref, g_ref, u_ref, h_ref, wg_s, wu_s, wd_s, sems = rest

        @pl.when(pl.program_id(0) == 0)
        def _():
            _load_ffn_weights(w_hbm, offs, fs, (wg_s, wu_s, wd_s), sems)

        xnv = xn_ref[...]
        acc = jnp.zeros((tm, d), F32)
        for c in range(f // ch):
            sl = slice(c * ch, (c + 1) * ch)
            gb = lax.dot_general(xnv, wg_s[sl, :], NT, preferred_element_type=F32).astype(BF16)
            ub = lax.dot_general(xnv, wu_s[sl, :], NT, preferred_element_type=F32).astype(BF16)
            g_ref[:, sl] = gb
            u_ref[:, sl] = ub
            g = gb.astype(F32)
            hb = (g * _sigmoid(g) * ub.astype(F32)).astype(BF16)
            h_ref[:, sl] = hb
            acc = acc + jnp.dot(hb, wd_s[sl, :], preferred_element_type=F32)
        out_ref[...] = x_ref[...] + FFN_RES * acc
        if gathers:
            pl.when(pl.program_id(0) == n - 1)(finish)

    in_specs, args = [_rows(tm, d), _rows(tm, d), ANY], [x, xn, wall]
    out_specs = [_rows(tm, d), _rows(tm, f), _rows(tm, f), _rows(tm, f)]
    out_shape = [SDS((s, d), F32), SDS((s, f), BF16), SDS((s, f), BF16), SDS((s, f), BF16)]
    scratch = [pltpu.VMEM((f, d), BF16)] * 3 + [pltpu.SemaphoreType.DMA((3 * N_DEV,))]
    if gathers:
        in_specs.append(ANY)
        args.append(gather_src)
        out_specs.append(ANY)
        out_shape.append(SDS((N_DEV,) + gather_src.shape, gather_src.dtype))
        scratch += GATHER_SEMS
    return _call(body, name=name, grid=(n,), in_specs=in_specs, out_specs=out_specs, out_shape=out_shape,
                 scratch=scratch, dims=("arbitrary",), vmem_mb=56)(*args)


def _ffn_bwd_act(dout, g, u, wall, offs, fs, name, exchange_src=None):
    s, d = dout.shape
    f = fs * N_DEV
    tm, ch = _tile(s, FFN_TM), FFN_CHUNK
    n = s // tm
    exchanges = exchange_src is not None

    def body(dout_ref, g_ref, u_ref, w_hbm, *rest):
        if exchanges:
            t_hbm, dg_ref, du_ref, dy_ref, dxn_ref, rcv_ref, wg_s, wu_s, wd_s, sems, send_sems, recv_sems = rest
            start, finish = _chip_exchange_phases(t_hbm, rcv_ref, send_sems, recv_sems)
            pl.when(pl.program_id(0) == 0)(start)
        else:
            dg_ref, du_ref, dy_ref, dxn_ref, wg_s, wu_s, wd_s, sems = rest

        @pl.when(pl.program_id(0) == 0)
        def _():
            _load_ffn_weights(w_hbm, offs, fs, (wg_s, wu_s, wd_s), sems)

        dy = (FFN_RES * dout_ref[...]).astype(BF16)
        dy_ref[...] = dy
        acc = jnp.zeros((tm, d), F32)
        for c in range(f // ch):
            sl = slice(c * ch, (c + 1) * ch)
            dh = lax.dot_general(dy, wd_s[sl, :], NT, preferred_element_type=F32)
            gv = g_ref[:, sl].astype(F32)
            uv = u_ref[:, sl].astype(F32)
            sg = _sigmoid(gv)
            dgb = (dh * uv * sg * (1.0 + gv * (1.0 - sg))).astype(BF16)
            dub = (dh * gv * sg).astype(BF16)
            dg_ref[:, sl] = dgb
            du_ref[:, sl] = dub
            acc = acc + jnp.dot(dgb, wg_s[sl, :], preferred_element_type=F32)
            acc = acc + jnp.dot(dub, wu_s[sl, :], preferred_element_type=F32)
        dxn_ref[...] = acc
        if exchanges:
            pl.when(pl.program_id(0) == n - 1)(finish)

    in_specs, args = [_rows(tm, d), _rows(tm, f), _rows(tm, f), ANY], [dout, g, u, wall]
    out_specs = [_rows(tm, f), _rows(tm, f), _rows(tm, d), _rows(tm, d)]
    out_shape = [SDS((s, f), BF16), SDS((s, f), BF16), SDS((s, d), BF16), SDS((s, d), F32)]
    scratch = [pltpu.VMEM((f, d), BF16)] * 3 + [pltpu.SemaphoreType.DMA((3 * N_DEV,))]
    if exchanges:
        in_specs.append(ANY)
        args.append(exchange_src)
        out_specs.append(ANY)
        out_shape.append(SDS((3,) + exchange_src.shape[1:], exchange_src.dtype))
        scratch += EXCHANGE_SEMS
    return _call(body, name=name, grid=(n,), in_specs=in_specs, out_specs=out_specs, out_shape=out_shape,
                 scratch=scratch, dims=("arbitrary",), vmem_mb=56)(*args)


def _prev_rows(halo, tm, c, col):
    return pl.BlockSpec((halo, c), lambda i: (jnp.maximum(i * (tm // halo) - 1, 0), col))


def _next_rows(halo, tm, c, col, n_blocks):
    return pl.BlockSpec((halo, c), lambda i: (jnp.minimum((i + 1) * (tm // halo), n_blocks - 1), col))


def _conv_a_fwd(z, cw, cb, cn, name):
    s = z.shape[0]
    c = cb.shape[1]
    tm, halo, kw = _tile(s), CONV_A_HALO, CONV_A_WIDTH

    def body(u_ref, g_ref, up_ref, gp_ref, cw_ref, cb_ref, cn_ref, a_ref, a1_ref, buf):
        i = pl.program_id(0)
        buf[0:halo, :] = jnp.where(i > 0, up_ref[...] * _sigmoid(gp_ref[...]), 0.0)
        buf[halo:halo + tm, :] = u_ref[...] * _sigmoid(g_ref[...])
        acc = jnp.zeros((tm, c), F32)
        for k in range(kw):
            acc = acc + cw_ref[k:k + 1, :] * buf[pl.ds(halo - (kw - 1) + k, tm), :]
        a1 = acc + cb_ref[...]
        a1_ref[...] = a1
        a2 = a1 * lax.rsqrt(jnp.mean(a1 * a1, axis=-1, keepdims=True) + EPS) * cn_ref[...]
        a_ref[...] = (a2 * _sigmoid(a2)).astype(BF16)

    return _call(body, name=name, grid=(s // tm,),
                 in_specs=[_rows(tm, c, 0), _rows(tm, c, 1), _prev_rows(halo, tm, c, 0), _prev_rows(halo, tm, c, 1),
                           _const(cw.shape), _const((1, c)), _const((1, c))],
                 out_specs=[_rows(tm, c), _rows(tm, c)],
                 out_shape=[SDS((s, c), BF16), SDS((s, c), F32)],
                 scratch=[pltpu.VMEM((tm + halo, c), F32)], dims=("parallel",))(z, z, z, z, cw, cb, cn)


def _conv_a_bwd_norm(dao, a1, cn, name):
    s, c = a1.shape
    tm = _tile(s)

    def body(da_ref, a1_ref, cn_ref, da1_ref, dcn_ref, dcb_ref):
        a1v = a1_ref[...]
        r = lax.rsqrt(jnp.mean(a1v * a1v, axis=-1, keepdims=True) + EPS)
        xh = a1v * r
        a2 = xh * cn_ref[...]
        sg = _sigmoid(a2)
        da2 = da_ref[...] * sg * (1.0 + a2 * (1.0 - sg))
        dxh = da2 * cn_ref[...]
        da1 = r * (dxh - xh * jnp.mean(dxh * xh, axis=-1, keepdims=True))
        da1_ref[...] = da1

        @pl.when(pl.program_id(0) == 0)
        def _():
            dcn_ref[...] = jnp.zeros_like(dcn_ref)
            dcb_ref[...] = jnp.zeros_like(dcb_ref)

        dcn_ref[...] += jnp.sum(da2 * xh, axis=0, keepdims=True)
        dcb_ref[...] += jnp.sum(da1, axis=0, keepdims=True)

    return _call(body, name=name, grid=(s // tm,),
                 in_specs=[_rows(tm, c, 0), _rows(tm, c), _const((1, c))],
                 out_specs=[_rows(tm, c), _const((1, c)), _const((1, c))],
                 out_shape=[SDS((s, c), F32), SDS((1, c), F32), SDS((1, c), F32)], dims=("arbitrary",))(dao, a1, cn)


def _conv_a_bwd_conv(da1, z, cw, name):
    s, c = da1.shape
    tm, halo, kw = _tile(s), CONV_A_HALO, CONV_A_WIDTH
    n = s // tm

    def body(d_ref, dn_ref, u_ref, g_ref, up_ref, gp_ref, cw_ref, du_ref, dg_ref, dcw_ref, buf, bd):
        i = pl.program_id(0)
        uv = u_ref[...]
        sg = _sigmoid(g_ref[...])
        buf[0:halo, :] = jnp.where(i > 0, up_ref[...] * _sigmoid(gp_ref[...]), 0.0)
        buf[halo:halo + tm, :] = uv * sg
        dv = d_ref[...]
        bd[0:tm, :] = dv
        bd[tm:tm + halo, :] = jnp.where(i < n - 1, dn_ref[...], 0.0)

        @pl.when(i == 0)
        def _():
            dcw_ref[...] = jnp.zeros_like(dcw_ref)

        da0 = jnp.zeros((tm, c), F32)
        for k in range(kw):
            da0 = da0 + cw_ref[k:k + 1, :] * bd[pl.ds(kw - 1 - k, tm), :]
            dcw_ref[k:k + 1, :] += jnp.sum(dv * buf[pl.ds(halo - (kw - 1) + k, tm), :], axis=0, keepdims=True)
        du_ref[...] = (da0 * sg).astype(BF16)
        dg_ref[...] = (da0 * uv * sg * (1.0 - sg)).astype(BF16)

    return _call(body, name=name, grid=(n,),
                 in_specs=[_rows(tm, c), _next_rows(halo, tm, c, 0, s // halo), _rows(tm, c, 0), _rows(tm, c, 1),
                           _prev_rows(halo, tm, c, 0), _prev_rows(halo, tm, c, 1), _const(cw.shape)],
                 out_specs=[_rows(tm, c), _rows(tm, c), _const(cw.shape)],
                 out_shape=[SDS((s, c), BF16), SDS((s, c), BF16), SDS(cw.shape, F32)],
                 scratch=[pltpu.VMEM((tm + halo, c), F32)] * 2, dims=("arbitrary",))(da1, da1, z, z, z, z, cw)


def _lane_is_first_head(tm):
    return lax.broadcasted_iota(jnp.int32, (tm, PAIR), 1) < HEAD_DIM


def _pair_rms(xp, first):
    x2 = xp * xp
    s0 = jnp.sum(jnp.where(first, x2, 0.0), axis=-1, keepdims=True)
    s1 = jnp.sum(jnp.where(first, 0.0, x2), axis=-1, keepdims=True)
    return jnp.where(first, lax.rsqrt(s0 / HEAD_DIM + EPS), lax.rsqrt(s1 / HEAD_DIM + EPS))


def _split3(x):
    hi = x.astype(BF16)
    r1 = x - hi.astype(F32)
    mid = r1.astype(BF16)
    lo = (r1 - mid.astype(F32)).astype(BF16)
    return hi, mid, lo


def _qk_fwd(z, flog, bf, qn2, kn2, name):
    s = z.shape[0]
    tm = _tile(s)
    col0 = (z.shape[1] - 3 * D_ATTN) // D_ATTN

    def body(q_ref, k_ref, v_ref, fl_ref, bf_ref, qn_ref, kn_ref,
             qs_ref, kh_ref, vb_ref, fb_ref, ft_ref, xt_ref, carry):
        i = pl.program_id(0)
        first = _lane_is_first_head(tm)
        for p in range(N_PAIRS):
            sl = slice(p * PAIR, (p + 1) * PAIR)
            q = q_ref[:, sl]
            qs_ref[:, sl] = (q * _pair_rms(q, first) * qn_ref[...] * ATTN_SCALE).astype(BF16)
            k = k_ref[:, sl]
            kh_ref[:, sl] = (k * _pair_rms(k, first) * kn_ref[...]).astype(BF16)
        vb_ref[...] = v_ref[...].astype(BF16)

        xg = fl_ref[...] + bf_ref[...]
        valid = lax.broadcasted_iota(jnp.int32, (tm, V7X_LANES), 1) < N_HEADS
        ls = jnp.where(valid, jnp.minimum(xg, 0.0) - jnp.log(1.0 + jnp.exp(-jnp.abs(xg))), 0.0)
        tri = (lax.broadcasted_iota(jnp.int32, (tm, tm), 1) <= lax.broadcasted_iota(jnp.int32, (tm, tm), 0)).astype(BF16)
        cs = jnp.zeros((tm, V7X_LANES), F32)
        for part in _split3(ls):
            cs = cs + jnp.dot(tri, part, preferred_element_type=F32)

        @pl.when(i == 0)
        def _():
            carry[...] = jnp.zeros_like(carry)

        fv = cs + carry[0:1, :]
        carry[0:1, :] = fv[tm - 1:tm, :]
        ft_ref[...] = fv.T[0:N_HEADS, :]
        xt_ref[...] = xg.T[0:N_HEADS, :]
        for p in range(N_PAIRS):
            fb_ref[:, p * PAIR:(p + 1) * PAIR] = jnp.where(first, fv[:, 2 * p:2 * p + 1], fv[:, 2 * p + 1:2 * p + 2])

    wide = lambda col: pl.BlockSpec((tm, D_ATTN), lambda i: (i, col))
    tcol = pl.BlockSpec((N_HEADS, tm), lambda i: (0, i))
    return _call(body, name=name, grid=(s // tm,),
                 in_specs=[wide(col0), wide(col0 + 1), wide(col0 + 2), _rows(tm, V7X_LANES),
                           _const((1, V7X_LANES)), _const((1, PAIR)), _const((1, PAIR))],
                 out_specs=[wide(0), wide(0), wide(0), wide(0), tcol, tcol],
                 out_shape=[SDS((s, D_ATTN), BF16)] * 3 + [SDS((s, D_ATTN), F32), SDS((N_HEADS, s), F32),
                                                          SDS((N_HEADS, s), F32)],
                 scratch=[pltpu.VMEM((8, V7X_LANES), F32)], dims=("arbitrary",))(z, z, z, flog, bf, qn2, kn2)


ATTN_FWD_SUB = 256
ATTN_BWD_SUB = 512


def _causal_schedule(nq, key_major):
    if key_major:
        pairs = [(i, j) for j in range(nq) for i in range(j, nq)]
    else:
        pairs = [(i, j) for i in range(nq) for j in range(i + 1)]
    return (jnp.asarray([p[0] for p in pairs], jnp.int32), jnp.asarray([p[1] for p in pairs], jnp.int32))


def _sub_scores(qp, kp, ft_row, mine, r, masked, sub, tk):
    qm = jnp.where(mine, qp, jnp.zeros_like(qp))
    s2 = lax.dot_general(qm, kp, NT, preferred_element_type=F32) - ft_row
    if masked:
        row = r * sub + lax.broadcasted_iota(jnp.int32, (sub, tk), 0)
        s2 = jnp.where(lax.broadcasted_iota(jnp.int32, (sub, tk), 1) <= row, s2, NEG_BIG)
    return s2


def _attn_fwd(qs, kh, vb, fb, ft, name, gather_src=None):
    s = qs.shape[0]
    tq = tk = _tile(s)
    nq = s // tq
    sub = min(ATTN_FWD_SUB, tq)
    ii, jj = _causal_schedule(nq, key_major=False)
    n_steps = ii.shape[0]
    gathers = gather_src is not None

    def body(ii_ref, jj_ref, q_ref, k_ref, v_ref, fq_ref, ft_ref, *rest):
        if gathers:
            x_hbm, o_ref, lse_ref, wall_ref, m_s, l_s, acc_s, send_sems, recv_sems, local_sem = rest
        else:
            o_ref, lse_ref, m_s, l_s, acc_s = rest
        p, t = pl.program_id(0), pl.program_id(1)
        i, j = ii_ref[t], jj_ref[t]
        first = _lane_is_first_head(sub)
        if gathers:
            start, forward, finish = _gather_phases(x_hbm, wall_ref, send_sems, recv_sems, local_sem)
            pl.when(jnp.logical_and(p == 0, t == 0))(start)
            pl.when(jnp.logical_and(p == N_PAIRS - 1, t == 0))(forward)

        @pl.when(j == 0)
        def _():
            m_s[...] = jnp.full_like(m_s, NEG_BIG)
            l_s[...] = jnp.zeros_like(l_s)
            acc_s[...] = jnp.zeros_like(acc_s)

        def tile(masked):
            kp, vp = k_ref[...], v_ref[...]
            q_all, fq_all, acc_all = q_ref[...], fq_ref[...], acc_s[...]
            m_all, l_all = (m_s[0], m_s[1]), (l_s[0], l_s[1])
            ft_rows = [ft_ref[pl.ds(2 * p + h, 1), :] for h in range(2)]
            m_out, l_out, acc_out = ([], []), ([], []), []
            for r in range(tq // sub):
                rows = slice(r * sub, (r + 1) * sub)
                qp, fq, acc = q_all[rows, :], fq_all[rows, :], acc_all[rows, :]
                new = []
                for h in range(2):
                    mine = first if h == 0 else jnp.logical_not(first)
                    s2 = _sub_scores(qp, kp, ft_rows[h], mine, r, masked, sub, tk)
                    fqh = fq[:, h * HEAD_DIM:h * HEAD_DIM + 1]
                    m_old = m_all[h][rows, :]
                    m_new = jnp.maximum(m_old, jnp.max(s2, axis=-1, keepdims=True) + fqh)
                    pr = jnp.exp(s2 - (m_new - fqh))
                    alpha = jnp.exp(m_old - m_new)
                    l_out[h].append(alpha * l_all[h][rows, :] + jnp.sum(pr, axis=-1, keepdims=True))
                    m_out[h].append(m_new)
                    new.append(alpha * acc + jnp.dot(pr.astype(BF16), vp, preferred_element_type=F32))
                acc_out.append(jnp.where(first, new[0], new[1]))
            for h in range(2):
                m_s[h] = jnp.concatenate(m_out[h], axis=0)
                l_s[h] = jnp.concatenate(l_out[h], axis=0)
            acc_s[...] = jnp.concatenate(acc_out, axis=0)

        @pl.when(j < i)
        def _():
            tile(False)

        @pl.when(j == i)
        def _():
            tile(True)
            whole = _lane_is_first_head(tq)
            l_pair = jnp.where(whole, l_s[0], l_s[1])
            o_ref[...] = acc_s[...] / l_pair
            lse_ref[...] = jnp.where(whole, m_s[0], m_s[1]) + jnp.log(l_pair)

        if gathers:
            pl.when(jnp.logical_and(p == N_PAIRS - 1, t == n_steps - 1))(finish)

    qblk = pl.BlockSpec((tq, PAIR), lambda p, t, ii_r, jj_r: (ii_r[t], p))
    kblk = pl.BlockSpec((tk, PAIR), lambda p, t, ii_r, jj_r: (jj_r[t], p))
    in_specs = [qblk, kblk, kblk, qblk, pl.BlockSpec((N_HEADS, tk), lambda p, t, ii_r, jj_r: (0, jj_r[t]))]
    out_specs, out_shape = [qblk, qblk], [SDS((s, D_ATTN), F32)] * 2
    scratch = [pltpu.VMEM((2, tq, 1), F32), pltpu.VMEM((2, tq, 1), F32), pltpu.VMEM((tq, PAIR), F32)]
    args = [ii, jj, qs, kh, vb, fb, ft]
    if gathers:
        in_specs.append(ANY)
        out_specs.append(ANY)
        out_shape.append(SDS((N_DEV,) + gather_src.shape, gather_src.dtype))
        scratch += GATHER_SEMS
        args.append(gather_src)
    grid_spec = pltpu.PrefetchScalarGridSpec(num_scalar_prefetch=2, grid=(N_PAIRS, n_steps), in_specs=in_specs,
                                             out_specs=out_specs, scratch_shapes=scratch)
    return pl.pallas_call(
        body, name=name, grid_spec=grid_spec, out_shape=_in_hbm(out_shape),
        compiler_params=pltpu.CompilerParams(dimension_semantics=("arbitrary", "arbitrary"),
                                             vmem_limit_bytes=32 * MIB))(*[_keep_in_hbm(a) for a in args])


def _attn_bwd(qs, kh, vb, fb, ft, lse, o, dao, name, exchange_src=None):
    s = qs.shape[0]
    tq = tk = _tile(s)
    nq = s // tq
    sub = min(ATTN_BWD_SUB, tq)
    ii, jj = _causal_schedule(nq, key_major=True)
    n_steps = ii.shape[0]

    exchanges = exchange_src is not None

    def body(ii_ref, jj_ref, q_ref, k_ref, v_ref, fq_ref, ft_ref, lse_ref, o_ref, do_ref, *rest):
        if exchanges:
            t_hbm, dq_ref, rs_ref, dk_ref, dv_ref, df_ref, rcv_ref, dk_s, dv_s, df_s, send_sems, recv_sems = rest
        else:
            dq_ref, rs_ref, dk_ref, dv_ref, df_ref, dk_s, dv_s, df_s = rest
        p, t = pl.program_id(0), pl.program_id(1)
        i, j = ii_ref[t], jj_ref[t]
        first = _lane_is_first_head(sub)
        first_k = _lane_is_first_head(tk)
        if exchanges:
            start, finish = _chip_exchange_phases(t_hbm, rcv_ref, send_sems, recv_sems)
            pl.when(jnp.logical_and(p == 0, t == 0))(start)

        @pl.when(t == 0)
        def _():
            dq_ref[...] = jnp.zeros_like(dq_ref)
            rs_ref[...] = jnp.zeros_like(rs_ref)

        @pl.when(i == j)
        def _():
            dk_s[...] = jnp.zeros_like(dk_s)
            dv_s[...] = jnp.zeros_like(dv_s)
            df_s[...] = jnp.zeros_like(df_s)

        def tile(masked):
            kp, vp = k_ref[...], v_ref[...]
            q_all, fq_all, lse_all, o_all, do_all = q_ref[...], fq_ref[...], lse_ref[...], o_ref[...], do_ref[...]
            ft_rows = [ft_ref[pl.ds(2 * p + h, 1), :] for h in range(2)]
            dq_out, rs_out = [], []
            dk_acc, dv_acc = jnp.zeros((tk, PAIR), F32), jnp.zeros((tk, PAIR), F32)
            df_acc = [jnp.zeros((1, tk), F32), jnp.zeros((1, tk), F32)]
            for r in range(tq // sub):
                rows = slice(r * sub, (r + 1) * sub)
                qp, fq, lse, ov, dall = q_all[rows, :], fq_all[rows, :], lse_all[rows, :], o_all[rows, :], do_all[rows, :]
                dq_h, dk_h, dv_h, rs_h = [], [], [], []
                for h in range(2):
                    mine = first if h == 0 else jnp.logical_not(first)
                    s2 = _sub_scores(qp, kp, ft_rows[h], mine, r, masked, sub, tk)
                    lane = slice(h * HEAD_DIM, h * HEAD_DIM + 1)
                    pr = jnp.exp(s2 - (lse[:, lane] - fq[:, lane]))
                    dov = jnp.where(mine, dall, 0.0)
                    dsum = jnp.sum(dov * ov, axis=-1, keepdims=True)
                    dom = dov.astype(BF16)
                    dom_lo = (dov - dom.astype(F32)).astype(BF16)
                    dp = lax.dot_general(dom, vp, NT, preferred_element_type=F32)
                    dp = dp + lax.dot_general(dom_lo, vp, NT, preferred_element_type=F32)
                    ds = pr * (dp - dsum)
                    dsb = ds.astype(BF16)
                    dq_h.append(jnp.dot(dsb, kp, preferred_element_type=F32))
                    dk_h.append(lax.dot_general(dsb, qp, TN, preferred_element_type=F32))
                    dv_h.append(lax.dot_general(pr.astype(BF16), dom, TN, preferred_element_type=F32))
                    rs_h.append(jnp.sum(ds, axis=-1, keepdims=True))
                    df_acc[h] = df_acc[h] - jnp.sum(ds, axis=0, keepdims=True)
                dq_out.append(jnp.where(first, dq_h[0], dq_h[1]))
                rs_out.append(jnp.where(first, rs_h[0], rs_h[1]))
                dk_acc = dk_acc + jnp.where(first_k, dk_h[0], dk_h[1])
                dv_acc = dv_acc + jnp.where(first_k, dv_h[0], dv_h[1])
            grows = pl.ds(pl.multiple_of(i * tq, tq), tq)
            dq_ref[grows, :] += jnp.concatenate(dq_out, axis=0)
            rs_ref[grows, :] += jnp.concatenate(rs_out, axis=0)
            dk_s[...] += dk_acc
            dv_s[...] += dv_acc
            for h in range(2):
                df_s[h:h + 1, :] += df_acc[h]

        @pl.when(j < i)
        def _():
            tile(False)

        @pl.when(j == i)
        def _():
            tile(True)

        @pl.when(i == nq - 1)
        def _():
            dk_ref[...] = dk_s[...]
            dv_ref[...] = dv_s[...]
            df_ref[0] = df_s[...]

        if exchanges:
            pl.when(jnp.logical_and(p == N_PAIRS - 1, t == n_steps - 1))(finish)

    qblk = pl.BlockSpec((tq, PAIR), lambda p, t, ii_r, jj_r: (ii_r[t], p))
    kblk = pl.BlockSpec((tk, PAIR), lambda p, t, ii_r, jj_r: (jj_r[t], p))
    doblk = pl.BlockSpec((tq, PAIR), lambda p, t, ii_r, jj_r: (ii_r[t], N_PAIRS + p))
    whole = pl.BlockSpec((s, PAIR), lambda p, t, ii_r, jj_r: (0, p))
    in_specs = [qblk, kblk, kblk, qblk, pl.BlockSpec((N_HEADS, tk), lambda p, t, ii_r, jj_r: (0, jj_r[t])),
                qblk, qblk, doblk]
    out_specs = [whole, whole, kblk, kblk, pl.BlockSpec((1, 8, tk), lambda p, t, ii_r, jj_r: (p, 0, jj_r[t]))]
    out_shape = [SDS((s, D_ATTN), F32)] * 4 + [SDS((N_PAIRS, 8, s), F32)]
    scratch = [pltpu.VMEM((tk, PAIR), F32), pltpu.VMEM((tk, PAIR), F32), pltpu.VMEM((8, tk), F32)]
    args = [ii, jj, qs, kh, vb, fb, ft, lse, o, dao]
    if exchanges:
        in_specs.append(ANY)
        out_specs.append(ANY)
        out_shape.append(SDS((3,) + exchange_src.shape[1:], exchange_src.dtype))
        scratch += EXCHANGE_SEMS
        args.append(exchange_src)
    grid_spec = pltpu.PrefetchScalarGridSpec(num_scalar_prefetch=2, grid=(N_PAIRS, n_steps), in_specs=in_specs,
                                             out_specs=out_specs, scratch_shapes=scratch)
    return pl.pallas_call(
        body, name=name, grid_spec=grid_spec, out_shape=_in_hbm(out_shape),
        compiler_params=pltpu.CompilerParams(dimension_semantics=("arbitrary", "arbitrary"),
                                             vmem_limit_bytes=40 * MIB))(*[_keep_in_hbm(a) for a in args])


def _qk_bwd(z, dqs, dkh, dv, qn2, kn2, name):
    s = z.shape[0]
    tm = _tile(s)
    col0 = (z.shape[1] - 3 * D_ATTN) // D_ATTN

    def body(q_ref, k_ref, dqs_ref, dkh_ref, dv_ref, qn_ref, kn_ref, dq_ref, dk_ref, dvb_ref, dqn_ref, dkn_ref):
        first = _lane_is_first_head(tm)

        @pl.when(pl.program_id(0) == 0)
        def _():
            dqn_ref[...] = jnp.zeros_like(dqn_ref)
            dkn_ref[...] = jnp.zeros_like(dkn_ref)

        def through(x_ref, dy_ref, gain_ref, dx_ref, dgain_ref, scale):
            for p in range(N_PAIRS):
                sl = slice(p * PAIR, (p + 1) * PAIR)
                xv = x_ref[:, sl]
                r = _pair_rms(xv, first)
                xh = xv * r
                dy = dy_ref[:, sl] * scale
                dgain_ref[:, sl] += jnp.sum(dy * xh, axis=0, keepdims=True)
                dxh = dy * gain_ref[...]
                t = dxh * xh
                m0 = jnp.sum(jnp.where(first, t, 0.0), axis=-1, keepdims=True)
                m1 = jnp.sum(jnp.where(first, 0.0, t), axis=-1, keepdims=True)
                mean = jnp.where(first, m0, m1) / HEAD_DIM
                dx_ref[:, sl] = (r * (dxh - xh * mean)).astype(BF16)

        through(q_ref, dqs_ref, qn_ref, dq_ref, dqn_ref, ATTN_SCALE)
        through(k_ref, dkh_ref, kn_ref, dk_ref, dkn_ref, 1.0)
        dvb_ref[...] = dv_ref[...].astype(BF16)

    wide = lambda col: pl.BlockSpec((tm, D_ATTN), lambda i: (i, col))
    return _call(body, name=name, grid=(s // tm,),
                 in_specs=[wide(col0), wide(col0 + 1), wide(0), wide(0), wide(0), _const((1, PAIR)), _const((1, PAIR))],
                 out_specs=[wide(0), wide(0), wide(0), _const((1, D_ATTN)), _const((1, D_ATTN))],
                 out_shape=[SDS((s, D_ATTN), BF16)] * 3 + [SDS((1, D_ATTN), F32)] * 2,
                 dims=("arbitrary",))(z, z, dqs, dkh, dv, qn2, kn2)


def _gate_bwd(dft, xt, name):
    s = xt.shape[1]
    tm = _tile(s)
    n = s // tm

    def body(df_ref, xt_ref, dxt_ref, dx_ref, db_ref, carry):
        i = pl.program_id(0)

        @pl.when(i == 0)
        def _():
            carry[...] = jnp.zeros_like(carry)
            db_ref[...] = jnp.zeros_like(db_ref)

        tri = (lax.broadcasted_iota(jnp.int32, (tm, tm), 0) >= lax.broadcasted_iota(jnp.int32, (tm, tm), 1)).astype(BF16)
        rc = jnp.zeros((N_HEADS, tm), F32)
        for part in _split3(df_ref[...]):
            rc = rc + jnp.dot(part, tri, preferred_element_type=F32)
        dls = rc + carry[:, 0:1]
        carry[...] = jnp.broadcast_to(dls[:, 0:1], carry.shape)
        dxt = dls * _sigmoid(-xt_ref[...])
        dxt_ref[...] = dxt
        db_ref[...] += jnp.broadcast_to(jnp.sum(dxt, axis=-1, keepdims=True), db_ref.shape)
        padded = jnp.concatenate([dxt, jnp.zeros((V7X_LANES - N_HEADS, tm), F32)], axis=0)
        dx_ref[...] = padded.T

    rev = pl.BlockSpec((N_HEADS, tm), lambda i: (0, n - 1 - i))
    return _call(body, name=name, grid=(n,), in_specs=[rev, rev],
                 out_specs=[rev, pl.BlockSpec((tm, V7X_LANES), lambda i: (n - 1 - i, 0)), _const((N_HEADS, V7X_LANES))],
                 out_shape=[SDS((N_HEADS, s), F32), SDS((s, V7X_LANES), F32), SDS((N_HEADS, V7X_LANES), F32)],
                 scratch=[pltpu.VMEM((N_HEADS, V7X_LANES), F32)], dims=("arbitrary",))(dft, xt)


def _conv_c_fwd(z, cw, name):
    s = z.shape[0]
    c = z.shape[1] // 3
    tm, halo, kw = _tile(s), CONV_C_HALO, CONV_C_WIDTH

    def body(gb_ref, gc_ref, hh_ref, gcp_ref, hhp_ref, cw_ref, y_ref, buf):
        i = pl.program_id(0)
        buf[0:halo, :] = jnp.where(i > 0, gcp_ref[...] * hhp_ref[...], 0.0)
        buf[halo:halo + tm, :] = gc_ref[...] * hh_ref[...]
        c1 = jnp.zeros((tm, c), F32)
        for k in range(kw):
            c1 = c1 + cw_ref[k:k + 1, :] * buf[pl.ds(halo - (kw - 1) + k, tm), :]
        y_ref[...] = (gb_ref[...] * c1).astype(BF16)

    return _call(body, name=name, grid=(s // tm,),
                 in_specs=[_rows(tm, c, 0), _rows(tm, c, 1), _rows(tm, c, 2), _prev_rows(halo, tm, c, 1),
                           _prev_rows(halo, tm, c, 2), _const(cw.shape)],
                 out_specs=_rows(tm, c), out_shape=SDS((s, c), BF16),
                 scratch=[pltpu.VMEM((tm + halo, c), F32)], dims=("parallel",))(z, z, z, z, z, cw)


def _conv_c_bwd(dy0, z, cw, name):
    s = z.shape[0]
    c = z.shape[1] // 3
    tm, halo, kw = _tile(s), CONV_C_HALO, CONV_C_WIDTH
    n = s // tm

    def body(dy_ref, dyn_ref, gb_ref, gbn_ref, gc_ref, hh_ref, gcp_ref, hhp_ref, cw_ref, dz_ref, dcw_ref, buf, bd):
        i = pl.program_id(0)
        gcv, hhv, dyv = gc_ref[...], hh_ref[...], dy_ref[...]
        buf[0:halo, :] = jnp.where(i > 0, gcp_ref[...] * hhp_ref[...], 0.0)
        buf[halo:halo + tm, :] = gcv * hhv
        dc1 = dyv * gb_ref[...]
        bd[0:tm, :] = dc1
        bd[tm:tm + halo, :] = jnp.where(i < n - 1, dyn_ref[...] * gbn_ref[...], 0.0)

        @pl.when(i == 0)
        def _():
            dcw_ref[...] = jnp.zeros_like(dcw_ref)

        c1 = jnp.zeros((tm, c), F32)
        dc0 = jnp.zeros((tm, c), F32)
        for k in range(kw):
            shifted = buf[pl.ds(halo - (kw - 1) + k, tm), :]
            c1 = c1 + cw_ref[k:k + 1, :] * shifted
            dc0 = dc0 + cw_ref[k:k + 1, :] * bd[pl.ds(kw - 1 - k, tm), :]
            dcw_ref[k:k + 1, :] += jnp.sum(dc1 * shifted, axis=0, keepdims=True)
        dz_ref[:, 0:c] = (dyv * c1).astype(BF16)
        dz_ref[:, c:2 * c] = (dc0 * hhv).astype(BF16)
        dz_ref[:, 2 * c:3 * c] = (dc0 * gcv).astype(BF16)

    return _call(body, name=name, grid=(n,),
                 in_specs=[_rows(tm, c), _next_rows(halo, tm, c, 0, s // halo), _rows(tm, c, 0),
                           _next_rows(halo, tm, c, 0, s // halo), _rows(tm, c, 1), _rows(tm, c, 2),
                           _prev_rows(halo, tm, c, 1), _prev_rows(halo, tm, c, 2), _const(cw.shape)],
                 out_specs=[_rows(tm, 3 * c), _const(cw.shape)],
                 out_shape=[SDS((s, 3 * c), BF16), SDS(cw.shape, F32)],
                 scratch=[pltpu.VMEM((tm + halo, c), F32)] * 2, dims=("arbitrary",),
                 vmem_mb=48)(dy0, dy0, z, z, z, z, z, z, cw)


def _loss_head(y, target, name):
    s, d = y.shape
    tm = _tile(s)

    def body(y_ref, t_ref, loss_ref, dy_ref):
        e = y_ref[...] - t_ref[...]

        @pl.when(pl.program_id(0) == 0)
        def _():
            loss_ref[...] = jnp.zeros_like(loss_ref)

        loss_ref[...] += 0.5 * jnp.sum(jnp.mean(e * e, axis=-1, keepdims=True))
        dy_ref[...] = e / d

    return _call(body, name=name, grid=(s // tm,), in_specs=[_rows(tm, d), _rows(tm, d)],
                 out_specs=[_const((8, V7X_LANES)), _rows(tm, d)],
                 out_shape=[SDS((8, V7X_LANES), F32), SDS((s, d), F32)], dims=("arbitrary",))(y, target)


def _adamw(w, g, m, v, name):
    r, c = w.shape
    tr = next((t for t in (512, 256, 128, 64, 32, 16, 8) if r % t == 0), r)

    def body(w_ref, g_ref, m_ref, v_ref, d_ref, mo_ref, vo_ref):
        gv = g_ref[...]
        mn = ADAM_B1 * m_ref[...] + (1.0 - ADAM_B1) * gv
        vn = ADAM_B2 * v_ref[...] + (1.0 - ADAM_B2) * (gv * gv)
        m_hat = mn / (1.0 - ADAM_B1 ** ADAM_STEP)
        v_hat = vn / (1.0 - ADAM_B2 ** ADAM_STEP)
        d_ref[...] = -ADAM_LR * (m_hat / (jnp.sqrt(v_hat) + ADAM_EPS) + ADAM_WD * w_ref[...])
        mo_ref[...] = mn
        vo_ref[...] = vn

    spec = _rows(tr, c)
    return _call(body, name=name, grid=(r // tr,), in_specs=[spec] * 4, out_specs=[spec] * 3,
                 out_shape=[SDS((r, c), F32)] * 3, dims=("parallel",))(w, g, m, v)


def _position():
    return lax.axis_index("x"), lax.axis_index("y"), lax.axis_index("c")


def _other_chips(x, y):
    return [(1 - x, y), (x, 1 - y), (1 - x, 1 - y)]


def _dev_index(px, py, pc):
    return 4 * px + 2 * py + pc


def _all_gather(wloc):
    r, d = wloc.shape

    def body(x_ref, out_ref, send_sems, recv_sems, local_sem):
        start, forward, finish = _gather_phases(x_ref, out_ref, send_sems, recv_sems, local_sem)
        start()
        forward()
        finish()

    return _call(body, name="all_gather_weights", in_specs=[ANY], out_specs=ANY,
                 out_shape=SDS((N_DEV, r, d), wloc.dtype), scratch=GATHER_SEMS)(wloc)


GATHER_SEMS = [pltpu.SemaphoreType.DMA((7,)), pltpu.SemaphoreType.DMA((7,)), pltpu.SemaphoreType.DMA((1,))]


def _gather_phases(x_ref, out_ref, send_sems, recv_sems, local_sem):
    x, y, c = _position()
    me, sibling = (x, y, c), (x, y, 1 - c)
    chips = _other_chips(x, y)

    def slot(dev):
        return out_ref.at[_dev_index(*dev)]

    def copy(k, block, to, src=None):
        return pltpu.make_async_remote_copy(
            src_ref=slot(block) if src is None else src, dst_ref=slot(block),
            send_sem=send_sems.at[k], recv_sem=recv_sems.at[k], device_id=to, device_id_type=MESH)

    mine = pltpu.make_async_copy(x_ref, slot(me), local_sem.at[0])
    first = [copy(0, me, sibling, src=x_ref)] + [copy(1 + j, me, (*chip, c), src=x_ref) for j, chip in enumerate(chips)]
    passed = [copy(4 + j, (*chip, c), sibling) for j, chip in enumerate(chips)]

    def start():
        mine.start()
        for cp in first:
            cp.start()

    def forward():
        for j, chip in enumerate(chips):
            copy(1 + j, (*chip, c), me).wait_recv()
            passed[j].start()

    def finish():
        copy(0, sibling, me).wait_recv()
        for j, chip in enumerate(chips):
            copy(4 + j, (*chip, 1 - c), me).wait_recv()
        for cp in first + passed:
            cp.wait_send()
        mine.wait()

    return start, forward, finish


def _row_block(r):
    return next(t for t in range(704, 0, -BF16_ROWS) if r % t == 0)


def _pair_exchange(gall, name):
    _, r, d = gall.shape

    def body(g_ref, out_ref, send_sems, recv_sems):
        x, y, c = _position()
        sibling = (x, y, 1 - c)
        dests = [sibling] + [(*chip, 1 - c) for chip in _other_chips(x, y)]
        copies = [pltpu.make_async_remote_copy(
            src_ref=g_ref.at[_dev_index(*dest)], dst_ref=out_ref.at[k], send_sem=send_sems.at[k],
            recv_sem=recv_sems.at[k], device_id=sibling, device_id_type=MESH) for k, dest in enumerate(dests)]
        for cp in copies:
            cp.start()
        for cp in copies:
            cp.wait()

    return _call(body, name=name, in_specs=[ANY], out_specs=ANY,
                 out_shape=SDS((4, r, d), gall.dtype),
                 scratch=[pltpu.SemaphoreType.DMA((4,)), pltpu.SemaphoreType.DMA((4,))])(gall)


def _pair_sum(gall, sib, idx, name):
    _, r, d = gall.shape
    tr = _row_block(r)

    def body(idx_ref, a_ref, b_ref, o_ref):
        o_ref[...] = (a_ref[...].astype(F32) + b_ref[...].astype(F32)).astype(o_ref.dtype)

    grid_spec = pltpu.PrefetchScalarGridSpec(
        num_scalar_prefetch=1, grid=(4, r // tr),
        in_specs=[pl.BlockSpec((1, tr, d), lambda k, i, idx_ref: (idx_ref[k], i, 0)),
                  pl.BlockSpec((1, tr, d), lambda k, i, idx_ref: (k, i, 0))],
        out_specs=pl.BlockSpec((1, tr, d), lambda k, i, idx_ref: (k, i, 0)))
    return pl.pallas_call(body, name=name, grid_spec=grid_spec,
                          out_shape=_in_hbm(SDS((4, r, d), gall.dtype)),
                          compiler_params=pltpu.CompilerParams(dimension_semantics=("parallel", "parallel")))(
        idx, _keep_in_hbm(gall), _keep_in_hbm(sib))


def _chip_exchange(tsum):
    _, r, d = tsum.shape

    def body(t_ref, out_ref, send_sems, recv_sems):
        start, finish = _chip_exchange_phases(t_ref, out_ref, send_sems, recv_sems)
        start()
        finish()

    return _call(body, name="reduce_scatter_chip_exchange", in_specs=[ANY], out_specs=ANY,
                 out_shape=SDS((3, r, d), tsum.dtype), scratch=EXCHANGE_SEMS)(tsum)


EXCHANGE_SEMS = [pltpu.SemaphoreType.DMA((3,)), pltpu.SemaphoreType.DMA((3,))]


def _chip_exchange_phases(t_ref, out_ref, send_sems, recv_sems):
    x, y, c = _position()
    copies = [pltpu.make_async_remote_copy(
        src_ref=t_ref.at[1 + k], dst_ref=out_ref.at[k], send_sem=send_sems.at[k], recv_sem=recv_sems.at[k],
        device_id=(*chip, c), device_id_type=MESH) for k, chip in enumerate(_other_chips(x, y))]

    def start():
        for cp in copies:
            cp.start()

    def finish():
        for cp in copies:
            cp.wait()

    return start, finish


def _final_sum(tsum, rcv, name):
    _, r, d = tsum.shape
    tr = _row_block(r)

    def body(t_ref, r_ref, o_ref):
        acc = t_ref[0].astype(F32)
        for k in range(3):
            acc = acc + r_ref[k].astype(F32)
        o_ref[...] = acc

    return _call(body, name=name, grid=(r // tr,),
                 in_specs=[pl.BlockSpec((1, tr, d), lambda i: (0, i, 0)), pl.BlockSpec((3, tr, d), lambda i: (0, i, 0))],
                 out_specs=_rows(tr, d), out_shape=SDS((r, d), F32), dims=("parallel",))(tsum, rcv)


def _all_reduce_small(buf):
    nr, lanes = buf.shape

    def body(b_ref, out_ref, gath, send_sems, recv_sems):
        x, y, c = _position()
        my_slot = _dev_index(x, y, c)
        gath[my_slot] = b_ref[...]
        copies = []
        for k in range(1, N_DEV):
            dx, dy, dc = (k >> 2) & 1, (k >> 1) & 1, k & 1
            peer = (1 - x if dx else x, 1 - y if dy else y, 1 - c if dc else c)
            copies.append(pltpu.make_async_remote_copy(
                src_ref=b_ref, dst_ref=gath.at[my_slot], send_sem=send_sems.at[k - 1], recv_sem=recv_sems.at[k - 1],
                device_id=peer, device_id_type=MESH))
        for cp in copies:
            cp.start()
        for cp in copies:
            cp.wait()
        acc = gath[0]
        for sidx in range(1, N_DEV):
            acc = acc + gath[sidx]
        out_ref[...] = acc

    return _call(body, name="all_reduce_small", in_specs=[VMEM], out_specs=VMEM, out_shape=SDS((nr, lanes), F32),
                 scratch=[pltpu.VMEM((N_DEV, nr, lanes), F32), pltpu.SemaphoreType.DMA((7,)),
                          pltpu.SemaphoreType.DMA((7,))])(buf)


def _ffn_block_fwd(x, gain, wall, offs, fs, tag, gather_src=None):
    xn = _rmsnorm_fwd(x, gain, f"{tag}_norm")
    res = _ffn_fwd(x, xn, wall, offs, fs, f"{tag}_fwd", gather_src)
    out, g, u, h = res[:4]
    return out, (x, gain, xn, g, u, h), (res[4] if gather_src is not None else None)


def _ffn_block_bwd(dout, saved, wall, offs, fs, tag, exchange_src=None):
    x, gain, xn, g, u, h = saved
    res = _ffn_bwd_act(dout, g, u, wall, offs, fs, f"{tag}_bwd_act", exchange_src)
    dg, du, dy_b, dxn = res[:4]
    dwg = _mm_tn(dg, xn, f"{tag}_dwg", BF16)
    dwu = _mm_tn(du, xn, f"{tag}_dwu", BF16)
    dwd = _mm_tn(h, dy_b, f"{tag}_dwd", BF16)
    dx, dgain = _rmsnorm_bwd(x, gain, dxn, dout, f"{tag}_norm_bwd")
    return dx, (dwg, dwu, dwd), dgain, (res[4] if exchange_src is not None else None)


def _local_step(x, target, wall_a, fs, small, plan):
    grads = {}
    first, second = (0, fs, 2 * fs), (3 * fs, 4 * fs, 5 * fs)

    x1, s_f1a, wall_b = _ffn_block_fwd(x, small["ffn1_norm"][0], wall_a, first, fs, "l0_ffn1", plan.get("shard_b"))
    wall_b = plan.get("wall_b", wall_b)
    mixw = plan["mix_b"](wall_b)
    hn0 = _rmsnorm_fwd(x1, small["mix_norm"][0], "l0_mix_norm")
    z = _mm(hn0, mixw["ev_w_main_t"], "nt", "ev_in_proj")
    flog = _mm(hn0, mixw["ev_w_f_t"], "nt", "ev_in_proj_gate")
    a, a1 = _conv_a_fwd(z, small["ev_conv_w32"], small["ev_conv_b"], small["ev_conv_norm"], "ev_conv_fwd")
    qs, kh, vb, fb, ft, xt = _qk_fwd(z, flog, small["ev_b_f128"], small["ev_q_norm2"], small["ev_k_norm2"], "ev_qk_fwd")
    if "wall_c" in plan:
        o, lse = _attn_fwd(qs, kh, vb, fb, ft, "ev_attn_fwd")
        wall_c = plan["wall_c"]
    else:
        o, lse, wall_c = _attn_fwd(qs, kh, vb, fb, ft, "ev_attn_fwd", gather_src=plan["shard_c"])
    mixw = {**mixw, **plan["mix_c"](wall_c)}
    ao = jnp.concatenate([a, o.astype(BF16)], axis=1)
    x2 = _mm(ao, mixw["ev_w_out"], "nn", "ev_out_proj", add=x1)
    x3, s_f2a, _ = _ffn_block_fwd(x2, small["ffn2_norm"][0], wall_b, first, fs, "l0_ffn2")

    x4, s_f1b, _ = _ffn_block_fwd(x3, small["ffn1_norm"][1], wall_c, first, fs, "l1_ffn1")
    hn1 = _rmsnorm_fwd(x4, small["mix_norm"][1], "l1_mix_norm")
    zo = _mm(hn1, mixw["od_w_in_t"], "nt", "od_in_proj")
    y0 = _conv_c_fwd(zo, small["od_conv_w8"], "od_conv_fwd")
    x5 = _mm(y0, mixw["od_w_out"], "nn", "od_out_proj", add=x4)
    x6, s_f2b, _ = _ffn_block_fwd(x5, small["ffn2_norm"][1], wall_c, second, fs, "l1_ffn2")

    loss, d6 = _loss_head(x6, target, "loss_head")

    d5, grads["l1_ffn2"], grads["ffn2_norm_1"], _ = _ffn_block_bwd(d6, s_f2b, wall_c, second, fs, "l1_ffn2")
    d5b = d5.astype(BF16)
    dy0 = _mm(d5b, mixw["od_w_out"], "nt", "od_out_proj_bwd")
    grads["od_w_out"] = _mm_tn(y0, d5b, "od_dw_out", BF16)
    dzo, grads["od_conv_w"] = _conv_c_bwd(dy0, zo, small["od_conv_w8"], "od_conv_bwd")
    dh1 = _mm(dzo, mixw["od_w_in_t"], "nn", "od_in_proj_bwd")
    grads["od_w_in_t"] = _mm_tn(dzo, hn1, "od_dw_in", BF16)
    d4, grads["mix_norm_1"] = _rmsnorm_bwd(x4, small["mix_norm"][1], dh1, d5, "l1_mix_norm_bwd")
    d3, grads["l1_ffn1"], grads["ffn1_norm_1"], _ = _ffn_block_bwd(d4, s_f1b, wall_c, first, fs, "l1_ffn1")
    sums_c = plan["reduce_c"](grads) if "reduce_c" in plan else None

    d2, grads["l0_ffn2"], grads["ffn2_norm_0"], _ = _ffn_block_bwd(d3, s_f2a, wall_b, first, fs, "l0_ffn2")
    d2b = d2.astype(BF16)
    dao = _mm(d2b, mixw["ev_w_out"], "nt", "ev_out_proj_bwd")
    grads["ev_w_out"] = _mm_tn(ao, d2b, "ev_dw_out", BF16)
    da1, grads["ev_conv_norm"], grads["ev_conv_b"] = _conv_a_bwd_norm(dao, a1, small["ev_conv_norm"], "ev_conv_bwd_norm")
    du, dg, grads["ev_conv_w"] = _conv_a_bwd_conv(da1, z, small["ev_conv_w32"], "ev_conv_bwd_conv")
    res = _attn_bwd(qs, kh, vb, fb, ft, lse, o, dao, "ev_attn_bwd", exchange_src=sums_c)
    dqs, rs, dkh, dv, df4 = res[:5]
    if sums_c is not None:
        grads["pair_sums_c"], grads["exchanged_c"] = sums_c, res[5]
    dq, dk, dvb, grads["ev_q_norm"], grads["ev_k_norm"] = _qk_bwd(
        z, dqs, dkh, dv, small["ev_q_norm2"], small["ev_k_norm2"], "ev_qk_bwd")
    dft = df4[:, 0:2, :].reshape(N_HEADS, -1) + rs.reshape(-1, N_HEADS, HEAD_DIM)[:, :, 0].T
    dxt, dflog, grads["ev_b_f"] = _gate_bwd(dft, xt, "ev_gate_bwd")
    dz = jnp.concatenate([du, dg, dq, dk, dvb], axis=1)
    dflog_b = dflog.astype(BF16)
    dh0 = _mm(dz, mixw["ev_w_main_t"], "nn", "ev_in_proj_bwd")
    dh0 = _mm(dflog_b, mixw["ev_w_f_t"], "nn", "ev_in_proj_gate_bwd", add=dh0)
    dw_main = _mm_tn(dz, hn0, "ev_dw_in", BF16)
    dw_f = _mm(dxt.astype(BF16), hn0, "nn", "ev_dw_in_gate", BF16)
    grads["ev_w_in_t"] = jnp.concatenate([dw_main, dw_f], axis=0)
    d1, grads["mix_norm_0"] = _rmsnorm_bwd(x1, small["mix_norm"][0], dh0, d2, "l0_mix_norm_bwd")
    sums_b = plan["reduce_b"](grads) if "reduce_b" in plan else None
    d0, grads["l0_ffn1"], grads["ffn1_norm_0"], exchanged_b = _ffn_block_bwd(d1, s_f1a, wall_a, first, fs, "l0_ffn1",
                                                                               exchange_src=sums_b)
    if sums_b is not None:
        grads["pair_sums_b"], grads["exchanged_b"] = sums_b, exchanged_b
    return loss, d0, grads


def _round_up(n, m):
    return -(-n // m) * m


def _pad_rows(a, rows):
    return jnp.pad(a, ((0, rows - a.shape[0]), (0, 0)))


SMALL_ORDER = ("loss", "ffn1_norm", "mix_norm", "ffn2_norm", "ev_b_f", "ev_conv_b", "ev_conv_norm",
               "ev_q_norm", "ev_k_norm", "ev_conv_w", "od_conv_w")


def _pack_small(parts):
    flat = jnp.concatenate([parts[k].reshape(-1).astype(F32) for k in SMALL_ORDER])
    n = _round_up(flat.shape[0], 8 * V7X_LANES)
    return jnp.pad(flat, (0, n - flat.shape[0])).reshape(-1, V7X_LANES)


def _unpack_small(buf, shapes):
    flat = buf.reshape(-1)
    out, pos = {}, 0
    for k in SMALL_ORDER:
        n = math.prod(shapes[k])
        out[k] = flat[pos:pos + n].reshape(shapes[k])
        pos += n
    return out


def kernel(x, ffn1_norm, ffn1_w_gate, ffn1_w_up, ffn1_w_down, mix_norm, ffn2_norm, ffn2_w_gate, ffn2_w_up, ffn2_w_down, ev_w_in, ev_b_f, ev_conv_w, ev_conv_b, ev_conv_norm, ev_q_norm, ev_k_norm, ev_w_out, od_w_in, od_conv_w, od_w_out, loss_target, m_ffn1_norm, m_ffn1_w_gate, m_ffn1_w_up, m_ffn1_w_down, m_mix_norm, m_ffn2_norm, m_ffn2_w_gate, m_ffn2_w_up, m_ffn2_w_down, m_ev_w_in, m_ev_b_f, m_ev_conv_w, m_ev_conv_b, m_ev_conv_norm, m_ev_q_norm, m_ev_k_norm, m_ev_w_out, m_od_w_in, m_od_conv_w, m_od_w_out, v_ffn1_norm, v_ffn1_w_gate, v_ffn1_w_up, v_ffn1_w_down, v_mix_norm, v_ffn2_norm, v_ffn2_w_gate, v_ffn2_w_up, v_ffn2_w_down, v_ev_w_in, v_ev_b_f, v_ev_conv_w, v_ev_conv_b, v_ev_conv_norm, v_ev_q_norm, v_ev_k_norm, v_ev_w_out, v_od_w_in, v_od_conv_w, v_od_w_out):
    weights = dict(ffn1_norm=ffn1_norm, ffn1_w_gate=ffn1_w_gate, ffn1_w_up=ffn1_w_up, ffn1_w_down=ffn1_w_down,
                   mix_norm=mix_norm, ffn2_norm=ffn2_norm, ffn2_w_gate=ffn2_w_gate, ffn2_w_up=ffn2_w_up,
                   ffn2_w_down=ffn2_w_down, ev_w_in=ev_w_in, ev_b_f=ev_b_f, ev_conv_w=ev_conv_w, ev_conv_b=ev_conv_b,
                   ev_conv_norm=ev_conv_norm, ev_q_norm=ev_q_norm, ev_k_norm=ev_k_norm, ev_w_out=ev_w_out,
                   od_w_in=od_w_in, od_conv_w=od_conv_w, od_w_out=od_w_out)
    m_in = dict(ffn1_norm=m_ffn1_norm, ffn1_w_gate=m_ffn1_w_gate, ffn1_w_up=m_ffn1_w_up, ffn1_w_down=m_ffn1_w_down,
                mix_norm=m_mix_norm, ffn2_norm=m_ffn2_norm, ffn2_w_gate=m_ffn2_w_gate, ffn2_w_up=m_ffn2_w_up,
                ffn2_w_down=m_ffn2_w_down, ev_w_in=m_ev_w_in, ev_b_f=m_ev_b_f, ev_conv_w=m_ev_conv_w,
                ev_conv_b=m_ev_conv_b, ev_conv_norm=m_ev_conv_norm, ev_q_norm=m_ev_q_norm, ev_k_norm=m_ev_k_norm,
                ev_w_out=m_ev_w_out, od_w_in=m_od_w_in, od_conv_w=m_od_conv_w, od_w_out=m_od_w_out)
    v_in = dict(ffn1_norm=v_ffn1_norm, ffn1_w_gate=v_ffn1_w_gate, ffn1_w_up=v_ffn1_w_up, ffn1_w_down=v_ffn1_w_down,
                mix_norm=v_mix_norm, ffn2_norm=v_ffn2_norm, ffn2_w_gate=v_ffn2_w_gate, ffn2_w_up=v_ffn2_w_up,
                ffn2_w_down=v_ffn2_w_down, ev_w_in=v_ev_w_in, ev_b_f=v_ev_b_f, ev_conv_w=v_ev_conv_w,
                ev_conv_b=v_ev_conv_b, ev_conv_norm=v_ev_conv_norm, ev_q_norm=v_ev_q_norm, ev_k_norm=v_ev_k_norm,
                ev_w_out=v_ev_w_out, od_w_in=v_od_w_in, od_conv_w=v_od_conv_w, od_w_out=v_od_w_out)
    order = list(weights)

    d = x.shape[-1]
    fs = ffn1_w_gate.shape[2]
    n_in = ev_w_in.shape[2]
    n_in_pad = _round_up(n_in, BF16_ROWS)
    n_out = ev_w_out.shape[1]
    n_od = od_w_in.shape[2]
    d_conv = ev_conv_b.shape[1]
    d_in_even = n_in * N_DEV
    d_main = d_in_even - N_HEADS
    cx, cy, cc = _position()
    me = _dev_index(cx, cy, cc)

    def block(wg, wu, wd, layer):
        return [wg[layer].T, wu[layer].T, wd[layer]]

    def stack(parts):
        return jnp.concatenate([p.astype(BF16) for p in parts], axis=0)

    ffn1, ffn2 = (ffn1_w_gate, ffn1_w_up, ffn1_w_down), (ffn2_w_gate, ffn2_w_up, ffn2_w_down)
    shard_a = stack(block(*ffn1, 0))
    shard_b = stack(block(*ffn2, 0) + [_pad_rows(ev_w_in[0].T, n_in_pad), ev_w_out[0]])
    shard_c = stack(block(*ffn1, 1) + block(*ffn2, 1) + [od_w_in[0].T, od_w_out[0]])
    off_ev_in, off_ev_out = 3 * fs, 3 * fs + n_in_pad
    off_od_in, off_od_out = 6 * fs, 6 * fs + n_od
    wall_a = _all_gather(shard_a)

    def even_mixer_weights(wall_b):
        ev_w_in_t = wall_b[:, off_ev_in:off_ev_in + n_in, :].reshape(d_in_even, d)
        return dict(ev_w_main_t=ev_w_in_t[:d_main], ev_w_f_t=_pad_rows(ev_w_in_t[d_main:], V7X_LANES),
                    ev_w_out=wall_b[:, off_ev_out:off_ev_out + n_out, :].reshape(N_DEV * n_out, d))

    def odd_mixer_weights(wall_c):
        return dict(od_w_in_t=wall_c[:, off_od_in:off_od_in + n_od, :].reshape(N_DEV * n_od, d),
                    od_w_out=wall_c[:, off_od_out:off_od_out + n_out, :].reshape(N_DEV * n_out, d))

    def by_dev(a, rows, pad_to=None):
        a = a.reshape(N_DEV, rows, d)
        return a if pad_to is None else jnp.pad(a, ((0, 0), (0, pad_to - rows), (0, 0)))

    idx = jnp.stack([me] + [_dev_index(*chip, cc) for chip in _other_chips(cx, cy)]).astype(jnp.int32)

    def pair_sums_of(pieces, tag):
        gall = jnp.concatenate(pieces, axis=1)
        return _pair_sum(gall, _pair_exchange(gall, f"reduce_scatter_pair_exchange_{tag}"), idx,
                         f"reduce_scatter_pair_sum_{tag}")

    def ffn_pieces(g, key):
        return [by_dev(t, fs) for t in g[key]]

    conv_shapes = dict(ev_conv_w=(CONV_A_WIDTH, d_conv), od_conv_w=(CONV_C_WIDTH, d))
    zero_small = {k: jnp.zeros(s_, F32) for k, s_ in conv_shapes.items()}
    ev_cw_part = lax.dynamic_update_slice(zero_small["ev_conv_w"], ev_conv_w[0], (0, me * ev_conv_w.shape[2]))
    od_cw_part = lax.dynamic_update_slice(zero_small["od_conv_w"], od_conv_w[0], (0, me * od_conv_w.shape[2]))
    zeros_like_small = {k: jnp.zeros((1,), F32) for k in SMALL_ORDER}
    taps = _unpack_small(_all_reduce_small(_pack_small({**zeros_like_small, "ev_conv_w": ev_cw_part,
                                                        "od_conv_w": od_cw_part})),
                         {**{k: (1,) for k in SMALL_ORDER}, **conv_shapes})
    small = dict(
        ffn1_norm=[ffn1_norm[l][None] for l in range(2)], mix_norm=[mix_norm[l][None] for l in range(2)],
        ffn2_norm=[ffn2_norm[l][None] for l in range(2)],
        ev_conv_w32=_pad_rows(taps["ev_conv_w"], CONV_A_WIDTH + 1), ev_conv_b=ev_conv_b, ev_conv_norm=ev_conv_norm,
        ev_b_f128=jnp.pad(ev_b_f, ((0, 0), (0, V7X_LANES - N_HEADS))),
        ev_q_norm2=jnp.tile(ev_q_norm, (1, 2)), ev_k_norm2=jnp.tile(ev_k_norm, (1, 2)),
        od_conv_w8=_pad_rows(taps["od_conv_w"], 8),
    )

    plan = dict(
        shard_b=shard_b, shard_c=shard_c, mix_b=even_mixer_weights, mix_c=odd_mixer_weights,
        reduce_c=lambda g1: pair_sums_of(ffn_pieces(g1, "l1_ffn1") + ffn_pieces(g1, "l1_ffn2")
                                         + [by_dev(g1["od_w_in_t"], n_od), by_dev(g1["od_w_out"], n_out)], "c"),
        reduce_b=lambda g1: pair_sums_of(ffn_pieces(g1, "l0_ffn2")
                                         + [by_dev(g1["ev_w_in_t"], n_in, n_in_pad), by_dev(g1["ev_w_out"], n_out)], "b"))
    loss_p, grad_x, g = _local_step(x[0], loss_target[0], wall_a, fs, small, plan)

    sums_a = pair_sums_of(ffn_pieces(g, "l0_ffn1"), "a")
    gsum_a = _final_sum(sums_a, _chip_exchange(sums_a), "reduce_scatter_final_sum_a")
    gsum_b = _final_sum(g["pair_sums_b"], g["exchanged_b"], "reduce_scatter_final_sum_b")
    gsum_c = _final_sum(g["pair_sums_c"], g["exchanged_c"], "reduce_scatter_final_sum_c")

    grad = {}
    where = dict(ffn1=((gsum_a, 0), (gsum_c, 0)), ffn2=((gsum_b, 0), (gsum_c, 3 * fs)))
    for blk, places in where.items():
        for wi, kind in enumerate(("gate", "up", "down")):
            rows = [buf[off + wi * fs:off + (wi + 1) * fs] for buf, off in places]
            grad[f"{blk}_w_{kind}"] = jnp.stack(rows if kind == "down" else [r.T for r in rows])
    grad["ev_w_in"] = gsum_b[off_ev_in:off_ev_in + n_in].T[None]
    grad["ev_w_out"] = gsum_b[off_ev_out:off_ev_out + n_out][None]
    grad["od_w_in"] = gsum_c[off_od_in:off_od_in + n_od].T[None]
    grad["od_w_out"] = gsum_c[off_od_out:off_od_out + n_out][None]

    heads = lambda t: t.reshape(N_HEADS, HEAD_DIM).sum(axis=0)
    parts = dict(
        loss=loss_p[0, 0:1],
        ffn1_norm=jnp.stack([g["ffn1_norm_0"][0], g["ffn1_norm_1"][0]]),
        mix_norm=jnp.stack([g["mix_norm_0"][0], g["mix_norm_1"][0]]),
        ffn2_norm=jnp.stack([g["ffn2_norm_0"][0], g["ffn2_norm_1"][0]]),
        ev_b_f=g["ev_b_f"][:, 0], ev_conv_b=g["ev_conv_b"], ev_conv_norm=g["ev_conv_norm"],
        ev_q_norm=heads(g["ev_q_norm"]), ev_k_norm=heads(g["ev_k_norm"]),
        ev_conv_w=g["ev_conv_w"][:CONV_A_WIDTH], od_conv_w=g["od_conv_w"][:CONV_C_WIDTH])
    small_shapes = dict(loss=(1,), ffn1_norm=ffn1_norm.shape, mix_norm=mix_norm.shape, ffn2_norm=ffn2_norm.shape,
                        ev_b_f=ev_b_f.shape, ev_conv_b=ev_conv_b.shape, ev_conv_norm=ev_conv_norm.shape,
                        ev_q_norm=ev_q_norm.shape, ev_k_norm=ev_k_norm.shape, **conv_shapes)
    red = _unpack_small(_all_reduce_small(_pack_small(parts)), small_shapes)
    loss = red["loss"][0]
    for k in ("ffn1_norm", "mix_norm", "ffn2_norm", "ev_b_f", "ev_conv_b", "ev_conv_norm", "ev_q_norm", "ev_k_norm"):
        grad[k] = red[k]
    grad["ev_conv_w"] = lax.dynamic_slice(red["ev_conv_w"], (0, me * ev_conv_w.shape[2]),
                                          (CONV_A_WIDTH, ev_conv_w.shape[2]))[None]
    grad["od_conv_w"] = lax.dynamic_slice(red["od_conv_w"], (0, me * od_conv_w.shape[2]),
                                          (CONV_C_WIDTH, od_conv_w.shape[2]))[None]

    big = ("ffn1_w_gate", "ffn1_w_up", "ffn1_w_down", "ffn2_w_gate", "ffn2_w_up", "ffn2_w_down",
           "ev_w_in", "ev_w_out", "od_w_in", "od_w_out")
    delta, new_m, new_v = {}, {}, {}
    for k in big:
        shp = weights[k].shape
        flat = lambda t: t.reshape(-1, shp[-1])
        dk, mk, vk = _adamw(flat(weights[k]), flat(grad[k]), flat(m_in[k]), flat(v_in[k]), f"adamw_{k}")
        delta[k], new_m[k], new_v[k] = dk.reshape(shp), mk.reshape(shp), vk.reshape(shp)
    rest = [k for k in order if k not in big]
    cat = lambda src: jnp.concatenate([src[k].reshape(-1) for k in rest])
    n_small = sum(math.prod(weights[k].shape) for k in rest)
    n_pad = _round_up(n_small, 8 * V7X_LANES)
    as_rows = lambda t: jnp.pad(t, (0, n_pad - n_small)).reshape(-1, V7X_LANES)
    v_rows = jnp.pad(cat(v_in), (0, n_pad - n_small), constant_values=1.0).reshape(-1, V7X_LANES)
    ds, ms, vs = _adamw(as_rows(cat(weights)), as_rows(cat(grad)), as_rows(cat(m_in)), v_rows, "adamw_small")
    pos = 0
    for k in rest:
        n = math.prod(weights[k].shape)
        for dst, src in ((delta, ds), (new_m, ms), (new_v, vs)):
            dst[k] = src.reshape(-1)[pos:pos + n].reshape(weights[k].shape)
        pos += n

    return (loss, grad_x[None], *[grad[k] for k in order], *[delta[k] for k in order],
            *[new_m[k] for k in order], *[new_v[k] for k in order])
```

```python
import functools
import math

import jax
import jax.numpy as jnp
from jax import lax
from jax.experimental import pallas as pl
from jax.experimental.pallas import tpu as pltpu

F32 = jnp.float32
BF16 = jnp.bfloat16
SDS = jax.ShapeDtypeStruct
MESH = pl.DeviceIdType.MESH

N_DEV = 8
EPS = 1e-6
FFN_RES = 0.5
HEAD_DIM = 64
N_HEADS = 8
D_ATTN = N_HEADS * HEAD_DIM
N_PAIRS = N_HEADS // 2
PAIR = 2 * HEAD_DIM
ATTN_SCALE = 1.0 / math.sqrt(HEAD_DIM)
CONV_A_WIDTH = 31
CONV_A_HALO = 32
CONV_C_WIDTH = 3
CONV_C_HALO = 8
NEG_BIG = -1e30
ADAM_LR, ADAM_B1, ADAM_B2, ADAM_EPS, ADAM_WD, ADAM_STEP = 0.001, 0.9, 0.999, 1e-08, 0.01, 10

V7X_VMEM_BYTES = 64 * 1024 * 1024
V7X_LANES = 128
BF16_ROWS = 16
MIB = 1024 * 1024

NT = (((1,), (1,)), ((), ()))
TN = (((0,), (0,)), ((), ()))


def _call(body, *, name, out_shape, in_specs, out_specs, grid=(), scratch=(), dims=None, vmem_mb=32, **kw):
    params = dict(vmem_limit_bytes=min(vmem_mb * MIB, V7X_VMEM_BYTES - 4 * MIB))
    if dims is not None:
        params["dimension_semantics"] = dims
    call = pl.pallas_call(
        body, name=name, grid=grid, in_specs=in_specs, out_specs=out_specs, out_shape=_in_hbm(out_shape),
        scratch_shapes=list(scratch), compiler_params=pltpu.CompilerParams(**params), **kw)
    return lambda *args: call(*[_keep_in_hbm(a) for a in args])


LARGE_OPERAND_BYTES = MIB


def _is_large(a):
    return a.ndim >= 2 and math.prod(a.shape) * jnp.dtype(a.dtype).itemsize >= LARGE_OPERAND_BYTES


def _keep_in_hbm(a):
    return pltpu.with_memory_space_constraint(a, pltpu.HBM) if _is_large(a) else a


def _in_hbm(out_shape):
    one = lambda s: pltpu.HBM(s.shape, s.dtype) if _is_large(s) else s
    return [one(s) for s in out_shape] if isinstance(out_shape, (list, tuple)) else one(out_shape)


def _tile(n, want=512):
    return want if n % want == 0 else n


def _rows(tm, d, col=0):
    return pl.BlockSpec((tm, d), lambda i: (i, col))


def _const(shape):
    return pl.BlockSpec(shape, lambda *_: (0,) * len(shape))


ANY = pl.BlockSpec(memory_space=pl.ANY)
VMEM = pl.BlockSpec(memory_space=pltpu.VMEM)


def _sigmoid(x):
    return 1.0 / (1.0 + jnp.exp(-x))


def _rmsnorm_fwd(x, gain, name):
    s, d = x.shape
    tm = _tile(s)

    def body(x_ref, g_ref, o_ref):
        xv = x_ref[...]
        r = lax.rsqrt(jnp.mean(xv * xv, axis=-1, keepdims=True) + EPS)
        o_ref[...] = (xv * r * g_ref[...]).astype(BF16)

    return _call(body, name=name, grid=(s // tm,), in_specs=[_rows(tm, d), _const((1, d))],
                 out_specs=_rows(tm, d), out_shape=SDS((s, d), BF16), dims=("parallel",))(x, gain)


def _rmsnorm_bwd(x, gain, dxn, dres, name):
    s, d = x.shape
    tm = _tile(s)

    def body(x_ref, g_ref, dxn_ref, dres_ref, dx_ref, dg_ref):
        xv = x_ref[...]
        r = lax.rsqrt(jnp.mean(xv * xv, axis=-1, keepdims=True) + EPS)
        xh = xv * r
        dv = dxn_ref[...]

        @pl.when(pl.program_id(0) == 0)
        def _():
            dg_ref[...] = jnp.zeros_like(dg_ref)

        dg_ref[...] += jnp.sum(dv * xh, axis=0, keepdims=True)
        dxh = dv * g_ref[...]
        dx_ref[...] = dres_ref[...] + r * (dxh - xh * jnp.mean(dxh * xh, axis=-1, keepdims=True))

    return _call(body, name=name, grid=(s // tm,),
                 in_specs=[_rows(tm, d), _const((1, d)), _rows(tm, d), _rows(tm, d)],
                 out_specs=[_rows(tm, d), _const((1, d))],
                 out_shape=[SDS((s, d), F32), SDS((1, d), F32)], dims=("arbitrary",))(x, gain, dxn, dres)


def _col_tile(n):
    for t in (1024, 768, 512, 256, 128):
        if n % t == 0:
            return t
    return n


def _mm(a, b, mode, name, out_dtype=F32, add=None):
    if mode == "tn":
        k, m = a.shape
        n = b.shape[1]
        bm = 256 if m % 256 == 0 else m

        def body_tn(a_ref, b_ref, o_ref):
            o_ref[...] = lax.dot_general(a_ref[...].astype(BF16), b_ref[...].astype(BF16), TN,
                                         preferred_element_type=F32).astype(out_dtype)

        return _call(body_tn, name=name, grid=(m // bm,),
                     in_specs=[pl.BlockSpec((k, bm), lambda i: (0, i)), _const((k, n))],
                     out_specs=pl.BlockSpec((bm, n), lambda i: (i, 0)),
                     out_shape=SDS((m, n), out_dtype), dims=("parallel",), vmem_mb=48)(a, b)
    m, k = a.shape
    n = b.shape[0] if mode == "nt" else b.shape[1]
    tm, tn = _tile(m), _col_tile(n)
    dn = NT if mode == "nt" else (((1,), (0,)), ((), ()))

    def body(a_ref, b_ref, *rest):
        o_ref = rest[-1]
        acc = lax.dot_general(a_ref[...].astype(BF16), b_ref[...].astype(BF16), dn, preferred_element_type=F32)
        if add is not None:
            acc = acc + rest[0][...]
        o_ref[...] = acc.astype(out_dtype)

    b_spec = (pl.BlockSpec((tn, k), lambda i, j: (j, 0)) if mode == "nt"
              else pl.BlockSpec((k, tn), lambda i, j: (0, j)))
    in_specs = [pl.BlockSpec((tm, k), lambda i, j: (i, 0)), b_spec]
    args = [a, b]
    if add is not None:
        in_specs.append(pl.BlockSpec((tm, tn), lambda i, j: (i, j)))
        args.append(add)
    return _call(body, name=name, grid=(m // tm, n // tn), in_specs=in_specs,
                 out_specs=pl.BlockSpec((tm, tn), lambda i, j: (i, j)),
                 out_shape=SDS((m, n), out_dtype), dims=("parallel", "parallel"), vmem_mb=48)(*args)


def _mm_tn(a, b, name, out_dtype=F32):
    return _mm(a, b, "tn", name, out_dtype)


FFN_TM = 256
FFN_FWD_TM = 512
FFN_CHUNK = 256


def _load_ffn_weights(w_hbm, offs, fs, dsts, sems):
    copies = []
    for wi, (off, dst) in enumerate(zip(offs, dsts)):
        for j in range(N_DEV):
            cp = pltpu.make_async_copy(w_hbm.at[j, pl.ds(off, fs), :], dst.at[pl.ds(j * fs, fs), :],
                                       sems.at[wi * N_DEV + j])
            cp.start()
            copies.append(cp)
    for cp in copies:
        cp.wait()


def _ffn_fwd(x, gain, wall, offs, fs, name, gather_src=None):
    s, d = x.shape
    f = fs * N_DEV
    tm, ch = _tile(s, FFN_FWD_TM), FFN_CHUNK
    n = s // tm
    gathers = gather_src is not None

    def body(x_ref, gain_ref, w_hbm, *rest):
        if gathers:
            (src_hbm, out_ref, xn_ref, g_ref, u_ref, h_ref, gathered, wg_s, wu_s, wd_s, sems,
             send_sems, recv_sems, local_sem) = rest
            start, forward, finish = _gather_phases(src_hbm, gathered, send_sems, recv_sems, local_sem)
            pl.when(pl.program_id(0) == 0)(start)
            pl.when(pl.program_id(0) == (3 * n) // 4)(forward)
        else:
            out_ref, xn_ref, g_ref, u_ref, h_ref, wg_s, wu_s, wd_s, sems = rest

        @pl.when(pl.program_id(0) == 0)
        def _():
            _load_ffn_weights(w_hbm, offs, fs, (wg_s, wu_s, wd_s), sems)

        xv = x_ref[...]
        xnv = (xv * lax.rsqrt(jnp.mean(xv * xv, axis=-1, keepdims=True) + EPS) * gain_ref[...]).astype(BF16)
        xn_ref[...] = xnv
        acc = jnp.zeros((tm, d), F32)
        for c in range(f // ch):
            sl = slice(c * ch, (c + 1) * ch)
            gb = lax.dot_general(xnv, wg_s[sl, :], NT, preferred_element_type=F32).astype(BF16)
            ub = lax.dot_general(xnv, wu_s[sl, :], NT, preferred_element_type=F32).astype(BF16)
            g_ref[:, sl] = gb
            u_ref[:, sl] = ub
            g = gb.astype(F32)
            hb = (g * _sigmoid(g) * ub.astype(F32)).astype(BF16)
            h_ref[:, sl] = hb
            acc = acc + jnp.dot(hb, wd_s[sl, :], preferred_element_type=F32)
        out_ref[...] = xv + FFN_RES * acc
        if gathers:
            pl.when(pl.program_id(0) == n - 1)(finish)

    in_specs, args = [_rows(tm, d), _const((1, d)), ANY], [x, gain, wall]
    out_specs = [_rows(tm, d), _rows(tm, d), _rows(tm, f), _rows(tm, f), _rows(tm, f)]
    out_shape = [SDS((s, d), F32), SDS((s, d), BF16), SDS((s, f), BF16), SDS((s, f), BF16), SDS((s, f), BF16)]
    scratch = [pltpu.VMEM((f, d), BF16)] * 3 + [pltpu.SemaphoreType.DMA((3 * N_DEV,))]
    if gathers:
        in_specs.append(ANY)
        args.append(gather_src)
        out_specs.append(ANY)
        out_shape.append(SDS((N_DEV,) + gather_src.shape, gather_src.dtype))
        scratch += GATHER_SEMS
    return _call(body, name=name, grid=(n,), in_specs=in_specs, out_specs=out_specs, out_shape=out_shape,
                 scratch=scratch, dims=("arbitrary",), vmem_mb=56)(*args)


def _ffn_bwd_act(dout, g, u, wall, offs, fs, name, exchange_src=None):
    s, d = dout.shape
    f = fs * N_DEV
    tm, ch = _tile(s, FFN_TM), FFN_CHUNK
    n = s // tm
    exchanges = exchange_src is not None
    if exchanges:
        phases_of, shape_of, exchange_sems = EXCHANGES[exchange_src[0]]

    def body(dout_ref, g_ref, u_ref, w_hbm, *rest):
        if exchanges:
            t_hbm, dg_ref, du_ref, dy_ref, dxn_ref, rcv_ref, wg_s, wu_s, wd_s, sems, send_sems, recv_sems = rest
            start, finish = phases_of(t_hbm, rcv_ref, send_sems, recv_sems)
            pl.when(pl.program_id(0) == 0)(start)
        else:
            dg_ref, du_ref, dy_ref, dxn_ref, wg_s, wu_s, wd_s, sems = rest

        @pl.when(pl.program_id(0) == 0)
        def _():
            _load_ffn_weights(w_hbm, offs, fs, (wg_s, wu_s, wd_s), sems)

        dy = (FFN_RES * dout_ref[...]).astype(BF16)
        dy_ref[...] = dy
        acc = jnp.zeros((tm, d), F32)
        for c in range(f // ch):
            sl = slice(c * ch, (c + 1) * ch)
            dh = lax.dot_general(dy, wd_s[sl, :], NT, preferred_element_type=F32)
            gv = g_ref[:, sl].astype(F32)
            uv = u_ref[:, sl].astype(F32)
            sg = _sigmoid(gv)
            dgb = (dh * uv * sg * (1.0 + gv * (1.0 - sg))).astype(BF16)
            dub = (dh * gv * sg).astype(BF16)
            dg_ref[:, sl] = dgb
            du_ref[:, sl] = dub
            acc = acc + jnp.dot(dgb, wg_s[sl, :], preferred_element_type=F32)
            acc = acc + jnp.dot(dub, wu_s[sl, :], preferred_element_type=F32)
        dxn_ref[...] = acc
        if exchanges:
            pl.when(pl.program_id(0) == n - 1)(finish)

    in_specs, args = [_rows(tm, d), _rows(tm, f), _rows(tm, f), ANY], [dout, g, u, wall]
    out_specs = [_rows(tm, f), _rows(tm, f), _rows(tm, d), _rows(tm, d)]
    out_shape = [SDS((s, f), BF16), SDS((s, f), BF16), SDS((s, d), BF16), SDS((s, d), F32)]
    scratch = [pltpu.VMEM((f, d), BF16)] * 3 + [pltpu.SemaphoreType.DMA((3 * N_DEV,))]
    if exchanges:
        in_specs.append(ANY)
        args.append(exchange_src[1])
        out_specs.append(ANY)
        out_shape.append(shape_of(exchange_src[1]))
        scratch += exchange_sems
    return _call(body, name=name, grid=(n,), in_specs=in_specs, out_specs=out_specs, out_shape=out_shape,
                 scratch=scratch, dims=("arbitrary",), vmem_mb=56)(*args)


def _prev_rows(halo, tm, c, col):
    return pl.BlockSpec((halo, c), lambda i: (jnp.maximum(i * (tm // halo) - 1, 0), col))


def _next_rows(halo, tm, c, col, n_blocks):
    return pl.BlockSpec((halo, c), lambda i: (jnp.minimum((i + 1) * (tm // halo), n_blocks - 1), col))


def _conv_a_fwd(z, cw, cb, cn, name):
    s = z.shape[0]
    c = cb.shape[1]
    tm, halo, kw = _tile(s), CONV_A_HALO, CONV_A_WIDTH

    def body(u_ref, g_ref, up_ref, gp_ref, cw_ref, cb_ref, cn_ref, a_ref, a1_ref, buf):
        i = pl.program_id(0)
        buf[0:halo, :] = jnp.where(i > 0, up_ref[...] * _sigmoid(gp_ref[...]), 0.0)
        buf[halo:halo + tm, :] = u_ref[...] * _sigmoid(g_ref[...])
        acc = jnp.zeros((tm, c), F32)
        for k in range(kw):
            acc = acc + cw_ref[k:k + 1, :] * buf[pl.ds(halo - (kw - 1) + k, tm), :]
        a1 = acc + cb_ref[...]
        a1_ref[...] = a1
        a2 = a1 * lax.rsqrt(jnp.mean(a1 * a1, axis=-1, keepdims=True) + EPS) * cn_ref[...]
        a_ref[...] = (a2 * _sigmoid(a2)).astype(BF16)

    return _call(body, name=name, grid=(s // tm,),
                 in_specs=[_rows(tm, c, 0), _rows(tm, c, 1), _prev_rows(halo, tm, c, 0), _prev_rows(halo, tm, c, 1),
                           _const(cw.shape), _const((1, c)), _const((1, c))],
                 out_specs=[_rows(tm, c), _rows(tm, c)],
                 out_shape=[SDS((s, c), BF16), SDS((s, c), F32)],
                 scratch=[pltpu.VMEM((tm + halo, c), F32)], dims=("parallel",))(z, z, z, z, cw, cb, cn)


def _conv_a_bwd_norm(dao, a1, cn, name):
    s, c = a1.shape
    tm = _tile(s)

    def body(da_ref, a1_ref, cn_ref, da1_ref, dcn_ref, dcb_ref):
        a1v = a1_ref[...]
        r = lax.rsqrt(jnp.mean(a1v * a1v, axis=-1, keepdims=True) + EPS)
        xh = a1v * r
        a2 = xh * cn_ref[...]
        sg = _sigmoid(a2)
        da2 = da_ref[...] * sg * (1.0 + a2 * (1.0 - sg))
        dxh = da2 * cn_ref[...]
        da1 = r * (dxh - xh * jnp.mean(dxh * xh, axis=-1, keepdims=True))
        da1_ref[...] = da1

        @pl.when(pl.program_id(0) == 0)
        def _():
            dcn_ref[...] = jnp.zeros_like(dcn_ref)
            dcb_ref[...] = jnp.zeros_like(dcb_ref)

        dcn_ref[...] += jnp.sum(da2 * xh, axis=0, keepdims=True)
        dcb_ref[...] += jnp.sum(da1, axis=0, keepdims=True)

    return _call(body, name=name, grid=(s // tm,),
                 in_specs=[_rows(tm, c, 0), _rows(tm, c), _const((1, c))],
                 out_specs=[_rows(tm, c), _const((1, c)), _const((1, c))],
                 out_shape=[SDS((s, c), F32), SDS((1, c), F32), SDS((1, c), F32)], dims=("arbitrary",))(dao, a1, cn)


def _conv_a_bwd_conv(da1, z, cw, name, exchange_src=None):
    s, c = da1.shape
    tm, halo, kw = _tile(s), CONV_A_HALO, CONV_A_WIDTH
    n = s // tm
    exchanges = exchange_src is not None
    if exchanges:
        phases_of, shape_of, exchange_sems = EXCHANGES[exchange_src[0]]

    def body(d_ref, dn_ref, u_ref, g_ref, up_ref, gp_ref, cw_ref, *rest):
        i = pl.program_id(0)
        if exchanges:
            t_hbm, du_ref, dg_ref, dcw_ref, rcv_ref, buf, bd, send_sems, recv_sems = rest
            start, finish = phases_of(t_hbm, rcv_ref, send_sems, recv_sems)
            pl.when(i == 0)(start)
        else:
            du_ref, dg_ref, dcw_ref, buf, bd = rest
        uv = u_ref[...]
        sg = _sigmoid(g_ref[...])
        buf[0:halo, :] = jnp.where(i > 0, up_ref[...] * _sigmoid(gp_ref[...]), 0.0)
        buf[halo:halo + tm, :] = uv * sg
        dv = d_ref[...]
        bd[0:tm, :] = dv
        bd[tm:tm + halo, :] = jnp.where(i < n - 1, dn_ref[...], 0.0)

        @pl.when(i == 0)
        def _():
            dcw_ref[...] = jnp.zeros_like(dcw_ref)

        da0 = jnp.zeros((tm, c), F32)
        for k in range(kw):
            da0 = da0 + cw_ref[k:k + 1, :] * bd[pl.ds(kw - 1 - k, tm), :]
            dcw_ref[k:k + 1, :] += jnp.sum(dv * buf[pl.ds(halo - (kw - 1) + k, tm), :], axis=0, keepdims=True)
        du_ref[...] = (da0 * sg).astype(BF16)
        dg_ref[...] = (da0 * uv * sg * (1.0 - sg)).astype(BF16)
        if exchanges:
            pl.when(i == n - 1)(finish)

    in_specs = [_rows(tm, c), _next_rows(halo, tm, c, 0, s // halo), _rows(tm, c, 0), _rows(tm, c, 1),
                _prev_rows(halo, tm, c, 0), _prev_rows(halo, tm, c, 1), _const(cw.shape)]
    args = [da1, da1, z, z, z, z, cw]
    out_specs = [_rows(tm, c), _rows(tm, c), _const(cw.shape)]
    out_shape = [SDS((s, c), BF16), SDS((s, c), BF16), SDS(cw.shape, F32)]
    scratch = [pltpu.VMEM((tm + halo, c), F32)] * 2
    if exchanges:
        in_specs.append(ANY)
        args.append(exchange_src[1])
        out_specs.append(ANY)
        out_shape.append(shape_of(exchange_src[1]))
        scratch += exchange_sems
    return _call(body, name=name, grid=(n,), in_specs=in_specs, out_specs=out_specs, out_shape=out_shape,
                 scratch=scratch, dims=("arbitrary",))(*args)


def _lane_is_first_head(tm):
    return lax.broadcasted_iota(jnp.int32, (tm, PAIR), 1) < HEAD_DIM


def _pair_rms(xp, first):
    x2 = xp * xp
    s0 = jnp.sum(jnp.where(first, x2, 0.0), axis=-1, keepdims=True)
    s1 = jnp.sum(jnp.where(first, 0.0, x2), axis=-1, keepdims=True)
    return jnp.where(first, lax.rsqrt(s0 / HEAD_DIM + EPS), lax.rsqrt(s1 / HEAD_DIM + EPS))


def _split3(x):
    hi = x.astype(BF16)
    r1 = x - hi.astype(F32)
    mid = r1.astype(BF16)
    lo = (r1 - mid.astype(F32)).astype(BF16)
    return hi, mid, lo


def _qk_fwd(z, flog, bf, qn2, kn2, name):
    s = z.shape[0]
    tm = _tile(s)
    col0 = (z.shape[1] - 3 * D_ATTN) // D_ATTN

    def body(q_ref, k_ref, v_ref, fl_ref, bf_ref, qn_ref, kn_ref,
             qs_ref, kh_ref, vb_ref, fb_ref, ft_ref, xt_ref, carry):
        i = pl.program_id(0)
        first = _lane_is_first_head(tm)
        for p in range(N_PAIRS):
            sl = slice(p * PAIR, (p + 1) * PAIR)
            q = q_ref[:, sl]
            qs_ref[:, sl] = (q * _pair_rms(q, first) * qn_ref[...] * ATTN_SCALE).astype(BF16)
            k = k_ref[:, sl]
            kh_ref[:, sl] = (k * _pair_rms(k, first) * kn_ref[...]).astype(BF16)
        vb_ref[...] = v_ref[...].astype(BF16)

        xg = fl_ref[...] + bf_ref[...]
        valid = lax.broadcasted_iota(jnp.int32, (tm, V7X_LANES), 1) < N_HEADS
        ls = jnp.where(valid, jnp.minimum(xg, 0.0) - jnp.log(1.0 + jnp.exp(-jnp.abs(xg))), 0.0)
        tri = (lax.broadcasted_iota(jnp.int32, (tm, tm), 1) <= lax.broadcasted_iota(jnp.int32, (tm, tm), 0)).astype(BF16)
        cs = jnp.zeros((tm, V7X_LANES), F32)
        for part in _split3(ls):
            cs = cs + jnp.dot(tri, part, preferred_element_type=F32)

        @pl.when(i == 0)
        def _():
            carry[...] = jnp.zeros_like(carry)

        fv = cs + carry[0:1, :]
        carry[0:1, :] = fv[tm - 1:tm, :]
        ft_ref[...] = fv.T[0:N_HEADS, :]
        xt_ref[...] = xg.T[0:N_HEADS, :]
        for p in range(N_PAIRS):
            fb_ref[:, p * PAIR:(p + 1) * PAIR] = jnp.where(first, fv[:, 2 * p:2 * p + 1], fv[:, 2 * p + 1:2 * p + 2])

    wide = lambda col: pl.BlockSpec((tm, D_ATTN), lambda i: (i, col))
    tcol = pl.BlockSpec((N_HEADS, tm), lambda i: (0, i))
    return _call(body, name=name, grid=(s // tm,),
                 in_specs=[wide(col0), wide(col0 + 1), wide(col0 + 2), _rows(tm, V7X_LANES),
                           _const((1, V7X_LANES)), _const((1, PAIR)), _const((1, PAIR))],
                 out_specs=[wide(0), wide(0), wide(0), wide(0), tcol, tcol],
                 out_shape=[SDS((s, D_ATTN), BF16)] * 3 + [SDS((s, D_ATTN), F32), SDS((N_HEADS, s), F32),
                                                          SDS((N_HEADS, s), F32)],
                 scratch=[pltpu.VMEM((8, V7X_LANES), F32)], dims=("arbitrary",))(z, z, z, flog, bf, qn2, kn2)


ATTN_FWD_SUB = 256
ATTN_BWD_SUB = 512


def _causal_schedule(nq, key_major):
    if key_major:
        pairs = [(i, j) for j in range(nq) for i in range(j, nq)]
    else:
        pairs = [(i, j) for i in range(nq) for j in range(i + 1)]
    return (jnp.asarray([p[0] for p in pairs], jnp.int32), jnp.asarray([p[1] for p in pairs], jnp.int32))


def _sub_scores(qp, kp, ft_row, mine, r, masked, sub, tk):
    qm = jnp.where(mine, qp, jnp.zeros_like(qp))
    s2 = lax.dot_general(qm, kp, NT, preferred_element_type=F32) - ft_row
    if masked:
        row = r * sub + lax.broadcasted_iota(jnp.int32, (sub, tk), 0)
        s2 = jnp.where(lax.broadcasted_iota(jnp.int32, (sub, tk), 1) <= row, s2, NEG_BIG)
    return s2


def _attn_fwd(qs, kh, vb, fb, ft, name, gather_src=None):
    s = qs.shape[0]
    tq = tk = _tile(s)
    nq = s // tq
    sub = min(ATTN_FWD_SUB, tq)
    ii, jj = _causal_schedule(nq, key_major=False)
    n_steps = ii.shape[0]
    gathers = gather_src is not None

    def body(ii_ref, jj_ref, q_ref, k_ref, v_ref, fq_ref, ft_ref, *rest):
        if gathers:
            x_hbm, o_ref, lse_ref, wall_ref, m_s, l_s, acc_s, send_sems, recv_sems, local_sem = rest
        else:
            o_ref, lse_ref, m_s, l_s, acc_s = rest
        p, t = pl.program_id(0), pl.program_id(1)
        i, j = ii_ref[t], jj_ref[t]
        first = _lane_is_first_head(sub)
        if gathers:
            start, forward, finish = _gather_phases(x_hbm, wall_ref, send_sems, recv_sems, local_sem)
            pl.when(jnp.logical_and(p == 0, t == 0))(start)
            pl.when(jnp.logical_and(p == N_PAIRS - 1, t == 0))(forward)

        @pl.when(j == 0)
        def _():
            m_s[...] = jnp.full_like(m_s, NEG_BIG)
            l_s[...] = jnp.zeros_like(l_s)
            acc_s[...] = jnp.zeros_like(acc_s)

        def tile(masked):
            kp, vp = k_ref[...], v_ref[...]
            q_all, fq_all, acc_all = q_ref[...], fq_ref[...], acc_s[...]
            m_all, l_all = (m_s[0], m_s[1]), (l_s[0], l_s[1])
            ft_rows = [ft_ref[pl.ds(2 * p + h, 1), :] for h in range(2)]
            m_out, l_out, acc_out = ([], []), ([], []), []
            for r in range(tq // sub):
                rows = slice(r * sub, (r + 1) * sub)
                qp, fq, acc = q_all[rows, :], fq_all[rows, :], acc_all[rows, :]
                new = []
                for h in range(2):
                    mine = first if h == 0 else jnp.logical_not(first)
                    s2 = _sub_scores(qp, kp, ft_rows[h], mine, r, masked, sub, tk)
                    fqh = fq[:, h * HEAD_DIM:h * HEAD_DIM + 1]
                    m_old = m_all[h][rows, :]
                    m_new = jnp.maximum(m_old, jnp.max(s2, axis=-1, keepdims=True) + fqh)
                    pr = jnp.exp(s2 - (m_new - fqh))
                    alpha = jnp.exp(m_old - m_new)
                    l_out[h].append(alpha * l_all[h][rows, :] + jnp.sum(pr, axis=-1, keepdims=True))
                    m_out[h].append(m_new)
                    new.append(alpha * acc + jnp.dot(pr.astype(BF16), vp, preferred_element_type=F32))
                acc_out.append(jnp.where(first, new[0], new[1]))
            for h in range(2):
                m_s[h] = jnp.concatenate(m_out[h], axis=0)
                l_s[h] = jnp.concatenate(l_out[h], axis=0)
            acc_s[...] = jnp.concatenate(acc_out, axis=0)

        @pl.when(j < i)
        def _():
            tile(False)

        @pl.when(j == i)
        def _():
            tile(True)
            whole = _lane_is_first_head(tq)
            l_pair = jnp.where(whole, l_s[0], l_s[1])
            o_ref[...] = acc_s[...] / l_pair
            lse_ref[...] = jnp.where(whole, m_s[0], m_s[1]) + jnp.log(l_pair)

        if gathers:
            pl.when(jnp.logical_and(p == N_PAIRS - 1, t == n_steps - 1))(finish)

    qblk = pl.BlockSpec((tq, PAIR), lambda p, t, ii_r, jj_r: (ii_r[t], p))
    kblk = pl.BlockSpec((tk, PAIR), lambda p, t, ii_r, jj_r: (jj_r[t], p))
    in_specs = [qblk, kblk, kblk, qblk, pl.BlockSpec((N_HEADS, tk), lambda p, t, ii_r, jj_r: (0, jj_r[t]))]
    out_specs, out_shape = [qblk, qblk], [SDS((s, D_ATTN), F32)] * 2
    scratch = [pltpu.VMEM((2, tq, 1), F32), pltpu.VMEM((2, tq, 1), F32), pltpu.VMEM((tq, PAIR), F32)]
    args = [ii, jj, qs, kh, vb, fb, ft]
    if gathers:
        in_specs.append(ANY)
        out_specs.append(ANY)
        out_shape.append(SDS((N_DEV,) + gather_src.shape, gather_src.dtype))
        scratch += GATHER_SEMS
        args.append(gather_src)
    grid_spec = pltpu.PrefetchScalarGridSpec(num_scalar_prefetch=2, grid=(N_PAIRS, n_steps), in_specs=in_specs,
                                             out_specs=out_specs, scratch_shapes=scratch)
    return pl.pallas_call(
        body, name=name, grid_spec=grid_spec, out_shape=_in_hbm(out_shape),
        compiler_params=pltpu.CompilerParams(dimension_semantics=("arbitrary", "arbitrary"),
                                             vmem_limit_bytes=32 * MIB))(*[_keep_in_hbm(a) for a in args])


def _attn_bwd(qs, kh, vb, fb, ft, lse, o, dao, name, exchange_src=None):
    s = qs.shape[0]
    tq = tk = _tile(s)
    nq = s // tq
    sub = min(ATTN_BWD_SUB, tq)
    ii, jj = _causal_schedule(nq, key_major=True)
    n_steps = ii.shape[0]

    exchanges = exchange_src is not None

    def body(ii_ref, jj_ref, q_ref, k_ref, v_ref, fq_ref, ft_ref, lse_ref, o_ref, do_ref, *rest):
        if exchanges:
            t_hbm, dq_ref, rs_ref, dk_ref, dv_ref, df_ref, rcv_ref, dk_s, dv_s, df_s, send_sems, recv_sems = rest
        else:
            dq_ref, rs_ref, dk_ref, dv_ref, df_ref, dk_s, dv_s, df_s = rest
        p, t = pl.program_id(0), pl.program_id(1)
        i, j = ii_ref[t], jj_ref[t]
        first = _lane_is_first_head(sub)
        first_k = _lane_is_first_head(tk)
        if exchanges:
            start, finish = _chip_exchange_phases(t_hbm, rcv_ref, send_sems, recv_sems)
            pl.when(jnp.logical_and(p == 0, t == 0))(start)

        @pl.when(t == 0)
        def _():
            dq_ref[...] = jnp.zeros_like(dq_ref)
            rs_ref[...] = jnp.zeros_like(rs_ref)

        @pl.when(i == j)
        def _():
            dk_s[...] = jnp.zeros_like(dk_s)
            dv_s[...] = jnp.zeros_like(dv_s)
            df_s[...] = jnp.zeros_like(df_s)

        def tile(masked):
            kp, vp = k_ref[...], v_ref[...]
            q_all, fq_all, lse_all, o_all, do_all = q_ref[...], fq_ref[...], lse_ref[...], o_ref[...], do_ref[...]
            ft_rows = [ft_ref[pl.ds(2 * p + h, 1), :] for h in range(2)]
            dq_out, rs_out = [], []
            dk_acc, dv_acc = jnp.zeros((tk, PAIR), F32), jnp.zeros((tk, PAIR), F32)
            df_acc = [jnp.zeros((1, tk), F32), jnp.zeros((1, tk), F32)]
            for r in range(tq // sub):
                rows = slice(r * sub, (r + 1) * sub)
                qp, fq, lse, ov, dall = q_all[rows, :], fq_all[rows, :], lse_all[rows, :], o_all[rows, :], do_all[rows, :]
                dq_h, dk_h, dv_h, rs_h = [], [], [], []
                for h in range(2):
                    mine = first if h == 0 else jnp.logical_not(first)
                    s2 = _sub_scores(qp, kp, ft_rows[h], mine, r, masked, sub, tk)
                    lane = slice(h * HEAD_DIM, h * HEAD_DIM + 1)
                    pr = jnp.exp(s2 - (lse[:, lane] - fq[:, lane]))
                    dov = jnp.where(mine, dall, 0.0)
                    dsum = jnp.sum(dov * ov, axis=-1, keepdims=True)
                    dom = dov.astype(BF16)
                    dom_lo = (dov - dom.astype(F32)).astype(BF16)
                    dp = lax.dot_general(dom, vp, NT, preferred_element_type=F32)
                    dp = dp + lax.dot_general(dom_lo, vp, NT, preferred_element_type=F32)
                    ds = pr * (dp - dsum)
                    dsb = ds.astype(BF16)
                    dq_h.append(jnp.dot(dsb, kp, preferred_element_type=F32))
                    dk_h.append(lax.dot_general(dsb, qp, TN, preferred_element_type=F32))
                    dv_h.append(lax.dot_general(pr.astype(BF16), dom, TN, preferred_element_type=F32))
                    rs_h.append(jnp.sum(ds, axis=-1, keepdims=True))
                    df_acc[h] = df_acc[h] - jnp.sum(ds, axis=0, keepdims=True)
                dq_out.append(jnp.where(first, dq_h[0], dq_h[1]))
                rs_out.append(jnp.where(first, rs_h[0], rs_h[1]))
                dk_acc = dk_acc + jnp.where(first_k, dk_h[0], dk_h[1])
                dv_acc = dv_acc + jnp.where(first_k, dv_h[0], dv_h[1])
            grows = pl.ds(pl.multiple_of(i * tq, tq), tq)
            dq_ref[grows, :] += jnp.concatenate(dq_out, axis=0)
            rs_ref[grows, :] += jnp.concatenate(rs_out, axis=0)
            dk_s[...] += dk_acc
            dv_s[...] += dv_acc
            for h in range(2):
                df_s[h:h + 1, :] += df_acc[h]

        @pl.when(j < i)
        def _():
            tile(False)

        @pl.when(j == i)
        def _():
            tile(True)

        @pl.when(i == nq - 1)
        def _():
            dk_ref[...] = dk_s[...]
            dv_ref[...] = dv_s[...]
            df_ref[0] = df_s[...]

        if exchanges:
            pl.when(jnp.logical_and(p == N_PAIRS - 1, t == n_steps - 1))(finish)

    qblk = pl.BlockSpec((tq, PAIR), lambda p, t, ii_r, jj_r: (ii_r[t], p))
    kblk = pl.BlockSpec((tk, PAIR), lambda p, t, ii_r, jj_r: (jj_r[t], p))
    doblk = pl.BlockSpec((tq, PAIR), lambda p, t, ii_r, jj_r: (ii_r[t], N_PAIRS + p))
    whole = pl.BlockSpec((s, PAIR), lambda p, t, ii_r, jj_r: (0, p))
    in_specs = [qblk, kblk, kblk, qblk, pl.BlockSpec((N_HEADS, tk), lambda p, t, ii_r, jj_r: (0, jj_r[t])),
                qblk, qblk, doblk]
    out_specs = [whole, whole, kblk, kblk, pl.BlockSpec((1, 8, tk), lambda p, t, ii_r, jj_r: (p, 0, jj_r[t]))]
    out_shape = [SDS((s, D_ATTN), F32)] * 4 + [SDS((N_PAIRS, 8, s), F32)]
    scratch = [pltpu.VMEM((tk, PAIR), F32), pltpu.VMEM((tk, PAIR), F32), pltpu.VMEM((8, tk), F32)]
    args = [ii, jj, qs, kh, vb, fb, ft, lse, o, dao]
    if exchanges:
        in_specs.append(ANY)
        out_specs.append(ANY)
        out_shape.append(SDS((3,) + exchange_src.shape[1:], exchange_src.dtype))
        scratch += EXCHANGE_SEMS
        args.append(exchange_src)
    grid_spec = pltpu.PrefetchScalarGridSpec(num_scalar_prefetch=2, grid=(N_PAIRS, n_steps), in_specs=in_specs,
                                             out_specs=out_specs, scratch_shapes=scratch)
    return pl.pallas_call(
        body, name=name, grid_spec=grid_spec, out_shape=_in_hbm(out_shape),
        compiler_params=pltpu.CompilerParams(dimension_semantics=("arbitrary", "arbitrary"),
                                             vmem_limit_bytes=40 * MIB))(*[_keep_in_hbm(a) for a in args])


def _qk_bwd(z, dqs, dkh, dv, qn2, kn2, name):
    s = z.shape[0]
    tm = _tile(s)
    col0 = (z.shape[1] - 3 * D_ATTN) // D_ATTN

    def body(q_ref, k_ref, dqs_ref, dkh_ref, dv_ref, qn_ref, kn_ref, dq_ref, dk_ref, dvb_ref, dqn_ref, dkn_ref):
        first = _lane_is_first_head(tm)

        @pl.when(pl.program_id(0) == 0)
        def _():
            dqn_ref[...] = jnp.zeros_like(dqn_ref)
            dkn_ref[...] = jnp.zeros_like(dkn_ref)

        def through(x_ref, dy_ref, gain_ref, dx_ref, dgain_ref, scale):
            for p in range(N_PAIRS):
                sl = slice(p * PAIR, (p + 1) * PAIR)
                xv = x_ref[:, sl]
                r = _pair_rms(xv, first)
                xh = xv * r
                dy = dy_ref[:, sl] * scale
                dgain_ref[:, sl] += jnp.sum(dy * xh, axis=0, keepdims=True)
                dxh = dy * gain_ref[...]
                t = dxh * xh
                m0 = jnp.sum(jnp.where(first, t, 0.0), axis=-1, keepdims=True)
                m1 = jnp.sum(jnp.where(first, 0.0, t), axis=-1, keepdims=True)
                mean = jnp.where(first, m0, m1) / HEAD_DIM
                dx_ref[:, sl] = (r * (dxh - xh * mean)).astype(BF16)

        through(q_ref, dqs_ref, qn_ref, dq_ref, dqn_ref, ATTN_SCALE)
        through(k_ref, dkh_ref, kn_ref, dk_ref, dkn_ref, 1.0)
        dvb_ref[...] = dv_ref[...].astype(BF16)

    wide = lambda col: pl.BlockSpec((tm, D_ATTN), lambda i: (i, col))
    return _call(body, name=name, grid=(s // tm,),
                 in_specs=[wide(col0), wide(col0 + 1), wide(0), wide(0), wide(0), _const((1, PAIR)), _const((1, PAIR))],
                 out_specs=[wide(0), wide(0), wide(0), _const((1, D_ATTN)), _const((1, D_ATTN))],
                 out_shape=[SDS((s, D_ATTN), BF16)] * 3 + [SDS((1, D_ATTN), F32)] * 2,
                 dims=("arbitrary",))(z, z, dqs, dkh, dv, qn2, kn2)


def _gate_bwd(dft, xt, name):
    s = xt.shape[1]
    tm = _tile(s)
    n = s // tm

    def body(df_ref, xt_ref, dxt_ref, dx_ref, db_ref, carry):
        i = pl.program_id(0)

        @pl.when(i == 0)
        def _():
            carry[...] = jnp.zeros_like(carry)
            db_ref[...] = jnp.zeros_like(db_ref)

        tri = (lax.broadcasted_iota(jnp.int32, (tm, tm), 0) >= lax.broadcasted_iota(jnp.int32, (tm, tm), 1)).astype(BF16)
        rc = jnp.zeros((N_HEADS, tm), F32)
        for part in _split3(df_ref[...]):
            rc = rc + jnp.dot(part, tri, preferred_element_type=F32)
        dls = rc + carry[:, 0:1]
        carry[...] = jnp.broadcast_to(dls[:, 0:1], carry.shape)
        dxt = dls * _sigmoid(-xt_ref[...])
        dxt_ref[...] = dxt
        db_ref[...] += jnp.broadcast_to(jnp.sum(dxt, axis=-1, keepdims=True), db_ref.shape)
        padded = jnp.concatenate([dxt, jnp.zeros((V7X_LANES - N_HEADS, tm), F32)], axis=0)
        dx_ref[...] = padded.T

    rev = pl.BlockSpec((N_HEADS, tm), lambda i: (0, n - 1 - i))
    return _call(body, name=name, grid=(n,), in_specs=[rev, rev],
                 out_specs=[rev, pl.BlockSpec((tm, V7X_LANES), lambda i: (n - 1 - i, 0)), _const((N_HEADS, V7X_LANES))],
                 out_shape=[SDS((N_HEADS, s), F32), SDS((s, V7X_LANES), F32), SDS((N_HEADS, V7X_LANES), F32)],
                 scratch=[pltpu.VMEM((N_HEADS, V7X_LANES), F32)], dims=("arbitrary",))(dft, xt)


def _conv_c_fwd(z, cw, name):
    s = z.shape[0]
    c = z.shape[1] // 3
    tm, halo, kw = _tile(s), CONV_C_HALO, CONV_C_WIDTH

    def body(gb_ref, gc_ref, hh_ref, gcp_ref, hhp_ref, cw_ref, y_ref, buf):
        i = pl.program_id(0)
        buf[0:halo, :] = jnp.where(i > 0, gcp_ref[...] * hhp_ref[...], 0.0)
        buf[halo:halo + tm, :] = gc_ref[...] * hh_ref[...]
        c1 = jnp.zeros((tm, c), F32)
        for k in range(kw):
            c1 = c1 + cw_ref[k:k + 1, :] * buf[pl.ds(halo - (kw - 1) + k, tm), :]
        y_ref[...] = (gb_ref[...] * c1).astype(BF16)

    return _call(body, name=name, grid=(s // tm,),
                 in_specs=[_rows(tm, c, 0), _rows(tm, c, 1), _rows(tm, c, 2), _prev_rows(halo, tm, c, 1),
                           _prev_rows(halo, tm, c, 2), _const(cw.shape)],
                 out_specs=_rows(tm, c), out_shape=SDS((s, c), BF16),
                 scratch=[pltpu.VMEM((tm + halo, c), F32)], dims=("parallel",))(z, z, z, z, z, cw)


def _conv_c_bwd(dy0, z, cw, name):
    s = z.shape[0]
    c = z.shape[1] // 3
    tm, halo, kw = _tile(s), CONV_C_HALO, CONV_C_WIDTH
    n = s // tm

    def body(dy_ref, dyn_ref, gb_ref, gbn_ref, gc_ref, hh_ref, gcp_ref, hhp_ref, cw_ref, dz_ref, dcw_ref, buf, bd):
        i = pl.program_id(0)
        gcv, hhv, dyv = gc_ref[...], hh_ref[...], dy_ref[...]
        buf[0:halo, :] = jnp.where(i > 0, gcp_ref[...] * hhp_ref[...], 0.0)
        buf[halo:halo + tm, :] = gcv * hhv
        dc1 = dyv * gb_ref[...]
        bd[0:tm, :] = dc1
        bd[tm:tm + halo, :] = jnp.where(i < n - 1, dyn_ref[...] * gbn_ref[...], 0.0)

        @pl.when(i == 0)
        def _():
            dcw_ref[...] = jnp.zeros_like(dcw_ref)

        c1 = jnp.zeros((tm, c), F32)
        dc0 = jnp.zeros((tm, c), F32)
        for k in range(kw):
            shifted = buf[pl.ds(halo - (kw - 1) + k, tm), :]
            c1 = c1 + cw_ref[k:k + 1, :] * shifted
            dc0 = dc0 + cw_ref[k:k + 1, :] * bd[pl.ds(kw - 1 - k, tm), :]
            dcw_ref[k:k + 1, :] += jnp.sum(dc1 * shifted, axis=0, keepdims=True)
        dz_ref[:, 0:c] = (dyv * c1).astype(BF16)
        dz_ref[:, c:2 * c] = (dc0 * hhv).astype(BF16)
        dz_ref[:, 2 * c:3 * c] = (dc0 * gcv).astype(BF16)

    return _call(body, name=name, grid=(n,),
                 in_specs=[_rows(tm, c), _next_rows(halo, tm, c, 0, s // halo), _rows(tm, c, 0),
                           _next_rows(halo, tm, c, 0, s // halo), _rows(tm, c, 1), _rows(tm, c, 2),
                           _prev_rows(halo, tm, c, 1), _prev_rows(halo, tm, c, 2), _const(cw.shape)],
                 out_specs=[_rows(tm, 3 * c), _const(cw.shape)],
                 out_shape=[SDS((s, 3 * c), BF16), SDS(cw.shape, F32)],
                 scratch=[pltpu.VMEM((tm + halo, c), F32)] * 2, dims=("arbitrary",),
                 vmem_mb=48)(dy0, dy0, z, z, z, z, z, z, cw)


def _loss_head(y, target, name):
    s, d = y.shape
    tm = _tile(s)

    def body(y_ref, t_ref, loss_ref, dy_ref):
        e = y_ref[...] - t_ref[...]

        @pl.when(pl.program_id(0) == 0)
        def _():
            loss_ref[...] = jnp.zeros_like(loss_ref)

        loss_ref[...] += 0.5 * jnp.sum(jnp.mean(e * e, axis=-1, keepdims=True))
        dy_ref[...] = e / d

    return _call(body, name=name, grid=(s // tm,), in_specs=[_rows(tm, d), _rows(tm, d)],
                 out_specs=[_const((8, V7X_LANES)), _rows(tm, d)],
                 out_shape=[SDS((8, V7X_LANES), F32), SDS((s, d), F32)], dims=("arbitrary",))(y, target)


def _adamw(w, g, m, v, name):
    r, c = w.shape
    tr = next((t for t in (512, 256, 128, 64, 32, 16, 8) if r % t == 0), r)

    def body(w_ref, g_ref, m_ref, v_ref, d_ref, mo_ref, vo_ref):
        gv = g_ref[...]
        mn = ADAM_B1 * m_ref[...] + (1.0 - ADAM_B1) * gv
        vn = ADAM_B2 * v_ref[...] + (1.0 - ADAM_B2) * (gv * gv)
        m_hat = mn / (1.0 - ADAM_B1 ** ADAM_STEP)
        v_hat = vn / (1.0 - ADAM_B2 ** ADAM_STEP)
        d_ref[...] = -ADAM_LR * (m_hat / (jnp.sqrt(v_hat) + ADAM_EPS) + ADAM_WD * w_ref[...])
        mo_ref[...] = mn
        vo_ref[...] = vn

    spec = _rows(tr, c)
    return _call(body, name=name, grid=(r // tr,), in_specs=[spec] * 4, out_specs=[spec] * 3,
                 out_shape=[SDS((r, c), F32)] * 3, dims=("parallel",))(w, g, m, v)


def _position():
    return lax.axis_index("x"), lax.axis_index("y"), lax.axis_index("c")


def _other_chips(x, y):
    return [(1 - x, y), (x, 1 - y), (1 - x, 1 - y)]


def _dev_index(px, py, pc):
    return 4 * px + 2 * py + pc


def _all_gather(wloc):
    r, d = wloc.shape

    def body(x_ref, out_ref, send_sems, recv_sems, local_sem):
        start, forward, finish = _gather_phases(x_ref, out_ref, send_sems, recv_sems, local_sem)
        start()
        forward()
        finish()

    return _call(body, name="all_gather_weights", in_specs=[ANY], out_specs=ANY,
                 out_shape=SDS((N_DEV, r, d), wloc.dtype), scratch=GATHER_SEMS)(wloc)


GATHER_SEMS = [pltpu.SemaphoreType.DMA((7,)), pltpu.SemaphoreType.DMA((7,)), pltpu.SemaphoreType.DMA((1,))]


def _gather_phases(x_ref, out_ref, send_sems, recv_sems, local_sem):
    x, y, c = _position()
    me, sibling = (x, y, c), (x, y, 1 - c)
    chips = _other_chips(x, y)

    def slot(dev):
        return out_ref.at[_dev_index(*dev)]

    def copy(k, block, to, src=None):
        return pltpu.make_async_remote_copy(
            src_ref=slot(block) if src is None else src, dst_ref=slot(block),
            send_sem=send_sems.at[k], recv_sem=recv_sems.at[k], device_id=to, device_id_type=MESH)

    mine = pltpu.make_async_copy(x_ref, slot(me), local_sem.at[0])
    first = [copy(0, me, sibling, src=x_ref)] + [copy(1 + j, me, (*chip, c), src=x_ref) for j, chip in enumerate(chips)]
    passed = [copy(4 + j, (*chip, c), sibling) for j, chip in enumerate(chips)]

    def start():
        mine.start()
        for cp in first:
            cp.start()

    def forward():
        for j, chip in enumerate(chips):
            copy(1 + j, (*chip, c), me).wait_recv()
            passed[j].start()

    def finish():
        copy(0, sibling, me).wait_recv()
        for j, chip in enumerate(chips):
            copy(4 + j, (*chip, 1 - c), me).wait_recv()
        for cp in first + passed:
            cp.wait_send()
        mine.wait()

    return start, forward, finish


def _row_block(r):
    return next(t for t in range(704, 0, -BF16_ROWS) if r % t == 0)


def _pair_exchange(gall, name):
    def body(g_ref, out_ref, send_sems, recv_sems):
        start, finish = _pair_exchange_phases(g_ref, out_ref, send_sems, recv_sems)
        start()
        finish()

    return _call(body, name=name, in_specs=[ANY], out_specs=ANY, out_shape=_pair_exchange_shape(gall),
                 scratch=PAIR_EXCHANGE_SEMS)(gall)


PAIR_EXCHANGE_SEMS = [pltpu.SemaphoreType.DMA((4,)), pltpu.SemaphoreType.DMA((4,))]


def _pair_exchange_shape(gall):
    return SDS((4,) + gall.shape[1:], gall.dtype)


def _pair_exchange_phases(g_ref, out_ref, send_sems, recv_sems):
    x, y, c = _position()
    sibling = (x, y, 1 - c)
    dests = [sibling] + [(*chip, 1 - c) for chip in _other_chips(x, y)]
    copies = [pltpu.make_async_remote_copy(
        src_ref=g_ref.at[_dev_index(*dest)], dst_ref=out_ref.at[k], send_sem=send_sems.at[k],
        recv_sem=recv_sems.at[k], device_id=sibling, device_id_type=MESH) for k, dest in enumerate(dests)]

    def start():
        for cp in copies:
            cp.start()

    def finish():
        for cp in copies:
            cp.wait()

    return start, finish


def _pair_sum(gall, sib, idx, name):
    _, r, d = gall.shape
    tr = _row_block(r)

    def body(idx_ref, a_ref, b_ref, o_ref):
        o_ref[...] = (a_ref[...].astype(F32) + b_ref[...].astype(F32)).astype(o_ref.dtype)

    grid_spec = pltpu.PrefetchScalarGridSpec(
        num_scalar_prefetch=1, grid=(4, r // tr),
        in_specs=[pl.BlockSpec((1, tr, d), lambda k, i, idx_ref: (idx_ref[k], i, 0)),
                  pl.BlockSpec((1, tr, d), lambda k, i, idx_ref: (k, i, 0))],
        out_specs=pl.BlockSpec((1, tr, d), lambda k, i, idx_ref: (k, i, 0)))
    return pl.pallas_call(body, name=name, grid_spec=grid_spec,
                          out_shape=_in_hbm(SDS((4, r, d), gall.dtype)),
                          compiler_params=pltpu.CompilerParams(dimension_semantics=("parallel", "parallel")))(
        idx, _keep_in_hbm(gall), _keep_in_hbm(sib))


def _chip_exchange(tsum):
    _, r, d = tsum.shape

    def body(t_ref, out_ref, send_sems, recv_sems):
        start, finish = _chip_exchange_phases(t_ref, out_ref, send_sems, recv_sems)
        start()
        finish()

    return _call(body, name="reduce_scatter_chip_exchange", in_specs=[ANY], out_specs=ANY,
                 out_shape=SDS((3, r, d), tsum.dtype), scratch=EXCHANGE_SEMS)(tsum)


EXCHANGE_SEMS = [pltpu.SemaphoreType.DMA((3,)), pltpu.SemaphoreType.DMA((3,))]


def _chip_exchange_phases(t_ref, out_ref, send_sems, recv_sems):
    x, y, c = _position()
    copies = [pltpu.make_async_remote_copy(
        src_ref=t_ref.at[1 + k], dst_ref=out_ref.at[k], send_sem=send_sems.at[k], recv_sem=recv_sems.at[k],
        device_id=(*chip, c), device_id_type=MESH) for k, chip in enumerate(_other_chips(x, y))]

    def start():
        for cp in copies:
            cp.start()

    def finish():
        for cp in copies:
            cp.wait()

    return start, finish


def _chip_exchange_shape(tsum):
    return SDS((3,) + tsum.shape[1:], tsum.dtype)


EXCHANGES = dict(pair=(_pair_exchange_phases, _pair_exchange_shape, PAIR_EXCHANGE_SEMS),
                 chip=(_chip_exchange_phases, _chip_exchange_shape, EXCHANGE_SEMS))


def _final_sum(tsum, rcv, name):
    _, r, d = tsum.shape
    tr = _row_block(r)

    def body(t_ref, r_ref, o_ref):
        acc = t_ref[0].astype(F32)
        for k in range(3):
            acc = acc + r_ref[k].astype(F32)
        o_ref[...] = acc

    return _call(body, name=name, grid=(r // tr,),
                 in_specs=[pl.BlockSpec((1, tr, d), lambda i: (0, i, 0)), pl.BlockSpec((3, tr, d), lambda i: (0, i, 0))],
                 out_specs=_rows(tr, d), out_shape=SDS((r, d), F32), dims=("parallel",))(tsum, rcv)


def _all_reduce_small(buf):
    nr, lanes = buf.shape

    def body(b_ref, out_ref, gath, send_sems, recv_sems):
        x, y, c = _position()
        my_slot = _dev_index(x, y, c)
        gath[my_slot] = b_ref[...]
        copies = []
        for k in range(1, N_DEV):
            dx, dy, dc = (k >> 2) & 1, (k >> 1) & 1, k & 1
            peer = (1 - x if dx else x, 1 - y if dy else y, 1 - c if dc else c)
            copies.append(pltpu.make_async_remote_copy(
                src_ref=b_ref, dst_ref=gath.at[my_slot], send_sem=send_sems.at[k - 1], recv_sem=recv_sems.at[k - 1],
                device_id=peer, device_id_type=MESH))
        for cp in copies:
            cp.start()
        for cp in copies:
            cp.wait()
        acc = gath[0]
        for sidx in range(1, N_DEV):
            acc = acc + gath[sidx]
        out_ref[...] = acc

    return _call(body, name="all_reduce_small", in_specs=[VMEM], out_specs=VMEM, out_shape=SDS((nr, lanes), F32),
                 scratch=[pltpu.VMEM((N_DEV, nr, lanes), F32), pltpu.SemaphoreType.DMA((7,)),
                          pltpu.SemaphoreType.DMA((7,))])(buf)


def _ffn_block_fwd(x, gain, wall, offs, fs, tag, gather_src=None):
    res = _ffn_fwd(x, gain, wall, offs, fs, f"{tag}_fwd", gather_src)
    out, xn, g, u, h = res[:5]
    return out, (x, gain, xn, g, u, h), (res[5] if gather_src is not None else None)


def _ffn_block_bwd(dout, saved, wall, offs, fs, tag, exchange_src=None):
    x, gain, xn, g, u, h = saved
    res = _ffn_bwd_act(dout, g, u, wall, offs, fs, f"{tag}_bwd_act", exchange_src)
    dg, du, dy_b, dxn = res[:4]
    dwg = _mm_tn(dg, xn, f"{tag}_dwg", BF16)
    dwu = _mm_tn(du, xn, f"{tag}_dwu", BF16)
    dwd = _mm_tn(h, dy_b, f"{tag}_dwd", BF16)
    dx, dgain = _rmsnorm_bwd(x, gain, dxn, dout, f"{tag}_norm_bwd")
    return dx, (dwg, dwu, dwd), dgain, (res[4] if exchange_src is not None else None)


def _local_step(x, target, wall_a, fs, small, plan):
    grads = {}
    first, second, third = (0, fs, 2 * fs), (3 * fs, 4 * fs, 5 * fs), (6 * fs, 7 * fs, 8 * fs)
    reduces = "pair_sum" in plan

    x1, s_f1a, wall_b = _ffn_block_fwd(x, small["ffn1_norm"][0], wall_a, first, fs, "l0_ffn1", plan.get("shard_b"))
    wall_b = plan.get("wall_b", wall_b)
    mixw = plan["mix_b"](wall_b)
    hn0 = _rmsnorm_fwd(x1, small["mix_norm"][0], "l0_mix_norm")
    z = _mm(hn0, mixw["ev_w_main_t"], "nt", "ev_in_proj")
    flog = _mm(hn0, mixw["ev_w_f_t"], "nt", "ev_in_proj_gate")
    a, a1 = _conv_a_fwd(z, small["ev_conv_w32"], small["ev_conv_b"], small["ev_conv_norm"], "ev_conv_fwd")
    qs, kh, vb, fb, ft, xt = _qk_fwd(z, flog, small["ev_b_f128"], small["ev_q_norm2"], small["ev_k_norm2"], "ev_qk_fwd")
    if "wall_c" in plan:
        o, lse = _attn_fwd(qs, kh, vb, fb, ft, "ev_attn_fwd")
        wall_c = plan["wall_c"]
    else:
        o, lse, wall_c = _attn_fwd(qs, kh, vb, fb, ft, "ev_attn_fwd", gather_src=plan["shard_c"])
    mixw = {**mixw, **plan["mix_c"](wall_c)}
    ao = jnp.concatenate([a, o.astype(BF16)], axis=1)
    x2 = _mm(ao, mixw["ev_w_out"], "nn", "ev_out_proj", add=x1)
    x3, s_f2a, _ = _ffn_block_fwd(x2, small["ffn2_norm"][0], wall_c, first, fs, "l0_ffn2")

    x4, s_f1b, _ = _ffn_block_fwd(x3, small["ffn1_norm"][1], wall_c, second, fs, "l1_ffn1")
    hn1 = _rmsnorm_fwd(x4, small["mix_norm"][1], "l1_mix_norm")
    zo = _mm(hn1, mixw["od_w_in_t"], "nt", "od_in_proj")
    y0 = _conv_c_fwd(zo, small["od_conv_w8"], "od_conv_fwd")
    x5 = _mm(y0, mixw["od_w_out"], "nn", "od_out_proj", add=x4)
    x6, s_f2b, _ = _ffn_block_fwd(x5, small["ffn2_norm"][1], wall_c, third, fs, "l1_ffn2")

    loss, d6 = _loss_head(x6, target, "loss_head")

    d5, grads["l1_ffn2"], grads["ffn2_norm_1"], _ = _ffn_block_bwd(d6, s_f2b, wall_c, third, fs, "l1_ffn2")
    d5b = d5.astype(BF16)
    dy0 = _mm(d5b, mixw["od_w_out"], "nt", "od_out_proj_bwd")
    grads["od_w_out"] = _mm_tn(y0, d5b, "od_dw_out", BF16)
    dzo, grads["od_conv_w"] = _conv_c_bwd(dy0, zo, small["od_conv_w8"], "od_conv_bwd")
    dh1 = _mm(dzo, mixw["od_w_in_t"], "nn", "od_in_proj_bwd")
    grads["od_w_in_t"] = _mm_tn(dzo, hn1, "od_dw_in", BF16)
    d4, grads["mix_norm_1"] = _rmsnorm_bwd(x4, small["mix_norm"][1], dh1, d5, "l1_mix_norm_bwd")
    d3, grads["l1_ffn1"], grads["ffn1_norm_1"], _ = _ffn_block_bwd(d4, s_f1b, wall_c, second, fs, "l1_ffn1")

    d2, grads["l0_ffn2"], grads["ffn2_norm_0"], _ = _ffn_block_bwd(d3, s_f2a, wall_c, first, fs, "l0_ffn2")
    partials_c = plan["partials_c"](grads) if reduces else None
    d2b = d2.astype(BF16)
    dao = _mm(d2b, mixw["ev_w_out"], "nt", "ev_out_proj_bwd")
    grads["ev_w_out"] = _mm_tn(ao, d2b, "ev_dw_out", BF16)
    da1, grads["ev_conv_norm"], grads["ev_conv_b"] = _conv_a_bwd_norm(dao, a1, small["ev_conv_norm"], "ev_conv_bwd_norm")
    res = _conv_a_bwd_conv(da1, z, small["ev_conv_w32"], "ev_conv_bwd_conv",
                           exchange_src=("pair", partials_c) if reduces else None)
    du, dg, grads["ev_conv_w"] = res[:3]
    sums_c = plan["pair_sum"](partials_c, res[3], "c") if reduces else None
    res = _attn_bwd(qs, kh, vb, fb, ft, lse, o, dao, "ev_attn_bwd", exchange_src=sums_c)
    dqs, rs, dkh, dv, df4 = res[:5]
    if reduces:
        grads["pair_sums_c"], grads["exchanged_c"] = sums_c, res[5]
    dq, dk, dvb, grads["ev_q_norm"], grads["ev_k_norm"] = _qk_bwd(
        z, dqs, dkh, dv, small["ev_q_norm2"], small["ev_k_norm2"], "ev_qk_bwd")
    dft = df4[:, 0:2, :].reshape(N_HEADS, -1) + rs.reshape(-1, N_HEADS, HEAD_DIM)[:, :, 0].T
    dxt, dflog, grads["ev_b_f"] = _gate_bwd(dft, xt, "ev_gate_bwd")
    dz = jnp.concatenate([du, dg, dq, dk, dvb], axis=1)
    dflog_b = dflog.astype(BF16)
    dh0 = _mm(dz, mixw["ev_w_main_t"], "nn", "ev_in_proj_bwd")
    dh0 = _mm(dflog_b, mixw["ev_w_f_t"], "nn", "ev_in_proj_gate_bwd", add=dh0)
    dw_main = _mm_tn(dz, hn0, "ev_dw_in", BF16)
    dw_f = _mm(dxt.astype(BF16), hn0, "nn", "ev_dw_in_gate", BF16)
    grads["ev_w_in_t"] = jnp.concatenate([dw_main, dw_f], axis=0)
    d1, grads["mix_norm_0"] = _rmsnorm_bwd(x1, small["mix_norm"][0], dh0, d2, "l0_mix_norm_bwd")
    sums_b = None
    if reduces:
        partials_b = plan["partials_b"](grads)
        sums_b = plan["pair_sum"](partials_b, _pair_exchange(partials_b, "reduce_scatter_pair_exchange_b"), "b")
    d0, grads["l0_ffn1"], grads["ffn1_norm_0"], exchanged_b = _ffn_block_bwd(
        d1, s_f1a, wall_a, first, fs, "l0_ffn1", exchange_src=("chip", sums_b) if reduces else None)
    if reduces:
        grads["pair_sums_b"], grads["exchanged_b"] = sums_b, exchanged_b
    return loss, d0, grads


def _round_up(n, m):
    return -(-n // m) * m


def _pad_rows(a, rows):
    return jnp.pad(a, ((0, rows - a.shape[0]), (0, 0)))


SMALL_ORDER = ("loss", "ffn1_norm", "mix_norm", "ffn2_norm", "ev_b_f", "ev_conv_b", "ev_conv_norm",
               "ev_q_norm", "ev_k_norm", "ev_conv_w", "od_conv_w")


def _pack_small(parts):
    flat = jnp.concatenate([parts[k].reshape(-1).astype(F32) for k in SMALL_ORDER])
    n = _round_up(flat.shape[0], 8 * V7X_LANES)
    return jnp.pad(flat, (0, n - flat.shape[0])).reshape(-1, V7X_LANES)


def _unpack_small(buf, shapes):
    flat = buf.reshape(-1)
    out, pos = {}, 0
    for k in SMALL_ORDER:
        n = math.prod(shapes[k])
        out[k] = flat[pos:pos + n].reshape(shapes[k])
        pos += n
    return out


def kernel(x, ffn1_norm, ffn1_w_gate, ffn1_w_up, ffn1_w_down, mix_norm, ffn2_norm, ffn2_w_gate, ffn2_w_up, ffn2_w_down, ev_w_in, ev_b_f, ev_conv_w, ev_conv_b, ev_conv_norm, ev_q_norm, ev_k_norm, ev_w_out, od_w_in, od_conv_w, od_w_out, loss_target, m_ffn1_norm, m_ffn1_w_gate, m_ffn1_w_up, m_ffn1_w_down, m_mix_norm, m_ffn2_norm, m_ffn2_w_gate, m_ffn2_w_up, m_ffn2_w_down, m_ev_w_in, m_ev_b_f, m_ev_conv_w, m_ev_conv_b, m_ev_conv_norm, m_ev_q_norm, m_ev_k_norm, m_ev_w_out, m_od_w_in, m_od_conv_w, m_od_w_out, v_ffn1_norm, v_ffn1_w_gate, v_ffn1_w_up, v_ffn1_w_down, v_mix_norm, v_ffn2_norm, v_ffn2_w_gate, v_ffn2_w_up, v_ffn2_w_down, v_ev_w_in, v_ev_b_f, v_ev_conv_w, v_ev_conv_b, v_ev_conv_norm, v_ev_q_norm, v_ev_k_norm, v_ev_w_out, v_od_w_in, v_od_conv_w, v_od_w_out):
    weights = dict(ffn1_norm=ffn1_norm, ffn1_w_gate=ffn1_w_gate, ffn1_w_up=ffn1_w_up, ffn1_w_down=ffn1_w_down,
                   mix_norm=mix_norm, ffn2_norm=ffn2_norm, ffn2_w_gate=ffn2_w_gate, ffn2_w_up=ffn2_w_up,
                   ffn2_w_down=ffn2_w_down, ev_w_in=ev_w_in, ev_b_f=ev_b_f, ev_conv_w=ev_conv_w, ev_conv_b=ev_conv_b,
                   ev_conv_norm=ev_conv_norm, ev_q_norm=ev_q_norm, ev_k_norm=ev_k_norm, ev_w_out=ev_w_out,
                   od_w_in=od_w_in, od_conv_w=od_conv_w, od_w_out=od_w_out)
    m_in = dict(ffn1_norm=m_ffn1_norm, ffn1_w_gate=m_ffn1_w_gate, ffn1_w_up=m_ffn1_w_up, ffn1_w_down=m_ffn1_w_down,
                mix_norm=m_mix_norm, ffn2_norm=m_ffn2_norm, ffn2_w_gate=m_ffn2_w_gate, ffn2_w_up=m_ffn2_w_up,
                ffn2_w_down=m_ffn2_w_down, ev_w_in=m_ev_w_in, ev_b_f=m_ev_b_f, ev_conv_w=m_ev_conv_w,
                ev_conv_b=m_ev_conv_b, ev_conv_norm=m_ev_conv_norm, ev_q_norm=m_ev_q_norm, ev_k_norm=m_ev_k_norm,
                ev_w_out=m_ev_w_out, od_w_in=m_od_w_in, od_conv_w=m_od_conv_w, od_w_out=m_od_w_out)
    v_in = dict(ffn1_norm=v_ffn1_norm, ffn1_w_gate=v_ffn1_w_gate, ffn1_w_up=v_ffn1_w_up, ffn1_w_down=v_ffn1_w_down,
                mix_norm=v_mix_norm, ffn2_norm=v_ffn2_norm, ffn2_w_gate=v_ffn2_w_gate, ffn2_w_up=v_ffn2_w_up,
                ffn2_w_down=v_ffn2_w_down, ev_w_in=v_ev_w_in, ev_b_f=v_ev_b_f, ev_conv_w=v_ev_conv_w,
                ev_conv_b=v_ev_conv_b, ev_conv_norm=v_ev_conv_norm, ev_q_norm=v_ev_q_norm, ev_k_norm=v_ev_k_norm,
                ev_w_out=v_ev_w_out, od_w_in=v_od_w_in, od_conv_w=v_od_conv_w, od_w_out=v_od_w_out)
    order = list(weights)

    d = x.shape[-1]
    fs = ffn1_w_gate.shape[2]
    n_in = ev_w_in.shape[2]
    n_in_pad = _round_up(n_in, BF16_ROWS)
    n_out = ev_w_out.shape[1]
    n_od = od_w_in.shape[2]
    d_conv = ev_conv_b.shape[1]
    d_in_even = n_in * N_DEV
    d_main = d_in_even - N_HEADS
    cx, cy, cc = _position()
    me = _dev_index(cx, cy, cc)

    def block(wg, wu, wd, layer):
        return [wg[layer].T, wu[layer].T, wd[layer]]

    def stack(parts):
        return jnp.concatenate([p.astype(BF16) for p in parts], axis=0)

    ffn1, ffn2 = (ffn1_w_gate, ffn1_w_up, ffn1_w_down), (ffn2_w_gate, ffn2_w_up, ffn2_w_down)
    shard_a = stack(block(*ffn1, 0))
    shard_b = stack([_pad_rows(ev_w_in[0].T, n_in_pad), ev_w_out[0]])
    shard_c = stack(block(*ffn2, 0) + block(*ffn1, 1) + block(*ffn2, 1) + [od_w_in[0].T, od_w_out[0]])
    off_ev_in, off_ev_out = 0, n_in_pad
    off_od_in, off_od_out = 9 * fs, 9 * fs + n_od
    wall_a = _all_gather(shard_a)

    def even_mixer_weights(wall_b):
        ev_w_in_t = wall_b[:, off_ev_in:off_ev_in + n_in, :].reshape(d_in_even, d)
        return dict(ev_w_main_t=ev_w_in_t[:d_main], ev_w_f_t=_pad_rows(ev_w_in_t[d_main:], V7X_LANES),
                    ev_w_out=wall_b[:, off_ev_out:off_ev_out + n_out, :].reshape(N_DEV * n_out, d))

    def odd_mixer_weights(wall_c):
        return dict(od_w_in_t=wall_c[:, off_od_in:off_od_in + n_od, :].reshape(N_DEV * n_od, d),
                    od_w_out=wall_c[:, off_od_out:off_od_out + n_out, :].reshape(N_DEV * n_out, d))

    def by_dev(a, rows, pad_to=None):
        a = a.reshape(N_DEV, rows, d)
        return a if pad_to is None else jnp.pad(a, ((0, 0), (0, pad_to - rows), (0, 0)))

    idx = jnp.stack([me] + [_dev_index(*chip, cc) for chip in _other_chips(cx, cy)]).astype(jnp.int32)

    def pair_sum(partials, from_sibling, tag):
        return _pair_sum(partials, from_sibling, idx, f"reduce_scatter_pair_sum_{tag}")

    def ffn_pieces(g, key):
        return [by_dev(t, fs) for t in g[key]]

    conv_shapes = dict(ev_conv_w=(CONV_A_WIDTH, d_conv), od_conv_w=(CONV_C_WIDTH, d))
    zero_small = {k: jnp.zeros(s_, F32) for k, s_ in conv_shapes.items()}
    ev_cw_part = lax.dynamic_update_slice(zero_small["ev_conv_w"], ev_conv_w[0], (0, me * ev_conv_w.shape[2]))
    od_cw_part = lax.dynamic_update_slice(zero_small["od_conv_w"], od_conv_w[0], (0, me * od_conv_w.shape[2]))
    zeros_like_small = {k: jnp.zeros((1,), F32) for k in SMALL_ORDER}
    taps = _unpack_small(_all_reduce_small(_pack_small({**zeros_like_small, "ev_conv_w": ev_cw_part,
                                                        "od_conv_w": od_cw_part})),
                         {**{k: (1,) for k in SMALL_ORDER}, **conv_shapes})
    small = dict(
        ffn1_norm=[ffn1_norm[l][None] for l in range(2)], mix_norm=[mix_norm[l][None] for l in range(2)],
        ffn2_norm=[ffn2_norm[l][None] for l in range(2)],
        ev_conv_w32=_pad_rows(taps["ev_conv_w"], CONV_A_WIDTH + 1), ev_conv_b=ev_conv_b, ev_conv_norm=ev_conv_norm,
        ev_b_f128=jnp.pad(ev_b_f, ((0, 0), (0, V7X_LANES - N_HEADS))),
        ev_q_norm2=jnp.tile(ev_q_norm, (1, 2)), ev_k_norm2=jnp.tile(ev_k_norm, (1, 2)),
        od_conv_w8=_pad_rows(taps["od_conv_w"], 8),
    )

    plan = dict(
        shard_b=shard_b, shard_c=shard_c, mix_b=even_mixer_weights, mix_c=odd_mixer_weights, pair_sum=pair_sum,
        partials_c=lambda g1: jnp.concatenate(
            ffn_pieces(g1, "l0_ffn2") + ffn_pieces(g1, "l1_ffn1") + ffn_pieces(g1, "l1_ffn2")
            + [by_dev(g1["od_w_in_t"], n_od), by_dev(g1["od_w_out"], n_out)], axis=1),
        partials_b=lambda g1: jnp.concatenate(
            [by_dev(g1["ev_w_in_t"], n_in, n_in_pad), by_dev(g1["ev_w_out"], n_out)], axis=1))
    loss_p, grad_x, g = _local_step(x[0], loss_target[0], wall_a, fs, small, plan)

    partials_a = jnp.concatenate(ffn_pieces(g, "l0_ffn1"), axis=1)
    sums_a = pair_sum(partials_a, _pair_exchange(partials_a, "reduce_scatter_pair_exchange_a"), "a")
    gsum_a = _final_sum(sums_a, _chip_exchange(sums_a), "reduce_scatter_final_sum_a")
    gsum_b = _final_sum(g["pair_sums_b"], g["exchanged_b"], "reduce_scatter_final_sum_b")
    gsum_c = _final_sum(g["pair_sums_c"], g["exchanged_c"], "reduce_scatter_final_sum_c")

    grad = {}
    where = dict(ffn1=((gsum_a, 0), (gsum_c, 3 * fs)), ffn2=((gsum_c, 0), (gsum_c, 6 * fs)))
    for blk, places in where.items():
        for wi, kind in enumerate(("gate", "up", "down")):
            rows = [buf[off + wi * fs:off + (wi + 1) * fs] for buf, off in places]
            grad[f"{blk}_w_{kind}"] = jnp.stack(rows if kind == "down" else [r.T for r in rows])
    grad["ev_w_in"] = gsum_b[off_ev_in:off_ev_in + n_in].T[None]
    grad["ev_w_out"] = gsum_b[off_ev_out:off_ev_out + n_out][None]
    grad["od_w_in"] = gsum_c[off_od_in:off_od_in + n_od].T[None]
    grad["od_w_out"] = gsum_c[off_od_out:off_od_out + n_out][None]

    heads = lambda t: t.reshape(N_HEADS, HEAD_DIM).sum(axis=0)
    parts = dict(
        loss=loss_p[0, 0:1],
        ffn1_norm=jnp.stack([g["ffn1_norm_0"][0], g["ffn1_norm_1"][0]]),
        mix_norm=jnp.stack([g["mix_norm_0"][0], g["mix_norm_1"][0]]),
        ffn2_norm=jnp.stack([g["ffn2_norm_0"][0], g["ffn2_norm_1"][0]]),
        ev_b_f=g["ev_b_f"][:, 0], ev_conv_b=g["ev_conv_b"], ev_conv_norm=g["ev_conv_norm"],
        ev_q_norm=heads(g["ev_q_norm"]), ev_k_norm=heads(g["ev_k_norm"]),
        ev_conv_w=g["ev_conv_w"][:CONV_A_WIDTH], od_conv_w=g["od_conv_w"][:CONV_C_WIDTH])
    small_shapes = dict(loss=(1,), ffn1_norm=ffn1_norm.shape, mix_norm=mix_norm.shape, ffn2_norm=ffn2_norm.shape,
                        ev_b_f=ev_b_f.shape, ev_conv_b=ev_conv_b.shape, ev_conv_norm=ev_conv_norm.shape,
                        ev_q_norm=ev_q_norm.shape, ev_k_norm=ev_k_norm.shape, **conv_shapes)
    red = _unpack_small(_all_reduce_small(_pack_small(parts)), small_shapes)
    loss = red["loss"][0]
    for k in ("ffn1_norm", "mix_norm", "ffn2_norm", "ev_b_f", "ev_conv_b", "ev_conv_norm", "ev_q_norm", "ev_k_norm"):
        grad[k] = red[k]
    grad["ev_conv_w"] = lax.dynamic_slice(red["ev_conv_w"], (0, me * ev_conv_w.shape[2]),
                                          (CONV_A_WIDTH, ev_conv_w.shape[2]))[None]
    grad["od_conv_w"] = lax.dynamic_slice(red["od_conv_w"], (0, me * od_conv_w.shape[2]),
                                          (CONV_C_WIDTH, od_conv_w.shape[2]))[None]

    big = ("ffn1_w_gate", "ffn1_w_up", "ffn1_w_down", "ffn2_w_gate", "ffn2_w_up", "ffn2_w_down",
           "ev_w_in", "ev_w_out", "od_w_in", "od_w_out")
    delta, new_m, new_v = {}, {}, {}
    for k in big:
        shp = weights[k].shape
        flat = lambda t: t.reshape(-1, shp[-1])
        dk, mk, vk = _adamw(flat(weights[k]), flat(grad[k]), flat(m_in[k]), flat(v_in[k]), f"adamw_{k}")
        delta[k], new_m[k], new_v[k] = dk.reshape(shp), mk.reshape(shp), vk.reshape(shp)
    rest = [k for k in order if k not in big]
    cat = lambda src: jnp.concatenate([src[k].reshape(-1) for k in rest])
    n_small = sum(math.prod(weights[k].shape) for k in rest)
    n_pad = _round_up(n_small, 8 * V7X_LANES)
    as_rows = lambda t: jnp.pad(t, (0, n_pad - n_small)).reshape(-1, V7X_LANES)
    v_rows = jnp.pad(cat(v_in), (0, n_pad - n_small), constant_values=1.0).reshape(-1, V7X_LANES)
    ds, ms, vs = _adamw(as_rows(cat(weights)), as_rows(cat(grad)), as_rows(cat(m_in)), v_rows, "adamw_small")
    pos = 0
    for k in rest:
        n = math.prod(weights[k].shape)
        for dst, src in ((delta, ds), (new_m, ms), (new_v, vs)):
            dst[k] = src.reshape(-1)[pos:pos + n].reshape(weights[k].shape)
        pos += n

    return (loss, grad_x[None], *[grad[k] for k in order], *[delta[k] for k in order],
            *[new_m[k] for k in order], *[new_v[k] for k in order])
```

```python
import functools
import math

import jax
import jax.numpy as jnp
from jax import lax
from jax.experimental import pallas as pl
from jax.experimental.pallas import tpu as pltpu

F32 = jnp.float32
BF16 = jnp.bfloat16
SDS = jax.ShapeDtypeStruct
MESH = pl.DeviceIdType.MESH

N_DEV = 8
EPS = 1e-6
FFN_RES = 0.5
HEAD_DIM = 64
N_HEADS = 8
D_ATTN = N_HEADS * HEAD_DIM
N_PAIRS = N_HEADS // 2
PAIR = 2 * HEAD_DIM
ATTN_SCALE = 1.0 / math.sqrt(HEAD_DIM)
CONV_A_WIDTH = 31
CONV_A_HALO = 32
CONV_C_WIDTH = 3
CONV_C_HALO = 8
NEG_BIG = -1e30
ADAM_LR, ADAM_B1, ADAM_B2, ADAM_EPS, ADAM_WD, ADAM_STEP = 0.001, 0.9, 0.999, 1e-08, 0.01, 10

V7X_VMEM_BYTES = 64 * 1024 * 1024
V7X_LANES = 128
BF16_ROWS = 16
MIB = 1024 * 1024

NT = (((1,), (1,)), ((), ()))
TN = (((0,), (0,)), ((), ()))


def _call(body, *, name, out_shape, in_specs, out_specs, grid=(), scratch=(), dims=None, vmem_mb=32, **kw):
    params = dict(vmem_limit_bytes=min(vmem_mb * MIB, V7X_VMEM_BYTES - 4 * MIB))
    if dims is not None:
        params["dimension_semantics"] = dims
    call = pl.pallas_call(
        body, name=name, grid=grid, in_specs=in_specs, out_specs=out_specs, out_shape=_in_hbm(out_shape),
        scratch_shapes=list(scratch), compiler_params=pltpu.CompilerParams(**params), **kw)
    return lambda *args: call(*[_keep_in_hbm(a) for a in args])


LARGE_OPERAND_BYTES = MIB


def _is_large(a):
    return a.ndim >= 2 and math.prod(a.shape) * jnp.dtype(a.dtype).itemsize >= LARGE_OPERAND_BYTES


def _keep_in_hbm(a):
    return pltpu.with_memory_space_constraint(a, pltpu.HBM) if _is_large(a) else a


def _in_hbm(out_shape):
    one = lambda s: pltpu.HBM(s.shape, s.dtype) if _is_large(s) else s
    return [one(s) for s in out_shape] if isinstance(out_shape, (list, tuple)) else one(out_shape)


def _tile(n, want=512):
    return want if n % want == 0 else n


def _rows(tm, d, col=0):
    return pl.BlockSpec((tm, d), lambda i: (i, col))


def _const(shape):
    return pl.BlockSpec(shape, lambda *_: (0,) * len(shape))


ANY = pl.BlockSpec(memory_space=pl.ANY)
VMEM = pl.BlockSpec(memory_space=pltpu.VMEM)


def _sigmoid(x):
    return 1.0 / (1.0 + jnp.exp(-x))


def _rmsnorm_fwd(x, gain, name):
    s, d = x.shape
    tm = _tile(s)

    def body(x_ref, g_ref, o_ref):
        xv = x_ref[...]
        r = lax.rsqrt(jnp.mean(xv * xv, axis=-1, keepdims=True) + EPS)
        o_ref[...] = (xv * r * g_ref[...]).astype(BF16)

    return _call(body, name=name, grid=(s // tm,), in_specs=[_rows(tm, d), _const((1, d))],
                 out_specs=_rows(tm, d), out_shape=SDS((s, d), BF16), dims=("parallel",))(x, gain)


def _rmsnorm_bwd(x, gain, dxn, dres, name):
    s, d = x.shape
    tm = _tile(s)

    def body(x_ref, g_ref, dxn_ref, dres_ref, dx_ref, dg_ref):
        xv = x_ref[...]
        r = lax.rsqrt(jnp.mean(xv * xv, axis=-1, keepdims=True) + EPS)
        xh = xv * r
        dv = dxn_ref[...]

        @pl.when(pl.program_id(0) == 0)
        def _():
            dg_ref[...] = jnp.zeros_like(dg_ref)

        dg_ref[...] += jnp.sum(dv * xh, axis=0, keepdims=True)
        dxh = dv * g_ref[...]
        dx_ref[...] = dres_ref[...] + r * (dxh - xh * jnp.mean(dxh * xh, axis=-1, keepdims=True))

    return _call(body, name=name, grid=(s // tm,),
                 in_specs=[_rows(tm, d), _const((1, d)), _rows(tm, d), _rows(tm, d)],
                 out_specs=[_rows(tm, d), _const((1, d))],
                 out_shape=[SDS((s, d), F32), SDS((1, d), F32)], dims=("arbitrary",))(x, gain, dxn, dres)


def _col_tile(n):
    for t in (1024, 768, 512, 256, 128):
        if n % t == 0:
            return t
    return n


def _mm(a, b, mode, name, out_dtype=F32, add=None):
    if mode == "tn":
        k, m = a.shape
        n = b.shape[1]
        bm = 256 if m % 256 == 0 else m

        def body_tn(a_ref, b_ref, o_ref):
            o_ref[...] = lax.dot_general(a_ref[...].astype(BF16), b_ref[...].astype(BF16), TN,
                                         preferred_element_type=F32).astype(out_dtype)

        return _call(body_tn, name=name, grid=(m // bm,),
                     in_specs=[pl.BlockSpec((k, bm), lambda i: (0, i)), _const((k, n))],
                     out_specs=pl.BlockSpec((bm, n), lambda i: (i, 0)),
                     out_shape=SDS((m, n), out_dtype), dims=("parallel",), vmem_mb=48)(a, b)
    m, k = a.shape
    n = b.shape[0] if mode == "nt" else b.shape[1]
    tm, tn = _tile(m), _col_tile(n)
    dn = NT if mode == "nt" else (((1,), (0,)), ((), ()))

    def body(a_ref, b_ref, *rest):
        o_ref = rest[-1]
        acc = lax.dot_general(a_ref[...].astype(BF16), b_ref[...].astype(BF16), dn, preferred_element_type=F32)
        if add is not None:
            acc = acc + rest[0][...]
        o_ref[...] = acc.astype(out_dtype)

    b_spec = (pl.BlockSpec((tn, k), lambda i, j: (j, 0)) if mode == "nt"
              else pl.BlockSpec((k, tn), lambda i, j: (0, j)))
    in_specs = [pl.BlockSpec((tm, k), lambda i, j: (i, 0)), b_spec]
    args = [a, b]
    if add is not None:
        in_specs.append(pl.BlockSpec((tm, tn), lambda i, j: (i, j)))
        args.append(add)
    return _call(body, name=name, grid=(m // tm, n // tn), in_specs=in_specs,
                 out_specs=pl.BlockSpec((tm, tn), lambda i, j: (i, j)),
                 out_shape=SDS((m, n), out_dtype), dims=("parallel", "parallel"), vmem_mb=48)(*args)


def _mm_tn(a, b, name, out_dtype=F32):
    return _mm(a, b, "tn", name, out_dtype)


FFN_TM = 256
FFN_FWD_TM = 512
FFN_CHUNK = 256


def _load_ffn_weights(w_hbm, offs, fs, dsts, sems):
    copies = []
    for wi, (off, dst) in enumerate(zip(offs, dsts)):
        for j in range(N_DEV):
            cp = pltpu.make_async_copy(w_hbm.at[j, pl.ds(off, fs), :], dst.at[pl.ds(j * fs, fs), :],
                                       sems.at[wi * N_DEV + j])
            cp.start()
            copies.append(cp)
    for cp in copies:
        cp.wait()


def _ffn_fwd(x, gain, wall, offs, fs, name, gather_src=None):
    s, d = x.shape
    f = fs * N_DEV
    tm, ch = _tile(s, FFN_FWD_TM), FFN_CHUNK
    n = s // tm
    gathers = gather_src is not None

    def body(x_ref, gain_ref, w_hbm, *rest):
        if gathers:
            (src_hbm, out_ref, xn_ref, g_ref, u_ref, h_ref, gathered, wg_s, wu_s, wd_s, sems,
             send_sems, recv_sems, local_sem) = rest
            start, forward, finish = _gather_phases(src_hbm, gathered, send_sems, recv_sems, local_sem)
            pl.when(pl.program_id(0) == 0)(start)
            pl.when(pl.program_id(0) == (3 * n) // 4)(forward)
        else:
            out_ref, xn_ref, g_ref, u_ref, h_ref, wg_s, wu_s, wd_s, sems = rest

        @pl.when(pl.program_id(0) == 0)
        def _():
            _load_ffn_weights(w_hbm, offs, fs, (wg_s, wu_s, wd_s), sems)

        xv = x_ref[...]
        xnv = (xv * lax.rsqrt(jnp.mean(xv * xv, axis=-1, keepdims=True) + EPS) * gain_ref[...]).astype(BF16)
        xn_ref[...] = xnv
        acc = jnp.zeros((tm, d), F32)
        for c in range(f // ch):
            sl = slice(c * ch, (c + 1) * ch)
            gb = lax.dot_general(xnv, wg_s[sl, :], NT, preferred_element_type=F32).astype(BF16)
            ub = lax.dot_general(xnv, wu_s[sl, :], NT, preferred_element_type=F32).astype(BF16)
            g_ref[:, sl] = gb
            u_ref[:, sl] = ub
            g = gb.astype(F32)
            hb = (g * _sigmoid(g) * ub.astype(F32)).astype(BF16)
            h_ref[:, sl] = hb
            acc = acc + jnp.dot(hb, wd_s[sl, :], preferred_element_type=F32)
        out_ref[...] = xv + FFN_RES * acc
        if gathers:
            pl.when(pl.program_id(0) == n - 1)(finish)

    in_specs, args = [_rows(tm, d), _const((1, d)), ANY], [x, gain, wall]
    out_specs = [_rows(tm, d), _rows(tm, d), _rows(tm, f), _rows(tm, f), _rows(tm, f)]
    out_shape = [SDS((s, d), F32), SDS((s, d), BF16), SDS((s, f), BF16), SDS((s, f), BF16), SDS((s, f), BF16)]
    scratch = [pltpu.VMEM((f, d), BF16)] * 3 + [pltpu.SemaphoreType.DMA((3 * N_DEV,))]
    if gathers:
        in_specs.append(ANY)
        args.append(gather_src)
        out_specs.append(ANY)
        out_shape.append(SDS((N_DEV,) + gather_src.shape, gather_src.dtype))
        scratch += GATHER_SEMS
    return _call(body, name=name, grid=(n,), in_specs=in_specs, out_specs=out_specs, out_shape=out_shape,
                 scratch=scratch, dims=("arbitrary",), vmem_mb=56)(*args)


def _ffn_bwd_act(dout, g, u, wall, offs, fs, name, exchange_src=None):
    s, d = dout.shape
    f = fs * N_DEV
    tm, ch = _tile(s, FFN_TM), FFN_CHUNK
    n = s // tm
    exchanges = exchange_src is not None
    if exchanges:
        phases_of, shape_of, exchange_sems = EXCHANGES[exchange_src[0]]

    def body(dout_ref, g_ref, u_ref, w_hbm, *rest):
        if exchanges:
            t_hbm, dg_ref, du_ref, dy_ref, dxn_ref, rcv_ref, wg_s, wu_s, wd_s, sems, send_sems, recv_sems = rest
            start, finish = phases_of(t_hbm, rcv_ref, send_sems, recv_sems)
            pl.when(pl.program_id(0) == 0)(start)
        else:
            dg_ref, du_ref, dy_ref, dxn_ref, wg_s, wu_s, wd_s, sems = rest

        @pl.when(pl.program_id(0) == 0)
        def _():
            _load_ffn_weights(w_hbm, offs, fs, (wg_s, wu_s, wd_s), sems)

        dy = (FFN_RES * dout_ref[...]).astype(BF16)
        dy_ref[...] = dy
        acc = jnp.zeros((tm, d), F32)
        for c in range(f // ch):
            sl = slice(c * ch, (c + 1) * ch)
            dh = lax.dot_general(dy, wd_s[sl, :], NT, preferred_element_type=F32)
            gv = g_ref[:, sl].astype(F32)
            uv = u_ref[:, sl].astype(F32)
            sg = _sigmoid(gv)
            dgb = (dh * uv * sg * (1.0 + gv * (1.0 - sg))).astype(BF16)
            dub = (dh * gv * sg).astype(BF16)
            dg_ref[:, sl] = dgb
            du_ref[:, sl] = dub
            acc = acc + jnp.dot(dgb, wg_s[sl, :], preferred_element_type=F32)
            acc = acc + jnp.dot(dub, wu_s[sl, :], preferred_element_type=F32)
        dxn_ref[...] = acc
        if exchanges:
            pl.when(pl.program_id(0) == n - 1)(finish)

    in_specs, args = [_rows(tm, d), _rows(tm, f), _rows(tm, f), ANY], [dout, g, u, wall]
    out_specs = [_rows(tm, f), _rows(tm, f), _rows(tm, d), _rows(tm, d)]
    out_shape = [SDS((s, f), BF16), SDS((s, f), BF16), SDS((s, d), BF16), SDS((s, d), F32)]
    scratch = [pltpu.VMEM((f, d), BF16)] * 3 + [pltpu.SemaphoreType.DMA((3 * N_DEV,))]
    if exchanges:
        in_specs.append(ANY)
        args.append(exchange_src[1])
        out_specs.append(ANY)
        out_shape.append(shape_of(exchange_src[1]))
        scratch += exchange_sems
    return _call(body, name=name, grid=(n,), in_specs=in_specs, out_specs=out_specs, out_shape=out_shape,
                 scratch=scratch, dims=("arbitrary",), vmem_mb=56)(*args)


def _prev_rows(halo, tm, c, col):
    return pl.BlockSpec((halo, c), lambda i: (jnp.maximum(i * (tm // halo) - 1, 0), col))


def _next_rows(halo, tm, c, col, n_blocks):
    return pl.BlockSpec((halo, c), lambda i: (jnp.minimum((i + 1) * (tm // halo), n_blocks - 1), col))


def _conv_a_fwd(z, cw, cb, cn, name):
    s = z.shape[0]
    c = cb.shape[1]
    tm, halo, kw = _tile(s), CONV_A_HALO, CONV_A_WIDTH

    def body(u_ref, g_ref, up_ref, gp_ref, cw_ref, cb_ref, cn_ref, a_ref, a1_ref, buf):
        i = pl.program_id(0)
        buf[0:halo, :] = jnp.where(i > 0, up_ref[...] * _sigmoid(gp_ref[...]), 0.0)
        buf[halo:halo + tm, :] = u_ref[...] * _sigmoid(g_ref[...])
        acc = jnp.zeros((tm, c), F32)
        for k in range(kw):
            acc = acc + cw_ref[k:k + 1, :] * buf[pl.ds(halo - (kw - 1) + k, tm), :]
        a1 = acc + cb_ref[...]
        a1_ref[...] = a1
        a2 = a1 * lax.rsqrt(jnp.mean(a1 * a1, axis=-1, keepdims=True) + EPS) * cn_ref[...]
        a_ref[...] = (a2 * _sigmoid(a2)).astype(BF16)

    return _call(body, name=name, grid=(s // tm,),
                 in_specs=[_rows(tm, c, 0), _rows(tm, c, 1), _prev_rows(halo, tm, c, 0), _prev_rows(halo, tm, c, 1),
                           _const(cw.shape), _const((1, c)), _const((1, c))],
                 out_specs=[_rows(tm, c), _rows(tm, c)],
                 out_shape=[SDS((s, c), BF16), SDS((s, c), F32)],
                 scratch=[pltpu.VMEM((tm + halo, c), F32)], dims=("parallel",))(z, z, z, z, cw, cb, cn)


def _conv_a_bwd_norm(dao, a1, cn, name):
    s, c = a1.shape
    tm = _tile(s)

    def body(da_ref, a1_ref, cn_ref, da1_ref, dcn_ref, dcb_ref):
        a1v = a1_ref[...]
        r = lax.rsqrt(jnp.mean(a1v * a1v, axis=-1, keepdims=True) + EPS)
        xh = a1v * r
        a2 = xh * cn_ref[...]
        sg = _sigmoid(a2)
        da2 = da_ref[...] * sg * (1.0 + a2 * (1.0 - sg))
        dxh = da2 * cn_ref[...]
        da1 = r * (dxh - xh * jnp.mean(dxh * xh, axis=-1, keepdims=True))
        da1_ref[...] = da1

        @pl.when(pl.program_id(0) == 0)
        def _():
            dcn_ref[...] = jnp.zeros_like(dcn_ref)
            dcb_ref[...] = jnp.zeros_like(dcb_ref)

        dcn_ref[...] += jnp.sum(da2 * xh, axis=0, keepdims=True)
        dcb_ref[...] += jnp.sum(da1, axis=0, keepdims=True)

    return _call(body, name=name, grid=(s // tm,),
                 in_specs=[_rows(tm, c, 0), _rows(tm, c), _const((1, c))],
                 out_specs=[_rows(tm, c), _const((1, c)), _const((1, c))],
                 out_shape=[SDS((s, c), F32), SDS((1, c), F32), SDS((1, c), F32)], dims=("arbitrary",))(dao, a1, cn)


def _conv_a_bwd_conv(da1, z, cw, name, exchange_src=None):
    s, c = da1.shape
    tm, halo, kw = _tile(s), CONV_A_HALO, CONV_A_WIDTH
    n = s // tm
    exchanges = exchange_src is not None
    if exchanges:
        phases_of, shape_of, exchange_sems = EXCHANGES[exchange_src[0]]

    def body(d_ref, dn_ref, u_ref, g_ref, up_ref, gp_ref, cw_ref, *rest):
        i = pl.program_id(0)
        if exchanges:
            t_hbm, du_ref, dg_ref, dcw_ref, rcv_ref, buf, bd, send_sems, recv_sems = rest
            start, finish = phases_of(t_hbm, rcv_ref, send_sems, recv_sems)
            pl.when(i == 0)(start)
        else:
            du_ref, dg_ref, dcw_ref, buf, bd = rest
        uv = u_ref[...]
        sg = _sigmoid(g_ref[...])
        buf[0:halo, :] = jnp.where(i > 0, up_ref[...] * _sigmoid(gp_ref[...]), 0.0)
        buf[halo:halo + tm, :] = uv * sg
        dv = d_ref[...]
        bd[0:tm, :] = dv
        bd[tm:tm + halo, :] = jnp.where(i < n - 1, dn_ref[...], 0.0)

        @pl.when(i == 0)
        def _():
            dcw_ref[...] = jnp.zeros_like(dcw_ref)

        da0 = jnp.zeros((tm, c), F32)
        for k in range(kw):
            da0 = da0 + cw_ref[k:k + 1, :] * bd[pl.ds(kw - 1 - k, tm), :]
            dcw_ref[k:k + 1, :] += jnp.sum(dv * buf[pl.ds(halo - (kw - 1) + k, tm), :], axis=0, keepdims=True)
        du_ref[...] = (da0 * sg).astype(BF16)
        dg_ref[...] = (da0 * uv * sg * (1.0 - sg)).astype(BF16)
        if exchanges:
            pl.when(i == n - 1)(finish)

    in_specs = [_rows(tm, c), _next_rows(halo, tm, c, 0, s // halo), _rows(tm, c, 0), _rows(tm, c, 1),
                _prev_rows(halo, tm, c, 0), _prev_rows(halo, tm, c, 1), _const(cw.shape)]
    args = [da1, da1, z, z, z, z, cw]
    out_specs = [_rows(tm, c), _rows(tm, c), _const(cw.shape)]
    out_shape = [SDS((s, c), BF16), SDS((s, c), BF16), SDS(cw.shape, F32)]
    scratch = [pltpu.VMEM((tm + halo, c), F32)] * 2
    if exchanges:
        in_specs.append(ANY)
        args.append(exchange_src[1])
        out_specs.append(ANY)
        out_shape.append(shape_of(exchange_src[1]))
        scratch += exchange_sems
    return _call(body, name=name, grid=(n,), in_specs=in_specs, out_specs=out_specs, out_shape=out_shape,
                 scratch=scratch, dims=("arbitrary",))(*args)


def _lane_is_first_head(tm):
    return lax.broadcasted_iota(jnp.int32, (tm, PAIR), 1) < HEAD_DIM


def _pair_rms(xp, first):
    x2 = xp * xp
    s0 = jnp.sum(jnp.where(first, x2, 0.0), axis=-1, keepdims=True)
    s1 = jnp.sum(jnp.where(first, 0.0, x2), axis=-1, keepdims=True)
    return jnp.where(first, lax.rsqrt(s0 / HEAD_DIM + EPS), lax.rsqrt(s1 / HEAD_DIM + EPS))


def _split3(x):
    hi = x.astype(BF16)
    r1 = x - hi.astype(F32)
    mid = r1.astype(BF16)
    lo = (r1 - mid.astype(F32)).astype(BF16)
    return hi, mid, lo


def _qk_fwd(z, flog, bf, qn2, kn2, name):
    s = z.shape[0]
    tm = _tile(s)
    col0 = (z.shape[1] - 3 * D_ATTN) // D_ATTN

    def body(q_ref, k_ref, v_ref, fl_ref, bf_ref, qn_ref, kn_ref,
             qs_ref, kh_ref, vb_ref, fb_ref, ft_ref, xt_ref, carry):
        i = pl.program_id(0)
        first = _lane_is_first_head(tm)
        for p in range(N_PAIRS):
            sl = slice(p * PAIR, (p + 1) * PAIR)
            q = q_ref[:, sl]
            qs_ref[:, sl] = (q * _pair_rms(q, first) * qn_ref[...] * ATTN_SCALE).astype(BF16)
            k = k_ref[:, sl]
            kh_ref[:, sl] = (k * _pair_rms(k, first) * kn_ref[...]).astype(BF16)
        vb_ref[...] = v_ref[...].astype(BF16)

        xg = fl_ref[...] + bf_ref[...]
        valid = lax.broadcasted_iota(jnp.int32, (tm, V7X_LANES), 1) < N_HEADS
        ls = jnp.where(valid, jnp.minimum(xg, 0.0) - jnp.log(1.0 + jnp.exp(-jnp.abs(xg))), 0.0)
        tri = (lax.broadcasted_iota(jnp.int32, (tm, tm), 1) <= lax.broadcasted_iota(jnp.int32, (tm, tm), 0)).astype(BF16)
        cs = jnp.zeros((tm, V7X_LANES), F32)
        for part in _split3(ls):
            cs = cs + jnp.dot(tri, part, preferred_element_type=F32)

        @pl.when(i == 0)
        def _():
            carry[...] = jnp.zeros_like(carry)

        fv = cs + carry[0:1, :]
        carry[0:1, :] = fv[tm - 1:tm, :]
        ft_ref[...] = fv.T[0:N_HEADS, :]
        xt_ref[...] = xg.T[0:N_HEADS, :]
        for p in range(N_PAIRS):
            fb_ref[:, p * PAIR:(p + 1) * PAIR] = jnp.where(first, fv[:, 2 * p:2 * p + 1], fv[:, 2 * p + 1:2 * p + 2])

    wide = lambda col: pl.BlockSpec((tm, D_ATTN), lambda i: (i, col))
    tcol = pl.BlockSpec((N_HEADS, tm), lambda i: (0, i))
    return _call(body, name=name, grid=(s // tm,),
                 in_specs=[wide(col0), wide(col0 + 1), wide(col0 + 2), _rows(tm, V7X_LANES),
                           _const((1, V7X_LANES)), _const((1, PAIR)), _const((1, PAIR))],
                 out_specs=[wide(0), wide(0), wide(0), wide(0), tcol, tcol],
                 out_shape=[SDS((s, D_ATTN), BF16)] * 3 + [SDS((s, D_ATTN), F32), SDS((N_HEADS, s), F32),
                                                          SDS((N_HEADS, s), F32)],
                 scratch=[pltpu.VMEM((8, V7X_LANES), F32)], dims=("arbitrary",))(z, z, z, flog, bf, qn2, kn2)


ATTN_FWD_SUB = 256
ATTN_BWD_SUB = 512


def _causal_schedule(nq, key_major):
    if key_major:
        pairs = [(i, j) for j in range(nq) for i in range(j, nq)]
    else:
        pairs = [(i, j) for i in range(nq) for j in range(i + 1)]
    return (jnp.asarray([p[0] for p in pairs], jnp.int32), jnp.asarray([p[1] for p in pairs], jnp.int32))


def _sub_scores(qp, kp, ft_row, mine, r, masked, sub, tk):
    qm = jnp.where(mine, qp, jnp.zeros_like(qp))
    s2 = lax.dot_general(qm, kp, NT, preferred_element_type=F32) - ft_row
    if masked:
        row = r * sub + lax.broadcasted_iota(jnp.int32, (sub, tk), 0)
        s2 = jnp.where(lax.broadcasted_iota(jnp.int32, (sub, tk), 1) <= row, s2, NEG_BIG)
    return s2


def _attn_fwd(qs, kh, vb, fb, ft, name, gather_src=None):
    s = qs.shape[0]
    tq = tk = _tile(s)
    nq = s // tq
    sub = min(ATTN_FWD_SUB, tq)
    ii, jj = _causal_schedule(nq, key_major=False)
    n_steps = ii.shape[0]
    gathers = gather_src is not None

    def body(ii_ref, jj_ref, q_ref, k_ref, v_ref, fq_ref, ft_ref, *rest):
        if gathers:
            x_hbm, o_ref, lse_ref, wall_ref, m_s, l_s, acc_s, send_sems, recv_sems, local_sem = rest
        else:
            o_ref, lse_ref, m_s, l_s, acc_s = rest
        p, t = pl.program_id(0), pl.program_id(1)
        i, j = ii_ref[t], jj_ref[t]
        first = _lane_is_first_head(sub)
        if gathers:
            start, forward, finish = _gather_phases(x_hbm, wall_ref, send_sems, recv_sems, local_sem)
            pl.when(jnp.logical_and(p == 0, t == 0))(start)
            pl.when(jnp.logical_and(p == N_PAIRS - 1, t == 0))(forward)

        @pl.when(j == 0)
        def _():
            m_s[...] = jnp.full_like(m_s, NEG_BIG)
            l_s[...] = jnp.zeros_like(l_s)
            acc_s[...] = jnp.zeros_like(acc_s)

        def tile(masked):
            kp, vp = k_ref[...], v_ref[...]
            q_all, fq_all, acc_all = q_ref[...], fq_ref[...], acc_s[...]
            m_all, l_all = (m_s[0], m_s[1]), (l_s[0], l_s[1])
            ft_rows = [ft_ref[pl.ds(2 * p + h, 1), :] for h in range(2)]
            m_out, l_out, acc_out = ([], []), ([], []), []
            for r in range(tq // sub):
                rows = slice(r * sub, (r + 1) * sub)
                qp, fq, acc = q_all[rows, :], fq_all[rows, :], acc_all[rows, :]
                new = []
                for h in range(2):
                    mine = first if h == 0 else jnp.logical_not(first)
                    s2 = _sub_scores(qp, kp, ft_rows[h], mine, r, masked, sub, tk)
                    fqh = fq[:, h * HEAD_DIM:h * HEAD_DIM + 1]
                    m_old = m_all[h][rows, :]
                    m_new = jnp.maximum(m_old, jnp.max(s2, axis=-1, keepdims=True) + fqh)
                    pr = jnp.exp(s2 - (m_new - fqh))
                    alpha = jnp.exp(m_old - m_new)
                    l_out[h].append(alpha * l_all[h][rows, :] + jnp.sum(pr, axis=-1, keepdims=True))
                    m_out[h].append(m_new)
                    new.append(alpha * acc + jnp.dot(pr.astype(BF16), vp, preferred_element_type=F32))
                acc_out.append(jnp.where(first, new[0], new[1]))
            for h in range(2):
                m_s[h] = jnp.concatenate(m_out[h], axis=0)
                l_s[h] = jnp.concatenate(l_out[h], axis=0)
            acc_s[...] = jnp.concatenate(acc_out, axis=0)

        @pl.when(j < i)
        def _():
            tile(False)

        @pl.when(j == i)
        def _():
            tile(True)
            whole = _lane_is_first_head(tq)
            l_pair = jnp.where(whole, l_s[0], l_s[1])
            o_ref[...] = acc_s[...] / l_pair
            lse_ref[...] = jnp.where(whole, m_s[0], m_s[1]) + jnp.log(l_pair)

        if gathers:
            pl.when(jnp.logical_and(p == N_PAIRS - 1, t == n_steps - 1))(finish)

    qblk = pl.BlockSpec((tq, PAIR), lambda p, t, ii_r, jj_r: (ii_r[t], p))
    kblk = pl.BlockSpec((tk, PAIR), lambda p, t, ii_r, jj_r: (jj_r[t], p))
    in_specs = [qblk, kblk, kblk, qblk, pl.BlockSpec((N_HEADS, tk), lambda p, t, ii_r, jj_r: (0, jj_r[t]))]
    out_specs, out_shape = [qblk, qblk], [SDS((s, D_ATTN), F32)] * 2
    scratch = [pltpu.VMEM((2, tq, 1), F32), pltpu.VMEM((2, tq, 1), F32), pltpu.VMEM((tq, PAIR), F32)]
    args = [ii, jj, qs, kh, vb, fb, ft]
    if gathers:
        in_specs.append(ANY)
        out_specs.append(ANY)
        out_shape.append(SDS((N_DEV,) + gather_src.shape, gather_src.dtype))
        scratch += GATHER_SEMS
        args.append(gather_src)
    grid_spec = pltpu.PrefetchScalarGridSpec(num_scalar_prefetch=2, grid=(N_PAIRS, n_steps), in_specs=in_specs,
                                             out_specs=out_specs, scratch_shapes=scratch)
    return pl.pallas_call(
        body, name=name, grid_spec=grid_spec, out_shape=_in_hbm(out_shape),
        compiler_params=pltpu.CompilerParams(dimension_semantics=("arbitrary", "arbitrary"),
                                             vmem_limit_bytes=32 * MIB))(*[_keep_in_hbm(a) for a in args])


def _attn_bwd(qs, kh, vb, fb, ft, lse, o, dao, name, exchange_src=None):
    s = qs.shape[0]
    tq = tk = _tile(s)
    nq = s // tq
    sub = min(ATTN_BWD_SUB, tq)
    ii, jj = _causal_schedule(nq, key_major=True)
    n_steps = ii.shape[0]

    exchanges = exchange_src is not None

    def body(ii_ref, jj_ref, q_ref, k_ref, v_ref, fq_ref, ft_ref, lse_ref, o_ref, do_ref, *rest):
        if exchanges:
            t_hbm, dq_ref, rs_ref, dk_ref, dv_ref, df_ref, rcv_ref, dk_s, dv_s, df_s, send_sems, recv_sems = rest
        else:
            dq_ref, rs_ref, dk_ref, dv_ref, df_ref, dk_s, dv_s, df_s = rest
        p, t = pl.program_id(0), pl.program_id(1)
        i, j = ii_ref[t], jj_ref[t]
        first = _lane_is_first_head(sub)
        first_k = _lane_is_first_head(tk)
        if exchanges:
            start, finish = _chip_exchange_phases(t_hbm, rcv_ref, send_sems, recv_sems)
            pl.when(jnp.logical_and(p == 0, t == 0))(start)

        @pl.when(t == 0)
        def _():
            dq_ref[...] = jnp.zeros_like(dq_ref)
            rs_ref[...] = jnp.zeros_like(rs_ref)

        @pl.when(i == j)
        def _():
            dk_s[...] = jnp.zeros_like(dk_s)
            dv_s[...] = jnp.zeros_like(dv_s)
            df_s[...] = jnp.zeros_like(df_s)

        def tile(masked):
            kp, vp = k_ref[...], v_ref[...]
            q_all, fq_all, lse_all, o_all, do_all = q_ref[...], fq_ref[...], lse_ref[...], o_ref[...], do_ref[...]
            ft_rows = [ft_ref[pl.ds(2 * p + h, 1), :] for h in range(2)]
            dq_out, rs_out = [], []
            dk_acc, dv_acc = jnp.zeros((tk, PAIR), F32), jnp.zeros((tk, PAIR), F32)
            df_acc = [jnp.zeros((1, tk), F32), jnp.zeros((1, tk), F32)]
            for r in range(tq // sub):
                rows = slice(r * sub, (r + 1) * sub)
                qp, fq, lse, ov, dall = q_all[rows, :], fq_all[rows, :], lse_all[rows, :], o_all[rows, :], do_all[rows, :]
                dq_h, dk_h, dv_h, rs_h = [], [], [], []
                for h in range(2):
                    mine = first if h == 0 else jnp.logical_not(first)
                    s2 = _sub_scores(qp, kp, ft_rows[h], mine, r, masked, sub, tk)
                    lane = slice(h * HEAD_DIM, h * HEAD_DIM + 1)
                    pr = jnp.exp(s2 - (lse[:, lane] - fq[:, lane]))
                    dov = jnp.where(mine, dall, 0.0)
                    dsum = jnp.sum(dov * ov, axis=-1, keepdims=True)
                    dom = dov.astype(BF16)
                    dom_lo = (dov - dom.astype(F32)).astype(BF16)
                    dp = lax.dot_general(dom, vp, NT, preferred_element_type=F32)
                    dp = dp + lax.dot_general(dom_lo, vp, NT, preferred_element_type=F32)
                    ds = pr * (dp - dsum)
                    dsb = ds.astype(BF16)
                    dq_h.append(jnp.dot(dsb, kp, preferred_element_type=F32))
                    dk_h.append(lax.dot_general(dsb, qp, TN, preferred_element_type=F32))
                    dv_h.append(lax.dot_general(pr.astype(BF16), dom, TN, preferred_element_type=F32))
                    rs_h.append(jnp.sum(ds, axis=-1, keepdims=True))
                    df_acc[h] = df_acc[h] - jnp.sum(ds, axis=0, keepdims=True)
                dq_out.append(jnp.where(first, dq_h[0], dq_h[1]))
                rs_out.append(jnp.where(first, rs_h[0], rs_h[1]))
                dk_acc = dk_acc + jnp.where(first_k, dk_h[0], dk_h[1])
                dv_acc = dv_acc + jnp.where(first_k, dv_h[0], dv_h[1])
            grows = pl.ds(pl.multiple_of(i * tq, tq), tq)
            dq_ref[grows, :] += jnp.concatenate(dq_out, axis=0)
            rs_ref[grows, :] += jnp.concatenate(rs_out, axis=0)
            dk_s[...] += dk_acc
            dv_s[...] += dv_acc
            for h in range(2):
                df_s[h:h + 1, :] += df_acc[h]

        @pl.when(j < i)
        def _():
            tile(False)

        @pl.when(j == i)
        def _():
            tile(True)

        @pl.when(i == nq - 1)
        def _():
            dk_ref[...] = dk_s[...]
            dv_ref[...] = dv_s[...]
            df_ref[0] = df_s[...]

        if exchanges:
            pl.when(jnp.logical_and(p == N_PAIRS - 1, t == n_steps - 1))(finish)

    qblk = pl.BlockSpec((tq, PAIR), lambda p, t, ii_r, jj_r: (ii_r[t], p))
    kblk = pl.BlockSpec((tk, PAIR), lambda p, t, ii_r, jj_r: (jj_r[t], p))
    doblk = pl.BlockSpec((tq, PAIR), lambda p, t, ii_r, jj_r: (ii_r[t], N_PAIRS + p))
    whole = pl.BlockSpec((s, PAIR), lambda p, t, ii_r, jj_r: (0, p))
    in_specs = [qblk, kblk, kblk, qblk, pl.BlockSpec((N_HEADS, tk), lambda p, t, ii_r, jj_r: (0, jj_r[t])),
                qblk, qblk, doblk]
    out_specs = [whole, whole, kblk, kblk, pl.BlockSpec((1, 8, tk), lambda p, t, ii_r, jj_r: (p, 0, jj_r[t]))]
    out_shape = [SDS((s, D_ATTN), F32)] * 4 + [SDS((N_PAIRS, 8, s), F32)]
    scratch = [pltpu.VMEM((tk, PAIR), F32), pltpu.VMEM((tk, PAIR), F32), pltpu.VMEM((8, tk), F32)]
    args = [ii, jj, qs, kh, vb, fb, ft, lse, o, dao]
    if exchanges:
        in_specs.append(ANY)
        out_specs.append(ANY)
        out_shape.append(SDS((3,) + exchange_src.shape[1:], exchange_src.dtype))
        scratch += EXCHANGE_SEMS
        args.append(exchange_src)
    grid_spec = pltpu.PrefetchScalarGridSpec(num_scalar_prefetch=2, grid=(N_PAIRS, n_steps), in_specs=in_specs,
                                             out_specs=out_specs, scratch_shapes=scratch)
    return pl.pallas_call(
        body, name=name, grid_spec=grid_spec, out_shape=_in_hbm(out_shape),
        compiler_params=pltpu.CompilerParams(dimension_semantics=("arbitrary", "arbitrary"),
                                             vmem_limit_bytes=40 * MIB))(*[_keep_in_hbm(a) for a in args])


def _qk_bwd(z, dqs, dkh, dv, qn2, kn2, name):
    s = z.shape[0]
    tm = _tile(s)
    col0 = (z.shape[1] - 3 * D_ATTN) // D_ATTN

    def body(q_ref, k_ref, dqs_ref, dkh_ref, dv_ref, qn_ref, kn_ref, dq_ref, dk_ref, dvb_ref, dqn_ref, dkn_ref):
        first = _lane_is_first_head(tm)

        @pl.when(pl.program_id(0) == 0)
        def _():
            dqn_ref[...] = jnp.zeros_like(dqn_ref)
            dkn_ref[...] = jnp.zeros_like(dkn_ref)

        def through(x_ref, dy_ref, gain_ref, dx_ref, dgain_ref, scale):
            for p in range(N_PAIRS):
                sl = slice(p * PAIR, (p + 1) * PAIR)
                xv = x_ref[:, sl]
                r = _pair_rms(xv, first)
                xh = xv * r
                dy = dy_ref[:, sl] * scale
                dgain_ref[:, sl] += jnp.sum(dy * xh, axis=0, keepdims=True)
                dxh = dy * gain_ref[...]
                t = dxh * xh
                m0 = jnp.sum(jnp.where(first, t, 0.0), axis=-1, keepdims=True)
                m1 = jnp.sum(jnp.where(first, 0.0, t), axis=-1, keepdims=True)
                mean = jnp.where(first, m0, m1) / HEAD_DIM
                dx_ref[:, sl] = (r * (dxh - xh * mean)).astype(BF16)

        through(q_ref, dqs_ref, qn_ref, dq_ref, dqn_ref, ATTN_SCALE)
        through(k_ref, dkh_ref, kn_ref, dk_ref, dkn_ref, 1.0)
        dvb_ref[...] = dv_ref[...].astype(BF16)

    wide = lambda col: pl.BlockSpec((tm, D_ATTN), lambda i: (i, col))
    return _call(body, name=name, grid=(s // tm,),
                 in_specs=[wide(col0), wide(col0 + 1), wide(0), wide(0), wide(0), _const((1, PAIR)), _const((1, PAIR))],
                 out_specs=[wide(0), wide(0), wide(0), _const((1, D_ATTN)), _const((1, D_ATTN))],
                 out_shape=[SDS((s, D_ATTN), BF16)] * 3 + [SDS((1, D_ATTN), F32)] * 2,
                 dims=("arbitrary",))(z, z, dqs, dkh, dv, qn2, kn2)


def _gate_bwd(dft, xt, name):
    s = xt.shape[1]
    tm = _tile(s)
    n = s // tm

    def body(df_ref, xt_ref, dxt_ref, dx_ref, db_ref, carry):
        i = pl.program_id(0)

        @pl.when(i == 0)
        def _():
            carry[...] = jnp.zeros_like(carry)
            db_ref[...] = jnp.zeros_like(db_ref)

        tri = (lax.broadcasted_iota(jnp.int32, (tm, tm), 0) >= lax.broadcasted_iota(jnp.int32, (tm, tm), 1)).astype(BF16)
        rc = jnp.zeros((N_HEADS, tm), F32)
        for part in _split3(df_ref[...]):
            rc = rc + jnp.dot(part, tri, preferred_element_type=F32)
        dls = rc + carry[:, 0:1]
        carry[...] = jnp.broadcast_to(dls[:, 0:1], carry.shape)
        dxt = dls * _sigmoid(-xt_ref[...])
        dxt_ref[...] = dxt
        db_ref[...] += jnp.broadcast_to(jnp.sum(dxt, axis=-1, keepdims=True), db_ref.shape)
        padded = jnp.concatenate([dxt, jnp.zeros((V7X_LANES - N_HEADS, tm), F32)], axis=0)
        dx_ref[...] = padded.T

    rev = pl.BlockSpec((N_HEADS, tm), lambda i: (0, n - 1 - i))
    return _call(body, name=name, grid=(n,), in_specs=[rev, rev],
                 out_specs=[rev, pl.BlockSpec((tm, V7X_LANES), lambda i: (n - 1 - i, 0)), _const((N_HEADS, V7X_LANES))],
                 out_shape=[SDS((N_HEADS, s), F32), SDS((s, V7X_LANES), F32), SDS((N_HEADS, V7X_LANES), F32)],
                 scratch=[pltpu.VMEM((N_HEADS, V7X_LANES), F32)], dims=("arbitrary",))(dft, xt)


def _conv_c_fwd(z, cw, name):
    s = z.shape[0]
    c = z.shape[1] // 3
    tm, halo, kw = _tile(s), CONV_C_HALO, CONV_C_WIDTH

    def body(gb_ref, gc_ref, hh_ref, gcp_ref, hhp_ref, cw_ref, y_ref, buf):
        i = pl.program_id(0)
        buf[0:halo, :] = jnp.where(i > 0, gcp_ref[...] * hhp_ref[...], 0.0)
        buf[halo:halo + tm, :] = gc_ref[...] * hh_ref[...]
        c1 = jnp.zeros((tm, c), F32)
        for k in range(kw):
            c1 = c1 + cw_ref[k:k + 1, :] * buf[pl.ds(halo - (kw - 1) + k, tm), :]
        y_ref[...] = (gb_ref[...] * c1).astype(BF16)

    return _call(body, name=name, grid=(s // tm,),
                 in_specs=[_rows(tm, c, 0), _rows(tm, c, 1), _rows(tm, c, 2), _prev_rows(halo, tm, c, 1),
                           _prev_rows(halo, tm, c, 2), _const(cw.shape)],
                 out_specs=_rows(tm, c), out_shape=SDS((s, c), BF16),
                 scratch=[pltpu.VMEM((tm + halo, c), F32)], dims=("parallel",))(z, z, z, z, z, cw)


def _conv_c_bwd(dy0, z, cw, name):
    s = z.shape[0]
    c = z.shape[1] // 3
    tm, halo, kw = _tile(s), CONV_C_HALO, CONV_C_WIDTH
    n = s // tm

    def body(dy_ref, dyn_ref, gb_ref, gbn_ref, gc_ref, hh_ref, gcp_ref, hhp_ref, cw_ref, dz_ref, dcw_ref, buf, bd):
        i = pl.program_id(0)
        gcv, hhv, dyv = gc_ref[...], hh_ref[...], dy_ref[...]
        buf[0:halo, :] = jnp.where(i > 0, gcp_ref[...] * hhp_ref[...], 0.0)
        buf[halo:halo + tm, :] = gcv * hhv
        dc1 = dyv * gb_ref[...]
        bd[0:tm, :] = dc1
        bd[tm:tm + halo, :] = jnp.where(i < n - 1, dyn_ref[...] * gbn_ref[...], 0.0)

        @pl.when(i == 0)
        def _():
            dcw_ref[...] = jnp.zeros_like(dcw_ref)

        c1 = jnp.zeros((tm, c), F32)
        dc0 = jnp.zeros((tm, c), F32)
        for k in range(kw):
            shifted = buf[pl.ds(halo - (kw - 1) + k, tm), :]
            c1 = c1 + cw_ref[k:k + 1, :] * shifted
            dc0 = dc0 + cw_ref[k:k + 1, :] * bd[pl.ds(kw - 1 - k, tm), :]
            dcw_ref[k:k + 1, :] += jnp.sum(dc1 * shifted, axis=0, keepdims=True)
        dz_ref[:, 0:c] = (dyv * c1).astype(BF16)
        dz_ref[:, c:2 * c] = (dc0 * hhv).astype(BF16)
        dz_ref[:, 2 * c:3 * c] = (dc0 * gcv).astype(BF16)

    return _call(body, name=name, grid=(n,),
                 in_specs=[_rows(tm, c), _next_rows(halo, tm, c, 0, s // halo), _rows(tm, c, 0),
                           _next_rows(halo, tm, c, 0, s // halo), _rows(tm, c, 1), _rows(tm, c, 2),
                           _prev_rows(halo, tm, c, 1), _prev_rows(halo, tm, c, 2), _const(cw.shape)],
                 out_specs=[_rows(tm, 3 * c), _const(cw.shape)],
                 out_shape=[SDS((s, 3 * c), BF16), SDS(cw.shape, F32)],
                 scratch=[pltpu.VMEM((tm + halo, c), F32)] * 2, dims=("arbitrary",),
                 vmem_mb=48)(dy0, dy0, z, z, z, z, z, z, cw)


def _loss_head(y, target, name):
    s, d = y.shape
    tm = _tile(s)

    def body(y_ref, t_ref, loss_ref, dy_ref):
        e = y_ref[...] - t_ref[...]

        @pl.when(pl.program_id(0) == 0)
        def _():
            loss_ref[...] = jnp.zeros_like(loss_ref)

        loss_ref[...] += 0.5 * jnp.sum(jnp.mean(e * e, axis=-1, keepdims=True))
        dy_ref[...] = e / d

    return _call(body, name=name, grid=(s // tm,), in_specs=[_rows(tm, d), _rows(tm, d)],
                 out_specs=[_const((8, V7X_LANES)), _rows(tm, d)],
                 out_shape=[SDS((8, V7X_LANES), F32), SDS((s, d), F32)], dims=("arbitrary",))(y, target)


def _adamw(w, g, m, v, name):
    r, c = w.shape
    tr = next((t for t in (512, 256, 128, 64, 32, 16, 8) if r % t == 0), r)

    def body(w_ref, g_ref, m_ref, v_ref, d_ref, mo_ref, vo_ref):
        gv = g_ref[...]
        mn = ADAM_B1 * m_ref[...] + (1.0 - ADAM_B1) * gv
        vn = ADAM_B2 * v_ref[...] + (1.0 - ADAM_B2) * (gv * gv)
        m_hat = mn / (1.0 - ADAM_B1 ** ADAM_STEP)
        v_hat = vn / (1.0 - ADAM_B2 ** ADAM_STEP)
        d_ref[...] = -ADAM_LR * (m_hat / (jnp.sqrt(v_hat) + ADAM_EPS) + ADAM_WD * w_ref[...])
        mo_ref[...] = mn
        vo_ref[...] = vn

    spec = _rows(tr, c)
    return _call(body, name=name, grid=(r // tr,), in_specs=[spec] * 4, out_specs=[spec] * 3,
                 out_shape=[SDS((r, c), F32)] * 3, dims=("parallel",))(w, g, m, v)


def _position():
    return lax.axis_index("x"), lax.axis_index("y"), lax.axis_index("c")


def _other_chips(x, y):
    return [(1 - x, y), (x, 1 - y), (1 - x, 1 - y)]


def _dev_index(px, py, pc):
    return 4 * px + 2 * py + pc


def _all_gather(wloc):
    r, d = wloc.shape

    def body(x_ref, out_ref, send_sems, recv_sems, local_sem):
        start, forward, finish = _gather_phases(x_ref, out_ref, send_sems, recv_sems, local_sem)
        start()
        forward()
        finish()

    return _call(body, name="all_gather_weights", in_specs=[ANY], out_specs=ANY,
                 out_shape=SDS((N_DEV, r, d), wloc.dtype), scratch=GATHER_SEMS)(wloc)


GATHER_SEMS = [pltpu.SemaphoreType.DMA((7,)), pltpu.SemaphoreType.DMA((7,)), pltpu.SemaphoreType.DMA((1,))]


def _gather_phases(x_ref, out_ref, send_sems, recv_sems, local_sem):
    x, y, c = _position()
    me, sibling = (x, y, c), (x, y, 1 - c)
    chips = _other_chips(x, y)

    def slot(dev):
        return out_ref.at[_dev_index(*dev)]

    def copy(k, block, to, src=None):
        return pltpu.make_async_remote_copy(
            src_ref=slot(block) if src is None else src, dst_ref=slot(block),
            send_sem=send_sems.at[k], recv_sem=recv_sems.at[k], device_id=to, device_id_type=MESH)

    mine = pltpu.make_async_copy(x_ref, slot(me), local_sem.at[0])
    first = [copy(0, me, sibling, src=x_ref)] + [copy(1 + j, me, (*chip, c), src=x_ref) for j, chip in enumerate(chips)]
    passed = [copy(4 + j, (*chip, c), sibling) for j, chip in enumerate(chips)]

    def start():
        mine.start()
        for cp in first:
            cp.start()

    def forward():
        for j, chip in enumerate(chips):
            copy(1 + j, (*chip, c), me).wait_recv()
            passed[j].start()

    def finish():
        copy(0, sibling, me).wait_recv()
        for j, chip in enumerate(chips):
            copy(4 + j, (*chip, 1 - c), me).wait_recv()
        for cp in first + passed:
            cp.wait_send()
        mine.wait()

    return start, forward, finish


def _row_block(r):
    return next(t for t in range(704, 0, -BF16_ROWS) if r % t == 0)


def _pair_exchange(gall, name):
    def body(g_ref, out_ref, send_sems, recv_sems):
        start, finish = _pair_exchange_phases(g_ref, out_ref, send_sems, recv_sems)
        start()
        finish()

    return _call(body, name=name, in_specs=[ANY], out_specs=ANY, out_shape=_pair_exchange_shape(gall),
                 scratch=PAIR_EXCHANGE_SEMS)(gall)


PAIR_EXCHANGE_SEMS = [pltpu.SemaphoreType.DMA((4,)), pltpu.SemaphoreType.DMA((4,))]


def _pair_exchange_shape(gall):
    return SDS((4,) + gall.shape[1:], gall.dtype)


def _pair_exchange_phases(g_ref, out_ref, send_sems, recv_sems):
    x, y, c = _position()
    sibling = (x, y, 1 - c)
    dests = [sibling] + [(*chip, 1 - c) for chip in _other_chips(x, y)]
    copies = [pltpu.make_async_remote_copy(
        src_ref=g_ref.at[_dev_index(*dest)], dst_ref=out_ref.at[k], send_sem=send_sems.at[k],
        recv_sem=recv_sems.at[k], device_id=sibling, device_id_type=MESH) for k, dest in enumerate(dests)]

    def start():
        for cp in copies:
            cp.start()

    def finish():
        for cp in copies:
            cp.wait()

    return start, finish


def _pair_sum(gall, sib, idx, name):
    _, r, d = gall.shape
    tr = _row_block(r)

    def body(idx_ref, a_ref, b_ref, o_ref):
        o_ref[...] = (a_ref[...].astype(F32) + b_ref[...].astype(F32)).astype(o_ref.dtype)

    grid_spec = pltpu.PrefetchScalarGridSpec(
        num_scalar_prefetch=1, grid=(4, r // tr),
        in_specs=[pl.BlockSpec((1, tr, d), lambda k, i, idx_ref: (idx_ref[k], i, 0)),
                  pl.BlockSpec((1, tr, d), lambda k, i, idx_ref: (k, i, 0))],
        out_specs=pl.BlockSpec((1, tr, d), lambda k, i, idx_ref: (k, i, 0)))
    return pl.pallas_call(body, name=name, grid_spec=grid_spec,
                          out_shape=_in_hbm(SDS((4, r, d), gall.dtype)),
                          compiler_params=pltpu.CompilerParams(dimension_semantics=("parallel", "parallel")))(
        idx, _keep_in_hbm(gall), _keep_in_hbm(sib))


def _chip_exchange(tsum):
    _, r, d = tsum.shape

    def body(t_ref, out_ref, send_sems, recv_sems):
        start, finish = _chip_exchange_phases(t_ref, out_ref, send_sems, recv_sems)
        start()
        finish()

    return _call(body, name="reduce_scatter_chip_exchange", in_specs=[ANY], out_specs=ANY,
                 out_shape=SDS((3, r, d), tsum.dtype), scratch=EXCHANGE_SEMS)(tsum)


EXCHANGE_SEMS = [pltpu.SemaphoreType.DMA((3,)), pltpu.SemaphoreType.DMA((3,))]


def _chip_exchange_phases(t_ref, out_ref, send_sems, recv_sems):
    x, y, c = _position()
    copies = [pltpu.make_async_remote_copy(
        src_ref=t_ref.at[1 + k], dst_ref=out_ref.at[k], send_sem=send_sems.at[k], recv_sem=recv_sems.at[k],
        device_id=(*chip, c), device_id_type=MESH) for k, chip in enumerate(_other_chips(x, y))]

    def start():
        for cp in copies:
            cp.start()

    def finish():
        for cp in copies:
            cp.wait()

    return start, finish


def _chip_exchange_shape(tsum):
    return SDS((3,) + tsum.shape[1:], tsum.dtype)


EXCHANGES = dict(pair=(_pair_exchange_phases, _pair_exchange_shape, PAIR_EXCHANGE_SEMS),
                 chip=(_chip_exchange_phases, _chip_exchange_shape, EXCHANGE_SEMS))


def _final_sum(tsum, rcv, name):
    _, r, d = tsum.shape
    tr = _row_block(r)

    def body(t_ref, r_ref, o_ref):
        acc = t_ref[0].astype(F32)
        for k in range(3):
            acc = acc + r_ref[k].astype(F32)
        o_ref[...] = acc

    return _call(body, name=name, grid=(r // tr,),
                 in_specs=[pl.BlockSpec((1, tr, d), lambda i: (0, i, 0)), pl.BlockSpec((3, tr, d), lambda i: (0, i, 0))],
                 out_specs=_rows(tr, d), out_shape=SDS((r, d), F32), dims=("parallel",))(tsum, rcv)


def _all_reduce_small(buf):
    nr, lanes = buf.shape

    def body(b_ref, out_ref, gath, send_sems, recv_sems):
        x, y, c = _position()
        my_slot = _dev_index(x, y, c)
        gath[my_slot] = b_ref[...]
        copies = []
        for k in range(1, N_DEV):
            dx, dy, dc = (k >> 2) & 1, (k >> 1) & 1, k & 1
            peer = (1 - x if dx else x, 1 - y if dy else y, 1 - c if dc else c)
            copies.append(pltpu.make_async_remote_copy(
                src_ref=b_ref, dst_ref=gath.at[my_slot], send_sem=send_sems.at[k - 1], recv_sem=recv_sems.at[k - 1],
                device_id=peer, device_id_type=MESH))
        for cp in copies:
            cp.start()
        for cp in copies:
            cp.wait()
        acc = gath[0]
        for sidx in range(1, N_DEV):
            acc = acc + gath[sidx]
        out_ref[...] = acc

    return _call(body, name="all_reduce_small", in_specs=[VMEM], out_specs=VMEM, out_shape=SDS((nr, lanes), F32),
                 scratch=[pltpu.VMEM((N_DEV, nr, lanes), F32), pltpu.SemaphoreType.DMA((7,)),
                          pltpu.SemaphoreType.DMA((7,))])(buf)


def _ffn_block_fwd(x, gain, wall, offs, fs, tag, gather_src=None):
    res = _ffn_fwd(x, gain, wall, offs, fs, f"{tag}_fwd", gather_src)
    out, xn, g, u, h = res[:5]
    return out, (x, gain, xn, g, u, h), (res[5] if gather_src is not None else None)


def _ffn_block_bwd(dout, saved, wall, offs, fs, tag, exchange_src=None):
    x, gain, xn, g, u, h = saved
    res = _ffn_bwd_act(dout, g, u, wall, offs, fs, f"{tag}_bwd_act", exchange_src)
    dg, du, dy_b, dxn = res[:4]
    dwg = _mm_tn(dg, xn, f"{tag}_dwg", BF16)
    dwu = _mm_tn(du, xn, f"{tag}_dwu", BF16)
    dwd = _mm_tn(h, dy_b, f"{tag}_dwd", BF16)
    dx, dgain = _rmsnorm_bwd(x, gain, dxn, dout, f"{tag}_norm_bwd")
    return dx, (dwg, dwu, dwd), dgain, (res[4] if exchange_src is not None else None)


def _local_step(x, target, wall_a, fs, small, plan):
    grads = {}
    first, second = (0, fs, 2 * fs), (3 * fs, 4 * fs, 5 * fs)
    reduces = "pair_sum" in plan

    x1, s_f1a, wall_b = _ffn_block_fwd(x, small["ffn1_norm"][0], wall_a, first, fs, "l0_ffn1", plan.get("shard_b"))
    wall_b = plan.get("wall_b", wall_b)
    mixw = plan["mix_b"](wall_b)
    hn0 = _rmsnorm_fwd(x1, small["mix_norm"][0], "l0_mix_norm")
    z = _mm(hn0, mixw["ev_w_main_t"], "nt", "ev_in_proj")
    flog = _mm(hn0, mixw["ev_w_f_t"], "nt", "ev_in_proj_gate")
    a, a1 = _conv_a_fwd(z, small["ev_conv_w32"], small["ev_conv_b"], small["ev_conv_norm"], "ev_conv_fwd")
    qs, kh, vb, fb, ft, xt = _qk_fwd(z, flog, small["ev_b_f128"], small["ev_q_norm2"], small["ev_k_norm2"], "ev_qk_fwd")
    if "wall_c" in plan:
        o, lse = _attn_fwd(qs, kh, vb, fb, ft, "ev_attn_fwd")
        wall_c = plan["wall_c"]
    else:
        o, lse, wall_c = _attn_fwd(qs, kh, vb, fb, ft, "ev_attn_fwd", gather_src=plan["shard_c"])
    mixw = {**mixw, **plan["mix_c"](wall_c)}
    ao = jnp.concatenate([a, o.astype(BF16)], axis=1)
    x2 = _mm(ao, mixw["ev_w_out"], "nn", "ev_out_proj", add=x1)
    x3, s_f2a, _ = _ffn_block_fwd(x2, small["ffn2_norm"][0], wall_c, first, fs, "l0_ffn2")

    x4, s_f1b, wall_d = _ffn_block_fwd(x3, small["ffn1_norm"][1], wall_c, second, fs, "l1_ffn1", plan.get("shard_d"))
    wall_d = plan.get("wall_d", wall_d)
    hn1 = _rmsnorm_fwd(x4, small["mix_norm"][1], "l1_mix_norm")
    zo = _mm(hn1, mixw["od_w_in_t"], "nt", "od_in_proj")
    y0 = _conv_c_fwd(zo, small["od_conv_w8"], "od_conv_fwd")
    x5 = _mm(y0, mixw["od_w_out"], "nn", "od_out_proj", add=x4)
    x6, s_f2b, _ = _ffn_block_fwd(x5, small["ffn2_norm"][1], wall_d, first, fs, "l1_ffn2")

    loss, d6 = _loss_head(x6, target, "loss_head")

    d5, grads["l1_ffn2"], grads["ffn2_norm_1"], _ = _ffn_block_bwd(d6, s_f2b, wall_d, first, fs, "l1_ffn2")
    d5b = d5.astype(BF16)
    dy0 = _mm(d5b, mixw["od_w_out"], "nt", "od_out_proj_bwd")
    grads["od_w_out"] = _mm_tn(y0, d5b, "od_dw_out", BF16)
    dzo, grads["od_conv_w"] = _conv_c_bwd(dy0, zo, small["od_conv_w8"], "od_conv_bwd")
    dh1 = _mm(dzo, mixw["od_w_in_t"], "nn", "od_in_proj_bwd")
    grads["od_w_in_t"] = _mm_tn(dzo, hn1, "od_dw_in", BF16)
    d4, grads["mix_norm_1"] = _rmsnorm_bwd(x4, small["mix_norm"][1], dh1, d5, "l1_mix_norm_bwd")
    d3, grads["l1_ffn1"], grads["ffn1_norm_1"], _ = _ffn_block_bwd(d4, s_f1b, wall_c, second, fs, "l1_ffn1")

    d2, grads["l0_ffn2"], grads["ffn2_norm_0"], _ = _ffn_block_bwd(d3, s_f2a, wall_c, first, fs, "l0_ffn2")
    partials_c = plan["partials_c"](grads) if reduces else None
    d2b = d2.astype(BF16)
    dao = _mm(d2b, mixw["ev_w_out"], "nt", "ev_out_proj_bwd")
    grads["ev_w_out"] = _mm_tn(ao, d2b, "ev_dw_out", BF16)
    da1, grads["ev_conv_norm"], grads["ev_conv_b"] = _conv_a_bwd_norm(dao, a1, small["ev_conv_norm"], "ev_conv_bwd_norm")
    res = _conv_a_bwd_conv(da1, z, small["ev_conv_w32"], "ev_conv_bwd_conv",
                           exchange_src=("pair", partials_c) if reduces else None)
    du, dg, grads["ev_conv_w"] = res[:3]
    sums_c = plan["pair_sum"](partials_c, res[3], "c") if reduces else None
    res = _attn_bwd(qs, kh, vb, fb, ft, lse, o, dao, "ev_attn_bwd", exchange_src=sums_c)
    dqs, rs, dkh, dv, df4 = res[:5]
    if reduces:
        grads["pair_sums_c"], grads["exchanged_c"] = sums_c, res[5]
    dq, dk, dvb, grads["ev_q_norm"], grads["ev_k_norm"] = _qk_bwd(
        z, dqs, dkh, dv, small["ev_q_norm2"], small["ev_k_norm2"], "ev_qk_bwd")
    dft = df4[:, 0:2, :].reshape(N_HEADS, -1) + rs.reshape(-1, N_HEADS, HEAD_DIM)[:, :, 0].T
    dxt, dflog, grads["ev_b_f"] = _gate_bwd(dft, xt, "ev_gate_bwd")
    dz = jnp.concatenate([du, dg, dq, dk, dvb], axis=1)
    dflog_b = dflog.astype(BF16)
    dh0 = _mm(dz, mixw["ev_w_main_t"], "nn", "ev_in_proj_bwd")
    dh0 = _mm(dflog_b, mixw["ev_w_f_t"], "nn", "ev_in_proj_gate_bwd", add=dh0)
    dw_main = _mm_tn(dz, hn0, "ev_dw_in", BF16)
    dw_f = _mm(dxt.astype(BF16), hn0, "nn", "ev_dw_in_gate", BF16)
    grads["ev_w_in_t"] = jnp.concatenate([dw_main, dw_f], axis=0)
    d1, grads["mix_norm_0"] = _rmsnorm_bwd(x1, small["mix_norm"][0], dh0, d2, "l0_mix_norm_bwd")
    sums_b = None
    if reduces:
        partials_b = plan["partials_b"](grads)
        sums_b = plan["pair_sum"](partials_b, _pair_exchange(partials_b, "reduce_scatter_pair_exchange_b"), "b")
    d0, grads["l0_ffn1"], grads["ffn1_norm_0"], exchanged_b = _ffn_block_bwd(
        d1, s_f1a, wall_a, first, fs, "l0_ffn1", exchange_src=("chip", sums_b) if reduces else None)
    if reduces:
        grads["pair_sums_b"], grads["exchanged_b"] = sums_b, exchanged_b
    return loss, d0, grads


def _round_up(n, m):
    return -(-n // m) * m


def _pad_rows(a, rows):
    return jnp.pad(a, ((0, rows - a.shape[0]), (0, 0)))


SMALL_ORDER = ("loss", "ffn1_norm", "mix_norm", "ffn2_norm", "ev_b_f", "ev_conv_b", "ev_conv_norm",
               "ev_q_norm", "ev_k_norm", "ev_conv_w", "od_conv_w")


def _pack_small(parts):
    flat = jnp.concatenate([parts[k].reshape(-1).astype(F32) for k in SMALL_ORDER])
    n = _round_up(flat.shape[0], 8 * V7X_LANES)
    return jnp.pad(flat, (0, n - flat.shape[0])).reshape(-1, V7X_LANES)


def _unpack_small(buf, shapes):
    flat = buf.reshape(-1)
    out, pos = {}, 0
    for k in SMALL_ORDER:
        n = math.prod(shapes[k])
        out[k] = flat[pos:pos + n].reshape(shapes[k])
        pos += n
    return out


def kernel(x, ffn1_norm, ffn1_w_gate, ffn1_w_up, ffn1_w_down, mix_norm, ffn2_norm, ffn2_w_gate, ffn2_w_up, ffn2_w_down, ev_w_in, ev_b_f, ev_conv_w, ev_conv_b, ev_conv_norm, ev_q_norm, ev_k_norm, ev_w_out, od_w_in, od_conv_w, od_w_out, loss_target, m_ffn1_norm, m_ffn1_w_gate, m_ffn1_w_up, m_ffn1_w_down, m_mix_norm, m_ffn2_norm, m_ffn2_w_gate, m_ffn2_w_up, m_ffn2_w_down, m_ev_w_in, m_ev_b_f, m_ev_conv_w, m_ev_conv_b, m_ev_conv_norm, m_ev_q_norm, m_ev_k_norm, m_ev_w_out, m_od_w_in, m_od_conv_w, m_od_w_out, v_ffn1_norm, v_ffn1_w_gate, v_ffn1_w_up, v_ffn1_w_down, v_mix_norm, v_ffn2_norm, v_ffn2_w_gate, v_ffn2_w_up, v_ffn2_w_down, v_ev_w_in, v_ev_b_f, v_ev_conv_w, v_ev_conv_b, v_ev_conv_norm, v_ev_q_norm, v_ev_k_norm, v_ev_w_out, v_od_w_in, v_od_conv_w, v_od_w_out):
    weights = dict(ffn1_norm=ffn1_norm, ffn1_w_gate=ffn1_w_gate, ffn1_w_up=ffn1_w_up, ffn1_w_down=ffn1_w_down,
                   mix_norm=mix_norm, ffn2_norm=ffn2_norm, ffn2_w_gate=ffn2_w_gate, ffn2_w_up=ffn2_w_up,
                   ffn2_w_down=ffn2_w_down, ev_w_in=ev_w_in, ev_b_f=ev_b_f, ev_conv_w=ev_conv_w, ev_conv_b=ev_conv_b,
                   ev_conv_norm=ev_conv_norm, ev_q_norm=ev_q_norm, ev_k_norm=ev_k_norm, ev_w_out=ev_w_out,
                   od_w_in=od_w_in, od_conv_w=od_conv_w, od_w_out=od_w_out)
    m_in = dict(ffn1_norm=m_ffn1_norm, ffn1_w_gate=m_ffn1_w_gate, ffn1_w_up=m_ffn1_w_up, ffn1_w_down=m_ffn1_w_down,
                mix_norm=m_mix_norm, ffn2_norm=m_ffn2_norm, ffn2_w_gate=m_ffn2_w_gate, ffn2_w_up=m_ffn2_w_up,
                ffn2_w_down=m_ffn2_w_down, ev_w_in=m_ev_w_in, ev_b_f=m_ev_b_f, ev_conv_w=m_ev_conv_w,
                ev_conv_b=m_ev_conv_b, ev_conv_norm=m_ev_conv_norm, ev_q_norm=m_ev_q_norm, ev_k_norm=m_ev_k_norm,
                ev_w_out=m_ev_w_out, od_w_in=m_od_w_in, od_conv_w=m_od_conv_w, od_w_out=m_od_w_out)
    v_in = dict(ffn1_norm=v_ffn1_norm, ffn1_w_gate=v_ffn1_w_gate, ffn1_w_up=v_ffn1_w_up, ffn1_w_down=v_ffn1_w_down,
                mix_norm=v_mix_norm, ffn2_norm=v_ffn2_norm, ffn2_w_gate=v_ffn2_w_gate, ffn2_w_up=v_ffn2_w_up,
                ffn2_w_down=v_ffn2_w_down, ev_w_in=v_ev_w_in, ev_b_f=v_ev_b_f, ev_conv_w=v_ev_conv_w,
                ev_conv_b=v_ev_conv_b, ev_conv_norm=v_ev_conv_norm, ev_q_norm=v_ev_q_norm, ev_k_norm=v_ev_k_norm,
                ev_w_out=v_ev_w_out, od_w_in=v_od_w_in, od_conv_w=v_od_conv_w, od_w_out=v_od_w_out)
    order = list(weights)

    d = x.shape[-1]
    fs = ffn1_w_gate.shape[2]
    n_in = ev_w_in.shape[2]
    n_in_pad = _round_up(n_in, BF16_ROWS)
    n_out = ev_w_out.shape[1]
    n_od = od_w_in.shape[2]
    d_conv = ev_conv_b.shape[1]
    d_in_even = n_in * N_DEV
    d_main = d_in_even - N_HEADS
    cx, cy, cc = _position()
    me = _dev_index(cx, cy, cc)

    def block(wg, wu, wd, layer):
        return [wg[layer].T, wu[layer].T, wd[layer]]

    def stack(parts):
        return jnp.concatenate([p.astype(BF16) for p in parts], axis=0)

    ffn1, ffn2 = (ffn1_w_gate, ffn1_w_up, ffn1_w_down), (ffn2_w_gate, ffn2_w_up, ffn2_w_down)
    shard_a = stack(block(*ffn1, 0))
    shard_b = stack([_pad_rows(ev_w_in[0].T, n_in_pad), ev_w_out[0]])
    shard_c = stack(block(*ffn2, 0) + block(*ffn1, 1) + [od_w_in[0].T, od_w_out[0]])
    shard_d = stack(block(*ffn2, 1))
    off_ev_in, off_ev_out = 0, n_in_pad
    off_od_in, off_od_out = 6 * fs, 6 * fs + n_od
    wall_a = _all_gather(shard_a)

    def even_mixer_weights(wall_b):
        ev_w_in_t = wall_b[:, off_ev_in:off_ev_in + n_in, :].reshape(d_in_even, d)
        return dict(ev_w_main_t=ev_w_in_t[:d_main], ev_w_f_t=_pad_rows(ev_w_in_t[d_main:], V7X_LANES),
                    ev_w_out=wall_b[:, off_ev_out:off_ev_out + n_out, :].reshape(N_DEV * n_out, d))

    def odd_mixer_weights(wall_c):
        return dict(od_w_in_t=wall_c[:, off_od_in:off_od_in + n_od, :].reshape(N_DEV * n_od, d),
                    od_w_out=wall_c[:, off_od_out:off_od_out + n_out, :].reshape(N_DEV * n_out, d))

    def by_dev(a, rows, pad_to=None):
        a = a.reshape(N_DEV, rows, d)
        return a if pad_to is None else jnp.pad(a, ((0, 0), (0, pad_to - rows), (0, 0)))

    idx = jnp.stack([me] + [_dev_index(*chip, cc) for chip in _other_chips(cx, cy)]).astype(jnp.int32)

    def pair_sum(partials, from_sibling, tag):
        return _pair_sum(partials, from_sibling, idx, f"reduce_scatter_pair_sum_{tag}")

    def ffn_pieces(g, key):
        return [by_dev(t, fs) for t in g[key]]

    conv_shapes = dict(ev_conv_w=(CONV_A_WIDTH, d_conv), od_conv_w=(CONV_C_WIDTH, d))
    zero_small = {k: jnp.zeros(s_, F32) for k, s_ in conv_shapes.items()}
    ev_cw_part = lax.dynamic_update_slice(zero_small["ev_conv_w"], ev_conv_w[0], (0, me * ev_conv_w.shape[2]))
    od_cw_part = lax.dynamic_update_slice(zero_small["od_conv_w"], od_conv_w[0], (0, me * od_conv_w.shape[2]))
    zeros_like_small = {k: jnp.zeros((1,), F32) for k in SMALL_ORDER}
    taps = _unpack_small(_all_reduce_small(_pack_small({**zeros_like_small, "ev_conv_w": ev_cw_part,
                                                        "od_conv_w": od_cw_part})),
                         {**{k: (1,) for k in SMALL_ORDER}, **conv_shapes})
    small = dict(
        ffn1_norm=[ffn1_norm[l][None] for l in range(2)], mix_norm=[mix_norm[l][None] for l in range(2)],
        ffn2_norm=[ffn2_norm[l][None] for l in range(2)],
        ev_conv_w32=_pad_rows(taps["ev_conv_w"], CONV_A_WIDTH + 1), ev_conv_b=ev_conv_b, ev_conv_norm=ev_conv_norm,
        ev_b_f128=jnp.pad(ev_b_f, ((0, 0), (0, V7X_LANES - N_HEADS))),
        ev_q_norm2=jnp.tile(ev_q_norm, (1, 2)), ev_k_norm2=jnp.tile(ev_k_norm, (1, 2)),
        od_conv_w8=_pad_rows(taps["od_conv_w"], 8),
    )

    plan = dict(
        shard_b=shard_b, shard_c=shard_c, shard_d=shard_d, mix_b=even_mixer_weights, mix_c=odd_mixer_weights, pair_sum=pair_sum,
        partials_c=lambda g1: jnp.concatenate(
            ffn_pieces(g1, "l0_ffn2") + ffn_pieces(g1, "l1_ffn1") + ffn_pieces(g1, "l1_ffn2")
            + [by_dev(g1["od_w_in_t"], n_od), by_dev(g1["od_w_out"], n_out)], axis=1),
        partials_b=lambda g1: jnp.concatenate(
            [by_dev(g1["ev_w_in_t"], n_in, n_in_pad), by_dev(g1["ev_w_out"], n_out)], axis=1))
    loss_p, grad_x, g = _local_step(x[0], loss_target[0], wall_a, fs, small, plan)

    partials_a = jnp.concatenate(ffn_pieces(g, "l0_ffn1"), axis=1)
    sums_a = pair_sum(partials_a, _pair_exchange(partials_a, "reduce_scatter_pair_exchange_a"), "a")
    gsum_a = _final_sum(sums_a, _chip_exchange(sums_a), "reduce_scatter_final_sum_a")
    gsum_b = _final_sum(g["pair_sums_b"], g["exchanged_b"], "reduce_scatter_final_sum_b")
    gsum_c = _final_sum(g["pair_sums_c"], g["exchanged_c"], "reduce_scatter_final_sum_c")

    grad = {}
    where = dict(ffn1=((gsum_a, 0), (gsum_c, 3 * fs)), ffn2=((gsum_c, 0), (gsum_c, 6 * fs)))
    for blk, places in where.items():
        for wi, kind in enumerate(("gate", "up", "down")):
            rows = [buf[off + wi * fs:off + (wi + 1) * fs] for buf, off in places]
            grad[f"{blk}_w_{kind}"] = jnp.stack(rows if kind == "down" else [r.T for r in rows])
    grad["ev_w_in"] = gsum_b[off_ev_in:off_ev_in + n_in].T[None]
    grad["ev_w_out"] = gsum_b[off_ev_out:off_ev_out + n_out][None]
    grad["od_w_in"] = gsum_c[9 * fs:9 * fs + n_od].T[None]
    grad["od_w_out"] = gsum_c[9 * fs + n_od:9 * fs + n_od + n_out][None]

    heads = lambda t: t.reshape(N_HEADS, HEAD_DIM).sum(axis=0)
    parts = dict(
        loss=loss_p[0, 0:1],
        ffn1_norm=jnp.stack([g["ffn1_norm_0"][0], g["ffn1_norm_1"][0]]),
        mix_norm=jnp.stack([g["mix_norm_0"][0], g["mix_norm_1"][0]]),
        ffn2_norm=jnp.stack([g["ffn2_norm_0"][0], g["ffn2_norm_1"][0]]),
        ev_b_f=g["ev_b_f"][:, 0], ev_conv_b=g["ev_conv_b"], ev_conv_norm=g["ev_conv_norm"],
        ev_q_norm=heads(g["ev_q_norm"]), ev_k_norm=heads(g["ev_k_norm"]),
        ev_conv_w=g["ev_conv_w"][:CONV_A_WIDTH], od_conv_w=g["od_conv_w"][:CONV_C_WIDTH])
    small_shapes = dict(loss=(1,), ffn1_norm=ffn1_norm.shape, mix_norm=mix_norm.shape, ffn2_norm=ffn2_norm.shape,
                        ev_b_f=ev_b_f.shape, ev_conv_b=ev_conv_b.shape, ev_conv_norm=ev_conv_norm.shape,
                        ev_q_norm=ev_q_norm.shape, ev_k_norm=ev_k_norm.shape, **conv_shapes)
    red = _unpack_small(_all_reduce_small(_pack_small(parts)), small_shapes)
    loss = red["loss"][0]
    for k in ("ffn1_norm", "mix_norm", "ffn2_norm", "ev_b_f", "ev_conv_b", "ev_conv_norm", "ev_q_norm", "ev_k_norm"):
        grad[k] = red[k]
    grad["ev_conv_w"] = lax.dynamic_slice(red["ev_conv_w"], (0, me * ev_conv_w.shape[2]),
                                          (CONV_A_WIDTH, ev_conv_w.shape[2]))[None]
    grad["od_conv_w"] = lax.dynamic_slice(red["od_conv_w"], (0, me * od_conv_w.shape[2]),
                                          (CONV_C_WIDTH, od_conv_w.shape[2]))[None]

    big = ("ffn1_w_gate", "ffn1_w_up", "ffn1_w_down", "ffn2_w_gate", "ffn2_w_up", "ffn2_w_down",
           "ev_w_in", "ev_w_out", "od_w_in", "od_w_out")
    delta, new_m, new_v = {}, {}, {}
    for k in big:
        shp = weights[k].shape
        flat = lambda t: t.reshape(-1, shp[-1])
        dk, mk, vk = _adamw(flat(weights[k]), flat(grad[k]), flat(m_in[k]), flat(v_in[k]), f"adamw_{k}")
        delta[k], new_m[k], new_v[k] = dk.reshape(shp), mk.reshape(shp), vk.reshape(shp)
    rest = [k for k in order if k not in big]
    cat = lambda src: jnp.concatenate([src[k].reshape(-1) for k in rest])
    n_small = sum(math.prod(weights[k].shape) for k in rest)
    n_pad = _round_up(n_small, 8 * V7X_LANES)
    as_rows = lambda t: jnp.pad(t, (0, n_pad - n_small)).reshape(-1, V7X_LANES)
    v_rows = jnp.pad(cat(v_in), (0, n_pad - n_small), constant_values=1.0).reshape(-1, V7X_LANES)
    ds, ms, vs = _adamw(as_rows(cat(weights)), as_rows(cat(grad)), as_rows(cat(m_in)), v_rows, "adamw_small")
    pos = 0
    for k in rest:
        n = math.prod(weights[k].shape)
        for dst, src in ((delta, ds), (new_m, ms), (new_v, vs)):
            dst[k] = src.reshape(-1)[pos:pos + n].reshape(weights[k].shape)
        pos += n

    return (loss, grad_x[None], *[grad[k] for k in order], *[delta[k] for k in order],
            *[new_m[k] for k in order], *[new_v[k] for k in order])
```

```python
import functools
import math

import jax
import jax.numpy as jnp
from jax import lax
from jax.experimental import pallas as pl
from jax.experimental.pallas import tpu as pltpu

F32 = jnp.float32
BF16 = jnp.bfloat16
SDS = jax.ShapeDtypeStruct
MESH = pl.DeviceIdType.MESH

N_DEV = 8
EPS = 1e-6
FFN_RES = 0.5
HEAD_DIM = 64
N_HEADS = 8
D_ATTN = N_HEADS * HEAD_DIM
N_PAIRS = N_HEADS // 2
PAIR = 2 * HEAD_DIM
ATTN_SCALE = 1.0 / math.sqrt(HEAD_DIM)
CONV_A_WIDTH = 31
CONV_A_HALO = 32
CONV_C_WIDTH = 3
CONV_C_HALO = 8
NEG_BIG = -1e30
ADAM_LR, ADAM_B1, ADAM_B2, ADAM_EPS, ADAM_WD, ADAM_STEP = 0.001, 0.9, 0.999, 1e-08, 0.01, 10

V7X_VMEM_BYTES = 64 * 1024 * 1024
V7X_LANES = 128
BF16_ROWS = 16
MIB = 1024 * 1024

NT = (((1,), (1,)), ((), ()))
TN = (((0,), (0,)), ((), ()))


def _call(body, *, name, out_shape, in_specs, out_specs, grid=(), scratch=(), dims=None, vmem_mb=32, **kw):
    params = dict(vmem_limit_bytes=min(vmem_mb * MIB, V7X_VMEM_BYTES - 4 * MIB))
    if dims is not None:
        params["dimension_semantics"] = dims
    call = pl.pallas_call(
        body, name=name, grid=grid, in_specs=in_specs, out_specs=out_specs, out_shape=_in_hbm(out_shape),
        scratch_shapes=list(scratch), compiler_params=pltpu.CompilerParams(**params), **kw)
    return lambda *args: call(*[_keep_in_hbm(a) for a in args])


LARGE_OPERAND_BYTES = MIB


def _is_large(a):
    return a.ndim >= 2 and math.prod(a.shape) * jnp.dtype(a.dtype).itemsize >= LARGE_OPERAND_BYTES


def _keep_in_hbm(a):
    return pltpu.with_memory_space_constraint(a, pltpu.HBM) if _is_large(a) else a


def _in_hbm(out_shape):
    one = lambda s: pltpu.HBM(s.shape, s.dtype) if _is_large(s) else s
    return [one(s) for s in out_shape] if isinstance(out_shape, (list, tuple)) else one(out_shape)


def _tile(n, want=512):
    return want if n % want == 0 else n


def _rows(tm, d, col=0):
    return pl.BlockSpec((tm, d), lambda i: (i, col))


def _const(shape):
    return pl.BlockSpec(shape, lambda *_: (0,) * len(shape))


ANY = pl.BlockSpec(memory_space=pl.ANY)
VMEM = pl.BlockSpec(memory_space=pltpu.VMEM)


def _sigmoid(x):
    return 1.0 / (1.0 + jnp.exp(-x))


def _rmsnorm_fwd(x, gain, name):
    s, d = x.shape
    tm = _tile(s)

    def body(x_ref, g_ref, o_ref):
        xv = x_ref[...]
        r = lax.rsqrt(jnp.mean(xv * xv, axis=-1, keepdims=True) + EPS)
        o_ref[...] = (xv * r * g_ref[...]).astype(BF16)

    return _call(body, name=name, grid=(s // tm,), in_specs=[_rows(tm, d), _const((1, d))],
                 out_specs=_rows(tm, d), out_shape=SDS((s, d), BF16), dims=("parallel",))(x, gain)


def _rmsnorm_bwd(x, gain, dxn, dres, name):
    s, d = x.shape
    tm = _tile(s)

    def body(x_ref, g_ref, dxn_ref, dres_ref, dx_ref, dg_ref):
        xv = x_ref[...]
        r = lax.rsqrt(jnp.mean(xv * xv, axis=-1, keepdims=True) + EPS)
        xh = xv * r
        dv = dxn_ref[...]

        @pl.when(pl.program_id(0) == 0)
        def _():
            dg_ref[...] = jnp.zeros_like(dg_ref)

        dg_ref[...] += jnp.sum(dv * xh, axis=0, keepdims=True)
        dxh = dv * g_ref[...]
        dx_ref[...] = dres_ref[...] + r * (dxh - xh * jnp.mean(dxh * xh, axis=-1, keepdims=True))

    return _call(body, name=name, grid=(s // tm,),
                 in_specs=[_rows(tm, d), _const((1, d)), _rows(tm, d), _rows(tm, d)],
                 out_specs=[_rows(tm, d), _const((1, d))],
                 out_shape=[SDS((s, d), F32), SDS((1, d), F32)], dims=("arbitrary",))(x, gain, dxn, dres)


def _col_tile(n):
    for t in (1024, 768, 512, 256, 128):
        if n % t == 0:
            return t
    return n


def _mm(a, b, mode, name, out_dtype=F32, add=None):
    if mode == "tn":
        k, m = a.shape
        n = b.shape[1]
        bm = 256 if m % 256 == 0 else m

        def body_tn(a_ref, b_ref, o_ref):
            o_ref[...] = lax.dot_general(a_ref[...].astype(BF16), b_ref[...].astype(BF16), TN,
                                         preferred_element_type=F32).astype(out_dtype)

        return _call(body_tn, name=name, grid=(m // bm,),
                     in_specs=[pl.BlockSpec((k, bm), lambda i: (0, i)), _const((k, n))],
                     out_specs=pl.BlockSpec((bm, n), lambda i: (i, 0)),
                     out_shape=SDS((m, n), out_dtype), dims=("parallel",), vmem_mb=48)(a, b)
    m, k = a.shape
    n = b.shape[0] if mode == "nt" else b.shape[1]
    tm, tn = _tile(m), _col_tile(n)
    dn = NT if mode == "nt" else (((1,), (0,)), ((), ()))

    def body(a_ref, b_ref, *rest):
        o_ref = rest[-1]
        acc = lax.dot_general(a_ref[...].astype(BF16), b_ref[...].astype(BF16), dn, preferred_element_type=F32)
        if add is not None:
            acc = acc + rest[0][...]
        o_ref[...] = acc.astype(out_dtype)

    b_spec = (pl.BlockSpec((tn, k), lambda i, j: (j, 0)) if mode == "nt"
              else pl.BlockSpec((k, tn), lambda i, j: (0, j)))
    in_specs = [pl.BlockSpec((tm, k), lambda i, j: (i, 0)), b_spec]
    args = [a, b]
    if add is not None:
        in_specs.append(pl.BlockSpec((tm, tn), lambda i, j: (i, j)))
        args.append(add)
    return _call(body, name=name, grid=(m // tm, n // tn), in_specs=in_specs,
                 out_specs=pl.BlockSpec((tm, tn), lambda i, j: (i, j)),
                 out_shape=SDS((m, n), out_dtype), dims=("parallel", "parallel"), vmem_mb=48)(*args)


def _mm_tn(a, b, name, out_dtype=F32):
    return _mm(a, b, "tn", name, out_dtype)


FFN_TM = 256
FFN_FWD_TM = 512
FFN_CHUNK = 256


def _load_ffn_weights(w_hbm, offs, fs, dsts, sems):
    copies = []
    for wi, (off, dst) in enumerate(zip(offs, dsts)):
        for j in range(N_DEV):
            cp = pltpu.make_async_copy(w_hbm.at[j, pl.ds(off, fs), :], dst.at[pl.ds(j * fs, fs), :],
                                       sems.at[wi * N_DEV + j])
            cp.start()
            copies.append(cp)
    for cp in copies:
        cp.wait()


def _ffn_fwd(x, gain, wall, offs, fs, name, gather_src=None):
    s, d = x.shape
    f = fs * N_DEV
    tm, ch = _tile(s, FFN_FWD_TM), FFN_CHUNK
    n = s // tm
    gathers = gather_src is not None

    def body(x_ref, gain_ref, w_hbm, *rest):
        if gathers:
            (src_hbm, out_ref, xn_ref, g_ref, u_ref, h_ref, gathered, wg_s, wu_s, wd_s, sems,
             send_sems, recv_sems, local_sem) = rest
            start, forward, finish = _gather_phases(src_hbm, gathered, send_sems, recv_sems, local_sem)
            pl.when(pl.program_id(0) == 0)(start)
            pl.when(pl.program_id(0) == (3 * n) // 4)(forward)
        else:
            out_ref, xn_ref, g_ref, u_ref, h_ref, wg_s, wu_s, wd_s, sems = rest

        @pl.when(pl.program_id(0) == 0)
        def _():
            _load_ffn_weights(w_hbm, offs, fs, (wg_s, wu_s, wd_s), sems)

        xv = x_ref[...]
        xnv = (xv * lax.rsqrt(jnp.mean(xv * xv, axis=-1, keepdims=True) + EPS) * gain_ref[...]).astype(BF16)
        xn_ref[...] = xnv
        acc = jnp.zeros((tm, d), F32)
        for c in range(f // ch):
            sl = slice(c * ch, (c + 1) * ch)
            gb = lax.dot_general(xnv, wg_s[sl, :], NT, preferred_element_type=F32).astype(BF16)
            ub = lax.dot_general(xnv, wu_s[sl, :], NT, preferred_element_type=F32).astype(BF16)
            g_ref[:, sl] = gb
            u_ref[:, sl] = ub
            g = gb.astype(F32)
            hb = (g * _sigmoid(g) * ub.astype(F32)).astype(BF16)
            h_ref[:, sl] = hb
            acc = acc + jnp.dot(hb, wd_s[sl, :], preferred_element_type=F32)
        out_ref[...] = xv + FFN_RES * acc
        if gathers:
            pl.when(pl.program_id(0) == n - 1)(finish)

    in_specs, args = [_rows(tm, d), _const((1, d)), ANY], [x, gain, wall]
    out_specs = [_rows(tm, d), _rows(tm, d), _rows(tm, f), _rows(tm, f), _rows(tm, f)]
    out_shape = [SDS((s, d), F32), SDS((s, d), BF16), SDS((s, f), BF16), SDS((s, f), BF16), SDS((s, f), BF16)]
    scratch = [pltpu.VMEM((f, d), BF16)] * 3 + [pltpu.SemaphoreType.DMA((3 * N_DEV,))]
    if gathers:
        in_specs.append(ANY)
        args.append(gather_src)
        out_specs.append(ANY)
        out_shape.append(SDS((N_DEV,) + gather_src.shape, gather_src.dtype))
        scratch += GATHER_SEMS
    return _call(body, name=name, grid=(n,), in_specs=in_specs, out_specs=out_specs, out_shape=out_shape,
                 scratch=scratch, dims=("arbitrary",), vmem_mb=56)(*args)


def _ffn_bwd_act(dout, g, u, wall, offs, fs, name, exchange_src=None):
    s, d = dout.shape
    f = fs * N_DEV
    tm, ch = _tile(s, FFN_TM), FFN_CHUNK
    n = s // tm
    exchanges = exchange_src is not None
    if exchanges:
        phases_of, shape_of, exchange_sems = EXCHANGES[exchange_src[0]]

    def body(dout_ref, g_ref, u_ref, w_hbm, *rest):
        if exchanges:
            t_hbm, dg_ref, du_ref, dy_ref, dxn_ref, rcv_ref, wg_s, wu_s, wd_s, sems, send_sems, recv_sems = rest
            start, finish = phases_of(t_hbm, rcv_ref, send_sems, recv_sems)
            pl.when(pl.program_id(0) == 0)(start)
        else:
            dg_ref, du_ref, dy_ref, dxn_ref, wg_s, wu_s, wd_s, sems = rest

        @pl.when(pl.program_id(0) == 0)
        def _():
            _load_ffn_weights(w_hbm, offs, fs, (wg_s, wu_s, wd_s), sems)

        dy = (FFN_RES * dout_ref[...]).astype(BF16)
        dy_ref[...] = dy
        acc = jnp.zeros((tm, d), F32)
        for c in range(f // ch):
            sl = slice(c * ch, (c + 1) * ch)
            dh = lax.dot_general(dy, wd_s[sl, :], NT, preferred_element_type=F32)
            gv = g_ref[:, sl].astype(F32)
            uv = u_ref[:, sl].astype(F32)
            sg = _sigmoid(gv)
            dgb = (dh * uv * sg * (1.0 + gv * (1.0 - sg))).astype(BF16)
            dub = (dh * gv * sg).astype(BF16)
            dg_ref[:, sl] = dgb
            du_ref[:, sl] = dub
            acc = acc + jnp.dot(dgb, wg_s[sl, :], preferred_element_type=F32)
            acc = acc + jnp.dot(dub, wu_s[sl, :], preferred_element_type=F32)
        dxn_ref[...] = acc
        if exchanges:
            pl.when(pl.program_id(0) == n - 1)(finish)

    in_specs, args = [_rows(tm, d), _rows(tm, f), _rows(tm, f), ANY], [dout, g, u, wall]
    out_specs = [_rows(tm, f), _rows(tm, f), _rows(tm, d), _rows(tm, d)]
    out_shape = [SDS((s, f), BF16), SDS((s, f), BF16), SDS((s, d), BF16), SDS((s, d), F32)]
    scratch = [pltpu.VMEM((f, d), BF16)] * 3 + [pltpu.SemaphoreType.DMA((3 * N_DEV,))]
    if exchanges:
        in_specs.append(ANY)
        args.append(exchange_src[1])
        out_specs.append(ANY)
        out_shape.append(shape_of(exchange_src[1]))
        scratch += exchange_sems
    return _call(body, name=name, grid=(n,), in_specs=in_specs, out_specs=out_specs, out_shape=out_shape,
                 scratch=scratch, dims=("arbitrary",), vmem_mb=56)(*args)


def _prev_rows(halo, tm, c, col):
    return pl.BlockSpec((halo, c), lambda i: (jnp.maximum(i * (tm // halo) - 1, 0), col))


def _next_rows(halo, tm, c, col, n_blocks):
    return pl.BlockSpec((halo, c), lambda i: (jnp.minimum((i + 1) * (tm // halo), n_blocks - 1), col))


def _conv_a_fwd(z, cw, cb, cn, name):
    s = z.shape[0]
    c = cb.shape[1]
    tm, halo, kw = _tile(s), CONV_A_HALO, CONV_A_WIDTH

    def body(u_ref, g_ref, up_ref, gp_ref, cw_ref, cb_ref, cn_ref, a_ref, a1_ref, buf):
        i = pl.program_id(0)
        buf[0:halo, :] = jnp.where(i > 0, up_ref[...] * _sigmoid(gp_ref[...]), 0.0)
        buf[halo:halo + tm, :] = u_ref[...] * _sigmoid(g_ref[...])
        acc = jnp.zeros((tm, c), F32)
        for k in range(kw):
            acc = acc + cw_ref[k:k + 1, :] * buf[pl.ds(halo - (kw - 1) + k, tm), :]
        a1 = acc + cb_ref[...]
        a1_ref[...] = a1
        a2 = a1 * lax.rsqrt(jnp.mean(a1 * a1, axis=-1, keepdims=True) + EPS) * cn_ref[...]
        a_ref[...] = (a2 * _sigmoid(a2)).astype(BF16)

    return _call(body, name=name, grid=(s // tm,),
                 in_specs=[_rows(tm, c, 0), _rows(tm, c, 1), _prev_rows(halo, tm, c, 0), _prev_rows(halo, tm, c, 1),
                           _const(cw.shape), _const((1, c)), _const((1, c))],
                 out_specs=[_rows(tm, c), _rows(tm, c)],
                 out_shape=[SDS((s, c), BF16), SDS((s, c), F32)],
                 scratch=[pltpu.VMEM((tm + halo, c), F32)], dims=("parallel",))(z, z, z, z, cw, cb, cn)


def _conv_a_bwd_norm(dao, a1, cn, name):
    s, c = a1.shape
    tm = _tile(s)

    def body(da_ref, a1_ref, cn_ref, da1_ref, dcn_ref, dcb_ref):
        a1v = a1_ref[...]
        r = lax.rsqrt(jnp.mean(a1v * a1v, axis=-1, keepdims=True) + EPS)
        xh = a1v * r
        a2 = xh * cn_ref[...]
        sg = _sigmoid(a2)
        da2 = da_ref[...] * sg * (1.0 + a2 * (1.0 - sg))
        dxh = da2 * cn_ref[...]
        da1 = r * (dxh - xh * jnp.mean(dxh * xh, axis=-1, keepdims=True))
        da1_ref[...] = da1

        @pl.when(pl.program_id(0) == 0)
        def _():
            dcn_ref[...] = jnp.zeros_like(dcn_ref)
            dcb_ref[...] = jnp.zeros_like(dcb_ref)

        dcn_ref[...] += jnp.sum(da2 * xh, axis=0, keepdims=True)
        dcb_ref[...] += jnp.sum(da1, axis=0, keepdims=True)

    return _call(body, name=name, grid=(s // tm,),
                 in_specs=[_rows(tm, c, 0), _rows(tm, c), _const((1, c))],
                 out_specs=[_rows(tm, c), _const((1, c)), _const((1, c))],
                 out_shape=[SDS((s, c), F32), SDS((1, c), F32), SDS((1, c), F32)], dims=("arbitrary",))(dao, a1, cn)


def _conv_a_bwd_conv(da1, z, cw, name, exchange_src=None):
    s, c = da1.shape
    tm, halo, kw = _tile(s), CONV_A_HALO, CONV_A_WIDTH
    n = s // tm
    exchanges = exchange_src is not None
    if exchanges:
        phases_of, shape_of, exchange_sems = EXCHANGES[exchange_src[0]]

    def body(d_ref, dn_ref, u_ref, g_ref, up_ref, gp_ref, cw_ref, *rest):
        i = pl.program_id(0)
        if exchanges:
            t_hbm, du_ref, dg_ref, dcw_ref, rcv_ref, buf, bd, send_sems, recv_sems = rest
            start, finish = phases_of(t_hbm, rcv_ref, send_sems, recv_sems)
            pl.when(i == 0)(start)
        else:
            du_ref, dg_ref, dcw_ref, buf, bd = rest
        uv = u_ref[...]
        sg = _sigmoid(g_ref[...])
        buf[0:halo, :] = jnp.where(i > 0, up_ref[...] * _sigmoid(gp_ref[...]), 0.0)
        buf[halo:halo + tm, :] = uv * sg
        dv = d_ref[...]
        bd[0:tm, :] = dv
        bd[tm:tm + halo, :] = jnp.where(i < n - 1, dn_ref[...], 0.0)

        @pl.when(i == 0)
        def _():
            dcw_ref[...] = jnp.zeros_like(dcw_ref)

        da0 = jnp.zeros((tm, c), F32)
        for k in range(kw):
            da0 = da0 + cw_ref[k:k + 1, :] * bd[pl.ds(kw - 1 - k, tm), :]
            dcw_ref[k:k + 1, :] += jnp.sum(dv * buf[pl.ds(halo - (kw - 1) + k, tm), :], axis=0, keepdims=True)
        du_ref[...] = (da0 * sg).astype(BF16)
        dg_ref[...] = (da0 * uv * sg * (1.0 - sg)).astype(BF16)
        if exchanges:
            pl.when(i == n - 1)(finish)

    in_specs = [_rows(tm, c), _next_rows(halo, tm, c, 0, s // halo), _rows(tm, c, 0), _rows(tm, c, 1),
                _prev_rows(halo, tm, c, 0), _prev_rows(halo, tm, c, 1), _const(cw.shape)]
    args = [da1, da1, z, z, z, z, cw]
    out_specs = [_rows(tm, c), _rows(tm, c), _const(cw.shape)]
    out_shape = [SDS((s, c), BF16), SDS((s, c), BF16), SDS(cw.shape, F32)]
    scratch = [pltpu.VMEM((tm + halo, c), F32)] * 2
    if exchanges:
        in_specs.append(ANY)
        args.append(exchange_src[1])
        out_specs.append(ANY)
        out_shape.append(shape_of(exchange_src[1]))
        scratch += exchange_sems
    return _call(body, name=name, grid=(n,), in_specs=in_specs, out_specs=out_specs, out_shape=out_shape,
                 scratch=scratch, dims=("arbitrary",))(*args)


def _lane_is_first_head(tm):
    return lax.broadcasted_iota(jnp.int32, (tm, PAIR), 1) < HEAD_DIM


def _pair_rms(xp, first):
    x2 = xp * xp
    s0 = jnp.sum(jnp.where(first, x2, 0.0), axis=-1, keepdims=True)
    s1 = jnp.sum(jnp.where(first, 0.0, x2), axis=-1, keepdims=True)
    return jnp.where(first, lax.rsqrt(s0 / HEAD_DIM + EPS), lax.rsqrt(s1 / HEAD_DIM + EPS))


def _split3(x):
    hi = x.astype(BF16)
    r1 = x - hi.astype(F32)
    mid = r1.astype(BF16)
    lo = (r1 - mid.astype(F32)).astype(BF16)
    return hi, mid, lo


def _qk_fwd(z, flog, bf, qn2, kn2, name):
    s = z.shape[0]
    tm = _tile(s)
    col0 = (z.shape[1] - 3 * D_ATTN) // D_ATTN

    def body(q_ref, k_ref, v_ref, fl_ref, bf_ref, qn_ref, kn_ref,
             qs_ref, kh_ref, vb_ref, fb_ref, ft_ref, xt_ref, carry):
        i = pl.program_id(0)
        first = _lane_is_first_head(tm)
        for p in range(N_PAIRS):
            sl = slice(p * PAIR, (p + 1) * PAIR)
            q = q_ref[:, sl]
            qs_ref[:, sl] = (q * _pair_rms(q, first) * qn_ref[...] * ATTN_SCALE).astype(BF16)
            k = k_ref[:, sl]
            kh_ref[:, sl] = (k * _pair_rms(k, first) * kn_ref[...]).astype(BF16)
        vb_ref[...] = v_ref[...].astype(BF16)

        xg = fl_ref[...] + bf_ref[...]
        valid = lax.broadcasted_iota(jnp.int32, (tm, V7X_LANES), 1) < N_HEADS
        ls = jnp.where(valid, jnp.minimum(xg, 0.0) - jnp.log(1.0 + jnp.exp(-jnp.abs(xg))), 0.0)
        tri = (lax.broadcasted_iota(jnp.int32, (tm, tm), 1) <= lax.broadcasted_iota(jnp.int32, (tm, tm), 0)).astype(BF16)
        cs = jnp.zeros((tm, V7X_LANES), F32)
        for part in _split3(ls):
            cs = cs + jnp.dot(tri, part, preferred_element_type=F32)

        @pl.when(i == 0)
        def _():
            carry[...] = jnp.zeros_like(carry)

        fv = cs + carry[0:1, :]
        carry[0:1, :] = fv[tm - 1:tm, :]
        ft_ref[...] = fv.T[0:N_HEADS, :]
        xt_ref[...] = xg.T[0:N_HEADS, :]
        for p in range(N_PAIRS):
            fb_ref[:, p * PAIR:(p + 1) * PAIR] = jnp.where(first, fv[:, 2 * p:2 * p + 1], fv[:, 2 * p + 1:2 * p + 2])

    wide = lambda col: pl.BlockSpec((tm, D_ATTN), lambda i: (i, col))
    tcol = pl.BlockSpec((N_HEADS, tm), lambda i: (0, i))
    return _call(body, name=name, grid=(s // tm,),
                 in_specs=[wide(col0), wide(col0 + 1), wide(col0 + 2), _rows(tm, V7X_LANES),
                           _const((1, V7X_LANES)), _const((1, PAIR)), _const((1, PAIR))],
                 out_specs=[wide(0), wide(0), wide(0), wide(0), tcol, tcol],
                 out_shape=[SDS((s, D_ATTN), BF16)] * 3 + [SDS((s, D_ATTN), F32), SDS((N_HEADS, s), F32),
                                                          SDS((N_HEADS, s), F32)],
                 scratch=[pltpu.VMEM((8, V7X_LANES), F32)], dims=("arbitrary",))(z, z, z, flog, bf, qn2, kn2)


ATTN_FWD_SUB = 256
ATTN_BWD_SUB = 512


def _causal_schedule(nq, key_major):
    if key_major:
        pairs = [(i, j) for j in range(nq) for i in range(j, nq)]
    else:
        pairs = [(i, j) for i in range(nq) for j in range(i + 1)]
    return (jnp.asarray([p[0] for p in pairs], jnp.int32), jnp.asarray([p[1] for p in pairs], jnp.int32))


def _sub_scores(qp, kp, ft_row, mine, r, masked, sub, tk):
    qm = jnp.where(mine, qp, jnp.zeros_like(qp))
    s2 = lax.dot_general(qm, kp, NT, preferred_element_type=F32) - ft_row
    if masked:
        row = r * sub + lax.broadcasted_iota(jnp.int32, (sub, tk), 0)
        s2 = jnp.where(lax.broadcasted_iota(jnp.int32, (sub, tk), 1) <= row, s2, NEG_BIG)
    return s2


def _attn_fwd(qs, kh, vb, fb, ft, name, gather_src=None):
    s = qs.shape[0]
    tq = tk = _tile(s)
    nq = s // tq
    sub = min(ATTN_FWD_SUB, tq)
    ii, jj = _causal_schedule(nq, key_major=False)
    n_steps = ii.shape[0]
    gathers = gather_src is not None

    def body(ii_ref, jj_ref, q_ref, k_ref, v_ref, fq_ref, ft_ref, *rest):
        if gathers:
            x_hbm, o_ref, lse_ref, wall_ref, m_s, l_s, acc_s, send_sems, recv_sems, local_sem = rest
        else:
            o_ref, lse_ref, m_s, l_s, acc_s = rest
        p, t = pl.program_id(0), pl.program_id(1)
        i, j = ii_ref[t], jj_ref[t]
        first = _lane_is_first_head(sub)
        if gathers:
            start, forward, finish = _gather_phases(x_hbm, wall_ref, send_sems, recv_sems, local_sem)
            pl.when(jnp.logical_and(p == 0, t == 0))(start)
            pl.when(jnp.logical_and(p == N_PAIRS - 1, t == 0))(forward)

        @pl.when(j == 0)
        def _():
            m_s[...] = jnp.full_like(m_s, NEG_BIG)
            l_s[...] = jnp.zeros_like(l_s)
            acc_s[...] = jnp.zeros_like(acc_s)

        def tile(masked):
            kp, vp = k_ref[...], v_ref[...]
            q_all, fq_all, acc_all = q_ref[...], fq_ref[...], acc_s[...]
            m_all, l_all = (m_s[0], m_s[1]), (l_s[0], l_s[1])
            ft_rows = [ft_ref[pl.ds(2 * p + h, 1), :] for h in range(2)]
            m_out, l_out, acc_out = ([], []), ([], []), []
            for r in range(tq // sub):
                rows = slice(r * sub, (r + 1) * sub)
                qp, fq, acc = q_all[rows, :], fq_all[rows, :], acc_all[rows, :]
                new = []
                for h in range(2):
                    mine = first if h == 0 else jnp.logical_not(first)
                    s2 = _sub_scores(qp, kp, ft_rows[h], mine, r, masked, sub, tk)
                    fqh = fq[:, h * HEAD_DIM:h * HEAD_DIM + 1]
                    m_old = m_all[h][rows, :]
                    m_new = jnp.maximum(m_old, jnp.max(s2, axis=-1, keepdims=True) + fqh)
                    pr = jnp.exp(s2 - (m_new - fqh))
                    alpha = jnp.exp(m_old - m_new)
                    l_out[h].append(alpha * l_all[h][rows, :] + jnp.sum(pr, axis=-1, keepdims=True))
                    m_out[h].append(m_new)
                    new.append(alpha * acc + jnp.dot(pr.astype(BF16), vp, preferred_element_type=F32))
                acc_out.append(jnp.where(first, new[0], new[1]))
            for h in range(2):
                m_s[h] = jnp.concatenate(m_out[h], axis=0)
                l_s[h] = jnp.concatenate(l_out[h], axis=0)
            acc_s[...] = jnp.concatenate(acc_out, axis=0)

        @pl.when(j < i)
        def _():
            tile(False)

        @pl.when(j == i)
        def _():
            tile(True)
            whole = _lane_is_first_head(tq)
            l_pair = jnp.where(whole, l_s[0], l_s[1])
            o_ref[...] = acc_s[...] / l_pair
            lse_ref[...] = jnp.where(whole, m_s[0], m_s[1]) + jnp.log(l_pair)

        if gathers:
            pl.when(jnp.logical_and(p == N_PAIRS - 1, t == n_steps - 1))(finish)

    qblk = pl.BlockSpec((tq, PAIR), lambda p, t, ii_r, jj_r: (ii_r[t], p))
    kblk = pl.BlockSpec((tk, PAIR), lambda p, t, ii_r, jj_r: (jj_r[t], p))
    in_specs = [qblk, kblk, kblk, qblk, pl.BlockSpec((N_HEADS, tk), lambda p, t, ii_r, jj_r: (0, jj_r[t]))]
    out_specs, out_shape = [qblk, qblk], [SDS((s, D_ATTN), F32)] * 2
    scratch = [pltpu.VMEM((2, tq, 1), F32), pltpu.VMEM((2, tq, 1), F32), pltpu.VMEM((tq, PAIR), F32)]
    args = [ii, jj, qs, kh, vb, fb, ft]
    if gathers:
        in_specs.append(ANY)
        out_specs.append(ANY)
        out_shape.append(SDS((N_DEV,) + gather_src.shape, gather_src.dtype))
        scratch += GATHER_SEMS
        args.append(gather_src)
    grid_spec = pltpu.PrefetchScalarGridSpec(num_scalar_prefetch=2, grid=(N_PAIRS, n_steps), in_specs=in_specs,
                                             out_specs=out_specs, scratch_shapes=scratch)
    return pl.pallas_call(
        body, name=name, grid_spec=grid_spec, out_shape=_in_hbm(out_shape),
        compiler_params=pltpu.CompilerParams(dimension_semantics=("arbitrary", "arbitrary"),
                                             vmem_limit_bytes=32 * MIB))(*[_keep_in_hbm(a) for a in args])


def _attn_bwd(qs, kh, vb, fb, ft, lse, o, dao, name, exchange_src=None):
    s = qs.shape[0]
    tq = tk = _tile(s)
    nq = s // tq
    sub = min(ATTN_BWD_SUB, tq)
    ii, jj = _causal_schedule(nq, key_major=True)
    n_steps = ii.shape[0]

    exchanges = exchange_src is not None

    def body(ii_ref, jj_ref, q_ref, k_ref, v_ref, fq_ref, ft_ref, lse_ref, o_ref, do_ref, *rest):
        if exchanges:
            t_hbm, dq_ref, rs_ref, dk_ref, dv_ref, df_ref, rcv_ref, dk_s, dv_s, df_s, send_sems, recv_sems = rest
        else:
            dq_ref, rs_ref, dk_ref, dv_ref, df_ref, dk_s, dv_s, df_s = rest
        p, t = pl.program_id(0), pl.program_id(1)
        i, j = ii_ref[t], jj_ref[t]
        first = _lane_is_first_head(sub)
        first_k = _lane_is_first_head(tk)
        if exchanges:
            start, finish = _chip_exchange_phases(t_hbm, rcv_ref, send_sems, recv_sems)
            pl.when(jnp.logical_and(p == 0, t == 0))(start)

        @pl.when(t == 0)
        def _():
            dq_ref[...] = jnp.zeros_like(dq_ref)
            rs_ref[...] = jnp.zeros_like(rs_ref)

        @pl.when(i == j)
        def _():
            dk_s[...] = jnp.zeros_like(dk_s)
            dv_s[...] = jnp.zeros_like(dv_s)
            df_s[...] = jnp.zeros_like(df_s)

        def tile(masked):
            kp, vp = k_ref[...], v_ref[...]
            q_all, fq_all, lse_all, o_all, do_all = q_ref[...], fq_ref[...], lse_ref[...], o_ref[...], do_ref[...]
            ft_rows = [ft_ref[pl.ds(2 * p + h, 1), :] for h in range(2)]
            dq_out, rs_out = [], []
            dk_acc, dv_acc = jnp.zeros((tk, PAIR), F32), jnp.zeros((tk, PAIR), F32)
            df_acc = [jnp.zeros((1, tk), F32), jnp.zeros((1, tk), F32)]
            for r in range(tq // sub):
                rows = slice(r * sub, (r + 1) * sub)
                qp, fq, lse, ov, dall = q_all[rows, :], fq_all[rows, :], lse_all[rows, :], o_all[rows, :], do_all[rows, :]
                dq_h, dk_h, dv_h, rs_h = [], [], [], []
                for h in range(2):
                    mine = first if h == 0 else jnp.logical_not(first)
                    s2 = _sub_scores(qp, kp, ft_rows[h], mine, r, masked, sub, tk)
                    lane = slice(h * HEAD_DIM, h * HEAD_DIM + 1)
                    pr = jnp.exp(s2 - (lse[:, lane] - fq[:, lane]))
                    dov = jnp.where(mine, dall, 0.0)
                    dsum = jnp.sum(dov * ov, axis=-1, keepdims=True)
                    dom = dov.astype(BF16)
                    dom_lo = (dov - dom.astype(F32)).astype(BF16)
                    dp = lax.dot_general(dom, vp, NT, preferred_element_type=F32)
                    dp = dp + lax.dot_general(dom_lo, vp, NT, preferred_element_type=F32)
                    ds = pr * (dp - dsum)
                    dsb = ds.astype(BF16)
                    dq_h.append(jnp.dot(dsb, kp, preferred_element_type=F32))
                    dk_h.append(lax.dot_general(dsb, qp, TN, preferred_element_type=F32))
                    dv_h.append(lax.dot_general(pr.astype(BF16), dom, TN, preferred_element_type=F32))
                    rs_h.append(jnp.sum(ds, axis=-1, keepdims=True))
                    df_acc[h] = df_acc[h] - jnp.sum(ds, axis=0, keepdims=True)
                dq_out.append(jnp.where(first, dq_h[0], dq_h[1]))
                rs_out.append(jnp.where(first, rs_h[0], rs_h[1]))
                dk_acc = dk_acc + jnp.where(first_k, dk_h[0], dk_h[1])
                dv_acc = dv_acc + jnp.where(first_k, dv_h[0], dv_h[1])
            grows = pl.ds(pl.multiple_of(i * tq, tq), tq)
            dq_ref[grows, :] += jnp.concatenate(dq_out, axis=0)
            rs_ref[grows, :] += jnp.concatenate(rs_out, axis=0)
            dk_s[...] += dk_acc
            dv_s[...] += dv_acc
            for h in range(2):
                df_s[h:h + 1, :] += df_acc[h]

        @pl.when(j < i)
        def _():
            tile(False)

        @pl.when(j == i)
        def _():
            tile(True)

        @pl.when(i == nq - 1)
        def _():
            dk_ref[...] = dk_s[...]
            dv_ref[...] = dv_s[...]
            df_ref[0] = df_s[...]

        if exchanges:
            pl.when(jnp.logical_and(p == N_PAIRS - 1, t == n_steps - 1))(finish)

    qblk = pl.BlockSpec((tq, PAIR), lambda p, t, ii_r, jj_r: (ii_r[t], p))
    kblk = pl.BlockSpec((tk, PAIR), lambda p, t, ii_r, jj_r: (jj_r[t], p))
    doblk = pl.BlockSpec((tq, PAIR), lambda p, t, ii_r, jj_r: (ii_r[t], N_PAIRS + p))
    whole = pl.BlockSpec((s, PAIR), lambda p, t, ii_r, jj_r: (0, p))
    in_specs = [qblk, kblk, kblk, qblk, pl.BlockSpec((N_HEADS, tk), lambda p, t, ii_r, jj_r: (0, jj_r[t])),
                qblk, qblk, doblk]
    out_specs = [whole, whole, kblk, kblk, pl.BlockSpec((1, 8, tk), lambda p, t, ii_r, jj_r: (p, 0, jj_r[t]))]
    out_shape = [SDS((s, D_ATTN), F32)] * 4 + [SDS((N_PAIRS, 8, s), F32)]
    scratch = [pltpu.VMEM((tk, PAIR), F32), pltpu.VMEM((tk, PAIR), F32), pltpu.VMEM((8, tk), F32)]
    args = [ii, jj, qs, kh, vb, fb, ft, lse, o, dao]
    if exchanges:
        in_specs.append(ANY)
        out_specs.append(ANY)
        out_shape.append(SDS((3,) + exchange_src.shape[1:], exchange_src.dtype))
        scratch += EXCHANGE_SEMS
        args.append(exchange_src)
    grid_spec = pltpu.PrefetchScalarGridSpec(num_scalar_prefetch=2, grid=(N_PAIRS, n_steps), in_specs=in_specs,
                                             out_specs=out_specs, scratch_shapes=scratch)
    return pl.pallas_call(
        body, name=name, grid_spec=grid_spec, out_shape=_in_hbm(out_shape),
        compiler_params=pltpu.CompilerParams(dimension_semantics=("arbitrary", "arbitrary"),
                                             vmem_limit_bytes=40 * MIB))(*[_keep_in_hbm(a) for a in args])


def _qk_bwd(z, dqs, dkh, dv, qn2, kn2, name):
    s = z.shape[0]
    tm = _tile(s)
    col0 = (z.shape[1] - 3 * D_ATTN) // D_ATTN

    def body(q_ref, k_ref, dqs_ref, dkh_ref, dv_ref, qn_ref, kn_ref, dq_ref, dk_ref, dvb_ref, dqn_ref, dkn_ref):
        first = _lane_is_first_head(tm)

        @pl.when(pl.program_id(0) == 0)
        def _():
            dqn_ref[...] = jnp.zeros_like(dqn_ref)
            dkn_ref[...] = jnp.zeros_like(dkn_ref)

        def through(x_ref, dy_ref, gain_ref, dx_ref, dgain_ref, scale):
            for p in range(N_PAIRS):
                sl = slice(p * PAIR, (p + 1) * PAIR)
                xv = x_ref[:, sl]
                r = _pair_rms(xv, first)
                xh = xv * r
                dy = dy_ref[:, sl] * scale
                dgain_ref[:, sl] += jnp.sum(dy * xh, axis=0, keepdims=True)
                dxh = dy * gain_ref[...]
                t = dxh * xh
                m0 = jnp.sum(jnp.where(first, t, 0.0), axis=-1, keepdims=True)
                m1 = jnp.sum(jnp.where(first, 0.0, t), axis=-1, keepdims=True)
                mean = jnp.where(first, m0, m1) / HEAD_DIM
                dx_ref[:, sl] = (r * (dxh - xh * mean)).astype(BF16)

        through(q_ref, dqs_ref, qn_ref, dq_ref, dqn_ref, ATTN_SCALE)
        through(k_ref, dkh_ref, kn_ref, dk_ref, dkn_ref, 1.0)
        dvb_ref[...] = dv_ref[...].astype(BF16)

    wide = lambda col: pl.BlockSpec((tm, D_ATTN), lambda i: (i, col))
    return _call(body, name=name, grid=(s // tm,),
                 in_specs=[wide(col0), wide(col0 + 1), wide(0), wide(0), wide(0), _const((1, PAIR)), _const((1, PAIR))],
                 out_specs=[wide(0), wide(0), wide(0), _const((1, D_ATTN)), _const((1, D_ATTN))],
                 out_shape=[SDS((s, D_ATTN), BF16)] * 3 + [SDS((1, D_ATTN), F32)] * 2,
                 dims=("arbitrary",))(z, z, dqs, dkh, dv, qn2, kn2)


def _gate_bwd(dft, xt, name):
    s = xt.shape[1]
    tm = _tile(s)
    n = s // tm

    def body(df_ref, xt_ref, dxt_ref, dx_ref, db_ref, carry):
        i = pl.program_id(0)

        @pl.when(i == 0)
        def _():
            carry[...] = jnp.zeros_like(carry)
            db_ref[...] = jnp.zeros_like(db_ref)

        tri = (lax.broadcasted_iota(jnp.int32, (tm, tm), 0) >= lax.broadcasted_iota(jnp.int32, (tm, tm), 1)).astype(BF16)
        rc = jnp.zeros((N_HEADS, tm), F32)
        for part in _split3(df_ref[...]):
            rc = rc + jnp.dot(part, tri, preferred_element_type=F32)
        dls = rc + carry[:, 0:1]
        carry[...] = jnp.broadcast_to(dls[:, 0:1], carry.shape)
        dxt = dls * _sigmoid(-xt_ref[...])
        dxt_ref[...] = dxt
        db_ref[...] += jnp.broadcast_to(jnp.sum(dxt, axis=-1, keepdims=True), db_ref.shape)
        padded = jnp.concatenate([dxt, jnp.zeros((V7X_LANES - N_HEADS, tm), F32)], axis=0)
        dx_ref[...] = padded.T

    rev = pl.BlockSpec((N_HEADS, tm), lambda i: (0, n - 1 - i))
    return _call(body, name=name, grid=(n,), in_specs=[rev, rev],
                 out_specs=[rev, pl.BlockSpec((tm, V7X_LANES), lambda i: (n - 1 - i, 0)), _const((N_HEADS, V7X_LANES))],
                 out_shape=[SDS((N_HEADS, s), F32), SDS((s, V7X_LANES), F32), SDS((N_HEADS, V7X_LANES), F32)],
                 scratch=[pltpu.VMEM((N_HEADS, V7X_LANES), F32)], dims=("arbitrary",))(dft, xt)


def _conv_c_fwd(z, cw, name):
    s = z.shape[0]
    c = z.shape[1] // 3
    tm, halo, kw = _tile(s), CONV_C_HALO, CONV_C_WIDTH

    def body(gb_ref, gc_ref, hh_ref, gcp_ref, hhp_ref, cw_ref, y_ref, buf):
        i = pl.program_id(0)
        buf[0:halo, :] = jnp.where(i > 0, gcp_ref[...] * hhp_ref[...], 0.0)
        buf[halo:halo + tm, :] = gc_ref[...] * hh_ref[...]
        c1 = jnp.zeros((tm, c), F32)
        for k in range(kw):
            c1 = c1 + cw_ref[k:k + 1, :] * buf[pl.ds(halo - (kw - 1) + k, tm), :]
        y_ref[...] = (gb_ref[...] * c1).astype(BF16)

    return _call(body, name=name, grid=(s // tm,),
                 in_specs=[_rows(tm, c, 0), _rows(tm, c, 1), _rows(tm, c, 2), _prev_rows(halo, tm, c, 1),
                           _prev_rows(halo, tm, c, 2), _const(cw.shape)],
                 out_specs=_rows(tm, c), out_shape=SDS((s, c), BF16),
                 scratch=[pltpu.VMEM((tm + halo, c), F32)], dims=("parallel",))(z, z, z, z, z, cw)


def _conv_c_bwd(dy0, z, cw, name):
    s = z.shape[0]
    c = z.shape[1] // 3
    tm, halo, kw = _tile(s), CONV_C_HALO, CONV_C_WIDTH
    n = s // tm

    def body(dy_ref, dyn_ref, gb_ref, gbn_ref, gc_ref, hh_ref, gcp_ref, hhp_ref, cw_ref, dz_ref, dcw_ref, buf, bd):
        i = pl.program_id(0)
        gcv, hhv, dyv = gc_ref[...], hh_ref[...], dy_ref[...]
        buf[0:halo, :] = jnp.where(i > 0, gcp_ref[...] * hhp_ref[...], 0.0)
        buf[halo:halo + tm, :] = gcv * hhv
        dc1 = dyv * gb_ref[...]
        bd[0:tm, :] = dc1
        bd[tm:tm + halo, :] = jnp.where(i < n - 1, dyn_ref[...] * gbn_ref[...], 0.0)

        @pl.when(i == 0)
        def _():
            dcw_ref[...] = jnp.zeros_like(dcw_ref)

        c1 = jnp.zeros((tm, c), F32)
        dc0 = jnp.zeros((tm, c), F32)
        for k in range(kw):
            shifted = buf[pl.ds(halo - (kw - 1) + k, tm), :]
            c1 = c1 + cw_ref[k:k + 1, :] * shifted
            dc0 = dc0 + cw_ref[k:k + 1, :] * bd[pl.ds(kw - 1 - k, tm), :]
            dcw_ref[k:k + 1, :] += jnp.sum(dc1 * shifted, axis=0, keepdims=True)
        dz_ref[:, 0:c] = (dyv * c1).astype(BF16)
        dz_ref[:, c:2 * c] = (dc0 * hhv).astype(BF16)
        dz_ref[:, 2 * c:3 * c] = (dc0 * gcv).astype(BF16)

    return _call(body, name=name, grid=(n,),
                 in_specs=[_rows(tm, c), _next_rows(halo, tm, c, 0, s // halo), _rows(tm, c, 0),
                           _next_rows(halo, tm, c, 0, s // halo), _rows(tm, c, 1), _rows(tm, c, 2),
                           _prev_rows(halo, tm, c, 1), _prev_rows(halo, tm, c, 2), _const(cw.shape)],
                 out_specs=[_rows(tm, 3 * c), _const(cw.shape)],
                 out_shape=[SDS((s, 3 * c), BF16), SDS(cw.shape, F32)],
                 scratch=[pltpu.VMEM((tm + halo, c), F32)] * 2, dims=("arbitrary",),
                 vmem_mb=48)(dy0, dy0, z, z, z, z, z, z, cw)


def _loss_head(y, target, name):
    s, d = y.shape
    tm = _tile(s)

    def body(y_ref, t_ref, loss_ref, dy_ref):
        e = y_ref[...] - t_ref[...]

        @pl.when(pl.program_id(0) == 0)
        def _():
            loss_ref[...] = jnp.zeros_like(loss_ref)

        loss_ref[...] += 0.5 * jnp.sum(jnp.mean(e * e, axis=-1, keepdims=True))
        dy_ref[...] = e / d

    return _call(body, name=name, grid=(s // tm,), in_specs=[_rows(tm, d), _rows(tm, d)],
                 out_specs=[_const((8, V7X_LANES)), _rows(tm, d)],
                 out_shape=[SDS((8, V7X_LANES), F32), SDS((s, d), F32)], dims=("arbitrary",))(y, target)


def _adamw(w, g, m, v, name):
    r, c = w.shape
    tr = next((t for t in (512, 256, 128, 64, 32, 16, 8) if r % t == 0), r)

    def body(w_ref, g_ref, m_ref, v_ref, d_ref, mo_ref, vo_ref):
        gv = g_ref[...]
        mn = ADAM_B1 * m_ref[...] + (1.0 - ADAM_B1) * gv
        vn = ADAM_B2 * v_ref[...] + (1.0 - ADAM_B2) * (gv * gv)
        m_hat = mn / (1.0 - ADAM_B1 ** ADAM_STEP)
        v_hat = vn / (1.0 - ADAM_B2 ** ADAM_STEP)
        d_ref[...] = -ADAM_LR * (m_hat / (jnp.sqrt(v_hat) + ADAM_EPS) + ADAM_WD * w_ref[...])
        mo_ref[...] = mn
        vo_ref[...] = vn

    spec = _rows(tr, c)
    return _call(body, name=name, grid=(r // tr,), in_specs=[spec] * 4, out_specs=[spec] * 3,
                 out_shape=[SDS((r, c), F32)] * 3, dims=("parallel",))(w, g, m, v)


def _position():
    return lax.axis_index("x"), lax.axis_index("y"), lax.axis_index("c")


def _other_chips(x, y):
    return [(1 - x, y), (x, 1 - y), (1 - x, 1 - y)]


def _dev_index(px, py, pc):
    return 4 * px + 2 * py + pc


def _all_gather(wloc):
    r, d = wloc.shape

    def body(x_ref, out_ref, send_sems, recv_sems, local_sem):
        start, forward, finish = _gather_phases(x_ref, out_ref, send_sems, recv_sems, local_sem)
        start()
        forward()
        finish()

    return _call(body, name="all_gather_weights", in_specs=[ANY], out_specs=ANY,
                 out_shape=SDS((N_DEV, r, d), wloc.dtype), scratch=GATHER_SEMS)(wloc)


GATHER_SEMS = [pltpu.SemaphoreType.DMA((7,)), pltpu.SemaphoreType.DMA((7,)), pltpu.SemaphoreType.DMA((1,))]


def _gather_phases(x_ref, out_ref, send_sems, recv_sems, local_sem):
    x, y, c = _position()
    me, sibling = (x, y, c), (x, y, 1 - c)
    chips = _other_chips(x, y)

    def slot(dev):
        return out_ref.at[_dev_index(*dev)]

    def copy(k, block, to, src=None):
        return pltpu.make_async_remote_copy(
            src_ref=slot(block) if src is None else src, dst_ref=slot(block),
            send_sem=send_sems.at[k], recv_sem=recv_sems.at[k], device_id=to, device_id_type=MESH)

    mine = pltpu.make_async_copy(x_ref, slot(me), local_sem.at[0])
    first = [copy(0, me, sibling, src=x_ref)] + [copy(1 + j, me, (*chip, c), src=x_ref) for j, chip in enumerate(chips)]
    passed = [copy(4 + j, (*chip, c), sibling) for j, chip in enumerate(chips)]

    def start():
        mine.start()
        for cp in first:
            cp.start()

    def forward():
        for j, chip in enumerate(chips):
            copy(1 + j, (*chip, c), me).wait_recv()
            passed[j].start()

    def finish():
        copy(0, sibling, me).wait_recv()
        for j, chip in enumerate(chips):
            copy(4 + j, (*chip, 1 - c), me).wait_recv()
        for cp in first + passed:
            cp.wait_send()
        mine.wait()

    return start, forward, finish


def _row_block(r):
    return next(t for t in range(704, 0, -BF16_ROWS) if r % t == 0)


def _pair_exchange(gall, name):
    def body(g_ref, out_ref, send_sems, recv_sems):
        start, finish = _pair_exchange_phases(g_ref, out_ref, send_sems, recv_sems)
        start()
        finish()

    return _call(body, name=name, in_specs=[ANY], out_specs=ANY, out_shape=_pair_exchange_shape(gall),
                 scratch=PAIR_EXCHANGE_SEMS)(gall)


PAIR_EXCHANGE_SEMS = [pltpu.SemaphoreType.DMA((4,)), pltpu.SemaphoreType.DMA((4,))]


def _pair_exchange_shape(gall):
    return SDS((4,) + gall.shape[1:], gall.dtype)


def _pair_exchange_phases(g_ref, out_ref, send_sems, recv_sems):
    x, y, c = _position()
    sibling = (x, y, 1 - c)
    dests = [sibling] + [(*chip, 1 - c) for chip in _other_chips(x, y)]
    copies = [pltpu.make_async_remote_copy(
        src_ref=g_ref.at[_dev_index(*dest)], dst_ref=out_ref.at[k], send_sem=send_sems.at[k],
        recv_sem=recv_sems.at[k], device_id=sibling, device_id_type=MESH) for k, dest in enumerate(dests)]

    def start():
        for cp in copies:
            cp.start()

    def finish():
        for cp in copies:
            cp.wait()

    return start, finish


def _pair_sum(gall, sib, idx, name):
    _, r, d = gall.shape
    tr = _row_block(r)

    def body(idx_ref, a_ref, b_ref, o_ref):
        o_ref[...] = (a_ref[...].astype(F32) + b_ref[...].astype(F32)).astype(o_ref.dtype)

    grid_spec = pltpu.PrefetchScalarGridSpec(
        num_scalar_prefetch=1, grid=(4, r // tr),
        in_specs=[pl.BlockSpec((1, tr, d), lambda k, i, idx_ref: (idx_ref[k], i, 0)),
                  pl.BlockSpec((1, tr, d), lambda k, i, idx_ref: (k, i, 0))],
        out_specs=pl.BlockSpec((1, tr, d), lambda k, i, idx_ref: (k, i, 0)))
    return pl.pallas_call(body, name=name, grid_spec=grid_spec,
                          out_shape=_in_hbm(SDS((4, r, d), gall.dtype)),
                          compiler_params=pltpu.CompilerParams(dimension_semantics=("parallel", "parallel")))(
        idx, _keep_in_hbm(gall), _keep_in_hbm(sib))


def _chip_exchange(tsum):
    _, r, d = tsum.shape

    def body(t_ref, out_ref, send_sems, recv_sems):
        start, finish = _chip_exchange_phases(t_ref, out_ref, send_sems, recv_sems)
        start()
        finish()

    return _call(body, name="reduce_scatter_chip_exchange", in_specs=[ANY], out_specs=ANY,
                 out_shape=SDS((3, r, d), tsum.dtype), scratch=EXCHANGE_SEMS)(tsum)


EXCHANGE_SEMS = [pltpu.SemaphoreType.DMA((3,)), pltpu.SemaphoreType.DMA((3,))]


def _chip_exchange_phases(t_ref, out_ref, send_sems, recv_sems):
    x, y, c = _position()
    copies = [pltpu.make_async_remote_copy(
        src_ref=t_ref.at[1 + k], dst_ref=out_ref.at[k], send_sem=send_sems.at[k], recv_sem=recv_sems.at[k],
        device_id=(*chip, c), device_id_type=MESH) for k, chip in enumerate(_other_chips(x, y))]

    def start():
        for cp in copies:
            cp.start()

    def finish():
        for cp in copies:
            cp.wait()

    return start, finish


def _chip_exchange_shape(tsum):
    return SDS((3,) + tsum.shape[1:], tsum.dtype)


EXCHANGES = dict(pair=(_pair_exchange_phases, _pair_exchange_shape, PAIR_EXCHANGE_SEMS),
                 chip=(_chip_exchange_phases, _chip_exchange_shape, EXCHANGE_SEMS))


def _final_sum(tsum, rcv, name):
    _, r, d = tsum.shape
    tr = _row_block(r)

    def body(t_ref, r_ref, o_ref):
        acc = t_ref[0].astype(F32)
        for k in range(3):
            acc = acc + r_ref[k].astype(F32)
        o_ref[...] = acc

    return _call(body, name=name, grid=(r // tr,),
                 in_specs=[pl.BlockSpec((1, tr, d), lambda i: (0, i, 0)), pl.BlockSpec((3, tr, d), lambda i: (0, i, 0))],
                 out_specs=_rows(tr, d), out_shape=SDS((r, d), F32), dims=("parallel",))(tsum, rcv)


def _all_reduce_small(buf):
    nr, lanes = buf.shape

    def body(b_ref, out_ref, gath, send_sems, recv_sems):
        x, y, c = _position()
        my_slot = _dev_index(x, y, c)
        gath[my_slot] = b_ref[...]
        copies = []
        for k in range(1, N_DEV):
            dx, dy, dc = (k >> 2) & 1, (k >> 1) & 1, k & 1
            peer = (1 - x if dx else x, 1 - y if dy else y, 1 - c if dc else c)
            copies.append(pltpu.make_async_remote_copy(
                src_ref=b_ref, dst_ref=gath.at[my_slot], send_sem=send_sems.at[k - 1], recv_sem=recv_sems.at[k - 1],
                device_id=peer, device_id_type=MESH))
        for cp in copies:
            cp.start()
        for cp in copies:
            cp.wait()
        acc = gath[0]
        for sidx in range(1, N_DEV):
            acc = acc + gath[sidx]
        out_ref[...] = acc

    return _call(body, name="all_reduce_small", in_specs=[VMEM], out_specs=VMEM, out_shape=SDS((nr, lanes), F32),
                 scratch=[pltpu.VMEM((N_DEV, nr, lanes), F32), pltpu.SemaphoreType.DMA((7,)),
                          pltpu.SemaphoreType.DMA((7,))])(buf)


def _ffn_block_fwd(x, gain, wall, offs, fs, tag, gather_src=None):
    res = _ffn_fwd(x, gain, wall, offs, fs, f"{tag}_fwd", gather_src)
    out, xn, g, u, h = res[:5]
    return out, (x, gain, xn, g, u, h), (res[5] if gather_src is not None else None)


def _ffn_block_bwd(dout, saved, wall, offs, fs, tag, exchange_src=None):
    x, gain, xn, g, u, h = saved
    res = _ffn_bwd_act(dout, g, u, wall, offs, fs, f"{tag}_bwd_act", exchange_src)
    dg, du, dy_b, dxn = res[:4]
    dwg = _mm_tn(dg, xn, f"{tag}_dwg", BF16)
    dwu = _mm_tn(du, xn, f"{tag}_dwu", BF16)
    dwd = _mm_tn(h, dy_b, f"{tag}_dwd", BF16)
    dx, dgain = _rmsnorm_bwd(x, gain, dxn, dout, f"{tag}_norm_bwd")
    return dx, (dwg, dwu, dwd), dgain, (res[4] if exchange_src is not None else None)


def _local_step(x, target, wall_a, fs, small, plan):
    grads = {}
    first, second = (0, fs, 2 * fs), (3 * fs, 4 * fs, 5 * fs)
    reduces = "pair_sum" in plan

    x1, s_f1a, wall_b = _ffn_block_fwd(x, small["ffn1_norm"][0], wall_a, first, fs, "l0_ffn1", plan.get("shard_b"))
    wall_b = plan.get("wall_b", wall_b)
    mixw = plan["mix_b"](wall_b)
    hn0 = _rmsnorm_fwd(x1, small["mix_norm"][0], "l0_mix_norm")
    z = _mm(hn0, mixw["ev_w_main_t"], "nt", "ev_in_proj")
    flog = _mm(hn0, mixw["ev_w_f_t"], "nt", "ev_in_proj_gate")
    a, a1 = _conv_a_fwd(z, mixw["ev_conv_w32"], small["ev_conv_b"], small["ev_conv_norm"], "ev_conv_fwd")
    qs, kh, vb, fb, ft, xt = _qk_fwd(z, flog, small["ev_b_f128"], small["ev_q_norm2"], small["ev_k_norm2"], "ev_qk_fwd")
    if "wall_c" in plan:
        o, lse = _attn_fwd(qs, kh, vb, fb, ft, "ev_attn_fwd")
        wall_c = plan["wall_c"]
    else:
        o, lse, wall_c = _attn_fwd(qs, kh, vb, fb, ft, "ev_attn_fwd", gather_src=plan["shard_c"])
    mixw = {**mixw, **plan["mix_c"](wall_c)}
    ao = jnp.concatenate([a, o.astype(BF16)], axis=1)
    x2 = _mm(ao, mixw["ev_w_out"], "nn", "ev_out_proj", add=x1)
    x3, s_f2a, _ = _ffn_block_fwd(x2, small["ffn2_norm"][0], wall_c, first, fs, "l0_ffn2")

    x4, s_f1b, wall_d = _ffn_block_fwd(x3, small["ffn1_norm"][1], wall_c, second, fs, "l1_ffn1", plan.get("shard_d"))
    wall_d = plan.get("wall_d", wall_d)
    hn1 = _rmsnorm_fwd(x4, small["mix_norm"][1], "l1_mix_norm")
    zo = _mm(hn1, mixw["od_w_in_t"], "nt", "od_in_proj")
    y0 = _conv_c_fwd(zo, mixw["od_conv_w8"], "od_conv_fwd")
    x5 = _mm(y0, mixw["od_w_out"], "nn", "od_out_proj", add=x4)
    x6, s_f2b, _ = _ffn_block_fwd(x5, small["ffn2_norm"][1], wall_d, first, fs, "l1_ffn2")

    loss, d6 = _loss_head(x6, target, "loss_head")

    d5, grads["l1_ffn2"], grads["ffn2_norm_1"], _ = _ffn_block_bwd(d6, s_f2b, wall_d, first, fs, "l1_ffn2")
    d5b = d5.astype(BF16)
    dy0 = _mm(d5b, mixw["od_w_out"], "nt", "od_out_proj_bwd")
    grads["od_w_out"] = _mm_tn(y0, d5b, "od_dw_out", BF16)
    dzo, grads["od_conv_w"] = _conv_c_bwd(dy0, zo, mixw["od_conv_w8"], "od_conv_bwd")
    dh1 = _mm(dzo, mixw["od_w_in_t"], "nn", "od_in_proj_bwd")
    grads["od_w_in_t"] = _mm_tn(dzo, hn1, "od_dw_in", BF16)
    d4, grads["mix_norm_1"] = _rmsnorm_bwd(x4, small["mix_norm"][1], dh1, d5, "l1_mix_norm_bwd")
    d3, grads["l1_ffn1"], grads["ffn1_norm_1"], _ = _ffn_block_bwd(d4, s_f1b, wall_c, second, fs, "l1_ffn1")

    d2, grads["l0_ffn2"], grads["ffn2_norm_0"], _ = _ffn_block_bwd(d3, s_f2a, wall_c, first, fs, "l0_ffn2")
    partials_c = plan["partials_c"](grads) if reduces else None
    d2b = d2.astype(BF16)
    dao = _mm(d2b, mixw["ev_w_out"], "nt", "ev_out_proj_bwd")
    grads["ev_w_out"] = _mm_tn(ao, d2b, "ev_dw_out", BF16)
    da1, grads["ev_conv_norm"], grads["ev_conv_b"] = _conv_a_bwd_norm(dao, a1, small["ev_conv_norm"], "ev_conv_bwd_norm")
    res = _conv_a_bwd_conv(da1, z, mixw["ev_conv_w32"], "ev_conv_bwd_conv",
                           exchange_src=("pair", partials_c) if reduces else None)
    du, dg, grads["ev_conv_w"] = res[:3]
    sums_c = plan["pair_sum"](partials_c, res[3], "c") if reduces else None
    res = _attn_bwd(qs, kh, vb, fb, ft, lse, o, dao, "ev_attn_bwd", exchange_src=sums_c)
    dqs, rs, dkh, dv, df4 = res[:5]
    if reduces:
        grads["pair_sums_c"], grads["exchanged_c"] = sums_c, res[5]
    dq, dk, dvb, grads["ev_q_norm"], grads["ev_k_norm"] = _qk_bwd(
        z, dqs, dkh, dv, small["ev_q_norm2"], small["ev_k_norm2"], "ev_qk_bwd")
    dft = df4[:, 0:2, :].reshape(N_HEADS, -1) + rs.reshape(-1, N_HEADS, HEAD_DIM)[:, :, 0].T
    dxt, dflog, grads["ev_b_f"] = _gate_bwd(dft, xt, "ev_gate_bwd")
    dz = jnp.concatenate([du, dg, dq, dk, dvb], axis=1)
    dflog_b = dflog.astype(BF16)
    dh0 = _mm(dz, mixw["ev_w_main_t"], "nn", "ev_in_proj_bwd")
    dh0 = _mm(dflog_b, mixw["ev_w_f_t"], "nn", "ev_in_proj_gate_bwd", add=dh0)
    dw_main = _mm_tn(dz, hn0, "ev_dw_in", BF16)
    dw_f = _mm(dxt.astype(BF16), hn0, "nn", "ev_dw_in_gate", BF16)
    grads["ev_w_in_t"] = jnp.concatenate([dw_main, dw_f], axis=0)
    d1, grads["mix_norm_0"] = _rmsnorm_bwd(x1, small["mix_norm"][0], dh0, d2, "l0_mix_norm_bwd")
    sums_b = None
    if reduces:
        partials_b = plan["partials_b"](grads)
        sums_b = plan["pair_sum"](partials_b, _pair_exchange(partials_b, "reduce_scatter_pair_exchange_b"), "b")
    d0, grads["l0_ffn1"], grads["ffn1_norm_0"], exchanged_b = _ffn_block_bwd(
        d1, s_f1a, wall_a, first, fs, "l0_ffn1", exchange_src=("chip", sums_b) if reduces else None)
    if reduces:
        grads["pair_sums_b"], grads["exchanged_b"] = sums_b, exchanged_b
    return loss, d0, grads


def _round_up(n, m):
    return -(-n // m) * m


def _pad_rows(a, rows):
    return jnp.pad(a, ((0, rows - a.shape[0]), (0, 0)))


SMALL_ORDER = ("loss", "ffn1_norm", "mix_norm", "ffn2_norm", "ev_b_f", "ev_conv_b", "ev_conv_norm",
               "ev_q_norm", "ev_k_norm", "ev_conv_w", "od_conv_w")


def _pack_small(parts):
    flat = jnp.concatenate([parts[k].reshape(-1).astype(F32) for k in SMALL_ORDER])
    n = _round_up(flat.shape[0], 8 * V7X_LANES)
    return jnp.pad(flat, (0, n - flat.shape[0])).reshape(-1, V7X_LANES)


def _unpack_small(buf, shapes):
    flat = buf.reshape(-1)
    out, pos = {}, 0
    for k in SMALL_ORDER:
        n = math.prod(shapes[k])
        out[k] = flat[pos:pos + n].reshape(shapes[k])
        pos += n
    return out


def kernel(x, ffn1_norm, ffn1_w_gate, ffn1_w_up, ffn1_w_down, mix_norm, ffn2_norm, ffn2_w_gate, ffn2_w_up, ffn2_w_down, ev_w_in, ev_b_f, ev_conv_w, ev_conv_b, ev_conv_norm, ev_q_norm, ev_k_norm, ev_w_out, od_w_in, od_conv_w, od_w_out, loss_target, m_ffn1_norm, m_ffn1_w_gate, m_ffn1_w_up, m_ffn1_w_down, m_mix_norm, m_ffn2_norm, m_ffn2_w_gate, m_ffn2_w_up, m_ffn2_w_down, m_ev_w_in, m_ev_b_f, m_ev_conv_w, m_ev_conv_b, m_ev_conv_norm, m_ev_q_norm, m_ev_k_norm, m_ev_w_out, m_od_w_in, m_od_conv_w, m_od_w_out, v_ffn1_norm, v_ffn1_w_gate, v_ffn1_w_up, v_ffn1_w_down, v_mix_norm, v_ffn2_norm, v_ffn2_w_gate, v_ffn2_w_up, v_ffn2_w_down, v_ev_w_in, v_ev_b_f, v_ev_conv_w, v_ev_conv_b, v_ev_conv_norm, v_ev_q_norm, v_ev_k_norm, v_ev_w_out, v_od_w_in, v_od_conv_w, v_od_w_out):
    weights = dict(ffn1_norm=ffn1_norm, ffn1_w_gate=ffn1_w_gate, ffn1_w_up=ffn1_w_up, ffn1_w_down=ffn1_w_down,
                   mix_norm=mix_norm, ffn2_norm=ffn2_norm, ffn2_w_gate=ffn2_w_gate, ffn2_w_up=ffn2_w_up,
                   ffn2_w_down=ffn2_w_down, ev_w_in=ev_w_in, ev_b_f=ev_b_f, ev_conv_w=ev_conv_w, ev_conv_b=ev_conv_b,
                   ev_conv_norm=ev_conv_norm, ev_q_norm=ev_q_norm, ev_k_norm=ev_k_norm, ev_w_out=ev_w_out,
                   od_w_in=od_w_in, od_conv_w=od_conv_w, od_w_out=od_w_out)
    m_in = dict(ffn1_norm=m_ffn1_norm, ffn1_w_gate=m_ffn1_w_gate, ffn1_w_up=m_ffn1_w_up, ffn1_w_down=m_ffn1_w_down,
                mix_norm=m_mix_norm, ffn2_norm=m_ffn2_norm, ffn2_w_gate=m_ffn2_w_gate, ffn2_w_up=m_ffn2_w_up,
                ffn2_w_down=m_ffn2_w_down, ev_w_in=m_ev_w_in, ev_b_f=m_ev_b_f, ev_conv_w=m_ev_conv_w,
                ev_conv_b=m_ev_conv_b, ev_conv_norm=m_ev_conv_norm, ev_q_norm=m_ev_q_norm, ev_k_norm=m_ev_k_norm,
                ev_w_out=m_ev_w_out, od_w_in=m_od_w_in, od_conv_w=m_od_conv_w, od_w_out=m_od_w_out)
    v_in = dict(ffn1_norm=v_ffn1_norm, ffn1_w_gate=v_ffn1_w_gate, ffn1_w_up=v_ffn1_w_up, ffn1_w_down=v_ffn1_w_down,
                mix_norm=v_mix_norm, ffn2_norm=v_ffn2_norm, ffn2_w_gate=v_ffn2_w_gate, ffn2_w_up=v_ffn2_w_up,
                ffn2_w_down=v_ffn2_w_down, ev_w_in=v_ev_w_in, ev_b_f=v_ev_b_f, ev_conv_w=v_ev_conv_w,
                ev_conv_b=v_ev_conv_b, ev_conv_norm=v_ev_conv_norm, ev_q_norm=v_ev_q_norm, ev_k_norm=v_ev_k_norm,
                ev_w_out=v_ev_w_out, od_w_in=v_od_w_in, od_conv_w=v_od_conv_w, od_w_out=v_od_w_out)
    order = list(weights)

    d = x.shape[-1]
    fs = ffn1_w_gate.shape[2]
    n_in = ev_w_in.shape[2]
    n_in_pad = _round_up(n_in, BF16_ROWS)
    n_out = ev_w_out.shape[1]
    n_od = od_w_in.shape[2]
    d_conv = ev_conv_b.shape[1]
    d_in_even = n_in * N_DEV
    d_main = d_in_even - N_HEADS
    cx, cy, cc = _position()
    me = _dev_index(cx, cy, cc)

    def block(wg, wu, wd, layer):
        return [wg[layer].T, wu[layer].T, wd[layer]]

    def stack(parts):
        return jnp.concatenate([p.astype(BF16) for p in parts], axis=0)

    ffn1, ffn2 = (ffn1_w_gate, ffn1_w_up, ffn1_w_down), (ffn2_w_gate, ffn2_w_up, ffn2_w_down)
    shard_a = stack(block(*ffn1, 0))
    def bits_rows(t, rows):
        flat = lax.bitcast_convert_type(t.reshape(-1), BF16).reshape(-1)
        return jnp.pad(flat, (0, rows * d - flat.shape[0])).reshape(rows, d)

    def rows_bits(rows_bf16, per_dev):
        flat = rows_bf16.reshape(N_DEV, -1)[:, :2 * per_dev].reshape(N_DEV, per_dev, 2)
        return lax.bitcast_convert_type(flat, F32)

    ev_tap_rows = -(-2 * ev_conv_w[0].size // d)
    tap_rows = jnp.concatenate([bits_rows(ev_conv_w[0], ev_tap_rows),
                                bits_rows(od_conv_w[0], BF16_ROWS - ev_tap_rows)], axis=0)
    shard_b = jnp.concatenate([stack([_pad_rows(ev_w_in[0].T, n_in_pad), ev_w_out[0]]), tap_rows], axis=0)
    off_taps = n_in_pad + n_out
    shard_c = stack(block(*ffn2, 0) + block(*ffn1, 1) + [od_w_in[0].T, od_w_out[0]])
    shard_d = stack(block(*ffn2, 1))
    off_ev_in, off_ev_out = 0, n_in_pad
    off_od_in, off_od_out = 6 * fs, 6 * fs + n_od
    wall_a = _all_gather(shard_a)

    def even_mixer_weights(wall_b):
        ev_w_in_t = wall_b[:, off_ev_in:off_ev_in + n_in, :].reshape(d_in_even, d)
        taps = wall_b[:, off_taps:off_taps + BF16_ROWS, :]
        ev_taps = rows_bits(taps[:, :ev_tap_rows], ev_conv_w[0].size).reshape(N_DEV, CONV_A_WIDTH, -1)
        od_taps = rows_bits(taps[:, ev_tap_rows:], od_conv_w[0].size).reshape(N_DEV, CONV_C_WIDTH, -1)
        by_channel = lambda t: t.transpose(1, 0, 2).reshape(t.shape[1], -1)
        return dict(ev_w_main_t=ev_w_in_t[:d_main], ev_w_f_t=_pad_rows(ev_w_in_t[d_main:], V7X_LANES),
                    ev_w_out=wall_b[:, off_ev_out:off_ev_out + n_out, :].reshape(N_DEV * n_out, d),
                    ev_conv_w32=_pad_rows(by_channel(ev_taps), CONV_A_WIDTH + 1),
                    od_conv_w8=_pad_rows(by_channel(od_taps), 8))

    def odd_mixer_weights(wall_c):
        return dict(od_w_in_t=wall_c[:, off_od_in:off_od_in + n_od, :].reshape(N_DEV * n_od, d),
                    od_w_out=wall_c[:, off_od_out:off_od_out + n_out, :].reshape(N_DEV * n_out, d))

    def by_dev(a, rows, pad_to=None):
        a = a.reshape(N_DEV, rows, d)
        return a if pad_to is None else jnp.pad(a, ((0, 0), (0, pad_to - rows), (0, 0)))

    idx = jnp.stack([me] + [_dev_index(*chip, cc) for chip in _other_chips(cx, cy)]).astype(jnp.int32)

    def pair_sum(partials, from_sibling, tag):
        return _pair_sum(partials, from_sibling, idx, f"reduce_scatter_pair_sum_{tag}")

    def ffn_pieces(g, key):
        return [by_dev(t, fs) for t in g[key]]

    conv_shapes = dict(ev_conv_w=(CONV_A_WIDTH, d_conv), od_conv_w=(CONV_C_WIDTH, d))
    small = dict(
        ffn1_norm=[ffn1_norm[l][None] for l in range(2)], mix_norm=[mix_norm[l][None] for l in range(2)],
        ffn2_norm=[ffn2_norm[l][None] for l in range(2)],
        ev_conv_b=ev_conv_b, ev_conv_norm=ev_conv_norm,
        ev_b_f128=jnp.pad(ev_b_f, ((0, 0), (0, V7X_LANES - N_HEADS))),
        ev_q_norm2=jnp.tile(ev_q_norm, (1, 2)), ev_k_norm2=jnp.tile(ev_k_norm, (1, 2)),
    )

    plan = dict(
        shard_b=shard_b, shard_c=shard_c, shard_d=shard_d, mix_b=even_mixer_weights, mix_c=odd_mixer_weights, pair_sum=pair_sum,
        partials_c=lambda g1: jnp.concatenate(
            ffn_pieces(g1, "l0_ffn2") + ffn_pieces(g1, "l1_ffn1") + ffn_pieces(g1, "l1_ffn2")
            + [by_dev(g1["od_w_in_t"], n_od), by_dev(g1["od_w_out"], n_out)], axis=1),
        partials_b=lambda g1: jnp.concatenate(
            [by_dev(g1["ev_w_in_t"], n_in, n_in_pad), by_dev(g1["ev_w_out"], n_out)], axis=1))
    loss_p, grad_x, g = _local_step(x[0], loss_target[0], wall_a, fs, small, plan)

    partials_a = jnp.concatenate(ffn_pieces(g, "l0_ffn1"), axis=1)
    sums_a = pair_sum(partials_a, _pair_exchange(partials_a, "reduce_scatter_pair_exchange_a"), "a")
    gsum_a = _final_sum(sums_a, _chip_exchange(sums_a), "reduce_scatter_final_sum_a")
    gsum_b = _final_sum(g["pair_sums_b"], g["exchanged_b"], "reduce_scatter_final_sum_b")
    gsum_c = _final_sum(g["pair_sums_c"], g["exchanged_c"], "reduce_scatter_final_sum_c")

    grad = {}
    where = dict(ffn1=((gsum_a, 0), (gsum_c, 3 * fs)), ffn2=((gsum_c, 0), (gsum_c, 6 * fs)))
    for blk, places in where.items():
        for wi, kind in enumerate(("gate", "up", "down")):
            rows = [buf[off + wi * fs:off + (wi + 1) * fs] for buf, off in places]
            grad[f"{blk}_w_{kind}"] = jnp.stack(rows if kind == "down" else [r.T for r in rows])
    grad["ev_w_in"] = gsum_b[off_ev_in:off_ev_in + n_in].T[None]
    grad["ev_w_out"] = gsum_b[off_ev_out:off_ev_out + n_out][None]
    grad["od_w_in"] = gsum_c[9 * fs:9 * fs + n_od].T[None]
    grad["od_w_out"] = gsum_c[9 * fs + n_od:9 * fs + n_od + n_out][None]

    heads = lambda t: t.reshape(N_HEADS, HEAD_DIM).sum(axis=0)
    parts = dict(
        loss=loss_p[0, 0:1],
        ffn1_norm=jnp.stack([g["ffn1_norm_0"][0], g["ffn1_norm_1"][0]]),
        mix_norm=jnp.stack([g["mix_norm_0"][0], g["mix_norm_1"][0]]),
        ffn2_norm=jnp.stack([g["ffn2_norm_0"][0], g["ffn2_norm_1"][0]]),
        ev_b_f=g["ev_b_f"][:, 0], ev_conv_b=g["ev_conv_b"], ev_conv_norm=g["ev_conv_norm"],
        ev_q_norm=heads(g["ev_q_norm"]), ev_k_norm=heads(g["ev_k_norm"]),
        ev_conv_w=g["ev_conv_w"][:CONV_A_WIDTH], od_conv_w=g["od_conv_w"][:CONV_C_WIDTH])
    small_shapes = dict(loss=(1,), ffn1_norm=ffn1_norm.shape, mix_norm=mix_norm.shape, ffn2_norm=ffn2_norm.shape,
                        ev_b_f=ev_b_f.shape, ev_conv_b=ev_conv_b.shape, ev_conv_norm=ev_conv_norm.shape,
                        ev_q_norm=ev_q_norm.shape, ev_k_norm=ev_k_norm.shape, **conv_shapes)
    red = _unpack_small(_all_reduce_small(_pack_small(parts)), small_shapes)
    loss = red["loss"][0]
    for k in ("ffn1_norm", "mix_norm", "ffn2_norm", "ev_b_f", "ev_conv_b", "ev_conv_norm", "ev_q_norm", "ev_k_norm"):
        grad[k] = red[k]
    grad["ev_conv_w"] = lax.dynamic_slice(red["ev_conv_w"], (0, me * ev_conv_w.shape[2]),
                                          (CONV_A_WIDTH, ev_conv_w.shape[2]))[None]
    grad["od_conv_w"] = lax.dynamic_slice(red["od_conv_w"], (0, me * od_conv_w.shape[2]),
                                          (CONV_C_WIDTH, od_conv_w.shape[2]))[None]

    big = ("ffn1_w_gate", "ffn1_w_up", "ffn1_w_down", "ffn2_w_gate", "ffn2_w_up", "ffn2_w_down",
           "ev_w_in", "ev_w_out", "od_w_in", "od_w_out")
    delta, new_m, new_v = {}, {}, {}
    for k in big:
        shp = weights[k].shape
        flat = lambda t: t.reshape(-1, shp[-1])
        dk, mk, vk = _adamw(flat(weights[k]), flat(grad[k]), flat(m_in[k]), flat(v_in[k]), f"adamw_{k}")
        delta[k], new_m[k], new_v[k] = dk.reshape(shp), mk.reshape(shp), vk.reshape(shp)
    rest = [k for k in order if k not in big]
    cat = lambda src: jnp.concatenate([src[k].reshape(-1) for k in rest])
    n_small = sum(math.prod(weights[k].shape) for k in rest)
    n_pad = _round_up(n_small, 8 * V7X_LANES)
    as_rows = lambda t: jnp.pad(t, (0, n_pad - n_small)).reshape(-1, V7X_LANES)
    v_rows = jnp.pad(cat(v_in), (0, n_pad - n_small), constant_values=1.0).reshape(-1, V7X_LANES)
    ds, ms, vs = _adamw(as_rows(cat(weights)), as_rows(cat(grad)), as_rows(cat(m_in)), v_rows, "adamw_small")
    pos = 0
    for k in rest:
        n = math.prod(weights[k].shape)
        for dst, src in ((delta, ds), (new_m, ms), (new_v, vs)):
            dst[k] = src.reshape(-1)[pos:pos + n].reshape(weights[k].shape)
        pos += n

    return (loss, grad_x[None], *[grad[k] for k in order], *[delta[k] for k in order],
            *[new_m[k] for k in order], *[new_v[k] for k in order])
```

```python
import math

import jax
import jax.numpy as jnp
from jax import lax
from jax.experimental import pallas as pl
from jax.experimental.pallas import tpu as pltpu

F32 = jnp.float32
BF16 = jnp.bfloat16
SDS = jax.ShapeDtypeStruct
MESH = pl.DeviceIdType.MESH

N_DEV = 8
EPS = 1e-6
FFN_RES = 0.5
HEAD_DIM = 64
N_HEADS = 8
D_ATTN = N_HEADS * HEAD_DIM
N_PAIRS = N_HEADS // 2
PAIR = 2 * HEAD_DIM
ATTN_SCALE = 1.0 / math.sqrt(HEAD_DIM)
CONV_A_WIDTH = 31
CONV_A_HALO = 32
CONV_C_WIDTH = 3
CONV_C_HALO = 8
NEG_BIG = -1e30
ADAM_LR, ADAM_B1, ADAM_B2, ADAM_EPS, ADAM_WD, ADAM_STEP = 0.001, 0.9, 0.999, 1e-08, 0.01, 10

V7X_VMEM_BYTES = 64 * 1024 * 1024
V7X_LANES = 128
BF16_ROWS = 16
MIB = 1024 * 1024

NT = (((1,), (1,)), ((), ()))
TN = (((0,), (0,)), ((), ()))


def _call(body, *, name, out_shape, in_specs, out_specs, grid=(), scratch=(), dims=None, vmem_mb=32, **kw):
    params = dict(vmem_limit_bytes=min(vmem_mb * MIB, V7X_VMEM_BYTES - 4 * MIB))
    if dims is not None:
        params["dimension_semantics"] = dims
    call = pl.pallas_call(
        body, name=name, grid=grid, in_specs=in_specs, out_specs=out_specs, out_shape=_in_hbm(out_shape),
        scratch_shapes=list(scratch), compiler_params=pltpu.CompilerParams(**params), **kw)
    return lambda *args: call(*[_keep_in_hbm(a) for a in args])


LARGE_OPERAND_BYTES = MIB


def _is_large(a):
    return a.ndim >= 2 and math.prod(a.shape) * jnp.dtype(a.dtype).itemsize >= LARGE_OPERAND_BYTES


def _keep_in_hbm(a):
    return pltpu.with_memory_space_constraint(a, pltpu.HBM) if _is_large(a) else a


def _in_hbm(out_shape):
    one = lambda s: pltpu.HBM(s.shape, s.dtype) if _is_large(s) else s
    return [one(s) for s in out_shape] if isinstance(out_shape, (list, tuple)) else one(out_shape)


def _tile(n, want=512):
    return want if n % want == 0 else n


def _rows(tm, d, col=0):
    return pl.BlockSpec((tm, d), lambda i: (i, col))


def _const(shape):
    return pl.BlockSpec(shape, lambda *_: (0,) * len(shape))


ANY = pl.BlockSpec(memory_space=pl.ANY)
VMEM = pl.BlockSpec(memory_space=pltpu.VMEM)


def _sigmoid(x):
    return 1.0 / (1.0 + jnp.exp(-x))


def _rmsnorm_fwd(x, gain, name):
    s, d = x.shape
    tm = _tile(s)

    def body(x_ref, g_ref, o_ref):
        xv = x_ref[...]
        r = lax.rsqrt(jnp.mean(xv * xv, axis=-1, keepdims=True) + EPS)
        o_ref[...] = (xv * r * g_ref[...]).astype(BF16)

    return _call(body, name=name, grid=(s // tm,), in_specs=[_rows(tm, d), _const((1, d))],
                 out_specs=_rows(tm, d), out_shape=SDS((s, d), BF16), dims=("parallel",))(x, gain)


def _rmsnorm_bwd(x, gain, dxn, dres, name):
    s, d = x.shape
    tm = _tile(s)

    def body(x_ref, g_ref, dxn_ref, dres_ref, dx_ref, dg_ref):
        xv = x_ref[...]
        r = lax.rsqrt(jnp.mean(xv * xv, axis=-1, keepdims=True) + EPS)
        xh = xv * r
        dv = dxn_ref[...]

        @pl.when(pl.program_id(0) == 0)
        def _():
            dg_ref[...] = jnp.zeros_like(dg_ref)

        dg_ref[...] += jnp.sum(dv * xh, axis=0, keepdims=True)
        dxh = dv * g_ref[...]
        dx_ref[...] = dres_ref[...] + r * (dxh - xh * jnp.mean(dxh * xh, axis=-1, keepdims=True))

    return _call(body, name=name, grid=(s // tm,),
                 in_specs=[_rows(tm, d), _const((1, d)), _rows(tm, d), _rows(tm, d)],
                 out_specs=[_rows(tm, d), _const((1, d))],
                 out_shape=[SDS((s, d), F32), SDS((1, d), F32)], dims=("arbitrary",))(x, gain, dxn, dres)


def _col_tile(n):
    for t in (1280, 1024, 768, 512, 256, 128):
        if n % t == 0:
            return t
    return n


def _mm(a, b, mode, name, out_dtype=F32, add=None):
    if mode == "tn":
        k, m = a.shape
        n = b.shape[1]
        bm = 256 if m % 256 == 0 else m

        def body_tn(a_ref, b_ref, o_ref):
            o_ref[...] = lax.dot_general(a_ref[...].astype(BF16), b_ref[...].astype(BF16), TN,
                                         preferred_element_type=F32).astype(out_dtype)

        return _call(body_tn, name=name, grid=(m // bm,),
                     in_specs=[pl.BlockSpec((k, bm), lambda i: (0, i)), _const((k, n))],
                     out_specs=pl.BlockSpec((bm, n), lambda i: (i, 0)),
                     out_shape=SDS((m, n), out_dtype), dims=("parallel",), vmem_mb=48)(a, b)
    m, k = a.shape
    n = b.shape[0] if mode == "nt" else b.shape[1]
    tm, tn = _tile(m), _col_tile(n)
    dn = NT if mode == "nt" else (((1,), (0,)), ((), ()))

    def body(a_ref, b_ref, *rest):
        o_ref = rest[-1]
        acc = lax.dot_general(a_ref[...].astype(BF16), b_ref[...].astype(BF16), dn, preferred_element_type=F32)
        if add is not None:
            acc = acc + rest[0][...]
        o_ref[...] = acc.astype(out_dtype)

    b_spec = (pl.BlockSpec((tn, k), lambda i, j: (j, 0)) if mode == "nt"
              else pl.BlockSpec((k, tn), lambda i, j: (0, j)))
    in_specs = [pl.BlockSpec((tm, k), lambda i, j: (i, 0)), b_spec]
    args = [a, b]
    if add is not None:
        in_specs.append(pl.BlockSpec((tm, tn), lambda i, j: (i, j)))
        args.append(add)
    return _call(body, name=name, grid=(m // tm, n // tn), in_specs=in_specs,
                 out_specs=pl.BlockSpec((tm, tn), lambda i, j: (i, j)),
                 out_shape=SDS((m, n), out_dtype), dims=("parallel", "parallel"), vmem_mb=48)(*args)


def _mm_tn(a, b, name, out_dtype=F32):
    return _mm(a, b, "tn", name, out_dtype)


FFN_TM = 256
FFN_FWD_TM = 512
FFN_CHUNK = 256


def _load_ffn_weights(w_hbm, offs, fs, dsts, sems):
    copies = []
    for wi, (off, dst) in enumerate(zip(offs, dsts)):
        for j in range(N_DEV):
            cp = pltpu.make_async_copy(w_hbm.at[j, pl.ds(off, fs), :], dst.at[pl.ds(j * fs, fs), :],
                                       sems.at[wi * N_DEV + j])
            cp.start()
            copies.append(cp)
    for cp in copies:
        cp.wait()


def _ffn_fwd(x, gain, wall, offs, fs, name, gather_src=None):
    s, d = x.shape
    f = fs * N_DEV
    tm, ch = _tile(s, FFN_FWD_TM), FFN_CHUNK
    n = s // tm
    gathers = gather_src is not None

    def body(x_ref, gain_ref, w_hbm, *rest):
        if gathers:
            (src_hbm, out_ref, xn_ref, g_ref, u_ref, h_ref, gathered, wg_s, wu_s, wd_s, sems,
             send_sems, recv_sems, local_sem) = rest
            start, forward, finish = _gather_phases(src_hbm, gathered, send_sems, recv_sems, local_sem)
            pl.when(pl.program_id(0) == 0)(start)
            pl.when(pl.program_id(0) == (3 * n) // 4)(forward)
        else:
            out_ref, xn_ref, g_ref, u_ref, h_ref, wg_s, wu_s, wd_s, sems = rest

        @pl.when(pl.program_id(0) == 0)
        def _():
            _load_ffn_weights(w_hbm, offs, fs, (wg_s, wu_s, wd_s), sems)

        xv = x_ref[...]
        xnv = (xv * lax.rsqrt(jnp.mean(xv * xv, axis=-1, keepdims=True) + EPS) * gain_ref[...]).astype(BF16)
        xn_ref[...] = xnv
        acc = jnp.zeros((tm, d), F32)
        for c in range(f // ch):
            sl = slice(c * ch, (c + 1) * ch)
            gb = lax.dot_general(xnv, wg_s[sl, :], NT, preferred_element_type=F32).astype(BF16)
            ub = lax.dot_general(xnv, wu_s[sl, :], NT, preferred_element_type=F32).astype(BF16)
            g_ref[:, sl] = gb
            u_ref[:, sl] = ub
            g = gb.astype(F32)
            hb = (g * _sigmoid(g) * ub.astype(F32)).astype(BF16)
            h_ref[:, sl] = hb
            acc = acc + jnp.dot(hb, wd_s[sl, :], preferred_element_type=F32)
        out_ref[...] = xv + FFN_RES * acc
        if gathers:
            pl.when(pl.program_id(0) == n - 1)(finish)

    in_specs, args = [_rows(tm, d), _const((1, d)), ANY], [x, gain, wall]
    out_specs = [_rows(tm, d), _rows(tm, d), _rows(tm, f), _rows(tm, f), _rows(tm, f)]
    out_shape = [SDS((s, d), F32), SDS((s, d), BF16), SDS((s, f), BF16), SDS((s, f), BF16), SDS((s, f), BF16)]
    scratch = [pltpu.VMEM((f, d), BF16)] * 3 + [pltpu.SemaphoreType.DMA((3 * N_DEV,))]
    if gathers:
        in_specs.append(ANY)
        args.append(gather_src)
        out_specs.append(ANY)
        out_shape.append(SDS((N_DEV,) + gather_src.shape, gather_src.dtype))
        scratch += GATHER_SEMS
    return _call(body, name=name, grid=(n,), in_specs=in_specs, out_specs=out_specs, out_shape=out_shape,
                 scratch=scratch, dims=("arbitrary",), vmem_mb=56)(*args)


def _ffn_bwd_act(dout, g, u, wall, offs, fs, name, exchange_src=None):
    s, d = dout.shape
    f = fs * N_DEV
    tm, ch = _tile(s, FFN_TM), FFN_CHUNK
    n = s // tm
    exchanges = exchange_src is not None
    if exchanges:
        phases_of, shape_of, exchange_sems = EXCHANGES[exchange_src[0]]

    def body(dout_ref, g_ref, u_ref, w_hbm, *rest):
        if exchanges:
            t_hbm, dg_ref, du_ref, dy_ref, dxn_ref, rcv_ref, wg_s, wu_s, wd_s, sems, send_sems, recv_sems = rest
            start, finish = phases_of(t_hbm, rcv_ref, send_sems, recv_sems)
            pl.when(pl.program_id(0) == 0)(start)
        else:
            dg_ref, du_ref, dy_ref, dxn_ref, wg_s, wu_s, wd_s, sems = rest

        @pl.when(pl.program_id(0) == 0)
        def _():
            _load_ffn_weights(w_hbm, offs, fs, (wg_s, wu_s, wd_s), sems)

        dy = (FFN_RES * dout_ref[...]).astype(BF16)
        dy_ref[...] = dy
        acc = jnp.zeros((tm, d), F32)
        for c in range(f // ch):
            sl = slice(c * ch, (c + 1) * ch)
            dh = lax.dot_general(dy, wd_s[sl, :], NT, preferred_element_type=F32)
            gv = g_ref[:, sl].astype(F32)
            uv = u_ref[:, sl].astype(F32)
            sg = _sigmoid(gv)
            dgb = (dh * uv * sg * (1.0 + gv * (1.0 - sg))).astype(BF16)
            dub = (dh * gv * sg).astype(BF16)
            dg_ref[:, sl] = dgb
            du_ref[:, sl] = dub
            acc = acc + jnp.dot(dgb, wg_s[sl, :], preferred_element_type=F32)
            acc = acc + jnp.dot(dub, wu_s[sl, :], preferred_element_type=F32)
        dxn_ref[...] = acc
        if exchanges:
            pl.when(pl.program_id(0) == n - 1)(finish)

    in_specs, args = [_rows(tm, d), _rows(tm, f), _rows(tm, f), ANY], [dout, g, u, wall]
    out_specs = [_rows(tm, f), _rows(tm, f), _rows(tm, d), _rows(tm, d)]
    out_shape = [SDS((s, f), BF16), SDS((s, f), BF16), SDS((s, d), BF16), SDS((s, d), F32)]
    scratch = [pltpu.VMEM((f, d), BF16)] * 3 + [pltpu.SemaphoreType.DMA((3 * N_DEV,))]
    if exchanges:
        in_specs.append(ANY)
        args.append(exchange_src[1])
        out_specs.append(ANY)
        out_shape.append(shape_of(exchange_src[1]))
        scratch += exchange_sems
    return _call(body, name=name, grid=(n,), in_specs=in_specs, out_specs=out_specs, out_shape=out_shape,
                 scratch=scratch, dims=("arbitrary",), vmem_mb=56)(*args)


def _prev_rows(halo, tm, c, col):
    return pl.BlockSpec((halo, c), lambda i: (jnp.maximum(i * (tm // halo) - 1, 0), col))


def _next_rows(halo, tm, c, col, n_blocks):
    return pl.BlockSpec((halo, c), lambda i: (jnp.minimum((i + 1) * (tm // halo), n_blocks - 1), col))


def _conv_a_fwd(z, cw, cb, cn, name):
    s = z.shape[0]
    c = cb.shape[1]
    tm, halo, kw = _tile(s), CONV_A_HALO, CONV_A_WIDTH

    def body(u_ref, g_ref, up_ref, gp_ref, cw_ref, cb_ref, cn_ref, a_ref, a1_ref, buf):
        i = pl.program_id(0)
        buf[0:halo, :] = jnp.where(i > 0, up_ref[...] * _sigmoid(gp_ref[...]), 0.0)
        buf[halo:halo + tm, :] = u_ref[...] * _sigmoid(g_ref[...])
        acc = jnp.zeros((tm, c), F32)
        for k in range(kw):
            acc = acc + cw_ref[k:k + 1, :] * buf[pl.ds(halo - (kw - 1) + k, tm), :]
        a1 = acc + cb_ref[...]
        a1_ref[...] = a1
        a2 = a1 * lax.rsqrt(jnp.mean(a1 * a1, axis=-1, keepdims=True) + EPS) * cn_ref[...]
        a_ref[...] = (a2 * _sigmoid(a2)).astype(BF16)

    return _call(body, name=name, grid=(s // tm,),
                 in_specs=[_rows(tm, c, 0), _rows(tm, c, 1), _prev_rows(halo, tm, c, 0), _prev_rows(halo, tm, c, 1),
                           _const(cw.shape), _const((1, c)), _const((1, c))],
                 out_specs=[_rows(tm, c), _rows(tm, c)],
                 out_shape=[SDS((s, c), BF16), SDS((s, c), F32)],
                 scratch=[pltpu.VMEM((tm + halo, c), F32)], dims=("parallel",))(z, z, z, z, cw, cb, cn)


def _conv_a_bwd_norm(dao, a1, cn, name):
    s, c = a1.shape
    tm = _tile(s)

    def body(da_ref, a1_ref, cn_ref, da1_ref, dcn_ref, dcb_ref):
        a1v = a1_ref[...]
        r = lax.rsqrt(jnp.mean(a1v * a1v, axis=-1, keepdims=True) + EPS)
        xh = a1v * r
        a2 = xh * cn_ref[...]
        sg = _sigmoid(a2)
        da2 = da_ref[...] * sg * (1.0 + a2 * (1.0 - sg))
        dxh = da2 * cn_ref[...]
        da1 = r * (dxh - xh * jnp.mean(dxh * xh, axis=-1, keepdims=True))
        da1_ref[...] = da1

        @pl.when(pl.program_id(0) == 0)
        def _():
            dcn_ref[...] = jnp.zeros_like(dcn_ref)
            dcb_ref[...] = jnp.zeros_like(dcb_ref)

        dcn_ref[...] += jnp.sum(da2 * xh, axis=0, keepdims=True)
        dcb_ref[...] += jnp.sum(da1, axis=0, keepdims=True)

    return _call(body, name=name, grid=(s // tm,),
                 in_specs=[_rows(tm, c, 0), _rows(tm, c), _const((1, c))],
                 out_specs=[_rows(tm, c), _const((1, c)), _const((1, c))],
                 out_shape=[SDS((s, c), F32), SDS((1, c), F32), SDS((1, c), F32)], dims=("arbitrary",))(dao, a1, cn)


def _conv_a_bwd_conv(da1, z, cw, name, exchange_src=None):
    s, c = da1.shape
    tm, halo, kw = _tile(s), CONV_A_HALO, CONV_A_WIDTH
    n = s // tm
    exchanges = exchange_src is not None
    if exchanges:
        phases_of, shape_of, exchange_sems = EXCHANGES[exchange_src[0]]

    def body(d_ref, dn_ref, u_ref, g_ref, up_ref, gp_ref, cw_ref, *rest):
        i = pl.program_id(0)
        if exchanges:
            t_hbm, du_ref, dg_ref, dcw_ref, rcv_ref, buf, bd, send_sems, recv_sems = rest
            start, finish = phases_of(t_hbm, rcv_ref, send_sems, recv_sems)
            pl.when(i == 0)(start)
        else:
            du_ref, dg_ref, dcw_ref, buf, bd = rest
        uv = u_ref[...]
        sg = _sigmoid(g_ref[...])
        buf[0:halo, :] = jnp.where(i > 0, up_ref[...] * _sigmoid(gp_ref[...]), 0.0)
        buf[halo:halo + tm, :] = uv * sg
        dv = d_ref[...]
        bd[0:tm, :] = dv
        bd[tm:tm + halo, :] = jnp.where(i < n - 1, dn_ref[...], 0.0)

        @pl.when(i == 0)
        def _():
            dcw_ref[...] = jnp.zeros_like(dcw_ref)

        da0 = jnp.zeros((tm, c), F32)
        for k in range(kw):
            da0 = da0 + cw_ref[k:k + 1, :] * bd[pl.ds(kw - 1 - k, tm), :]
            dcw_ref[k:k + 1, :] += jnp.sum(dv * buf[pl.ds(halo - (kw - 1) + k, tm), :], axis=0, keepdims=True)
        du_ref[...] = (da0 * sg).astype(BF16)
        dg_ref[...] = (da0 * uv * sg * (1.0 - sg)).astype(BF16)
        if exchanges:
            pl.when(i == n - 1)(finish)

    in_specs = [_rows(tm, c), _next_rows(halo, tm, c, 0, s // halo), _rows(tm, c, 0), _rows(tm, c, 1),
                _prev_rows(halo, tm, c, 0), _prev_rows(halo, tm, c, 1), _const(cw.shape)]
    args = [da1, da1, z, z, z, z, cw]
    out_specs = [_rows(tm, c), _rows(tm, c), _const(cw.shape)]
    out_shape = [SDS((s, c), BF16), SDS((s, c), BF16), SDS(cw.shape, F32)]
    scratch = [pltpu.VMEM((tm + halo, c), F32)] * 2
    if exchanges:
        in_specs.append(ANY)
        args.append(exchange_src[1])
        out_specs.append(ANY)
        out_shape.append(shape_of(exchange_src[1]))
        scratch += exchange_sems
    return _call(body, name=name, grid=(n,), in_specs=in_specs, out_specs=out_specs, out_shape=out_shape,
                 scratch=scratch, dims=("arbitrary",))(*args)


def _lane_is_first_head(tm):
    return lax.broadcasted_iota(jnp.int32, (tm, PAIR), 1) < HEAD_DIM


def _pair_rms(xp, first):
    x2 = xp * xp
    s0 = jnp.sum(jnp.where(first, x2, 0.0), axis=-1, keepdims=True)
    s1 = jnp.sum(jnp.where(first, 0.0, x2), axis=-1, keepdims=True)
    return jnp.where(first, lax.rsqrt(s0 / HEAD_DIM + EPS), lax.rsqrt(s1 / HEAD_DIM + EPS))


def _split3(x):
    hi = x.astype(BF16)
    r1 = x - hi.astype(F32)
    mid = r1.astype(BF16)
    lo = (r1 - mid.astype(F32)).astype(BF16)
    return hi, mid, lo


def _qk_fwd(z, flog, bf, qn2, kn2, name):
    s = z.shape[0]
    tm = _tile(s)
    col0 = (z.shape[1] - 3 * D_ATTN) // D_ATTN

    def body(q_ref, k_ref, v_ref, fl_ref, bf_ref, qn_ref, kn_ref,
             qs_ref, kh_ref, vb_ref, fb_ref, ft_ref, xt_ref, carry):
        i = pl.program_id(0)
        first = _lane_is_first_head(tm)
        for p in range(N_PAIRS):
            sl = slice(p * PAIR, (p + 1) * PAIR)
            q = q_ref[:, sl]
            qs_ref[:, sl] = (q * _pair_rms(q, first) * qn_ref[...] * ATTN_SCALE).astype(BF16)
            k = k_ref[:, sl]
            kh_ref[:, sl] = (k * _pair_rms(k, first) * kn_ref[...]).astype(BF16)
        vb_ref[...] = v_ref[...].astype(BF16)

        xg = fl_ref[...] + bf_ref[...]
        valid = lax.broadcasted_iota(jnp.int32, (tm, V7X_LANES), 1) < N_HEADS
        ls = jnp.where(valid, jnp.minimum(xg, 0.0) - jnp.log(1.0 + jnp.exp(-jnp.abs(xg))), 0.0)
        tri = (lax.broadcasted_iota(jnp.int32, (tm, tm), 1) <= lax.broadcasted_iota(jnp.int32, (tm, tm), 0)).astype(BF16)
        cs = jnp.zeros((tm, V7X_LANES), F32)
        for part in _split3(ls):
            cs = cs + jnp.dot(tri, part, preferred_element_type=F32)

        @pl.when(i == 0)
        def _():
            carry[...] = jnp.zeros_like(carry)

        fv = cs + carry[0:1, :]
        carry[0:1, :] = fv[tm - 1:tm, :]
        ft_ref[...] = fv.T[0:N_HEADS, :]
        xt_ref[...] = xg.T[0:N_HEADS, :]
        for p in range(N_PAIRS):
            fb_ref[:, p * PAIR:(p + 1) * PAIR] = jnp.where(first, fv[:, 2 * p:2 * p + 1], fv[:, 2 * p + 1:2 * p + 2])

    wide = lambda col: pl.BlockSpec((tm, D_ATTN), lambda i: (i, col))
    tcol = pl.BlockSpec((N_HEADS, tm), lambda i: (0, i))
    return _call(body, name=name, grid=(s // tm,),
                 in_specs=[wide(col0), wide(col0 + 1), wide(col0 + 2), _rows(tm, V7X_LANES),
                           _const((1, V7X_LANES)), _const((1, PAIR)), _const((1, PAIR))],
                 out_specs=[wide(0), wide(0), wide(0), wide(0), tcol, tcol],
                 out_shape=[SDS((s, D_ATTN), BF16)] * 3 + [SDS((s, D_ATTN), F32), SDS((N_HEADS, s), F32),
                                                          SDS((N_HEADS, s), F32)],
                 scratch=[pltpu.VMEM((8, V7X_LANES), F32)], dims=("arbitrary",))(z, z, z, flog, bf, qn2, kn2)


ATTN_FWD_SUB = 256
ATTN_BWD_SUB = 512


def _causal_schedule(nq, key_major):
    if key_major:
        pairs = [(i, j) for j in range(nq) for i in range(j, nq)]
    else:
        pairs = [(i, j) for i in range(nq) for j in range(i + 1)]
    return (jnp.asarray([p[0] for p in pairs], jnp.int32), jnp.asarray([p[1] for p in pairs], jnp.int32))


def _sub_scores(qp, kp, ft_row, mine, r, masked, sub, tk):
    qm = jnp.where(mine, qp, jnp.zeros_like(qp))
    s2 = lax.dot_general(qm, kp, NT, preferred_element_type=F32) - ft_row
    if masked:
        row = r * sub + lax.broadcasted_iota(jnp.int32, (sub, tk), 0)
        s2 = jnp.where(lax.broadcasted_iota(jnp.int32, (sub, tk), 1) <= row, s2, NEG_BIG)
    return s2


def _attn_fwd(qs, kh, vb, fb, ft, name, gather_src=None):
    s = qs.shape[0]
    tq = tk = _tile(s)
    nq = s // tq
    sub = min(ATTN_FWD_SUB, tq)
    ii, jj = _causal_schedule(nq, key_major=False)
    n_steps = ii.shape[0]
    gathers = gather_src is not None

    def body(ii_ref, jj_ref, q_ref, k_ref, v_ref, fq_ref, ft_ref, *rest):
        if gathers:
            x_hbm, o_ref, lse_ref, wall_ref, m_s, l_s, acc_s, send_sems, recv_sems, local_sem = rest
        else:
            o_ref, lse_ref, m_s, l_s, acc_s = rest
        p, t = pl.program_id(0), pl.program_id(1)
        i, j = ii_ref[t], jj_ref[t]
        first = _lane_is_first_head(sub)
        if gathers:
            start, forward, finish = _gather_phases(x_hbm, wall_ref, send_sems, recv_sems, local_sem)
            pl.when(jnp.logical_and(p == 0, t == 0))(start)
            pl.when(jnp.logical_and(p == N_PAIRS - 1, t == 0))(forward)

        @pl.when(j == 0)
        def _():
            m_s[...] = jnp.full_like(m_s, NEG_BIG)
            l_s[...] = jnp.zeros_like(l_s)
            acc_s[...] = jnp.zeros_like(acc_s)

        def tile(masked):
            kp, vp = k_ref[...], v_ref[...]
            q_all, fq_all, acc_all = q_ref[...], fq_ref[...], acc_s[...]
            m_all, l_all = (m_s[0], m_s[1]), (l_s[0], l_s[1])
            ft_rows = [ft_ref[pl.ds(2 * p + h, 1), :] for h in range(2)]
            m_out, l_out, acc_out = ([], []), ([], []), []
            for r in range(tq // sub):
                rows = slice(r * sub, (r + 1) * sub)
                qp, fq, acc = q_all[rows, :], fq_all[rows, :], acc_all[rows, :]
                kc = (r + 1) * sub if masked else tk
                new = []
                for h in range(2):
                    mine = first if h == 0 else jnp.logical_not(first)
                    s2 = _sub_scores(qp, kp[:kc, :], ft_rows[h][:, :kc], mine, r, masked, sub, kc)
                    fqh = fq[:, h * HEAD_DIM:h * HEAD_DIM + 1]
                    m_old = m_all[h][rows, :]
                    m_new = jnp.maximum(m_old, jnp.max(s2, axis=-1, keepdims=True) + fqh)
                    pr = jnp.exp(s2 - (m_new - fqh))
                    alpha = jnp.exp(m_old - m_new)
                    l_out[h].append(alpha * l_all[h][rows, :] + jnp.sum(pr, axis=-1, keepdims=True))
                    m_out[h].append(m_new)
                    new.append(alpha * acc + jnp.dot(pr.astype(BF16), vp[:kc, :], preferred_element_type=F32))
                acc_out.append(jnp.where(first, new[0], new[1]))
            for h in range(2):
                m_s[h] = jnp.concatenate(m_out[h], axis=0)
                l_s[h] = jnp.concatenate(l_out[h], axis=0)
            acc_s[...] = jnp.concatenate(acc_out, axis=0)

        @pl.when(j < i)
        def _():
            tile(False)

        @pl.when(j == i)
        def _():
            tile(True)
            whole = _lane_is_first_head(tq)
            l_pair = jnp.where(whole, l_s[0], l_s[1])
            o_ref[...] = acc_s[...] / l_pair
            lse_ref[...] = jnp.where(whole, m_s[0], m_s[1]) + jnp.log(l_pair)

        if gathers:
            pl.when(jnp.logical_and(p == N_PAIRS - 1, t == n_steps - 1))(finish)

    qblk = pl.BlockSpec((tq, PAIR), lambda p, t, ii_r, jj_r: (ii_r[t], p))
    kblk = pl.BlockSpec((tk, PAIR), lambda p, t, ii_r, jj_r: (jj_r[t], p))
    in_specs = [qblk, kblk, kblk, qblk, pl.BlockSpec((N_HEADS, tk), lambda p, t, ii_r, jj_r: (0, jj_r[t]))]
    out_specs, out_shape = [qblk, qblk], [SDS((s, D_ATTN), F32)] * 2
    scratch = [pltpu.VMEM((2, tq, 1), F32), pltpu.VMEM((2, tq, 1), F32), pltpu.VMEM((tq, PAIR), F32)]
    args = [ii, jj, qs, kh, vb, fb, ft]
    if gathers:
        in_specs.append(ANY)
        out_specs.append(ANY)
        out_shape.append(SDS((N_DEV,) + gather_src.shape, gather_src.dtype))
        scratch += GATHER_SEMS
        args.append(gather_src)
    grid_spec = pltpu.PrefetchScalarGridSpec(num_scalar_prefetch=2, grid=(N_PAIRS, n_steps), in_specs=in_specs,
                                             out_specs=out_specs, scratch_shapes=scratch)
    return pl.pallas_call(
        body, name=name, grid_spec=grid_spec, out_shape=_in_hbm(out_shape),
        compiler_params=pltpu.CompilerParams(dimension_semantics=("arbitrary", "arbitrary"),
                                             vmem_limit_bytes=32 * MIB))(*[_keep_in_hbm(a) for a in args])


def _attn_bwd(qs, kh, vb, fb, ft, lse, o, dao, name, exchange_src=None):
    s = qs.shape[0]
    tq = tk = _tile(s)
    nq = s // tq
    sub = min(ATTN_BWD_SUB, tq)
    ii, jj = _causal_schedule(nq, key_major=True)
    n_steps = ii.shape[0]

    exchanges = exchange_src is not None

    def body(ii_ref, jj_ref, q_ref, k_ref, v_ref, fq_ref, ft_ref, lse_ref, o_ref, do_ref, *rest):
        if exchanges:
            t_hbm, dq_ref, rs_ref, dk_ref, dv_ref, df_ref, rcv_ref, dk_s, dv_s, df_s, send_sems, recv_sems = rest
        else:
            dq_ref, rs_ref, dk_ref, dv_ref, df_ref, dk_s, dv_s, df_s = rest
        p, t = pl.program_id(0), pl.program_id(1)
        i, j = ii_ref[t], jj_ref[t]
        first = _lane_is_first_head(sub)
        first_k = _lane_is_first_head(tk)
        if exchanges:
            start, finish = _chip_exchange_phases(t_hbm, rcv_ref, send_sems, recv_sems)
            pl.when(jnp.logical_and(p == 0, t == 0))(start)

        @pl.when(t == 0)
        def _():
            dq_ref[...] = jnp.zeros_like(dq_ref)
            rs_ref[...] = jnp.zeros_like(rs_ref)

        @pl.when(i == j)
        def _():
            dk_s[...] = jnp.zeros_like(dk_s)
            dv_s[...] = jnp.zeros_like(dv_s)
            df_s[...] = jnp.zeros_like(df_s)

        def tile(masked):
            kp, vp = k_ref[...], v_ref[...]
            q_all, fq_all, lse_all, o_all, do_all = q_ref[...], fq_ref[...], lse_ref[...], o_ref[...], do_ref[...]
            ft_rows = [ft_ref[pl.ds(2 * p + h, 1), :] for h in range(2)]
            dq_out, rs_out = [], []
            dk_acc, dv_acc = jnp.zeros((tk, PAIR), F32), jnp.zeros((tk, PAIR), F32)
            df_acc = [jnp.zeros((1, tk), F32), jnp.zeros((1, tk), F32)]
            for r in range(tq // sub):
                rows = slice(r * sub, (r + 1) * sub)
                qp, fq, lse, ov, dall = q_all[rows, :], fq_all[rows, :], lse_all[rows, :], o_all[rows, :], do_all[rows, :]
                dq_h, dk_h, dv_h, rs_h = [], [], [], []
                for h in range(2):
                    mine = first if h == 0 else jnp.logical_not(first)
                    s2 = _sub_scores(qp, kp, ft_rows[h], mine, r, masked, sub, tk)
                    lane = slice(h * HEAD_DIM, h * HEAD_DIM + 1)
                    pr = jnp.exp(s2 - (lse[:, lane] - fq[:, lane]))
                    dov = jnp.where(mine, dall, 0.0)
                    dsum = jnp.sum(dov * ov, axis=-1, keepdims=True)
                    dom = dov.astype(BF16)
                    dom_lo = (dov - dom.astype(F32)).astype(BF16)
                    dp = lax.dot_general(dom, vp, NT, preferred_element_type=F32)
                    dp = dp + lax.dot_general(dom_lo, vp, NT, preferred_element_type=F32)
                    ds = pr * (dp - dsum)
                    dsb = ds.astype(BF16)
                    dq_h.append(jnp.dot(dsb, kp, preferred_element_type=F32))
                    dk_h.append(lax.dot_general(dsb, qp, TN, preferred_element_type=F32))
                    dv_h.append(lax.dot_general(pr.astype(BF16), dom, TN, preferred_element_type=F32))
                    rs_h.append(jnp.sum(ds, axis=-1, keepdims=True))
                    df_acc[h] = df_acc[h] - jnp.sum(ds, axis=0, keepdims=True)
                dq_out.append(jnp.where(first, dq_h[0], dq_h[1]))
                rs_out.append(jnp.where(first, rs_h[0], rs_h[1]))
                dk_acc = dk_acc + jnp.where(first_k, dk_h[0], dk_h[1])
                dv_acc = dv_acc + jnp.where(first_k, dv_h[0], dv_h[1])
            grows = pl.ds(pl.multiple_of(i * tq, tq), tq)
            dq_ref[grows, :] += jnp.concatenate(dq_out, axis=0)
            rs_ref[grows, :] += jnp.concatenate(rs_out, axis=0)
            dk_s[...] += dk_acc
            dv_s[...] += dv_acc
            for h in range(2):
                df_s[h:h + 1, :] += df_acc[h]

        @pl.when(j < i)
        def _():
            tile(False)

        @pl.when(j == i)
        def _():
            tile(True)

        @pl.when(i == nq - 1)
        def _():
            dk_ref[...] = dk_s[...]
            dv_ref[...] = dv_s[...]
            df_ref[0] = df_s[...]

        if exchanges:
            pl.when(jnp.logical_and(p == N_PAIRS - 1, t == n_steps - 1))(finish)

    qblk = pl.BlockSpec((tq, PAIR), lambda p, t, ii_r, jj_r: (ii_r[t], p))
    kblk = pl.BlockSpec((tk, PAIR), lambda p, t, ii_r, jj_r: (jj_r[t], p))
    doblk = pl.BlockSpec((tq, PAIR), lambda p, t, ii_r, jj_r: (ii_r[t], N_PAIRS + p))
    whole = pl.BlockSpec((s, PAIR), lambda p, t, ii_r, jj_r: (0, p))
    in_specs = [qblk, kblk, kblk, qblk, pl.BlockSpec((N_HEADS, tk), lambda p, t, ii_r, jj_r: (0, jj_r[t])),
                qblk, qblk, doblk]
    out_specs = [whole, whole, kblk, kblk, pl.BlockSpec((1, 8, tk), lambda p, t, ii_r, jj_r: (p, 0, jj_r[t]))]
    out_shape = [SDS((s, D_ATTN), F32)] * 4 + [SDS((N_PAIRS, 8, s), F32)]
    scratch = [pltpu.VMEM((tk, PAIR), F32), pltpu.VMEM((tk, PAIR), F32), pltpu.VMEM((8, tk), F32)]
    args = [ii, jj, qs, kh, vb, fb, ft, lse, o, dao]
    if exchanges:
        in_specs.append(ANY)
        out_specs.append(ANY)
        out_shape.append(SDS((3,) + exchange_src.shape[1:], exchange_src.dtype))
        scratch += EXCHANGE_SEMS
        args.append(exchange_src)
    grid_spec = pltpu.PrefetchScalarGridSpec(num_scalar_prefetch=2, grid=(N_PAIRS, n_steps), in_specs=in_specs,
                                             out_specs=out_specs, scratch_shapes=scratch)
    return pl.pallas_call(
        body, name=name, grid_spec=grid_spec, out_shape=_in_hbm(out_shape),
        compiler_params=pltpu.CompilerParams(dimension_semantics=("arbitrary", "arbitrary"),
                                             vmem_limit_bytes=40 * MIB))(*[_keep_in_hbm(a) for a in args])


def _qk_bwd(z, dqs, dkh, dv, qn2, kn2, name):
    s = z.shape[0]
    tm = _tile(s)
    col0 = (z.shape[1] - 3 * D_ATTN) // D_ATTN

    def body(q_ref, k_ref, dqs_ref, dkh_ref, dv_ref, qn_ref, kn_ref, dq_ref, dk_ref, dvb_ref, dqn_ref, dkn_ref):
        first = _lane_is_first_head(tm)

        @pl.when(pl.program_id(0) == 0)
        def _():
            dqn_ref[...] = jnp.zeros_like(dqn_ref)
            dkn_ref[...] = jnp.zeros_like(dkn_ref)

        def through(x_ref, dy_ref, gain_ref, dx_ref, dgain_ref, scale):
            for p in range(N_PAIRS):
                sl = slice(p * PAIR, (p + 1) * PAIR)
                xv = x_ref[:, sl]
                r = _pair_rms(xv, first)
                xh = xv * r
                dy = dy_ref[:, sl] * scale
                dgain_ref[:, sl] += jnp.sum(dy * xh, axis=0, keepdims=True)
                dxh = dy * gain_ref[...]
                t = dxh * xh
                m0 = jnp.sum(jnp.where(first, t, 0.0), axis=-1, keepdims=True)
                m1 = jnp.sum(jnp.where(first, 0.0, t), axis=-1, keepdims=True)
                mean = jnp.where(first, m0, m1) / HEAD_DIM
                dx_ref[:, sl] = (r * (dxh - xh * mean)).astype(BF16)

        through(q_ref, dqs_ref, qn_ref, dq_ref, dqn_ref, ATTN_SCALE)
        through(k_ref, dkh_ref, kn_ref, dk_ref, dkn_ref, 1.0)
        dvb_ref[...] = dv_ref[...].astype(BF16)

    wide = lambda col: pl.BlockSpec((tm, D_ATTN), lambda i: (i, col))
    return _call(body, name=name, grid=(s // tm,),
                 in_specs=[wide(col0), wide(col0 + 1), wide(0), wide(0), wide(0), _const((1, PAIR)), _const((1, PAIR))],
                 out_specs=[wide(0), wide(0), wide(0), _const((1, D_ATTN)), _const((1, D_ATTN))],
                 out_shape=[SDS((s, D_ATTN), BF16)] * 3 + [SDS((1, D_ATTN), F32)] * 2,
                 dims=("arbitrary",))(z, z, dqs, dkh, dv, qn2, kn2)


def _gate_bwd(dft, xt, name):
    s = xt.shape[1]
    tm = _tile(s)
    n = s // tm

    def body(df_ref, xt_ref, dxt_ref, dx_ref, db_ref, carry):
        i = pl.program_id(0)

        @pl.when(i == 0)
        def _():
            carry[...] = jnp.zeros_like(carry)
            db_ref[...] = jnp.zeros_like(db_ref)

        tri = (lax.broadcasted_iota(jnp.int32, (tm, tm), 0) >= lax.broadcasted_iota(jnp.int32, (tm, tm), 1)).astype(BF16)
        rc = jnp.zeros((N_HEADS, tm), F32)
        for part in _split3(df_ref[...]):
            rc = rc + jnp.dot(part, tri, preferred_element_type=F32)
        dls = rc + carry[:, 0:1]
        carry[...] = jnp.broadcast_to(dls[:, 0:1], carry.shape)
        dxt = dls * _sigmoid(-xt_ref[...])
        dxt_ref[...] = dxt
        db_ref[...] += jnp.broadcast_to(jnp.sum(dxt, axis=-1, keepdims=True), db_ref.shape)
        padded = jnp.concatenate([dxt, jnp.zeros((V7X_LANES - N_HEADS, tm), F32)], axis=0)
        dx_ref[...] = padded.T

    rev = pl.BlockSpec((N_HEADS, tm), lambda i: (0, n - 1 - i))
    return _call(body, name=name, grid=(n,), in_specs=[rev, rev],
                 out_specs=[rev, pl.BlockSpec((tm, V7X_LANES), lambda i: (n - 1 - i, 0)), _const((N_HEADS, V7X_LANES))],
                 out_shape=[SDS((N_HEADS, s), F32), SDS((s, V7X_LANES), F32), SDS((N_HEADS, V7X_LANES), F32)],
                 scratch=[pltpu.VMEM((N_HEADS, V7X_LANES), F32)], dims=("arbitrary",))(dft, xt)


def _conv_c_fwd(z, cw, name):
    s = z.shape[0]
    c = z.shape[1] // 3
    tm, halo, kw = _tile(s), CONV_C_HALO, CONV_C_WIDTH

    def body(gb_ref, gc_ref, hh_ref, gcp_ref, hhp_ref, cw_ref, y_ref, buf):
        i = pl.program_id(0)
        buf[0:halo, :] = jnp.where(i > 0, gcp_ref[...] * hhp_ref[...], 0.0)
        buf[halo:halo + tm, :] = gc_ref[...] * hh_ref[...]
        c1 = jnp.zeros((tm, c), F32)
        for k in range(kw):
            c1 = c1 + cw_ref[k:k + 1, :] * buf[pl.ds(halo - (kw - 1) + k, tm), :]
        y_ref[...] = (gb_ref[...] * c1).astype(BF16)

    return _call(body, name=name, grid=(s // tm,),
                 in_specs=[_rows(tm, c, 0), _rows(tm, c, 1), _rows(tm, c, 2), _prev_rows(halo, tm, c, 1),
                           _prev_rows(halo, tm, c, 2), _const(cw.shape)],
                 out_specs=_rows(tm, c), out_shape=SDS((s, c), BF16),
                 scratch=[pltpu.VMEM((tm + halo, c), F32)], dims=("parallel",))(z, z, z, z, z, cw)


def _conv_c_bwd(dy0, z, cw, name):
    s = z.shape[0]
    c = z.shape[1] // 3
    tm, halo, kw = _tile(s), CONV_C_HALO, CONV_C_WIDTH
    n = s // tm

    def body(dy_ref, dyn_ref, gb_ref, gbn_ref, gc_ref, hh_ref, gcp_ref, hhp_ref, cw_ref, dz_ref, dcw_ref, buf, bd):
        i = pl.program_id(0)
        gcv, hhv, dyv = gc_ref[...], hh_ref[...], dy_ref[...]
        buf[0:halo, :] = jnp.where(i > 0, gcp_ref[...] * hhp_ref[...], 0.0)
        buf[halo:halo + tm, :] = gcv * hhv
        dc1 = dyv * gb_ref[...]
        bd[0:tm, :] = dc1
        bd[tm:tm + halo, :] = jnp.where(i < n - 1, dyn_ref[...] * gbn_ref[...], 0.0)

        @pl.when(i == 0)
        def _():
            dcw_ref[...] = jnp.zeros_like(dcw_ref)

        c1 = jnp.zeros((tm, c), F32)
        dc0 = jnp.zeros((tm, c), F32)
        for k in range(kw):
            shifted = buf[pl.ds(halo - (kw - 1) + k, tm), :]
            c1 = c1 + cw_ref[k:k + 1, :] * shifted
            dc0 = dc0 + cw_ref[k:k + 1, :] * bd[pl.ds(kw - 1 - k, tm), :]
            dcw_ref[k:k + 1, :] += jnp.sum(dc1 * shifted, axis=0, keepdims=True)
        dz_ref[:, 0:c] = (dyv * c1).astype(BF16)
        dz_ref[:, c:2 * c] = (dc0 * hhv).astype(BF16)
        dz_ref[:, 2 * c:3 * c] = (dc0 * gcv).astype(BF16)

    return _call(body, name=name, grid=(n,),
                 in_specs=[_rows(tm, c), _next_rows(halo, tm, c, 0, s // halo), _rows(tm, c, 0),
                           _next_rows(halo, tm, c, 0, s // halo), _rows(tm, c, 1), _rows(tm, c, 2),
                           _prev_rows(halo, tm, c, 1), _prev_rows(halo, tm, c, 2), _const(cw.shape)],
                 out_specs=[_rows(tm, 3 * c), _const(cw.shape)],
                 out_shape=[SDS((s, 3 * c), BF16), SDS(cw.shape, F32)],
                 scratch=[pltpu.VMEM((tm + halo, c), F32)] * 2, dims=("arbitrary",),
                 vmem_mb=48)(dy0, dy0, z, z, z, z, z, z, cw)


def _loss_head(y, target, name):
    s, d = y.shape
    tm = _tile(s)

    def body(y_ref, t_ref, loss_ref, dy_ref):
        e = y_ref[...] - t_ref[...]

        @pl.when(pl.program_id(0) == 0)
        def _():
            loss_ref[...] = jnp.zeros_like(loss_ref)

        loss_ref[...] += 0.5 * jnp.sum(jnp.mean(e * e, axis=-1, keepdims=True))
        dy_ref[...] = e / d

    return _call(body, name=name, grid=(s // tm,), in_specs=[_rows(tm, d), _rows(tm, d)],
                 out_specs=[_const((8, V7X_LANES)), _rows(tm, d)],
                 out_shape=[SDS((8, V7X_LANES), F32), SDS((s, d), F32)], dims=("arbitrary",))(y, target)


def _adamw(w, g, m, v, name):
    r, c = w.shape
    tr = next((t for t in (512, 256, 128, 64, 32, 16, 8) if r % t == 0), r)

    def body(w_ref, g_ref, m_ref, v_ref, d_ref, mo_ref, vo_ref):
        gv = g_ref[...]
        mn = ADAM_B1 * m_ref[...] + (1.0 - ADAM_B1) * gv
        vn = ADAM_B2 * v_ref[...] + (1.0 - ADAM_B2) * (gv * gv)
        m_hat = mn / (1.0 - ADAM_B1 ** ADAM_STEP)
        v_hat = vn / (1.0 - ADAM_B2 ** ADAM_STEP)
        d_ref[...] = -ADAM_LR * (m_hat / (jnp.sqrt(v_hat) + ADAM_EPS) + ADAM_WD * w_ref[...])
        mo_ref[...] = mn
        vo_ref[...] = vn

    spec = _rows(tr, c)
    return _call(body, name=name, grid=(r // tr,), in_specs=[spec] * 4, out_specs=[spec] * 3,
                 out_shape=[SDS((r, c), F32)] * 3, dims=("parallel",))(w, g, m, v)


def _position():
    return lax.axis_index("x"), lax.axis_index("y"), lax.axis_index("c")


def _other_chips(x, y):
    return [(1 - x, y), (x, 1 - y), (1 - x, 1 - y)]


def _dev_index(px, py, pc):
    return 4 * px + 2 * py + pc


def _all_gather(wloc):
    r, d = wloc.shape

    def body(x_ref, out_ref, send_sems, recv_sems, local_sem):
        start, forward, finish = _gather_phases(x_ref, out_ref, send_sems, recv_sems, local_sem)
        start()
        forward()
        finish()

    return _call(body, name="all_gather_weights", in_specs=[ANY], out_specs=ANY,
                 out_shape=SDS((N_DEV, r, d), wloc.dtype), scratch=GATHER_SEMS)(wloc)


GATHER_SEMS = [pltpu.SemaphoreType.DMA((7,)), pltpu.SemaphoreType.DMA((7,)), pltpu.SemaphoreType.DMA((1,))]


def _gather_phases(x_ref, out_ref, send_sems, recv_sems, local_sem):
    x, y, c = _position()
    me, sibling = (x, y, c), (x, y, 1 - c)
    chips = _other_chips(x, y)

    def slot(dev):
        return out_ref.at[_dev_index(*dev)]

    def copy(k, block, to, src=None):
        return pltpu.make_async_remote_copy(
            src_ref=slot(block) if src is None else src, dst_ref=slot(block),
            send_sem=send_sems.at[k], recv_sem=recv_sems.at[k], device_id=to, device_id_type=MESH)

    mine = pltpu.make_async_copy(x_ref, slot(me), local_sem.at[0])
    first = [copy(0, me, sibling, src=x_ref)] + [copy(1 + j, me, (*chip, c), src=x_ref) for j, chip in enumerate(chips)]
    passed = [copy(4 + j, (*chip, c), sibling) for j, chip in enumerate(chips)]

    def start():
        mine.start()
        for cp in first:
            cp.start()

    def forward():
        for j, chip in enumerate(chips):
            copy(1 + j, (*chip, c), me).wait_recv()
            passed[j].start()

    def finish():
        copy(0, sibling, me).wait_recv()
        for j, chip in enumerate(chips):
            copy(4 + j, (*chip, 1 - c), me).wait_recv()
        for cp in first + passed:
            cp.wait_send()
        mine.wait()

    return start, forward, finish


def _row_block(r):
    return next(t for t in range(704, 0, -BF16_ROWS) if r % t == 0)


def _pair_exchange(gall, name):
    def body(g_ref, out_ref, send_sems, recv_sems):
        start, finish = _pair_exchange_phases(g_ref, out_ref, send_sems, recv_sems)
        start()
        finish()

    return _call(body, name=name, in_specs=[ANY], out_specs=ANY, out_shape=_pair_exchange_shape(gall),
                 scratch=PAIR_EXCHANGE_SEMS)(gall)


PAIR_EXCHANGE_SEMS = [pltpu.SemaphoreType.DMA((4,)), pltpu.SemaphoreType.DMA((4,))]


def _pair_exchange_shape(gall):
    return SDS((4,) + gall.shape[1:], gall.dtype)


def _pair_exchange_phases(g_ref, out_ref, send_sems, recv_sems):
    x, y, c = _position()
    sibling = (x, y, 1 - c)
    dests = [sibling] + [(*chip, 1 - c) for chip in _other_chips(x, y)]
    copies = [pltpu.make_async_remote_copy(
        src_ref=g_ref.at[_dev_index(*dest)], dst_ref=out_ref.at[k], send_sem=send_sems.at[k],
        recv_sem=recv_sems.at[k], device_id=sibling, device_id_type=MESH) for k, dest in enumerate(dests)]

    def start():
        for cp in copies:
            cp.start()

    def finish():
        for cp in copies:
            cp.wait()

    return start, finish


def _pair_sum(gall, sib, idx, name):
    _, r, d = gall.shape
    tr = _row_block(r)

    def body(idx_ref, a_ref, b_ref, o_ref):
        o_ref[...] = (a_ref[...].astype(F32) + b_ref[...].astype(F32)).astype(o_ref.dtype)

    grid_spec = pltpu.PrefetchScalarGridSpec(
        num_scalar_prefetch=1, grid=(4, r // tr),
        in_specs=[pl.BlockSpec((1, tr, d), lambda k, i, idx_ref: (idx_ref[k], i, 0)),
                  pl.BlockSpec((1, tr, d), lambda k, i, idx_ref: (k, i, 0))],
        out_specs=pl.BlockSpec((1, tr, d), lambda k, i, idx_ref: (k, i, 0)))
    return pl.pallas_call(body, name=name, grid_spec=grid_spec,
                          out_shape=_in_hbm(SDS((4, r, d), gall.dtype)),
                          compiler_params=pltpu.CompilerParams(dimension_semantics=("parallel", "parallel")))(
        idx, _keep_in_hbm(gall), _keep_in_hbm(sib))


def _chip_exchange(tsum):
    _, r, d = tsum.shape

    def body(t_ref, out_ref, send_sems, recv_sems):
        start, finish = _chip_exchange_phases(t_ref, out_ref, send_sems, recv_sems)
        start()
        finish()

    return _call(body, name="reduce_scatter_chip_exchange", in_specs=[ANY], out_specs=ANY,
                 out_shape=SDS((3, r, d), tsum.dtype), scratch=EXCHANGE_SEMS)(tsum)


EXCHANGE_SEMS = [pltpu.SemaphoreType.DMA((3,)), pltpu.SemaphoreType.DMA((3,))]


def _chip_exchange_phases(t_ref, out_ref, send_sems, recv_sems):
    x, y, c = _position()
    copies = [pltpu.make_async_remote_copy(
        src_ref=t_ref.at[1 + k], dst_ref=out_ref.at[k], send_sem=send_sems.at[k], recv_sem=recv_sems.at[k],
        device_id=(*chip, c), device_id_type=MESH) for k, chip in enumerate(_other_chips(x, y))]

    def start():
        for cp in copies:
            cp.start()

    def finish():
        for cp in copies:
            cp.wait()

    return start, finish


def _chip_exchange_shape(tsum):
    return SDS((3,) + tsum.shape[1:], tsum.dtype)


EXCHANGES = dict(pair=(_pair_exchange_phases, _pair_exchange_shape, PAIR_EXCHANGE_SEMS),
                 chip=(_chip_exchange_phases, _chip_exchange_shape, EXCHANGE_SEMS))


def _final_sum(tsum, rcv, name):
    _, r, d = tsum.shape
    tr = _row_block(r)

    def body(t_ref, r_ref, o_ref):
        acc = t_ref[0].astype(F32)
        for k in range(3):
            acc = acc + r_ref[k].astype(F32)
        o_ref[...] = acc

    return _call(body, name=name, grid=(r // tr,),
                 in_specs=[pl.BlockSpec((1, tr, d), lambda i: (0, i, 0)), pl.BlockSpec((3, tr, d), lambda i: (0, i, 0))],
                 out_specs=_rows(tr, d), out_shape=SDS((r, d), F32), dims=("parallel",))(tsum, rcv)


def _all_reduce_small(buf):
    nr, lanes = buf.shape

    def body(b_ref, out_ref, gath, send_sems, recv_sems):
        x, y, c = _position()
        my_slot = _dev_index(x, y, c)
        gath[my_slot] = b_ref[...]
        copies = []
        for k in range(1, N_DEV):
            dx, dy, dc = (k >> 2) & 1, (k >> 1) & 1, k & 1
            peer = (1 - x if dx else x, 1 - y if dy else y, 1 - c if dc else c)
            copies.append(pltpu.make_async_remote_copy(
                src_ref=b_ref, dst_ref=gath.at[my_slot], send_sem=send_sems.at[k - 1], recv_sem=recv_sems.at[k - 1],
                device_id=peer, device_id_type=MESH))
        for cp in copies:
            cp.start()
        for cp in copies:
            cp.wait()
        acc = gath[0]
        for sidx in range(1, N_DEV):
            acc = acc + gath[sidx]
        out_ref[...] = acc

    return _call(body, name="all_reduce_small", in_specs=[VMEM], out_specs=VMEM, out_shape=SDS((nr, lanes), F32),
                 scratch=[pltpu.VMEM((N_DEV, nr, lanes), F32), pltpu.SemaphoreType.DMA((7,)),
                          pltpu.SemaphoreType.DMA((7,))])(buf)


def _ffn_block_fwd(x, gain, wall, offs, fs, tag, gather_src=None):
    res = _ffn_fwd(x, gain, wall, offs, fs, f"{tag}_fwd", gather_src)
    out, xn, g, u, h = res[:5]
    return out, (x, gain, xn, g, u, h), (res[5] if gather_src is not None else None)


def _ffn_block_bwd(dout, saved, wall, offs, fs, tag, exchange_src=None):
    x, gain, xn, g, u, h = saved
    res = _ffn_bwd_act(dout, g, u, wall, offs, fs, f"{tag}_bwd_act", exchange_src)
    dg, du, dy_b, dxn = res[:4]
    dwg = _mm_tn(dg, xn, f"{tag}_dwg", BF16)
    dwu = _mm_tn(du, xn, f"{tag}_dwu", BF16)
    dwd = _mm_tn(h, dy_b, f"{tag}_dwd", BF16)
    dx, dgain = _rmsnorm_bwd(x, gain, dxn, dout, f"{tag}_norm_bwd")
    return dx, (dwg, dwu, dwd), dgain, (res[4] if exchange_src is not None else None)


def _local_step(x, target, wall_a, fs, small, plan):
    grads = {}
    first, second = (0, fs, 2 * fs), (3 * fs, 4 * fs, 5 * fs)
    reduces = "pair_sum" in plan

    x1, s_f1a, wall_b = _ffn_block_fwd(x, small["ffn1_norm"][0], wall_a, first, fs, "l0_ffn1", plan.get("shard_b"))
    wall_b = plan.get("wall_b", wall_b)
    mixw = plan["mix_b"](wall_b)
    hn0 = _rmsnorm_fwd(x1, small["mix_norm"][0], "l0_mix_norm")
    z = _mm(hn0, mixw["ev_w_main_t"], "nt", "ev_in_proj")
    flog = _mm(hn0, mixw["ev_w_f_t"], "nt", "ev_in_proj_gate")
    a, a1 = _conv_a_fwd(z, small["ev_conv_w32"], small["ev_conv_b"], small["ev_conv_norm"], "ev_conv_fwd")
    qs, kh, vb, fb, ft, xt = _qk_fwd(z, flog, small["ev_b_f128"], small["ev_q_norm2"], small["ev_k_norm2"], "ev_qk_fwd")
    if "wall_c" in plan:
        o, lse = _attn_fwd(qs, kh, vb, fb, ft, "ev_attn_fwd")
        wall_c = plan["wall_c"]
    else:
        o, lse, wall_c = _attn_fwd(qs, kh, vb, fb, ft, "ev_attn_fwd", gather_src=plan["shard_c"])
    mixw = {**mixw, **plan["mix_c"](wall_c)}
    ao = jnp.concatenate([a, o.astype(BF16)], axis=1)
    x2 = _mm(ao, mixw["ev_w_out"], "nn", "ev_out_proj", add=x1)
    x3, s_f2a, _ = _ffn_block_fwd(x2, small["ffn2_norm"][0], wall_c, first, fs, "l0_ffn2")

    x4, s_f1b, wall_d = _ffn_block_fwd(x3, small["ffn1_norm"][1], wall_c, second, fs, "l1_ffn1", plan.get("shard_d"))
    wall_d = plan.get("wall_d", wall_d)
    hn1 = _rmsnorm_fwd(x4, small["mix_norm"][1], "l1_mix_norm")
    zo = _mm(hn1, mixw["od_w_in_t"], "nt", "od_in_proj")
    y0 = _conv_c_fwd(zo, small["od_conv_w8"], "od_conv_fwd")
    x5 = _mm(y0, mixw["od_w_out"], "nn", "od_out_proj", add=x4)
    x6, s_f2b, _ = _ffn_block_fwd(x5, small["ffn2_norm"][1], wall_d, first, fs, "l1_ffn2")

    loss, d6 = _loss_head(x6, target, "loss_head")

    d5, grads["l1_ffn2"], grads["ffn2_norm_1"], _ = _ffn_block_bwd(d6, s_f2b, wall_d, first, fs, "l1_ffn2")
    d5b = d5.astype(BF16)
    dy0 = _mm(d5b, mixw["od_w_out"], "nt", "od_out_proj_bwd")
    grads["od_w_out"] = _mm_tn(y0, d5b, "od_dw_out", BF16)
    dzo, grads["od_conv_w"] = _conv_c_bwd(dy0, zo, small["od_conv_w8"], "od_conv_bwd")
    dh1 = _mm(dzo, mixw["od_w_in_t"], "nn", "od_in_proj_bwd")
    grads["od_w_in_t"] = _mm_tn(dzo, hn1, "od_dw_in", BF16)
    d4, grads["mix_norm_1"] = _rmsnorm_bwd(x4, small["mix_norm"][1], dh1, d5, "l1_mix_norm_bwd")
    d3, grads["l1_ffn1"], grads["ffn1_norm_1"], _ = _ffn_block_bwd(d4, s_f1b, wall_c, second, fs, "l1_ffn1")

    d2, grads["l0_ffn2"], grads["ffn2_norm_0"], _ = _ffn_block_bwd(d3, s_f2a, wall_c, first, fs, "l0_ffn2")
    partials_c = plan["partials_c"](grads) if reduces else None
    d2b = d2.astype(BF16)
    dao = _mm(d2b, mixw["ev_w_out"], "nt", "ev_out_proj_bwd")
    grads["ev_w_out"] = _mm_tn(ao, d2b, "ev_dw_out", BF16)
    da1, grads["ev_conv_norm"], grads["ev_conv_b"] = _conv_a_bwd_norm(dao, a1, small["ev_conv_norm"], "ev_conv_bwd_norm")
    res = _conv_a_bwd_conv(da1, z, small["ev_conv_w32"], "ev_conv_bwd_conv",
                           exchange_src=("pair", partials_c) if reduces else None)
    du, dg, grads["ev_conv_w"] = res[:3]
    sums_c = plan["pair_sum"](partials_c, res[3], "c") if reduces else None
    res = _attn_bwd(qs, kh, vb, fb, ft, lse, o, dao, "ev_attn_bwd", exchange_src=sums_c)
    dqs, rs, dkh, dv, df4 = res[:5]
    if reduces:
        grads["pair_sums_c"], grads["exchanged_c"] = sums_c, res[5]
    dq, dk, dvb, grads["ev_q_norm"], grads["ev_k_norm"] = _qk_bwd(
        z, dqs, dkh, dv, small["ev_q_norm2"], small["ev_k_norm2"], "ev_qk_bwd")
    dft = df4[:, 0:2, :].reshape(N_HEADS, -1) + rs.reshape(-1, N_HEADS, HEAD_DIM)[:, :, 0].T
    dxt, dflog, grads["ev_b_f"] = _gate_bwd(dft, xt, "ev_gate_bwd")
    dz = jnp.concatenate([du, dg, dq, dk, dvb], axis=1)
    dflog_b = dflog.astype(BF16)
    dh0 = _mm(dz, mixw["ev_w_main_t"], "nn", "ev_in_proj_bwd")
    dh0 = _mm(dflog_b, mixw["ev_w_f_t"], "nn", "ev_in_proj_gate_bwd", add=dh0)
    dw_main = _mm_tn(dz, hn0, "ev_dw_in", BF16)
    dw_f = _mm(dxt.astype(BF16), hn0, "nn", "ev_dw_in_gate", BF16)
    grads["ev_w_in_t"] = jnp.concatenate([dw_main, dw_f], axis=0)
    d1, grads["mix_norm_0"] = _rmsnorm_bwd(x1, small["mix_norm"][0], dh0, d2, "l0_mix_norm_bwd")
    sums_b = None
    if reduces:
        partials_b = plan["partials_b"](grads)
        sums_b = plan["pair_sum"](partials_b, _pair_exchange(partials_b, "reduce_scatter_pair_exchange_b"), "b")
    d0, grads["l0_ffn1"], grads["ffn1_norm_0"], exchanged_b = _ffn_block_bwd(
        d1, s_f1a, wall_a, first, fs, "l0_ffn1", exchange_src=("chip", sums_b) if reduces else None)
    if reduces:
        grads["pair_sums_b"], grads["exchanged_b"] = sums_b, exchanged_b
    return loss, d0, grads


def _round_up(n, m):
    return -(-n // m) * m


def _pad_rows(a, rows):
    return jnp.pad(a, ((0, rows - a.shape[0]), (0, 0)))


SMALL_ORDER = ("loss", "ffn1_norm", "mix_norm", "ffn2_norm", "ev_b_f", "ev_conv_b", "ev_conv_norm",
               "ev_q_norm", "ev_k_norm", "ev_conv_w", "od_conv_w")


def _pack_small(parts):
    flat = jnp.concatenate([parts[k].reshape(-1).astype(F32) for k in SMALL_ORDER])
    n = _round_up(flat.shape[0], 8 * V7X_LANES)
    return jnp.pad(flat, (0, n - flat.shape[0])).reshape(-1, V7X_LANES)


def _unpack_small(buf, shapes):
    flat = buf.reshape(-1)
    out, pos = {}, 0
    for k in SMALL_ORDER:
        n = math.prod(shapes[k])
        out[k] = flat[pos:pos + n].reshape(shapes[k])
        pos += n
    return out


def kernel(x, ffn1_norm, ffn1_w_gate, ffn1_w_up, ffn1_w_down, mix_norm, ffn2_norm, ffn2_w_gate, ffn2_w_up, ffn2_w_down, ev_w_in, ev_b_f, ev_conv_w, ev_conv_b, ev_conv_norm, ev_q_norm, ev_k_norm, ev_w_out, od_w_in, od_conv_w, od_w_out, loss_target, m_ffn1_norm, m_ffn1_w_gate, m_ffn1_w_up, m_ffn1_w_down, m_mix_norm, m_ffn2_norm, m_ffn2_w_gate, m_ffn2_w_up, m_ffn2_w_down, m_ev_w_in, m_ev_b_f, m_ev_conv_w, m_ev_conv_b, m_ev_conv_norm, m_ev_q_norm, m_ev_k_norm, m_ev_w_out, m_od_w_in, m_od_conv_w, m_od_w_out, v_ffn1_norm, v_ffn1_w_gate, v_ffn1_w_up, v_ffn1_w_down, v_mix_norm, v_ffn2_norm, v_ffn2_w_gate, v_ffn2_w_up, v_ffn2_w_down, v_ev_w_in, v_ev_b_f, v_ev_conv_w, v_ev_conv_b, v_ev_conv_norm, v_ev_q_norm, v_ev_k_norm, v_ev_w_out, v_od_w_in, v_od_conv_w, v_od_w_out):
    weights = dict(ffn1_norm=ffn1_norm, ffn1_w_gate=ffn1_w_gate, ffn1_w_up=ffn1_w_up, ffn1_w_down=ffn1_w_down,
                   mix_norm=mix_norm, ffn2_norm=ffn2_norm, ffn2_w_gate=ffn2_w_gate, ffn2_w_up=ffn2_w_up,
                   ffn2_w_down=ffn2_w_down, ev_w_in=ev_w_in, ev_b_f=ev_b_f, ev_conv_w=ev_conv_w, ev_conv_b=ev_conv_b,
                   ev_conv_norm=ev_conv_norm, ev_q_norm=ev_q_norm, ev_k_norm=ev_k_norm, ev_w_out=ev_w_out,
                   od_w_in=od_w_in, od_conv_w=od_conv_w, od_w_out=od_w_out)
    m_in = dict(ffn1_norm=m_ffn1_norm, ffn1_w_gate=m_ffn1_w_gate, ffn1_w_up=m_ffn1_w_up, ffn1_w_down=m_ffn1_w_down,
                mix_norm=m_mix_norm, ffn2_norm=m_ffn2_norm, ffn2_w_gate=m_ffn2_w_gate, ffn2_w_up=m_ffn2_w_up,
                ffn2_w_down=m_ffn2_w_down, ev_w_in=m_ev_w_in, ev_b_f=m_ev_b_f, ev_conv_w=m_ev_conv_w,
                ev_conv_b=m_ev_conv_b, ev_conv_norm=m_ev_conv_norm, ev_q_norm=m_ev_q_norm, ev_k_norm=m_ev_k_norm,
                ev_w_out=m_ev_w_out, od_w_in=m_od_w_in, od_conv_w=m_od_conv_w, od_w_out=m_od_w_out)
    v_in = dict(ffn1_norm=v_ffn1_norm, ffn1_w_gate=v_ffn1_w_gate, ffn1_w_up=v_ffn1_w_up, ffn1_w_down=v_ffn1_w_down,
                mix_norm=v_mix_norm, ffn2_norm=v_ffn2_norm, ffn2_w_gate=v_ffn2_w_gate, ffn2_w_up=v_ffn2_w_up,
                ffn2_w_down=v_ffn2_w_down, ev_w_in=v_ev_w_in, ev_b_f=v_ev_b_f, ev_conv_w=v_ev_conv_w,
                ev_conv_b=v_ev_conv_b, ev_conv_norm=v_ev_conv_norm, ev_q_norm=v_ev_q_norm, ev_k_norm=v_ev_k_norm,
                ev_w_out=v_ev_w_out, od_w_in=v_od_w_in, od_conv_w=v_od_conv_w, od_w_out=v_od_w_out)
    order = list(weights)

    d = x.shape[-1]
    fs = ffn1_w_gate.shape[2]
    n_in = ev_w_in.shape[2]
    n_in_pad = _round_up(n_in, BF16_ROWS)
    n_out = ev_w_out.shape[1]
    n_od = od_w_in.shape[2]
    d_conv = ev_conv_b.shape[1]
    d_in_even = n_in * N_DEV
    d_main = d_in_even - N_HEADS
    cx, cy, cc = _position()
    me = _dev_index(cx, cy, cc)

    def block(wg, wu, wd, layer):
        return [wg[layer].T, wu[layer].T, wd[layer]]

    def stack(parts):
        return jnp.concatenate([p.astype(BF16) for p in parts], axis=0)

    ffn1, ffn2 = (ffn1_w_gate, ffn1_w_up, ffn1_w_down), (ffn2_w_gate, ffn2_w_up, ffn2_w_down)
    shard_a = stack(block(*ffn1, 0))
    shard_b = stack([_pad_rows(ev_w_in[0].T, n_in_pad), ev_w_out[0]])
    shard_c = stack(block(*ffn2, 0) + block(*ffn1, 1) + [od_w_in[0].T, od_w_out[0]])
    shard_d = stack(block(*ffn2, 1))
    off_ev_in, off_ev_out = 0, n_in_pad
    off_od_in, off_od_out = 6 * fs, 6 * fs + n_od
    wall_a = _all_gather(shard_a)

    def even_mixer_weights(wall_b):
        ev_w_in_t = wall_b[:, off_ev_in:off_ev_in + n_in, :].reshape(d_in_even, d)
        return dict(ev_w_main_t=ev_w_in_t[:d_main], ev_w_f_t=_pad_rows(ev_w_in_t[d_main:], V7X_LANES),
                    ev_w_out=wall_b[:, off_ev_out:off_ev_out + n_out, :].reshape(N_DEV * n_out, d))

    def odd_mixer_weights(wall_c):
        return dict(od_w_in_t=wall_c[:, off_od_in:off_od_in + n_od, :].reshape(N_DEV * n_od, d),
                    od_w_out=wall_c[:, off_od_out:off_od_out + n_out, :].reshape(N_DEV * n_out, d))

    def by_dev(a, rows, pad_to=None):
        a = a.reshape(N_DEV, rows, d)
        return a if pad_to is None else jnp.pad(a, ((0, 0), (0, pad_to - rows), (0, 0)))

    idx = jnp.stack([me] + [_dev_index(*chip, cc) for chip in _other_chips(cx, cy)]).astype(jnp.int32)

    def pair_sum(partials, from_sibling, tag):
        return _pair_sum(partials, from_sibling, idx, f"reduce_scatter_pair_sum_{tag}")

    def ffn_pieces(g, key):
        return [by_dev(t, fs) for t in g[key]]

    conv_shapes = dict(ev_conv_w=(CONV_A_WIDTH, d_conv), od_conv_w=(CONV_C_WIDTH, d))
    zero_small = {k: jnp.zeros(s_, F32) for k, s_ in conv_shapes.items()}
    ev_cw_part = lax.dynamic_update_slice(zero_small["ev_conv_w"], ev_conv_w[0], (0, me * ev_conv_w.shape[2]))
    od_cw_part = lax.dynamic_update_slice(zero_small["od_conv_w"], od_conv_w[0], (0, me * od_conv_w.shape[2]))
    zeros_like_small = {k: jnp.zeros((1,), F32) for k in SMALL_ORDER}
    taps = _unpack_small(_all_reduce_small(_pack_small({**zeros_like_small, "ev_conv_w": ev_cw_part,
                                                        "od_conv_w": od_cw_part})),
                         {**{k: (1,) for k in SMALL_ORDER}, **conv_shapes})
    small = dict(
        ffn1_norm=[ffn1_norm[l][None] for l in range(2)], mix_norm=[mix_norm[l][None] for l in range(2)],
        ffn2_norm=[ffn2_norm[l][None] for l in range(2)],
        ev_conv_w32=_pad_rows(taps["ev_conv_w"], CONV_A_WIDTH + 1), ev_conv_b=ev_conv_b, ev_conv_norm=ev_conv_norm,
        ev_b_f128=jnp.pad(ev_b_f, ((0, 0), (0, V7X_LANES - N_HEADS))),
        ev_q_norm2=jnp.tile(ev_q_norm, (1, 2)), ev_k_norm2=jnp.tile(ev_k_norm, (1, 2)),
        od_conv_w8=_pad_rows(taps["od_conv_w"], 8),
    )

    plan = dict(
        shard_b=shard_b, shard_c=shard_c, shard_d=shard_d, mix_b=even_mixer_weights, mix_c=odd_mixer_weights, pair_sum=pair_sum,
        partials_c=lambda g1: jnp.concatenate(
            ffn_pieces(g1, "l0_ffn2") + ffn_pieces(g1, "l1_ffn1") + ffn_pieces(g1, "l1_ffn2")
            + [by_dev(g1["od_w_in_t"], n_od), by_dev(g1["od_w_out"], n_out)], axis=1),
        partials_b=lambda g1: jnp.concatenate(
            [by_dev(g1["ev_w_in_t"], n_in, n_in_pad), by_dev(g1["ev_w_out"], n_out)], axis=1))
    loss_p, grad_x, g = _local_step(x[0], loss_target[0], wall_a, fs, small, plan)

    partials_a = jnp.concatenate(ffn_pieces(g, "l0_ffn1"), axis=1)
    sums_a = pair_sum(partials_a, _pair_exchange(partials_a, "reduce_scatter_pair_exchange_a"), "a")
    gsum_a = _final_sum(sums_a, _chip_exchange(sums_a), "reduce_scatter_final_sum_a")
    gsum_b = _final_sum(g["pair_sums_b"], g["exchanged_b"], "reduce_scatter_final_sum_b")
    gsum_c = _final_sum(g["pair_sums_c"], g["exchanged_c"], "reduce_scatter_final_sum_c")

    grad = {}
    where = dict(ffn1=((gsum_a, 0), (gsum_c, 3 * fs)), ffn2=((gsum_c, 0), (gsum_c, 6 * fs)))
    for blk, places in where.items():
        for wi, kind in enumerate(("gate", "up", "down")):
            rows = [buf[off + wi * fs:off + (wi + 1) * fs] for buf, off in places]
            grad[f"{blk}_w_{kind}"] = jnp.stack(rows if kind == "down" else [r.T for r in rows])
    grad["ev_w_in"] = gsum_b[off_ev_in:off_ev_in + n_in].T[None]
    grad["ev_w_out"] = gsum_b[off_ev_out:off_ev_out + n_out][None]
    grad["od_w_in"] = gsum_c[9 * fs:9 * fs + n_od].T[None]
    grad["od_w_out"] = gsum_c[9 * fs + n_od:9 * fs + n_od + n_out][None]

    heads = lambda t: t.reshape(N_HEADS, HEAD_DIM).sum(axis=0)
    parts = dict(
        loss=loss_p[0, 0:1],
        ffn1_norm=jnp.stack([g["ffn1_norm_0"][0], g["ffn1_norm_1"][0]]),
        mix_norm=jnp.stack([g["mix_norm_0"][0], g["mix_norm_1"][0]]),
        ffn2_norm=jnp.stack([g["ffn2_norm_0"][0], g["ffn2_norm_1"][0]]),
        ev_b_f=g["ev_b_f"][:, 0], ev_conv_b=g["ev_conv_b"], ev_conv_norm=g["ev_conv_norm"],
        ev_q_norm=heads(g["ev_q_norm"]), ev_k_norm=heads(g["ev_k_norm"]),
        ev_conv_w=g["ev_conv_w"][:CONV_A_WIDTH], od_conv_w=g["od_conv_w"][:CONV_C_WIDTH])
    small_shapes = dict(loss=(1,), ffn1_norm=ffn1_norm.shape, mix_norm=mix_norm.shape, ffn2_norm=ffn2_norm.shape,
                        ev_b_f=ev_b_f.shape, ev_conv_b=ev_conv_b.shape, ev_conv_norm=ev_conv_norm.shape,
                        ev_q_norm=ev_q_norm.shape, ev_k_norm=ev_k_norm.shape, **conv_shapes)
    red = _unpack_small(_all_reduce_small(_pack_small(parts)), small_shapes)
    loss = red["loss"][0]
    for k in ("ffn1_norm", "mix_norm", "ffn2_norm", "ev_b_f", "ev_conv_b", "ev_conv_norm", "ev_q_norm", "ev_k_norm"):
        grad[k] = red[k]
    grad["ev_conv_w"] = lax.dynamic_slice(red["ev_conv_w"], (0, me * ev_conv_w.shape[2]),
                                          (CONV_A_WIDTH, ev_conv_w.shape[2]))[None]
    grad["od_conv_w"] = lax.dynamic_slice(red["od_conv_w"], (0, me * od_conv_w.shape[2]),
                                          (CONV_C_WIDTH, od_conv_w.shape[2]))[None]

    big = ("ffn1_w_gate", "ffn1_w_up", "ffn1_w_down", "ffn2_w_gate", "ffn2_w_up", "ffn2_w_down",
           "ev_w_in", "ev_w_out", "od_w_in", "od_w_out")
    delta, new_m, new_v = {}, {}, {}
    for k in big:
        shp = weights[k].shape
        flat = lambda t: t.reshape(-1, shp[-1])
        dk, mk, vk = _adamw(flat(weights[k]), flat(grad[k]), flat(m_in[k]), flat(v_in[k]), f"adamw_{k}")
        delta[k], new_m[k], new_v[k] = dk.reshape(shp), mk.reshape(shp), vk.reshape(shp)
    rest = [k for k in order if k not in big]
    cat = lambda src: jnp.concatenate([src[k].reshape(-1) for k in rest])
    n_small = sum(math.prod(weights[k].shape) for k in rest)
    n_pad = _round_up(n_small, 8 * V7X_LANES)
    as_rows = lambda t: jnp.pad(t, (0, n_pad - n_small)).reshape(-1, V7X_LANES)
    v_rows = jnp.pad(cat(v_in), (0, n_pad - n_small), constant_values=1.0).reshape(-1, V7X_LANES)
    ds, ms, vs = _adamw(as_rows(cat(weights)), as_rows(cat(grad)), as_rows(cat(m_in)), v_rows, "adamw_small")
    pos = 0
    for k in rest:
        n = math.prod(weights[k].shape)
        for dst, src in ((delta, ds), (new_m, ms), (new_v, vs)):
            dst[k] = src.reshape(-1)[pos:pos + n].reshape(weights[k].shape)
        pos += n

    return (loss, grad_x[None], *[grad[k] for k in order], *[delta[k] for k in order],
            *[new_m[k] for k in order], *[new_v[k] for k in order])
```

```python
import math

import jax
import jax.numpy as jnp
from jax import lax
from jax.experimental import pallas as pl
from jax.experimental.pallas import tpu as pltpu

F32 = jnp.float32
BF16 = jnp.bfloat16
SDS = jax.ShapeDtypeStruct
MESH = pl.DeviceIdType.MESH

N_DEV = 8
EPS = 1e-6
FFN_RES = 0.5
HEAD_DIM = 64
N_HEADS = 8
D_ATTN = N_HEADS * HEAD_DIM
N_PAIRS = N_HEADS // 2
PAIR = 2 * HEAD_DIM
ATTN_SCALE = 1.0 / math.sqrt(HEAD_DIM)
CONV_A_WIDTH = 31
CONV_A_HALO = 32
CONV_C_WIDTH = 3
CONV_C_HALO = 8
NEG_BIG = -1e30
ADAM_LR, ADAM_B1, ADAM_B2, ADAM_EPS, ADAM_WD, ADAM_STEP = 0.001, 0.9, 0.999, 1e-08, 0.01, 10

V7X_VMEM_BYTES = 64 * 1024 * 1024
V7X_LANES = 128
BF16_ROWS = 16
MIB = 1024 * 1024

NT = (((1,), (1,)), ((), ()))
TN = (((0,), (0,)), ((), ()))


def _call(body, *, name, out_shape, in_specs, out_specs, grid=(), scratch=(), dims=None, vmem_mb=32, **kw):
    params = dict(vmem_limit_bytes=min(vmem_mb * MIB, V7X_VMEM_BYTES - 4 * MIB))
    if dims is not None:
        params["dimension_semantics"] = dims
    call = pl.pallas_call(
        body, name=name, grid=grid, in_specs=in_specs, out_specs=out_specs, out_shape=_in_hbm(out_shape),
        scratch_shapes=list(scratch), compiler_params=pltpu.CompilerParams(**params), **kw)
    return lambda *args: call(*[_keep_in_hbm(a) for a in args])


LARGE_OPERAND_BYTES = MIB


def _is_large(a):
    return a.ndim >= 2 and math.prod(a.shape) * jnp.dtype(a.dtype).itemsize >= LARGE_OPERAND_BYTES


def _keep_in_hbm(a):
    return pltpu.with_memory_space_constraint(a, pltpu.HBM) if _is_large(a) else a


def _in_hbm(out_shape):
    one = lambda s: pltpu.HBM(s.shape, s.dtype) if _is_large(s) else s
    return [one(s) for s in out_shape] if isinstance(out_shape, (list, tuple)) else one(out_shape)


def _tile(n, want=512):
    return want if n % want == 0 else n


def _rows(tm, d, col=0):
    return pl.BlockSpec((tm, d), lambda i: (i, col))


def _const(shape):
    return pl.BlockSpec(shape, lambda *_: (0,) * len(shape))


ANY = pl.BlockSpec(memory_space=pl.ANY)
VMEM = pl.BlockSpec(memory_space=pltpu.VMEM)


def _sigmoid(x):
    return 1.0 / (1.0 + jnp.exp(-x))


def _rmsnorm_fwd(x, gain, name):
    s, d = x.shape
    tm = _tile(s)

    def body(x_ref, g_ref, o_ref):
        xv = x_ref[...]
        r = lax.rsqrt(jnp.mean(xv * xv, axis=-1, keepdims=True) + EPS)
        o_ref[...] = (xv * r * g_ref[...]).astype(BF16)

    return _call(body, name=name, grid=(s // tm,), in_specs=[_rows(tm, d), _const((1, d))],
                 out_specs=_rows(tm, d), out_shape=SDS((s, d), BF16), dims=("parallel",))(x, gain)


def _rmsnorm_bwd(x, gain, dxn, dres, name):
    s, d = x.shape
    tm = _tile(s)

    def body(x_ref, g_ref, dxn_ref, dres_ref, dx_ref, dg_ref):
        xv = x_ref[...]
        r = lax.rsqrt(jnp.mean(xv * xv, axis=-1, keepdims=True) + EPS)
        xh = xv * r
        dv = dxn_ref[...]

        @pl.when(pl.program_id(0) == 0)
        def _():
            dg_ref[...] = jnp.zeros_like(dg_ref)

        dg_ref[...] += jnp.sum(dv * xh, axis=0, keepdims=True)
        dxh = dv * g_ref[...]
        dx_ref[...] = dres_ref[...] + r * (dxh - xh * jnp.mean(dxh * xh, axis=-1, keepdims=True))

    return _call(body, name=name, grid=(s // tm,),
                 in_specs=[_rows(tm, d), _const((1, d)), _rows(tm, d), _rows(tm, d)],
                 out_specs=[_rows(tm, d), _const((1, d))],
                 out_shape=[SDS((s, d), F32), SDS((1, d), F32)], dims=("arbitrary",))(x, gain, dxn, dres)


def _col_tile(n):
    for t in (1536, 1280, 1024, 768, 512, 256, 128):
        if n % t == 0:
            return t
    return n


def _mm(a, b, mode, name, out_dtype=F32, add=None):
    if mode == "tn":
        k, m = a.shape
        n = b.shape[1]
        bm = next((t for t in range(768, 0, -V7X_LANES) if m % t == 0), m)

        def body_tn(a_ref, b_ref, o_ref):
            o_ref[...] = lax.dot_general(a_ref[...].astype(BF16), b_ref[...].astype(BF16), TN,
                                         preferred_element_type=F32).astype(out_dtype)

        return _call(body_tn, name=name, grid=(m // bm,),
                     in_specs=[pl.BlockSpec((k, bm), lambda i: (0, i)), _const((k, n))],
                     out_specs=pl.BlockSpec((bm, n), lambda i: (i, 0)),
                     out_shape=SDS((m, n), out_dtype), dims=("parallel",), vmem_mb=48)(a, b)
    m, k = a.shape
    n = b.shape[0] if mode == "nt" else b.shape[1]
    tm, tn = _tile(m), _col_tile(n)
    dn = NT if mode == "nt" else (((1,), (0,)), ((), ()))

    def body(a_ref, b_ref, *rest):
        o_ref = rest[-1]
        acc = lax.dot_general(a_ref[...].astype(BF16), b_ref[...].astype(BF16), dn, preferred_element_type=F32)
        if add is not None:
            acc = acc + rest[0][...]
        o_ref[...] = acc.astype(out_dtype)

    b_spec = (pl.BlockSpec((tn, k), lambda i, j: (j, 0)) if mode == "nt"
              else pl.BlockSpec((k, tn), lambda i, j: (0, j)))
    in_specs = [pl.BlockSpec((tm, k), lambda i, j: (i, 0)), b_spec]
    args = [a, b]
    if add is not None:
        in_specs.append(pl.BlockSpec((tm, tn), lambda i, j: (i, j)))
        args.append(add)
    return _call(body, name=name, grid=(m // tm, n // tn), in_specs=in_specs,
                 out_specs=pl.BlockSpec((tm, tn), lambda i, j: (i, j)),
                 out_shape=SDS((m, n), out_dtype), dims=("parallel", "parallel"), vmem_mb=48)(*args)


def _mm_tn(a, b, name, out_dtype=F32):
    return _mm(a, b, "tn", name, out_dtype)


FFN_TM = 256
FFN_FWD_TM = 512
FFN_CHUNK = 256


def _load_ffn_weights(w_hbm, offs, fs, dsts, sems):
    copies = []
    for wi, (off, dst) in enumerate(zip(offs, dsts)):
        for j in range(N_DEV):
            cp = pltpu.make_async_copy(w_hbm.at[j, pl.ds(off, fs), :], dst.at[pl.ds(j * fs, fs), :],
                                       sems.at[wi * N_DEV + j])
            cp.start()
            copies.append(cp)
    for cp in copies:
        cp.wait()


def _ffn_fwd(x, gain, wall, offs, fs, name, gather_src=None):
    s, d = x.shape
    f = fs * N_DEV
    tm, ch = _tile(s, FFN_FWD_TM), FFN_CHUNK
    n = s // tm
    gathers = gather_src is not None

    def body(x_ref, gain_ref, w_hbm, *rest):
        if gathers:
            (src_hbm, out_ref, xn_ref, g_ref, u_ref, h_ref, gathered, wg_s, wu_s, wd_s, sems,
             send_sems, recv_sems, local_sem) = rest
            start, forward, finish = _gather_phases(src_hbm, gathered, send_sems, recv_sems, local_sem)
            pl.when(pl.program_id(0) == 0)(start)
            pl.when(pl.program_id(0) == (3 * n) // 4)(forward)
        else:
            out_ref, xn_ref, g_ref, u_ref, h_ref, wg_s, wu_s, wd_s, sems = rest

        @pl.when(pl.program_id(0) == 0)
        def _():
            _load_ffn_weights(w_hbm, offs, fs, (wg_s, wu_s, wd_s), sems)

        xv = x_ref[...]
        xnv = (xv * lax.rsqrt(jnp.mean(xv * xv, axis=-1, keepdims=True) + EPS) * gain_ref[...]).astype(BF16)
        xn_ref[...] = xnv
        acc = jnp.zeros((tm, d), F32)
        for c in range(f // ch):
            sl = slice(c * ch, (c + 1) * ch)
            gb = lax.dot_general(xnv, wg_s[sl, :], NT, preferred_element_type=F32).astype(BF16)
            ub = lax.dot_general(xnv, wu_s[sl, :], NT, preferred_element_type=F32).astype(BF16)
            g_ref[:, sl] = gb
            u_ref[:, sl] = ub
            g = gb.astype(F32)
            hb = (g * _sigmoid(g) * ub.astype(F32)).astype(BF16)
            h_ref[:, sl] = hb
            acc = acc + jnp.dot(hb, wd_s[sl, :], preferred_element_type=F32)
        out_ref[...] = xv + FFN_RES * acc
        if gathers:
            pl.when(pl.program_id(0) == n - 1)(finish)

    in_specs, args = [_rows(tm, d), _const((1, d)), ANY], [x, gain, wall]
    out_specs = [_rows(tm, d), _rows(tm, d), _rows(tm, f), _rows(tm, f), _rows(tm, f)]
    out_shape = [SDS((s, d), F32), SDS((s, d), BF16), SDS((s, f), BF16), SDS((s, f), BF16), SDS((s, f), BF16)]
    scratch = [pltpu.VMEM((f, d), BF16)] * 3 + [pltpu.SemaphoreType.DMA((3 * N_DEV,))]
    if gathers:
        in_specs.append(ANY)
        args.append(gather_src)
        out_specs.append(ANY)
        out_shape.append(SDS((N_DEV,) + gather_src.shape, gather_src.dtype))
        scratch += GATHER_SEMS
    return _call(body, name=name, grid=(n,), in_specs=in_specs, out_specs=out_specs, out_shape=out_shape,
                 scratch=scratch, dims=("arbitrary",), vmem_mb=56)(*args)


def _ffn_bwd_act(dout, g, u, wall, offs, fs, name, exchange_src=None):
    s, d = dout.shape
    f = fs * N_DEV
    tm, ch = _tile(s, FFN_TM), FFN_CHUNK
    n = s // tm
    exchanges = exchange_src is not None
    if exchanges:
        phases_of, shape_of, exchange_sems = EXCHANGES[exchange_src[0]]

    def body(dout_ref, g_ref, u_ref, w_hbm, *rest):
        if exchanges:
            t_hbm, dg_ref, du_ref, dy_ref, dxn_ref, rcv_ref, wg_s, wu_s, wd_s, sems, send_sems, recv_sems = rest
            start, finish = phases_of(t_hbm, rcv_ref, send_sems, recv_sems)
            pl.when(pl.program_id(0) == 0)(start)
        else:
            dg_ref, du_ref, dy_ref, dxn_ref, wg_s, wu_s, wd_s, sems = rest

        @pl.when(pl.program_id(0) == 0)
        def _():
            _load_ffn_weights(w_hbm, offs, fs, (wg_s, wu_s, wd_s), sems)

        dy = (FFN_RES * dout_ref[...]).astype(BF16)
        dy_ref[...] = dy
        acc = jnp.zeros((tm, d), F32)
        for c in range(f // ch):
            sl = slice(c * ch, (c + 1) * ch)
            dh = lax.dot_general(dy, wd_s[sl, :], NT, preferred_element_type=F32)
            gv = g_ref[:, sl].astype(F32)
            uv = u_ref[:, sl].astype(F32)
            sg = _sigmoid(gv)
            dgb = (dh * uv * sg * (1.0 + gv * (1.0 - sg))).astype(BF16)
            dub = (dh * gv * sg).astype(BF16)
            dg_ref[:, sl] = dgb
            du_ref[:, sl] = dub
            acc = acc + jnp.dot(dgb, wg_s[sl, :], preferred_element_type=F32)
            acc = acc + jnp.dot(dub, wu_s[sl, :], preferred_element_type=F32)
        dxn_ref[...] = acc
        if exchanges:
            pl.when(pl.program_id(0) == n - 1)(finish)

    in_specs, args = [_rows(tm, d), _rows(tm, f), _rows(tm, f), ANY], [dout, g, u, wall]
    out_specs = [_rows(tm, f), _rows(tm, f), _rows(tm, d), _rows(tm, d)]
    out_shape = [SDS((s, f), BF16), SDS((s, f), BF16), SDS((s, d), BF16), SDS((s, d), F32)]
    scratch = [pltpu.VMEM((f, d), BF16)] * 3 + [pltpu.SemaphoreType.DMA((3 * N_DEV,))]
    if exchanges:
        in_specs.append(ANY)
        args.append(exchange_src[1])
        out_specs.append(ANY)
        out_shape.append(shape_of(exchange_src[1]))
        scratch += exchange_sems
    return _call(body, name=name, grid=(n,), in_specs=in_specs, out_specs=out_specs, out_shape=out_shape,
                 scratch=scratch, dims=("arbitrary",), vmem_mb=56)(*args)


def _prev_rows(halo, tm, c, col):
    return pl.BlockSpec((halo, c), lambda i: (jnp.maximum(i * (tm // halo) - 1, 0), col))


def _next_rows(halo, tm, c, col, n_blocks):
    return pl.BlockSpec((halo, c), lambda i: (jnp.minimum((i + 1) * (tm // halo), n_blocks - 1), col))


def _conv_a_fwd(z, cw, cb, cn, name):
    s = z.shape[0]
    c = cb.shape[1]
    tm, halo, kw = _tile(s), CONV_A_HALO, CONV_A_WIDTH

    def body(u_ref, g_ref, up_ref, gp_ref, cw_ref, cb_ref, cn_ref, a_ref, a1_ref, buf):
        i = pl.program_id(0)
        buf[0:halo, :] = jnp.where(i > 0, up_ref[...] * _sigmoid(gp_ref[...]), 0.0)
        buf[halo:halo + tm, :] = u_ref[...] * _sigmoid(g_ref[...])
        acc = jnp.zeros((tm, c), F32)
        for k in range(kw):
            acc = acc + cw_ref[k:k + 1, :] * buf[pl.ds(halo - (kw - 1) + k, tm), :]
        a1 = acc + cb_ref[...]
        a1_ref[...] = a1
        a2 = a1 * lax.rsqrt(jnp.mean(a1 * a1, axis=-1, keepdims=True) + EPS) * cn_ref[...]
        a_ref[...] = (a2 * _sigmoid(a2)).astype(BF16)

    return _call(body, name=name, grid=(s // tm,),
                 in_specs=[_rows(tm, c, 0), _rows(tm, c, 1), _prev_rows(halo, tm, c, 0), _prev_rows(halo, tm, c, 1),
                           _const(cw.shape), _const((1, c)), _const((1, c))],
                 out_specs=[_rows(tm, c), _rows(tm, c)],
                 out_shape=[SDS((s, c), BF16), SDS((s, c), F32)],
                 scratch=[pltpu.VMEM((tm + halo, c), F32)], dims=("parallel",))(z, z, z, z, cw, cb, cn)


def _conv_a_bwd_norm(dao, a1, cn, name):
    s, c = a1.shape
    tm = _tile(s)

    def body(da_ref, a1_ref, cn_ref, da1_ref, dcn_ref, dcb_ref):
        a1v = a1_ref[...]
        r = lax.rsqrt(jnp.mean(a1v * a1v, axis=-1, keepdims=True) + EPS)
        xh = a1v * r
        a2 = xh * cn_ref[...]
        sg = _sigmoid(a2)
        da2 = da_ref[...] * sg * (1.0 + a2 * (1.0 - sg))
        dxh = da2 * cn_ref[...]
        da1 = r * (dxh - xh * jnp.mean(dxh * xh, axis=-1, keepdims=True))
        da1_ref[...] = da1

        @pl.when(pl.program_id(0) == 0)
        def _():
            dcn_ref[...] = jnp.zeros_like(dcn_ref)
            dcb_ref[...] = jnp.zeros_like(dcb_ref)

        dcn_ref[...] += jnp.sum(da2 * xh, axis=0, keepdims=True)
        dcb_ref[...] += jnp.sum(da1, axis=0, keepdims=True)

    return _call(body, name=name, grid=(s // tm,),
                 in_specs=[_rows(tm, c, 0), _rows(tm, c), _const((1, c))],
                 out_specs=[_rows(tm, c), _const((1, c)), _const((1, c))],
                 out_shape=[SDS((s, c), F32), SDS((1, c), F32), SDS((1, c), F32)], dims=("arbitrary",))(dao, a1, cn)


def _conv_a_bwd_conv(da1, z, cw, name, exchange_src=None):
    s, c = da1.shape
    tm, halo, kw = _tile(s), CONV_A_HALO, CONV_A_WIDTH
    n = s // tm
    exchanges = exchange_src is not None
    if exchanges:
        phases_of, shape_of, exchange_sems = EXCHANGES[exchange_src[0]]

    def body(d_ref, dn_ref, u_ref, g_ref, up_ref, gp_ref, cw_ref, *rest):
        i = pl.program_id(0)
        if exchanges:
            t_hbm, du_ref, dg_ref, dcw_ref, rcv_ref, buf, bd, send_sems, recv_sems = rest
            start, finish = phases_of(t_hbm, rcv_ref, send_sems, recv_sems)
            pl.when(i == 0)(start)
        else:
            du_ref, dg_ref, dcw_ref, buf, bd = rest
        uv = u_ref[...]
        sg = _sigmoid(g_ref[...])
        buf[0:halo, :] = jnp.where(i > 0, up_ref[...] * _sigmoid(gp_ref[...]), 0.0)
        buf[halo:halo + tm, :] = uv * sg
        dv = d_ref[...]
        bd[0:tm, :] = dv
        bd[tm:tm + halo, :] = jnp.where(i < n - 1, dn_ref[...], 0.0)

        @pl.when(i == 0)
        def _():
            dcw_ref[...] = jnp.zeros_like(dcw_ref)

        da0 = jnp.zeros((tm, c), F32)
        for k in range(kw):
            da0 = da0 + cw_ref[k:k + 1, :] * bd[pl.ds(kw - 1 - k, tm), :]
            dcw_ref[k:k + 1, :] += jnp.sum(dv * buf[pl.ds(halo - (kw - 1) + k, tm), :], axis=0, keepdims=True)
        du_ref[...] = (da0 * sg).astype(BF16)
        dg_ref[...] = (da0 * uv * sg * (1.0 - sg)).astype(BF16)
        if exchanges:
            pl.when(i == n - 1)(finish)

    in_specs = [_rows(tm, c), _next_rows(halo, tm, c, 0, s // halo), _rows(tm, c, 0), _rows(tm, c, 1),
                _prev_rows(halo, tm, c, 0), _prev_rows(halo, tm, c, 1), _const(cw.shape)]
    args = [da1, da1, z, z, z, z, cw]
    out_specs = [_rows(tm, c), _rows(tm, c), _const(cw.shape)]
    out_shape = [SDS((s, c), BF16), SDS((s, c), BF16), SDS(cw.shape, F32)]
    scratch = [pltpu.VMEM((tm + halo, c), F32)] * 2
    if exchanges:
        in_specs.append(ANY)
        args.append(exchange_src[1])
        out_specs.append(ANY)
        out_shape.append(shape_of(exchange_src[1]))
        scratch += exchange_sems
    return _call(body, name=name, grid=(n,), in_specs=in_specs, out_specs=out_specs, out_shape=out_shape,
                 scratch=scratch, dims=("arbitrary",))(*args)


def _lane_is_first_head(tm):
    return lax.broadcasted_iota(jnp.int32, (tm, PAIR), 1) < HEAD_DIM


def _pair_rms(xp, first):
    x2 = xp * xp
    s0 = jnp.sum(jnp.where(first, x2, 0.0), axis=-1, keepdims=True)
    s1 = jnp.sum(jnp.where(first, 0.0, x2), axis=-1, keepdims=True)
    return jnp.where(first, lax.rsqrt(s0 / HEAD_DIM + EPS), lax.rsqrt(s1 / HEAD_DIM + EPS))


def _split3(x):
    hi = x.astype(BF16)
    r1 = x - hi.astype(F32)
    mid = r1.astype(BF16)
    lo = (r1 - mid.astype(F32)).astype(BF16)
    return hi, mid, lo


def _qk_fwd(z, flog, bf, qn2, kn2, name):
    s = z.shape[0]
    tm = _tile(s)
    col0 = (z.shape[1] - 3 * D_ATTN) // D_ATTN

    def body(q_ref, k_ref, v_ref, fl_ref, bf_ref, qn_ref, kn_ref,
             qs_ref, kh_ref, vb_ref, fb_ref, ft_ref, xt_ref, carry):
        i = pl.program_id(0)
        first = _lane_is_first_head(tm)
        for p in range(N_PAIRS):
            sl = slice(p * PAIR, (p + 1) * PAIR)
            q = q_ref[:, sl]
            qs_ref[:, sl] = (q * _pair_rms(q, first) * qn_ref[...] * ATTN_SCALE).astype(BF16)
            k = k_ref[:, sl]
            kh_ref[:, sl] = (k * _pair_rms(k, first) * kn_ref[...]).astype(BF16)
        vb_ref[...] = v_ref[...].astype(BF16)

        xg = fl_ref[...] + bf_ref[...]
        valid = lax.broadcasted_iota(jnp.int32, (tm, V7X_LANES), 1) < N_HEADS
        ls = jnp.where(valid, jnp.minimum(xg, 0.0) - jnp.log(1.0 + jnp.exp(-jnp.abs(xg))), 0.0)
        tri = (lax.broadcasted_iota(jnp.int32, (tm, tm), 1) <= lax.broadcasted_iota(jnp.int32, (tm, tm), 0)).astype(BF16)
        cs = jnp.zeros((tm, V7X_LANES), F32)
        for part in _split3(ls):
            cs = cs + jnp.dot(tri, part, preferred_element_type=F32)

        @pl.when(i == 0)
        def _():
            carry[...] = jnp.zeros_like(carry)

        fv = cs + carry[0:1, :]
        carry[0:1, :] = fv[tm - 1:tm, :]
        ft_ref[...] = fv.T[0:N_HEADS, :]
        xt_ref[...] = xg.T[0:N_HEADS, :]
        for p in range(N_PAIRS):
            fb_ref[:, p * PAIR:(p + 1) * PAIR] = jnp.where(first, fv[:, 2 * p:2 * p + 1], fv[:, 2 * p + 1:2 * p + 2])

    wide = lambda col: pl.BlockSpec((tm, D_ATTN), lambda i: (i, col))
    tcol = pl.BlockSpec((N_HEADS, tm), lambda i: (0, i))
    return _call(body, name=name, grid=(s // tm,),
                 in_specs=[wide(col0), wide(col0 + 1), wide(col0 + 2), _rows(tm, V7X_LANES),
                           _const((1, V7X_LANES)), _const((1, PAIR)), _const((1, PAIR))],
                 out_specs=[wide(0), wide(0), wide(0), wide(0), tcol, tcol],
                 out_shape=[SDS((s, D_ATTN), BF16)] * 3 + [SDS((s, D_ATTN), F32), SDS((N_HEADS, s), F32),
                                                          SDS((N_HEADS, s), F32)],
                 scratch=[pltpu.VMEM((8, V7X_LANES), F32)], dims=("arbitrary",))(z, z, z, flog, bf, qn2, kn2)


ATTN_FWD_SUB = 256
ATTN_BWD_SUB = 512


def _causal_schedule(nq, key_major):
    if key_major:
        pairs = [(i, j) for j in range(nq) for i in range(j, nq)]
    else:
        pairs = [(i, j) for i in range(nq) for j in range(i + 1)]
    return (jnp.asarray([p[0] for p in pairs], jnp.int32), jnp.asarray([p[1] for p in pairs], jnp.int32))


def _sub_scores(qp, kp, ft_row, mine, r, masked, sub, tk):
    qm = jnp.where(mine, qp, jnp.zeros_like(qp))
    s2 = lax.dot_general(qm, kp, NT, preferred_element_type=F32) - ft_row
    if masked:
        row = r * sub + lax.broadcasted_iota(jnp.int32, (sub, tk), 0)
        s2 = jnp.where(lax.broadcasted_iota(jnp.int32, (sub, tk), 1) <= row, s2, NEG_BIG)
    return s2


def _attn_fwd(qs, kh, vb, fb, ft, name, gather_src=None):
    s = qs.shape[0]
    tq = tk = _tile(s)
    nq = s // tq
    sub = min(ATTN_FWD_SUB, tq)
    ii, jj = _causal_schedule(nq, key_major=False)
    n_steps = ii.shape[0]
    gathers = gather_src is not None

    def body(ii_ref, jj_ref, q_ref, k_ref, v_ref, fq_ref, ft_ref, *rest):
        if gathers:
            x_hbm, o_ref, lse_ref, wall_ref, m_s, l_s, acc_s, send_sems, recv_sems, local_sem = rest
        else:
            o_ref, lse_ref, m_s, l_s, acc_s = rest
        p, t = pl.program_id(0), pl.program_id(1)
        i, j = ii_ref[t], jj_ref[t]
        first = _lane_is_first_head(sub)
        if gathers:
            start, forward, finish = _gather_phases(x_hbm, wall_ref, send_sems, recv_sems, local_sem)
            pl.when(jnp.logical_and(p == 0, t == 0))(start)
            pl.when(jnp.logical_and(p == N_PAIRS - 1, t == 0))(forward)

        @pl.when(j == 0)
        def _():
            m_s[...] = jnp.full_like(m_s, NEG_BIG)
            l_s[...] = jnp.zeros_like(l_s)
            acc_s[...] = jnp.zeros_like(acc_s)

        def tile(masked):
            kp, vp = k_ref[...], v_ref[...]
            q_all, fq_all, acc_all = q_ref[...], fq_ref[...], acc_s[...]
            m_all, l_all = (m_s[0], m_s[1]), (l_s[0], l_s[1])
            ft_rows = [ft_ref[pl.ds(2 * p + h, 1), :] for h in range(2)]
            m_out, l_out, acc_out = ([], []), ([], []), []
            for r in range(tq // sub):
                rows = slice(r * sub, (r + 1) * sub)
                qp, fq, acc = q_all[rows, :], fq_all[rows, :], acc_all[rows, :]
                kc = (r + 1) * sub if masked else tk
                new = []
                for h in range(2):
                    mine = first if h == 0 else jnp.logical_not(first)
                    s2 = _sub_scores(qp, kp[:kc, :], ft_rows[h][:, :kc], mine, r, masked, sub, kc)
                    fqh = fq[:, h * HEAD_DIM:h * HEAD_DIM + 1]
                    m_old = m_all[h][rows, :]
                    m_new = jnp.maximum(m_old, jnp.max(s2, axis=-1, keepdims=True) + fqh)
                    pr = jnp.exp(s2 - (m_new - fqh))
                    alpha = jnp.exp(m_old - m_new)
                    l_out[h].append(alpha * l_all[h][rows, :] + jnp.sum(pr, axis=-1, keepdims=True))
                    m_out[h].append(m_new)
                    new.append(alpha * acc + jnp.dot(pr.astype(BF16), vp[:kc, :], preferred_element_type=F32))
                acc_out.append(jnp.where(first, new[0], new[1]))
            for h in range(2):
                m_s[h] = jnp.concatenate(m_out[h], axis=0)
                l_s[h] = jnp.concatenate(l_out[h], axis=0)
            acc_s[...] = jnp.concatenate(acc_out, axis=0)

        @pl.when(j < i)
        def _():
            tile(False)

        @pl.when(j == i)
        def _():
            tile(True)
            whole = _lane_is_first_head(tq)
            l_pair = jnp.where(whole, l_s[0], l_s[1])
            o_ref[...] = acc_s[...] / l_pair
            lse_ref[...] = jnp.where(whole, m_s[0], m_s[1]) + jnp.log(l_pair)

        if gathers:
            pl.when(jnp.logical_and(p == N_PAIRS - 1, t == n_steps - 1))(finish)

    qblk = pl.BlockSpec((tq, PAIR), lambda p, t, ii_r, jj_r: (ii_r[t], p))
    kblk = pl.BlockSpec((tk, PAIR), lambda p, t, ii_r, jj_r: (jj_r[t], p))
    in_specs = [qblk, kblk, kblk, qblk, pl.BlockSpec((N_HEADS, tk), lambda p, t, ii_r, jj_r: (0, jj_r[t]))]
    out_specs, out_shape = [qblk, qblk], [SDS((s, D_ATTN), F32)] * 2
    scratch = [pltpu.VMEM((2, tq, 1), F32), pltpu.VMEM((2, tq, 1), F32), pltpu.VMEM((tq, PAIR), F32)]
    args = [ii, jj, qs, kh, vb, fb, ft]
    if gathers:
        in_specs.append(ANY)
        out_specs.append(ANY)
        out_shape.append(SDS((N_DEV,) + gather_src.shape, gather_src.dtype))
        scratch += GATHER_SEMS
        args.append(gather_src)
    grid_spec = pltpu.PrefetchScalarGridSpec(num_scalar_prefetch=2, grid=(N_PAIRS, n_steps), in_specs=in_specs,
                                             out_specs=out_specs, scratch_shapes=scratch)
    return pl.pallas_call(
        body, name=name, grid_spec=grid_spec, out_shape=_in_hbm(out_shape),
        compiler_params=pltpu.CompilerParams(dimension_semantics=("arbitrary", "arbitrary"),
                                             vmem_limit_bytes=32 * MIB))(*[_keep_in_hbm(a) for a in args])


def _attn_bwd(qs, kh, vb, fb, ft, lse, o, dao, name, exchange_src=None):
    s = qs.shape[0]
    tq = tk = _tile(s)
    nq = s // tq
    sub = min(ATTN_BWD_SUB, tq)
    ii, jj = _causal_schedule(nq, key_major=True)
    n_steps = ii.shape[0]

    exchanges = exchange_src is not None

    def body(ii_ref, jj_ref, q_ref, k_ref, v_ref, fq_ref, ft_ref, lse_ref, o_ref, do_ref, *rest):
        if exchanges:
            t_hbm, dq_ref, rs_ref, dk_ref, dv_ref, df_ref, rcv_ref, dk_s, dv_s, df_s, send_sems, recv_sems = rest
        else:
            dq_ref, rs_ref, dk_ref, dv_ref, df_ref, dk_s, dv_s, df_s = rest
        p, t = pl.program_id(0), pl.program_id(1)
        i, j = ii_ref[t], jj_ref[t]
        first = _lane_is_first_head(sub)
        first_k = _lane_is_first_head(tk)
        if exchanges:
            start, finish = _chip_exchange_phases(t_hbm, rcv_ref, send_sems, recv_sems)
            pl.when(jnp.logical_and(p == 0, t == 0))(start)

        @pl.when(t == 0)
        def _():
            dq_ref[...] = jnp.zeros_like(dq_ref)
            rs_ref[...] = jnp.zeros_like(rs_ref)

        @pl.when(i == j)
        def _():
            dk_s[...] = jnp.zeros_like(dk_s)
            dv_s[...] = jnp.zeros_like(dv_s)
            df_s[...] = jnp.zeros_like(df_s)

        def tile(masked):
            kp, vp = k_ref[...], v_ref[...]
            q_all, fq_all, lse_all, o_all, do_all = q_ref[...], fq_ref[...], lse_ref[...], o_ref[...], do_ref[...]
            ft_rows = [ft_ref[pl.ds(2 * p + h, 1), :] for h in range(2)]
            dq_out, rs_out = [], []
            dk_acc, dv_acc = jnp.zeros((tk, PAIR), F32), jnp.zeros((tk, PAIR), F32)
            df_acc = [jnp.zeros((1, tk), F32), jnp.zeros((1, tk), F32)]
            for r in range(tq // sub):
                rows = slice(r * sub, (r + 1) * sub)
                qp, fq, lse, ov, dall = q_all[rows, :], fq_all[rows, :], lse_all[rows, :], o_all[rows, :], do_all[rows, :]
                dq_h, dk_h, dv_h, rs_h = [], [], [], []
                for h in range(2):
                    mine = first if h == 0 else jnp.logical_not(first)
                    s2 = _sub_scores(qp, kp, ft_rows[h], mine, r, masked, sub, tk)
                    lane = slice(h * HEAD_DIM, h * HEAD_DIM + 1)
                    pr = jnp.exp(s2 - (lse[:, lane] - fq[:, lane]))
                    dov = jnp.where(mine, dall, 0.0)
                    dsum = jnp.sum(dov * ov, axis=-1, keepdims=True)
                    dom = dov.astype(BF16)
                    dom_lo = (dov - dom.astype(F32)).astype(BF16)
                    dp = lax.dot_general(dom, vp, NT, preferred_element_type=F32)
                    dp = dp + lax.dot_general(dom_lo, vp, NT, preferred_element_type=F32)
                    ds = pr * (dp - dsum)
                    dsb = ds.astype(BF16)
                    dq_h.append(jnp.dot(dsb, kp, preferred_element_type=F32))
                    dk_h.append(lax.dot_general(dsb, qp, TN, preferred_element_type=F32))
                    dv_h.append(lax.dot_general(pr.astype(BF16), dom, TN, preferred_element_type=F32))
                    rs_h.append(jnp.sum(ds, axis=-1, keepdims=True))
                    df_acc[h] = df_acc[h] - jnp.sum(ds, axis=0, keepdims=True)
                dq_out.append(jnp.where(first, dq_h[0], dq_h[1]))
                rs_out.append(jnp.where(first, rs_h[0], rs_h[1]))
                dk_acc = dk_acc + jnp.where(first_k, dk_h[0], dk_h[1])
                dv_acc = dv_acc + jnp.where(first_k, dv_h[0], dv_h[1])
            grows = pl.ds(pl.multiple_of(i * tq, tq), tq)
            dq_ref[grows, :] += jnp.concatenate(dq_out, axis=0)
            rs_ref[grows, :] += jnp.concatenate(rs_out, axis=0)
            dk_s[...] += dk_acc
            dv_s[...] += dv_acc
            for h in range(2):
                df_s[h:h + 1, :] += df_acc[h]

        @pl.when(j < i)
        def _():
            tile(False)

        @pl.when(j == i)
        def _():
            tile(True)

        @pl.when(i == nq - 1)
        def _():
            dk_ref[...] = dk_s[...]
            dv_ref[...] = dv_s[...]
            df_ref[0] = df_s[...]

        if exchanges:
            pl.when(jnp.logical_and(p == N_PAIRS - 1, t == n_steps - 1))(finish)

    qblk = pl.BlockSpec((tq, PAIR), lambda p, t, ii_r, jj_r: (ii_r[t], p))
    kblk = pl.BlockSpec((tk, PAIR), lambda p, t, ii_r, jj_r: (jj_r[t], p))
    doblk = pl.BlockSpec((tq, PAIR), lambda p, t, ii_r, jj_r: (ii_r[t], N_PAIRS + p))
    whole = pl.BlockSpec((s, PAIR), lambda p, t, ii_r, jj_r: (0, p))
    in_specs = [qblk, kblk, kblk, qblk, pl.BlockSpec((N_HEADS, tk), lambda p, t, ii_r, jj_r: (0, jj_r[t])),
                qblk, qblk, doblk]
    out_specs = [whole, whole, kblk, kblk, pl.BlockSpec((1, 8, tk), lambda p, t, ii_r, jj_r: (p, 0, jj_r[t]))]
    out_shape = [SDS((s, D_ATTN), F32)] * 4 + [SDS((N_PAIRS, 8, s), F32)]
    scratch = [pltpu.VMEM((tk, PAIR), F32), pltpu.VMEM((tk, PAIR), F32), pltpu.VMEM((8, tk), F32)]
    args = [ii, jj, qs, kh, vb, fb, ft, lse, o, dao]
    if exchanges:
        in_specs.append(ANY)
        out_specs.append(ANY)
        out_shape.append(SDS((3,) + exchange_src.shape[1:], exchange_src.dtype))
        scratch += EXCHANGE_SEMS
        args.append(exchange_src)
    grid_spec = pltpu.PrefetchScalarGridSpec(num_scalar_prefetch=2, grid=(N_PAIRS, n_steps), in_specs=in_specs,
                                             out_specs=out_specs, scratch_shapes=scratch)
    return pl.pallas_call(
        body, name=name, grid_spec=grid_spec, out_shape=_in_hbm(out_shape),
        compiler_params=pltpu.CompilerParams(dimension_semantics=("arbitrary", "arbitrary"),
                                             vmem_limit_bytes=40 * MIB))(*[_keep_in_hbm(a) for a in args])


def _qk_bwd(z, dqs, dkh, dv, qn2, kn2, name):
    s = z.shape[0]
    tm = _tile(s)
    col0 = (z.shape[1] - 3 * D_ATTN) // D_ATTN

    def body(q_ref, k_ref, dqs_ref, dkh_ref, dv_ref, qn_ref, kn_ref, dq_ref, dk_ref, dvb_ref, dqn_ref, dkn_ref):
        first = _lane_is_first_head(tm)

        @pl.when(pl.program_id(0) == 0)
        def _():
            dqn_ref[...] = jnp.zeros_like(dqn_ref)
            dkn_ref[...] = jnp.zeros_like(dkn_ref)

        def through(x_ref, dy_ref, gain_ref, dx_ref, dgain_ref, scale):
            for p in range(N_PAIRS):
                sl = slice(p * PAIR, (p + 1) * PAIR)
                xv = x_ref[:, sl]
                r = _pair_rms(xv, first)
                xh = xv * r
                dy = dy_ref[:, sl] * scale
                dgain_ref[:, sl] += jnp.sum(dy * xh, axis=0, keepdims=True)
                dxh = dy * gain_ref[...]
                t = dxh * xh
                m0 = jnp.sum(jnp.where(first, t, 0.0), axis=-1, keepdims=True)
                m1 = jnp.sum(jnp.where(first, 0.0, t), axis=-1, keepdims=True)
                mean = jnp.where(first, m0, m1) / HEAD_DIM
                dx_ref[:, sl] = (r * (dxh - xh * mean)).astype(BF16)

        through(q_ref, dqs_ref, qn_ref, dq_ref, dqn_ref, ATTN_SCALE)
        through(k_ref, dkh_ref, kn_ref, dk_ref, dkn_ref, 1.0)
        dvb_ref[...] = dv_ref[...].astype(BF16)

    wide = lambda col: pl.BlockSpec((tm, D_ATTN), lambda i: (i, col))
    return _call(body, name=name, grid=(s // tm,),
                 in_specs=[wide(col0), wide(col0 + 1), wide(0), wide(0), wide(0), _const((1, PAIR)), _const((1, PAIR))],
                 out_specs=[wide(0), wide(0), wide(0), _const((1, D_ATTN)), _const((1, D_ATTN))],
                 out_shape=[SDS((s, D_ATTN), BF16)] * 3 + [SDS((1, D_ATTN), F32)] * 2,
                 dims=("arbitrary",))(z, z, dqs, dkh, dv, qn2, kn2)


def _gate_bwd(dft, xt, name):
    s = xt.shape[1]
    tm = _tile(s)
    n = s // tm

    def body(df_ref, xt_ref, dxt_ref, dx_ref, db_ref, carry):
        i = pl.program_id(0)

        @pl.when(i == 0)
        def _():
            carry[...] = jnp.zeros_like(carry)
            db_ref[...] = jnp.zeros_like(db_ref)

        tri = (lax.broadcasted_iota(jnp.int32, (tm, tm), 0) >= lax.broadcasted_iota(jnp.int32, (tm, tm), 1)).astype(BF16)
        rc = jnp.zeros((N_HEADS, tm), F32)
        for part in _split3(df_ref[...]):
            rc = rc + jnp.dot(part, tri, preferred_element_type=F32)
        dls = rc + carry[:, 0:1]
        carry[...] = jnp.broadcast_to(dls[:, 0:1], carry.shape)
        dxt = dls * _sigmoid(-xt_ref[...])
        dxt_ref[...] = dxt
        db_ref[...] += jnp.broadcast_to(jnp.sum(dxt, axis=-1, keepdims=True), db_ref.shape)
        padded = jnp.concatenate([dxt, jnp.zeros((V7X_LANES - N_HEADS, tm), F32)], axis=0)
        dx_ref[...] = padded.T

    rev = pl.BlockSpec((N_HEADS, tm), lambda i: (0, n - 1 - i))
    return _call(body, name=name, grid=(n,), in_specs=[rev, rev],
                 out_specs=[rev, pl.BlockSpec((tm, V7X_LANES), lambda i: (n - 1 - i, 0)), _const((N_HEADS, V7X_LANES))],
                 out_shape=[SDS((N_HEADS, s), F32), SDS((s, V7X_LANES), F32), SDS((N_HEADS, V7X_LANES), F32)],
                 scratch=[pltpu.VMEM((N_HEADS, V7X_LANES), F32)], dims=("arbitrary",))(dft, xt)


def _conv_c_fwd(z, cw, name):
    s = z.shape[0]
    c = z.shape[1] // 3
    tm, halo, kw = _tile(s), CONV_C_HALO, CONV_C_WIDTH

    def body(gb_ref, gc_ref, hh_ref, gcp_ref, hhp_ref, cw_ref, y_ref, buf):
        i = pl.program_id(0)
        buf[0:halo, :] = jnp.where(i > 0, gcp_ref[...] * hhp_ref[...], 0.0)
        buf[halo:halo + tm, :] = gc_ref[...] * hh_ref[...]
        c1 = jnp.zeros((tm, c), F32)
        for k in range(kw):
            c1 = c1 + cw_ref[k:k + 1, :] * buf[pl.ds(halo - (kw - 1) + k, tm), :]
        y_ref[...] = (gb_ref[...] * c1).astype(BF16)

    return _call(body, name=name, grid=(s // tm,),
                 in_specs=[_rows(tm, c, 0), _rows(tm, c, 1), _rows(tm, c, 2), _prev_rows(halo, tm, c, 1),
                           _prev_rows(halo, tm, c, 2), _const(cw.shape)],
                 out_specs=_rows(tm, c), out_shape=SDS((s, c), BF16),
                 scratch=[pltpu.VMEM((tm + halo, c), F32)], dims=("parallel",))(z, z, z, z, z, cw)


def _conv_c_bwd(dy0, z, cw, name):
    s = z.shape[0]
    c = z.shape[1] // 3
    tm, halo, kw = _tile(s), CONV_C_HALO, CONV_C_WIDTH
    n = s // tm

    def body(dy_ref, dyn_ref, gb_ref, gbn_ref, gc_ref, hh_ref, gcp_ref, hhp_ref, cw_ref, dz_ref, dcw_ref, buf, bd):
        i = pl.program_id(0)
        gcv, hhv, dyv = gc_ref[...], hh_ref[...], dy_ref[...]
        buf[0:halo, :] = jnp.where(i > 0, gcp_ref[...] * hhp_ref[...], 0.0)
        buf[halo:halo + tm, :] = gcv * hhv
        dc1 = dyv * gb_ref[...]
        bd[0:tm, :] = dc1
        bd[tm:tm + halo, :] = jnp.where(i < n - 1, dyn_ref[...] * gbn_ref[...], 0.0)

        @pl.when(i == 0)
        def _():
            dcw_ref[...] = jnp.zeros_like(dcw_ref)

        c1 = jnp.zeros((tm, c), F32)
        dc0 = jnp.zeros((tm, c), F32)
        for k in range(kw):
            shifted = buf[pl.ds(halo - (kw - 1) + k, tm), :]
            c1 = c1 + cw_ref[k:k + 1, :] * shifted
            dc0 = dc0 + cw_ref[k:k + 1, :] * bd[pl.ds(kw - 1 - k, tm), :]
            dcw_ref[k:k + 1, :] += jnp.sum(dc1 * shifted, axis=0, keepdims=True)
        dz_ref[:, 0:c] = (dyv * c1).astype(BF16)
        dz_ref[:, c:2 * c] = (dc0 * hhv).astype(BF16)
        dz_ref[:, 2 * c:3 * c] = (dc0 * gcv).astype(BF16)

    return _call(body, name=name, grid=(n,),
                 in_specs=[_rows(tm, c), _next_rows(halo, tm, c, 0, s // halo), _rows(tm, c, 0),
                           _next_rows(halo, tm, c, 0, s // halo), _rows(tm, c, 1), _rows(tm, c, 2),
                           _prev_rows(halo, tm, c, 1), _prev_rows(halo, tm, c, 2), _const(cw.shape)],
                 out_specs=[_rows(tm, 3 * c), _const(cw.shape)],
                 out_shape=[SDS((s, 3 * c), BF16), SDS(cw.shape, F32)],
                 scratch=[pltpu.VMEM((tm + halo, c), F32)] * 2, dims=("arbitrary",),
                 vmem_mb=48)(dy0, dy0, z, z, z, z, z, z, cw)


def _loss_head(y, target, name):
    s, d = y.shape
    tm = _tile(s)

    def body(y_ref, t_ref, loss_ref, dy_ref):
        e = y_ref[...] - t_ref[...]

        @pl.when(pl.program_id(0) == 0)
        def _():
            loss_ref[...] = jnp.zeros_like(loss_ref)

        loss_ref[...] += 0.5 * jnp.sum(jnp.mean(e * e, axis=-1, keepdims=True))
        dy_ref[...] = e / d

    return _call(body, name=name, grid=(s // tm,), in_specs=[_rows(tm, d), _rows(tm, d)],
                 out_specs=[_const((8, V7X_LANES)), _rows(tm, d)],
                 out_shape=[SDS((8, V7X_LANES), F32), SDS((s, d), F32)], dims=("arbitrary",))(y, target)


def _adamw(w, g, m, v, name):
    r, c = w.shape
    tr = next((t for t in (512, 256, 128, 64, 32, 16, 8) if r % t == 0), r)

    def body(w_ref, g_ref, m_ref, v_ref, d_ref, mo_ref, vo_ref):
        gv = g_ref[...]
        mn = ADAM_B1 * m_ref[...] + (1.0 - ADAM_B1) * gv
        vn = ADAM_B2 * v_ref[...] + (1.0 - ADAM_B2) * (gv * gv)
        m_hat = mn / (1.0 - ADAM_B1 ** ADAM_STEP)
        v_hat = vn / (1.0 - ADAM_B2 ** ADAM_STEP)
        d_ref[...] = -ADAM_LR * (m_hat / (jnp.sqrt(v_hat) + ADAM_EPS) + ADAM_WD * w_ref[...])
        mo_ref[...] = mn
        vo_ref[...] = vn

    spec = _rows(tr, c)
    return _call(body, name=name, grid=(r // tr,), in_specs=[spec] * 4, out_specs=[spec] * 3,
                 out_shape=[SDS((r, c), F32)] * 3, dims=("parallel",))(w, g, m, v)


def _position():
    return lax.axis_index("x"), lax.axis_index("y"), lax.axis_index("c")


def _other_chips(x, y):
    return [(1 - x, y), (x, 1 - y), (1 - x, 1 - y)]


def _dev_index(px, py, pc):
    return 4 * px + 2 * py + pc


def _all_gather(wloc):
    r, d = wloc.shape

    def body(x_ref, out_ref, send_sems, recv_sems, local_sem):
        start, forward, finish = _gather_phases(x_ref, out_ref, send_sems, recv_sems, local_sem)
        start()
        forward()
        finish()

    return _call(body, name="all_gather_weights", in_specs=[ANY], out_specs=ANY,
                 out_shape=SDS((N_DEV, r, d), wloc.dtype), scratch=GATHER_SEMS)(wloc)


GATHER_SEMS = [pltpu.SemaphoreType.DMA((7,)), pltpu.SemaphoreType.DMA((7,)), pltpu.SemaphoreType.DMA((1,))]


def _gather_phases(x_ref, out_ref, send_sems, recv_sems, local_sem):
    x, y, c = _position()
    me, sibling = (x, y, c), (x, y, 1 - c)
    chips = _other_chips(x, y)

    def slot(dev):
        return out_ref.at[_dev_index(*dev)]

    def copy(k, block, to, src=None):
        return pltpu.make_async_remote_copy(
            src_ref=slot(block) if src is None else src, dst_ref=slot(block),
            send_sem=send_sems.at[k], recv_sem=recv_sems.at[k], device_id=to, device_id_type=MESH)

    mine = pltpu.make_async_copy(x_ref, slot(me), local_sem.at[0])
    first = [copy(0, me, sibling, src=x_ref)] + [copy(1 + j, me, (*chip, c), src=x_ref) for j, chip in enumerate(chips)]
    passed = [copy(4 + j, (*chip, c), sibling) for j, chip in enumerate(chips)]

    def start():
        mine.start()
        for cp in first:
            cp.start()

    def forward():
        for j, chip in enumerate(chips):
            copy(1 + j, (*chip, c), me).wait_recv()
            passed[j].start()

    def finish():
        copy(0, sibling, me).wait_recv()
        for j, chip in enumerate(chips):
            copy(4 + j, (*chip, 1 - c), me).wait_recv()
        for cp in first + passed:
            cp.wait_send()
        mine.wait()

    return start, forward, finish


def _row_block(r):
    return next(t for t in range(704, 0, -BF16_ROWS) if r % t == 0)


def _pair_exchange(gall, name):
    def body(g_ref, out_ref, send_sems, recv_sems):
        start, finish = _pair_exchange_phases(g_ref, out_ref, send_sems, recv_sems)
        start()
        finish()

    return _call(body, name=name, in_specs=[ANY], out_specs=ANY, out_shape=_pair_exchange_shape(gall),
                 scratch=PAIR_EXCHANGE_SEMS)(gall)


PAIR_EXCHANGE_SEMS = [pltpu.SemaphoreType.DMA((4,)), pltpu.SemaphoreType.DMA((4,))]


def _pair_exchange_shape(gall):
    return SDS((4,) + gall.shape[1:], gall.dtype)


def _pair_exchange_phases(g_ref, out_ref, send_sems, recv_sems):
    x, y, c = _position()
    sibling = (x, y, 1 - c)
    dests = [sibling] + [(*chip, 1 - c) for chip in _other_chips(x, y)]
    copies = [pltpu.make_async_remote_copy(
        src_ref=g_ref.at[_dev_index(*dest)], dst_ref=out_ref.at[k], send_sem=send_sems.at[k],
        recv_sem=recv_sems.at[k], device_id=sibling, device_id_type=MESH) for k, dest in enumerate(dests)]

    def start():
        for cp in copies:
            cp.start()

    def finish():
        for cp in copies:
            cp.wait()

    return start, finish


def _pair_sum(gall, sib, idx, name):
    _, r, d = gall.shape
    tr = _row_block(r)

    def body(idx_ref, a_ref, b_ref, o_ref):
        o_ref[...] = (a_ref[...].astype(F32) + b_ref[...].astype(F32)).astype(o_ref.dtype)

    grid_spec = pltpu.PrefetchScalarGridSpec(
        num_scalar_prefetch=1, grid=(4, r // tr),
        in_specs=[pl.BlockSpec((1, tr, d), lambda k, i, idx_ref: (idx_ref[k], i, 0)),
                  pl.BlockSpec((1, tr, d), lambda k, i, idx_ref: (k, i, 0))],
        out_specs=pl.BlockSpec((1, tr, d), lambda k, i, idx_ref: (k, i, 0)))
    return pl.pallas_call(body, name=name, grid_spec=grid_spec,
                          out_shape=_in_hbm(SDS((4, r, d), gall.dtype)),
                          compiler_params=pltpu.CompilerParams(dimension_semantics=("parallel", "parallel")))(
        idx, _keep_in_hbm(gall), _keep_in_hbm(sib))


def _chip_exchange(tsum):
    _, r, d = tsum.shape

    def body(t_ref, out_ref, send_sems, recv_sems):
        start, finish = _chip_exchange_phases(t_ref, out_ref, send_sems, recv_sems)
        start()
        finish()

    return _call(body, name="reduce_scatter_chip_exchange", in_specs=[ANY], out_specs=ANY,
                 out_shape=SDS((3, r, d), tsum.dtype), scratch=EXCHANGE_SEMS)(tsum)


EXCHANGE_SEMS = [pltpu.SemaphoreType.DMA((3,)), pltpu.SemaphoreType.DMA((3,))]


def _chip_exchange_phases(t_ref, out_ref, send_sems, recv_sems):
    x, y, c = _position()
    copies = [pltpu.make_async_remote_copy(
        src_ref=t_ref.at[1 + k], dst_ref=out_ref.at[k], send_sem=send_sems.at[k], recv_sem=recv_sems.at[k],
        device_id=(*chip, c), device_id_type=MESH) for k, chip in enumerate(_other_chips(x, y))]

    def start():
        for cp in copies:
            cp.start()

    def finish():
        for cp in copies:
            cp.wait()

    return start, finish


def _chip_exchange_shape(tsum):
    return SDS((3,) + tsum.shape[1:], tsum.dtype)


EXCHANGES = dict(pair=(_pair_exchange_phases, _pair_exchange_shape, PAIR_EXCHANGE_SEMS),
                 chip=(_chip_exchange_phases, _chip_exchange_shape, EXCHANGE_SEMS))


def _final_sum(tsum, rcv, name):
    _, r, d = tsum.shape
    tr = _row_block(r)

    def body(t_ref, r_ref, o_ref):
        acc = t_ref[0].astype(F32)
        for k in range(3):
            acc = acc + r_ref[k].astype(F32)
        o_ref[...] = acc

    return _call(body, name=name, grid=(r // tr,),
                 in_specs=[pl.BlockSpec((1, tr, d), lambda i: (0, i, 0)), pl.BlockSpec((3, tr, d), lambda i: (0, i, 0))],
                 out_specs=_rows(tr, d), out_shape=SDS((r, d), F32), dims=("parallel",))(tsum, rcv)


def _all_reduce_small(buf):
    nr, lanes = buf.shape

    def body(b_ref, out_ref, gath, send_sems, recv_sems):
        x, y, c = _position()
        my_slot = _dev_index(x, y, c)
        gath[my_slot] = b_ref[...]
        copies = []
        for k in range(1, N_DEV):
            dx, dy, dc = (k >> 2) & 1, (k >> 1) & 1, k & 1
            peer = (1 - x if dx else x, 1 - y if dy else y, 1 - c if dc else c)
            copies.append(pltpu.make_async_remote_copy(
                src_ref=b_ref, dst_ref=gath.at[my_slot], send_sem=send_sems.at[k - 1], recv_sem=recv_sems.at[k - 1],
                device_id=peer, device_id_type=MESH))
        for cp in copies:
            cp.start()
        for cp in copies:
            cp.wait()
        acc = gath[0]
        for sidx in range(1, N_DEV):
            acc = acc + gath[sidx]
        out_ref[...] = acc

    return _call(body, name="all_reduce_small", in_specs=[VMEM], out_specs=VMEM, out_shape=SDS((nr, lanes), F32),
                 scratch=[pltpu.VMEM((N_DEV, nr, lanes), F32), pltpu.SemaphoreType.DMA((7,)),
                          pltpu.SemaphoreType.DMA((7,))])(buf)


def _ffn_block_fwd(x, gain, wall, offs, fs, tag, gather_src=None):
    res = _ffn_fwd(x, gain, wall, offs, fs, f"{tag}_fwd", gather_src)
    out, xn, g, u, h = res[:5]
    return out, (x, gain, xn, g, u, h), (res[5] if gather_src is not None else None)


def _ffn_block_bwd(dout, saved, wall, offs, fs, tag, exchange_src=None):
    x, gain, xn, g, u, h = saved
    res = _ffn_bwd_act(dout, g, u, wall, offs, fs, f"{tag}_bwd_act", exchange_src)
    dg, du, dy_b, dxn = res[:4]
    dwg = _mm_tn(dg, xn, f"{tag}_dwg", BF16)
    dwu = _mm_tn(du, xn, f"{tag}_dwu", BF16)
    dwd = _mm_tn(h, dy_b, f"{tag}_dwd", BF16)
    dx, dgain = _rmsnorm_bwd(x, gain, dxn, dout, f"{tag}_norm_bwd")
    return dx, (dwg, dwu, dwd), dgain, (res[4] if exchange_src is not None else None)


def _local_step(x, target, wall_a, fs, small, plan):
    grads = {}
    first, second = (0, fs, 2 * fs), (3 * fs, 4 * fs, 5 * fs)
    reduces = "pair_sum" in plan

    x1, s_f1a, wall_b = _ffn_block_fwd(x, small["ffn1_norm"][0], wall_a, first, fs, "l0_ffn1", plan.get("shard_b"))
    wall_b = plan.get("wall_b", wall_b)
    mixw = plan["mix_b"](wall_b)
    hn0 = _rmsnorm_fwd(x1, small["mix_norm"][0], "l0_mix_norm")
    z = _mm(hn0, mixw["ev_w_main_t"], "nt", "ev_in_proj")
    flog = _mm(hn0, mixw["ev_w_f_t"], "nt", "ev_in_proj_gate")
    a, a1 = _conv_a_fwd(z, small["ev_conv_w32"], small["ev_conv_b"], small["ev_conv_norm"], "ev_conv_fwd")
    qs, kh, vb, fb, ft, xt = _qk_fwd(z, flog, small["ev_b_f128"], small["ev_q_norm2"], small["ev_k_norm2"], "ev_qk_fwd")
    if "wall_c" in plan:
        o, lse = _attn_fwd(qs, kh, vb, fb, ft, "ev_attn_fwd")
        wall_c = plan["wall_c"]
    else:
        o, lse, wall_c = _attn_fwd(qs, kh, vb, fb, ft, "ev_attn_fwd", gather_src=plan["shard_c"])
    mixw = {**mixw, **plan["mix_c"](wall_c)}
    ao = jnp.concatenate([a, o.astype(BF16)], axis=1)
    x2 = _mm(ao, mixw["ev_w_out"], "nn", "ev_out_proj", add=x1)
    x3, s_f2a, _ = _ffn_block_fwd(x2, small["ffn2_norm"][0], wall_c, first, fs, "l0_ffn2")

    x4, s_f1b, wall_d = _ffn_block_fwd(x3, small["ffn1_norm"][1], wall_c, second, fs, "l1_ffn1", plan.get("shard_d"))
    wall_d = plan.get("wall_d", wall_d)
    hn1 = _rmsnorm_fwd(x4, small["mix_norm"][1], "l1_mix_norm")
    zo = _mm(hn1, mixw["od_w_in_t"], "nt", "od_in_proj")
    y0 = _conv_c_fwd(zo, small["od_conv_w8"], "od_conv_fwd")
    x5 = _mm(y0, mixw["od_w_out"], "nn", "od_out_proj", add=x4)
    x6, s_f2b, _ = _ffn_block_fwd(x5, small["ffn2_norm"][1], wall_d, first, fs, "l1_ffn2")

    loss, d6 = _loss_head(x6, target, "loss_head")

    d5, grads["l1_ffn2"], grads["ffn2_norm_1"], _ = _ffn_block_bwd(d6, s_f2b, wall_d, first, fs, "l1_ffn2")
    d5b = d5.astype(BF16)
    dy0 = _mm(d5b, mixw["od_w_out"], "nt", "od_out_proj_bwd")
    grads["od_w_out"] = _mm_tn(y0, d5b, "od_dw_out", BF16)
    dzo, grads["od_conv_w"] = _conv_c_bwd(dy0, zo, small["od_conv_w8"], "od_conv_bwd")
    dh1 = _mm(dzo, mixw["od_w_in_t"], "nn", "od_in_proj_bwd")
    grads["od_w_in_t"] = _mm_tn(dzo, hn1, "od_dw_in", BF16)
    d4, grads["mix_norm_1"] = _rmsnorm_bwd(x4, small["mix_norm"][1], dh1, d5, "l1_mix_norm_bwd")
    d3, grads["l1_ffn1"], grads["ffn1_norm_1"], _ = _ffn_block_bwd(d4, s_f1b, wall_c, second, fs, "l1_ffn1")

    d2, grads["l0_ffn2"], grads["ffn2_norm_0"], _ = _ffn_block_bwd(d3, s_f2a, wall_c, first, fs, "l0_ffn2")
    partials_c = plan["partials_c"](grads) if reduces else None
    d2b = d2.astype(BF16)
    dao = _mm(d2b, mixw["ev_w_out"], "nt", "ev_out_proj_bwd")
    grads["ev_w_out"] = _mm_tn(ao, d2b, "ev_dw_out", BF16)
    da1, grads["ev_conv_norm"], grads["ev_conv_b"] = _conv_a_bwd_norm(dao, a1, small["ev_conv_norm"], "ev_conv_bwd_norm")
    res = _conv_a_bwd_conv(da1, z, small["ev_conv_w32"], "ev_conv_bwd_conv",
                           exchange_src=("pair", partials_c) if reduces else None)
    du, dg, grads["ev_conv_w"] = res[:3]
    sums_c = plan["pair_sum"](partials_c, res[3], "c") if reduces else None
    res = _attn_bwd(qs, kh, vb, fb, ft, lse, o, dao, "ev_attn_bwd", exchange_src=sums_c)
    dqs, rs, dkh, dv, df4 = res[:5]
    if reduces:
        grads["pair_sums_c"], grads["exchanged_c"] = sums_c, res[5]
    dq, dk, dvb, grads["ev_q_norm"], grads["ev_k_norm"] = _qk_bwd(
        z, dqs, dkh, dv, small["ev_q_norm2"], small["ev_k_norm2"], "ev_qk_bwd")
    dft = df4[:, 0:2, :].reshape(N_HEADS, -1) + rs.reshape(-1, N_HEADS, HEAD_DIM)[:, :, 0].T
    dxt, dflog, grads["ev_b_f"] = _gate_bwd(dft, xt, "ev_gate_bwd")
    dz = jnp.concatenate([du, dg, dq, dk, dvb], axis=1)
    dflog_b = dflog.astype(BF16)
    dh0 = _mm(dz, mixw["ev_w_main_t"], "nn", "ev_in_proj_bwd")
    dh0 = _mm(dflog_b, mixw["ev_w_f_t"], "nn", "ev_in_proj_gate_bwd", add=dh0)
    dw_main = _mm_tn(dz, hn0, "ev_dw_in", BF16)
    dw_f = _mm(dxt.astype(BF16), hn0, "nn", "ev_dw_in_gate", BF16)
    grads["ev_w_in_t"] = jnp.concatenate([dw_main, dw_f], axis=0)
    d1, grads["mix_norm_0"] = _rmsnorm_bwd(x1, small["mix_norm"][0], dh0, d2, "l0_mix_norm_bwd")
    sums_b = None
    if reduces:
        partials_b = plan["partials_b"](grads)
        sums_b = plan["pair_sum"](partials_b, _pair_exchange(partials_b, "reduce_scatter_pair_exchange_b"), "b")
    d0, grads["l0_ffn1"], grads["ffn1_norm_0"], exchanged_b = _ffn_block_bwd(
        d1, s_f1a, wall_a, first, fs, "l0_ffn1", exchange_src=("chip", sums_b) if reduces else None)
    if reduces:
        grads["pair_sums_b"], grads["exchanged_b"] = sums_b, exchanged_b
    return loss, d0, grads


def _round_up(n, m):
    return -(-n // m) * m


def _pad_rows(a, rows):
    return jnp.pad(a, ((0, rows - a.shape[0]), (0, 0)))


SMALL_ORDER = ("loss", "ffn1_norm", "mix_norm", "ffn2_norm", "ev_b_f", "ev_conv_b", "ev_conv_norm",
               "ev_q_norm", "ev_k_norm", "ev_conv_w", "od_conv_w")


def _pack_small(parts):
    flat = jnp.concatenate([parts[k].reshape(-1).astype(F32) for k in SMALL_ORDER])
    n = _round_up(flat.shape[0], 8 * V7X_LANES)
    return jnp.pad(flat, (0, n - flat.shape[0])).reshape(-1, V7X_LANES)


def _unpack_small(buf, shapes):
    flat = buf.reshape(-1)
    out, pos = {}, 0
    for k in SMALL_ORDER:
        n = math.prod(shapes[k])
        out[k] = flat[pos:pos + n].reshape(shapes[k])
        pos += n
    return out


def kernel(x, ffn1_norm, ffn1_w_gate, ffn1_w_up, ffn1_w_down, mix_norm, ffn2_norm, ffn2_w_gate, ffn2_w_up, ffn2_w_down, ev_w_in, ev_b_f, ev_conv_w, ev_conv_b, ev_conv_norm, ev_q_norm, ev_k_norm, ev_w_out, od_w_in, od_conv_w, od_w_out, loss_target, m_ffn1_norm, m_ffn1_w_gate, m_ffn1_w_up, m_ffn1_w_down, m_mix_norm, m_ffn2_norm, m_ffn2_w_gate, m_ffn2_w_up, m_ffn2_w_down, m_ev_w_in, m_ev_b_f, m_ev_conv_w, m_ev_conv_b, m_ev_conv_norm, m_ev_q_norm, m_ev_k_norm, m_ev_w_out, m_od_w_in, m_od_conv_w, m_od_w_out, v_ffn1_norm, v_ffn1_w_gate, v_ffn1_w_up, v_ffn1_w_down, v_mix_norm, v_ffn2_norm, v_ffn2_w_gate, v_ffn2_w_up, v_ffn2_w_down, v_ev_w_in, v_ev_b_f, v_ev_conv_w, v_ev_conv_b, v_ev_conv_norm, v_ev_q_norm, v_ev_k_norm, v_ev_w_out, v_od_w_in, v_od_conv_w, v_od_w_out):
    weights = dict(ffn1_norm=ffn1_norm, ffn1_w_gate=ffn1_w_gate, ffn1_w_up=ffn1_w_up, ffn1_w_down=ffn1_w_down,
                   mix_norm=mix_norm, ffn2_norm=ffn2_norm, ffn2_w_gate=ffn2_w_gate, ffn2_w_up=ffn2_w_up,
                   ffn2_w_down=ffn2_w_down, ev_w_in=ev_w_in, ev_b_f=ev_b_f, ev_conv_w=ev_conv_w, ev_conv_b=ev_conv_b,
                   ev_conv_norm=ev_conv_norm, ev_q_norm=ev_q_norm, ev_k_norm=ev_k_norm, ev_w_out=ev_w_out,
                   od_w_in=od_w_in, od_conv_w=od_conv_w, od_w_out=od_w_out)
    m_in = dict(ffn1_norm=m_ffn1_norm, ffn1_w_gate=m_ffn1_w_gate, ffn1_w_up=m_ffn1_w_up, ffn1_w_down=m_ffn1_w_down,
                mix_norm=m_mix_norm, ffn2_norm=m_ffn2_norm, ffn2_w_gate=m_ffn2_w_gate, ffn2_w_up=m_ffn2_w_up,
                ffn2_w_down=m_ffn2_w_down, ev_w_in=m_ev_w_in, ev_b_f=m_ev_b_f, ev_conv_w=m_ev_conv_w,
                ev_conv_b=m_ev_conv_b, ev_conv_norm=m_ev_conv_norm, ev_q_norm=m_ev_q_norm, ev_k_norm=m_ev_k_norm,
                ev_w_out=m_ev_w_out, od_w_in=m_od_w_in, od_conv_w=m_od_conv_w, od_w_out=m_od_w_out)
    v_in = dict(ffn1_norm=v_ffn1_norm, ffn1_w_gate=v_ffn1_w_gate, ffn1_w_up=v_ffn1_w_up, ffn1_w_down=v_ffn1_w_down,
                mix_norm=v_mix_norm, ffn2_norm=v_ffn2_norm, ffn2_w_gate=v_ffn2_w_gate, ffn2_w_up=v_ffn2_w_up,
                ffn2_w_down=v_ffn2_w_down, ev_w_in=v_ev_w_in, ev_b_f=v_ev_b_f, ev_conv_w=v_ev_conv_w,
                ev_conv_b=v_ev_conv_b, ev_conv_norm=v_ev_conv_norm, ev_q_norm=v_ev_q_norm, ev_k_norm=v_ev_k_norm,
                ev_w_out=v_ev_w_out, od_w_in=v_od_w_in, od_conv_w=v_od_conv_w, od_w_out=v_od_w_out)
    order = list(weights)

    d = x.shape[-1]
    fs = ffn1_w_gate.shape[2]
    n_in = ev_w_in.shape[2]
    n_in_pad = _round_up(n_in, BF16_ROWS)
    n_out = ev_w_out.shape[1]
    n_od = od_w_in.shape[2]
    d_conv = ev_conv_b.shape[1]
    d_in_even = n_in * N_DEV
    d_main = d_in_even - N_HEADS
    cx, cy, cc = _position()
    me = _dev_index(cx, cy, cc)

    def block(wg, wu, wd, layer):
        return [wg[layer].T, wu[layer].T, wd[layer]]

    def stack(parts):
        return jnp.concatenate([p.astype(BF16) for p in parts], axis=0)

    ffn1, ffn2 = (ffn1_w_gate, ffn1_w_up, ffn1_w_down), (ffn2_w_gate, ffn2_w_up, ffn2_w_down)
    shard_a = stack(block(*ffn1, 0))
    shard_b = stack([_pad_rows(ev_w_in[0].T, n_in_pad), ev_w_out[0]])
    shard_c = stack(block(*ffn2, 0) + block(*ffn1, 1) + [od_w_in[0].T, od_w_out[0]])
    shard_d = stack(block(*ffn2, 1))
    off_ev_in, off_ev_out = 0, n_in_pad
    off_od_in, off_od_out = 6 * fs, 6 * fs + n_od
    wall_a = _all_gather(shard_a)

    def even_mixer_weights(wall_b):
        ev_w_in_t = wall_b[:, off_ev_in:off_ev_in + n_in, :].reshape(d_in_even, d)
        return dict(ev_w_main_t=ev_w_in_t[:d_main], ev_w_f_t=_pad_rows(ev_w_in_t[d_main:], V7X_LANES),
                    ev_w_out=wall_b[:, off_ev_out:off_ev_out + n_out, :].reshape(N_DEV * n_out, d))

    def odd_mixer_weights(wall_c):
        return dict(od_w_in_t=wall_c[:, off_od_in:off_od_in + n_od, :].reshape(N_DEV * n_od, d),
                    od_w_out=wall_c[:, off_od_out:off_od_out + n_out, :].reshape(N_DEV * n_out, d))

    def by_dev(a, rows, pad_to=None):
        a = a.reshape(N_DEV, rows, d)
        return a if pad_to is None else jnp.pad(a, ((0, 0), (0, pad_to - rows), (0, 0)))

    idx = jnp.stack([me] + [_dev_index(*chip, cc) for chip in _other_chips(cx, cy)]).astype(jnp.int32)

    def pair_sum(partials, from_sibling, tag):
        return _pair_sum(partials, from_sibling, idx, f"reduce_scatter_pair_sum_{tag}")

    def ffn_pieces(g, key):
        return [by_dev(t, fs) for t in g[key]]

    conv_shapes = dict(ev_conv_w=(CONV_A_WIDTH, d_conv), od_conv_w=(CONV_C_WIDTH, d))
    zero_small = {k: jnp.zeros(s_, F32) for k, s_ in conv_shapes.items()}
    ev_cw_part = lax.dynamic_update_slice(zero_small["ev_conv_w"], ev_conv_w[0], (0, me * ev_conv_w.shape[2]))
    od_cw_part = lax.dynamic_update_slice(zero_small["od_conv_w"], od_conv_w[0], (0, me * od_conv_w.shape[2]))
    zeros_like_small = {k: jnp.zeros((1,), F32) for k in SMALL_ORDER}
    taps = _unpack_small(_all_reduce_small(_pack_small({**zeros_like_small, "ev_conv_w": ev_cw_part,
                                                        "od_conv_w": od_cw_part})),
                         {**{k: (1,) for k in SMALL_ORDER}, **conv_shapes})
    small = dict(
        ffn1_norm=[ffn1_norm[l][None] for l in range(2)], mix_norm=[mix_norm[l][None] for l in range(2)],
        ffn2_norm=[ffn2_norm[l][None] for l in range(2)],
        ev_conv_w32=_pad_rows(taps["ev_conv_w"], CONV_A_WIDTH + 1), ev_conv_b=ev_conv_b, ev_conv_norm=ev_conv_norm,
        ev_b_f128=jnp.pad(ev_b_f, ((0, 0), (0, V7X_LANES - N_HEADS))),
        ev_q_norm2=jnp.tile(ev_q_norm, (1, 2)), ev_k_norm2=jnp.tile(ev_k_norm, (1, 2)),
        od_conv_w8=_pad_rows(taps["od_conv_w"], 8),
    )

    plan = dict(
        shard_b=shard_b, shard_c=shard_c, shard_d=shard_d, mix_b=even_mixer_weights, mix_c=odd_mixer_weights, pair_sum=pair_sum,
        partials_c=lambda g1: jnp.concatenate(
            ffn_pieces(g1, "l0_ffn2") + ffn_pieces(g1, "l1_ffn1") + ffn_pieces(g1, "l1_ffn2")
            + [by_dev(g1["od_w_in_t"], n_od), by_dev(g1["od_w_out"], n_out)], axis=1),
        partials_b=lambda g1: jnp.concatenate(
            [by_dev(g1["ev_w_in_t"], n_in, n_in_pad), by_dev(g1["ev_w_out"], n_out)], axis=1))
    loss_p, grad_x, g = _local_step(x[0], loss_target[0], wall_a, fs, small, plan)

    partials_a = jnp.concatenate(ffn_pieces(g, "l0_ffn1"), axis=1)
    sums_a = pair_sum(partials_a, _pair_exchange(partials_a, "reduce_scatter_pair_exchange_a"), "a")
    gsum_a = _final_sum(sums_a, _chip_exchange(sums_a), "reduce_scatter_final_sum_a")
    gsum_b = _final_sum(g["pair_sums_b"], g["exchanged_b"], "reduce_scatter_final_sum_b")
    gsum_c = _final_sum(g["pair_sums_c"], g["exchanged_c"], "reduce_scatter_final_sum_c")

    grad = {}
    where = dict(ffn1=((gsum_a, 0), (gsum_c, 3 * fs)), ffn2=((gsum_c, 0), (gsum_c, 6 * fs)))
    for blk, places in where.items():
        for wi, kind in enumerate(("gate", "up", "down")):
            rows = [buf[off + wi * fs:off + (wi + 1) * fs] for buf, off in places]
            grad[f"{blk}_w_{kind}"] = jnp.stack(rows if kind == "down" else [r.T for r in rows])
    grad["ev_w_in"] = gsum_b[off_ev_in:off_ev_in + n_in].T[None]
    grad["ev_w_out"] = gsum_b[off_ev_out:off_ev_out + n_out][None]
    grad["od_w_in"] = gsum_c[9 * fs:9 * fs + n_od].T[None]
    grad["od_w_out"] = gsum_c[9 * fs + n_od:9 * fs + n_od + n_out][None]

    heads = lambda t: t.reshape(N_HEADS, HEAD_DIM).sum(axis=0)
    parts = dict(
        loss=loss_p[0, 0:1],
        ffn1_norm=jnp.stack([g["ffn1_norm_0"][0], g["ffn1_norm_1"][0]]),
        mix_norm=jnp.stack([g["mix_norm_0"][0], g["mix_norm_1"][0]]),
        ffn2_norm=jnp.stack([g["ffn2_norm_0"][0], g["ffn2_norm_1"][0]]),
        ev_b_f=g["ev_b_f"][:, 0], ev_conv_b=g["ev_conv_b"], ev_conv_norm=g["ev_conv_norm"],
        ev_q_norm=heads(g["ev_q_norm"]), ev_k_norm=heads(g["ev_k_norm"]),
        ev_conv_w=g["ev_conv_w"][:CONV_A_WIDTH], od_conv_w=g["od_conv_w"][:CONV_C_WIDTH])
    small_shapes = dict(loss=(1,), ffn1_norm=ffn1_norm.shape, mix_norm=mix_norm.shape, ffn2_norm=ffn2_norm.shape,
                        ev_b_f=ev_b_f.shape, ev_conv_b=ev_conv_b.shape, ev_conv_norm=ev_conv_norm.shape,
                        ev_q_norm=ev_q_norm.shape, ev_k_norm=ev_k_norm.shape, **conv_shapes)
    red = _unpack_small(_all_reduce_small(_pack_small(parts)), small_shapes)
    loss = red["loss"][0]
    for k in ("ffn1_norm", "mix_norm", "ffn2_norm", "ev_b_f", "ev_conv_b", "ev_conv_norm", "ev_q_norm", "ev_k_norm"):
        grad[k] = red[k]
    grad["ev_conv_w"] = lax.dynamic_slice(red["ev_conv_w"], (0, me * ev_conv_w.shape[2]),
                                          (CONV_A_WIDTH, ev_conv_w.shape[2]))[None]
    grad["od_conv_w"] = lax.dynamic_slice(red["od_conv_w"], (0, me * od_conv_w.shape[2]),
                                          (CONV_C_WIDTH, od_conv_w.shape[2]))[None]

    big = ("ffn1_w_gate", "ffn1_w_up", "ffn1_w_down", "ffn2_w_gate", "ffn2_w_up", "ffn2_w_down",
           "ev_w_in", "ev_w_out", "od_w_in", "od_w_out")
    delta, new_m, new_v = {}, {}, {}
    for k in big:
        shp = weights[k].shape
        flat = lambda t: t.reshape(-1, shp[-1])
        dk, mk, vk = _adamw(flat(weights[k]), flat(grad[k]), flat(m_in[k]), flat(v_in[k]), f"adamw_{k}")
        delta[k], new_m[k], new_v[k] = dk.reshape(shp), mk.reshape(shp), vk.reshape(shp)
    rest = [k for k in order if k not in big]
    cat = lambda src: jnp.concatenate([src[k].reshape(-1) for k in rest])
    n_small = sum(math.prod(weights[k].shape) for k in rest)
    n_pad = _round_up(n_small, 8 * V7X_LANES)
    as_rows = lambda t: jnp.pad(t, (0, n_pad - n_small)).reshape(-1, V7X_LANES)
    v_rows = jnp.pad(cat(v_in), (0, n_pad - n_small), constant_values=1.0).reshape(-1, V7X_LANES)
    ds, ms, vs = _adamw(as_rows(cat(weights)), as_rows(cat(grad)), as_rows(cat(m_in)), v_rows, "adamw_small")
    pos = 0
    for k in rest:
        n = math.prod(weights[k].shape)
        for dst, src in ((delta, ds), (new_m, ms), (new_v, vs)):
            dst[k] = src.reshape(-1)[pos:pos + n].reshape(weights[k].shape)
        pos += n

    return (loss, grad_x[None], *[grad[k] for k in order], *[delta[k] for k in order],
            *[new_m[k] for k in order], *[new_v[k] for k in order])
```

```python
import math

import jax
import jax.numpy as jnp
from jax import lax
from jax.experimental import pallas as pl
from jax.experimental.pallas import tpu as pltpu

F32 = jnp.float32
BF16 = jnp.bfloat16
SDS = jax.ShapeDtypeStruct
MESH = pl.DeviceIdType.MESH

N_DEV = 8
EPS = 1e-6
FFN_RES = 0.5
HEAD_DIM = 64
N_HEADS = 8
D_ATTN = N_HEADS * HEAD_DIM
N_PAIRS = N_HEADS // 2
PAIR = 2 * HEAD_DIM
ATTN_SCALE = 1.0 / math.sqrt(HEAD_DIM)
CONV_A_WIDTH = 31
CONV_A_HALO = 32
CONV_C_WIDTH = 3
CONV_C_HALO = 8
NEG_BIG = -1e30
ADAM_LR, ADAM_B1, ADAM_B2, ADAM_EPS, ADAM_WD, ADAM_STEP = 0.001, 0.9, 0.999, 1e-08, 0.01, 10

V7X_VMEM_BYTES = 64 * 1024 * 1024
V7X_LANES = 128
BF16_ROWS = 16
MIB = 1024 * 1024

NT = (((1,), (1,)), ((), ()))
TN = (((0,), (0,)), ((), ()))


def _call(body, *, name, out_shape, in_specs, out_specs, grid=(), scratch=(), dims=None, vmem_mb=32, **kw):
    params = dict(vmem_limit_bytes=min(vmem_mb * MIB, V7X_VMEM_BYTES - 4 * MIB))
    if dims is not None:
        params["dimension_semantics"] = dims
    call = pl.pallas_call(
        body, name=name, grid=grid, in_specs=in_specs, out_specs=out_specs, out_shape=_in_hbm(out_shape),
        scratch_shapes=list(scratch), compiler_params=pltpu.CompilerParams(**params), **kw)
    return lambda *args: call(*[_keep_in_hbm(a) for a in args])


LARGE_OPERAND_BYTES = MIB


def _is_large(a):
    return a.ndim >= 2 and math.prod(a.shape) * jnp.dtype(a.dtype).itemsize >= LARGE_OPERAND_BYTES


def _keep_in_hbm(a):
    return pltpu.with_memory_space_constraint(a, pltpu.HBM) if _is_large(a) else a


def _in_hbm(out_shape):
    one = lambda s: pltpu.HBM(s.shape, s.dtype) if _is_large(s) else s
    return [one(s) for s in out_shape] if isinstance(out_shape, (list, tuple)) else one(out_shape)


def _tile(n, want=512):
    return want if n % want == 0 else n


def _rows(tm, d, col=0):
    return pl.BlockSpec((tm, d), lambda i: (i, col))


def _const(shape):
    return pl.BlockSpec(shape, lambda *_: (0,) * len(shape))


ANY = pl.BlockSpec(memory_space=pl.ANY)
VMEM = pl.BlockSpec(memory_space=pltpu.VMEM)


def _sigmoid(x):
    return 1.0 / (1.0 + jnp.exp(-x))


def _rmsnorm_fwd(x, gain, name):
    s, d = x.shape
    tm = _tile(s)

    def body(x_ref, g_ref, o_ref):
        xv = x_ref[...]
        r = lax.rsqrt(jnp.mean(xv * xv, axis=-1, keepdims=True) + EPS)
        o_ref[...] = (xv * r * g_ref[...]).astype(BF16)

    return _call(body, name=name, grid=(s // tm,), in_specs=[_rows(tm, d), _const((1, d))],
                 out_specs=_rows(tm, d), out_shape=SDS((s, d), BF16), dims=("parallel",))(x, gain)


def _rmsnorm_bwd(x, gain, dxn, dres, name):
    s, d = x.shape
    tm = _tile(s)

    def body(x_ref, g_ref, dxn_ref, dres_ref, dx_ref, dg_ref):
        xv = x_ref[...]
        r = lax.rsqrt(jnp.mean(xv * xv, axis=-1, keepdims=True) + EPS)
        xh = xv * r
        dv = dxn_ref[...]

        @pl.when(pl.program_id(0) == 0)
        def _():
            dg_ref[...] = jnp.zeros_like(dg_ref)

        dg_ref[...] += jnp.sum(dv * xh, axis=0, keepdims=True)
        dxh = dv * g_ref[...]
        dx_ref[...] = dres_ref[...] + r * (dxh - xh * jnp.mean(dxh * xh, axis=-1, keepdims=True))

    return _call(body, name=name, grid=(s // tm,),
                 in_specs=[_rows(tm, d), _const((1, d)), _rows(tm, d), _rows(tm, d)],
                 out_specs=[_rows(tm, d), _const((1, d))],
                 out_shape=[SDS((s, d), F32), SDS((1, d), F32)], dims=("arbitrary",))(x, gain, dxn, dres)


def _col_tile(n):
    for t in (1536, 1280, 1024, 768, 512, 256, 128):
        if n % t == 0:
            return t
    return n


def _mm(a, b, mode, name, out_dtype=F32, add=None):
    if mode == "tn":
        k, m = a.shape
        n = b.shape[1]
        bm = next((t for t in range(768, 0, -V7X_LANES) if m % t == 0), m)

        def body_tn(a_ref, b_ref, o_ref):
            o_ref[...] = lax.dot_general(a_ref[...].astype(BF16), b_ref[...].astype(BF16), TN,
                                         preferred_element_type=F32).astype(out_dtype)

        return _call(body_tn, name=name, grid=(m // bm,),
                     in_specs=[pl.BlockSpec((k, bm), lambda i: (0, i)), _const((k, n))],
                     out_specs=pl.BlockSpec((bm, n), lambda i: (i, 0)),
                     out_shape=SDS((m, n), out_dtype), dims=("parallel",), vmem_mb=48)(a, b)
    m, k = a.shape
    n = b.shape[0] if mode == "nt" else b.shape[1]
    tm, tn = _tile(m), _col_tile(n)
    dn = NT if mode == "nt" else (((1,), (0,)), ((), ()))

    def body(a_ref, b_ref, *rest):
        o_ref = rest[-1]
        acc = lax.dot_general(a_ref[...].astype(BF16), b_ref[...].astype(BF16), dn, preferred_element_type=F32)
        if add is not None:
            acc = acc + rest[0][...]
        o_ref[...] = acc.astype(out_dtype)

    b_spec = (pl.BlockSpec((tn, k), lambda i, j: (j, 0)) if mode == "nt"
              else pl.BlockSpec((k, tn), lambda i, j: (0, j)))
    in_specs = [pl.BlockSpec((tm, k), lambda i, j: (i, 0)), b_spec]
    args = [a, b]
    if add is not None:
        in_specs.append(pl.BlockSpec((tm, tn), lambda i, j: (i, j)))
        args.append(add)
    return _call(body, name=name, grid=(m // tm, n // tn), in_specs=in_specs,
                 out_specs=pl.BlockSpec((tm, tn), lambda i, j: (i, j)),
                 out_shape=SDS((m, n), out_dtype), dims=("parallel", "parallel"), vmem_mb=48)(*args)


def _mm_tn(a, b, name, out_dtype=F32):
    return _mm(a, b, "tn", name, out_dtype)


FFN_TM = 256
FFN_FWD_TM = 512
FFN_CHUNK = 256


def _load_ffn_weights(w_hbm, offs, fs, dsts, sems):
    copies = []
    for wi, (off, dst) in enumerate(zip(offs, dsts)):
        for j in range(N_DEV):
            cp = pltpu.make_async_copy(w_hbm.at[j, pl.ds(off, fs), :], dst.at[pl.ds(j * fs, fs), :],
                                       sems.at[wi * N_DEV + j])
            cp.start()
            copies.append(cp)
    for cp in copies:
        cp.wait()


def _ffn_fwd(x, gain, wall, offs, fs, name, gather_src=None):
    s, d = x.shape
    f = fs * N_DEV
    tm, ch = _tile(s, FFN_FWD_TM), FFN_CHUNK
    n = s // tm
    gathers = gather_src is not None

    def body(x_ref, gain_ref, w_hbm, *rest):
        if gathers:
            (src_hbm, out_ref, xn_ref, g_ref, u_ref, h_ref, gathered, wg_s, wu_s, wd_s, sems,
             send_sems, recv_sems, local_sem) = rest
            start, forward, finish = _gather_phases(src_hbm, gathered, send_sems, recv_sems, local_sem)
            pl.when(pl.program_id(0) == 0)(start)
            pl.when(pl.program_id(0) == (7 * n) // 8)(forward)
        else:
            out_ref, xn_ref, g_ref, u_ref, h_ref, wg_s, wu_s, wd_s, sems = rest

        @pl.when(pl.program_id(0) == 0)
        def _():
            _load_ffn_weights(w_hbm, offs, fs, (wg_s, wu_s, wd_s), sems)

        xv = x_ref[...]
        xnv = (xv * lax.rsqrt(jnp.mean(xv * xv, axis=-1, keepdims=True) + EPS) * gain_ref[...]).astype(BF16)
        xn_ref[...] = xnv
        acc = jnp.zeros((tm, d), F32)
        for c in range(f // ch):
            sl = slice(c * ch, (c + 1) * ch)
            gb = lax.dot_general(xnv, wg_s[sl, :], NT, preferred_element_type=F32).astype(BF16)
            ub = lax.dot_general(xnv, wu_s[sl, :], NT, preferred_element_type=F32).astype(BF16)
            g_ref[:, sl] = gb
            u_ref[:, sl] = ub
            g = gb.astype(F32)
            hb = (g * _sigmoid(g) * ub.astype(F32)).astype(BF16)
            h_ref[:, sl] = hb
            acc = acc + jnp.dot(hb, wd_s[sl, :], preferred_element_type=F32)
        out_ref[...] = xv + FFN_RES * acc
        if gathers:
            pl.when(pl.program_id(0) == n - 1)(finish)

    in_specs, args = [_rows(tm, d), _const((1, d)), ANY], [x, gain, wall]
    out_specs = [_rows(tm, d), _rows(tm, d), _rows(tm, f), _rows(tm, f), _rows(tm, f)]
    out_shape = [SDS((s, d), F32), SDS((s, d), BF16), SDS((s, f), BF16), SDS((s, f), BF16), SDS((s, f), BF16)]
    scratch = [pltpu.VMEM((f, d), BF16)] * 3 + [pltpu.SemaphoreType.DMA((3 * N_DEV,))]
    if gathers:
        in_specs.append(ANY)
        args.append(gather_src)
        out_specs.append(ANY)
        out_shape.append(SDS((N_DEV,) + gather_src.shape, gather_src.dtype))
        scratch += GATHER_SEMS
    return _call(body, name=name, grid=(n,), in_specs=in_specs, out_specs=out_specs, out_shape=out_shape,
                 scratch=scratch, dims=("arbitrary",), vmem_mb=56)(*args)


def _ffn_bwd_act(dout, g, u, x, gain, wall, offs, fs, name, exchange_src=None):
    s, d = dout.shape
    f = fs * N_DEV
    tm, ch = _tile(s, FFN_TM), FFN_CHUNK
    n = s // tm
    exchanges = exchange_src is not None
    if exchanges:
        phases_of, shape_of, exchange_sems = EXCHANGES[exchange_src[0]]

    def body(dout_ref, g_ref, u_ref, x_ref, gain_ref, w_hbm, *rest):
        if exchanges:
            (t_hbm, dg_ref, du_ref, dy_ref, dx_ref, dgain_ref, rcv_ref, wg_s, wu_s, wd_s, sems,
             send_sems, recv_sems) = rest
            start, finish = phases_of(t_hbm, rcv_ref, send_sems, recv_sems)
            pl.when(pl.program_id(0) == 0)(start)
        else:
            dg_ref, du_ref, dy_ref, dx_ref, dgain_ref, wg_s, wu_s, wd_s, sems = rest

        @pl.when(pl.program_id(0) == 0)
        def _():
            _load_ffn_weights(w_hbm, offs, fs, (wg_s, wu_s, wd_s), sems)
            dgain_ref[...] = jnp.zeros_like(dgain_ref)

        doutv = dout_ref[...]
        dy = (FFN_RES * doutv).astype(BF16)
        dy_ref[...] = dy
        acc = jnp.zeros((tm, d), F32)
        for c in range(f // ch):
            sl = slice(c * ch, (c + 1) * ch)
            dh = lax.dot_general(dy, wd_s[sl, :], NT, preferred_element_type=F32)
            gv = g_ref[:, sl].astype(F32)
            uv = u_ref[:, sl].astype(F32)
            sg = _sigmoid(gv)
            dgb = (dh * uv * sg * (1.0 + gv * (1.0 - sg))).astype(BF16)
            dub = (dh * gv * sg).astype(BF16)
            dg_ref[:, sl] = dgb
            du_ref[:, sl] = dub
            acc = acc + jnp.dot(dgb, wg_s[sl, :], preferred_element_type=F32)
            acc = acc + jnp.dot(dub, wu_s[sl, :], preferred_element_type=F32)
        xv = x_ref[...]
        r = lax.rsqrt(jnp.mean(xv * xv, axis=-1, keepdims=True) + EPS)
        xh = xv * r
        dgain_ref[...] += jnp.sum(acc * xh, axis=0, keepdims=True)
        dxh = acc * gain_ref[...]
        dx_ref[...] = doutv + r * (dxh - xh * jnp.mean(dxh * xh, axis=-1, keepdims=True))
        if exchanges:
            pl.when(pl.program_id(0) == n - 1)(finish)

    in_specs = [_rows(tm, d), _rows(tm, f), _rows(tm, f), _rows(tm, d), _const((1, d)), ANY]
    args = [dout, g, u, x, gain, wall]
    out_specs = [_rows(tm, f), _rows(tm, f), _rows(tm, d), _rows(tm, d), _const((1, d))]
    out_shape = [SDS((s, f), BF16), SDS((s, f), BF16), SDS((s, d), BF16), SDS((s, d), F32), SDS((1, d), F32)]
    scratch = [pltpu.VMEM((f, d), BF16)] * 3 + [pltpu.SemaphoreType.DMA((3 * N_DEV,))]
    if exchanges:
        in_specs.append(ANY)
        args.append(exchange_src[1])
        out_specs.append(ANY)
        out_shape.append(shape_of(exchange_src[1]))
        scratch += exchange_sems
    return _call(body, name=name, grid=(n,), in_specs=in_specs, out_specs=out_specs, out_shape=out_shape,
                 scratch=scratch, dims=("arbitrary",), vmem_mb=56)(*args)


def _prev_rows(halo, tm, c, col):
    return pl.BlockSpec((halo, c), lambda i: (jnp.maximum(i * (tm // halo) - 1, 0), col))


def _next_rows(halo, tm, c, col, n_blocks):
    return pl.BlockSpec((halo, c), lambda i: (jnp.minimum((i + 1) * (tm // halo), n_blocks - 1), col))


def _conv_a_fwd(z, cw, cb, cn, name):
    s = z.shape[0]
    c = cb.shape[1]
    tm, halo, kw = _tile(s), CONV_A_HALO, CONV_A_WIDTH

    def body(u_ref, g_ref, up_ref, gp_ref, cw_ref, cb_ref, cn_ref, a_ref, a1_ref, buf):
        i = pl.program_id(0)
        buf[0:halo, :] = jnp.where(i > 0, up_ref[...] * _sigmoid(gp_ref[...]), 0.0)
        buf[halo:halo + tm, :] = u_ref[...] * _sigmoid(g_ref[...])
        acc = jnp.zeros((tm, c), F32)
        for k in range(kw):
            acc = acc + cw_ref[k:k + 1, :] * buf[pl.ds(halo - (kw - 1) + k, tm), :]
        a1 = acc + cb_ref[...]
        a1_ref[...] = a1
        a2 = a1 * lax.rsqrt(jnp.mean(a1 * a1, axis=-1, keepdims=True) + EPS) * cn_ref[...]
        a_ref[...] = (a2 * _sigmoid(a2)).astype(BF16)

    return _call(body, name=name, grid=(s // tm,),
                 in_specs=[_rows(tm, c, 0), _rows(tm, c, 1), _prev_rows(halo, tm, c, 0), _prev_rows(halo, tm, c, 1),
                           _const(cw.shape), _const((1, c)), _const((1, c))],
                 out_specs=[_rows(tm, c), _rows(tm, c)],
                 out_shape=[SDS((s, c), BF16), SDS((s, c), F32)],
                 scratch=[pltpu.VMEM((tm + halo, c), F32)], dims=("parallel",))(z, z, z, z, cw, cb, cn)


def _conv_a_bwd_norm(dao, a1, cn, name):
    s, c = a1.shape
    tm = _tile(s)

    def body(da_ref, a1_ref, cn_ref, da1_ref, dcn_ref, dcb_ref):
        a1v = a1_ref[...]
        r = lax.rsqrt(jnp.mean(a1v * a1v, axis=-1, keepdims=True) + EPS)
        xh = a1v * r
        a2 = xh * cn_ref[...]
        sg = _sigmoid(a2)
        da2 = da_ref[...] * sg * (1.0 + a2 * (1.0 - sg))
        dxh = da2 * cn_ref[...]
        da1 = r * (dxh - xh * jnp.mean(dxh * xh, axis=-1, keepdims=True))
        da1_ref[...] = da1

        @pl.when(pl.program_id(0) == 0)
        def _():
            dcn_ref[...] = jnp.zeros_like(dcn_ref)
            dcb_ref[...] = jnp.zeros_like(dcb_ref)

        dcn_ref[...] += jnp.sum(da2 * xh, axis=0, keepdims=True)
        dcb_ref[...] += jnp.sum(da1, axis=0, keepdims=True)

    return _call(body, name=name, grid=(s // tm,),
                 in_specs=[_rows(tm, c, 0), _rows(tm, c), _const((1, c))],
                 out_specs=[_rows(tm, c), _const((1, c)), _const((1, c))],
                 out_shape=[SDS((s, c), F32), SDS((1, c), F32), SDS((1, c), F32)], dims=("arbitrary",))(dao, a1, cn)


def _conv_a_bwd_conv(da1, z, cw, name, exchange_src=None):
    s, c = da1.shape
    tm, halo, kw = _tile(s), CONV_A_HALO, CONV_A_WIDTH
    n = s // tm
    exchanges = exchange_src is not None
    if exchanges:
        phases_of, shape_of, exchange_sems = EXCHANGES[exchange_src[0]]

    def body(d_ref, dn_ref, u_ref, g_ref, up_ref, gp_ref, cw_ref, *rest):
        i = pl.program_id(0)
        if exchanges:
            t_hbm, du_ref, dg_ref, dcw_ref, rcv_ref, buf, bd, send_sems, recv_sems = rest
            start, finish = phases_of(t_hbm, rcv_ref, send_sems, recv_sems)
            pl.when(i == 0)(start)
        else:
            du_ref, dg_ref, dcw_ref, buf, bd = rest
        uv = u_ref[...]
        sg = _sigmoid(g_ref[...])
        buf[0:halo, :] = jnp.where(i > 0, up_ref[...] * _sigmoid(gp_ref[...]), 0.0)
        buf[halo:halo + tm, :] = uv * sg
        dv = d_ref[...]
        bd[0:tm, :] = dv
        bd[tm:tm + halo, :] = jnp.where(i < n - 1, dn_ref[...], 0.0)

        @pl.when(i == 0)
        def _():
            dcw_ref[...] = jnp.zeros_like(dcw_ref)

        da0 = jnp.zeros((tm, c), F32)
        for k in range(kw):
            da0 = da0 + cw_ref[k:k + 1, :] * bd[pl.ds(kw - 1 - k, tm), :]
            dcw_ref[k:k + 1, :] += jnp.sum(dv * buf[pl.ds(halo - (kw - 1) + k, tm), :], axis=0, keepdims=True)
        du_ref[...] = (da0 * sg).astype(BF16)
        dg_ref[...] = (da0 * uv * sg * (1.0 - sg)).astype(BF16)
        if exchanges:
            pl.when(i == n - 1)(finish)

    in_specs = [_rows(tm, c), _next_rows(halo, tm, c, 0, s // halo), _rows(tm, c, 0), _rows(tm, c, 1),
                _prev_rows(halo, tm, c, 0), _prev_rows(halo, tm, c, 1), _const(cw.shape)]
    args = [da1, da1, z, z, z, z, cw]
    out_specs = [_rows(tm, c), _rows(tm, c), _const(cw.shape)]
    out_shape = [SDS((s, c), BF16), SDS((s, c), BF16), SDS(cw.shape, F32)]
    scratch = [pltpu.VMEM((tm + halo, c), F32)] * 2
    if exchanges:
        in_specs.append(ANY)
        args.append(exchange_src[1])
        out_specs.append(ANY)
        out_shape.append(shape_of(exchange_src[1]))
        scratch += exchange_sems
    return _call(body, name=name, grid=(n,), in_specs=in_specs, out_specs=out_specs, out_shape=out_shape,
                 scratch=scratch, dims=("arbitrary",))(*args)


def _lane_is_first_head(tm):
    return lax.broadcasted_iota(jnp.int32, (tm, PAIR), 1) < HEAD_DIM


def _pair_rms(xp, first):
    x2 = xp * xp
    s0 = jnp.sum(jnp.where(first, x2, 0.0), axis=-1, keepdims=True)
    s1 = jnp.sum(jnp.where(first, 0.0, x2), axis=-1, keepdims=True)
    return jnp.where(first, lax.rsqrt(s0 / HEAD_DIM + EPS), lax.rsqrt(s1 / HEAD_DIM + EPS))


def _split3(x):
    hi = x.astype(BF16)
    r1 = x - hi.astype(F32)
    mid = r1.astype(BF16)
    lo = (r1 - mid.astype(F32)).astype(BF16)
    return hi, mid, lo


def _qk_fwd(z, flog, bf, qn2, kn2, name):
    s = z.shape[0]
    tm = _tile(s)
    col0 = (z.shape[1] - 3 * D_ATTN) // D_ATTN

    def body(q_ref, k_ref, v_ref, fl_ref, bf_ref, qn_ref, kn_ref,
             qs_ref, kh_ref, vb_ref, fb_ref, ft_ref, xt_ref, carry):
        i = pl.program_id(0)
        first = _lane_is_first_head(tm)
        for p in range(N_PAIRS):
            sl = slice(p * PAIR, (p + 1) * PAIR)
            q = q_ref[:, sl]
            qs_ref[:, sl] = (q * _pair_rms(q, first) * qn_ref[...] * ATTN_SCALE).astype(BF16)
            k = k_ref[:, sl]
            kh_ref[:, sl] = (k * _pair_rms(k, first) * kn_ref[...]).astype(BF16)
        vb_ref[...] = v_ref[...].astype(BF16)

        xg = fl_ref[...] + bf_ref[...]
        valid = lax.broadcasted_iota(jnp.int32, (tm, V7X_LANES), 1) < N_HEADS
        ls = jnp.where(valid, jnp.minimum(xg, 0.0) - jnp.log(1.0 + jnp.exp(-jnp.abs(xg))), 0.0)
        tri = (lax.broadcasted_iota(jnp.int32, (tm, tm), 1) <= lax.broadcasted_iota(jnp.int32, (tm, tm), 0)).astype(BF16)
        cs = jnp.zeros((tm, V7X_LANES), F32)
        for part in _split3(ls):
            cs = cs + jnp.dot(tri, part, preferred_element_type=F32)

        @pl.when(i == 0)
        def _():
            carry[...] = jnp.zeros_like(carry)

        fv = cs + carry[0:1, :]
        carry[0:1, :] = fv[tm - 1:tm, :]
        ft_ref[...] = fv.T[0:N_HEADS, :]
        xt_ref[...] = xg.T[0:N_HEADS, :]
        for p in range(N_PAIRS):
            fb_ref[:, p * PAIR:(p + 1) * PAIR] = jnp.where(first, fv[:, 2 * p:2 * p + 1], fv[:, 2 * p + 1:2 * p + 2])

    wide = lambda col: pl.BlockSpec((tm, D_ATTN), lambda i: (i, col))
    tcol = pl.BlockSpec((N_HEADS, tm), lambda i: (0, i))
    return _call(body, name=name, grid=(s // tm,),
                 in_specs=[wide(col0), wide(col0 + 1), wide(col0 + 2), _rows(tm, V7X_LANES),
                           _const((1, V7X_LANES)), _const((1, PAIR)), _const((1, PAIR))],
                 out_specs=[wide(0), wide(0), wide(0), wide(0), tcol, tcol],
                 out_shape=[SDS((s, D_ATTN), BF16)] * 3 + [SDS((s, D_ATTN), F32), SDS((N_HEADS, s), F32),
                                                          SDS((N_HEADS, s), F32)],
                 scratch=[pltpu.VMEM((8, V7X_LANES), F32)], dims=("arbitrary",))(z, z, z, flog, bf, qn2, kn2)


ATTN_FWD_SUB = 256
ATTN_BWD_SUB = 512


def _causal_schedule(nq, key_major):
    if key_major:
        pairs = [(i, j) for j in range(nq) for i in range(j, nq)]
    else:
        pairs = [(i, j) for i in range(nq) for j in range(i + 1)]
    return (jnp.asarray([p[0] for p in pairs], jnp.int32), jnp.asarray([p[1] for p in pairs], jnp.int32))


def _sub_scores(qp, kp, ft_row, mine, r, masked, sub, tk):
    qm = jnp.where(mine, qp, jnp.zeros_like(qp))
    s2 = lax.dot_general(qm, kp, NT, preferred_element_type=F32) - ft_row
    if masked:
        row = r * sub + lax.broadcasted_iota(jnp.int32, (sub, tk), 0)
        s2 = jnp.where(lax.broadcasted_iota(jnp.int32, (sub, tk), 1) <= row, s2, NEG_BIG)
    return s2


def _attn_fwd(qs, kh, vb, fb, ft, name, gather_src=None):
    s = qs.shape[0]
    tq = tk = _tile(s)
    nq = s // tq
    sub = min(ATTN_FWD_SUB, tq)
    ii, jj = _causal_schedule(nq, key_major=False)
    n_steps = ii.shape[0]
    gathers = gather_src is not None

    def body(ii_ref, jj_ref, q_ref, k_ref, v_ref, fq_ref, ft_ref, *rest):
        if gathers:
            x_hbm, o_ref, lse_ref, wall_ref, m_s, l_s, acc_s, send_sems, recv_sems, local_sem = rest
        else:
            o_ref, lse_ref, m_s, l_s, acc_s = rest
        p, t = pl.program_id(0), pl.program_id(1)
        i, j = ii_ref[t], jj_ref[t]
        first = _lane_is_first_head(sub)
        if gathers:
            start, forward, finish = _gather_phases(x_hbm, wall_ref, send_sems, recv_sems, local_sem)
            pl.when(jnp.logical_and(p == 0, t == 0))(start)
            pl.when(jnp.logical_and(p == N_PAIRS - 1, t == 0))(forward)

        @pl.when(j == 0)
        def _():
            m_s[...] = jnp.full_like(m_s, NEG_BIG)
            l_s[...] = jnp.zeros_like(l_s)
            acc_s[...] = jnp.zeros_like(acc_s)

        def tile(masked):
            kp, vp = k_ref[...], v_ref[...]
            q_all, fq_all, acc_all = q_ref[...], fq_ref[...], acc_s[...]
            m_all, l_all = (m_s[0], m_s[1]), (l_s[0], l_s[1])
            ft_rows = [ft_ref[pl.ds(2 * p + h, 1), :] for h in range(2)]
            m_out, l_out, acc_out = ([], []), ([], []), []
            for r in range(tq // sub):
                rows = slice(r * sub, (r + 1) * sub)
                qp, fq, acc = q_all[rows, :], fq_all[rows, :], acc_all[rows, :]
                kc = (r + 1) * sub if masked else tk
                new = []
                for h in range(2):
                    mine = first if h == 0 else jnp.logical_not(first)
                    s2 = _sub_scores(qp, kp[:kc, :], ft_rows[h][:, :kc], mine, r, masked, sub, kc)
                    fqh = fq[:, h * HEAD_DIM:h * HEAD_DIM + 1]
                    m_old = m_all[h][rows, :]
                    m_new = jnp.maximum(m_old, jnp.max(s2, axis=-1, keepdims=True) + fqh)
                    pr = jnp.exp(s2 - (m_new - fqh))
                    alpha = jnp.exp(m_old - m_new)
                    l_out[h].append(alpha * l_all[h][rows, :] + jnp.sum(pr, axis=-1, keepdims=True))
                    m_out[h].append(m_new)
                    new.append(alpha * acc + jnp.dot(pr.astype(BF16), vp[:kc, :], preferred_element_type=F32))
                acc_out.append(jnp.where(first, new[0], new[1]))
            for h in range(2):
                m_s[h] = jnp.concatenate(m_out[h], axis=0)
                l_s[h] = jnp.concatenate(l_out[h], axis=0)
            acc_s[...] = jnp.concatenate(acc_out, axis=0)

        @pl.when(j < i)
        def _():
            tile(False)

        @pl.when(j == i)
        def _():
            tile(True)
            whole = _lane_is_first_head(tq)
            l_pair = jnp.where(whole, l_s[0], l_s[1])
            o_ref[...] = acc_s[...] / l_pair
            lse_ref[...] = jnp.where(whole, m_s[0], m_s[1]) + jnp.log(l_pair)

        if gathers:
            pl.when(jnp.logical_and(p == N_PAIRS - 1, t == n_steps - 1))(finish)

    qblk = pl.BlockSpec((tq, PAIR), lambda p, t, ii_r, jj_r: (ii_r[t], p))
    kblk = pl.BlockSpec((tk, PAIR), lambda p, t, ii_r, jj_r: (jj_r[t], p))
    in_specs = [qblk, kblk, kblk, qblk, pl.BlockSpec((N_HEADS, tk), lambda p, t, ii_r, jj_r: (0, jj_r[t]))]
    out_specs, out_shape = [qblk, qblk], [SDS((s, D_ATTN), F32)] * 2
    scratch = [pltpu.VMEM((2, tq, 1), F32), pltpu.VMEM((2, tq, 1), F32), pltpu.VMEM((tq, PAIR), F32)]
    args = [ii, jj, qs, kh, vb, fb, ft]
    if gathers:
        in_specs.append(ANY)
        out_specs.append(ANY)
        out_shape.append(SDS((N_DEV,) + gather_src.shape, gather_src.dtype))
        scratch += GATHER_SEMS
        args.append(gather_src)
    grid_spec = pltpu.PrefetchScalarGridSpec(num_scalar_prefetch=2, grid=(N_PAIRS, n_steps), in_specs=in_specs,
                                             out_specs=out_specs, scratch_shapes=scratch)
    return pl.pallas_call(
        body, name=name, grid_spec=grid_spec, out_shape=_in_hbm(out_shape),
        compiler_params=pltpu.CompilerParams(dimension_semantics=("arbitrary", "arbitrary"),
                                             vmem_limit_bytes=32 * MIB))(*[_keep_in_hbm(a) for a in args])


def _attn_bwd(qs, kh, vb, fb, ft, lse, o, dao, name, exchange_src=None):
    s = qs.shape[0]
    tq = tk = _tile(s)
    nq = s // tq
    sub = min(ATTN_BWD_SUB, tq)
    ii, jj = _causal_schedule(nq, key_major=True)
    n_steps = ii.shape[0]

    exchanges = exchange_src is not None

    def body(ii_ref, jj_ref, q_ref, k_ref, v_ref, fq_ref, ft_ref, lse_ref, o_ref, do_ref, *rest):
        if exchanges:
            t_hbm, dq_ref, rs_ref, dk_ref, dv_ref, df_ref, rcv_ref, dk_s, dv_s, df_s, send_sems, recv_sems = rest
        else:
            dq_ref, rs_ref, dk_ref, dv_ref, df_ref, dk_s, dv_s, df_s = rest
        p, t = pl.program_id(0), pl.program_id(1)
        i, j = ii_ref[t], jj_ref[t]
        first = _lane_is_first_head(sub)
        first_k = _lane_is_first_head(tk)
        if exchanges:
            start, finish = _chip_exchange_phases(t_hbm, rcv_ref, send_sems, recv_sems)
            pl.when(jnp.logical_and(p == 0, t == 0))(start)

        @pl.when(t == 0)
        def _():
            dq_ref[...] = jnp.zeros_like(dq_ref)
            rs_ref[...] = jnp.zeros_like(rs_ref)

        @pl.when(i == j)
        def _():
            dk_s[...] = jnp.zeros_like(dk_s)
            dv_s[...] = jnp.zeros_like(dv_s)
            df_s[...] = jnp.zeros_like(df_s)

        def tile(masked):
            kp, vp = k_ref[...], v_ref[...]
            q_all, fq_all, lse_all, o_all, do_all = q_ref[...], fq_ref[...], lse_ref[...], o_ref[...], do_ref[...]
            ft_rows = [ft_ref[pl.ds(2 * p + h, 1), :] for h in range(2)]
            dq_out, rs_out = [], []
            dk_acc, dv_acc = jnp.zeros((tk, PAIR), F32), jnp.zeros((tk, PAIR), F32)
            df_acc = [jnp.zeros((1, tk), F32), jnp.zeros((1, tk), F32)]
            for r in range(tq // sub):
                rows = slice(r * sub, (r + 1) * sub)
                qp, fq, lse, ov, dall = q_all[rows, :], fq_all[rows, :], lse_all[rows, :], o_all[rows, :], do_all[rows, :]
                dq_h, dk_h, dv_h, rs_h = [], [], [], []
                for h in range(2):
                    mine = first if h == 0 else jnp.logical_not(first)
                    s2 = _sub_scores(qp, kp, ft_rows[h], mine, r, masked, sub, tk)
                    lane = slice(h * HEAD_DIM, h * HEAD_DIM + 1)
                    pr = jnp.exp(s2 - (lse[:, lane] - fq[:, lane]))
                    dov = jnp.where(mine, dall, 0.0)
                    dsum = jnp.sum(dov * ov, axis=-1, keepdims=True)
                    dom = dov.astype(BF16)
                    dom_lo = (dov - dom.astype(F32)).astype(BF16)
                    dp = lax.dot_general(dom, vp, NT, preferred_element_type=F32)
                    dp = dp + lax.dot_general(dom_lo, vp, NT, preferred_element_type=F32)
                    ds = pr * (dp - dsum)
                    dsb = ds.astype(BF16)
                    dq_h.append(jnp.dot(dsb, kp, preferred_element_type=F32))
                    dk_h.append(lax.dot_general(dsb, qp, TN, preferred_element_type=F32))
                    dv_h.append(lax.dot_general(pr.astype(BF16), dom, TN, preferred_element_type=F32))
                    rs_h.append(jnp.sum(ds, axis=-1, keepdims=True))
                    df_acc[h] = df_acc[h] - jnp.sum(ds, axis=0, keepdims=True)
                dq_out.append(jnp.where(first, dq_h[0], dq_h[1]))
                rs_out.append(jnp.where(first, rs_h[0], rs_h[1]))
                dk_acc = dk_acc + jnp.where(first_k, dk_h[0], dk_h[1])
                dv_acc = dv_acc + jnp.where(first_k, dv_h[0], dv_h[1])
            grows = pl.ds(pl.multiple_of(i * tq, tq), tq)
            dq_ref[grows, :] += jnp.concatenate(dq_out, axis=0)
            rs_ref[grows, :] += jnp.concatenate(rs_out, axis=0)
            dk_s[...] += dk_acc
            dv_s[...] += dv_acc
            for h in range(2):
                df_s[h:h + 1, :] += df_acc[h]

        @pl.when(j < i)
        def _():
            tile(False)

        @pl.when(j == i)
        def _():
            tile(True)

        @pl.when(i == nq - 1)
        def _():
            dk_ref[...] = dk_s[...]
            dv_ref[...] = dv_s[...]
            df_ref[0] = df_s[...]

        if exchanges:
            pl.when(jnp.logical_and(p == N_PAIRS - 1, t == n_steps - 1))(finish)

    qblk = pl.BlockSpec((tq, PAIR), lambda p, t, ii_r, jj_r: (ii_r[t], p))
    kblk = pl.BlockSpec((tk, PAIR), lambda p, t, ii_r, jj_r: (jj_r[t], p))
    doblk = pl.BlockSpec((tq, PAIR), lambda p, t, ii_r, jj_r: (ii_r[t], N_PAIRS + p))
    whole = pl.BlockSpec((s, PAIR), lambda p, t, ii_r, jj_r: (0, p))
    in_specs = [qblk, kblk, kblk, qblk, pl.BlockSpec((N_HEADS, tk), lambda p, t, ii_r, jj_r: (0, jj_r[t])),
                qblk, qblk, doblk]
    out_specs = [whole, whole, kblk, kblk, pl.BlockSpec((1, 8, tk), lambda p, t, ii_r, jj_r: (p, 0, jj_r[t]))]
    out_shape = [SDS((s, D_ATTN), F32)] * 4 + [SDS((N_PAIRS, 8, s), F32)]
    scratch = [pltpu.VMEM((tk, PAIR), F32), pltpu.VMEM((tk, PAIR), F32), pltpu.VMEM((8, tk), F32)]
    args = [ii, jj, qs, kh, vb, fb, ft, lse, o, dao]
    if exchanges:
        in_specs.append(ANY)
        out_specs.append(ANY)
        out_shape.append(SDS((3,) + exchange_src.shape[1:], exchange_src.dtype))
        scratch += EXCHANGE_SEMS
        args.append(exchange_src)
    grid_spec = pltpu.PrefetchScalarGridSpec(num_scalar_prefetch=2, grid=(N_PAIRS, n_steps), in_specs=in_specs,
                                             out_specs=out_specs, scratch_shapes=scratch)
    return pl.pallas_call(
        body, name=name, grid_spec=grid_spec, out_shape=_in_hbm(out_shape),
        compiler_params=pltpu.CompilerParams(dimension_semantics=("arbitrary", "arbitrary"),
                                             vmem_limit_bytes=40 * MIB))(*[_keep_in_hbm(a) for a in args])


def _qk_bwd(z, dqs, dkh, dv, qn2, kn2, name):
    s = z.shape[0]
    tm = _tile(s)
    col0 = (z.shape[1] - 3 * D_ATTN) // D_ATTN

    def body(q_ref, k_ref, dqs_ref, dkh_ref, dv_ref, qn_ref, kn_ref, dq_ref, dk_ref, dvb_ref, dqn_ref, dkn_ref):
        first = _lane_is_first_head(tm)

        @pl.when(pl.program_id(0) == 0)
        def _():
            dqn_ref[...] = jnp.zeros_like(dqn_ref)
            dkn_ref[...] = jnp.zeros_like(dkn_ref)

        def through(x_ref, dy_ref, gain_ref, dx_ref, dgain_ref, scale):
            for p in range(N_PAIRS):
                sl = slice(p * PAIR, (p + 1) * PAIR)
                xv = x_ref[:, sl]
                r = _pair_rms(xv, first)
                xh = xv * r
                dy = dy_ref[:, sl] * scale
                dgain_ref[:, sl] += jnp.sum(dy * xh, axis=0, keepdims=True)
                dxh = dy * gain_ref[...]
                t = dxh * xh
                m0 = jnp.sum(jnp.where(first, t, 0.0), axis=-1, keepdims=True)
                m1 = jnp.sum(jnp.where(first, 0.0, t), axis=-1, keepdims=True)
                mean = jnp.where(first, m0, m1) / HEAD_DIM
                dx_ref[:, sl] = (r * (dxh - xh * mean)).astype(BF16)

        through(q_ref, dqs_ref, qn_ref, dq_ref, dqn_ref, ATTN_SCALE)
        through(k_ref, dkh_ref, kn_ref, dk_ref, dkn_ref, 1.0)
        dvb_ref[...] = dv_ref[...].astype(BF16)

    wide = lambda col: pl.BlockSpec((tm, D_ATTN), lambda i: (i, col))
    return _call(body, name=name, grid=(s // tm,),
                 in_specs=[wide(col0), wide(col0 + 1), wide(0), wide(0), wide(0), _const((1, PAIR)), _const((1, PAIR))],
                 out_specs=[wide(0), wide(0), wide(0), _const((1, D_ATTN)), _const((1, D_ATTN))],
                 out_shape=[SDS((s, D_ATTN), BF16)] * 3 + [SDS((1, D_ATTN), F32)] * 2,
                 dims=("arbitrary",))(z, z, dqs, dkh, dv, qn2, kn2)


def _gate_bwd(dft, xt, name):
    s = xt.shape[1]
    tm = _tile(s)
    n = s // tm

    def body(df_ref, xt_ref, dxt_ref, dx_ref, db_ref, carry):
        i = pl.program_id(0)

        @pl.when(i == 0)
        def _():
            carry[...] = jnp.zeros_like(carry)
            db_ref[...] = jnp.zeros_like(db_ref)

        tri = (lax.broadcasted_iota(jnp.int32, (tm, tm), 0) >= lax.broadcasted_iota(jnp.int32, (tm, tm), 1)).astype(BF16)
        rc = jnp.zeros((N_HEADS, tm), F32)
        for part in _split3(df_ref[...]):
            rc = rc + jnp.dot(part, tri, preferred_element_type=F32)
        dls = rc + carry[:, 0:1]
        carry[...] = jnp.broadcast_to(dls[:, 0:1], carry.shape)
        dxt = dls * _sigmoid(-xt_ref[...])
        dxt_ref[...] = dxt
        db_ref[...] += jnp.broadcast_to(jnp.sum(dxt, axis=-1, keepdims=True), db_ref.shape)
        padded = jnp.concatenate([dxt, jnp.zeros((V7X_LANES - N_HEADS, tm), F32)], axis=0)
        dx_ref[...] = padded.T

    rev = pl.BlockSpec((N_HEADS, tm), lambda i: (0, n - 1 - i))
    return _call(body, name=name, grid=(n,), in_specs=[rev, rev],
                 out_specs=[rev, pl.BlockSpec((tm, V7X_LANES), lambda i: (n - 1 - i, 0)), _const((N_HEADS, V7X_LANES))],
                 out_shape=[SDS((N_HEADS, s), F32), SDS((s, V7X_LANES), F32), SDS((N_HEADS, V7X_LANES), F32)],
                 scratch=[pltpu.VMEM((N_HEADS, V7X_LANES), F32)], dims=("arbitrary",))(dft, xt)


def _conv_c_fwd(z, cw, name):
    s = z.shape[0]
    c = z.shape[1] // 3
    tm, halo, kw = _tile(s), CONV_C_HALO, CONV_C_WIDTH

    def body(gb_ref, gc_ref, hh_ref, gcp_ref, hhp_ref, cw_ref, y_ref, buf):
        i = pl.program_id(0)
        buf[0:halo, :] = jnp.where(i > 0, gcp_ref[...] * hhp_ref[...], 0.0)
        buf[halo:halo + tm, :] = gc_ref[...] * hh_ref[...]
        c1 = jnp.zeros((tm, c), F32)
        for k in range(kw):
            c1 = c1 + cw_ref[k:k + 1, :] * buf[pl.ds(halo - (kw - 1) + k, tm), :]
        y_ref[...] = (gb_ref[...] * c1).astype(BF16)

    return _call(body, name=name, grid=(s // tm,),
                 in_specs=[_rows(tm, c, 0), _rows(tm, c, 1), _rows(tm, c, 2), _prev_rows(halo, tm, c, 1),
                           _prev_rows(halo, tm, c, 2), _const(cw.shape)],
                 out_specs=_rows(tm, c), out_shape=SDS((s, c), BF16),
                 scratch=[pltpu.VMEM((tm + halo, c), F32)], dims=("parallel",))(z, z, z, z, z, cw)


def _conv_c_bwd(dy0, z, cw, name):
    s = z.shape[0]
    c = z.shape[1] // 3
    tm, halo, kw = _tile(s), CONV_C_HALO, CONV_C_WIDTH
    n = s // tm

    def body(dy_ref, dyn_ref, gb_ref, gbn_ref, gc_ref, hh_ref, gcp_ref, hhp_ref, cw_ref, dz_ref, dcw_ref, buf, bd):
        i = pl.program_id(0)
        gcv, hhv, dyv = gc_ref[...], hh_ref[...], dy_ref[...]
        buf[0:halo, :] = jnp.where(i > 0, gcp_ref[...] * hhp_ref[...], 0.0)
        buf[halo:halo + tm, :] = gcv * hhv
        dc1 = dyv * gb_ref[...]
        bd[0:tm, :] = dc1
        bd[tm:tm + halo, :] = jnp.where(i < n - 1, dyn_ref[...] * gbn_ref[...], 0.0)

        @pl.when(i == 0)
        def _():
            dcw_ref[...] = jnp.zeros_like(dcw_ref)

        c1 = jnp.zeros((tm, c), F32)
        dc0 = jnp.zeros((tm, c), F32)
        for k in range(kw):
            shifted = buf[pl.ds(halo - (kw - 1) + k, tm), :]
            c1 = c1 + cw_ref[k:k + 1, :] * shifted
            dc0 = dc0 + cw_ref[k:k + 1, :] * bd[pl.ds(kw - 1 - k, tm), :]
            dcw_ref[k:k + 1, :] += jnp.sum(dc1 * shifted, axis=0, keepdims=True)
        dz_ref[:, 0:c] = (dyv * c1).astype(BF16)
        dz_ref[:, c:2 * c] = (dc0 * hhv).astype(BF16)
        dz_ref[:, 2 * c:3 * c] = (dc0 * gcv).astype(BF16)

    return _call(body, name=name, grid=(n,),
                 in_specs=[_rows(tm, c), _next_rows(halo, tm, c, 0, s // halo), _rows(tm, c, 0),
                           _next_rows(halo, tm, c, 0, s // halo), _rows(tm, c, 1), _rows(tm, c, 2),
                           _prev_rows(halo, tm, c, 1), _prev_rows(halo, tm, c, 2), _const(cw.shape)],
                 out_specs=[_rows(tm, 3 * c), _const(cw.shape)],
                 out_shape=[SDS((s, 3 * c), BF16), SDS(cw.shape, F32)],
                 scratch=[pltpu.VMEM((tm + halo, c), F32)] * 2, dims=("arbitrary",),
                 vmem_mb=48)(dy0, dy0, z, z, z, z, z, z, cw)


def _loss_head(y, target, name):
    s, d = y.shape
    tm = _tile(s)

    def body(y_ref, t_ref, loss_ref, dy_ref):
        e = y_ref[...] - t_ref[...]

        @pl.when(pl.program_id(0) == 0)
        def _():
            loss_ref[...] = jnp.zeros_like(loss_ref)

        loss_ref[...] += 0.5 * jnp.sum(jnp.mean(e * e, axis=-1, keepdims=True))
        dy_ref[...] = e / d

    return _call(body, name=name, grid=(s // tm,), in_specs=[_rows(tm, d), _rows(tm, d)],
                 out_specs=[_const((8, V7X_LANES)), _rows(tm, d)],
                 out_shape=[SDS((8, V7X_LANES), F32), SDS((s, d), F32)], dims=("arbitrary",))(y, target)


def _adamw(w, g, m, v, name):
    r, c = w.shape
    tr = next((t for t in (512, 256, 128, 64, 32, 16, 8) if r % t == 0), r)

    def body(w_ref, g_ref, m_ref, v_ref, d_ref, mo_ref, vo_ref):
        gv = g_ref[...]
        mn = ADAM_B1 * m_ref[...] + (1.0 - ADAM_B1) * gv
        vn = ADAM_B2 * v_ref[...] + (1.0 - ADAM_B2) * (gv * gv)
        m_hat = mn / (1.0 - ADAM_B1 ** ADAM_STEP)
        v_hat = vn / (1.0 - ADAM_B2 ** ADAM_STEP)
        d_ref[...] = -ADAM_LR * (m_hat / (jnp.sqrt(v_hat) + ADAM_EPS) + ADAM_WD * w_ref[...])
        mo_ref[...] = mn
        vo_ref[...] = vn

    spec = _rows(tr, c)
    return _call(body, name=name, grid=(r // tr,), in_specs=[spec] * 4, out_specs=[spec] * 3,
                 out_shape=[SDS((r, c), F32)] * 3, dims=("parallel",))(w, g, m, v)


def _position():
    return lax.axis_index("x"), lax.axis_index("y"), lax.axis_index("c")


def _other_chips(x, y):
    return [(1 - x, y), (x, 1 - y), (1 - x, 1 - y)]


def _dev_index(px, py, pc):
    return 4 * px + 2 * py + pc


def _all_gather(wloc):
    r, d = wloc.shape

    def body(x_ref, out_ref, send_sems, recv_sems, local_sem):
        start, forward, finish = _gather_phases(x_ref, out_ref, send_sems, recv_sems, local_sem)
        start()
        forward()
        finish()

    return _call(body, name="all_gather_weights", in_specs=[ANY], out_specs=ANY,
                 out_shape=SDS((N_DEV, r, d), wloc.dtype), scratch=GATHER_SEMS)(wloc)


GATHER_SEMS = [pltpu.SemaphoreType.DMA((7,)), pltpu.SemaphoreType.DMA((7,)), pltpu.SemaphoreType.DMA((1,))]


def _gather_phases(x_ref, out_ref, send_sems, recv_sems, local_sem):
    x, y, c = _position()
    me, sibling = (x, y, c), (x, y, 1 - c)
    chips = _other_chips(x, y)

    def slot(dev):
        return out_ref.at[_dev_index(*dev)]

    def copy(k, block, to, src=None):
        return pltpu.make_async_remote_copy(
            src_ref=slot(block) if src is None else src, dst_ref=slot(block),
            send_sem=send_sems.at[k], recv_sem=recv_sems.at[k], device_id=to, device_id_type=MESH)

    mine = pltpu.make_async_copy(x_ref, slot(me), local_sem.at[0])
    first = [copy(0, me, sibling, src=x_ref)] + [copy(1 + j, me, (*chip, c), src=x_ref) for j, chip in enumerate(chips)]
    passed = [copy(4 + j, (*chip, c), sibling) for j, chip in enumerate(chips)]

    def start():
        mine.start()
        for cp in first:
            cp.start()

    def forward():
        for j, chip in enumerate(chips):
            copy(1 + j, (*chip, c), me).wait_recv()
            passed[j].start()

    def finish():
        copy(0, sibling, me).wait_recv()
        for j, chip in enumerate(chips):
            copy(4 + j, (*chip, 1 - c), me).wait_recv()
        for cp in first + passed:
            cp.wait_send()
        mine.wait()

    return start, forward, finish


def _row_block(r):
    return next(t for t in range(704, 0, -BF16_ROWS) if r % t == 0)


def _pair_exchange(gall, name):
    def body(g_ref, out_ref, send_sems, recv_sems):
        start, finish = _pair_exchange_phases(g_ref, out_ref, send_sems, recv_sems)
        start()
        finish()

    return _call(body, name=name, in_specs=[ANY], out_specs=ANY, out_shape=_pair_exchange_shape(gall),
                 scratch=PAIR_EXCHANGE_SEMS)(gall)


PAIR_EXCHANGE_SEMS = [pltpu.SemaphoreType.DMA((4,)), pltpu.SemaphoreType.DMA((4,))]


def _pair_exchange_shape(gall):
    return SDS((4,) + gall.shape[1:], gall.dtype)


def _pair_exchange_phases(g_ref, out_ref, send_sems, recv_sems):
    x, y, c = _position()
    sibling = (x, y, 1 - c)
    dests = [sibling] + [(*chip, 1 - c) for chip in _other_chips(x, y)]
    copies = [pltpu.make_async_remote_copy(
        src_ref=g_ref.at[_dev_index(*dest)], dst_ref=out_ref.at[k], send_sem=send_sems.at[k],
        recv_sem=recv_sems.at[k], device_id=sibling, device_id_type=MESH) for k, dest in enumerate(dests)]

    def start():
        for cp in copies:
            cp.start()

    def finish():
        for cp in copies:
            cp.wait()

    return start, finish


def _pair_sum(gall, sib, idx, name):
    _, r, d = gall.shape
    tr = _row_block(r)

    def body(idx_ref, a_ref, b_ref, o_ref):
        o_ref[...] = (a_ref[...].astype(F32) + b_ref[...].astype(F32)).astype(o_ref.dtype)

    grid_spec = pltpu.PrefetchScalarGridSpec(
        num_scalar_prefetch=1, grid=(4, r // tr),
        in_specs=[pl.BlockSpec((1, tr, d), lambda k, i, idx_ref: (idx_ref[k], i, 0)),
                  pl.BlockSpec((1, tr, d), lambda k, i, idx_ref: (k, i, 0))],
        out_specs=pl.BlockSpec((1, tr, d), lambda k, i, idx_ref: (k, i, 0)))
    return pl.pallas_call(body, name=name, grid_spec=grid_spec,
                          out_shape=_in_hbm(SDS((4, r, d), gall.dtype)),
                          compiler_params=pltpu.CompilerParams(dimension_semantics=("parallel", "parallel")))(
        idx, _keep_in_hbm(gall), _keep_in_hbm(sib))


def _chip_exchange(tsum):
    _, r, d = tsum.shape

    def body(t_ref, out_ref, send_sems, recv_sems):
        start, finish = _chip_exchange_phases(t_ref, out_ref, send_sems, recv_sems)
        start()
        finish()

    return _call(body, name="reduce_scatter_chip_exchange", in_specs=[ANY], out_specs=ANY,
                 out_shape=SDS((3, r, d), tsum.dtype), scratch=EXCHANGE_SEMS)(tsum)


EXCHANGE_SEMS = [pltpu.SemaphoreType.DMA((3,)), pltpu.SemaphoreType.DMA((3,))]


def _chip_exchange_phases(t_ref, out_ref, send_sems, recv_sems):
    x, y, c = _position()
    copies = [pltpu.make_async_remote_copy(
        src_ref=t_ref.at[1 + k], dst_ref=out_ref.at[k], send_sem=send_sems.at[k], recv_sem=recv_sems.at[k],
        device_id=(*chip, c), device_id_type=MESH) for k, chip in enumerate(_other_chips(x, y))]

    def start():
        for cp in copies:
            cp.start()

    def finish():
        for cp in copies:
            cp.wait()

    return start, finish


def _chip_exchange_shape(tsum):
    return SDS((3,) + tsum.shape[1:], tsum.dtype)


EXCHANGES = dict(pair=(_pair_exchange_phases, _pair_exchange_shape, PAIR_EXCHANGE_SEMS),
                 chip=(_chip_exchange_phases, _chip_exchange_shape, EXCHANGE_SEMS))


def _final_sum(tsum, rcv, name):
    _, r, d = tsum.shape
    tr = _row_block(r)

    def body(t_ref, r_ref, o_ref):
        acc = t_ref[0].astype(F32)
        for k in range(3):
            acc = acc + r_ref[k].astype(F32)
        o_ref[...] = acc

    return _call(body, name=name, grid=(r // tr,),
                 in_specs=[pl.BlockSpec((1, tr, d), lambda i: (0, i, 0)), pl.BlockSpec((3, tr, d), lambda i: (0, i, 0))],
                 out_specs=_rows(tr, d), out_shape=SDS((r, d), F32), dims=("parallel",))(tsum, rcv)


def _all_reduce_small(buf):
    nr, lanes = buf.shape

    def body(b_ref, out_ref, gath, send_sems, recv_sems):
        x, y, c = _position()
        my_slot = _dev_index(x, y, c)
        gath[my_slot] = b_ref[...]
        copies = []
        for k in range(1, N_DEV):
            dx, dy, dc = (k >> 2) & 1, (k >> 1) & 1, k & 1
            peer = (1 - x if dx else x, 1 - y if dy else y, 1 - c if dc else c)
            copies.append(pltpu.make_async_remote_copy(
                src_ref=b_ref, dst_ref=gath.at[my_slot], send_sem=send_sems.at[k - 1], recv_sem=recv_sems.at[k - 1],
                device_id=peer, device_id_type=MESH))
        for cp in copies:
            cp.start()
        for cp in copies:
            cp.wait()
        acc = gath[0]
        for sidx in range(1, N_DEV):
            acc = acc + gath[sidx]
        out_ref[...] = acc

    return _call(body, name="all_reduce_small", in_specs=[VMEM], out_specs=VMEM, out_shape=SDS((nr, lanes), F32),
                 scratch=[pltpu.VMEM((N_DEV, nr, lanes), F32), pltpu.SemaphoreType.DMA((7,)),
                          pltpu.SemaphoreType.DMA((7,))])(buf)


def _ffn_block_fwd(x, gain, wall, offs, fs, tag, gather_src=None):
    res = _ffn_fwd(x, gain, wall, offs, fs, f"{tag}_fwd", gather_src)
    out, xn, g, u, h = res[:5]
    return out, (x, gain, xn, g, u, h), (res[5] if gather_src is not None else None)


def _ffn_block_bwd(dout, saved, wall, offs, fs, tag, exchange_src=None):
    x, gain, xn, g, u, h = saved
    res = _ffn_bwd_act(dout, g, u, x, gain, wall, offs, fs, f"{tag}_bwd_act", exchange_src)
    dg, du, dy_b, dx, dgain = res[:5]
    dwg = _mm_tn(dg, xn, f"{tag}_dwg", BF16)
    dwu = _mm_tn(du, xn, f"{tag}_dwu", BF16)
    dwd = _mm_tn(h, dy_b, f"{tag}_dwd", BF16)
    return dx, (dwg, dwu, dwd), dgain, (res[5] if exchange_src is not None else None)


def _local_step(x, target, wall_a, fs, small, plan):
    grads = {}
    first, second = (0, fs, 2 * fs), (3 * fs, 4 * fs, 5 * fs)
    reduces = "pair_sum" in plan

    x1, s_f1a, wall_b = _ffn_block_fwd(x, small["ffn1_norm"][0], wall_a, first, fs, "l0_ffn1", plan.get("shard_b"))
    wall_b = plan.get("wall_b", wall_b)
    mixw = plan["mix_b"](wall_b)
    hn0 = _rmsnorm_fwd(x1, small["mix_norm"][0], "l0_mix_norm")
    z = _mm(hn0, mixw["ev_w_main_t"], "nt", "ev_in_proj")
    flog = _mm(hn0, mixw["ev_w_f_t"], "nt", "ev_in_proj_gate")
    a, a1 = _conv_a_fwd(z, small["ev_conv_w32"], small["ev_conv_b"], small["ev_conv_norm"], "ev_conv_fwd")
    qs, kh, vb, fb, ft, xt = _qk_fwd(z, flog, small["ev_b_f128"], small["ev_q_norm2"], small["ev_k_norm2"], "ev_qk_fwd")
    if "wall_c" in plan:
        o, lse = _attn_fwd(qs, kh, vb, fb, ft, "ev_attn_fwd")
        wall_c = plan["wall_c"]
    else:
        o, lse, wall_c = _attn_fwd(qs, kh, vb, fb, ft, "ev_attn_fwd", gather_src=plan["shard_c"])
    mixw = {**mixw, **plan["mix_c"](wall_c)}
    ao = jnp.concatenate([a, o.astype(BF16)], axis=1)
    x2 = _mm(ao, mixw["ev_w_out"], "nn", "ev_out_proj", add=x1)
    x3, s_f2a, _ = _ffn_block_fwd(x2, small["ffn2_norm"][0], wall_c, first, fs, "l0_ffn2")

    x4, s_f1b, wall_d = _ffn_block_fwd(x3, small["ffn1_norm"][1], wall_c, second, fs, "l1_ffn1", plan.get("shard_d"))
    wall_d = plan.get("wall_d", wall_d)
    hn1 = _rmsnorm_fwd(x4, small["mix_norm"][1], "l1_mix_norm")
    zo = _mm(hn1, mixw["od_w_in_t"], "nt", "od_in_proj")
    y0 = _conv_c_fwd(zo, small["od_conv_w8"], "od_conv_fwd")
    x5 = _mm(y0, mixw["od_w_out"], "nn", "od_out_proj", add=x4)
    x6, s_f2b, _ = _ffn_block_fwd(x5, small["ffn2_norm"][1], wall_d, first, fs, "l1_ffn2")

    loss, d6 = _loss_head(x6, target, "loss_head")

    d5, grads["l1_ffn2"], grads["ffn2_norm_1"], _ = _ffn_block_bwd(d6, s_f2b, wall_d, first, fs, "l1_ffn2")
    d5b = d5.astype(BF16)
    dy0 = _mm(d5b, mixw["od_w_out"], "nt", "od_out_proj_bwd")
    grads["od_w_out"] = _mm_tn(y0, d5b, "od_dw_out", BF16)
    dzo, grads["od_conv_w"] = _conv_c_bwd(dy0, zo, small["od_conv_w8"], "od_conv_bwd")
    dh1 = _mm(dzo, mixw["od_w_in_t"], "nn", "od_in_proj_bwd")
    grads["od_w_in_t"] = _mm_tn(dzo, hn1, "od_dw_in", BF16)
    d4, grads["mix_norm_1"] = _rmsnorm_bwd(x4, small["mix_norm"][1], dh1, d5, "l1_mix_norm_bwd")
    d3, grads["l1_ffn1"], grads["ffn1_norm_1"], _ = _ffn_block_bwd(d4, s_f1b, wall_c, second, fs, "l1_ffn1")

    d2, grads["l0_ffn2"], grads["ffn2_norm_0"], _ = _ffn_block_bwd(d3, s_f2a, wall_c, first, fs, "l0_ffn2")
    partials_c = plan["partials_c"](grads) if reduces else None
    d2b = d2.astype(BF16)
    dao = _mm(d2b, mixw["ev_w_out"], "nt", "ev_out_proj_bwd")
    grads["ev_w_out"] = _mm_tn(ao, d2b, "ev_dw_out", BF16)
    da1, grads["ev_conv_norm"], grads["ev_conv_b"] = _conv_a_bwd_norm(dao, a1, small["ev_conv_norm"], "ev_conv_bwd_norm")
    res = _conv_a_bwd_conv(da1, z, small["ev_conv_w32"], "ev_conv_bwd_conv",
                           exchange_src=("pair", partials_c) if reduces else None)
    du, dg, grads["ev_conv_w"] = res[:3]
    sums_c = plan["pair_sum"](partials_c, res[3], "c") if reduces else None
    res = _attn_bwd(qs, kh, vb, fb, ft, lse, o, dao, "ev_attn_bwd", exchange_src=sums_c)
    dqs, rs, dkh, dv, df4 = res[:5]
    if reduces:
        grads["pair_sums_c"], grads["exchanged_c"] = sums_c, res[5]
    dq, dk, dvb, grads["ev_q_norm"], grads["ev_k_norm"] = _qk_bwd(
        z, dqs, dkh, dv, small["ev_q_norm2"], small["ev_k_norm2"], "ev_qk_bwd")
    dft = df4[:, 0:2, :].reshape(N_HEADS, -1) + rs.reshape(-1, N_HEADS, HEAD_DIM)[:, :, 0].T
    dxt, dflog, grads["ev_b_f"] = _gate_bwd(dft, xt, "ev_gate_bwd")
    dz = jnp.concatenate([du, dg, dq, dk, dvb], axis=1)
    dflog_b = dflog.astype(BF16)
    dh0 = _mm(dz, mixw["ev_w_main_t"], "nn", "ev_in_proj_bwd")
    dh0 = _mm(dflog_b, mixw["ev_w_f_t"], "nn", "ev_in_proj_gate_bwd", add=dh0)
    dw_main = _mm_tn(dz, hn0, "ev_dw_in", BF16)
    dw_f = _mm(dxt.astype(BF16), hn0, "nn", "ev_dw_in_gate", BF16)
    grads["ev_w_in_t"] = jnp.concatenate([dw_main, dw_f], axis=0)
    d1, grads["mix_norm_0"] = _rmsnorm_bwd(x1, small["mix_norm"][0], dh0, d2, "l0_mix_norm_bwd")
    sums_b = None
    if reduces:
        partials_b = plan["partials_b"](grads)
        sums_b = plan["pair_sum"](partials_b, _pair_exchange(partials_b, "reduce_scatter_pair_exchange_b"), "b")
    d0, grads["l0_ffn1"], grads["ffn1_norm_0"], exchanged_b = _ffn_block_bwd(
        d1, s_f1a, wall_a, first, fs, "l0_ffn1", exchange_src=("chip", sums_b) if reduces else None)
    if reduces:
        grads["pair_sums_b"], grads["exchanged_b"] = sums_b, exchanged_b
    return loss, d0, grads


def _round_up(n, m):
    return -(-n // m) * m


def _pad_rows(a, rows):
    return jnp.pad(a, ((0, rows - a.shape[0]), (0, 0)))


SMALL_ORDER = ("loss", "ffn1_norm", "mix_norm", "ffn2_norm", "ev_b_f", "ev_conv_b", "ev_conv_norm",
               "ev_q_norm", "ev_k_norm", "ev_conv_w", "od_conv_w")


def _pack_small(parts):
    flat = jnp.concatenate([parts[k].reshape(-1).astype(F32) for k in SMALL_ORDER])
    n = _round_up(flat.shape[0], 8 * V7X_LANES)
    return jnp.pad(flat, (0, n - flat.shape[0])).reshape(-1, V7X_LANES)


def _unpack_small(buf, shapes):
    flat = buf.reshape(-1)
    out, pos = {}, 0
    for k in SMALL_ORDER:
        n = math.prod(shapes[k])
        out[k] = flat[pos:pos + n].reshape(shapes[k])
        pos += n
    return out


def kernel(x, ffn1_norm, ffn1_w_gate, ffn1_w_up, ffn1_w_down, mix_norm, ffn2_norm, ffn2_w_gate, ffn2_w_up, ffn2_w_down, ev_w_in, ev_b_f, ev_conv_w, ev_conv_b, ev_conv_norm, ev_q_norm, ev_k_norm, ev_w_out, od_w_in, od_conv_w, od_w_out, loss_target, m_ffn1_norm, m_ffn1_w_gate, m_ffn1_w_up, m_ffn1_w_down, m_mix_norm, m_ffn2_norm, m_ffn2_w_gate, m_ffn2_w_up, m_ffn2_w_down, m_ev_w_in, m_ev_b_f, m_ev_conv_w, m_ev_conv_b, m_ev_conv_norm, m_ev_q_norm, m_ev_k_norm, m_ev_w_out, m_od_w_in, m_od_conv_w, m_od_w_out, v_ffn1_norm, v_ffn1_w_gate, v_ffn1_w_up, v_ffn1_w_down, v_mix_norm, v_ffn2_norm, v_ffn2_w_gate, v_ffn2_w_up, v_ffn2_w_down, v_ev_w_in, v_ev_b_f, v_ev_conv_w, v_ev_conv_b, v_ev_conv_norm, v_ev_q_norm, v_ev_k_norm, v_ev_w_out, v_od_w_in, v_od_conv_w, v_od_w_out):
    weights = dict(ffn1_norm=ffn1_norm, ffn1_w_gate=ffn1_w_gate, ffn1_w_up=ffn1_w_up, ffn1_w_down=ffn1_w_down,
                   mix_norm=mix_norm, ffn2_norm=ffn2_norm, ffn2_w_gate=ffn2_w_gate, ffn2_w_up=ffn2_w_up,
                   ffn2_w_down=ffn2_w_down, ev_w_in=ev_w_in, ev_b_f=ev_b_f, ev_conv_w=ev_conv_w, ev_conv_b=ev_conv_b,
                   ev_conv_norm=ev_conv_norm, ev_q_norm=ev_q_norm, ev_k_norm=ev_k_norm, ev_w_out=ev_w_out,
                   od_w_in=od_w_in, od_conv_w=od_conv_w, od_w_out=od_w_out)
    m_in = dict(ffn1_norm=m_ffn1_norm, ffn1_w_gate=m_ffn1_w_gate, ffn1_w_up=m_ffn1_w_up, ffn1_w_down=m_ffn1_w_down,
                mix_norm=m_mix_norm, ffn2_norm=m_ffn2_norm, ffn2_w_gate=m_ffn2_w_gate, ffn2_w_up=m_ffn2_w_up,
                ffn2_w_down=m_ffn2_w_down, ev_w_in=m_ev_w_in, ev_b_f=m_ev_b_f, ev_conv_w=m_ev_conv_w,
                ev_conv_b=m_ev_conv_b, ev_conv_norm=m_ev_conv_norm, ev_q_norm=m_ev_q_norm, ev_k_norm=m_ev_k_norm,
                ev_w_out=m_ev_w_out, od_w_in=m_od_w_in, od_conv_w=m_od_conv_w, od_w_out=m_od_w_out)
    v_in = dict(ffn1_norm=v_ffn1_norm, ffn1_w_gate=v_ffn1_w_gate, ffn1_w_up=v_ffn1_w_up, ffn1_w_down=v_ffn1_w_down,
                mix_norm=v_mix_norm, ffn2_norm=v_ffn2_norm, ffn2_w_gate=v_ffn2_w_gate, ffn2_w_up=v_ffn2_w_up,
                ffn2_w_down=v_ffn2_w_down, ev_w_in=v_ev_w_in, ev_b_f=v_ev_b_f, ev_conv_w=v_ev_conv_w,
                ev_conv_b=v_ev_conv_b, ev_conv_norm=v_ev_conv_norm, ev_q_norm=v_ev_q_norm, ev_k_norm=v_ev_k_norm,
                ev_w_out=v_ev_w_out, od_w_in=v_od_w_in, od_conv_w=v_od_conv_w, od_w_out=v_od_w_out)
    order = list(weights)

    d = x.shape[-1]
    fs = ffn1_w_gate.shape[2]
    n_in = ev_w_in.shape[2]
    n_in_pad = _round_up(n_in, BF16_ROWS)
    n_out = ev_w_out.shape[1]
    n_od = od_w_in.shape[2]
    d_conv = ev_conv_b.shape[1]
    d_in_even = n_in * N_DEV
    d_main = d_in_even - N_HEADS
    cx, cy, cc = _position()
    me = _dev_index(cx, cy, cc)

    def block(wg, wu, wd, layer):
        return [wg[layer].T, wu[layer].T, wd[layer]]

    def stack(parts):
        return jnp.concatenate([p.astype(BF16) for p in parts], axis=0)

    ffn1, ffn2 = (ffn1_w_gate, ffn1_w_up, ffn1_w_down), (ffn2_w_gate, ffn2_w_up, ffn2_w_down)
    shard_a = stack(block(*ffn1, 0))
    shard_b = stack([_pad_rows(ev_w_in[0].T, n_in_pad), ev_w_out[0]])
    shard_c = stack(block(*ffn2, 0) + block(*ffn1, 1) + [od_w_in[0].T, od_w_out[0]])
    shard_d = stack(block(*ffn2, 1))
    off_ev_in, off_ev_out = 0, n_in_pad
    off_od_in, off_od_out = 6 * fs, 6 * fs + n_od
    wall_a = _all_gather(shard_a)

    def even_mixer_weights(wall_b):
        ev_w_in_t = wall_b[:, off_ev_in:off_ev_in + n_in, :].reshape(d_in_even, d)
        return dict(ev_w_main_t=ev_w_in_t[:d_main], ev_w_f_t=_pad_rows(ev_w_in_t[d_main:], V7X_LANES),
                    ev_w_out=wall_b[:, off_ev_out:off_ev_out + n_out, :].reshape(N_DEV * n_out, d))

    def odd_mixer_weights(wall_c):
        return dict(od_w_in_t=wall_c[:, off_od_in:off_od_in + n_od, :].reshape(N_DEV * n_od, d),
                    od_w_out=wall_c[:, off_od_out:off_od_out + n_out, :].reshape(N_DEV * n_out, d))

    def by_dev(a, rows, pad_to=None):
        a = a.reshape(N_DEV, rows, d)
        return a if pad_to is None else jnp.pad(a, ((0, 0), (0, pad_to - rows), (0, 0)))

    idx = jnp.stack([me] + [_dev_index(*chip, cc) for chip in _other_chips(cx, cy)]).astype(jnp.int32)

    def pair_sum(partials, from_sibling, tag):
        return _pair_sum(partials, from_sibling, idx, f"reduce_scatter_pair_sum_{tag}")

    def ffn_pieces(g, key):
        return [by_dev(t, fs) for t in g[key]]

    conv_shapes = dict(ev_conv_w=(CONV_A_WIDTH, d_conv), od_conv_w=(CONV_C_WIDTH, d))
    zero_small = {k: jnp.zeros(s_, F32) for k, s_ in conv_shapes.items()}
    ev_cw_part = lax.dynamic_update_slice(zero_small["ev_conv_w"], ev_conv_w[0], (0, me * ev_conv_w.shape[2]))
    od_cw_part = lax.dynamic_update_slice(zero_small["od_conv_w"], od_conv_w[0], (0, me * od_conv_w.shape[2]))
    zeros_like_small = {k: jnp.zeros((1,), F32) for k in SMALL_ORDER}
    taps = _unpack_small(_all_reduce_small(_pack_small({**zeros_like_small, "ev_conv_w": ev_cw_part,
                                                        "od_conv_w": od_cw_part})),
                         {**{k: (1,) for k in SMALL_ORDER}, **conv_shapes})
    small = dict(
        ffn1_norm=[ffn1_norm[l][None] for l in range(2)], mix_norm=[mix_norm[l][None] for l in range(2)],
        ffn2_norm=[ffn2_norm[l][None] for l in range(2)],
        ev_conv_w32=_pad_rows(taps["ev_conv_w"], CONV_A_WIDTH + 1), ev_conv_b=ev_conv_b, ev_conv_norm=ev_conv_norm,
        ev_b_f128=jnp.pad(ev_b_f, ((0, 0), (0, V7X_LANES - N_HEADS))),
        ev_q_norm2=jnp.tile(ev_q_norm, (1, 2)), ev_k_norm2=jnp.tile(ev_k_norm, (1, 2)),
        od_conv_w8=_pad_rows(taps["od_conv_w"], 8),
    )

    plan = dict(
        shard_b=shard_b, shard_c=shard_c, shard_d=shard_d, mix_b=even_mixer_weights, mix_c=odd_mixer_weights, pair_sum=pair_sum,
        partials_c=lambda g1: jnp.concatenate(
            ffn_pieces(g1, "l0_ffn2") + ffn_pieces(g1, "l1_ffn1") + ffn_pieces(g1, "l1_ffn2")
            + [by_dev(g1["od_w_in_t"], n_od), by_dev(g1["od_w_out"], n_out)], axis=1),
        partials_b=lambda g1: jnp.concatenate(
            [by_dev(g1["ev_w_in_t"], n_in, n_in_pad), by_dev(g1["ev_w_out"], n_out)], axis=1))
    loss_p, grad_x, g = _local_step(x[0], loss_target[0], wall_a, fs, small, plan)

    partials_a = jnp.concatenate(ffn_pieces(g, "l0_ffn1"), axis=1)
    sums_a = pair_sum(partials_a, _pair_exchange(partials_a, "reduce_scatter_pair_exchange_a"), "a")
    gsum_a = _final_sum(sums_a, _chip_exchange(sums_a), "reduce_scatter_final_sum_a")
    gsum_b = _final_sum(g["pair_sums_b"], g["exchanged_b"], "reduce_scatter_final_sum_b")
    gsum_c = _final_sum(g["pair_sums_c"], g["exchanged_c"], "reduce_scatter_final_sum_c")

    grad = {}
    where = dict(ffn1=((gsum_a, 0), (gsum_c, 3 * fs)), ffn2=((gsum_c, 0), (gsum_c, 6 * fs)))
    for blk, places in where.items():
        for wi, kind in enumerate(("gate", "up", "down")):
            rows = [buf[off + wi * fs:off + (wi + 1) * fs] for buf, off in places]
            grad[f"{blk}_w_{kind}"] = jnp.stack(rows if kind == "down" else [r.T for r in rows])
    grad["ev_w_in"] = gsum_b[off_ev_in:off_ev_in + n_in].T[None]
    grad["ev_w_out"] = gsum_b[off_ev_out:off_ev_out + n_out][None]
    grad["od_w_in"] = gsum_c[9 * fs:9 * fs + n_od].T[None]
    grad["od_w_out"] = gsum_c[9 * fs + n_od:9 * fs + n_od + n_out][None]

    heads = lambda t: t.reshape(N_HEADS, HEAD_DIM).sum(axis=0)
    parts = dict(
        loss=loss_p[0, 0:1],
        ffn1_norm=jnp.stack([g["ffn1_norm_0"][0], g["ffn1_norm_1"][0]]),
        mix_norm=jnp.stack([g["mix_norm_0"][0], g["mix_norm_1"][0]]),
        ffn2_norm=jnp.stack([g["ffn2_norm_0"][0], g["ffn2_norm_1"][0]]),
        ev_b_f=g["ev_b_f"][:, 0], ev_conv_b=g["ev_conv_b"], ev_conv_norm=g["ev_conv_norm"],
        ev_q_norm=heads(g["ev_q_norm"]), ev_k_norm=heads(g["ev_k_norm"]),
        ev_conv_w=g["ev_conv_w"][:CONV_A_WIDTH], od_conv_w=g["od_conv_w"][:CONV_C_WIDTH])
    small_shapes = dict(loss=(1,), ffn1_norm=ffn1_norm.shape, mix_norm=mix_norm.shape, ffn2_norm=ffn2_norm.shape,
                        ev_b_f=ev_b_f.shape, ev_conv_b=ev_conv_b.shape, ev_conv_norm=ev_conv_norm.shape,
                        ev_q_norm=ev_q_norm.shape, ev_k_norm=ev_k_norm.shape, **conv_shapes)
    red = _unpack_small(_all_reduce_small(_pack_small(parts)), small_shapes)
    loss = red["loss"][0]
    for k in ("ffn1_norm", "mix_norm", "ffn2_norm", "ev_b_f", "ev_conv_b", "ev_conv_norm", "ev_q_norm", "ev_k_norm"):
        grad[k] = red[k]
    grad["ev_conv_w"] = lax.dynamic_slice(red["ev_conv_w"], (0, me * ev_conv_w.shape[2]),
                                          (CONV_A_WIDTH, ev_conv_w.shape[2]))[None]
    grad["od_conv_w"] = lax.dynamic_slice(red["od_conv_w"], (0, me * od_conv_w.shape[2]),
                                          (CONV_C_WIDTH, od_conv_w.shape[2]))[None]

    big = ("ffn1_w_gate", "ffn1_w_up", "ffn1_w_down", "ffn2_w_gate", "ffn2_w_up", "ffn2_w_down",
           "ev_w_in", "ev_w_out", "od_w_in", "od_w_out")
    delta, new_m, new_v = {}, {}, {}
    for k in big:
        shp = weights[k].shape
        flat = lambda t: t.reshape(-1, shp[-1])
        dk, mk, vk = _adamw(flat(weights[k]), flat(grad[k]), flat(m_in[k]), flat(v_in[k]), f"adamw_{k}")
        delta[k], new_m[k], new_v[k] = dk.reshape(shp), mk.reshape(shp), vk.reshape(shp)
    rest = [k for k in order if k not in big]
    cat = lambda src: jnp.concatenate([src[k].reshape(-1) for k in rest])
    n_small = sum(math.prod(weights[k].shape) for k in rest)
    n_pad = _round_up(n_small, 8 * V7X_LANES)
    as_rows = lambda t: jnp.pad(t, (0, n_pad - n_small)).reshape(-1, V7X_LANES)
    v_rows = jnp.pad(cat(v_in), (0, n_pad - n_small), constant_values=1.0).reshape(-1, V7X_LANES)
    ds, ms, vs = _adamw(as_rows(cat(weights)), as_rows(cat(grad)), as_rows(cat(m_in)), v_rows, "adamw_small")
    pos = 0
    for k in rest:
        n = math.prod(weights[k].shape)
        for dst, src in ((delta, ds), (new_m, ms), (new_v, vs)):
            dst[k] = src.reshape(-1)[pos:pos + n].reshape(weights[k].shape)
        pos += n

    return (loss, grad_x[None], *[grad[k] for k in order], *[delta[k] for k in order],
            *[new_m[k] for k in order], *[new_v[k] for k in order])
```

```python
import math

import jax
import jax.numpy as jnp
from jax import lax
from jax.experimental import pallas as pl
from jax.experimental.pallas import tpu as pltpu

F32 = jnp.float32
BF16 = jnp.bfloat16
SDS = jax.ShapeDtypeStruct
MESH = pl.DeviceIdType.MESH

N_DEV = 8
EPS = 1e-6
FFN_RES = 0.5
HEAD_DIM = 64
N_HEADS = 8
D_ATTN = N_HEADS * HEAD_DIM
N_PAIRS = N_HEADS // 2
PAIR = 2 * HEAD_DIM
ATTN_SCALE = 1.0 / math.sqrt(HEAD_DIM)
CONV_A_WIDTH = 31
CONV_A_HALO = 32
CONV_C_WIDTH = 3
CONV_C_HALO = 8
NEG_BIG = -1e30
ADAM_LR, ADAM_B1, ADAM_B2, ADAM_EPS, ADAM_WD, ADAM_STEP = 0.001, 0.9, 0.999, 1e-08, 0.01, 10

V7X_VMEM_BYTES = 64 * 1024 * 1024
V7X_LANES = 128
BF16_ROWS = 16
MIB = 1024 * 1024

NT = (((1,), (1,)), ((), ()))
TN = (((0,), (0,)), ((), ()))


def _call(body, *, name, out_shape, in_specs, out_specs, grid=(), scratch=(), dims=None, vmem_mb=32, **kw):
    params = dict(vmem_limit_bytes=min(vmem_mb * MIB, V7X_VMEM_BYTES - 4 * MIB))
    if dims is not None:
        params["dimension_semantics"] = dims
    call = pl.pallas_call(
        body, name=name, grid=grid, in_specs=in_specs, out_specs=out_specs, out_shape=_in_hbm(out_shape),
        scratch_shapes=list(scratch), compiler_params=pltpu.CompilerParams(**params), **kw)
    return lambda *args: call(*[_keep_in_hbm(a) for a in args])


LARGE_OPERAND_BYTES = MIB


def _is_large(a):
    return a.ndim >= 2 and math.prod(a.shape) * jnp.dtype(a.dtype).itemsize >= LARGE_OPERAND_BYTES


def _keep_in_hbm(a):
    return pltpu.with_memory_space_constraint(a, pltpu.HBM) if _is_large(a) else a


def _in_hbm(out_shape):
    one = lambda s: pltpu.HBM(s.shape, s.dtype) if _is_large(s) else s
    return [one(s) for s in out_shape] if isinstance(out_shape, (list, tuple)) else one(out_shape)


def _tile(n, want=512):
    return want if n % want == 0 else n


def _rows(tm, d, col=0):
    return pl.BlockSpec((tm, d), lambda i: (i, col))


def _const(shape):
    return pl.BlockSpec(shape, lambda *_: (0,) * len(shape))


ANY = pl.BlockSpec(memory_space=pl.ANY)
VMEM = pl.BlockSpec(memory_space=pltpu.VMEM)


def _sigmoid(x):
    return 1.0 / (1.0 + jnp.exp(-x))


def _rmsnorm_fwd(x, gain, name):
    s, d = x.shape
    tm = _tile(s)

    def body(x_ref, g_ref, o_ref):
        xv = x_ref[...]
        r = lax.rsqrt(jnp.mean(xv * xv, axis=-1, keepdims=True) + EPS)
        o_ref[...] = (xv * r * g_ref[...]).astype(BF16)

    return _call(body, name=name, grid=(s // tm,), in_specs=[_rows(tm, d), _const((1, d))],
                 out_specs=_rows(tm, d), out_shape=SDS((s, d), BF16), dims=("parallel",))(x, gain)


def _col_tile(n):
    for t in (1536, 1280, 1024, 768, 512, 256, 128):
        if n % t == 0:
            return t
    return n


def _mm(a, b, mode, name, out_dtype=F32, add=None):
    if mode == "tn":
        k, m = a.shape
        n = b.shape[1]
        bm = next((t for t in range(768, 0, -V7X_LANES) if m % t == 0), m)

        def body_tn(a_ref, b_ref, o_ref):
            o_ref[...] = lax.dot_general(a_ref[...].astype(BF16), b_ref[...].astype(BF16), TN,
                                         preferred_element_type=F32).astype(out_dtype)

        return _call(body_tn, name=name, grid=(m // bm,),
                     in_specs=[pl.BlockSpec((k, bm), lambda i: (0, i)), _const((k, n))],
                     out_specs=pl.BlockSpec((bm, n), lambda i: (i, 0)),
                     out_shape=SDS((m, n), out_dtype), dims=("parallel",), vmem_mb=48)(a, b)
    m, k = a.shape
    n = b.shape[0] if mode == "nt" else b.shape[1]
    tm, tn = _tile(m), _col_tile(n)
    dn = NT if mode == "nt" else (((1,), (0,)), ((), ()))

    def body(a_ref, b_ref, *rest):
        o_ref = rest[-1]
        acc = lax.dot_general(a_ref[...].astype(BF16), b_ref[...].astype(BF16), dn, preferred_element_type=F32)
        if add is not None:
            acc = acc + rest[0][...]
        o_ref[...] = acc.astype(out_dtype)

    b_spec = (pl.BlockSpec((tn, k), lambda i, j: (j, 0)) if mode == "nt"
              else pl.BlockSpec((k, tn), lambda i, j: (0, j)))
    in_specs = [pl.BlockSpec((tm, k), lambda i, j: (i, 0)), b_spec]
    args = [a, b]
    if add is not None:
        in_specs.append(pl.BlockSpec((tm, tn), lambda i, j: (i, j)))
        args.append(add)
    return _call(body, name=name, grid=(m // tm, n // tn), in_specs=in_specs,
                 out_specs=pl.BlockSpec((tm, tn), lambda i, j: (i, j)),
                 out_shape=SDS((m, n), out_dtype), dims=("parallel", "parallel"), vmem_mb=48)(*args)


def _mm_norm_bwd(a, b, x, gain, dres, name, add=None):
    m, k = a.shape
    n = b.shape[1]
    tm = _tile(m)

    def body(a_ref, b_ref, x_ref, g_ref, dres_ref, *rest):
        dx_ref, dg_ref = rest[-2:]
        dh = jnp.dot(a_ref[...].astype(BF16), b_ref[...].astype(BF16), preferred_element_type=F32)
        if add is not None:
            dh = dh + rest[0][...]
        xv = x_ref[...]
        r = lax.rsqrt(jnp.mean(xv * xv, axis=-1, keepdims=True) + EPS)
        xh = xv * r

        @pl.when(pl.program_id(0) == 0)
        def _():
            dg_ref[...] = jnp.zeros_like(dg_ref)

        dg_ref[...] += jnp.sum(dh * xh, axis=0, keepdims=True)
        dxh = dh * g_ref[...]
        dx_ref[...] = dres_ref[...] + r * (dxh - xh * jnp.mean(dxh * xh, axis=-1, keepdims=True))

    in_specs = [_rows(tm, k), _const((k, n)), _rows(tm, n), _const((1, n)), _rows(tm, n)]
    args = [a, b, x, gain, dres]
    if add is not None:
        in_specs.append(_rows(tm, n))
        args.append(add)
    return _call(body, name=name, grid=(m // tm,), in_specs=in_specs, out_specs=[_rows(tm, n), _const((1, n))],
                 out_shape=[SDS((m, n), F32), SDS((1, n), F32)], dims=("arbitrary",), vmem_mb=48)(*args)


def _mm_tn(a, b, name, out_dtype=F32):
    return _mm(a, b, "tn", name, out_dtype)


FFN_TM = 256
FFN_FWD_TM = 512
FFN_CHUNK = 256


def _load_ffn_weights(w_hbm, offs, fs, dsts, sems):
    copies = []
    for wi, (off, dst) in enumerate(zip(offs, dsts)):
        for j in range(N_DEV):
            cp = pltpu.make_async_copy(w_hbm.at[j, pl.ds(off, fs), :], dst.at[pl.ds(j * fs, fs), :],
                                       sems.at[wi * N_DEV + j])
            cp.start()
            copies.append(cp)
    for cp in copies:
        cp.wait()


def _ffn_fwd(x, gain, wall, offs, fs, name, gather_src=None):
    s, d = x.shape
    f = fs * N_DEV
    tm, ch = _tile(s, FFN_FWD_TM), FFN_CHUNK
    n = s // tm
    gathers = gather_src is not None

    def body(x_ref, gain_ref, w_hbm, *rest):
        if gathers:
            (src_hbm, out_ref, xn_ref, g_ref, u_ref, h_ref, gathered, wg_s, wu_s, wd_s, sems,
             send_sems, recv_sems, local_sem) = rest
            start, forward, finish = _gather_phases(src_hbm, gathered, send_sems, recv_sems, local_sem)
            pl.when(pl.program_id(0) == 0)(start)
            pl.when(pl.program_id(0) == (7 * n) // 8)(forward)
        else:
            out_ref, xn_ref, g_ref, u_ref, h_ref, wg_s, wu_s, wd_s, sems = rest

        @pl.when(pl.program_id(0) == 0)
        def _():
            _load_ffn_weights(w_hbm, offs, fs, (wg_s, wu_s, wd_s), sems)

        xv = x_ref[...]
        xnv = (xv * lax.rsqrt(jnp.mean(xv * xv, axis=-1, keepdims=True) + EPS) * gain_ref[...]).astype(BF16)
        xn_ref[...] = xnv
        acc = jnp.zeros((tm, d), F32)
        for c in range(f // ch):
            sl = slice(c * ch, (c + 1) * ch)
            gb = lax.dot_general(xnv, wg_s[sl, :], NT, preferred_element_type=F32).astype(BF16)
            ub = lax.dot_general(xnv, wu_s[sl, :], NT, preferred_element_type=F32).astype(BF16)
            g_ref[:, sl] = gb
            u_ref[:, sl] = ub
            g = gb.astype(F32)
            hb = (g * _sigmoid(g) * ub.astype(F32)).astype(BF16)
            h_ref[:, sl] = hb
            acc = acc + jnp.dot(hb, wd_s[sl, :], preferred_element_type=F32)
        out_ref[...] = xv + FFN_RES * acc
        if gathers:
            pl.when(pl.program_id(0) == n - 1)(finish)

    in_specs, args = [_rows(tm, d), _const((1, d)), ANY], [x, gain, wall]
    out_specs = [_rows(tm, d), _rows(tm, d), _rows(tm, f), _rows(tm, f), _rows(tm, f)]
    out_shape = [SDS((s, d), F32), SDS((s, d), BF16), SDS((s, f), BF16), SDS((s, f), BF16), SDS((s, f), BF16)]
    scratch = [pltpu.VMEM((f, d), BF16)] * 3 + [pltpu.SemaphoreType.DMA((3 * N_DEV,))]
    if gathers:
        in_specs.append(ANY)
        args.append(gather_src)
        out_specs.append(ANY)
        out_shape.append(SDS((N_DEV,) + gather_src.shape, gather_src.dtype))
        scratch += GATHER_SEMS
    return _call(body, name=name, grid=(n,), in_specs=in_specs, out_specs=out_specs, out_shape=out_shape,
                 scratch=scratch, dims=("arbitrary",), vmem_mb=56)(*args)


def _ffn_bwd_act(dout, g, u, x, gain, wall, offs, fs, name, exchange_src=None):
    s, d = dout.shape
    f = fs * N_DEV
    tm, ch = _tile(s, FFN_TM), FFN_CHUNK
    n = s // tm
    exchanges = exchange_src is not None
    if exchanges:
        phases_of, shape_of, exchange_sems = EXCHANGES[exchange_src[0]]

    def body(dout_ref, g_ref, u_ref, x_ref, gain_ref, w_hbm, *rest):
        if exchanges:
            (t_hbm, dg_ref, du_ref, dy_ref, dx_ref, dgain_ref, rcv_ref, wg_s, wu_s, wd_s, sems,
             send_sems, recv_sems) = rest
            start, finish = phases_of(t_hbm, rcv_ref, send_sems, recv_sems)
            pl.when(pl.program_id(0) == 0)(start)
        else:
            dg_ref, du_ref, dy_ref, dx_ref, dgain_ref, wg_s, wu_s, wd_s, sems = rest

        @pl.when(pl.program_id(0) == 0)
        def _():
            _load_ffn_weights(w_hbm, offs, fs, (wg_s, wu_s, wd_s), sems)
            dgain_ref[...] = jnp.zeros_like(dgain_ref)

        doutv = dout_ref[...]
        dy = (FFN_RES * doutv).astype(BF16)
        dy_ref[...] = dy
        acc = jnp.zeros((tm, d), F32)
        for c in range(f // ch):
            sl = slice(c * ch, (c + 1) * ch)
            dh = lax.dot_general(dy, wd_s[sl, :], NT, preferred_element_type=F32)
            gv = g_ref[:, sl].astype(F32)
            uv = u_ref[:, sl].astype(F32)
            sg = _sigmoid(gv)
            dgb = (dh * uv * sg * (1.0 + gv * (1.0 - sg))).astype(BF16)
            dub = (dh * gv * sg).astype(BF16)
            dg_ref[:, sl] = dgb
            du_ref[:, sl] = dub
            acc = acc + jnp.dot(dgb, wg_s[sl, :], preferred_element_type=F32)
            acc = acc + jnp.dot(dub, wu_s[sl, :], preferred_element_type=F32)
        xv = x_ref[...]
        r = lax.rsqrt(jnp.mean(xv * xv, axis=-1, keepdims=True) + EPS)
        xh = xv * r
        dgain_ref[...] += jnp.sum(acc * xh, axis=0, keepdims=True)
        dxh = acc * gain_ref[...]
        dx_ref[...] = doutv + r * (dxh - xh * jnp.mean(dxh * xh, axis=-1, keepdims=True))
        if exchanges:
            pl.when(pl.program_id(0) == n - 1)(finish)

    in_specs = [_rows(tm, d), _rows(tm, f), _rows(tm, f), _rows(tm, d), _const((1, d)), ANY]
    args = [dout, g, u, x, gain, wall]
    out_specs = [_rows(tm, f), _rows(tm, f), _rows(tm, d), _rows(tm, d), _const((1, d))]
    out_shape = [SDS((s, f), BF16), SDS((s, f), BF16), SDS((s, d), BF16), SDS((s, d), F32), SDS((1, d), F32)]
    scratch = [pltpu.VMEM((f, d), BF16)] * 3 + [pltpu.SemaphoreType.DMA((3 * N_DEV,))]
    if exchanges:
        in_specs.append(ANY)
        args.append(exchange_src[1])
        out_specs.append(ANY)
        out_shape.append(shape_of(exchange_src[1]))
        scratch += exchange_sems
    return _call(body, name=name, grid=(n,), in_specs=in_specs, out_specs=out_specs, out_shape=out_shape,
                 scratch=scratch, dims=("arbitrary",), vmem_mb=56)(*args)


def _prev_rows(halo, tm, c, col):
    return pl.BlockSpec((halo, c), lambda i: (jnp.maximum(i * (tm // halo) - 1, 0), col))


def _next_rows(halo, tm, c, col, n_blocks):
    return pl.BlockSpec((halo, c), lambda i: (jnp.minimum((i + 1) * (tm // halo), n_blocks - 1), col))


def _conv_a_fwd(z, cw, cb, cn, name):
    s = z.shape[0]
    c = cb.shape[1]
    tm, halo, kw = _tile(s), CONV_A_HALO, CONV_A_WIDTH

    def body(u_ref, g_ref, up_ref, gp_ref, cw_ref, cb_ref, cn_ref, a_ref, a1_ref, buf):
        i = pl.program_id(0)
        buf[0:halo, :] = jnp.where(i > 0, up_ref[...] * _sigmoid(gp_ref[...]), 0.0)
        buf[halo:halo + tm, :] = u_ref[...] * _sigmoid(g_ref[...])
        acc = jnp.zeros((tm, c), F32)
        for k in range(kw):
            acc = acc + cw_ref[k:k + 1, :] * buf[pl.ds(halo - (kw - 1) + k, tm), :]
        a1 = acc + cb_ref[...]
        a1_ref[...] = a1
        a2 = a1 * lax.rsqrt(jnp.mean(a1 * a1, axis=-1, keepdims=True) + EPS) * cn_ref[...]
        a_ref[...] = (a2 * _sigmoid(a2)).astype(BF16)

    return _call(body, name=name, grid=(s // tm,),
                 in_specs=[_rows(tm, c, 0), _rows(tm, c, 1), _prev_rows(halo, tm, c, 0), _prev_rows(halo, tm, c, 1),
                           _const(cw.shape), _const((1, c)), _const((1, c))],
                 out_specs=[_rows(tm, c), _rows(tm, c)],
                 out_shape=[SDS((s, c), BF16), SDS((s, c), F32)],
                 scratch=[pltpu.VMEM((tm + halo, c), F32)], dims=("parallel",))(z, z, z, z, cw, cb, cn)


def _conv_a_bwd_norm(dao, a1, cn, name):
    s, c = a1.shape
    tm = _tile(s)

    def body(da_ref, a1_ref, cn_ref, da1_ref, dcn_ref, dcb_ref):
        a1v = a1_ref[...]
        r = lax.rsqrt(jnp.mean(a1v * a1v, axis=-1, keepdims=True) + EPS)
        xh = a1v * r
        a2 = xh * cn_ref[...]
        sg = _sigmoid(a2)
        da2 = da_ref[...] * sg * (1.0 + a2 * (1.0 - sg))
        dxh = da2 * cn_ref[...]
        da1 = r * (dxh - xh * jnp.mean(dxh * xh, axis=-1, keepdims=True))
        da1_ref[...] = da1

        @pl.when(pl.program_id(0) == 0)
        def _():
            dcn_ref[...] = jnp.zeros_like(dcn_ref)
            dcb_ref[...] = jnp.zeros_like(dcb_ref)

        dcn_ref[...] += jnp.sum(da2 * xh, axis=0, keepdims=True)
        dcb_ref[...] += jnp.sum(da1, axis=0, keepdims=True)

    return _call(body, name=name, grid=(s // tm,),
                 in_specs=[_rows(tm, c, 0), _rows(tm, c), _const((1, c))],
                 out_specs=[_rows(tm, c), _const((1, c)), _const((1, c))],
                 out_shape=[SDS((s, c), F32), SDS((1, c), F32), SDS((1, c), F32)], dims=("arbitrary",))(dao, a1, cn)


def _conv_a_bwd_conv(da1, z, cw, name, exchange_src=None):
    s, c = da1.shape
    tm, halo, kw = _tile(s), CONV_A_HALO, CONV_A_WIDTH
    n = s // tm
    exchanges = exchange_src is not None
    if exchanges:
        phases_of, shape_of, exchange_sems = EXCHANGES[exchange_src[0]]

    def body(d_ref, dn_ref, u_ref, g_ref, up_ref, gp_ref, cw_ref, *rest):
        i = pl.program_id(0)
        if exchanges:
            t_hbm, du_ref, dg_ref, dcw_ref, rcv_ref, buf, bd, send_sems, recv_sems = rest
            start, finish = phases_of(t_hbm, rcv_ref, send_sems, recv_sems)
            pl.when(i == 0)(start)
        else:
            du_ref, dg_ref, dcw_ref, buf, bd = rest
        uv = u_ref[...]
        sg = _sigmoid(g_ref[...])
        buf[0:halo, :] = jnp.where(i > 0, up_ref[...] * _sigmoid(gp_ref[...]), 0.0)
        buf[halo:halo + tm, :] = uv * sg
        dv = d_ref[...]
        bd[0:tm, :] = dv
        bd[tm:tm + halo, :] = jnp.where(i < n - 1, dn_ref[...], 0.0)

        @pl.when(i == 0)
        def _():
            dcw_ref[...] = jnp.zeros_like(dcw_ref)

        da0 = jnp.zeros((tm, c), F32)
        for k in range(kw):
            da0 = da0 + cw_ref[k:k + 1, :] * bd[pl.ds(kw - 1 - k, tm), :]
            dcw_ref[k:k + 1, :] += jnp.sum(dv * buf[pl.ds(halo - (kw - 1) + k, tm), :], axis=0, keepdims=True)
        du_ref[...] = (da0 * sg).astype(BF16)
        dg_ref[...] = (da0 * uv * sg * (1.0 - sg)).astype(BF16)
        if exchanges:
            pl.when(i == n - 1)(finish)

    in_specs = [_rows(tm, c), _next_rows(halo, tm, c, 0, s // halo), _rows(tm, c, 0), _rows(tm, c, 1),
                _prev_rows(halo, tm, c, 0), _prev_rows(halo, tm, c, 1), _const(cw.shape)]
    args = [da1, da1, z, z, z, z, cw]
    out_specs = [_rows(tm, c), _rows(tm, c), _const(cw.shape)]
    out_shape = [SDS((s, c), BF16), SDS((s, c), BF16), SDS(cw.shape, F32)]
    scratch = [pltpu.VMEM((tm + halo, c), F32)] * 2
    if exchanges:
        in_specs.append(ANY)
        args.append(exchange_src[1])
        out_specs.append(ANY)
        out_shape.append(shape_of(exchange_src[1]))
        scratch += exchange_sems
    return _call(body, name=name, grid=(n,), in_specs=in_specs, out_specs=out_specs, out_shape=out_shape,
                 scratch=scratch, dims=("arbitrary",))(*args)


def _lane_is_first_head(tm):
    return lax.broadcasted_iota(jnp.int32, (tm, PAIR), 1) < HEAD_DIM


def _pair_rms(xp, first):
    x2 = xp * xp
    s0 = jnp.sum(jnp.where(first, x2, 0.0), axis=-1, keepdims=True)
    s1 = jnp.sum(jnp.where(first, 0.0, x2), axis=-1, keepdims=True)
    return jnp.where(first, lax.rsqrt(s0 / HEAD_DIM + EPS), lax.rsqrt(s1 / HEAD_DIM + EPS))


def _split3(x):
    hi = x.astype(BF16)
    r1 = x - hi.astype(F32)
    mid = r1.astype(BF16)
    lo = (r1 - mid.astype(F32)).astype(BF16)
    return hi, mid, lo


def _qk_fwd(z, flog, bf, qn2, kn2, name):
    s = z.shape[0]
    tm = _tile(s)
    col0 = (z.shape[1] - 3 * D_ATTN) // D_ATTN

    def body(q_ref, k_ref, v_ref, fl_ref, bf_ref, qn_ref, kn_ref,
             qs_ref, kh_ref, vb_ref, fb_ref, ft_ref, xt_ref, carry):
        i = pl.program_id(0)
        first = _lane_is_first_head(tm)
        for p in range(N_PAIRS):
            sl = slice(p * PAIR, (p + 1) * PAIR)
            q = q_ref[:, sl]
            qs_ref[:, sl] = (q * _pair_rms(q, first) * qn_ref[...] * ATTN_SCALE).astype(BF16)
            k = k_ref[:, sl]
            kh_ref[:, sl] = (k * _pair_rms(k, first) * kn_ref[...]).astype(BF16)
        vb_ref[...] = v_ref[...].astype(BF16)

        xg = fl_ref[...] + bf_ref[...]
        valid = lax.broadcasted_iota(jnp.int32, (tm, V7X_LANES), 1) < N_HEADS
        ls = jnp.where(valid, jnp.minimum(xg, 0.0) - jnp.log(1.0 + jnp.exp(-jnp.abs(xg))), 0.0)
        tri = (lax.broadcasted_iota(jnp.int32, (tm, tm), 1) <= lax.broadcasted_iota(jnp.int32, (tm, tm), 0)).astype(BF16)
        cs = jnp.zeros((tm, V7X_LANES), F32)
        for part in _split3(ls):
            cs = cs + jnp.dot(tri, part, preferred_element_type=F32)

        @pl.when(i == 0)
        def _():
            carry[...] = jnp.zeros_like(carry)

        fv = cs + carry[0:1, :]
        carry[0:1, :] = fv[tm - 1:tm, :]
        ft_ref[...] = fv.T[0:N_HEADS, :]
        xt_ref[...] = xg.T[0:N_HEADS, :]
        for p in range(N_PAIRS):
            fb_ref[:, p * PAIR:(p + 1) * PAIR] = jnp.where(first, fv[:, 2 * p:2 * p + 1], fv[:, 2 * p + 1:2 * p + 2])

    wide = lambda col: pl.BlockSpec((tm, D_ATTN), lambda i: (i, col))
    tcol = pl.BlockSpec((N_HEADS, tm), lambda i: (0, i))
    return _call(body, name=name, grid=(s // tm,),
                 in_specs=[wide(col0), wide(col0 + 1), wide(col0 + 2), _rows(tm, V7X_LANES),
                           _const((1, V7X_LANES)), _const((1, PAIR)), _const((1, PAIR))],
                 out_specs=[wide(0), wide(0), wide(0), wide(0), tcol, tcol],
                 out_shape=[SDS((s, D_ATTN), BF16)] * 3 + [SDS((s, D_ATTN), F32), SDS((N_HEADS, s), F32),
                                                          SDS((N_HEADS, s), F32)],
                 scratch=[pltpu.VMEM((8, V7X_LANES), F32)], dims=("arbitrary",))(z, z, z, flog, bf, qn2, kn2)


ATTN_FWD_SUB = 256
ATTN_BWD_SUB = 512


def _causal_schedule(nq, key_major):
    if key_major:
        pairs = [(i, j) for j in range(nq) for i in range(j, nq)]
    else:
        pairs = [(i, j) for i in range(nq) for j in range(i + 1)]
    return (jnp.asarray([p[0] for p in pairs], jnp.int32), jnp.asarray([p[1] for p in pairs], jnp.int32))


def _sub_scores(qp, kp, ft_row, mine, r, masked, sub, tk):
    qm = jnp.where(mine, qp, jnp.zeros_like(qp))
    s2 = lax.dot_general(qm, kp, NT, preferred_element_type=F32) - ft_row
    if masked:
        row = r * sub + lax.broadcasted_iota(jnp.int32, (sub, tk), 0)
        s2 = jnp.where(lax.broadcasted_iota(jnp.int32, (sub, tk), 1) <= row, s2, NEG_BIG)
    return s2


def _attn_fwd(qs, kh, vb, fb, ft, name, gather_src=None):
    s = qs.shape[0]
    tq = tk = _tile(s)
    nq = s // tq
    sub = min(ATTN_FWD_SUB, tq)
    ii, jj = _causal_schedule(nq, key_major=False)
    n_steps = ii.shape[0]
    gathers = gather_src is not None

    def body(ii_ref, jj_ref, q_ref, k_ref, v_ref, fq_ref, ft_ref, *rest):
        if gathers:
            x_hbm, o_ref, lse_ref, wall_ref, m_s, l_s, acc_s, send_sems, recv_sems, local_sem = rest
        else:
            o_ref, lse_ref, m_s, l_s, acc_s = rest
        p, t = pl.program_id(0), pl.program_id(1)
        i, j = ii_ref[t], jj_ref[t]
        first = _lane_is_first_head(sub)
        if gathers:
            start, forward, finish = _gather_phases(x_hbm, wall_ref, send_sems, recv_sems, local_sem)
            pl.when(jnp.logical_and(p == 0, t == 0))(start)
            pl.when(jnp.logical_and(p == N_PAIRS - 1, t == 0))(forward)

        @pl.when(j == 0)
        def _():
            m_s[...] = jnp.full_like(m_s, NEG_BIG)
            l_s[...] = jnp.zeros_like(l_s)
            acc_s[...] = jnp.zeros_like(acc_s)

        def tile(masked):
            kp, vp = k_ref[...], v_ref[...]
            q_all, fq_all, acc_all = q_ref[...], fq_ref[...], acc_s[...]
            m_all, l_all = (m_s[0], m_s[1]), (l_s[0], l_s[1])
            ft_rows = [ft_ref[pl.ds(2 * p + h, 1), :] for h in range(2)]
            m_out, l_out, acc_out = ([], []), ([], []), []
            for r in range(tq // sub):
                rows = slice(r * sub, (r + 1) * sub)
                qp, fq, acc = q_all[rows, :], fq_all[rows, :], acc_all[rows, :]
                kc = (r + 1) * sub if masked else tk
                new = []
                for h in range(2):
                    mine = first if h == 0 else jnp.logical_not(first)
                    s2 = _sub_scores(qp, kp[:kc, :], ft_rows[h][:, :kc], mine, r, masked, sub, kc)
                    fqh = fq[:, h * HEAD_DIM:h * HEAD_DIM + 1]
                    m_old = m_all[h][rows, :]
                    m_new = jnp.maximum(m_old, jnp.max(s2, axis=-1, keepdims=True) + fqh)
                    pr = jnp.exp(s2 - (m_new - fqh))
                    alpha = jnp.exp(m_old - m_new)
                    l_out[h].append(alpha * l_all[h][rows, :] + jnp.sum(pr, axis=-1, keepdims=True))
                    m_out[h].append(m_new)
                    new.append(alpha * acc + jnp.dot(pr.astype(BF16), vp[:kc, :], preferred_element_type=F32))
                acc_out.append(jnp.where(first, new[0], new[1]))
            for h in range(2):
                m_s[h] = jnp.concatenate(m_out[h], axis=0)
                l_s[h] = jnp.concatenate(l_out[h], axis=0)
            acc_s[...] = jnp.concatenate(acc_out, axis=0)

        @pl.when(j < i)
        def _():
            tile(False)

        @pl.when(j == i)
        def _():
            tile(True)
            whole = _lane_is_first_head(tq)
            l_pair = jnp.where(whole, l_s[0], l_s[1])
            o_ref[...] = acc_s[...] / l_pair
            lse_ref[...] = jnp.where(whole, m_s[0], m_s[1]) + jnp.log(l_pair)

        if gathers:
            pl.when(jnp.logical_and(p == N_PAIRS - 1, t == n_steps - 1))(finish)

    qblk = pl.BlockSpec((tq, PAIR), lambda p, t, ii_r, jj_r: (ii_r[t], p))
    kblk = pl.BlockSpec((tk, PAIR), lambda p, t, ii_r, jj_r: (jj_r[t], p))
    in_specs = [qblk, kblk, kblk, qblk, pl.BlockSpec((N_HEADS, tk), lambda p, t, ii_r, jj_r: (0, jj_r[t]))]
    out_specs, out_shape = [qblk, qblk], [SDS((s, D_ATTN), F32)] * 2
    scratch = [pltpu.VMEM((2, tq, 1), F32), pltpu.VMEM((2, tq, 1), F32), pltpu.VMEM((tq, PAIR), F32)]
    args = [ii, jj, qs, kh, vb, fb, ft]
    if gathers:
        in_specs.append(ANY)
        out_specs.append(ANY)
        out_shape.append(SDS((N_DEV,) + gather_src.shape, gather_src.dtype))
        scratch += GATHER_SEMS
        args.append(gather_src)
    grid_spec = pltpu.PrefetchScalarGridSpec(num_scalar_prefetch=2, grid=(N_PAIRS, n_steps), in_specs=in_specs,
                                             out_specs=out_specs, scratch_shapes=scratch)
    return pl.pallas_call(
        body, name=name, grid_spec=grid_spec, out_shape=_in_hbm(out_shape),
        compiler_params=pltpu.CompilerParams(dimension_semantics=("arbitrary", "arbitrary"),
                                             vmem_limit_bytes=32 * MIB))(*[_keep_in_hbm(a) for a in args])


def _attn_bwd(qs, kh, vb, fb, ft, lse, o, dao, name, exchange_src=None):
    s = qs.shape[0]
    tq = tk = _tile(s)
    nq = s // tq
    sub = min(ATTN_BWD_SUB, tq)
    ii, jj = _causal_schedule(nq, key_major=True)
    n_steps = ii.shape[0]

    exchanges = exchange_src is not None

    def body(ii_ref, jj_ref, q_ref, k_ref, v_ref, fq_ref, ft_ref, lse_ref, o_ref, do_ref, *rest):
        if exchanges:
            t_hbm, dq_ref, rs_ref, dk_ref, dv_ref, df_ref, rcv_ref, dk_s, dv_s, df_s, send_sems, recv_sems = rest
        else:
            dq_ref, rs_ref, dk_ref, dv_ref, df_ref, dk_s, dv_s, df_s = rest
        p, t = pl.program_id(0), pl.program_id(1)
        i, j = ii_ref[t], jj_ref[t]
        first = _lane_is_first_head(sub)
        first_k = _lane_is_first_head(tk)
        if exchanges:
            start, finish = _chip_exchange_phases(t_hbm, rcv_ref, send_sems, recv_sems)
            pl.when(jnp.logical_and(p == 0, t == 0))(start)

        @pl.when(t == 0)
        def _():
            dq_ref[...] = jnp.zeros_like(dq_ref)
            rs_ref[...] = jnp.zeros_like(rs_ref)

        @pl.when(i == j)
        def _():
            dk_s[...] = jnp.zeros_like(dk_s)
            dv_s[...] = jnp.zeros_like(dv_s)
            df_s[...] = jnp.zeros_like(df_s)

        def tile(masked):
            kp, vp = k_ref[...], v_ref[...]
            q_all, fq_all, lse_all, o_all, do_all = q_ref[...], fq_ref[...], lse_ref[...], o_ref[...], do_ref[...]
            ft_rows = [ft_ref[pl.ds(2 * p + h, 1), :] for h in range(2)]
            dq_out, rs_out = [], []
            dk_acc, dv_acc = jnp.zeros((tk, PAIR), F32), jnp.zeros((tk, PAIR), F32)
            df_acc = [jnp.zeros((1, tk), F32), jnp.zeros((1, tk), F32)]
            for r in range(tq // sub):
                rows = slice(r * sub, (r + 1) * sub)
                qp, fq, lse, ov, dall = q_all[rows, :], fq_all[rows, :], lse_all[rows, :], o_all[rows, :], do_all[rows, :]
                dq_h, dk_h, dv_h, rs_h = [], [], [], []
                for h in range(2):
                    mine = first if h == 0 else jnp.logical_not(first)
                    s2 = _sub_scores(qp, kp, ft_rows[h], mine, r, masked, sub, tk)
                    lane = slice(h * HEAD_DIM, h * HEAD_DIM + 1)
                    pr = jnp.exp(s2 - (lse[:, lane] - fq[:, lane]))
                    dov = jnp.where(mine, dall, 0.0)
                    dsum = jnp.sum(dov * ov, axis=-1, keepdims=True)
                    dom = dov.astype(BF16)
                    dom_lo = (dov - dom.astype(F32)).astype(BF16)
                    dp = lax.dot_general(dom, vp, NT, preferred_element_type=F32)
                    dp = dp + lax.dot_general(dom_lo, vp, NT, preferred_element_type=F32)
                    ds = pr * (dp - dsum)
                    dsb = ds.astype(BF16)
                    dq_h.append(jnp.dot(dsb, kp, preferred_element_type=F32))
                    dk_h.append(lax.dot_general(dsb, qp, TN, preferred_element_type=F32))
                    dv_h.append(lax.dot_general(pr.astype(BF16), dom, TN, preferred_element_type=F32))
                    rs_h.append(jnp.sum(ds, axis=-1, keepdims=True))
                    df_acc[h] = df_acc[h] - jnp.sum(ds, axis=0, keepdims=True)
                dq_out.append(jnp.where(first, dq_h[0], dq_h[1]))
                rs_out.append(jnp.where(first, rs_h[0], rs_h[1]))
                dk_acc = dk_acc + jnp.where(first_k, dk_h[0], dk_h[1])
                dv_acc = dv_acc + jnp.where(first_k, dv_h[0], dv_h[1])
            grows = pl.ds(pl.multiple_of(i * tq, tq), tq)
            dq_ref[grows, :] += jnp.concatenate(dq_out, axis=0)
            rs_ref[grows, :] += jnp.concatenate(rs_out, axis=0)
            dk_s[...] += dk_acc
            dv_s[...] += dv_acc
            for h in range(2):
                df_s[h:h + 1, :] += df_acc[h]

        @pl.when(j < i)
        def _():
            tile(False)

        @pl.when(j == i)
        def _():
            tile(True)

        @pl.when(i == nq - 1)
        def _():
            dk_ref[...] = dk_s[...]
            dv_ref[...] = dv_s[...]
            df_ref[0] = df_s[...]

        if exchanges:
            pl.when(jnp.logical_and(p == N_PAIRS - 1, t == n_steps - 1))(finish)

    qblk = pl.BlockSpec((tq, PAIR), lambda p, t, ii_r, jj_r: (ii_r[t], p))
    kblk = pl.BlockSpec((tk, PAIR), lambda p, t, ii_r, jj_r: (jj_r[t], p))
    doblk = pl.BlockSpec((tq, PAIR), lambda p, t, ii_r, jj_r: (ii_r[t], N_PAIRS + p))
    whole = pl.BlockSpec((s, PAIR), lambda p, t, ii_r, jj_r: (0, p))
    in_specs = [qblk, kblk, kblk, qblk, pl.BlockSpec((N_HEADS, tk), lambda p, t, ii_r, jj_r: (0, jj_r[t])),
                qblk, qblk, doblk]
    out_specs = [whole, whole, kblk, kblk, pl.BlockSpec((1, 8, tk), lambda p, t, ii_r, jj_r: (p, 0, jj_r[t]))]
    out_shape = [SDS((s, D_ATTN), F32)] * 4 + [SDS((N_PAIRS, 8, s), F32)]
    scratch = [pltpu.VMEM((tk, PAIR), F32), pltpu.VMEM((tk, PAIR), F32), pltpu.VMEM((8, tk), F32)]
    args = [ii, jj, qs, kh, vb, fb, ft, lse, o, dao]
    if exchanges:
        in_specs.append(ANY)
        out_specs.append(ANY)
        out_shape.append(SDS((3,) + exchange_src.shape[1:], exchange_src.dtype))
        scratch += EXCHANGE_SEMS
        args.append(exchange_src)
    grid_spec = pltpu.PrefetchScalarGridSpec(num_scalar_prefetch=2, grid=(N_PAIRS, n_steps), in_specs=in_specs,
                                             out_specs=out_specs, scratch_shapes=scratch)
    return pl.pallas_call(
        body, name=name, grid_spec=grid_spec, out_shape=_in_hbm(out_shape),
        compiler_params=pltpu.CompilerParams(dimension_semantics=("arbitrary", "arbitrary"),
                                             vmem_limit_bytes=40 * MIB))(*[_keep_in_hbm(a) for a in args])


def _qk_bwd(z, dqs, dkh, dv, qn2, kn2, name):
    s = z.shape[0]
    tm = _tile(s)
    col0 = (z.shape[1] - 3 * D_ATTN) // D_ATTN

    def body(q_ref, k_ref, dqs_ref, dkh_ref, dv_ref, qn_ref, kn_ref, dq_ref, dk_ref, dvb_ref, dqn_ref, dkn_ref):
        first = _lane_is_first_head(tm)

        @pl.when(pl.program_id(0) == 0)
        def _():
            dqn_ref[...] = jnp.zeros_like(dqn_ref)
            dkn_ref[...] = jnp.zeros_like(dkn_ref)

        def through(x_ref, dy_ref, gain_ref, dx_ref, dgain_ref, scale):
            for p in range(N_PAIRS):
                sl = slice(p * PAIR, (p + 1) * PAIR)
                xv = x_ref[:, sl]
                r = _pair_rms(xv, first)
                xh = xv * r
                dy = dy_ref[:, sl] * scale
                dgain_ref[:, sl] += jnp.sum(dy * xh, axis=0, keepdims=True)
                dxh = dy * gain_ref[...]
                t = dxh * xh
                m0 = jnp.sum(jnp.where(first, t, 0.0), axis=-1, keepdims=True)
                m1 = jnp.sum(jnp.where(first, 0.0, t), axis=-1, keepdims=True)
                mean = jnp.where(first, m0, m1) / HEAD_DIM
                dx_ref[:, sl] = (r * (dxh - xh * mean)).astype(BF16)

        through(q_ref, dqs_ref, qn_ref, dq_ref, dqn_ref, ATTN_SCALE)
        through(k_ref, dkh_ref, kn_ref, dk_ref, dkn_ref, 1.0)
        dvb_ref[...] = dv_ref[...].astype(BF16)

    wide = lambda col: pl.BlockSpec((tm, D_ATTN), lambda i: (i, col))
    return _call(body, name=name, grid=(s // tm,),
                 in_specs=[wide(col0), wide(col0 + 1), wide(0), wide(0), wide(0), _const((1, PAIR)), _const((1, PAIR))],
                 out_specs=[wide(0), wide(0), wide(0), _const((1, D_ATTN)), _const((1, D_ATTN))],
                 out_shape=[SDS((s, D_ATTN), BF16)] * 3 + [SDS((1, D_ATTN), F32)] * 2,
                 dims=("arbitrary",))(z, z, dqs, dkh, dv, qn2, kn2)


def _gate_bwd(dft, xt, name):
    s = xt.shape[1]
    tm = _tile(s)
    n = s // tm

    def body(df_ref, xt_ref, dxt_ref, dx_ref, db_ref, carry):
        i = pl.program_id(0)

        @pl.when(i == 0)
        def _():
            carry[...] = jnp.zeros_like(carry)
            db_ref[...] = jnp.zeros_like(db_ref)

        tri = (lax.broadcasted_iota(jnp.int32, (tm, tm), 0) >= lax.broadcasted_iota(jnp.int32, (tm, tm), 1)).astype(BF16)
        rc = jnp.zeros((N_HEADS, tm), F32)
        for part in _split3(df_ref[...]):
            rc = rc + jnp.dot(part, tri, preferred_element_type=F32)
        dls = rc + carry[:, 0:1]
        carry[...] = jnp.broadcast_to(dls[:, 0:1], carry.shape)
        dxt = dls * _sigmoid(-xt_ref[...])
        dxt_ref[...] = dxt
        db_ref[...] += jnp.broadcast_to(jnp.sum(dxt, axis=-1, keepdims=True), db_ref.shape)
        padded = jnp.concatenate([dxt, jnp.zeros((V7X_LANES - N_HEADS, tm), F32)], axis=0)
        dx_ref[...] = padded.T

    rev = pl.BlockSpec((N_HEADS, tm), lambda i: (0, n - 1 - i))
    return _call(body, name=name, grid=(n,), in_specs=[rev, rev],
                 out_specs=[rev, pl.BlockSpec((tm, V7X_LANES), lambda i: (n - 1 - i, 0)), _const((N_HEADS, V7X_LANES))],
                 out_shape=[SDS((N_HEADS, s), F32), SDS((s, V7X_LANES), F32), SDS((N_HEADS, V7X_LANES), F32)],
                 scratch=[pltpu.VMEM((N_HEADS, V7X_LANES), F32)], dims=("arbitrary",))(dft, xt)


def _conv_c_fwd(z, cw, name):
    s = z.shape[0]
    c = z.shape[1] // 3
    tm, halo, kw = _tile(s), CONV_C_HALO, CONV_C_WIDTH

    def body(gb_ref, gc_ref, hh_ref, gcp_ref, hhp_ref, cw_ref, y_ref, buf):
        i = pl.program_id(0)
        buf[0:halo, :] = jnp.where(i > 0, gcp_ref[...] * hhp_ref[...], 0.0)
        buf[halo:halo + tm, :] = gc_ref[...] * hh_ref[...]
        c1 = jnp.zeros((tm, c), F32)
        for k in range(kw):
            c1 = c1 + cw_ref[k:k + 1, :] * buf[pl.ds(halo - (kw - 1) + k, tm), :]
        y_ref[...] = (gb_ref[...] * c1).astype(BF16)

    return _call(body, name=name, grid=(s // tm,),
                 in_specs=[_rows(tm, c, 0), _rows(tm, c, 1), _rows(tm, c, 2), _prev_rows(halo, tm, c, 1),
                           _prev_rows(halo, tm, c, 2), _const(cw.shape)],
                 out_specs=_rows(tm, c), out_shape=SDS((s, c), BF16),
                 scratch=[pltpu.VMEM((tm + halo, c), F32)], dims=("parallel",))(z, z, z, z, z, cw)


def _conv_c_bwd(dy0, z, cw, name):
    s = z.shape[0]
    c = z.shape[1] // 3
    tm, halo, kw = _tile(s), CONV_C_HALO, CONV_C_WIDTH
    n = s // tm

    def body(dy_ref, dyn_ref, gb_ref, gbn_ref, gc_ref, hh_ref, gcp_ref, hhp_ref, cw_ref, dz_ref, dcw_ref, buf, bd):
        i = pl.program_id(0)
        gcv, hhv, dyv = gc_ref[...], hh_ref[...], dy_ref[...]
        buf[0:halo, :] = jnp.where(i > 0, gcp_ref[...] * hhp_ref[...], 0.0)
        buf[halo:halo + tm, :] = gcv * hhv
        dc1 = dyv * gb_ref[...]
        bd[0:tm, :] = dc1
        bd[tm:tm + halo, :] = jnp.where(i < n - 1, dyn_ref[...] * gbn_ref[...], 0.0)

        @pl.when(i == 0)
        def _():
            dcw_ref[...] = jnp.zeros_like(dcw_ref)

        c1 = jnp.zeros((tm, c), F32)
        dc0 = jnp.zeros((tm, c), F32)
        for k in range(kw):
            shifted = buf[pl.ds(halo - (kw - 1) + k, tm), :]
            c1 = c1 + cw_ref[k:k + 1, :] * shifted
            dc0 = dc0 + cw_ref[k:k + 1, :] * bd[pl.ds(kw - 1 - k, tm), :]
            dcw_ref[k:k + 1, :] += jnp.sum(dc1 * shifted, axis=0, keepdims=True)
        dz_ref[:, 0:c] = (dyv * c1).astype(BF16)
        dz_ref[:, c:2 * c] = (dc0 * hhv).astype(BF16)
        dz_ref[:, 2 * c:3 * c] = (dc0 * gcv).astype(BF16)

    return _call(body, name=name, grid=(n,),
                 in_specs=[_rows(tm, c), _next_rows(halo, tm, c, 0, s // halo), _rows(tm, c, 0),
                           _next_rows(halo, tm, c, 0, s // halo), _rows(tm, c, 1), _rows(tm, c, 2),
                           _prev_rows(halo, tm, c, 1), _prev_rows(halo, tm, c, 2), _const(cw.shape)],
                 out_specs=[_rows(tm, 3 * c), _const(cw.shape)],
                 out_shape=[SDS((s, 3 * c), BF16), SDS(cw.shape, F32)],
                 scratch=[pltpu.VMEM((tm + halo, c), F32)] * 2, dims=("arbitrary",),
                 vmem_mb=48)(dy0, dy0, z, z, z, z, z, z, cw)


def _loss_head(y, target, name):
    s, d = y.shape
    tm = _tile(s)

    def body(y_ref, t_ref, loss_ref, dy_ref):
        e = y_ref[...] - t_ref[...]

        @pl.when(pl.program_id(0) == 0)
        def _():
            loss_ref[...] = jnp.zeros_like(loss_ref)

        loss_ref[...] += 0.5 * jnp.sum(jnp.mean(e * e, axis=-1, keepdims=True))
        dy_ref[...] = e / d

    return _call(body, name=name, grid=(s // tm,), in_specs=[_rows(tm, d), _rows(tm, d)],
                 out_specs=[_const((8, V7X_LANES)), _rows(tm, d)],
                 out_shape=[SDS((8, V7X_LANES), F32), SDS((s, d), F32)], dims=("arbitrary",))(y, target)


def _adamw(w, g, m, v, name):
    r, c = w.shape
    tr = next((t for t in (512, 256, 128, 64, 32, 16, 8) if r % t == 0), r)

    def body(w_ref, g_ref, m_ref, v_ref, d_ref, mo_ref, vo_ref):
        gv = g_ref[...]
        mn = ADAM_B1 * m_ref[...] + (1.0 - ADAM_B1) * gv
        vn = ADAM_B2 * v_ref[...] + (1.0 - ADAM_B2) * (gv * gv)
        m_hat = mn / (1.0 - ADAM_B1 ** ADAM_STEP)
        v_hat = vn / (1.0 - ADAM_B2 ** ADAM_STEP)
        d_ref[...] = -ADAM_LR * (m_hat / (jnp.sqrt(v_hat) + ADAM_EPS) + ADAM_WD * w_ref[...])
        mo_ref[...] = mn
        vo_ref[...] = vn

    spec = _rows(tr, c)
    return _call(body, name=name, grid=(r // tr,), in_specs=[spec] * 4, out_specs=[spec] * 3,
                 out_shape=[SDS((r, c), F32)] * 3, dims=("parallel",))(w, g, m, v)


def _position():
    return lax.axis_index("x"), lax.axis_index("y"), lax.axis_index("c")


def _other_chips(x, y):
    return [(1 - x, y), (x, 1 - y), (1 - x, 1 - y)]


def _dev_index(px, py, pc):
    return 4 * px + 2 * py + pc


def _all_gather(wloc):
    r, d = wloc.shape

    def body(x_ref, out_ref, send_sems, recv_sems, local_sem):
        start, forward, finish = _gather_phases(x_ref, out_ref, send_sems, recv_sems, local_sem)
        start()
        forward()
        finish()

    return _call(body, name="all_gather_weights", in_specs=[ANY], out_specs=ANY,
                 out_shape=SDS((N_DEV, r, d), wloc.dtype), scratch=GATHER_SEMS)(wloc)


GATHER_SEMS = [pltpu.SemaphoreType.DMA((7,)), pltpu.SemaphoreType.DMA((7,)), pltpu.SemaphoreType.DMA((1,))]


def _gather_phases(x_ref, out_ref, send_sems, recv_sems, local_sem):
    x, y, c = _position()
    me, sibling = (x, y, c), (x, y, 1 - c)
    chips = _other_chips(x, y)

    def slot(dev):
        return out_ref.at[_dev_index(*dev)]

    def copy(k, block, to, src=None):
        return pltpu.make_async_remote_copy(
            src_ref=slot(block) if src is None else src, dst_ref=slot(block),
            send_sem=send_sems.at[k], recv_sem=recv_sems.at[k], device_id=to, device_id_type=MESH)

    mine = pltpu.make_async_copy(x_ref, slot(me), local_sem.at[0])
    first = [copy(0, me, sibling, src=x_ref)] + [copy(1 + j, me, (*chip, c), src=x_ref) for j, chip in enumerate(chips)]
    passed = [copy(4 + j, (*chip, c), sibling) for j, chip in enumerate(chips)]

    def start():
        mine.start()
        for cp in first:
            cp.start()

    def forward():
        for j, chip in enumerate(chips):
            copy(1 + j, (*chip, c), me).wait_recv()
            passed[j].start()

    def finish():
        copy(0, sibling, me).wait_recv()
        for j, chip in enumerate(chips):
            copy(4 + j, (*chip, 1 - c), me).wait_recv()
        for cp in first + passed:
            cp.wait_send()
        mine.wait()

    return start, forward, finish


def _row_block(r):
    return next(t for t in range(704, 0, -BF16_ROWS) if r % t == 0)


def _pair_exchange(gall, name):
    def body(g_ref, out_ref, send_sems, recv_sems):
        start, finish = _pair_exchange_phases(g_ref, out_ref, send_sems, recv_sems)
        start()
        finish()

    return _call(body, name=name, in_specs=[ANY], out_specs=ANY, out_shape=_pair_exchange_shape(gall),
                 scratch=PAIR_EXCHANGE_SEMS)(gall)


PAIR_EXCHANGE_SEMS = [pltpu.SemaphoreType.DMA((4,)), pltpu.SemaphoreType.DMA((4,))]


def _pair_exchange_shape(gall):
    return SDS((4,) + gall.shape[1:], gall.dtype)


def _pair_exchange_phases(g_ref, out_ref, send_sems, recv_sems):
    x, y, c = _position()
    sibling = (x, y, 1 - c)
    dests = [sibling] + [(*chip, 1 - c) for chip in _other_chips(x, y)]
    copies = [pltpu.make_async_remote_copy(
        src_ref=g_ref.at[_dev_index(*dest)], dst_ref=out_ref.at[k], send_sem=send_sems.at[k],
        recv_sem=recv_sems.at[k], device_id=sibling, device_id_type=MESH) for k, dest in enumerate(dests)]

    def start():
        for cp in copies:
            cp.start()

    def finish():
        for cp in copies:
            cp.wait()

    return start, finish


def _pair_sum(gall, sib, idx, name):
    _, r, d = gall.shape
    tr = _row_block(r)

    def body(idx_ref, a_ref, b_ref, o_ref):
        o_ref[...] = (a_ref[...].astype(F32) + b_ref[...].astype(F32)).astype(o_ref.dtype)

    grid_spec = pltpu.PrefetchScalarGridSpec(
        num_scalar_prefetch=1, grid=(4, r // tr),
        in_specs=[pl.BlockSpec((1, tr, d), lambda k, i, idx_ref: (idx_ref[k], i, 0)),
                  pl.BlockSpec((1, tr, d), lambda k, i, idx_ref: (k, i, 0))],
        out_specs=pl.BlockSpec((1, tr, d), lambda k, i, idx_ref: (k, i, 0)))
    return pl.pallas_call(body, name=name, grid_spec=grid_spec,
                          out_shape=_in_hbm(SDS((4, r, d), gall.dtype)),
                          compiler_params=pltpu.CompilerParams(dimension_semantics=("parallel", "parallel")))(
        idx, _keep_in_hbm(gall), _keep_in_hbm(sib))


def _chip_exchange(tsum):
    _, r, d = tsum.shape

    def body(t_ref, out_ref, send_sems, recv_sems):
        start, finish = _chip_exchange_phases(t_ref, out_ref, send_sems, recv_sems)
        start()
        finish()

    return _call(body, name="reduce_scatter_chip_exchange", in_specs=[ANY], out_specs=ANY,
                 out_shape=SDS((3, r, d), tsum.dtype), scratch=EXCHANGE_SEMS)(tsum)


EXCHANGE_SEMS = [pltpu.SemaphoreType.DMA((3,)), pltpu.SemaphoreType.DMA((3,))]


def _chip_exchange_phases(t_ref, out_ref, send_sems, recv_sems):
    x, y, c = _position()
    copies = [pltpu.make_async_remote_copy(
        src_ref=t_ref.at[1 + k], dst_ref=out_ref.at[k], send_sem=send_sems.at[k], recv_sem=recv_sems.at[k],
        device_id=(*chip, c), device_id_type=MESH) for k, chip in enumerate(_other_chips(x, y))]

    def start():
        for cp in copies:
            cp.start()

    def finish():
        for cp in copies:
            cp.wait()

    return start, finish


def _chip_exchange_shape(tsum):
    return SDS((3,) + tsum.shape[1:], tsum.dtype)


EXCHANGES = dict(pair=(_pair_exchange_phases, _pair_exchange_shape, PAIR_EXCHANGE_SEMS),
                 chip=(_chip_exchange_phases, _chip_exchange_shape, EXCHANGE_SEMS))


def _final_sum(tsum, rcv, name):
    _, r, d = tsum.shape
    tr = _row_block(r)

    def body(t_ref, r_ref, o_ref):
        acc = t_ref[0].astype(F32)
        for k in range(3):
            acc = acc + r_ref[k].astype(F32)
        o_ref[...] = acc

    return _call(body, name=name, grid=(r // tr,),
                 in_specs=[pl.BlockSpec((1, tr, d), lambda i: (0, i, 0)), pl.BlockSpec((3, tr, d), lambda i: (0, i, 0))],
                 out_specs=_rows(tr, d), out_shape=SDS((r, d), F32), dims=("parallel",))(tsum, rcv)


def _all_reduce_small(buf):
    nr, lanes = buf.shape

    def body(b_ref, out_ref, gath, send_sems, recv_sems):
        x, y, c = _position()
        my_slot = _dev_index(x, y, c)
        gath[my_slot] = b_ref[...]
        copies = []
        for k in range(1, N_DEV):
            dx, dy, dc = (k >> 2) & 1, (k >> 1) & 1, k & 1
            peer = (1 - x if dx else x, 1 - y if dy else y, 1 - c if dc else c)
            copies.append(pltpu.make_async_remote_copy(
                src_ref=b_ref, dst_ref=gath.at[my_slot], send_sem=send_sems.at[k - 1], recv_sem=recv_sems.at[k - 1],
                device_id=peer, device_id_type=MESH))
        for cp in copies:
            cp.start()
        for cp in copies:
            cp.wait()
        acc = gath[0]
        for sidx in range(1, N_DEV):
            acc = acc + gath[sidx]
        out_ref[...] = acc

    return _call(body, name="all_reduce_small", in_specs=[VMEM], out_specs=VMEM, out_shape=SDS((nr, lanes), F32),
                 scratch=[pltpu.VMEM((N_DEV, nr, lanes), F32), pltpu.SemaphoreType.DMA((7,)),
                          pltpu.SemaphoreType.DMA((7,))])(buf)


def _ffn_block_fwd(x, gain, wall, offs, fs, tag, gather_src=None):
    res = _ffn_fwd(x, gain, wall, offs, fs, f"{tag}_fwd", gather_src)
    out, xn, g, u, h = res[:5]
    return out, (x, gain, xn, g, u, h), (res[5] if gather_src is not None else None)


def _ffn_block_bwd(dout, saved, wall, offs, fs, tag, exchange_src=None):
    x, gain, xn, g, u, h = saved
    res = _ffn_bwd_act(dout, g, u, x, gain, wall, offs, fs, f"{tag}_bwd_act", exchange_src)
    dg, du, dy_b, dx, dgain = res[:5]
    dwg = _mm_tn(dg, xn, f"{tag}_dwg", BF16)
    dwu = _mm_tn(du, xn, f"{tag}_dwu", BF16)
    dwd = _mm_tn(h, dy_b, f"{tag}_dwd", BF16)
    return dx, (dwg, dwu, dwd), dgain, (res[5] if exchange_src is not None else None)


def _local_step(x, target, wall_a, fs, small, plan):
    grads = {}
    first, second = (0, fs, 2 * fs), (3 * fs, 4 * fs, 5 * fs)
    reduces = "pair_sum" in plan

    x1, s_f1a, wall_b = _ffn_block_fwd(x, small["ffn1_norm"][0], wall_a, first, fs, "l0_ffn1", plan.get("shard_b"))
    wall_b = plan.get("wall_b", wall_b)
    mixw = plan["mix_b"](wall_b)
    hn0 = _rmsnorm_fwd(x1, small["mix_norm"][0], "l0_mix_norm")
    z = _mm(hn0, mixw["ev_w_main_t"], "nt", "ev_in_proj")
    flog = _mm(hn0, mixw["ev_w_f_t"], "nt", "ev_in_proj_gate")
    a, a1 = _conv_a_fwd(z, small["ev_conv_w32"], small["ev_conv_b"], small["ev_conv_norm"], "ev_conv_fwd")
    qs, kh, vb, fb, ft, xt = _qk_fwd(z, flog, small["ev_b_f128"], small["ev_q_norm2"], small["ev_k_norm2"], "ev_qk_fwd")
    if "wall_c" in plan:
        o, lse = _attn_fwd(qs, kh, vb, fb, ft, "ev_attn_fwd")
        wall_c = plan["wall_c"]
    else:
        o, lse, wall_c = _attn_fwd(qs, kh, vb, fb, ft, "ev_attn_fwd", gather_src=plan["shard_c"])
    mixw = {**mixw, **plan["mix_c"](wall_c)}
    ao = jnp.concatenate([a, o.astype(BF16)], axis=1)
    x2 = _mm(ao, mixw["ev_w_out"], "nn", "ev_out_proj", add=x1)
    x3, s_f2a, _ = _ffn_block_fwd(x2, small["ffn2_norm"][0], wall_c, first, fs, "l0_ffn2")

    x4, s_f1b, wall_d = _ffn_block_fwd(x3, small["ffn1_norm"][1], wall_c, second, fs, "l1_ffn1", plan.get("shard_d"))
    wall_d = plan.get("wall_d", wall_d)
    hn1 = _rmsnorm_fwd(x4, small["mix_norm"][1], "l1_mix_norm")
    zo = _mm(hn1, mixw["od_w_in_t"], "nt", "od_in_proj")
    y0 = _conv_c_fwd(zo, small["od_conv_w8"], "od_conv_fwd")
    x5 = _mm(y0, mixw["od_w_out"], "nn", "od_out_proj", add=x4)
    x6, s_f2b, _ = _ffn_block_fwd(x5, small["ffn2_norm"][1], wall_d, first, fs, "l1_ffn2")

    loss, d6 = _loss_head(x6, target, "loss_head")

    d5, grads["l1_ffn2"], grads["ffn2_norm_1"], _ = _ffn_block_bwd(d6, s_f2b, wall_d, first, fs, "l1_ffn2")
    d5b = d5.astype(BF16)
    dy0 = _mm(d5b, mixw["od_w_out"], "nt", "od_out_proj_bwd")
    grads["od_w_out"] = _mm_tn(y0, d5b, "od_dw_out", BF16)
    dzo, grads["od_conv_w"] = _conv_c_bwd(dy0, zo, small["od_conv_w8"], "od_conv_bwd")
    d4, grads["mix_norm_1"] = _mm_norm_bwd(dzo, mixw["od_w_in_t"], x4, small["mix_norm"][1], d5, "od_in_proj_bwd")
    grads["od_w_in_t"] = _mm_tn(dzo, hn1, "od_dw_in", BF16)
    d3, grads["l1_ffn1"], grads["ffn1_norm_1"], _ = _ffn_block_bwd(d4, s_f1b, wall_c, second, fs, "l1_ffn1")

    d2, grads["l0_ffn2"], grads["ffn2_norm_0"], _ = _ffn_block_bwd(d3, s_f2a, wall_c, first, fs, "l0_ffn2")
    partials_c = plan["partials_c"](grads) if reduces else None
    d2b = d2.astype(BF16)
    dao = _mm(d2b, mixw["ev_w_out"], "nt", "ev_out_proj_bwd")
    grads["ev_w_out"] = _mm_tn(ao, d2b, "ev_dw_out", BF16)
    da1, grads["ev_conv_norm"], grads["ev_conv_b"] = _conv_a_bwd_norm(dao, a1, small["ev_conv_norm"], "ev_conv_bwd_norm")
    res = _conv_a_bwd_conv(da1, z, small["ev_conv_w32"], "ev_conv_bwd_conv",
                           exchange_src=("pair", partials_c) if reduces else None)
    du, dg, grads["ev_conv_w"] = res[:3]
    sums_c = plan["pair_sum"](partials_c, res[3], "c") if reduces else None
    res = _attn_bwd(qs, kh, vb, fb, ft, lse, o, dao, "ev_attn_bwd", exchange_src=sums_c)
    dqs, rs, dkh, dv, df4 = res[:5]
    if reduces:
        grads["pair_sums_c"], grads["exchanged_c"] = sums_c, res[5]
    dq, dk, dvb, grads["ev_q_norm"], grads["ev_k_norm"] = _qk_bwd(
        z, dqs, dkh, dv, small["ev_q_norm2"], small["ev_k_norm2"], "ev_qk_bwd")
    dft = df4[:, 0:2, :].reshape(N_HEADS, -1) + rs.reshape(-1, N_HEADS, HEAD_DIM)[:, :, 0].T
    dxt, dflog, grads["ev_b_f"] = _gate_bwd(dft, xt, "ev_gate_bwd")
    dz = jnp.concatenate([du, dg, dq, dk, dvb], axis=1)
    dflog_b = dflog.astype(BF16)
    dh0 = _mm(dz, mixw["ev_w_main_t"], "nn", "ev_in_proj_bwd")
    d1, grads["mix_norm_0"] = _mm_norm_bwd(dflog_b, mixw["ev_w_f_t"], x1, small["mix_norm"][0], d2,
                                           "ev_in_proj_gate_bwd", add=dh0)
    dw_main = _mm_tn(dz, hn0, "ev_dw_in", BF16)
    dw_f = _mm(dxt.astype(BF16), hn0, "nn", "ev_dw_in_gate", BF16)
    grads["ev_w_in_t"] = jnp.concatenate([dw_main, dw_f], axis=0)
    sums_b = None
    if reduces:
        partials_b = plan["partials_b"](grads)
        sums_b = plan["pair_sum"](partials_b, _pair_exchange(partials_b, "reduce_scatter_pair_exchange_b"), "b")
    d0, grads["l0_ffn1"], grads["ffn1_norm_0"], exchanged_b = _ffn_block_bwd(
        d1, s_f1a, wall_a, first, fs, "l0_ffn1", exchange_src=("chip", sums_b) if reduces else None)
    if reduces:
        grads["pair_sums_b"], grads["exchanged_b"] = sums_b, exchanged_b
    return loss, d0, grads


def _round_up(n, m):
    return -(-n // m) * m


def _pad_rows(a, rows):
    return jnp.pad(a, ((0, rows - a.shape[0]), (0, 0)))


SMALL_ORDER = ("loss", "ffn1_norm", "mix_norm", "ffn2_norm", "ev_b_f", "ev_conv_b", "ev_conv_norm",
               "ev_q_norm", "ev_k_norm", "ev_conv_w", "od_conv_w")


def _pack_small(parts):
    flat = jnp.concatenate([parts[k].reshape(-1).astype(F32) for k in SMALL_ORDER])
    n = _round_up(flat.shape[0], 8 * V7X_LANES)
    return jnp.pad(flat, (0, n - flat.shape[0])).reshape(-1, V7X_LANES)


def _unpack_small(buf, shapes):
    flat = buf.reshape(-1)
    out, pos = {}, 0
    for k in SMALL_ORDER:
        n = math.prod(shapes[k])
        out[k] = flat[pos:pos + n].reshape(shapes[k])
        pos += n
    return out


def kernel(x, ffn1_norm, ffn1_w_gate, ffn1_w_up, ffn1_w_down, mix_norm, ffn2_norm, ffn2_w_gate, ffn2_w_up, ffn2_w_down, ev_w_in, ev_b_f, ev_conv_w, ev_conv_b, ev_conv_norm, ev_q_norm, ev_k_norm, ev_w_out, od_w_in, od_conv_w, od_w_out, loss_target, m_ffn1_norm, m_ffn1_w_gate, m_ffn1_w_up, m_ffn1_w_down, m_mix_norm, m_ffn2_norm, m_ffn2_w_gate, m_ffn2_w_up, m_ffn2_w_down, m_ev_w_in, m_ev_b_f, m_ev_conv_w, m_ev_conv_b, m_ev_conv_norm, m_ev_q_norm, m_ev_k_norm, m_ev_w_out, m_od_w_in, m_od_conv_w, m_od_w_out, v_ffn1_norm, v_ffn1_w_gate, v_ffn1_w_up, v_ffn1_w_down, v_mix_norm, v_ffn2_norm, v_ffn2_w_gate, v_ffn2_w_up, v_ffn2_w_down, v_ev_w_in, v_ev_b_f, v_ev_conv_w, v_ev_conv_b, v_ev_conv_norm, v_ev_q_norm, v_ev_k_norm, v_ev_w_out, v_od_w_in, v_od_conv_w, v_od_w_out):
    weights = dict(ffn1_norm=ffn1_norm, ffn1_w_gate=ffn1_w_gate, ffn1_w_up=ffn1_w_up, ffn1_w_down=ffn1_w_down,
                   mix_norm=mix_norm, ffn2_norm=ffn2_norm, ffn2_w_gate=ffn2_w_gate, ffn2_w_up=ffn2_w_up,
                   ffn2_w_down=ffn2_w_down, ev_w_in=ev_w_in, ev_b_f=ev_b_f, ev_conv_w=ev_conv_w, ev_conv_b=ev_conv_b,
                   ev_conv_norm=ev_conv_norm, ev_q_norm=ev_q_norm, ev_k_norm=ev_k_norm, ev_w_out=ev_w_out,
                   od_w_in=od_w_in, od_conv_w=od_conv_w, od_w_out=od_w_out)
    m_in = dict(ffn1_norm=m_ffn1_norm, ffn1_w_gate=m_ffn1_w_gate, ffn1_w_up=m_ffn1_w_up, ffn1_w_down=m_ffn1_w_down,
                mix_norm=m_mix_norm, ffn2_norm=m_ffn2_norm, ffn2_w_gate=m_ffn2_w_gate, ffn2_w_up=m_ffn2_w_up,
                ffn2_w_down=m_ffn2_w_down, ev_w_in=m_ev_w_in, ev_b_f=m_ev_b_f, ev_conv_w=m_ev_conv_w,
                ev_conv_b=m_ev_conv_b, ev_conv_norm=m_ev_conv_norm, ev_q_norm=m_ev_q_norm, ev_k_norm=m_ev_k_norm,
                ev_w_out=m_ev_w_out, od_w_in=m_od_w_in, od_conv_w=m_od_conv_w, od_w_out=m_od_w_out)
    v_in = dict(ffn1_norm=v_ffn1_norm, ffn1_w_gate=v_ffn1_w_gate, ffn1_w_up=v_ffn1_w_up, ffn1_w_down=v_ffn1_w_down,
                mix_norm=v_mix_norm, ffn2_norm=v_ffn2_norm, ffn2_w_gate=v_ffn2_w_gate, ffn2_w_up=v_ffn2_w_up,
                ffn2_w_down=v_ffn2_w_down, ev_w_in=v_ev_w_in, ev_b_f=v_ev_b_f, ev_conv_w=v_ev_conv_w,
                ev_conv_b=v_ev_conv_b, ev_conv_norm=v_ev_conv_norm, ev_q_norm=v_ev_q_norm, ev_k_norm=v_ev_k_norm,
                ev_w_out=v_ev_w_out, od_w_in=v_od_w_in, od_conv_w=v_od_conv_w, od_w_out=v_od_w_out)
    order = list(weights)

    d = x.shape[-1]
    fs = ffn1_w_gate.shape[2]
    n_in = ev_w_in.shape[2]
    n_in_pad = _round_up(n_in, BF16_ROWS)
    n_out = ev_w_out.shape[1]
    n_od = od_w_in.shape[2]
    d_conv = ev_conv_b.shape[1]
    d_in_even = n_in * N_DEV
    d_main = d_in_even - N_HEADS
    cx, cy, cc = _position()
    me = _dev_index(cx, cy, cc)

    def block(wg, wu, wd, layer):
        return [wg[layer].T, wu[layer].T, wd[layer]]

    def stack(parts):
        return jnp.concatenate([p.astype(BF16) for p in parts], axis=0)

    ffn1, ffn2 = (ffn1_w_gate, ffn1_w_up, ffn1_w_down), (ffn2_w_gate, ffn2_w_up, ffn2_w_down)
    shard_a = stack(block(*ffn1, 0))
    shard_b = stack([_pad_rows(ev_w_in[0].T, n_in_pad), ev_w_out[0]])
    shard_c = stack(block(*ffn2, 0) + block(*ffn1, 1) + [od_w_in[0].T, od_w_out[0]])
    shard_d = stack(block(*ffn2, 1))
    off_ev_in, off_ev_out = 0, n_in_pad
    off_od_in, off_od_out = 6 * fs, 6 * fs + n_od
    wall_a = _all_gather(shard_a)

    def even_mixer_weights(wall_b):
        ev_w_in_t = wall_b[:, off_ev_in:off_ev_in + n_in, :].reshape(d_in_even, d)
        return dict(ev_w_main_t=ev_w_in_t[:d_main], ev_w_f_t=_pad_rows(ev_w_in_t[d_main:], V7X_LANES),
                    ev_w_out=wall_b[:, off_ev_out:off_ev_out + n_out, :].reshape(N_DEV * n_out, d))

    def odd_mixer_weights(wall_c):
        return dict(od_w_in_t=wall_c[:, off_od_in:off_od_in + n_od, :].reshape(N_DEV * n_od, d),
                    od_w_out=wall_c[:, off_od_out:off_od_out + n_out, :].reshape(N_DEV * n_out, d))

    def by_dev(a, rows, pad_to=None):
        a = a.reshape(N_DEV, rows, d)
        return a if pad_to is None else jnp.pad(a, ((0, 0), (0, pad_to - rows), (0, 0)))

    idx = jnp.stack([me] + [_dev_index(*chip, cc) for chip in _other_chips(cx, cy)]).astype(jnp.int32)

    def pair_sum(partials, from_sibling, tag):
        return _pair_sum(partials, from_sibling, idx, f"reduce_scatter_pair_sum_{tag}")

    def ffn_pieces(g, key):
        return [by_dev(t, fs) for t in g[key]]

    conv_shapes = dict(ev_conv_w=(CONV_A_WIDTH, d_conv), od_conv_w=(CONV_C_WIDTH, d))
    zero_small = {k: jnp.zeros(s_, F32) for k, s_ in conv_shapes.items()}
    ev_cw_part = lax.dynamic_update_slice(zero_small["ev_conv_w"], ev_conv_w[0], (0, me * ev_conv_w.shape[2]))
    od_cw_part = lax.dynamic_update_slice(zero_small["od_conv_w"], od_conv_w[0], (0, me * od_conv_w.shape[2]))
    zeros_like_small = {k: jnp.zeros((1,), F32) for k in SMALL_ORDER}
    taps = _unpack_small(_all_reduce_small(_pack_small({**zeros_like_small, "ev_conv_w": ev_cw_part,
                                                        "od_conv_w": od_cw_part})),
                         {**{k: (1,) for k in SMALL_ORDER}, **conv_shapes})
    small = dict(
        ffn1_norm=[ffn1_norm[l][None] for l in range(2)], mix_norm=[mix_norm[l][None] for l in range(2)],
        ffn2_norm=[ffn2_norm[l][None] for l in range(2)],
        ev_conv_w32=_pad_rows(taps["ev_conv_w"], CONV_A_WIDTH + 1), ev_conv_b=ev_conv_b, ev_conv_norm=ev_conv_norm,
        ev_b_f128=jnp.pad(ev_b_f, ((0, 0), (0, V7X_LANES - N_HEADS))),
        ev_q_norm2=jnp.tile(ev_q_norm, (1, 2)), ev_k_norm2=jnp.tile(ev_k_norm, (1, 2)),
        od_conv_w8=_pad_rows(taps["od_conv_w"], 8),
    )

    plan = dict(
        shard_b=shard_b, shard_c=shard_c, shard_d=shard_d, mix_b=even_mixer_weights, mix_c=odd_mixer_weights, pair_sum=pair_sum,
        partials_c=lambda g1: jnp.concatenate(
            ffn_pieces(g1, "l0_ffn2") + ffn_pieces(g1, "l1_ffn1") + ffn_pieces(g1, "l1_ffn2")
            + [by_dev(g1["od_w_in_t"], n_od), by_dev(g1["od_w_out"], n_out)], axis=1),
        partials_b=lambda g1: jnp.concatenate(
            [by_dev(g1["ev_w_in_t"], n_in, n_in_pad), by_dev(g1["ev_w_out"], n_out)], axis=1))
    loss_p, grad_x, g = _local_step(x[0], loss_target[0], wall_a, fs, small, plan)

    partials_a = jnp.concatenate(ffn_pieces(g, "l0_ffn1"), axis=1)
    sums_a = pair_sum(partials_a, _pair_exchange(partials_a, "reduce_scatter_pair_exchange_a"), "a")
    gsum_a = _final_sum(sums_a, _chip_exchange(sums_a), "reduce_scatter_final_sum_a")
    gsum_b = _final_sum(g["pair_sums_b"], g["exchanged_b"], "reduce_scatter_final_sum_b")
    gsum_c = _final_sum(g["pair_sums_c"], g["exchanged_c"], "reduce_scatter_final_sum_c")

    grad = {}
    where = dict(ffn1=((gsum_a, 0), (gsum_c, 3 * fs)), ffn2=((gsum_c, 0), (gsum_c, 6 * fs)))
    for blk, places in where.items():
        for wi, kind in enumerate(("gate", "up", "down")):
            rows = [buf[off + wi * fs:off + (wi + 1) * fs] for buf, off in places]
            grad[f"{blk}_w_{kind}"] = jnp.stack(rows if kind == "down" else [r.T for r in rows])
    grad["ev_w_in"] = gsum_b[off_ev_in:off_ev_in + n_in].T[None]
    grad["ev_w_out"] = gsum_b[off_ev_out:off_ev_out + n_out][None]
    grad["od_w_in"] = gsum_c[9 * fs:9 * fs + n_od].T[None]
    grad["od_w_out"] = gsum_c[9 * fs + n_od:9 * fs + n_od + n_out][None]

    heads = lambda t: t.reshape(N_HEADS, HEAD_DIM).sum(axis=0)
    parts = dict(
        loss=loss_p[0, 0:1],
        ffn1_norm=jnp.stack([g["ffn1_norm_0"][0], g["ffn1_norm_1"][0]]),
        mix_norm=jnp.stack([g["mix_norm_0"][0], g["mix_norm_1"][0]]),
        ffn2_norm=jnp.stack([g["ffn2_norm_0"][0], g["ffn2_norm_1"][0]]),
        ev_b_f=g["ev_b_f"][:, 0], ev_conv_b=g["ev_conv_b"], ev_conv_norm=g["ev_conv_norm"],
        ev_q_norm=heads(g["ev_q_norm"]), ev_k_norm=heads(g["ev_k_norm"]),
        ev_conv_w=g["ev_conv_w"][:CONV_A_WIDTH], od_conv_w=g["od_conv_w"][:CONV_C_WIDTH])
    small_shapes = dict(loss=(1,), ffn1_norm=ffn1_norm.shape, mix_norm=mix_norm.shape, ffn2_norm=ffn2_norm.shape,
                        ev_b_f=ev_b_f.shape, ev_conv_b=ev_conv_b.shape, ev_conv_norm=ev_conv_norm.shape,
                        ev_q_norm=ev_q_norm.shape, ev_k_norm=ev_k_norm.shape, **conv_shapes)
    red = _unpack_small(_all_reduce_small(_pack_small(parts)), small_shapes)
    loss = red["loss"][0]
    for k in ("ffn1_norm", "mix_norm", "ffn2_norm", "ev_b_f", "ev_conv_b", "ev_conv_norm", "ev_q_norm", "ev_k_norm"):
        grad[k] = red[k]
    grad["ev_conv_w"] = lax.dynamic_slice(red["ev_conv_w"], (0, me * ev_conv_w.shape[2]),
                                          (CONV_A_WIDTH, ev_conv_w.shape[2]))[None]
    grad["od_conv_w"] = lax.dynamic_slice(red["od_conv_w"], (0, me * od_conv_w.shape[2]),
                                          (CONV_C_WIDTH, od_conv_w.shape[2]))[None]

    big = ("ffn1_w_gate", "ffn1_w_up", "ffn1_w_down", "ffn2_w_gate", "ffn2_w_up", "ffn2_w_down",
           "ev_w_in", "ev_w_out", "od_w_in", "od_w_out")
    delta, new_m, new_v = {}, {}, {}
    for k in big:
        shp = weights[k].shape
        flat = lambda t: t.reshape(-1, shp[-1])
        dk, mk, vk = _adamw(flat(weights[k]), flat(grad[k]), flat(m_in[k]), flat(v_in[k]), f"adamw_{k}")
        delta[k], new_m[k], new_v[k] = dk.reshape(shp), mk.reshape(shp), vk.reshape(shp)
    rest = [k for k in order if k not in big]
    cat = lambda src: jnp.concatenate([src[k].reshape(-1) for k in rest])
    n_small = sum(math.prod(weights[k].shape) for k in rest)
    n_pad = _round_up(n_small, 8 * V7X_LANES)
    as_rows = lambda t: jnp.pad(t, (0, n_pad - n_small)).reshape(-1, V7X_LANES)
    v_rows = jnp.pad(cat(v_in), (0, n_pad - n_small), constant_values=1.0).reshape(-1, V7X_LANES)
    ds, ms, vs = _adamw(as_rows(cat(weights)), as_rows(cat(grad)), as_rows(cat(m_in)), v_rows, "adamw_small")
    pos = 0
    for k in rest:
        n = math.prod(weights[k].shape)
        for dst, src in ((delta, ds), (new_m, ms), (new_v, vs)):
            dst[k] = src.reshape(-1)[pos:pos + n].reshape(weights[k].shape)
        pos += n

    return (loss, grad_x[None], *[grad[k] for k in order], *[delta[k] for k in order],
            *[new_m[k] for k in order], *[new_v[k] for k in order])
```

```python
import math

import jax
import jax.numpy as jnp
from jax import lax
from jax.experimental import pallas as pl
from jax.experimental.pallas import tpu as pltpu

F32 = jnp.float32
BF16 = jnp.bfloat16
SDS = jax.ShapeDtypeStruct
MESH = pl.DeviceIdType.MESH

N_DEV = 8
EPS = 1e-6
FFN_RES = 0.5
HEAD_DIM = 64
N_HEADS = 8
D_ATTN = N_HEADS * HEAD_DIM
N_PAIRS = N_HEADS // 2
PAIR = 2 * HEAD_DIM
ATTN_SCALE = 1.0 / math.sqrt(HEAD_DIM)
CONV_A_WIDTH = 31
CONV_A_HALO = 32
CONV_C_WIDTH = 3
CONV_C_HALO = 8
NEG_BIG = -1e30
ADAM_LR, ADAM_B1, ADAM_B2, ADAM_EPS, ADAM_WD, ADAM_STEP = 0.001, 0.9, 0.999, 1e-08, 0.01, 10

V7X_VMEM_BYTES = 64 * 1024 * 1024
V7X_LANES = 128
BF16_ROWS = 16
MIB = 1024 * 1024

NT = (((1,), (1,)), ((), ()))
TN = (((0,), (0,)), ((), ()))


def _call(body, *, name, out_shape, in_specs, out_specs, grid=(), scratch=(), dims=None, vmem_mb=32, **kw):
    params = dict(vmem_limit_bytes=min(vmem_mb * MIB, V7X_VMEM_BYTES - 4 * MIB))
    if dims is not None:
        params["dimension_semantics"] = dims
    call = pl.pallas_call(
        body, name=name, grid=grid, in_specs=in_specs, out_specs=out_specs, out_shape=_in_hbm(out_shape),
        scratch_shapes=list(scratch), compiler_params=pltpu.CompilerParams(**params), **kw)
    return lambda *args: call(*[_keep_in_hbm(a) for a in args])


LARGE_OPERAND_BYTES = MIB


def _is_large(a):
    return a.ndim >= 2 and math.prod(a.shape) * jnp.dtype(a.dtype).itemsize >= LARGE_OPERAND_BYTES


def _keep_in_hbm(a):
    return pltpu.with_memory_space_constraint(a, pltpu.HBM) if _is_large(a) else a


def _in_hbm(out_shape):
    one = lambda s: pltpu.HBM(s.shape, s.dtype) if _is_large(s) else s
    return [one(s) for s in out_shape] if isinstance(out_shape, (list, tuple)) else one(out_shape)


def _tile(n, want=512):
    return want if n % want == 0 else n


def _rows(tm, d, col=0):
    return pl.BlockSpec((tm, d), lambda i: (i, col))


def _const(shape):
    return pl.BlockSpec(shape, lambda *_: (0,) * len(shape))


ANY = pl.BlockSpec(memory_space=pl.ANY)
VMEM = pl.BlockSpec(memory_space=pltpu.VMEM)


def _sigmoid(x):
    return 1.0 / (1.0 + jnp.exp(-x))


def _rmsnorm_fwd(x, gain, name):
    s, d = x.shape
    tm = _tile(s)

    def body(x_ref, g_ref, o_ref):
        xv = x_ref[...]
        r = lax.rsqrt(jnp.mean(xv * xv, axis=-1, keepdims=True) + EPS)
        o_ref[...] = (xv * r * g_ref[...]).astype(BF16)

    return _call(body, name=name, grid=(s // tm,), in_specs=[_rows(tm, d), _const((1, d))],
                 out_specs=_rows(tm, d), out_shape=SDS((s, d), BF16), dims=("parallel",))(x, gain)


def _col_tile(n):
    for t in (1536, 1280, 1024, 768, 512, 256, 128):
        if n % t == 0:
            return t
    return n


def _mm(a, b, mode, name, out_dtype=F32, add=None):
    if mode == "tn":
        k, m = a.shape
        n = b.shape[1]
        bm = next((t for t in range(768, 0, -V7X_LANES) if m % t == 0), m)

        def body_tn(a_ref, b_ref, o_ref):
            o_ref[...] = lax.dot_general(a_ref[...].astype(BF16), b_ref[...].astype(BF16), TN,
                                         preferred_element_type=F32).astype(out_dtype)

        return _call(body_tn, name=name, grid=(m // bm,),
                     in_specs=[pl.BlockSpec((k, bm), lambda i: (0, i)), _const((k, n))],
                     out_specs=pl.BlockSpec((bm, n), lambda i: (i, 0)),
                     out_shape=SDS((m, n), out_dtype), dims=("parallel",), vmem_mb=48)(a, b)
    m, k = a.shape
    n = b.shape[0] if mode == "nt" else b.shape[1]
    tm, tn = _tile(m), _col_tile(n)
    dn = NT if mode == "nt" else (((1,), (0,)), ((), ()))

    def body(a_ref, b_ref, *rest):
        o_ref = rest[-1]
        acc = lax.dot_general(a_ref[...].astype(BF16), b_ref[...].astype(BF16), dn, preferred_element_type=F32)
        if add is not None:
            acc = acc + rest[0][...]
        o_ref[...] = acc.astype(out_dtype)

    b_spec = (pl.BlockSpec((tn, k), lambda i, j: (j, 0)) if mode == "nt"
              else pl.BlockSpec((k, tn), lambda i, j: (0, j)))
    in_specs = [pl.BlockSpec((tm, k), lambda i, j: (i, 0)), b_spec]
    args = [a, b]
    if add is not None:
        in_specs.append(pl.BlockSpec((tm, tn), lambda i, j: (i, j)))
        args.append(add)
    return _call(body, name=name, grid=(m // tm, n // tn), in_specs=in_specs,
                 out_specs=pl.BlockSpec((tm, tn), lambda i, j: (i, j)),
                 out_shape=SDS((m, n), out_dtype), dims=("parallel", "parallel"), vmem_mb=48)(*args)


def _mm_norm_bwd(a, b, x, gain, dres, name, add=None):
    m, k = a.shape
    n = b.shape[1]
    tm = _tile(m)

    def body(a_ref, b_ref, x_ref, g_ref, dres_ref, *rest):
        dx_ref, dg_ref = rest[-2:]
        dh = jnp.dot(a_ref[...].astype(BF16), b_ref[...].astype(BF16), preferred_element_type=F32)
        if add is not None:
            dh = dh + rest[0][...]
        xv = x_ref[...]
        r = lax.rsqrt(jnp.mean(xv * xv, axis=-1, keepdims=True) + EPS)
        xh = xv * r

        @pl.when(pl.program_id(0) == 0)
        def _():
            dg_ref[...] = jnp.zeros_like(dg_ref)

        dg_ref[...] += jnp.sum(dh * xh, axis=0, keepdims=True)
        dxh = dh * g_ref[...]
        dx_ref[...] = dres_ref[...] + r * (dxh - xh * jnp.mean(dxh * xh, axis=-1, keepdims=True))

    in_specs = [_rows(tm, k), _const((k, n)), _rows(tm, n), _const((1, n)), _rows(tm, n)]
    args = [a, b, x, gain, dres]
    if add is not None:
        in_specs.append(_rows(tm, n))
        args.append(add)
    return _call(body, name=name, grid=(m // tm,), in_specs=in_specs, out_specs=[_rows(tm, n), _const((1, n))],
                 out_shape=[SDS((m, n), F32), SDS((1, n), F32)], dims=("arbitrary",), vmem_mb=48)(*args)


def _mm_tn(a, b, name, out_dtype=F32):
    return _mm(a, b, "tn", name, out_dtype)


FFN_TM = 256
FFN_FWD_TM = 512
FFN_CHUNK = 256


def _load_ffn_weights(w_hbm, offs, fs, dsts, sems):
    copies = []
    for wi, (off, dst) in enumerate(zip(offs, dsts)):
        for j in range(N_DEV):
            cp = pltpu.make_async_copy(w_hbm.at[j, pl.ds(off, fs), :], dst.at[pl.ds(j * fs, fs), :],
                                       sems.at[wi * N_DEV + j])
            cp.start()
            copies.append(cp)
    for cp in copies:
        cp.wait()


def _ffn_fwd(x, gain, wall, offs, fs, name, gather_src=None):
    s, d = x.shape
    f = fs * N_DEV
    tm, ch = _tile(s, FFN_FWD_TM), FFN_CHUNK
    n = s // tm
    gathers = gather_src is not None

    def body(x_ref, gain_ref, w_hbm, *rest):
        if gathers:
            (src_hbm, out_ref, xn_ref, g_ref, u_ref, h_ref, gathered, wg_s, wu_s, wd_s, sems,
             send_sems, recv_sems, local_sem) = rest
            start, forward, finish = _gather_phases(src_hbm, gathered, send_sems, recv_sems, local_sem)
            pl.when(pl.program_id(0) == 0)(start)
            pl.when(pl.program_id(0) == (7 * n) // 8)(forward)
        else:
            out_ref, xn_ref, g_ref, u_ref, h_ref, wg_s, wu_s, wd_s, sems = rest

        @pl.when(pl.program_id(0) == 0)
        def _():
            _load_ffn_weights(w_hbm, offs, fs, (wg_s, wu_s, wd_s), sems)

        xv = x_ref[...]
        xnv = (xv * lax.rsqrt(jnp.mean(xv * xv, axis=-1, keepdims=True) + EPS) * gain_ref[...]).astype(BF16)
        xn_ref[...] = xnv
        acc = jnp.zeros((tm, d), F32)
        for c in range(f // ch):
            sl = slice(c * ch, (c + 1) * ch)
            gb = lax.dot_general(xnv, wg_s[sl, :], NT, preferred_element_type=F32).astype(BF16)
            ub = lax.dot_general(xnv, wu_s[sl, :], NT, preferred_element_type=F32).astype(BF16)
            g_ref[:, sl] = gb
            u_ref[:, sl] = ub
            g = gb.astype(F32)
            hb = (g * _sigmoid(g) * ub.astype(F32)).astype(BF16)
            h_ref[:, sl] = hb
            acc = acc + jnp.dot(hb, wd_s[sl, :], preferred_element_type=F32)
        out_ref[...] = xv + FFN_RES * acc
        if gathers:
            pl.when(pl.program_id(0) == n - 1)(finish)

    in_specs, args = [_rows(tm, d), _const((1, d)), ANY], [x, gain, wall]
    out_specs = [_rows(tm, d), _rows(tm, d), _rows(tm, f), _rows(tm, f), _rows(tm, f)]
    out_shape = [SDS((s, d), F32), SDS((s, d), BF16), SDS((s, f), BF16), SDS((s, f), BF16), SDS((s, f), BF16)]
    scratch = [pltpu.VMEM((f, d), BF16)] * 3 + [pltpu.SemaphoreType.DMA((3 * N_DEV,))]
    if gathers:
        in_specs.append(ANY)
        args.append(gather_src)
        out_specs.append(ANY)
        out_shape.append(SDS((N_DEV,) + gather_src.shape, gather_src.dtype))
        scratch += GATHER_SEMS
    return _call(body, name=name, grid=(n,), in_specs=in_specs, out_specs=out_specs, out_shape=out_shape,
                 scratch=scratch, dims=("arbitrary",), vmem_mb=56)(*args)


def _ffn_bwd_act(dout, g, u, x, gain, wall, offs, fs, name, exchange_src=None):
    s, d = dout.shape
    f = fs * N_DEV
    tm, ch = _tile(s, FFN_TM), FFN_CHUNK
    n = s // tm
    exchanges = exchange_src is not None
    if exchanges:
        phases_of, shape_of, exchange_sems = EXCHANGES[exchange_src[0]]

    def body(dout_ref, g_ref, u_ref, x_ref, gain_ref, w_hbm, *rest):
        if exchanges:
            (t_hbm, dg_ref, du_ref, dy_ref, dx_ref, dgain_ref, rcv_ref, wg_s, wu_s, wd_s, sems,
             send_sems, recv_sems) = rest
            start, finish = phases_of(t_hbm, rcv_ref, send_sems, recv_sems)
            pl.when(pl.program_id(0) == 0)(start)
        else:
            dg_ref, du_ref, dy_ref, dx_ref, dgain_ref, wg_s, wu_s, wd_s, sems = rest

        @pl.when(pl.program_id(0) == 0)
        def _():
            _load_ffn_weights(w_hbm, offs, fs, (wg_s, wu_s, wd_s), sems)
            dgain_ref[...] = jnp.zeros_like(dgain_ref)

        doutv = dout_ref[...]
        dy = (FFN_RES * doutv).astype(BF16)
        dy_ref[...] = dy
        acc = jnp.zeros((tm, d), F32)
        for c in range(f // ch):
            sl = slice(c * ch, (c + 1) * ch)
            dh = lax.dot_general(dy, wd_s[sl, :], NT, preferred_element_type=F32)
            gv = g_ref[:, sl].astype(F32)
            uv = u_ref[:, sl].astype(F32)
            sg = _sigmoid(gv)
            dgb = (dh * uv * sg * (1.0 + gv * (1.0 - sg))).astype(BF16)
            dub = (dh * gv * sg).astype(BF16)
            dg_ref[:, sl] = dgb
            du_ref[:, sl] = dub
            acc = acc + jnp.dot(dgb, wg_s[sl, :], preferred_element_type=F32)
            acc = acc + jnp.dot(dub, wu_s[sl, :], preferred_element_type=F32)
        xv = x_ref[...]
        r = lax.rsqrt(jnp.mean(xv * xv, axis=-1, keepdims=True) + EPS)
        xh = xv * r
        dgain_ref[...] += jnp.sum(acc * xh, axis=0, keepdims=True)
        dxh = acc * gain_ref[...]
        dx_ref[...] = doutv + r * (dxh - xh * jnp.mean(dxh * xh, axis=-1, keepdims=True))
        if exchanges:
            pl.when(pl.program_id(0) == n - 1)(finish)

    in_specs = [_rows(tm, d), _rows(tm, f), _rows(tm, f), _rows(tm, d), _const((1, d)), ANY]
    args = [dout, g, u, x, gain, wall]
    out_specs = [_rows(tm, f), _rows(tm, f), _rows(tm, d), _rows(tm, d), _const((1, d))]
    out_shape = [SDS((s, f), BF16), SDS((s, f), BF16), SDS((s, d), BF16), SDS((s, d), F32), SDS((1, d), F32)]
    scratch = [pltpu.VMEM((f, d), BF16)] * 3 + [pltpu.SemaphoreType.DMA((3 * N_DEV,))]
    if exchanges:
        in_specs.append(ANY)
        args.append(exchange_src[1])
        out_specs.append(ANY)
        out_shape.append(shape_of(exchange_src[1]))
        scratch += exchange_sems
    return _call(body, name=name, grid=(n,), in_specs=in_specs, out_specs=out_specs, out_shape=out_shape,
                 scratch=scratch, dims=("arbitrary",), vmem_mb=56)(*args)


def _prev_rows(halo, tm, c, col):
    return pl.BlockSpec((halo, c), lambda i: (jnp.maximum(i * (tm // halo) - 1, 0), col))


def _next_rows(halo, tm, c, col, n_blocks):
    return pl.BlockSpec((halo, c), lambda i: (jnp.minimum((i + 1) * (tm // halo), n_blocks - 1), col))


def _conv_a_fwd(z, cw, cb, cn, name):
    s = z.shape[0]
    c = cb.shape[1]
    tm, halo, kw = _tile(s), CONV_A_HALO, CONV_A_WIDTH

    def body(u_ref, g_ref, up_ref, gp_ref, cw_ref, cb_ref, cn_ref, a_ref, a1_ref, buf):
        i = pl.program_id(0)
        buf[0:halo, :] = jnp.where(i > 0, up_ref[...] * _sigmoid(gp_ref[...]), 0.0)
        buf[halo:halo + tm, :] = u_ref[...] * _sigmoid(g_ref[...])
        acc = jnp.zeros((tm, c), F32)
        for k in range(kw):
            acc = acc + cw_ref[k:k + 1, :] * buf[pl.ds(halo - (kw - 1) + k, tm), :]
        a1 = acc + cb_ref[...]
        a1_ref[...] = a1
        a2 = a1 * lax.rsqrt(jnp.mean(a1 * a1, axis=-1, keepdims=True) + EPS) * cn_ref[...]
        a_ref[...] = (a2 * _sigmoid(a2)).astype(BF16)

    return _call(body, name=name, grid=(s // tm,),
                 in_specs=[_rows(tm, c, 0), _rows(tm, c, 1), _prev_rows(halo, tm, c, 0), _prev_rows(halo, tm, c, 1),
                           _const(cw.shape), _const((1, c)), _const((1, c))],
                 out_specs=[_rows(tm, c), _rows(tm, c)],
                 out_shape=[SDS((s, c), BF16), SDS((s, c), F32)],
                 scratch=[pltpu.VMEM((tm + halo, c), F32)], dims=("parallel",))(z, z, z, z, cw, cb, cn)


def _conv_a_bwd_norm(dao, a1, cn, name):
    s, c = a1.shape
    tm = _tile(s)

    def body(da_ref, a1_ref, cn_ref, da1_ref, dcn_ref, dcb_ref):
        a1v = a1_ref[...]
        r = lax.rsqrt(jnp.mean(a1v * a1v, axis=-1, keepdims=True) + EPS)
        xh = a1v * r
        a2 = xh * cn_ref[...]
        sg = _sigmoid(a2)
        da2 = da_ref[...] * sg * (1.0 + a2 * (1.0 - sg))
        dxh = da2 * cn_ref[...]
        da1 = r * (dxh - xh * jnp.mean(dxh * xh, axis=-1, keepdims=True))
        da1_ref[...] = da1

        @pl.when(pl.program_id(0) == 0)
        def _():
            dcn_ref[...] = jnp.zeros_like(dcn_ref)
            dcb_ref[...] = jnp.zeros_like(dcb_ref)

        dcn_ref[...] += jnp.sum(da2 * xh, axis=0, keepdims=True)
        dcb_ref[...] += jnp.sum(da1, axis=0, keepdims=True)

    return _call(body, name=name, grid=(s // tm,),
                 in_specs=[_rows(tm, c, 0), _rows(tm, c), _const((1, c))],
                 out_specs=[_rows(tm, c), _const((1, c)), _const((1, c))],
                 out_shape=[SDS((s, c), F32), SDS((1, c), F32), SDS((1, c), F32)], dims=("arbitrary",))(dao, a1, cn)


def _conv_a_bwd_conv(da1, z, cw, name, exchange_src=None):
    s, c = da1.shape
    tm, halo, kw = _tile(s), CONV_A_HALO, CONV_A_WIDTH
    n = s // tm
    exchanges = exchange_src is not None
    if exchanges:
        phases_of, shape_of, exchange_sems = EXCHANGES[exchange_src[0]]

    def body(d_ref, dn_ref, u_ref, g_ref, up_ref, gp_ref, cw_ref, *rest):
        i = pl.program_id(0)
        if exchanges:
            t_hbm, du_ref, dg_ref, dcw_ref, rcv_ref, buf, bd, send_sems, recv_sems = rest
            start, finish = phases_of(t_hbm, rcv_ref, send_sems, recv_sems)
            pl.when(i == 0)(start)
        else:
            du_ref, dg_ref, dcw_ref, buf, bd = rest
        uv = u_ref[...]
        sg = _sigmoid(g_ref[...])
        buf[0:halo, :] = jnp.where(i > 0, up_ref[...] * _sigmoid(gp_ref[...]), 0.0)
        buf[halo:halo + tm, :] = uv * sg
        dv = d_ref[...]
        bd[0:tm, :] = dv
        bd[tm:tm + halo, :] = jnp.where(i < n - 1, dn_ref[...], 0.0)

        @pl.when(i == 0)
        def _():
            dcw_ref[...] = jnp.zeros_like(dcw_ref)

        da0 = jnp.zeros((tm, c), F32)
        for k in range(kw):
            da0 = da0 + cw_ref[k:k + 1, :] * bd[pl.ds(kw - 1 - k, tm), :]
            dcw_ref[k:k + 1, :] += jnp.sum(dv * buf[pl.ds(halo - (kw - 1) + k, tm), :], axis=0, keepdims=True)
        du_ref[...] = (da0 * sg).astype(BF16)
        dg_ref[...] = (da0 * uv * sg * (1.0 - sg)).astype(BF16)
        if exchanges:
            pl.when(i == n - 1)(finish)

    in_specs = [_rows(tm, c), _next_rows(halo, tm, c, 0, s // halo), _rows(tm, c, 0), _rows(tm, c, 1),
                _prev_rows(halo, tm, c, 0), _prev_rows(halo, tm, c, 1), _const(cw.shape)]
    args = [da1, da1, z, z, z, z, cw]
    out_specs = [_rows(tm, c), _rows(tm, c), _const(cw.shape)]
    out_shape = [SDS((s, c), BF16), SDS((s, c), BF16), SDS(cw.shape, F32)]
    scratch = [pltpu.VMEM((tm + halo, c), F32)] * 2
    if exchanges:
        in_specs.append(ANY)
        args.append(exchange_src[1])
        out_specs.append(ANY)
        out_shape.append(shape_of(exchange_src[1]))
        scratch += exchange_sems
    return _call(body, name=name, grid=(n,), in_specs=in_specs, out_specs=out_specs, out_shape=out_shape,
                 scratch=scratch, dims=("arbitrary",))(*args)


def _lane_is_first_head(tm):
    return lax.broadcasted_iota(jnp.int32, (tm, PAIR), 1) < HEAD_DIM


def _pair_rms(xp, first):
    x2 = xp * xp
    s0 = jnp.sum(jnp.where(first, x2, 0.0), axis=-1, keepdims=True)
    s1 = jnp.sum(jnp.where(first, 0.0, x2), axis=-1, keepdims=True)
    return jnp.where(first, lax.rsqrt(s0 / HEAD_DIM + EPS), lax.rsqrt(s1 / HEAD_DIM + EPS))


def _split3(x):
    hi = x.astype(BF16)
    r1 = x - hi.astype(F32)
    mid = r1.astype(BF16)
    lo = (r1 - mid.astype(F32)).astype(BF16)
    return hi, mid, lo


def _qk_fwd(z, flog, bf, qn2, kn2, name):
    s = z.shape[0]
    tm = _tile(s)
    col0 = (z.shape[1] - 3 * D_ATTN) // D_ATTN

    def body(q_ref, k_ref, v_ref, fl_ref, bf_ref, qn_ref, kn_ref,
             qs_ref, kh_ref, vb_ref, fb_ref, ft_ref, xt_ref, carry):
        i = pl.program_id(0)
        first = _lane_is_first_head(tm)
        for p in range(N_PAIRS):
            sl = slice(p * PAIR, (p + 1) * PAIR)
            q = q_ref[:, sl]
            qs_ref[:, sl] = (q * _pair_rms(q, first) * qn_ref[...] * ATTN_SCALE).astype(BF16)
            k = k_ref[:, sl]
            kh_ref[:, sl] = (k * _pair_rms(k, first) * kn_ref[...]).astype(BF16)
        vb_ref[...] = v_ref[...].astype(BF16)

        xg = fl_ref[...] + bf_ref[...]
        valid = lax.broadcasted_iota(jnp.int32, (tm, V7X_LANES), 1) < N_HEADS
        ls = jnp.where(valid, jnp.minimum(xg, 0.0) - jnp.log(1.0 + jnp.exp(-jnp.abs(xg))), 0.0)
        tri = (lax.broadcasted_iota(jnp.int32, (tm, tm), 1) <= lax.broadcasted_iota(jnp.int32, (tm, tm), 0)).astype(BF16)
        cs = jnp.zeros((tm, V7X_LANES), F32)
        for part in _split3(ls):
            cs = cs + jnp.dot(tri, part, preferred_element_type=F32)

        @pl.when(i == 0)
        def _():
            carry[...] = jnp.zeros_like(carry)

        fv = cs + carry[0:1, :]
        carry[0:1, :] = fv[tm - 1:tm, :]
        ft_ref[...] = fv.T[0:N_HEADS, :]
        xt_ref[...] = xg.T[0:N_HEADS, :]
        for p in range(N_PAIRS):
            fb_ref[:, p * PAIR:(p + 1) * PAIR] = jnp.where(first, fv[:, 2 * p:2 * p + 1], fv[:, 2 * p + 1:2 * p + 2])

    wide = lambda col: pl.BlockSpec((tm, D_ATTN), lambda i: (i, col))
    tcol = pl.BlockSpec((N_HEADS, tm), lambda i: (0, i))
    return _call(body, name=name, grid=(s // tm,),
                 in_specs=[wide(col0), wide(col0 + 1), wide(col0 + 2), _rows(tm, V7X_LANES),
                           _const((1, V7X_LANES)), _const((1, PAIR)), _const((1, PAIR))],
                 out_specs=[wide(0), wide(0), wide(0), wide(0), tcol, tcol],
                 out_shape=[SDS((s, D_ATTN), BF16)] * 3 + [SDS((s, D_ATTN), F32), SDS((N_HEADS, s), F32),
                                                          SDS((N_HEADS, s), F32)],
                 scratch=[pltpu.VMEM((8, V7X_LANES), F32)], dims=("arbitrary",))(z, z, z, flog, bf, qn2, kn2)


ATTN_FWD_SUB = 256
ATTN_BWD_SUB = 512


def _causal_schedule(nq, key_major):
    if key_major:
        pairs = [(i, j) for j in range(nq) for i in range(j, nq)]
    else:
        pairs = [(i, j) for i in range(nq) for j in range(i + 1)]
    return (jnp.asarray([p[0] for p in pairs], jnp.int32), jnp.asarray([p[1] for p in pairs], jnp.int32))


def _sub_scores(qp, kp, ft_row, mine, r, masked, sub, tk):
    qm = jnp.where(mine, qp, jnp.zeros_like(qp))
    s2 = lax.dot_general(qm, kp, NT, preferred_element_type=F32) - ft_row
    if masked:
        row = r * sub + lax.broadcasted_iota(jnp.int32, (sub, tk), 0)
        s2 = jnp.where(lax.broadcasted_iota(jnp.int32, (sub, tk), 1) <= row, s2, NEG_BIG)
    return s2


def _attn_fwd(qs, kh, vb, fb, ft, name, gather_src=None):
    s = qs.shape[0]
    tq = tk = _tile(s)
    nq = s // tq
    sub = min(ATTN_FWD_SUB, tq)
    ii, jj = _causal_schedule(nq, key_major=False)
    n_steps = ii.shape[0]
    gathers = gather_src is not None

    def body(ii_ref, jj_ref, q_ref, k_ref, v_ref, fq_ref, ft_ref, *rest):
        if gathers:
            x_hbm, o_ref, lse_ref, wall_ref, m_s, l_s, acc_s, send_sems, recv_sems, local_sem = rest
        else:
            o_ref, lse_ref, m_s, l_s, acc_s = rest
        p, t = pl.program_id(0), pl.program_id(1)
        i, j = ii_ref[t], jj_ref[t]
        first = _lane_is_first_head(sub)
        if gathers:
            start, forward, finish = _gather_phases(x_hbm, wall_ref, send_sems, recv_sems, local_sem)
            pl.when(jnp.logical_and(p == 0, t == 0))(start)
            pl.when(jnp.logical_and(p == N_PAIRS - 1, t == 0))(forward)

        @pl.when(j == 0)
        def _():
            m_s[...] = jnp.full_like(m_s, NEG_BIG)
            l_s[...] = jnp.zeros_like(l_s)
            acc_s[...] = jnp.zeros_like(acc_s)

        def tile(masked):
            kp, vp = k_ref[...], v_ref[...]
            first_k = _lane_is_first_head(tk)
            v_ones = (jnp.where(first_k, vp, jnp.ones_like(vp)), jnp.where(first_k, jnp.ones_like(vp), vp))
            q_all, fq_all, acc_all = q_ref[...], fq_ref[...], acc_s[...]
            m_all, l_all = (m_s[0], m_s[1]), (l_s[0], l_s[1])
            ft_rows = [ft_ref[pl.ds(2 * p + h, 1), :] for h in range(2)]
            m_out, l_out, acc_out = ([], []), ([], []), []
            for r in range(tq // sub):
                rows = slice(r * sub, (r + 1) * sub)
                qp, fq, acc = q_all[rows, :], fq_all[rows, :], acc_all[rows, :]
                kc = (r + 1) * sub if masked else tk
                new = []
                for h in range(2):
                    mine = first if h == 0 else jnp.logical_not(first)
                    s2 = _sub_scores(qp, kp[:kc, :], ft_rows[h][:, :kc], mine, r, masked, sub, kc)
                    fqh = fq[:, h * HEAD_DIM:h * HEAD_DIM + 1]
                    m_old = m_all[h][rows, :]
                    m_new = jnp.maximum(m_old, jnp.max(s2, axis=-1, keepdims=True) + fqh)
                    pr = jnp.exp(s2 - (m_new - fqh))
                    alpha = jnp.exp(m_old - m_new)
                    pv = jnp.dot(pr.astype(BF16), v_ones[h][:kc, :], preferred_element_type=F32)
                    other = (1 - h) * HEAD_DIM
                    l_out[h].append(alpha * l_all[h][rows, :] + pv[:, other:other + 1])
                    m_out[h].append(m_new)
                    new.append(alpha * acc + pv)
                acc_out.append(jnp.where(first, new[0], new[1]))
            for h in range(2):
                m_s[h] = jnp.concatenate(m_out[h], axis=0)
                l_s[h] = jnp.concatenate(l_out[h], axis=0)
            acc_s[...] = jnp.concatenate(acc_out, axis=0)

        @pl.when(j < i)
        def _():
            tile(False)

        @pl.when(j == i)
        def _():
            tile(True)
            whole = _lane_is_first_head(tq)
            l_pair = jnp.where(whole, l_s[0], l_s[1])
            o_ref[...] = acc_s[...] / l_pair
            lse_ref[...] = jnp.where(whole, m_s[0], m_s[1]) + jnp.log(l_pair)

        if gathers:
            pl.when(jnp.logical_and(p == N_PAIRS - 1, t == n_steps - 1))(finish)

    qblk = pl.BlockSpec((tq, PAIR), lambda p, t, ii_r, jj_r: (ii_r[t], p))
    kblk = pl.BlockSpec((tk, PAIR), lambda p, t, ii_r, jj_r: (jj_r[t], p))
    in_specs = [qblk, kblk, kblk, qblk, pl.BlockSpec((N_HEADS, tk), lambda p, t, ii_r, jj_r: (0, jj_r[t]))]
    out_specs, out_shape = [qblk, qblk], [SDS((s, D_ATTN), F32)] * 2
    scratch = [pltpu.VMEM((2, tq, 1), F32), pltpu.VMEM((2, tq, 1), F32), pltpu.VMEM((tq, PAIR), F32)]
    args = [ii, jj, qs, kh, vb, fb, ft]
    if gathers:
        in_specs.append(ANY)
        out_specs.append(ANY)
        out_shape.append(SDS((N_DEV,) + gather_src.shape, gather_src.dtype))
        scratch += GATHER_SEMS
        args.append(gather_src)
    grid_spec = pltpu.PrefetchScalarGridSpec(num_scalar_prefetch=2, grid=(N_PAIRS, n_steps), in_specs=in_specs,
                                             out_specs=out_specs, scratch_shapes=scratch)
    return pl.pallas_call(
        body, name=name, grid_spec=grid_spec, out_shape=_in_hbm(out_shape),
        compiler_params=pltpu.CompilerParams(dimension_semantics=("arbitrary", "arbitrary"),
                                             vmem_limit_bytes=32 * MIB))(*[_keep_in_hbm(a) for a in args])


def _attn_bwd(qs, kh, vb, fb, ft, lse, o, dao, name, exchange_src=None):
    s = qs.shape[0]
    tq = tk = _tile(s)
    nq = s // tq
    sub = min(ATTN_BWD_SUB, tq)
    ii, jj = _causal_schedule(nq, key_major=True)
    n_steps = ii.shape[0]

    exchanges = exchange_src is not None

    def body(ii_ref, jj_ref, q_ref, k_ref, v_ref, fq_ref, ft_ref, lse_ref, o_ref, do_ref, *rest):
        if exchanges:
            t_hbm, dq_ref, rs_ref, dk_ref, dv_ref, df_ref, rcv_ref, dk_s, dv_s, df_s, send_sems, recv_sems = rest
        else:
            dq_ref, rs_ref, dk_ref, dv_ref, df_ref, dk_s, dv_s, df_s = rest
        p, t = pl.program_id(0), pl.program_id(1)
        i, j = ii_ref[t], jj_ref[t]
        first = _lane_is_first_head(sub)
        first_k = _lane_is_first_head(tk)
        if exchanges:
            start, finish = _chip_exchange_phases(t_hbm, rcv_ref, send_sems, recv_sems)
            pl.when(jnp.logical_and(p == 0, t == 0))(start)

        @pl.when(t == 0)
        def _():
            dq_ref[...] = jnp.zeros_like(dq_ref)
            rs_ref[...] = jnp.zeros_like(rs_ref)

        @pl.when(i == j)
        def _():
            dk_s[...] = jnp.zeros_like(dk_s)
            dv_s[...] = jnp.zeros_like(dv_s)
            df_s[...] = jnp.zeros_like(df_s)

        def tile(masked):
            kp, vp = k_ref[...], v_ref[...]
            q_all, fq_all, lse_all, o_all, do_all = q_ref[...], fq_ref[...], lse_ref[...], o_ref[...], do_ref[...]
            ft_rows = [ft_ref[pl.ds(2 * p + h, 1), :] for h in range(2)]
            dq_out, rs_out = [], []
            dk_acc, dv_acc = jnp.zeros((tk, PAIR), F32), jnp.zeros((tk, PAIR), F32)
            df_acc = [jnp.zeros((1, tk), F32), jnp.zeros((1, tk), F32)]
            for r in range(tq // sub):
                rows = slice(r * sub, (r + 1) * sub)
                qp, fq, lse, ov, dall = q_all[rows, :], fq_all[rows, :], lse_all[rows, :], o_all[rows, :], do_all[rows, :]
                dq_h, dk_h, dv_h, rs_h = [], [], [], []
                for h in range(2):
                    mine = first if h == 0 else jnp.logical_not(first)
                    s2 = _sub_scores(qp, kp, ft_rows[h], mine, r, masked, sub, tk)
                    lane = slice(h * HEAD_DIM, h * HEAD_DIM + 1)
                    pr = jnp.exp(s2 - (lse[:, lane] - fq[:, lane]))
                    dov = jnp.where(mine, dall, 0.0)
                    dsum = jnp.sum(dov * ov, axis=-1, keepdims=True)
                    dom = dov.astype(BF16)
                    dom_lo = (dov - dom.astype(F32)).astype(BF16)
                    dp = lax.dot_general(dom, vp, NT, preferred_element_type=F32)
                    dp = dp + lax.dot_general(dom_lo, vp, NT, preferred_element_type=F32)
                    ds = pr * (dp - dsum)
                    dsb = ds.astype(BF16)
                    dq_h.append(jnp.dot(dsb, kp, preferred_element_type=F32))
                    dk_h.append(lax.dot_general(dsb, qp, TN, preferred_element_type=F32))
                    dv_h.append(lax.dot_general(pr.astype(BF16), dom, TN, preferred_element_type=F32))
                    rs_h.append(jnp.sum(ds, axis=-1, keepdims=True))
                    df_acc[h] = df_acc[h] - jnp.sum(ds, axis=0, keepdims=True)
                dq_out.append(jnp.where(first, dq_h[0], dq_h[1]))
                rs_out.append(jnp.where(first, rs_h[0], rs_h[1]))
                dk_acc = dk_acc + jnp.where(first_k, dk_h[0], dk_h[1])
                dv_acc = dv_acc + jnp.where(first_k, dv_h[0], dv_h[1])
            grows = pl.ds(pl.multiple_of(i * tq, tq), tq)
            dq_ref[grows, :] += jnp.concatenate(dq_out, axis=0)
            rs_ref[grows, :] += jnp.concatenate(rs_out, axis=0)
            dk_s[...] += dk_acc
            dv_s[...] += dv_acc
            for h in range(2):
                df_s[h:h + 1, :] += df_acc[h]

        @pl.when(j < i)
        def _():
            tile(False)

        @pl.when(j == i)
        def _():
            tile(True)

        @pl.when(i == nq - 1)
        def _():
            dk_ref[...] = dk_s[...]
            dv_ref[...] = dv_s[...]
            df_ref[0] = df_s[...]

        if exchanges:
            pl.when(jnp.logical_and(p == N_PAIRS - 1, t == n_steps - 1))(finish)

    qblk = pl.BlockSpec((tq, PAIR), lambda p, t, ii_r, jj_r: (ii_r[t], p))
    kblk = pl.BlockSpec((tk, PAIR), lambda p, t, ii_r, jj_r: (jj_r[t], p))
    doblk = pl.BlockSpec((tq, PAIR), lambda p, t, ii_r, jj_r: (ii_r[t], N_PAIRS + p))
    whole = pl.BlockSpec((s, PAIR), lambda p, t, ii_r, jj_r: (0, p))
    in_specs = [qblk, kblk, kblk, qblk, pl.BlockSpec((N_HEADS, tk), lambda p, t, ii_r, jj_r: (0, jj_r[t])),
                qblk, qblk, doblk]
    out_specs = [whole, whole, kblk, kblk, pl.BlockSpec((1, 8, tk), lambda p, t, ii_r, jj_r: (p, 0, jj_r[t]))]
    out_shape = [SDS((s, D_ATTN), F32)] * 4 + [SDS((N_PAIRS, 8, s), F32)]
    scratch = [pltpu.VMEM((tk, PAIR), F32), pltpu.VMEM((tk, PAIR), F32), pltpu.VMEM((8, tk), F32)]
    args = [ii, jj, qs, kh, vb, fb, ft, lse, o, dao]
    if exchanges:
        in_specs.append(ANY)
        out_specs.append(ANY)
        out_shape.append(SDS((3,) + exchange_src.shape[1:], exchange_src.dtype))
        scratch += EXCHANGE_SEMS
        args.append(exchange_src)
    grid_spec = pltpu.PrefetchScalarGridSpec(num_scalar_prefetch=2, grid=(N_PAIRS, n_steps), in_specs=in_specs,
                                             out_specs=out_specs, scratch_shapes=scratch)
    return pl.pallas_call(
        body, name=name, grid_spec=grid_spec, out_shape=_in_hbm(out_shape),
        compiler_params=pltpu.CompilerParams(dimension_semantics=("arbitrary", "arbitrary"),
                                             vmem_limit_bytes=40 * MIB))(*[_keep_in_hbm(a) for a in args])


def _qk_bwd(z, dqs, dkh, dv, qn2, kn2, name):
    s = z.shape[0]
    tm = _tile(s)
    col0 = (z.shape[1] - 3 * D_ATTN) // D_ATTN

    def body(q_ref, k_ref, dqs_ref, dkh_ref, dv_ref, qn_ref, kn_ref, dq_ref, dk_ref, dvb_ref, dqn_ref, dkn_ref):
        first = _lane_is_first_head(tm)

        @pl.when(pl.program_id(0) == 0)
        def _():
            dqn_ref[...] = jnp.zeros_like(dqn_ref)
            dkn_ref[...] = jnp.zeros_like(dkn_ref)

        def through(x_ref, dy_ref, gain_ref, dx_ref, dgain_ref, scale):
            for p in range(N_PAIRS):
                sl = slice(p * PAIR, (p + 1) * PAIR)
                xv = x_ref[:, sl]
                r = _pair_rms(xv, first)
                xh = xv * r
                dy = dy_ref[:, sl] * scale
                dgain_ref[:, sl] += jnp.sum(dy * xh, axis=0, keepdims=True)
                dxh = dy * gain_ref[...]
                t = dxh * xh
                m0 = jnp.sum(jnp.where(first, t, 0.0), axis=-1, keepdims=True)
                m1 = jnp.sum(jnp.where(first, 0.0, t), axis=-1, keepdims=True)
                mean = jnp.where(first, m0, m1) / HEAD_DIM
                dx_ref[:, sl] = (r * (dxh - xh * mean)).astype(BF16)

        through(q_ref, dqs_ref, qn_ref, dq_ref, dqn_ref, ATTN_SCALE)
        through(k_ref, dkh_ref, kn_ref, dk_ref, dkn_ref, 1.0)
        dvb_ref[...] = dv_ref[...].astype(BF16)

    wide = lambda col: pl.BlockSpec((tm, D_ATTN), lambda i: (i, col))
    return _call(body, name=name, grid=(s // tm,),
                 in_specs=[wide(col0), wide(col0 + 1), wide(0), wide(0), wide(0), _const((1, PAIR)), _const((1, PAIR))],
                 out_specs=[wide(0), wide(0), wide(0), _const((1, D_ATTN)), _const((1, D_ATTN))],
                 out_shape=[SDS((s, D_ATTN), BF16)] * 3 + [SDS((1, D_ATTN), F32)] * 2,
                 dims=("arbitrary",))(z, z, dqs, dkh, dv, qn2, kn2)


def _gate_bwd(dft, xt, name):
    s = xt.shape[1]
    tm = _tile(s)
    n = s // tm

    def body(df_ref, xt_ref, dxt_ref, dx_ref, db_ref, carry):
        i = pl.program_id(0)

        @pl.when(i == 0)
        def _():
            carry[...] = jnp.zeros_like(carry)
            db_ref[...] = jnp.zeros_like(db_ref)

        tri = (lax.broadcasted_iota(jnp.int32, (tm, tm), 0) >= lax.broadcasted_iota(jnp.int32, (tm, tm), 1)).astype(BF16)
        rc = jnp.zeros((N_HEADS, tm), F32)
        for part in _split3(df_ref[...]):
            rc = rc + jnp.dot(part, tri, preferred_element_type=F32)
        dls = rc + carry[:, 0:1]
        carry[...] = jnp.broadcast_to(dls[:, 0:1], carry.shape)
        dxt = dls * _sigmoid(-xt_ref[...])
        dxt_ref[...] = dxt
        db_ref[...] += jnp.broadcast_to(jnp.sum(dxt, axis=-1, keepdims=True), db_ref.shape)
        padded = jnp.concatenate([dxt, jnp.zeros((V7X_LANES - N_HEADS, tm), F32)], axis=0)
        dx_ref[...] = padded.T

    rev = pl.BlockSpec((N_HEADS, tm), lambda i: (0, n - 1 - i))
    return _call(body, name=name, grid=(n,), in_specs=[rev, rev],
                 out_specs=[rev, pl.BlockSpec((tm, V7X_LANES), lambda i: (n - 1 - i, 0)), _const((N_HEADS, V7X_LANES))],
                 out_shape=[SDS((N_HEADS, s), F32), SDS((s, V7X_LANES), F32), SDS((N_HEADS, V7X_LANES), F32)],
                 scratch=[pltpu.VMEM((N_HEADS, V7X_LANES), F32)], dims=("arbitrary",))(dft, xt)


def _conv_c_fwd(z, cw, name):
    s = z.shape[0]
    c = z.shape[1] // 3
    tm, halo, kw = _tile(s), CONV_C_HALO, CONV_C_WIDTH

    def body(gb_ref, gc_ref, hh_ref, gcp_ref, hhp_ref, cw_ref, y_ref, buf):
        i = pl.program_id(0)
        buf[0:halo, :] = jnp.where(i > 0, gcp_ref[...] * hhp_ref[...], 0.0)
        buf[halo:halo + tm, :] = gc_ref[...] * hh_ref[...]
        c1 = jnp.zeros((tm, c), F32)
        for k in range(kw):
            c1 = c1 + cw_ref[k:k + 1, :] * buf[pl.ds(halo - (kw - 1) + k, tm), :]
        y_ref[...] = (gb_ref[...] * c1).astype(BF16)

    return _call(body, name=name, grid=(s // tm,),
                 in_specs=[_rows(tm, c, 0), _rows(tm, c, 1), _rows(tm, c, 2), _prev_rows(halo, tm, c, 1),
                           _prev_rows(halo, tm, c, 2), _const(cw.shape)],
                 out_specs=_rows(tm, c), out_shape=SDS((s, c), BF16),
                 scratch=[pltpu.VMEM((tm + halo, c), F32)], dims=("parallel",))(z, z, z, z, z, cw)


def _conv_c_bwd(dy0, z, cw, name):
    s = z.shape[0]
    c = z.shape[1] // 3
    tm, halo, kw = _tile(s), CONV_C_HALO, CONV_C_WIDTH
    n = s // tm

    def body(dy_ref, dyn_ref, gb_ref, gbn_ref, gc_ref, hh_ref, gcp_ref, hhp_ref, cw_ref, dz_ref, dcw_ref, buf, bd):
        i = pl.program_id(0)
        gcv, hhv, dyv = gc_ref[...], hh_ref[...], dy_ref[...]
        buf[0:halo, :] = jnp.where(i > 0, gcp_ref[...] * hhp_ref[...], 0.0)
        buf[halo:halo + tm, :] = gcv * hhv
        dc1 = dyv * gb_ref[...]
        bd[0:tm, :] = dc1
        bd[tm:tm + halo, :] = jnp.where(i < n - 1, dyn_ref[...] * gbn_ref[...], 0.0)

        @pl.when(i == 0)
        def _():
            dcw_ref[...] = jnp.zeros_like(dcw_ref)

        c1 = jnp.zeros((tm, c), F32)
        dc0 = jnp.zeros((tm, c), F32)
        for k in range(kw):
            shifted = buf[pl.ds(halo - (kw - 1) + k, tm), :]
            c1 = c1 + cw_ref[k:k + 1, :] * shifted
            dc0 = dc0 + cw_ref[k:k + 1, :] * bd[pl.ds(kw - 1 - k, tm), :]
            dcw_ref[k:k + 1, :] += jnp.sum(dc1 * shifted, axis=0, keepdims=True)
        dz_ref[:, 0:c] = (dyv * c1).astype(BF16)
        dz_ref[:, c:2 * c] = (dc0 * hhv).astype(BF16)
        dz_ref[:, 2 * c:3 * c] = (dc0 * gcv).astype(BF16)

    return _call(body, name=name, grid=(n,),
                 in_specs=[_rows(tm, c), _next_rows(halo, tm, c, 0, s // halo), _rows(tm, c, 0),
                           _next_rows(halo, tm, c, 0, s // halo), _rows(tm, c, 1), _rows(tm, c, 2),
                           _prev_rows(halo, tm, c, 1), _prev_rows(halo, tm, c, 2), _const(cw.shape)],
                 out_specs=[_rows(tm, 3 * c), _const(cw.shape)],
                 out_shape=[SDS((s, 3 * c), BF16), SDS(cw.shape, F32)],
                 scratch=[pltpu.VMEM((tm + halo, c), F32)] * 2, dims=("arbitrary",),
                 vmem_mb=48)(dy0, dy0, z, z, z, z, z, z, cw)


def _loss_head(y, target, name):
    s, d = y.shape
    tm = _tile(s)

    def body(y_ref, t_ref, loss_ref, dy_ref):
        e = y_ref[...] - t_ref[...]

        @pl.when(pl.program_id(0) == 0)
        def _():
            loss_ref[...] = jnp.zeros_like(loss_ref)

        loss_ref[...] += 0.5 * jnp.sum(jnp.mean(e * e, axis=-1, keepdims=True))
        dy_ref[...] = e / d

    return _call(body, name=name, grid=(s // tm,), in_specs=[_rows(tm, d), _rows(tm, d)],
                 out_specs=[_const((8, V7X_LANES)), _rows(tm, d)],
                 out_shape=[SDS((8, V7X_LANES), F32), SDS((s, d), F32)], dims=("arbitrary",))(y, target)


def _adamw(w, g, m, v, name):
    r, c = w.shape
    tr = next((t for t in (512, 256, 128, 64, 32, 16, 8) if r % t == 0), r)

    def body(w_ref, g_ref, m_ref, v_ref, d_ref, mo_ref, vo_ref):
        gv = g_ref[...]
        mn = ADAM_B1 * m_ref[...] + (1.0 - ADAM_B1) * gv
        vn = ADAM_B2 * v_ref[...] + (1.0 - ADAM_B2) * (gv * gv)
        m_hat = mn / (1.0 - ADAM_B1 ** ADAM_STEP)
        v_hat = vn / (1.0 - ADAM_B2 ** ADAM_STEP)
        d_ref[...] = -ADAM_LR * (m_hat / (jnp.sqrt(v_hat) + ADAM_EPS) + ADAM_WD * w_ref[...])
        mo_ref[...] = mn
        vo_ref[...] = vn

    spec = _rows(tr, c)
    return _call(body, name=name, grid=(r // tr,), in_specs=[spec] * 4, out_specs=[spec] * 3,
                 out_shape=[SDS((r, c), F32)] * 3, dims=("parallel",))(w, g, m, v)


def _position():
    return lax.axis_index("x"), lax.axis_index("y"), lax.axis_index("c")


def _other_chips(x, y):
    return [(1 - x, y), (x, 1 - y), (1 - x, 1 - y)]


def _dev_index(px, py, pc):
    return 4 * px + 2 * py + pc


def _all_gather(wloc):
    r, d = wloc.shape

    def body(x_ref, out_ref, send_sems, recv_sems, local_sem):
        start, forward, finish = _gather_phases(x_ref, out_ref, send_sems, recv_sems, local_sem)
        start()
        forward()
        finish()

    return _call(body, name="all_gather_weights", in_specs=[ANY], out_specs=ANY,
                 out_shape=SDS((N_DEV, r, d), wloc.dtype), scratch=GATHER_SEMS)(wloc)


GATHER_SEMS = [pltpu.SemaphoreType.DMA((7,)), pltpu.SemaphoreType.DMA((7,)), pltpu.SemaphoreType.DMA((1,))]


def _gather_phases(x_ref, out_ref, send_sems, recv_sems, local_sem):
    x, y, c = _position()
    me, sibling = (x, y, c), (x, y, 1 - c)
    chips = _other_chips(x, y)

    def slot(dev):
        return out_ref.at[_dev_index(*dev)]

    def copy(k, block, to, src=None):
        return pltpu.make_async_remote_copy(
            src_ref=slot(block) if src is None else src, dst_ref=slot(block),
            send_sem=send_sems.at[k], recv_sem=recv_sems.at[k], device_id=to, device_id_type=MESH)

    mine = pltpu.make_async_copy(x_ref, slot(me), local_sem.at[0])
    first = [copy(0, me, sibling, src=x_ref)] + [copy(1 + j, me, (*chip, c), src=x_ref) for j, chip in enumerate(chips)]
    passed = [copy(4 + j, (*chip, c), sibling) for j, chip in enumerate(chips)]

    def start():
        mine.start()
        for cp in first:
            cp.start()

    def forward():
        for j, chip in enumerate(chips):
            copy(1 + j, (*chip, c), me).wait_recv()
            passed[j].start()

    def finish():
        copy(0, sibling, me).wait_recv()
        for j, chip in enumerate(chips):
            copy(4 + j, (*chip, 1 - c), me).wait_recv()
        for cp in first + passed:
            cp.wait_send()
        mine.wait()

    return start, forward, finish


def _row_block(r):
    return next(t for t in range(704, 0, -BF16_ROWS) if r % t == 0)


def _pair_exchange(gall, name):
    def body(g_ref, out_ref, send_sems, recv_sems):
        start, finish = _pair_exchange_phases(g_ref, out_ref, send_sems, recv_sems)
        start()
        finish()

    return _call(body, name=name, in_specs=[ANY], out_specs=ANY, out_shape=_pair_exchange_shape(gall),
                 scratch=PAIR_EXCHANGE_SEMS)(gall)


PAIR_EXCHANGE_SEMS = [pltpu.SemaphoreType.DMA((4,)), pltpu.SemaphoreType.DMA((4,))]


def _pair_exchange_shape(gall):
    return SDS((4,) + gall.shape[1:], gall.dtype)


def _pair_exchange_phases(g_ref, out_ref, send_sems, recv_sems):
    x, y, c = _position()
    sibling = (x, y, 1 - c)
    dests = [sibling] + [(*chip, 1 - c) for chip in _other_chips(x, y)]
    copies = [pltpu.make_async_remote_copy(
        src_ref=g_ref.at[_dev_index(*dest)], dst_ref=out_ref.at[k], send_sem=send_sems.at[k],
        recv_sem=recv_sems.at[k], device_id=sibling, device_id_type=MESH) for k, dest in enumerate(dests)]

    def start():
        for cp in copies:
            cp.start()

    def finish():
        for cp in copies:
            cp.wait()

    return start, finish


def _pair_sum(gall, sib, idx, name):
    _, r, d = gall.shape
    tr = _row_block(r)

    def body(idx_ref, a_ref, b_ref, o_ref):
        o_ref[...] = (a_ref[...].astype(F32) + b_ref[...].astype(F32)).astype(o_ref.dtype)

    grid_spec = pltpu.PrefetchScalarGridSpec(
        num_scalar_prefetch=1, grid=(4, r // tr),
        in_specs=[pl.BlockSpec((1, tr, d), lambda k, i, idx_ref: (idx_ref[k], i, 0)),
                  pl.BlockSpec((1, tr, d), lambda k, i, idx_ref: (k, i, 0))],
        out_specs=pl.BlockSpec((1, tr, d), lambda k, i, idx_ref: (k, i, 0)))
    return pl.pallas_call(body, name=name, grid_spec=grid_spec,
                          out_shape=_in_hbm(SDS((4, r, d), gall.dtype)),
                          compiler_params=pltpu.CompilerParams(dimension_semantics=("parallel", "parallel")))(
        idx, _keep_in_hbm(gall), _keep_in_hbm(sib))


def _chip_exchange(tsum):
    _, r, d = tsum.shape

    def body(t_ref, out_ref, send_sems, recv_sems):
        start, finish = _chip_exchange_phases(t_ref, out_ref, send_sems, recv_sems)
        start()
        finish()

    return _call(body, name="reduce_scatter_chip_exchange", in_specs=[ANY], out_specs=ANY,
                 out_shape=SDS((3, r, d), tsum.dtype), scratch=EXCHANGE_SEMS)(tsum)


EXCHANGE_SEMS = [pltpu.SemaphoreType.DMA((3,)), pltpu.SemaphoreType.DMA((3,))]


def _chip_exchange_phases(t_ref, out_ref, send_sems, recv_sems):
    x, y, c = _position()
    copies = [pltpu.make_async_remote_copy(
        src_ref=t_ref.at[1 + k], dst_ref=out_ref.at[k], send_sem=send_sems.at[k], recv_sem=recv_sems.at[k],
        device_id=(*chip, c), device_id_type=MESH) for k, chip in enumerate(_other_chips(x, y))]

    def start():
        for cp in copies:
            cp.start()

    def finish():
        for cp in copies:
            cp.wait()

    return start, finish


def _chip_exchange_shape(tsum):
    return SDS((3,) + tsum.shape[1:], tsum.dtype)


EXCHANGES = dict(pair=(_pair_exchange_phases, _pair_exchange_shape, PAIR_EXCHANGE_SEMS),
                 chip=(_chip_exchange_phases, _chip_exchange_shape, EXCHANGE_SEMS))


def _final_sum(tsum, rcv, name):
    _, r, d = tsum.shape
    tr = _row_block(r)

    def body(t_ref, r_ref, o_ref):
        acc = t_ref[0].astype(F32)
        for k in range(3):
            acc = acc + r_ref[k].astype(F32)
        o_ref[...] = acc

    return _call(body, name=name, grid=(r // tr,),
                 in_specs=[pl.BlockSpec((1, tr, d), lambda i: (0, i, 0)), pl.BlockSpec((3, tr, d), lambda i: (0, i, 0))],
                 out_specs=_rows(tr, d), out_shape=SDS((r, d), F32), dims=("parallel",))(tsum, rcv)


def _all_reduce_small(buf):
    nr, lanes = buf.shape

    def body(b_ref, out_ref, gath, send_sems, recv_sems):
        x, y, c = _position()
        my_slot = _dev_index(x, y, c)
        gath[my_slot] = b_ref[...]
        copies = []
        for k in range(1, N_DEV):
            dx, dy, dc = (k >> 2) & 1, (k >> 1) & 1, k & 1
            peer = (1 - x if dx else x, 1 - y if dy else y, 1 - c if dc else c)
            copies.append(pltpu.make_async_remote_copy(
                src_ref=b_ref, dst_ref=gath.at[my_slot], send_sem=send_sems.at[k - 1], recv_sem=recv_sems.at[k - 1],
                device_id=peer, device_id_type=MESH))
        for cp in copies:
            cp.start()
        for cp in copies:
            cp.wait()
        acc = gath[0]
        for sidx in range(1, N_DEV):
            acc = acc + gath[sidx]
        out_ref[...] = acc

    return _call(body, name="all_reduce_small", in_specs=[VMEM], out_specs=VMEM, out_shape=SDS((nr, lanes), F32),
                 scratch=[pltpu.VMEM((N_DEV, nr, lanes), F32), pltpu.SemaphoreType.DMA((7,)),
                          pltpu.SemaphoreType.DMA((7,))])(buf)


def _ffn_block_fwd(x, gain, wall, offs, fs, tag, gather_src=None):
    res = _ffn_fwd(x, gain, wall, offs, fs, f"{tag}_fwd", gather_src)
    out, xn, g, u, h = res[:5]
    return out, (x, gain, xn, g, u, h), (res[5] if gather_src is not None else None)


def _ffn_block_bwd(dout, saved, wall, offs, fs, tag, exchange_src=None):
    x, gain, xn, g, u, h = saved
    res = _ffn_bwd_act(dout, g, u, x, gain, wall, offs, fs, f"{tag}_bwd_act", exchange_src)
    dg, du, dy_b, dx, dgain = res[:5]
    dwg = _mm_tn(dg, xn, f"{tag}_dwg", BF16)
    dwu = _mm_tn(du, xn, f"{tag}_dwu", BF16)
    dwd = _mm_tn(h, dy_b, f"{tag}_dwd", BF16)
    return dx, (dwg, dwu, dwd), dgain, (res[5] if exchange_src is not None else None)


def _local_step(x, target, wall_a, fs, small, plan):
    grads = {}
    first, second = (0, fs, 2 * fs), (3 * fs, 4 * fs, 5 * fs)
    reduces = "pair_sum" in plan

    x1, s_f1a, wall_b = _ffn_block_fwd(x, small["ffn1_norm"][0], wall_a, first, fs, "l0_ffn1", plan.get("shard_b"))
    wall_b = plan.get("wall_b", wall_b)
    mixw = plan["mix_b"](wall_b)
    hn0 = _rmsnorm_fwd(x1, small["mix_norm"][0], "l0_mix_norm")
    z = _mm(hn0, mixw["ev_w_main_t"], "nt", "ev_in_proj")
    flog = _mm(hn0, mixw["ev_w_f_t"], "nt", "ev_in_proj_gate")
    a, a1 = _conv_a_fwd(z, small["ev_conv_w32"], small["ev_conv_b"], small["ev_conv_norm"], "ev_conv_fwd")
    qs, kh, vb, fb, ft, xt = _qk_fwd(z, flog, small["ev_b_f128"], small["ev_q_norm2"], small["ev_k_norm2"], "ev_qk_fwd")
    if "wall_c" in plan:
        o, lse = _attn_fwd(qs, kh, vb, fb, ft, "ev_attn_fwd")
        wall_c = plan["wall_c"]
    else:
        o, lse, wall_c = _attn_fwd(qs, kh, vb, fb, ft, "ev_attn_fwd", gather_src=plan["shard_c"])
    mixw = {**mixw, **plan["mix_c"](wall_c)}
    ao = jnp.concatenate([a, o.astype(BF16)], axis=1)
    x2 = _mm(ao, mixw["ev_w_out"], "nn", "ev_out_proj", add=x1)
    x3, s_f2a, _ = _ffn_block_fwd(x2, small["ffn2_norm"][0], wall_c, first, fs, "l0_ffn2")

    x4, s_f1b, wall_d = _ffn_block_fwd(x3, small["ffn1_norm"][1], wall_c, second, fs, "l1_ffn1", plan.get("shard_d"))
    wall_d = plan.get("wall_d", wall_d)
    hn1 = _rmsnorm_fwd(x4, small["mix_norm"][1], "l1_mix_norm")
    zo = _mm(hn1, mixw["od_w_in_t"], "nt", "od_in_proj")
    y0 = _conv_c_fwd(zo, small["od_conv_w8"], "od_conv_fwd")
    x5 = _mm(y0, mixw["od_w_out"], "nn", "od_out_proj", add=x4)
    x6, s_f2b, _ = _ffn_block_fwd(x5, small["ffn2_norm"][1], wall_d, first, fs, "l1_ffn2")

    loss, d6 = _loss_head(x6, target, "loss_head")

    d5, grads["l1_ffn2"], grads["ffn2_norm_1"], _ = _ffn_block_bwd(d6, s_f2b, wall_d, first, fs, "l1_ffn2")
    d5b = d5.astype(BF16)
    dy0 = _mm(d5b, mixw["od_w_out"], "nt", "od_out_proj_bwd")
    grads["od_w_out"] = _mm_tn(y0, d5b, "od_dw_out", BF16)
    dzo, grads["od_conv_w"] = _conv_c_bwd(dy0, zo, small["od_conv_w8"], "od_conv_bwd")
    d4, grads["mix_norm_1"] = _mm_norm_bwd(dzo, mixw["od_w_in_t"], x4, small["mix_norm"][1], d5, "od_in_proj_bwd")
    grads["od_w_in_t"] = _mm_tn(dzo, hn1, "od_dw_in", BF16)
    d3, grads["l1_ffn1"], grads["ffn1_norm_1"], _ = _ffn_block_bwd(d4, s_f1b, wall_c, second, fs, "l1_ffn1")

    d2, grads["l0_ffn2"], grads["ffn2_norm_0"], _ = _ffn_block_bwd(d3, s_f2a, wall_c, first, fs, "l0_ffn2")
    partials_c = plan["partials_c"](grads) if reduces else None
    d2b = d2.astype(BF16)
    dao = _mm(d2b, mixw["ev_w_out"], "nt", "ev_out_proj_bwd")
    grads["ev_w_out"] = _mm_tn(ao, d2b, "ev_dw_out", BF16)
    da1, grads["ev_conv_norm"], grads["ev_conv_b"] = _conv_a_bwd_norm(dao, a1, small["ev_conv_norm"], "ev_conv_bwd_norm")
    res = _conv_a_bwd_conv(da1, z, small["ev_conv_w32"], "ev_conv_bwd_conv",
                           exchange_src=("pair", partials_c) if reduces else None)
    du, dg, grads["ev_conv_w"] = res[:3]
    sums_c = plan["pair_sum"](partials_c, res[3], "c") if reduces else None
    res = _attn_bwd(qs, kh, vb, fb, ft, lse, o, dao, "ev_attn_bwd", exchange_src=sums_c)
    dqs, rs, dkh, dv, df4 = res[:5]
    if reduces:
        grads["pair_sums_c"], grads["exchanged_c"] = sums_c, res[5]
    dq, dk, dvb, grads["ev_q_norm"], grads["ev_k_norm"] = _qk_bwd(
        z, dqs, dkh, dv, small["ev_q_norm2"], small["ev_k_norm2"], "ev_qk_bwd")
    dft = df4[:, 0:2, :].reshape(N_HEADS, -1) + rs.reshape(-1, N_HEADS, HEAD_DIM)[:, :, 0].T
    dxt, dflog, grads["ev_b_f"] = _gate_bwd(dft, xt, "ev_gate_bwd")
    dz = jnp.concatenate([du, dg, dq, dk, dvb], axis=1)
    dflog_b = dflog.astype(BF16)
    dh0 = _mm(dz, mixw["ev_w_main_t"], "nn", "ev_in_proj_bwd")
    d1, grads["mix_norm_0"] = _mm_norm_bwd(dflog_b, mixw["ev_w_f_t"], x1, small["mix_norm"][0], d2,
                                           "ev_in_proj_gate_bwd", add=dh0)
    dw_main = _mm_tn(dz, hn0, "ev_dw_in", BF16)
    dw_f = _mm(dxt.astype(BF16), hn0, "nn", "ev_dw_in_gate", BF16)
    grads["ev_w_in_t"] = jnp.concatenate([dw_main, dw_f], axis=0)
    sums_b = None
    if reduces:
        partials_b = plan["partials_b"](grads)
        sums_b = plan["pair_sum"](partials_b, _pair_exchange(partials_b, "reduce_scatter_pair_exchange_b"), "b")
    d0, grads["l0_ffn1"], grads["ffn1_norm_0"], exchanged_b = _ffn_block_bwd(
        d1, s_f1a, wall_a, first, fs, "l0_ffn1", exchange_src=("chip", sums_b) if reduces else None)
    if reduces:
        grads["pair_sums_b"], grads["exchanged_b"] = sums_b, exchanged_b
    return loss, d0, grads


def _round_up(n, m):
    return -(-n // m) * m


def _pad_rows(a, rows):
    return jnp.pad(a, ((0, rows - a.shape[0]), (0, 0)))


SMALL_ORDER = ("loss", "ffn1_norm", "mix_norm", "ffn2_norm", "ev_b_f", "ev_conv_b", "ev_conv_norm",
               "ev_q_norm", "ev_k_norm", "ev_conv_w", "od_conv_w")


def _pack_small(parts):
    flat = jnp.concatenate([parts[k].reshape(-1).astype(F32) for k in SMALL_ORDER])
    n = _round_up(flat.shape[0], 8 * V7X_LANES)
    return jnp.pad(flat, (0, n - flat.shape[0])).reshape(-1, V7X_LANES)


def _unpack_small(buf, shapes):
    flat = buf.reshape(-1)
    out, pos = {}, 0
    for k in SMALL_ORDER:
        n = math.prod(shapes[k])
        out[k] = flat[pos:pos + n].reshape(shapes[k])
        pos += n
    return out


def kernel(x, ffn1_norm, ffn1_w_gate, ffn1_w_up, ffn1_w_down, mix_norm, ffn2_norm, ffn2_w_gate, ffn2_w_up, ffn2_w_down, ev_w_in, ev_b_f, ev_conv_w, ev_conv_b, ev_conv_norm, ev_q_norm, ev_k_norm, ev_w_out, od_w_in, od_conv_w, od_w_out, loss_target, m_ffn1_norm, m_ffn1_w_gate, m_ffn1_w_up, m_ffn1_w_down, m_mix_norm, m_ffn2_norm, m_ffn2_w_gate, m_ffn2_w_up, m_ffn2_w_down, m_ev_w_in, m_ev_b_f, m_ev_conv_w, m_ev_conv_b, m_ev_conv_norm, m_ev_q_norm, m_ev_k_norm, m_ev_w_out, m_od_w_in, m_od_conv_w, m_od_w_out, v_ffn1_norm, v_ffn1_w_gate, v_ffn1_w_up, v_ffn1_w_down, v_mix_norm, v_ffn2_norm, v_ffn2_w_gate, v_ffn2_w_up, v_ffn2_w_down, v_ev_w_in, v_ev_b_f, v_ev_conv_w, v_ev_conv_b, v_ev_conv_norm, v_ev_q_norm, v_ev_k_norm, v_ev_w_out, v_od_w_in, v_od_conv_w, v_od_w_out):
    weights = dict(ffn1_norm=ffn1_norm, ffn1_w_gate=ffn1_w_gate, ffn1_w_up=ffn1_w_up, ffn1_w_down=ffn1_w_down,
                   mix_norm=mix_norm, ffn2_norm=ffn2_norm, ffn2_w_gate=ffn2_w_gate, ffn2_w_up=ffn2_w_up,
                   ffn2_w_down=ffn2_w_down, ev_w_in=ev_w_in, ev_b_f=ev_b_f, ev_conv_w=ev_conv_w, ev_conv_b=ev_conv_b,
                   ev_conv_norm=ev_conv_norm, ev_q_norm=ev_q_norm, ev_k_norm=ev_k_norm, ev_w_out=ev_w_out,
                   od_w_in=od_w_in, od_conv_w=od_conv_w, od_w_out=od_w_out)
    m_in = dict(ffn1_norm=m_ffn1_norm, ffn1_w_gate=m_ffn1_w_gate, ffn1_w_up=m_ffn1_w_up, ffn1_w_down=m_ffn1_w_down,
                mix_norm=m_mix_norm, ffn2_norm=m_ffn2_norm, ffn2_w_gate=m_ffn2_w_gate, ffn2_w_up=m_ffn2_w_up,
                ffn2_w_down=m_ffn2_w_down, ev_w_in=m_ev_w_in, ev_b_f=m_ev_b_f, ev_conv_w=m_ev_conv_w,
                ev_conv_b=m_ev_conv_b, ev_conv_norm=m_ev_conv_norm, ev_q_norm=m_ev_q_norm, ev_k_norm=m_ev_k_norm,
                ev_w_out=m_ev_w_out, od_w_in=m_od_w_in, od_conv_w=m_od_conv_w, od_w_out=m_od_w_out)
    v_in = dict(ffn1_norm=v_ffn1_norm, ffn1_w_gate=v_ffn1_w_gate, ffn1_w_up=v_ffn1_w_up, ffn1_w_down=v_ffn1_w_down,
                mix_norm=v_mix_norm, ffn2_norm=v_ffn2_norm, ffn2_w_gate=v_ffn2_w_gate, ffn2_w_up=v_ffn2_w_up,
                ffn2_w_down=v_ffn2_w_down, ev_w_in=v_ev_w_in, ev_b_f=v_ev_b_f, ev_conv_w=v_ev_conv_w,
                ev_conv_b=v_ev_conv_b, ev_conv_norm=v_ev_conv_norm, ev_q_norm=v_ev_q_norm, ev_k_norm=v_ev_k_norm,
                ev_w_out=v_ev_w_out, od_w_in=v_od_w_in, od_conv_w=v_od_conv_w, od_w_out=v_od_w_out)
    order = list(weights)

    d = x.shape[-1]
    fs = ffn1_w_gate.shape[2]
    n_in = ev_w_in.shape[2]
    n_in_pad = _round_up(n_in, BF16_ROWS)
    n_out = ev_w_out.shape[1]
    n_od = od_w_in.shape[2]
    d_conv = ev_conv_b.shape[1]
    d_in_even = n_in * N_DEV
    d_main = d_in_even - N_HEADS
    cx, cy, cc = _position()
    me = _dev_index(cx, cy, cc)

    def block(wg, wu, wd, layer):
        return [wg[layer].T, wu[layer].T, wd[layer]]

    def stack(parts):
        return jnp.concatenate([p.astype(BF16) for p in parts], axis=0)

    ffn1, ffn2 = (ffn1_w_gate, ffn1_w_up, ffn1_w_down), (ffn2_w_gate, ffn2_w_up, ffn2_w_down)
    shard_a = stack(block(*ffn1, 0))
    shard_b = stack([_pad_rows(ev_w_in[0].T, n_in_pad), ev_w_out[0]])
    shard_c = stack(block(*ffn2, 0) + block(*ffn1, 1) + [od_w_in[0].T, od_w_out[0]])
    shard_d = stack(block(*ffn2, 1))
    off_ev_in, off_ev_out = 0, n_in_pad
    off_od_in, off_od_out = 6 * fs, 6 * fs + n_od
    wall_a = _all_gather(shard_a)

    def even_mixer_weights(wall_b):
        ev_w_in_t = wall_b[:, off_ev_in:off_ev_in + n_in, :].reshape(d_in_even, d)
        return dict(ev_w_main_t=ev_w_in_t[:d_main], ev_w_f_t=_pad_rows(ev_w_in_t[d_main:], V7X_LANES),
                    ev_w_out=wall_b[:, off_ev_out:off_ev_out + n_out, :].reshape(N_DEV * n_out, d))

    def odd_mixer_weights(wall_c):
        return dict(od_w_in_t=wall_c[:, off_od_in:off_od_in + n_od, :].reshape(N_DEV * n_od, d),
                    od_w_out=wall_c[:, off_od_out:off_od_out + n_out, :].reshape(N_DEV * n_out, d))

    def by_dev(a, rows, pad_to=None):
        a = a.reshape(N_DEV, rows, d)
        return a if pad_to is None else jnp.pad(a, ((0, 0), (0, pad_to - rows), (0, 0)))

    idx = jnp.stack([me] + [_dev_index(*chip, cc) for chip in _other_chips(cx, cy)]).astype(jnp.int32)

    def pair_sum(partials, from_sibling, tag):
        return _pair_sum(partials, from_sibling, idx, f"reduce_scatter_pair_sum_{tag}")

    def ffn_pieces(g, key):
        return [by_dev(t, fs) for t in g[key]]

    conv_shapes = dict(ev_conv_w=(CONV_A_WIDTH, d_conv), od_conv_w=(CONV_C_WIDTH, d))
    zero_small = {k: jnp.zeros(s_, F32) for k, s_ in conv_shapes.items()}
    ev_cw_part = lax.dynamic_update_slice(zero_small["ev_conv_w"], ev_conv_w[0], (0, me * ev_conv_w.shape[2]))
    od_cw_part = lax.dynamic_update_slice(zero_small["od_conv_w"], od_conv_w[0], (0, me * od_conv_w.shape[2]))
    zeros_like_small = {k: jnp.zeros((1,), F32) for k in SMALL_ORDER}
    taps = _unpack_small(_all_reduce_small(_pack_small({**zeros_like_small, "ev_conv_w": ev_cw_part,
                                                        "od_conv_w": od_cw_part})),
                         {**{k: (1,) for k in SMALL_ORDER}, **conv_shapes})
    small = dict(
        ffn1_norm=[ffn1_norm[l][None] for l in range(2)], mix_norm=[mix_norm[l][None] for l in range(2)],
        ffn2_norm=[ffn2_norm[l][None] for l in range(2)],
        ev_conv_w32=_pad_rows(taps["ev_conv_w"], CONV_A_WIDTH + 1), ev_conv_b=ev_conv_b, ev_conv_norm=ev_conv_norm,
        ev_b_f128=jnp.pad(ev_b_f, ((0, 0), (0, V7X_LANES - N_HEADS))),
        ev_q_norm2=jnp.tile(ev_q_norm, (1, 2)), ev_k_norm2=jnp.tile(ev_k_norm, (1, 2)),
        od_conv_w8=_pad_rows(taps["od_conv_w"], 8),
    )

    plan = dict(
        shard_b=shard_b, shard_c=shard_c, shard_d=shard_d, mix_b=even_mixer_weights, mix_c=odd_mixer_weights, pair_sum=pair_sum,
        partials_c=lambda g1: jnp.concatenate(
            ffn_pieces(g1, "l0_ffn2") + ffn_pieces(g1, "l1_ffn1") + ffn_pieces(g1, "l1_ffn2")
            + [by_dev(g1["od_w_in_t"], n_od), by_dev(g1["od_w_out"], n_out)], axis=1),
        partials_b=lambda g1: jnp.concatenate(
            [by_dev(g1["ev_w_in_t"], n_in, n_in_pad), by_dev(g1["ev_w_out"], n_out)], axis=1))
    loss_p, grad_x, g = _local_step(x[0], loss_target[0], wall_a, fs, small, plan)

    partials_a = jnp.concatenate(ffn_pieces(g, "l0_ffn1"), axis=1)
    sums_a = pair_sum(partials_a, _pair_exchange(partials_a, "reduce_scatter_pair_exchange_a"), "a")
    gsum_a = _final_sum(sums_a, _chip_exchange(sums_a), "reduce_scatter_final_sum_a")
    gsum_b = _final_sum(g["pair_sums_b"], g["exchanged_b"], "reduce_scatter_final_sum_b")
    gsum_c = _final_sum(g["pair_sums_c"], g["exchanged_c"], "reduce_scatter_final_sum_c")

    grad = {}
    where = dict(ffn1=((gsum_a, 0), (gsum_c, 3 * fs)), ffn2=((gsum_c, 0), (gsum_c, 6 * fs)))
    for blk, places in where.items():
        for wi, kind in enumerate(("gate", "up", "down")):
            rows = [buf[off + wi * fs:off + (wi + 1) * fs] for buf, off in places]
            grad[f"{blk}_w_{kind}"] = jnp.stack(rows if kind == "down" else [r.T for r in rows])
    grad["ev_w_in"] = gsum_b[off_ev_in:off_ev_in + n_in].T[None]
    grad["ev_w_out"] = gsum_b[off_ev_out:off_ev_out + n_out][None]
    grad["od_w_in"] = gsum_c[9 * fs:9 * fs + n_od].T[None]
    grad["od_w_out"] = gsum_c[9 * fs + n_od:9 * fs + n_od + n_out][None]

    heads = lambda t: t.reshape(N_HEADS, HEAD_DIM).sum(axis=0)
    parts = dict(
        loss=loss_p[0, 0:1],
        ffn1_norm=jnp.stack([g["ffn1_norm_0"][0], g["ffn1_norm_1"][0]]),
        mix_norm=jnp.stack([g["mix_norm_0"][0], g["mix_norm_1"][0]]),
        ffn2_norm=jnp.stack([g["ffn2_norm_0"][0], g["ffn2_norm_1"][0]]),
        ev_b_f=g["ev_b_f"][:, 0], ev_conv_b=g["ev_conv_b"], ev_conv_norm=g["ev_conv_norm"],
        ev_q_norm=heads(g["ev_q_norm"]), ev_k_norm=heads(g["ev_k_norm"]),
        ev_conv_w=g["ev_conv_w"][:CONV_A_WIDTH], od_conv_w=g["od_conv_w"][:CONV_C_WIDTH])
    small_shapes = dict(loss=(1,), ffn1_norm=ffn1_norm.shape, mix_norm=mix_norm.shape, ffn2_norm=ffn2_norm.shape,
                        ev_b_f=ev_b_f.shape, ev_conv_b=ev_conv_b.shape, ev_conv_norm=ev_conv_norm.shape,
                        ev_q_norm=ev_q_norm.shape, ev_k_norm=ev_k_norm.shape, **conv_shapes)
    red = _unpack_small(_all_reduce_small(_pack_small(parts)), small_shapes)
    loss = red["loss"][0]
    for k in ("ffn1_norm", "mix_norm", "ffn2_norm", "ev_b_f", "ev_conv_b", "ev_conv_norm", "ev_q_norm", "ev_k_norm"):
        grad[k] = red[k]
    grad["ev_conv_w"] = lax.dynamic_slice(red["ev_conv_w"], (0, me * ev_conv_w.shape[2]),
                                          (CONV_A_WIDTH, ev_conv_w.shape[2]))[None]
    grad["od_conv_w"] = lax.dynamic_slice(red["od_conv_w"], (0, me * od_conv_w.shape[2]),
                                          (CONV_C_WIDTH, od_conv_w.shape[2]))[None]

    big = ("ffn1_w_gate", "ffn1_w_up", "ffn1_w_down", "ffn2_w_gate", "ffn2_w_up", "ffn2_w_down",
           "ev_w_in", "ev_w_out", "od_w_in", "od_w_out")
    delta, new_m, new_v = {}, {}, {}
    for k in big:
        shp = weights[k].shape
        flat = lambda t: t.reshape(-1, shp[-1])
        dk, mk, vk = _adamw(flat(weights[k]), flat(grad[k]), flat(m_in[k]), flat(v_in[k]), f"adamw_{k}")
        delta[k], new_m[k], new_v[k] = dk.reshape(shp), mk.reshape(shp), vk.reshape(shp)
    rest = [k for k in order if k not in big]
    cat = lambda src: jnp.concatenate([src[k].reshape(-1) for k in rest])
    n_small = sum(math.prod(weights[k].shape) for k in rest)
    n_pad = _round_up(n_small, 8 * V7X_LANES)
    as_rows = lambda t: jnp.pad(t, (0, n_pad - n_small)).reshape(-1, V7X_LANES)
    v_rows = jnp.pad(cat(v_in), (0, n_pad - n_small), constant_values=1.0).reshape(-1, V7X_LANES)
    ds, ms, vs = _adamw(as_rows(cat(weights)), as_rows(cat(grad)), as_rows(cat(m_in)), v_rows, "adamw_small")
    pos = 0
    for k in rest:
        n = math.prod(weights[k].shape)
        for dst, src in ((delta, ds), (new_m, ms), (new_v, vs)):
            dst[k] = src.reshape(-1)[pos:pos + n].reshape(weights[k].shape)
        pos += n

    return (loss, grad_x[None], *[grad[k] for k in order], *[delta[k] for k in order],
            *[new_m[k] for k in order], *[new_v[k] for k in order])
```
